```python
import jax, jax.numpy as jnp
from jax import lax
import numpy as np

D_MODEL = 2048
BATCH = 8
SEQ = 4096
DEPTH = 2

N_MEM = 256
EPS = 1e-6
N_EVEN = (DEPTH + 1) // 2
N_ODD = DEPTH // 2
MIX_A = D_MODEL // 2
POOL_WINDOWS = (2, 4, 8, 16)
N_POOL_GROUPS = len(POOL_WINDOWS)
POOL_GROUP = MIX_A // N_POOL_GROUPS
MIX_B = D_MODEL - MIX_A
HG_HEAD = 128
HG_HEADS = MIX_B // HG_HEAD
HG_CHUNK = 64
IN_EVEN = MIX_A + 4 * MIX_B
FOX_HEAD = 128
FOX_HEADS = D_MODEL // FOX_HEAD
FOX_BLOCK = 128
IN_ODD = 3 * D_MODEL + FOX_HEADS
XA_HEADS = 4
XA_HEAD = D_MODEL // XA_HEADS
D_FF = -(-8 * D_MODEL // (3 * 256)) * 256

kernel_name = "hybrid_pool_hgrn2_fox_trunk"


def rmsnorm(x, g):
    xf = x.astype(jnp.float32)
    y = xf * lax.rsqrt(jnp.mean(xf * xf, axis=-1, keepdims=True) + EPS)
    return (y * g.astype(jnp.float32)).astype(x.dtype)


def pool_mixer(u, w_pool, pool_scale):
    B, T, _ = u.shape
    uf = u.astype(jnp.float32)
    c = jnp.pad(jnp.cumsum(uf, axis=1), ((0, 0), (1, 0), (0, 0)))
    t = jnp.arange(T)
    outs = []
    for gi, w in enumerate(POOL_WINDOWS):
        cg = c[:, :, gi * POOL_GROUP:(gi + 1) * POOL_GROUP]
        c_lag = jnp.pad(cg, ((0, 0), (w - 1, 0), (0, 0)))[:, :T]
        cnt = jnp.minimum(t + 1, w).astype(jnp.float32)[None, :, None]
        mean = (cg[:, 1:] - c_lag) / cnt
        outs.append(mean - uf[:, :, gi * POOL_GROUP:(gi + 1) * POOL_GROUP])
    p = jnp.stack(outs, axis=2)
    y = jnp.einsum('btgc,gcd->btgd', p, w_pool.astype(jnp.float32)).reshape(B, T, MIX_A)
    return (y * pool_scale.astype(jnp.float32)).astype(u.dtype)


def hgrn2_mixer(q, fl, i, g, lb, norm_g):
    B, T, _ = q.shape
    H, Dh, C = HG_HEADS, HG_HEAD, HG_CHUNK
    N = T // C
    f = lb + (1.0 - lb) * jax.nn.sigmoid(fl.astype(jnp.float32))
    logf = jnp.log(f)
    k = 1.0 - f
    qf = jax.nn.silu(q.astype(jnp.float32)) * (Dh ** -0.5)

    def to_chunks(a):
        return a.reshape(B, N, C, H, Dh).transpose(1, 0, 3, 2, 4)

    qc, kc, vc = to_chunks(qf), to_chunks(k), to_chunks(i.astype(jnp.float32))
    bc = jnp.cumsum(to_chunks(logf), axis=3)
    causal = jnp.tril(jnp.ones((C, C), dtype=bool))[:, :, None]

    def step(S, inp):
        qh, kh, vh, bh = inp
        diff = bh[:, :, :, None, :] - bh[:, :, None, :, :]
        decay = jnp.exp(jnp.where(causal, diff, -jnp.inf))
        A = jnp.einsum('bhtk,bhsk,bhtsk->bhts', qh, kh, decay)
        o = (jnp.einsum('bhts,bhsv->bhtv', A, vh)
             + jnp.einsum('bhtk,bhkv->bhtv', qh * jnp.exp(bh), S))
        b_last = bh[:, :, -1:, :]
        S = (jnp.exp(b_last[:, :, 0, :])[..., None] * S
             + jnp.einsum('bhsk,bhsv->bhkv', kh * jnp.exp(b_last - bh), vh))
        return S, o

    S0 = jnp.zeros((B, H, Dh, Dh), jnp.float32)
    _, o = lax.scan(step, S0, (qc, kc, vc, bc))
    o = o.transpose(1, 0, 3, 2, 4).reshape(B, T, H, Dh)
    o = rmsnorm(o, norm_g).reshape(B, T, MIX_B)
    return (o * jax.nn.silu(g.astype(jnp.float32))).astype(q.dtype)


def fox_attention(q, k, v, fl):
    B, T, H, Dh = q.shape
    Fc = jnp.cumsum(jax.nn.log_sigmoid(fl.astype(jnp.float32)), axis=1).transpose(0, 2, 1)
    scale = Dh ** -0.5
    outs = []
    for blk in range(T // FOX_BLOCK):
        q0, q1 = blk * FOX_BLOCK, (blk + 1) * FOX_BLOCK
        s = jnp.einsum('bqhd,bkhd->bhqk', q[:, q0:q1], k[:, :q1]).astype(jnp.float32) * scale
        s = s + (Fc[:, :, q0:q1, None] - Fc[:, :, None, :q1])
        mask = (q0 + jnp.arange(FOX_BLOCK))[:, None] >= jnp.arange(q1)[None, :]
        p = jax.nn.softmax(jnp.where(mask, s, -jnp.inf), axis=-1)
        outs.append(jnp.einsum('bhqk,bkhd->bqhd', p.astype(v.dtype), v[:, :q1]))
    return jnp.concatenate(outs, axis=1).reshape(B, T, H * Dh)


def cross_attention(h, mem_n, wq, wkv, wo):
    B, T, _ = h.shape
    M = mem_n.shape[1]
    q = (h @ wq).reshape(B, T, XA_HEADS, XA_HEAD)
    kv = mem_n @ wkv
    k = kv[..., :D_MODEL].reshape(B, M, XA_HEADS, XA_HEAD)
    v = kv[..., D_MODEL:].reshape(B, M, XA_HEADS, XA_HEAD)
    s = jnp.einsum('bthd,bmhd->bhtm', q, k).astype(jnp.float32) * (XA_HEAD ** -0.5)
    p = jax.nn.softmax(s, axis=-1)
    o = jnp.einsum('bhtm,bmhd->bthd', p.astype(v.dtype), v).reshape(B, T, D_MODEL)
    return o @ wo


def _fwd_setup_inputs(seed: int = 0) -> dict:
    key = jax.random.key(seed)
    ks = jax.random.split(key, 24)
    D = D_MODEL

    def nrm(k, shape, s):
        return jax.random.normal(k, shape, jnp.float32) * s

    def gain(k, shape):
        return 1.0 + 0.05 * jax.random.normal(k, shape, jnp.float32)

    return {
        "x": nrm(ks[0], (BATCH, SEQ, D), 1.0),
        "mem": nrm(ks[1], (BATCH, N_MEM, D), 1.0),
        "lb_table": nrm(ks[2], (DEPTH + 1, MIX_B), 0.5),
        "ev_norm": gain(ks[3], (N_EVEN, D)),
        "ev_w_in": nrm(ks[4], (N_EVEN, D, IN_EVEN), D ** -0.5),
        "ev_w_pool": nrm(ks[5], (N_EVEN, N_POOL_GROUPS, POOL_GROUP, POOL_GROUP), POOL_GROUP ** -0.5),
        "ev_pool_scale": gain(ks[6], (N_EVEN, MIX_A)),
        "ev_hg_norm": gain(ks[7], (N_EVEN, HG_HEAD)),
        "ev_w_out": nrm(ks[8], (N_EVEN, D, D), D ** -0.5),
        "od_norm": gain(ks[9], (N_ODD, D)),
        "od_w_in": nrm(ks[10], (N_ODD, D, IN_ODD), D ** -0.5),
        "od_b_f": 2.0 + nrm(ks[11], (N_ODD, FOX_HEADS), 0.1),
        "od_w_out": nrm(ks[12], (N_ODD, D, D), D ** -0.5),
        "xa_norm": gain(ks[13], (DEPTH, D)),
        "xa_mem_norm": gain(ks[14], (DEPTH, D)),
        "xa_wq": nrm(ks[15], (DEPTH, D, D), D ** -0.5),
        "xa_wkv": nrm(ks[16], (DEPTH, D, 2 * D), D ** -0.5),
        "xa_wo": nrm(ks[17], (DEPTH, D, D), D ** -0.5),
        "ffn_norm": gain(ks[18], (DEPTH, D)),
        "ffn_w_gate": nrm(ks[19], (DEPTH, D, D_FF), D ** -0.5),
        "ffn_w_up": nrm(ks[20], (DEPTH, D, D_FF), D ** -0.5),
        "ffn_w_down": nrm(ks[21], (DEPTH, D_FF, D), D_FF ** -0.5),
        "final_norm": gain(ks[22], (D,)),
    }


def _fwd_reference(x, mem, lb_table, ev_norm, ev_w_in, ev_w_pool, ev_pool_scale, ev_hg_norm, ev_w_out,
              od_norm, od_w_in, od_b_f, od_w_out, xa_norm, xa_mem_norm, xa_wq, xa_wkv, xa_wo,
              ffn_norm, ffn_w_gate, ffn_w_up, ffn_w_down, final_norm):
    B, T, D = x.shape
    lb_cum = jnp.cumsum(jax.nn.softmax(lb_table.astype(jnp.float32), axis=0), axis=0)
    for l in range(DEPTH):
        if l % 2 == 0:
            e = l // 2
            h = rmsnorm(x, ev_norm[e])
            z = h @ ev_w_in[e]
            u = z[..., :MIX_A]
            q, fl, i, g = jnp.split(z[..., MIX_A:], 4, axis=-1)
            ya = pool_mixer(u, ev_w_pool[e], ev_pool_scale[e])
            yb = hgrn2_mixer(q, fl, i, g, lb_cum[l + 1] - lb_cum[0], ev_hg_norm[e])
            x = x + jnp.concatenate([ya, yb], axis=-1) @ ev_w_out[e]
        else:
            o = l // 2
            h = rmsnorm(x, od_norm[o])
            z = h @ od_w_in[o]
            q = z[..., :D].reshape(B, T, FOX_HEADS, FOX_HEAD)
            k = z[..., D:2 * D].reshape(B, T, FOX_HEADS, FOX_HEAD)
            v = z[..., 2 * D:3 * D].reshape(B, T, FOX_HEADS, FOX_HEAD)
            fl = z[..., 3 * D:] + od_b_f[o]
            x = x + fox_attention(q, k, v, fl) @ od_w_out[o]
        h = rmsnorm(x, xa_norm[l])
        x = x + cross_attention(h, rmsnorm(mem, xa_mem_norm[l]), xa_wq[l], xa_wkv[l], xa_wo[l])
        h = rmsnorm(x, ffn_norm[l])
        x = x + (jax.nn.silu(h @ ffn_w_gate[l]) * (h @ ffn_w_up[l])) @ ffn_w_down[l]
    return rmsnorm(x, final_norm)


import jax as _jax
import jax.numpy as _jnp

TWIN_FORMAT = 'train_step'
FWD_PARAMS = ['x', 'mem', 'lb_table', 'ev_norm', 'ev_w_in', 'ev_w_pool', 'ev_pool_scale', 'ev_hg_norm', 'ev_w_out', 'od_norm', 'od_w_in', 'od_b_f', 'od_w_out', 'xa_norm', 'xa_mem_norm', 'xa_wq', 'xa_wkv', 'xa_wo', 'ffn_norm', 'ffn_w_gate', 'ffn_w_up', 'ffn_w_down', 'final_norm']
TWIN_WEIGHTS = ['lb_table', 'ev_norm', 'ev_w_in', 'ev_w_pool', 'ev_pool_scale', 'ev_hg_norm', 'ev_w_out', 'od_norm', 'od_w_in', 'od_b_f', 'od_w_out', 'xa_norm', 'xa_mem_norm', 'xa_wq', 'xa_wkv', 'xa_wo', 'ffn_norm', 'ffn_w_gate', 'ffn_w_up', 'ffn_w_down', 'final_norm']
TWIN_DIFF_INPUT = 'x'
TWIN_INPUTS = ['x', 'mem', 'lb_table', 'ev_norm', 'ev_w_in', 'ev_w_pool', 'ev_pool_scale', 'ev_hg_norm', 'ev_w_out', 'od_norm', 'od_w_in', 'od_b_f', 'od_w_out', 'xa_norm', 'xa_mem_norm', 'xa_wq', 'xa_wkv', 'xa_wo', 'ffn_norm', 'ffn_w_gate', 'ffn_w_up', 'ffn_w_down', 'final_norm', 'loss_target', 'm_lb_table', 'm_ev_norm', 'm_ev_w_in', 'm_ev_w_pool', 'm_ev_pool_scale', 'm_ev_hg_norm', 'm_ev_w_out', 'm_od_norm', 'm_od_w_in', 'm_od_b_f', 'm_od_w_out', 'm_xa_norm', 'm_xa_mem_norm', 'm_xa_wq', 'm_xa_wkv', 'm_xa_wo', 'm_ffn_norm', 'm_ffn_w_gate', 'm_ffn_w_up', 'm_ffn_w_down', 'm_final_norm', 'v_lb_table', 'v_ev_norm', 'v_ev_w_in', 'v_ev_w_pool', 'v_ev_pool_scale', 'v_ev_hg_norm', 'v_ev_w_out', 'v_od_norm', 'v_od_w_in', 'v_od_b_f', 'v_od_w_out', 'v_xa_norm', 'v_xa_mem_norm', 'v_xa_wq', 'v_xa_wkv', 'v_xa_wo', 'v_ffn_norm', 'v_ffn_w_gate', 'v_ffn_w_up', 'v_ffn_w_down', 'v_final_norm']
TWIN_OUTPUTS = ['loss', 'grad_x', 'grad_lb_table', 'grad_ev_norm', 'grad_ev_w_in', 'grad_ev_w_pool', 'grad_ev_pool_scale', 'grad_ev_hg_norm', 'grad_ev_w_out', 'grad_od_norm', 'grad_od_w_in', 'grad_od_b_f', 'grad_od_w_out', 'grad_xa_norm', 'grad_xa_mem_norm', 'grad_xa_wq', 'grad_xa_wkv', 'grad_xa_wo', 'grad_ffn_norm', 'grad_ffn_w_gate', 'grad_ffn_w_up', 'grad_ffn_w_down', 'grad_final_norm', 'delta_lb_table', 'delta_ev_norm', 'delta_ev_w_in', 'delta_ev_w_pool', 'delta_ev_pool_scale', 'delta_ev_hg_norm', 'delta_ev_w_out', 'delta_od_norm', 'delta_od_w_in', 'delta_od_b_f', 'delta_od_w_out', 'delta_xa_norm', 'delta_xa_mem_norm', 'delta_xa_wq', 'delta_xa_wkv', 'delta_xa_wo', 'delta_ffn_norm', 'delta_ffn_w_gate', 'delta_ffn_w_up', 'delta_ffn_w_down', 'delta_final_norm', 'new_m_lb_table', 'new_m_ev_norm', 'new_m_ev_w_in', 'new_m_ev_w_pool', 'new_m_ev_pool_scale', 'new_m_ev_hg_norm', 'new_m_ev_w_out', 'new_m_od_norm', 'new_m_od_w_in', 'new_m_od_b_f', 'new_m_od_w_out', 'new_m_xa_norm', 'new_m_xa_mem_norm', 'new_m_xa_wq', 'new_m_xa_wkv', 'new_m_xa_wo', 'new_m_ffn_norm', 'new_m_ffn_w_gate', 'new_m_ffn_w_up', 'new_m_ffn_w_down', 'new_m_final_norm', 'new_v_lb_table', 'new_v_ev_norm', 'new_v_ev_w_in', 'new_v_ev_w_pool', 'new_v_ev_pool_scale', 'new_v_ev_hg_norm', 'new_v_ev_w_out', 'new_v_od_norm', 'new_v_od_w_in', 'new_v_od_b_f', 'new_v_od_w_out', 'new_v_xa_norm', 'new_v_xa_mem_norm', 'new_v_xa_wq', 'new_v_xa_wkv', 'new_v_xa_wo', 'new_v_ffn_norm', 'new_v_ffn_w_gate', 'new_v_ffn_w_up', 'new_v_ffn_w_down', 'new_v_final_norm']
TWIN_LEAF_KINDS = {'loss': 'loss', 'grad_x': 'grad_x', 'grad_lb_table': 'grad_w', 'grad_ev_norm': 'grad_w', 'grad_ev_w_in': 'grad_w', 'grad_ev_w_pool': 'grad_w', 'grad_ev_pool_scale': 'grad_w', 'grad_ev_hg_norm': 'grad_w', 'grad_ev_w_out': 'grad_w', 'grad_od_norm': 'grad_w', 'grad_od_w_in': 'grad_w', 'grad_od_b_f': 'grad_w', 'grad_od_w_out': 'grad_w', 'grad_xa_norm': 'grad_w', 'grad_xa_mem_norm': 'grad_w', 'grad_xa_wq': 'grad_w', 'grad_xa_wkv': 'grad_w', 'grad_xa_wo': 'grad_w', 'grad_ffn_norm': 'grad_w', 'grad_ffn_w_gate': 'grad_w', 'grad_ffn_w_up': 'grad_w', 'grad_ffn_w_down': 'grad_w', 'grad_final_norm': 'grad_w', 'delta_lb_table': 'delta_w', 'delta_ev_norm': 'delta_w', 'delta_ev_w_in': 'delta_w', 'delta_ev_w_pool': 'delta_w', 'delta_ev_pool_scale': 'delta_w', 'delta_ev_hg_norm': 'delta_w', 'delta_ev_w_out': 'delta_w', 'delta_od_norm': 'delta_w', 'delta_od_w_in': 'delta_w', 'delta_od_b_f': 'delta_w', 'delta_od_w_out': 'delta_w', 'delta_xa_norm': 'delta_w', 'delta_xa_mem_norm': 'delta_w', 'delta_xa_wq': 'delta_w', 'delta_xa_wkv': 'delta_w', 'delta_xa_wo': 'delta_w', 'delta_ffn_norm': 'delta_w', 'delta_ffn_w_gate': 'delta_w', 'delta_ffn_w_up': 'delta_w', 'delta_ffn_w_down': 'delta_w', 'delta_final_norm': 'delta_w', 'new_m_lb_table': 'new_m', 'new_m_ev_norm': 'new_m', 'new_m_ev_w_in': 'new_m', 'new_m_ev_w_pool': 'new_m', 'new_m_ev_pool_scale': 'new_m', 'new_m_ev_hg_norm': 'new_m', 'new_m_ev_w_out': 'new_m', 'new_m_od_norm': 'new_m', 'new_m_od_w_in': 'new_m', 'new_m_od_b_f': 'new_m', 'new_m_od_w_out': 'new_m', 'new_m_xa_norm': 'new_m', 'new_m_xa_mem_norm': 'new_m', 'new_m_xa_wq': 'new_m', 'new_m_xa_wkv': 'new_m', 'new_m_xa_wo': 'new_m', 'new_m_ffn_norm': 'new_m', 'new_m_ffn_w_gate': 'new_m', 'new_m_ffn_w_up': 'new_m', 'new_m_ffn_w_down': 'new_m', 'new_m_final_norm': 'new_m', 'new_v_lb_table': 'new_v', 'new_v_ev_norm': 'new_v', 'new_v_ev_w_in': 'new_v', 'new_v_ev_w_pool': 'new_v', 'new_v_ev_pool_scale': 'new_v', 'new_v_ev_hg_norm': 'new_v', 'new_v_ev_w_out': 'new_v', 'new_v_od_norm': 'new_v', 'new_v_od_w_in': 'new_v', 'new_v_od_b_f': 'new_v', 'new_v_od_w_out': 'new_v', 'new_v_xa_norm': 'new_v', 'new_v_xa_mem_norm': 'new_v', 'new_v_xa_wq': 'new_v', 'new_v_xa_wkv': 'new_v', 'new_v_xa_wo': 'new_v', 'new_v_ffn_norm': 'new_v', 'new_v_ffn_w_gate': 'new_v', 'new_v_ffn_w_up': 'new_v', 'new_v_ffn_w_down': 'new_v', 'new_v_final_norm': 'new_v'}


def _forward(args):
    return _fwd_reference(*[args[k] for k in FWD_PARAMS])


def _output_shape():
    def fwd():
        inp = _fwd_setup_inputs(0)
        return _fwd_reference(*[inp[k] for k in FWD_PARAMS])
    out = _jax.eval_shape(fwd)
    return out.shape, out.dtype

N_MICROBATCH = 1
ADAM_LR = 0.001
ADAM_B1 = 0.9
ADAM_B2 = 0.999
ADAM_EPS = 1e-08
ADAM_WD = 0.01
ADAM_STEP = 10
PER_EXAMPLE_BATCH_AXIS = {'x': 0, 'mem': 0, 'loss_target': 0}
SHARED_INPUTS = []
_WEIGHT_DTYPES = {'lb_table': _jnp.float32, 'ev_norm': _jnp.float32, 'ev_w_in': _jnp.float32, 'ev_w_pool': _jnp.float32, 'ev_pool_scale': _jnp.float32, 'ev_hg_norm': _jnp.float32, 'ev_w_out': _jnp.float32, 'od_norm': _jnp.float32, 'od_w_in': _jnp.float32, 'od_b_f': _jnp.float32, 'od_w_out': _jnp.float32, 'xa_norm': _jnp.float32, 'xa_mem_norm': _jnp.float32, 'xa_wq': _jnp.float32, 'xa_wkv': _jnp.float32, 'xa_wo': _jnp.float32, 'ffn_norm': _jnp.float32, 'ffn_w_gate': _jnp.float32, 'ffn_w_up': _jnp.float32, 'ffn_w_down': _jnp.float32, 'final_norm': _jnp.float32}
MOMENT_SCALE = {'lb_table': 2.459078e-03, 'ev_norm': 7.792799e-02, 'ev_w_in': 4.836474e-02, 'ev_w_pool': 7.752797e-02, 'ev_pool_scale': 7.861300e-02, 'ev_hg_norm': 1.565847e-01, 'ev_w_out': 6.651850e-02, 'od_norm': 4.167290e-02, 'od_w_in': 2.349142e-02, 'od_b_f': 2.430975e-01, 'od_w_out': 2.725661e-02, 'xa_norm': 8.031961e-03, 'xa_mem_norm': 1.194683e-02, 'xa_wq': 7.983470e-03, 'xa_wkv': 8.207335e-03, 'xa_wo': 8.425558e-03, 'ffn_norm': 5.605874e-02, 'ffn_w_gate': 2.426214e-02, 'ffn_w_up': 2.356508e-02, 'ffn_w_down': 3.911742e-02, 'final_norm': 1.600271e+01}


def _to_microbatches(a, axis):
    t = _jnp.moveaxis(a, axis, 0)
    t = t.reshape((N_MICROBATCH, t.shape[0] // N_MICROBATCH) + t.shape[1:])
    return _jnp.moveaxis(t, 1, axis + 1)


def setup_inputs(seed: int = 0) -> dict:
    inp = _fwd_setup_inputs(seed)
    key = _jax.random.fold_in(_jax.random.key(seed), 7919)
    shape, _ = _output_shape()
    out = dict(inp)
    out["loss_target"] = _jax.random.normal(_jax.random.fold_in(key, 0), shape, _jnp.float32)
    for i, name in enumerate(TWIN_WEIGHTS):
        w = inp[name].astype(_jnp.float32)
        if MOMENT_SCALE is None:
            s = _jnp.sqrt(_jnp.mean(_jnp.square(w)) + 1e-30)
        else:
            s = MOMENT_SCALE[name]
        km, kv = _jax.random.split(_jax.random.fold_in(key, i + 1))
        out[name] = w
        out["m_" + name] = s * _jax.random.normal(km, w.shape, _jnp.float32)
        out["v_" + name] = (s * s) * _jax.random.uniform(kv, w.shape, _jnp.float32, 0.5, 1.5)
    if N_MICROBATCH > 1:
        for name, axis in PER_EXAMPLE_BATCH_AXIS.items():
            out[name] = _to_microbatches(out[name], axis)
    return {'x': out['x'], 'mem': out['mem'], 'lb_table': out['lb_table'], 'ev_norm': out['ev_norm'], 'ev_w_in': out['ev_w_in'], 'ev_w_pool': out['ev_w_pool'], 'ev_pool_scale': out['ev_pool_scale'], 'ev_hg_norm': out['ev_hg_norm'], 'ev_w_out': out['ev_w_out'], 'od_norm': out['od_norm'], 'od_w_in': out['od_w_in'], 'od_b_f': out['od_b_f'], 'od_w_out': out['od_w_out'], 'xa_norm': out['xa_norm'], 'xa_mem_norm': out['xa_mem_norm'], 'xa_wq': out['xa_wq'], 'xa_wkv': out['xa_wkv'], 'xa_wo': out['xa_wo'], 'ffn_norm': out['ffn_norm'], 'ffn_w_gate': out['ffn_w_gate'], 'ffn_w_up': out['ffn_w_up'], 'ffn_w_down': out['ffn_w_down'], 'final_norm': out['final_norm'], 'loss_target': out['loss_target'], 'm_lb_table': out['m_lb_table'], 'm_ev_norm': out['m_ev_norm'], 'm_ev_w_in': out['m_ev_w_in'], 'm_ev_w_pool': out['m_ev_w_pool'], 'm_ev_pool_scale': out['m_ev_pool_scale'], 'm_ev_hg_norm': out['m_ev_hg_norm'], 'm_ev_w_out': out['m_ev_w_out'], 'm_od_norm': out['m_od_norm'], 'm_od_w_in': out['m_od_w_in'], 'm_od_b_f': out['m_od_b_f'], 'm_od_w_out': out['m_od_w_out'], 'm_xa_norm': out['m_xa_norm'], 'm_xa_mem_norm': out['m_xa_mem_norm'], 'm_xa_wq': out['m_xa_wq'], 'm_xa_wkv': out['m_xa_wkv'], 'm_xa_wo': out['m_xa_wo'], 'm_ffn_norm': out['m_ffn_norm'], 'm_ffn_w_gate': out['m_ffn_w_gate'], 'm_ffn_w_up': out['m_ffn_w_up'], 'm_ffn_w_down': out['m_ffn_w_down'], 'm_final_norm': out['m_final_norm'], 'v_lb_table': out['v_lb_table'], 'v_ev_norm': out['v_ev_norm'], 'v_ev_w_in': out['v_ev_w_in'], 'v_ev_w_pool': out['v_ev_w_pool'], 'v_ev_pool_scale': out['v_ev_pool_scale'], 'v_ev_hg_norm': out['v_ev_hg_norm'], 'v_ev_w_out': out['v_ev_w_out'], 'v_od_norm': out['v_od_norm'], 'v_od_w_in': out['v_od_w_in'], 'v_od_b_f': out['v_od_b_f'], 'v_od_w_out': out['v_od_w_out'], 'v_xa_norm': out['v_xa_norm'], 'v_xa_mem_norm': out['v_xa_mem_norm'], 'v_xa_wq': out['v_xa_wq'], 'v_xa_wkv': out['v_xa_wkv'], 'v_xa_wo': out['v_xa_wo'], 'v_ffn_norm': out['v_ffn_norm'], 'v_ffn_w_gate': out['v_ffn_w_gate'], 'v_ffn_w_up': out['v_ffn_w_up'], 'v_ffn_w_down': out['v_ffn_w_down'], 'v_final_norm': out['v_final_norm']}


def _loss(weights, diff, rest, loss_target):
    with _jax.named_scope("forward"):
        args = {**rest, TWIN_DIFF_INPUT: diff, **{k: w.astype(_WEIGHT_DTYPES[k]) for k, w in weights.items()}}
        y = _forward(args)
    with _jax.named_scope("loss_head"):
        err = _jnp.square(y.astype(_jnp.float32) - loss_target)
        return 0.5 * _jnp.sum(_jnp.mean(err, axis=-1)) if err.ndim else 0.5 * err


def _adamw(w, g, m, v):
    m = ADAM_B1 * m + (1.0 - ADAM_B1) * g
    v = ADAM_B2 * v + (1.0 - ADAM_B2) * _jnp.square(g)
    m_hat = m / (1.0 - ADAM_B1 ** ADAM_STEP)
    v_hat = v / (1.0 - ADAM_B2 ** ADAM_STEP)
    delta = -ADAM_LR * (m_hat / (_jnp.sqrt(v_hat) + ADAM_EPS) + ADAM_WD * w)
    return delta, m, v


def reference(x, mem, lb_table, ev_norm, ev_w_in, ev_w_pool, ev_pool_scale, ev_hg_norm, ev_w_out, od_norm, od_w_in, od_b_f, od_w_out, xa_norm, xa_mem_norm, xa_wq, xa_wkv, xa_wo, ffn_norm, ffn_w_gate, ffn_w_up, ffn_w_down, final_norm, loss_target, m_lb_table, m_ev_norm, m_ev_w_in, m_ev_w_pool, m_ev_pool_scale, m_ev_hg_norm, m_ev_w_out, m_od_norm, m_od_w_in, m_od_b_f, m_od_w_out, m_xa_norm, m_xa_mem_norm, m_xa_wq, m_xa_wkv, m_xa_wo, m_ffn_norm, m_ffn_w_gate, m_ffn_w_up, m_ffn_w_down, m_final_norm, v_lb_table, v_ev_norm, v_ev_w_in, v_ev_w_pool, v_ev_pool_scale, v_ev_hg_norm, v_ev_w_out, v_od_norm, v_od_w_in, v_od_b_f, v_od_w_out, v_xa_norm, v_xa_mem_norm, v_xa_wq, v_xa_wkv, v_xa_wo, v_ffn_norm, v_ffn_w_gate, v_ffn_w_up, v_ffn_w_down, v_final_norm):
    given = dict(x=x, mem=mem, lb_table=lb_table, ev_norm=ev_norm, ev_w_in=ev_w_in, ev_w_pool=ev_w_pool, ev_pool_scale=ev_pool_scale, ev_hg_norm=ev_hg_norm, ev_w_out=ev_w_out, od_norm=od_norm, od_w_in=od_w_in, od_b_f=od_b_f, od_w_out=od_w_out, xa_norm=xa_norm, xa_mem_norm=xa_mem_norm, xa_wq=xa_wq, xa_wkv=xa_wkv, xa_wo=xa_wo, ffn_norm=ffn_norm, ffn_w_gate=ffn_w_gate, ffn_w_up=ffn_w_up, ffn_w_down=ffn_w_down, final_norm=final_norm, loss_target=loss_target, m_lb_table=m_lb_table, m_ev_norm=m_ev_norm, m_ev_w_in=m_ev_w_in, m_ev_w_pool=m_ev_w_pool, m_ev_pool_scale=m_ev_pool_scale, m_ev_hg_norm=m_ev_hg_norm, m_ev_w_out=m_ev_w_out, m_od_norm=m_od_norm, m_od_w_in=m_od_w_in, m_od_b_f=m_od_b_f, m_od_w_out=m_od_w_out, m_xa_norm=m_xa_norm, m_xa_mem_norm=m_xa_mem_norm, m_xa_wq=m_xa_wq, m_xa_wkv=m_xa_wkv, m_xa_wo=m_xa_wo, m_ffn_norm=m_ffn_norm, m_ffn_w_gate=m_ffn_w_gate, m_ffn_w_up=m_ffn_w_up, m_ffn_w_down=m_ffn_w_down, m_final_norm=m_final_norm, v_lb_table=v_lb_table, v_ev_norm=v_ev_norm, v_ev_w_in=v_ev_w_in, v_ev_w_pool=v_ev_w_pool, v_ev_pool_scale=v_ev_pool_scale, v_ev_hg_norm=v_ev_hg_norm, v_ev_w_out=v_ev_w_out, v_od_norm=v_od_norm, v_od_w_in=v_od_w_in, v_od_b_f=v_od_b_f, v_od_w_out=v_od_w_out, v_xa_norm=v_xa_norm, v_xa_mem_norm=v_xa_mem_norm, v_xa_wq=v_xa_wq, v_xa_wkv=v_xa_wkv, v_xa_wo=v_xa_wo, v_ffn_norm=v_ffn_norm, v_ffn_w_gate=v_ffn_w_gate, v_ffn_w_up=v_ffn_w_up, v_ffn_w_down=v_ffn_w_down, v_final_norm=v_final_norm)
    weights = {n: given[n] for n in TWIN_WEIGHTS}
    shared = {n: given[n] for n in SHARED_INPUTS}
    per_example = {n: given[n] for n in ['x', 'mem']}
    grad_fn = _jax.value_and_grad(_loss, argnums=(0, 1))

    def one_microbatch(ex, loss_target):
        ex = dict(ex)
        diff = ex.pop(TWIN_DIFF_INPUT)
        return grad_fn(weights, diff, {**shared, **ex}, loss_target)

    if N_MICROBATCH == 1:
        loss, (grad_w, grad_x) = one_microbatch(per_example, given["loss_target"])
    else:
        def body(carry, xs):
            loss_sum, grad_sum = carry
            l_k, (gw_k, gx_k) = one_microbatch(xs[0], xs[1])
            with _jax.named_scope("update"):
                return (loss_sum + l_k, _jax.tree.map(_jnp.add, grad_sum, gw_k)), gx_k

        init = (_jnp.zeros((), _jnp.float32), _jax.tree.map(_jnp.zeros_like, weights))
        (loss, grad_w), grad_x = _jax.lax.scan(body, init, (per_example, given["loss_target"]))
    with _jax.named_scope("update"):
        delta_w, new_m, new_v = {}, {}, {}
        for n in TWIN_WEIGHTS:
            delta_w[n], new_m[n], new_v[n] = _adamw(weights[n], grad_w[n], given["m_" + n], given["v_" + n])
    return (loss, grad_x, *[grad_w[n] for n in TWIN_WEIGHTS], *[delta_w[n] for n in TWIN_WEIGHTS],
            *[new_m[n] for n in TWIN_WEIGHTS], *[new_v[n] for n in TWIN_WEIGHTS])
```

```python
import functools

import jax
import jax.numpy as jnp
from jax import lax
from jax.experimental import pallas as pl
from jax.experimental.pallas import tpu as pltpu

F32 = jnp.float32
MXU_DTYPE = jnp.bfloat16
EPS = 1e-6
N_DEV = 8
V7X_VMEM_BYTES = 64 * 1024 * 1024
VMEM_LIMIT_BYTES = V7X_VMEM_BYTES - 8 * 1024 * 1024
LANES = 128
HIGHEST = lax.Precision.HIGHEST
MESH = pl.DeviceIdType.MESH

HG_HEAD = 128
HG_CHUNK = 16
FOX_HEAD = 128
XA_HEADS = 4
POOL_GROUPS = 4

ADAM_LR = 0.001
ADAM_B1 = 0.9
ADAM_B2 = 0.999
ADAM_EPS = 1e-08
ADAM_WD = 0.01
ADAM_STEP = 10

_NN = ((1,), (0,))
_NT = ((1,), (1,))
_TN = ((0,), (0,))


def _dot(a, b, dims):
    return lax.dot_general(a.astype(MXU_DTYPE), b.astype(MXU_DTYPE), (dims, ((), ())),
                           preferred_element_type=F32)


def _dot_f32(a, b, dims):
    return lax.dot_general(a, b, (dims, ((), ())), preferred_element_type=F32, precision=HIGHEST)


def _cp(*sem):
    return pltpu.CompilerParams(dimension_semantics=sem, vmem_limit_bytes=VMEM_LIMIT_BYTES)


def _tile(n, pref, align=LANES):
    if n <= pref:
        return n
    t = (pref // align) * align
    while t >= align:
        if n % t == 0:
            return t
        t -= align
    return n


def _sigmoid(x):
    return jax.nn.sigmoid(x)


def _mm(a, b, mode, out_dtype, *, name, add=None, reduce_b=False, b_split=False, o_split=False,
        tm=1024, tn=1024, tk=512):
    ba, bb = a.shape[0], b.shape[0]
    if mode == "tn":
        kdim, m = a.shape[1], a.shape[2]
    else:
        m, kdim = a.shape[1], a.shape[2]
    if b_split:
        s_cnt, b_rows, w = b.shape
        if mode == "nt":
            n = b_rows
            assert kdim == s_cnt * w
            tk = w
        else:
            n = s_cnt * w
            assert b_rows == kdim
            tn = w
        nb = ba
    else:
        n = b.shape[1] if mode == "nt" else b.shape[2]
        nb = max(ba, bb)
    if not (b_split and mode != "nt"):
        tn = _tile(n, tn)
    if not (b_split and mode == "nt"):
        tk = _tile(kdim, tk)
    tm = _tile(m, tm)
    assert m % tm == 0 and n % tn == 0 and kdim % tk == 0, (name, m, n, kdim, tm, tn, tk)
    nk = kdim // tk
    if reduce_b:
        grid = (m // tm, n // tn, nb, nk)
        unpack = lambda i, j, bi, k: (bi, i, j, k)
        sem = ("parallel", "parallel", "arbitrary", "arbitrary")
        nred = nb * nk
    else:
        grid = (nb, m // tm, n // tn, nk)
        unpack = lambda bi, i, j, k: (bi, i, j, k)
        sem = ("parallel", "parallel", "parallel", "arbitrary")
        nred = nk

    def a_map(*g):
        bi, i, j, k = unpack(*g)
        ab = bi if ba > 1 else 0
        return (ab, k, i) if mode == "tn" else (ab, i, k)

    def b_map(*g):
        bi, i, j, k = unpack(*g)
        if b_split:
            return (k, j, 0) if mode == "nt" else (j, k, 0)
        bq = bi if bb > 1 else 0
        return (bq, j, k) if mode == "nt" else (bq, k, j)

    def o_map(*g):
        bi, i, j, k = unpack(*g)
        if o_split:
            return (j, i, 0)
        return (0 if reduce_b else bi, i, j)

    a_blk = (1, tk, tm) if mode == "tn" else (1, tm, tk)
    b_blk = (1, tn, tk) if mode == "nt" else (1, tk, tn)
    dims = {"nn": _NN, "nt": _NT, "tn": _TN}[mode]
    has_add = add is not None

    def body(*refs):
        if has_add:
            a_ref, b_ref, add_ref, o_ref, acc_ref = refs
        else:
            a_ref, b_ref, o_ref, acc_ref = refs
        if reduce_b:
            step = pl.program_id(2) * nk + pl.program_id(3)
        else:
            step = pl.program_id(3)

        @pl.when(step == 0)
        def _():
            acc_ref[...] = jnp.zeros_like(acc_ref)

        acc_ref[...] += _dot(a_ref[0], b_ref[0], dims)

        @pl.when(step == nred - 1)
        def _():
            r = acc_ref[...]
            if has_add:
                r = r + add_ref[0].astype(F32)
            o_ref[0] = r.astype(o_ref.dtype)

    in_specs = [pl.BlockSpec(a_blk, a_map), pl.BlockSpec(b_blk, b_map)]
    operands = [a, b]
    if has_add:
        in_specs.append(pl.BlockSpec((1, tm, tn), o_map))
        operands.append(add)
    if o_split:
        out_shape = jax.ShapeDtypeStruct((n // tn, m, tn), out_dtype)
    else:
        out_shape = jax.ShapeDtypeStruct((1 if reduce_b else nb, m, n), out_dtype)
    return pl.pallas_call(
        body, grid=grid, in_specs=in_specs, out_specs=pl.BlockSpec((1, tm, tn), o_map),
        out_shape=out_shape, scratch_shapes=[pltpu.VMEM((tm, tn), F32)],
        compiler_params=_cp(*sem), name=name)(*operands)


def _mm2(a, b, mode, out_dtype, *, name, add=None, **kw):
    b3 = b if kw.get("b_split") else b[None]
    r = _mm(a[None], b3, mode, out_dtype, name=name, add=None if add is None else add[None], **kw)
    return r if kw.get("o_split") else r[0]


def _rms_fwd(x, g, *, name, tb=512):
    t, d = x.shape
    tb = min(tb, t)

    def body(x_ref, g_ref, o_ref):
        xv = x_ref[...]
        r = lax.rsqrt(jnp.mean(xv * xv, axis=-1, keepdims=True) + EPS)
        o_ref[...] = (xv * r * g_ref[...]).astype(o_ref.dtype)

    return pl.pallas_call(
        body, grid=(t // tb,),
        in_specs=[pl.BlockSpec((tb, d), lambda i: (i, 0)), pl.BlockSpec((1, d), lambda i: (0, 0))],
        out_specs=pl.BlockSpec((tb, d), lambda i: (i, 0)),
        out_shape=jax.ShapeDtypeStruct((t, d), MXU_DTYPE), compiler_params=_cp("parallel"), name=name)(x, g)


def _rms_bwd(x, g, dh, dres, *, name, tb=512):
    t, d = x.shape
    tb = min(tb, t)
    has_res = dres is not None

    def body(*refs):
        if has_res:
            x_ref, g_ref, dh_ref, dres_ref, dx_ref, dxl_ref, dg_ref = refs
        else:
            x_ref, g_ref, dh_ref, dx_ref, dxl_ref, dg_ref = refs
        xv = x_ref[...]
        r = lax.rsqrt(jnp.mean(xv * xv, axis=-1, keepdims=True) + EPS)
        xh = xv * r
        dhv = dh_ref[...].astype(F32)

        @pl.when(pl.program_id(0) == 0)
        def _():
            dg_ref[...] = jnp.zeros_like(dg_ref)

        dg_ref[...] += jnp.sum(dhv * xh, axis=0, keepdims=True)
        dxh = dhv * g_ref[...]
        dx = r * (dxh - xh * jnp.mean(dxh * xh, axis=-1, keepdims=True))
        if has_res:
            dx = dx + dres_ref[...]
        dx_ref[...] = dx
        dxl_ref[...] = dx.astype(dxl_ref.dtype)

    row = pl.BlockSpec((tb, d), lambda i: (i, 0))
    vec = pl.BlockSpec((1, d), lambda i: (0, 0))
    operands = [x, g, dh] + ([dres] if has_res else [])
    return pl.pallas_call(
        body, grid=(t // tb,), in_specs=[row, vec, row] + ([row] if has_res else []),
        out_specs=[row, row, vec],
        out_shape=[jax.ShapeDtypeStruct((t, d), F32), jax.ShapeDtypeStruct((t, d), MXU_DTYPE),
                   jax.ShapeDtypeStruct((1, d), F32)],
        compiler_params=_cp("arbitrary"), name=name)(*operands)


def _loss_head(x, g, target, *, name, tb=512):
    t, d = x.shape
    tb = min(tb, t)

    def body(x_ref, g_ref, t_ref, loss_ref, dx_ref, dxl_ref, dg_ref):
        xv = x_ref[...]
        r = lax.rsqrt(jnp.mean(xv * xv, axis=-1, keepdims=True) + EPS)
        xh = xv * r
        gv = g_ref[...]
        err = xh * gv - t_ref[...]

        @pl.when(pl.program_id(0) == 0)
        def _():
            dg_ref[...] = jnp.zeros_like(dg_ref)
            loss_ref[...] = jnp.zeros_like(loss_ref)

        row_loss = jnp.mean(err * err, axis=-1, keepdims=True)
        loss_ref[...] += 0.5 * jnp.sum(row_loss, axis=0, keepdims=True)
        dy = err * (1.0 / d)
        dg_ref[...] += jnp.sum(dy * xh, axis=0, keepdims=True)
        dxh = dy * gv
        dx = r * (dxh - xh * jnp.mean(dxh * xh, axis=-1, keepdims=True))
        dx_ref[...] = dx
        dxl_ref[...] = dx.astype(dxl_ref.dtype)

    row = pl.BlockSpec((tb, d), lambda i: (i, 0))
    vec = pl.BlockSpec((1, d), lambda i: (0, 0))
    return pl.pallas_call(
        body, grid=(t // tb,), in_specs=[row, vec, row],
        out_specs=[pl.BlockSpec((1, 1), lambda i: (0, 0)), row, row, vec],
        out_shape=[jax.ShapeDtypeStruct((1, 1), F32), jax.ShapeDtypeStruct((t, d), F32),
                   jax.ShapeDtypeStruct((t, d), MXU_DTYPE), jax.ShapeDtypeStruct((1, d), F32)],
        compiler_params=_cp("arbitrary"), name=name)(x, g, target)


def _ffn_up(h, wg, wu, *, name, tb=512):
    t, d = h.shape
    s, _, f = wg.shape
    tb = min(tb, t)

    def body(h_ref, wg_ref, wu_ref, g_ref, u_ref, a_ref):
        hv = h_ref[...]
        gv = _dot(hv, wg_ref[0], _NN)
        uv = _dot(hv, wu_ref[0], _NN)
        g_ref[0] = gv
        u_ref[0] = uv
        a_ref[0] = (gv * _sigmoid(gv) * uv).astype(a_ref.dtype)

    wspec = pl.BlockSpec((1, d, f), lambda j, i: (j, 0, 0))
    ospec = pl.BlockSpec((1, tb, f), lambda j, i: (j, i, 0))
    return pl.pallas_call(
        body, grid=(s, t // tb),
        in_specs=[pl.BlockSpec((tb, d), lambda j, i: (i, 0)), wspec, wspec],
        out_specs=[ospec, ospec, ospec],
        out_shape=[jax.ShapeDtypeStruct((s, t, f), F32), jax.ShapeDtypeStruct((s, t, f), F32),
                   jax.ShapeDtypeStruct((s, t, f), MXU_DTYPE)],
        compiler_params=_cp("parallel", "parallel"), name=name)(h, wg, wu)


def _ffn_dact(dy, wd, gate, up, *, name, tb=512):
    t, d = dy.shape
    s, f, _ = wd.shape
    tb = min(tb, t)

    def body(dy_ref, wd_ref, g_ref, u_ref, dg_ref, du_ref):
        da = _dot(dy_ref[...], wd_ref[0], _NT)
        gv = g_ref[0]
        sg = _sigmoid(gv)
        du_ref[0] = (da * gv * sg).astype(du_ref.dtype)
        dg_ref[0] = (da * u_ref[0] * (sg * (1.0 + gv * (1.0 - sg)))).astype(dg_ref.dtype)

    aspec = pl.BlockSpec((1, tb, f), lambda j, i: (j, i, 0))
    return pl.pallas_call(
        body, grid=(s, t // tb),
        in_specs=[pl.BlockSpec((tb, d), lambda j, i: (i, 0)),
                  pl.BlockSpec((1, f, d), lambda j, i: (j, 0, 0)), aspec, aspec],
        out_specs=[aspec, aspec],
        out_shape=[jax.ShapeDtypeStruct((s, t, f), MXU_DTYPE), jax.ShapeDtypeStruct((s, t, f), MXU_DTYPE)],
        compiler_params=_cp("parallel", "parallel"), name=name)(dy, wd, gate, up)


def _xattn_fwd(q, kv, *, name, tb=512):
    t, d = q.shape
    m = kv.shape[0]
    hd = d // XA_HEADS
    tb = min(tb, t)
    scale = hd ** -0.5

    def body(q_ref, kv_ref, o_ref):
        for hh in range(XA_HEADS):
            cs = slice(hh * hd, (hh + 1) * hd)
            s = _dot(q_ref[:, cs], kv_ref[:, cs], _NT) * scale
            s = s - jnp.max(s, axis=-1, keepdims=True)
            e = jnp.exp(s)
            p = e / jnp.sum(e, axis=-1, keepdims=True)
            o_ref[:, cs] = _dot(p, kv_ref[:, d + hh * hd:d + (hh + 1) * hd], _NN).astype(o_ref.dtype)

    return pl.pallas_call(
        body, grid=(t // tb,),
        in_specs=[pl.BlockSpec((tb, d), lambda i: (i, 0)), pl.BlockSpec((m, 2 * d), lambda i: (0, 0))],
        out_specs=pl.BlockSpec((tb, d), lambda i: (i, 0)),
        out_shape=jax.ShapeDtypeStruct((t, d), MXU_DTYPE), compiler_params=_cp("parallel"), name=name)(q, kv)


def _xattn_bwd(q, kv, do, *, name, tb=512):
    t, d = q.shape
    m = kv.shape[0]
    hd = d // XA_HEADS
    tb = min(tb, t)
    scale = hd ** -0.5

    def body(q_ref, kv_ref, do_ref, dq_ref, dkv_ref):
        @pl.when(pl.program_id(0) == 0)
        def _():
            dkv_ref[...] = jnp.zeros_like(dkv_ref)

        for hh in range(XA_HEADS):
            cs = slice(hh * hd, (hh + 1) * hd)
            vs = slice(d + hh * hd, d + (hh + 1) * hd)
            qv, kk, vv, dov = q_ref[:, cs], kv_ref[:, cs], kv_ref[:, vs], do_ref[:, cs]
            s = _dot(qv, kk, _NT) * scale
            s = s - jnp.max(s, axis=-1, keepdims=True)
            e = jnp.exp(s)
            p = e / jnp.sum(e, axis=-1, keepdims=True)
            dkv_ref[:, vs] += _dot(p, dov, _TN)
            dp = _dot(dov, vv, _NT)
            ds = p * (dp - jnp.sum(p * dp, axis=-1, keepdims=True)) * scale
            dq_ref[:, cs] = _dot(ds, kk, _NN).astype(dq_ref.dtype)
            dkv_ref[:, cs] += _dot(ds, qv, _TN)

    row = pl.BlockSpec((tb, d), lambda i: (i, 0))
    full = pl.BlockSpec((m, 2 * d), lambda i: (0, 0))
    return pl.pallas_call(
        body, grid=(t // tb,), in_specs=[row, full, row], out_specs=[row, full],
        out_shape=[jax.ShapeDtypeStruct((t, d), MXU_DTYPE), jax.ShapeDtypeStruct((m, 2 * d), F32)],
        compiler_params=_cp("arbitrary"), name=name)(q, kv, do)


def _pool_window_stats(u, gi, reverse):
    t = u.shape[0]
    row = lax.broadcasted_iota(jnp.int32, u.shape, 0)
    s = u
    for j in range(POOL_GROUPS):
        sh = 1 << j
        if reverse:
            rolled = jnp.where(row < t - sh, pltpu.roll(s, t - sh, axis=0), 0.0)
        else:
            rolled = jnp.where(row >= sh, pltpu.roll(s, sh, axis=0), 0.0)
        s = jnp.where(j <= gi, s + rolled, s)
    return s, row


def _pool_fwd(z, w_pool, scale, *, name):
    t = z.shape[0]
    g_cnt, c, _ = w_pool.shape

    def body(z_ref, w_ref, s_ref, o_ref):
        gi = pl.program_id(0)
        u = z_ref[...]
        win, row = _pool_window_stats(u, gi, False)
        cnt = jnp.minimum(row + 1, lax.shift_left(jnp.int32(2), gi)).astype(F32)
        p = win / cnt - u
        o_ref[...] = (_dot(p, w_ref[0], _NN) * s_ref[...]).astype(o_ref.dtype)

    return pl.pallas_call(
        body, grid=(g_cnt,),
        in_specs=[pl.BlockSpec((t, c), lambda g: (0, g)), pl.BlockSpec((1, c, c), lambda g: (g, 0, 0)),
                  pl.BlockSpec((1, c), lambda g: (0, g))],
        out_specs=pl.BlockSpec((t, c), lambda g: (0, g)),
        out_shape=jax.ShapeDtypeStruct((t, 2 * g_cnt * c), MXU_DTYPE),
        compiler_params=_cp("parallel"), name=name)(z, w_pool, scale)


def _pool_bwd(z, w_pool, scale, dycat, *, name):
    t = z.shape[0]
    g_cnt, c, _ = w_pool.shape

    def body(z_ref, w_ref, s_ref, dy_ref, du_ref, dw_ref, ds_ref):
        gi = pl.program_id(0)
        u = z_ref[...]
        win, row = _pool_window_stats(u, gi, False)
        cnt = jnp.minimum(row + 1, lax.shift_left(jnp.int32(2), gi)).astype(F32)
        p = win / cnt - u
        y = _dot(p, w_ref[0], _NN)
        dya = dy_ref[...].astype(F32)
        ds_ref[...] = jnp.sum(dya * y, axis=0, keepdims=True)
        dy = dya * s_ref[...]
        dw_ref[0] = _dot(p, dy, _TN)
        dp = _dot(dy, w_ref[0], _NT)
        back, _ = _pool_window_stats(dp / cnt, gi, True)
        du_ref[...] = (back - dp).astype(du_ref.dtype)

    col = pl.BlockSpec((t, c), lambda g: (0, g))
    return pl.pallas_call(
        body, grid=(g_cnt,),
        in_specs=[col, pl.BlockSpec((1, c, c), lambda g: (g, 0, 0)), pl.BlockSpec((1, c), lambda g: (0, g)), col],
        out_specs=[col, pl.BlockSpec((1, c, c), lambda g: (g, 0, 0)), pl.BlockSpec((1, c), lambda g: (0, g))],
        out_shape=[jax.ShapeDtypeStruct((t, g_cnt * c), MXU_DTYPE), jax.ShapeDtypeStruct((g_cnt, c, c), F32),
                   jax.ShapeDtypeStruct((1, g_cnt * c), F32)],
        compiler_params=_cp("parallel"), name=name)(z, w_pool, scale, dycat)


def _chunk_tri(lower):
    r = lax.broadcasted_iota(jnp.int32, (LANES, LANES), 0)
    c = lax.broadcasted_iota(jnp.int32, (LANES, LANES), 1)
    same = (r // HG_CHUNK) == (c // HG_CHUNK)
    return jnp.where(same & ((c <= r) if lower else (c >= r)), 1.0, 0.0).astype(F32)


def _hgrn_prepare(q_ref, f_ref, lb_ref, qh_s, k_s, b_s, qt_s, kt_s, gl_s):
    tb = q_ref.shape[0]
    lb = lb_ref[...]
    sg = _sigmoid(f_ref[...])
    f = lb + (1.0 - lb) * sg
    logf = jnp.log(f)
    qv = q_ref[...]
    qh = qv * _sigmoid(qv) * (HG_HEAD ** -0.5)
    tri = _chunk_tri(True)
    for r in range(tb // LANES):
        rows = slice(r * LANES, (r + 1) * LANES)
        b_s[rows, :] = _dot_f32(tri, logf[rows, :], _NN)
    b = b_s[...]
    b3 = b.reshape(tb // HG_CHUNK, HG_CHUNK, HG_HEAD)
    bl = b3[:, HG_CHUNK - 1:HG_CHUNK, :]
    k = 1.0 - f
    qh_s[...] = qh
    k_s[...] = k
    qt_s[...] = qh * jnp.exp(b)
    kt_s[...] = k * jnp.exp(bl - b3).reshape(tb, HG_HEAD)
    gl_s[...] = jnp.exp(jnp.broadcast_to(bl, b3.shape)).reshape(tb, HG_HEAD)
    return sg, f


def _hgrn_intra(qh, kk, bq, rows_a, rows_b):
    ones = jnp.ones((HG_HEAD, HG_HEAD), MXU_DTYPE)
    es, stack_a, stack_b = [], [], []
    for s in range(HG_CHUNK):
        e = jnp.exp(jnp.minimum(bq - bq[s:s + 1, :], 0.0))
        es.append(e)
        stack_a.append(qh * e * kk[s:s + 1, :])
        if rows_a is not None:
            stack_b.append(rows_a * rows_b[s:s + 1, :])
    a_rep = _dot(jnp.concatenate(stack_a, axis=0), ones, _NN)
    d_rep = _dot(jnp.concatenate(stack_b, axis=0), ones, _NN) if rows_a is not None else None
    return es, a_rep, d_rep


def _hgrn_fwd(z, lb, hg_norm, ycat, *, name, tb=512):
    t = z.shape[0]
    mix_b = lb.shape[1]
    heads = mix_b // HG_HEAD
    off = (z.shape[1] - 4 * mix_b) // HG_HEAD
    tb = min(tb, t)
    ncb = tb // HG_CHUNK

    def body(q_ref, f_ref, i_ref, g_ref, lb_ref, hn_ref, ycat_in, y_ref, o_ref, st_ref,
             state, qh_s, k_s, b_s, qt_s, kt_s, gl_s, o_s):
        del ycat_in

        @pl.when(pl.program_id(1) == 0)
        def _():
            state[...] = jnp.zeros_like(state)

        _hgrn_prepare(q_ref, f_ref, lb_ref, qh_s, k_s, b_s, qt_s, kt_s, gl_s)
        row = lax.broadcasted_iota(jnp.int32, (HG_CHUNK, HG_HEAD), 0)

        def chunk(c, carry):
            rows = pl.ds(pl.multiple_of(c * HG_CHUNK, HG_CHUNK), HG_CHUNK)
            st = state[...]
            st_ref[0, c] = st
            vv = i_ref[rows, :]
            o = _dot(qt_s[rows, :], st, _NT)
            _, a_rep, _ = _hgrn_intra(qh_s[rows, :], k_s[rows, :], b_s[rows, :], None, None)
            for s in range(HG_CHUNK):
                o = o + jnp.where(row >= s, a_rep[s * HG_CHUNK:(s + 1) * HG_CHUNK, :] * vv[s:s + 1, :], 0.0)
            o_s[rows, :] = o
            state[...] = st * gl_s[rows, :][0:1, :] + _dot(vv, kt_s[rows, :], _TN)
            return carry

        lax.fori_loop(0, ncb, chunk, 0)
        o = o_s[...]
        o_ref[...] = o
        r = lax.rsqrt(jnp.mean(o * o, axis=-1, keepdims=True) + EPS)
        gv = g_ref[...]
        y_ref[...] = (o * r * hn_ref[...] * (gv * _sigmoid(gv))).astype(y_ref.dtype)

    def zcol(kind):
        return pl.BlockSpec((tb, HG_HEAD), lambda h, i: (i, off + kind * heads + h))

    scratch = [pltpu.VMEM((HG_HEAD, HG_HEAD), F32)] + [pltpu.VMEM((tb, HG_HEAD), F32)] * 7
    return pl.pallas_call(
        body, grid=(heads, t // tb),
        in_specs=[zcol(0), zcol(1), zcol(2), zcol(3), pl.BlockSpec((1, HG_HEAD), lambda h, i: (0, h)),
                  pl.BlockSpec((1, HG_HEAD), lambda h, i: (0, 0)), pl.BlockSpec(memory_space=pl.ANY)],
        out_specs=[pl.BlockSpec((tb, HG_HEAD), lambda h, i: (i, heads + h)),
                   pl.BlockSpec((tb, HG_HEAD), lambda h, i: (i, h)),
                   pl.BlockSpec((1, ncb, HG_HEAD, HG_HEAD), lambda h, i: (h, i, 0, 0))],
        out_shape=[jax.ShapeDtypeStruct(ycat.shape, ycat.dtype), jax.ShapeDtypeStruct((t, mix_b), F32),
                   jax.ShapeDtypeStruct((heads, t // HG_CHUNK, HG_HEAD, HG_HEAD), F32)],
        scratch_shapes=scratch, input_output_aliases={6: 0},
        compiler_params=_cp("parallel", "arbitrary"), name=name)(z, z, z, z, lb, hg_norm, ycat)


def _hgrn_bwd(z, lb, hg_norm, o_raw, states, dycat, *, name, tb=512):
    t = z.shape[0]
    mix_b = lb.shape[1]
    heads = mix_b // HG_HEAD
    off = (z.shape[1] - 4 * mix_b) // HG_HEAD
    tb = min(tb, t)
    ncb = tb // HG_CHUNK
    nt = t // tb

    def body(q_ref, f_ref, i_ref, g_ref, lb_ref, hn_ref, o_ref, st_ref, dy_ref,
             dq_ref, dfl_ref, di_ref, dg_ref, dlb_ref, dhn_ref,
             dstate, qh_s, k_s, b_s, qt_s, kt_s, gl_s, do_s, dqh_s, dk_s, db_s):
        first = pl.program_id(1) == 0

        @pl.when(first)
        def _():
            dstate[...] = jnp.zeros_like(dstate)
            dlb_ref[...] = jnp.zeros_like(dlb_ref)

        @pl.when(first & (pl.program_id(0) == 0))
        def _():
            dhn_ref[...] = jnp.zeros_like(dhn_ref)

        sg, f = _hgrn_prepare(q_ref, f_ref, lb_ref, qh_s, k_s, b_s, qt_s, kt_s, gl_s)
        o = o_ref[...]
        r = lax.rsqrt(jnp.mean(o * o, axis=-1, keepdims=True) + EPS)
        oh = o * r
        gv = g_ref[...]
        sgg = _sigmoid(gv)
        dy = dy_ref[...].astype(F32)
        hn = hn_ref[...]
        dg_ref[...] = (dy * oh * hn * (sgg * (1.0 + gv * (1.0 - sgg)))).astype(dg_ref.dtype)
        don = dy * (gv * sgg)
        dhn_ref[...] += jnp.sum(don * oh, axis=0, keepdims=True)
        doh = don * hn
        do_s[...] = r * (doh - oh * jnp.mean(doh * oh, axis=-1, keepdims=True))
        row = lax.broadcasted_iota(jnp.int32, (HG_CHUNK, HG_HEAD), 0)

        def chunk(ci, carry):
            c = ncb - 1 - ci
            rows = pl.ds(pl.multiple_of(c * HG_CHUNK, HG_CHUNK), HG_CHUNK)
            st_prev = st_ref[0, c]
            dst = dstate[...]
            qh, kk, bq, vv = qh_s[rows, :], k_s[rows, :], b_s[rows, :], i_ref[rows, :]
            qt, kt, doo = qt_s[rows, :], kt_s[rows, :], do_s[rows, :]
            gl = gl_s[rows, :][0:1, :]
            es, a_rep, d_rep = _hgrn_intra(qh, kk, bq, doo, vv)
            dqh = jnp.exp(bq) * _dot(doo, st_prev, _NN)
            dk = jnp.exp(bq[HG_CHUNK - 1:HG_CHUNK, :] - bq) * _dot(vv, dst, _NN)
            dv = _dot(kt, dst, _NT)
            for s in range(HG_CHUNK):
                blk = slice(s * HG_CHUNK, (s + 1) * HG_CHUNK)
                wgt = jnp.where(row >= s, d_rep[blk, :] * es[s], 0.0)
                dqh = dqh + wgt * kk[s:s + 1, :]
                dk = dk + jnp.where(row == s, jnp.sum(wgt * qh, axis=0, keepdims=True), 0.0)
                dv_row = jnp.sum(jnp.where(row >= s, a_rep[blk, :] * doo, 0.0), axis=0, keepdims=True)
                dv = dv + jnp.where(row == s, dv_row, 0.0)
            st_next = st_prev * gl + _dot(vv, kt, _TN)
            db = qh * dqh - kk * dk
            db = db + jnp.where(row == HG_CHUNK - 1, jnp.sum(st_next * dst, axis=0, keepdims=True), 0.0)
            dstate[...] = dst * gl + _dot(doo, qt, _TN)
            dqh_s[rows, :] = dqh
            dk_s[rows, :] = dk
            db_s[rows, :] = db
            di_ref[rows, :] = dv.astype(di_ref.dtype)
            return carry

        lax.fori_loop(0, ncb, chunk, 0)
        tri = _chunk_tri(False)
        lb_v = lb_ref[...]
        qv = q_ref[...]
        sgq = _sigmoid(qv)
        dq_ref[...] = (dqh_s[...] * (HG_HEAD ** -0.5) * (sgq * (1.0 + qv * (1.0 - sgq)))).astype(dq_ref.dtype)
        dlb = jnp.zeros((1, HG_HEAD), F32)
        for rr in range(tb // LANES):
            rws = slice(rr * LANES, (rr + 1) * LANES)
            dlogf = _dot_f32(tri, db_s[rws, :], _NN)
            df = dlogf / f[rws, :] - dk_s[rws, :]
            sgr = sg[rws, :]
            dfl_ref[rws, :] = (df * (1.0 - lb_v) * sgr * (1.0 - sgr)).astype(dfl_ref.dtype)
            dlb = dlb + jnp.sum(df * (1.0 - sgr), axis=0, keepdims=True)
        dlb_ref[...] += dlb

    def zcol(kind):
        return pl.BlockSpec((tb, HG_HEAD), lambda h, i: (nt - 1 - i, off + kind * heads + h))

    hcol = pl.BlockSpec((tb, HG_HEAD), lambda h, i: (nt - 1 - i, h))
    scratch = [pltpu.VMEM((HG_HEAD, HG_HEAD), F32)] + [pltpu.VMEM((tb, HG_HEAD), F32)] * 10
    out = jax.ShapeDtypeStruct((t, mix_b), MXU_DTYPE)
    return pl.pallas_call(
        body, grid=(heads, nt),
        in_specs=[zcol(0), zcol(1), zcol(2), zcol(3), pl.BlockSpec((1, HG_HEAD), lambda h, i: (0, h)),
                  pl.BlockSpec((1, HG_HEAD), lambda h, i: (0, 0)), hcol,
                  pl.BlockSpec((1, ncb, HG_HEAD, HG_HEAD), lambda h, i: (h, nt - 1 - i, 0, 0)),
                  pl.BlockSpec((tb, HG_HEAD), lambda h, i: (nt - 1 - i, heads + h))],
        out_specs=[hcol, hcol, hcol, hcol, pl.BlockSpec((1, HG_HEAD), lambda h, i: (0, h)),
                   pl.BlockSpec((1, HG_HEAD), lambda h, i: (0, 0))],
        out_shape=[out, out, out, out, jax.ShapeDtypeStruct((1, mix_b), F32),
                   jax.ShapeDtypeStruct((1, HG_HEAD), F32)],
        scratch_shapes=scratch, compiler_params=_cp("arbitrary", "arbitrary"),
        name=name)(z, z, z, z, lb, hg_norm, o_raw, states, dycat)


def _lb_fwd(lb_table, layer, *, name):
    rows, width = lb_table.shape

    def body(t_ref, o_ref):
        tv = t_ref[...]
        e = jnp.exp(tv - jnp.max(tv, axis=0, keepdims=True))
        sm = e / jnp.sum(e, axis=0, keepdims=True)
        o_ref[...] = jnp.sum(sm[1:layer + 2, :], axis=0, keepdims=True)

    return pl.pallas_call(body, out_shape=jax.ShapeDtypeStruct((1, width), F32), name=name)(lb_table)


def _lb_bwd(lb_table, dlb, layer, *, name):
    rows, width = lb_table.shape

    def body(t_ref, d_ref, o_ref):
        tv = t_ref[...]
        e = jnp.exp(tv - jnp.max(tv, axis=0, keepdims=True))
        sm = e / jnp.sum(e, axis=0, keepdims=True)
        ridx = lax.broadcasted_iota(jnp.int32, sm.shape, 0)
        dsm = jnp.where((ridx >= 1) & (ridx <= layer + 1), d_ref[...], 0.0)
        o_ref[...] = sm * (dsm - jnp.sum(sm * dsm, axis=0, keepdims=True))

    return pl.pallas_call(body, out_shape=jax.ShapeDtypeStruct((rows, width), F32), name=name)(lb_table, dlb)


FOX_BLOCK = 512


def _fox_prep(zf, b_f, heads, *, name, blk=256):
    t = zf.shape[0]

    def body(z_ref, b_ref, fc_ref, fr_ref):
        r = lax.broadcasted_iota(jnp.int32, (blk, blk), 0)
        c = lax.broadcasted_iota(jnp.int32, (blk, blk), 1)
        tri = jnp.where(c <= r, 1.0, 0.0).astype(F32)
        carry = jnp.zeros((1, LANES), F32)
        for j in range(t // blk):
            rows = slice(j * blk, (j + 1) * blk)
            ls = jax.nn.log_sigmoid(z_ref[rows, :] + b_ref[...])
            fb = _dot_f32(tri, ls, _NN) + carry
            carry = fb[blk - 1:blk, :]
            fc_ref[rows, :] = fb
            fbt = fb.T
            for hh in range(heads):
                fr_ref[hh, :, rows] = fbt[hh:hh + 1, :]

    return pl.pallas_call(
        body, out_shape=[jax.ShapeDtypeStruct((t, LANES), F32), jax.ShapeDtypeStruct((heads, 1, t), F32)],
        compiler_params=pltpu.CompilerParams(vmem_limit_bytes=VMEM_LIMIT_BYTES), name=name)(zf, b_f)


def _fox_scores(k_blk, q_blk, fcol_blk, frow_blk, head, kj, qi, blk):
    lane = lax.broadcasted_iota(jnp.int32, fcol_blk.shape, 1)
    fcol = jnp.sum(jnp.where(lane == head, fcol_blk, 0.0), axis=1, keepdims=True)
    s = _dot(k_blk, q_blk, _NT) * (FOX_HEAD ** -0.5) + (frow_blk - fcol)
    key = kj * blk + lax.broadcasted_iota(jnp.int32, s.shape, 0)
    qry = qi * blk + lax.broadcasted_iota(jnp.int32, s.shape, 1)
    return jnp.where(key <= qry, s, -jnp.inf)


def _fox_fwd(zqkv, fcol, frow, *, name):
    t = zqkv.shape[0]
    d = zqkv.shape[1] // 3
    heads = d // FOX_HEAD
    blk = min(FOX_BLOCK, t)
    nq = t // blk

    def body(q_ref, k_ref, v_ref, fc_ref, fr_ref, o_ref, lse_ref):
        head = pl.program_id(0)

        def q_block(qi, carry):
            qrows = pl.ds(pl.multiple_of(qi * blk, blk), blk)
            q_blk = q_ref[qrows, :]
            frow_blk = fr_ref[0, :, qrows]

            def k_block(kj, st):
                m, l, acc = st
                krows = pl.ds(pl.multiple_of(kj * blk, blk), blk)
                s = _fox_scores(k_ref[krows, :], q_blk, fc_ref[krows, :], frow_blk, head, kj, qi, blk)
                m_new = jnp.maximum(m, jnp.max(s, axis=0, keepdims=True))
                alpha = jnp.exp(m - m_new)
                p = jnp.exp(s - m_new)
                l = alpha * l + jnp.sum(p, axis=0, keepdims=True)
                acc = acc * alpha + _dot(v_ref[krows, :], p, _TN)
                return m_new, l, acc

            init = (jnp.full((1, blk), -jnp.inf, F32), jnp.zeros((1, blk), F32),
                    jnp.zeros((FOX_HEAD, blk), F32))
            m, l, acc = lax.fori_loop(0, qi + 1, k_block, init)
            o_ref[qrows, :] = (acc / l).T.astype(o_ref.dtype)
            lse_ref[0, :, qrows] = m + jnp.log(l)
            return carry

        lax.fori_loop(0, nq, q_block, 0)

    def col(kind):
        return pl.BlockSpec((t, FOX_HEAD), lambda h: (0, kind * heads + h))

    rowvec = pl.BlockSpec((1, 1, t), lambda h: (h, 0, 0))
    return pl.pallas_call(
        body, grid=(heads,),
        in_specs=[col(0), col(1), col(2), pl.BlockSpec((t, LANES), lambda h: (0, 0)), rowvec],
        out_specs=[pl.BlockSpec((t, FOX_HEAD), lambda h: (0, h)), rowvec],
        out_shape=[jax.ShapeDtypeStruct((t, d), MXU_DTYPE), jax.ShapeDtypeStruct((heads, 1, t), F32)],
        compiler_params=_cp("parallel"), name=name)(zqkv, zqkv, zqkv, fcol, frow)


def _fox_bwd(zqkv, fcol, frow, lse, o, do, *, name):
    t = zqkv.shape[0]
    d = zqkv.shape[1] // 3
    heads = d // FOX_HEAD
    blk = min(FOX_BLOCK, t)
    nq = t // blk
    scale = FOX_HEAD ** -0.5

    def body(q_ref, k_ref, v_ref, fc_ref, fr_ref, lse_ref, o_ref, do_ref,
             dq_ref, dk_ref, dv_ref, rq_ref, rk_ref, dq_s, drow_s):
        head = pl.program_id(0)
        ones = jnp.ones((8, blk), MXU_DTYPE)
        dq_s[...] = jnp.zeros_like(dq_s)
        rq_ref[...] = jnp.zeros_like(rq_ref)
        ones_f = jnp.ones((8, FOX_HEAD), F32)
        for j in range(nq):
            rows = slice(j * blk, (j + 1) * blk)
            prod = do_ref[rows, :].astype(F32) * o_ref[rows, :].astype(F32)
            drow_s[:, rows] = _dot_f32(ones_f, prod, _NT)

        def k_block(kj, carry):
            krows = pl.ds(pl.multiple_of(kj * blk, blk), blk)
            k_blk, v_blk, fc_blk = k_ref[krows, :], v_ref[krows, :], fc_ref[krows, :]

            def q_block(qi, st):
                dk, dv, rk = st
                qrows = pl.ds(pl.multiple_of(qi * blk, blk), blk)
                q_blk, do_blk = q_ref[qrows, :], do_ref[qrows, :]
                s = _fox_scores(k_blk, q_blk, fc_blk, fr_ref[0, :, qrows], head, kj, qi, blk)
                p = jnp.exp(s - lse_ref[0, :, qrows])
                dv = dv + _dot(p, do_blk, _NN)
                dp = _dot(v_blk, do_blk, _NT)
                ds = (p * (dp - drow_s[0:1, qrows])).astype(MXU_DTYPE)
                dk = dk + _dot(ds, q_blk, _NN)
                dq_s[qrows, :] += _dot(ds, k_blk, _TN)
                rq_ref[0, :, qrows] += _dot(ones, ds, _NN)[0:1, :]
                rk = rk + _dot(ones, ds, _NT)
                return dk, dv, rk

            init = (jnp.zeros((blk, FOX_HEAD), F32), jnp.zeros((blk, FOX_HEAD), F32), jnp.zeros((8, blk), F32))
            dk, dv, rk = lax.fori_loop(kj, nq, q_block, init)
            dk_ref[krows, :] = (dk * scale).astype(dk_ref.dtype)
            dv_ref[krows, :] = dv.astype(dv_ref.dtype)
            rk_ref[0, :, krows] = rk[0:1, :]
            return carry

        lax.fori_loop(0, nq, k_block, 0)
        dq_ref[...] = (dq_s[...] * scale).astype(dq_ref.dtype)

    def col(kind):
        return pl.BlockSpec((t, FOX_HEAD), lambda h: (0, kind * heads + h))

    hcol = pl.BlockSpec((t, FOX_HEAD), lambda h: (0, h))
    rowvec = pl.BlockSpec((1, 1, t), lambda h: (h, 0, 0))
    out = jax.ShapeDtypeStruct((t, d), MXU_DTYPE)
    vec = jax.ShapeDtypeStruct((heads, 1, t), F32)
    return pl.pallas_call(
        body, grid=(heads,),
        in_specs=[col(0), col(1), col(2), pl.BlockSpec((t, LANES), lambda h: (0, 0)), rowvec, rowvec, hcol, hcol],
        out_specs=[hcol, hcol, hcol, rowvec, rowvec],
        out_shape=[out, out, out, vec, vec],
        scratch_shapes=[pltpu.VMEM((t, FOX_HEAD), F32), pltpu.VMEM((8, t), F32)],
        compiler_params=_cp("parallel"), name=name)(zqkv, zqkv, zqkv, fcol, frow, lse, o, do)


def _fox_gate_bwd(rq, rk, zf, b_f, *, name, blk=256):
    heads, t = rq.shape

    def body(rq_ref, rk_ref, z_ref, b_ref, dfl_ref, db_ref):
        r = lax.broadcasted_iota(jnp.int32, (blk, blk), 0)
        c = lax.broadcasted_iota(jnp.int32, (blk, blk), 1)
        tri = jnp.where(r >= c, 1.0, 0.0).astype(F32)
        carry = jnp.zeros((heads, 1), F32)
        db = jnp.zeros((1, LANES), F32)
        pad = jnp.zeros((LANES - heads, blk), F32)
        for j in reversed(range(t // blk)):
            cols = slice(j * blk, (j + 1) * blk)
            df = rq_ref[:, cols] - rk_ref[:, cols]
            dls = _dot_f32(df, tri, _NN) + carry
            carry = dls[:, 0:1]
            dls_t = jnp.concatenate([dls, pad], axis=0).T
            dfl = dls_t * _sigmoid(-(z_ref[cols, :] + b_ref[...]))
            dfl_ref[cols, :] = dfl.astype(dfl_ref.dtype)
            db = db + jnp.sum(dfl, axis=0, keepdims=True)
        db_ref[...] = db

    return pl.pallas_call(
        body, out_shape=[jax.ShapeDtypeStruct((t, LANES), MXU_DTYPE), jax.ShapeDtypeStruct((1, LANES), F32)],
        compiler_params=pltpu.CompilerParams(vmem_limit_bytes=VMEM_LIMIT_BYTES), name=name)(rq, rk, zf, b_f)


def _adamw(w, m, v, parts, *, name, layer=None, prev=None, tr=128):
    lcnt, r, c = w.shape
    p = parts.shape[0]
    li = 0 if layer is None else layer
    tr = _tile(r, tr, 16)
    has_prev = prev is not None

    def body(*refs):
        w_ref, m_ref, v_ref, p_ref = refs[:4]
        g_ref, d_ref, nm_ref, nv_ref = refs[-4:]
        g = p_ref[0].astype(F32)
        for j in range(1, p):
            g = g + p_ref[j].astype(F32)
        wv = w_ref[0]
        mn = ADAM_B1 * m_ref[0] + (1.0 - ADAM_B1) * g
        vn = ADAM_B2 * v_ref[0] + (1.0 - ADAM_B2) * (g * g)
        m_hat = mn / (1.0 - ADAM_B1 ** ADAM_STEP)
        v_hat = vn / (1.0 - ADAM_B2 ** ADAM_STEP)
        g_ref[0] = g
        d_ref[0] = -ADAM_LR * (m_hat / (jnp.sqrt(v_hat) + ADAM_EPS) + ADAM_WD * wv)
        nm_ref[0] = mn
        nv_ref[0] = vn

    slab = pl.BlockSpec((1, tr, c), lambda i: (li, i, 0))
    in_specs = [slab, slab, slab, pl.BlockSpec((p, tr, c), lambda i: (0, i, 0))]
    operands = [w, m, v, parts]
    aliases = {}
    if has_prev:
        in_specs += [pl.BlockSpec(memory_space=pl.ANY)] * 4
        operands += list(prev)
        aliases = {4: 0, 5: 1, 6: 2, 7: 3}
    shp = jax.ShapeDtypeStruct((lcnt, r, c), F32)
    return pl.pallas_call(
        body, grid=(r // tr,), in_specs=in_specs, out_specs=[slab] * 4, out_shape=[shp] * 4,
        input_output_aliases=aliases, compiler_params=_cp("parallel"), name=name)(*operands)


def _my_place():
    return lax.axis_index("x"), lax.axis_index("y"), lax.axis_index("c")


def _slot(p):
    return 4 * p[0] + 2 * p[1] + p[2]


def _peer(me, mask):
    x, y, c = me
    return (1 - x if mask & 4 else x, 1 - y if mask & 2 else y, 1 - c if mask & 1 else c)


def _all_gather(shards, *, name):
    n = len(shards)

    def body(*refs):
        ins, outs = refs[:n], refs[n:2 * n]
        send_sems, recv_sems, local_sems = refs[2 * n:]
        me = _my_place()
        x, y, c = me
        sibling = (x, y, 1 - c)
        chips = [(1 - x, y), (x, 1 - y), (1 - x, 1 - y)]

        def copy(a, k, block, to, src=None):
            dst = outs[a].at[_slot(block)]
            return pltpu.make_async_remote_copy(
                src_ref=dst if src is None else src, dst_ref=dst, send_sem=send_sems.at[a, k],
                recv_sem=recv_sems.at[a, k], device_id=to, device_id_type=MESH)

        local = [pltpu.make_async_copy(ins[a], outs[a].at[_slot(me)], local_sems.at[a]) for a in range(n)]
        first = []
        for a in range(n):
            local[a].start()
            first.append(copy(a, 0, me, sibling, src=ins[a]))
            first += [copy(a, 1 + j, me, (*chip, c), src=ins[a]) for j, chip in enumerate(chips)]
        for cp in first:
            cp.start()
        passed = []
        for j, chip in enumerate(chips):
            for a in range(n):
                copy(a, 1 + j, (*chip, c), me).wait_recv()
                fwd = copy(a, 4 + j, (*chip, c), sibling)
                fwd.start()
                passed.append(fwd)
        for a in range(n):
            copy(a, 0, sibling, me).wait_recv()
            for j, chip in enumerate(chips):
                copy(a, 4 + j, (*chip, 1 - c), me).wait_recv()
        for cp in first + passed:
            cp.wait_send()
        for cp in local:
            cp.wait()

    hbm = pl.BlockSpec(memory_space=pl.ANY)
    return pl.pallas_call(
        body, in_specs=[hbm] * n, out_specs=[hbm] * n,
        out_shape=[jax.ShapeDtypeStruct((N_DEV,) + s.shape, s.dtype) for s in shards],
        scratch_shapes=[pltpu.SemaphoreType.DMA((n, 7)), pltpu.SemaphoreType.DMA((n, 7)),
                        pltpu.SemaphoreType.DMA((n,))],
        name=name)(*shards)


def _exchange(parts, *, name):
    n = len(parts)

    def body(*refs):
        ins, outs = refs[:n], refs[n:2 * n]
        send_sems, recv_sems, local_sems = refs[2 * n:]
        me = _my_place()
        mine = _slot(me)
        local = [pltpu.make_async_copy(ins[a].at[mine], outs[a].at[mine], local_sems.at[a]) for a in range(n)]
        sends = []
        for a in range(n):
            local[a].start()
            for mask in range(1, N_DEV):
                peer = _peer(me, mask)
                sends.append(pltpu.make_async_remote_copy(
                    src_ref=ins[a].at[_slot(peer)], dst_ref=outs[a].at[mine], send_sem=send_sems.at[a, mask - 1],
                    recv_sem=recv_sems.at[a, mask - 1], device_id=peer, device_id_type=MESH))
        for cp in sends:
            cp.start()
        for a in range(n):
            for mask in range(1, N_DEV):
                peer = _peer(me, mask)
                pltpu.make_async_remote_copy(
                    src_ref=ins[a].at[mine], dst_ref=outs[a].at[_slot(peer)], send_sem=send_sems.at[a, mask - 1],
                    recv_sem=recv_sems.at[a, mask - 1], device_id=peer, device_id_type=MESH).wait_recv()
        for cp in sends:
            cp.wait_send()
        for cp in local:
            cp.wait()

    hbm = pl.BlockSpec(memory_space=pl.ANY)
    return pl.pallas_call(
        body, in_specs=[hbm] * n, out_specs=[hbm] * n,
        out_shape=[jax.ShapeDtypeStruct(p.shape, p.dtype) for p in parts],
        scratch_shapes=[pltpu.SemaphoreType.DMA((n, 7)), pltpu.SemaphoreType.DMA((n, 7)),
                        pltpu.SemaphoreType.DMA((n,))],
        name=name)(*parts)


def _all_reduce_rows(v, *, name):
    r, c = v.shape

    def body(v_ref, o_ref, buf, send_sems, recv_sems):
        me = _my_place()
        mine = _slot(me)
        sends = []
        for mask in range(1, N_DEV):
            peer = _peer(me, mask)
            sends.append(pltpu.make_async_remote_copy(
                src_ref=v_ref, dst_ref=buf.at[mine], send_sem=send_sems.at[mask - 1],
                recv_sem=recv_sems.at[mask - 1], device_id=peer, device_id_type=MESH))
        for cp in sends:
            cp.start()
        buf[mine] = v_ref[...]
        for mask in range(1, N_DEV):
            peer = _peer(me, mask)
            pltpu.make_async_remote_copy(
                src_ref=v_ref, dst_ref=buf.at[_slot(peer)], send_sem=send_sems.at[mask - 1],
                recv_sem=recv_sems.at[mask - 1], device_id=peer, device_id_type=MESH).wait_recv()
        for cp in sends:
            cp.wait_send()
        total = buf[0]
        for j in range(1, N_DEV):
            total = total + buf[j]
        o_ref[...] = total

    vm = pl.BlockSpec(memory_space=pltpu.VMEM)
    return pl.pallas_call(
        body, in_specs=[vm], out_specs=vm, out_shape=jax.ShapeDtypeStruct((r, c), F32),
        scratch_shapes=[pltpu.VMEM((N_DEV, r, c), F32), pltpu.SemaphoreType.DMA((7,)),
                        pltpu.SemaphoreType.DMA((7,))],
        name=name)(v)


def _xa_fwd(x, mem, g_x, g_m, wq, wkv, wo, tag):
    hx = _rms_fwd(x, g_x, name=f"xa{tag}_norm")
    memn = _rms_fwd(mem, g_m, name=f"xa{tag}_mem_norm")
    q = _mm2(hx, wq, "nn", MXU_DTYPE, name=f"xa{tag}_q")
    kv = _mm2(memn, wkv, "nn", MXU_DTYPE, name=f"xa{tag}_kv", b_split=True)
    o = _xattn_fwd(q, kv, name=f"xa{tag}_attn")
    return _mm2(o, wo, "nn", F32, name=f"xa{tag}_out", add=x), (hx, memn, q, kv, o)


def _xa_bwd(x, mem, g_x, g_m, wq, wkv, wo, saved, dxo, dxo_lo, tag):
    hx, memn, q, kv, o = saved
    do = _mm2(dxo_lo, wo, "nt", MXU_DTYPE, name=f"xa{tag}_do")
    dwo = _mm2(o, dxo_lo, "tn", MXU_DTYPE, name=f"xa{tag}_dwo")
    dq, dkv = _xattn_bwd(q, kv, do, name=f"xa{tag}_attn_bwd")
    dwq = _mm2(hx, dq, "tn", MXU_DTYPE, name=f"xa{tag}_dwq")
    dhx = _mm2(dq, wq, "nt", F32, name=f"xa{tag}_dh")
    dx, dx_lo, dgx = _rms_bwd(x, g_x, dhx, dxo, name=f"xa{tag}_norm_bwd")
    dwkv = _mm2(memn, dkv, "tn", MXU_DTYPE, name=f"xa{tag}_dwkv", o_split=True, tn=wkv.shape[2])
    dmemn = _mm2(dkv, wkv, "nt", F32, name=f"xa{tag}_dmem", b_split=True)
    _, _, dgm = _rms_bwd(mem, g_m, dmemn, None, name=f"xa{tag}_mem_norm_bwd")
    return dx, dx_lo, dgx, dgm, (dwq, dwkv, dwo)


def _ffn_fwd(x, g, wg, wu, wd, tag):
    h = _rms_fwd(x, g, name=f"ffn{tag}_norm")
    gate, up, act = _ffn_up(h, wg, wu, name=f"ffn{tag}_up")
    x_new = _mm(act, wd, "nn", F32, name=f"ffn{tag}_down", reduce_b=True, add=x[None])[0]
    return x_new, (h, gate, up, act)


def _ffn_bwd(x, g, wg, wu, wd, saved, dxo, dxo_lo, tag):
    h, gate, up, act = saved
    dgate, dup = _ffn_dact(dxo_lo, wd, gate, up, name=f"ffn{tag}_dact")
    dwd = _mm(act, dxo_lo[None], "tn", MXU_DTYPE, name=f"ffn{tag}_dwd")
    dwg = _mm(h[None], dgate, "tn", MXU_DTYPE, name=f"ffn{tag}_dwg")
    dwu = _mm(h[None], dup, "tn", MXU_DTYPE, name=f"ffn{tag}_dwu")
    dh = _mm(dgate, wg, "nt", F32, name=f"ffn{tag}_dh_gate", reduce_b=True)
    dh = _mm(dup, wu, "nt", F32, name=f"ffn{tag}_dh_up", reduce_b=True, add=dh)[0]
    dx, dx_lo, dg = _rms_bwd(x, g, dh, dxo, name=f"ffn{tag}_norm_bwd")
    return dx, dx_lo, dg, (dwg, dwu, dwd)


def _even_fwd(x, g, lb, w_in, w_pool, pool_scale, hg_norm, w_out):
    h = _rms_fwd(x, g, name="ev_norm")
    z = _mm2(h, w_in, "nn", F32, name="ev_in", b_split=True)
    ycat = _pool_fwd(z, w_pool, pool_scale, name="ev_pool")
    ycat, o_raw, states = _hgrn_fwd(z, lb, hg_norm, ycat, name="ev_hgrn")
    return _mm2(ycat, w_out, "nn", F32, name="ev_out", add=x), (h, z, ycat, o_raw, states)


def _even_bwd(x, g, lb, w_in, w_pool, pool_scale, hg_norm, w_out, saved, dxo, dxo_lo):
    h, z, ycat, o_raw, states = saved
    dycat = _mm2(dxo_lo, w_out, "nt", MXU_DTYPE, name="ev_dy")
    dw_out = _mm2(ycat, dxo_lo, "tn", MXU_DTYPE, name="ev_dw_out")
    du, dw_pool, dscale = _pool_bwd(z, w_pool, pool_scale, dycat, name="ev_pool_bwd")
    dq, dfl, di, dg, dlb, dhn = _hgrn_bwd(z, lb, hg_norm, o_raw, states, dycat, name="ev_hgrn_bwd")
    dz = jnp.concatenate([du, dq, dfl, di, dg], axis=1)
    dw_in = _mm2(h, dz, "tn", MXU_DTYPE, name="ev_dw_in", o_split=True, tn=w_in.shape[2])
    dh = _mm2(dz, w_in, "nt", F32, name="ev_dh", b_split=True)
    dx, dx_lo, dgn = _rms_bwd(x, g, dh, dxo, name="ev_norm_bwd")
    return dx, dx_lo, dict(ev_norm=dgn, ev_pool_scale=dscale, ev_hg_norm=dhn, lb=dlb), (dw_in, dw_pool, dw_out)


def _odd_fwd(x, g, w_qkv, w_f, b_f, w_out):
    heads = w_qkv.shape[1] // (3 * FOX_HEAD)
    h = _rms_fwd(x, g, name="od_norm")
    zqkv = _mm2(h, w_qkv, "nn", MXU_DTYPE, name="od_qkv")
    zf = _mm2(h, w_f, "nn", F32, name="od_gate")
    fcol, frow = _fox_prep(zf, b_f, heads, name="od_fox_prep")
    o, lse = _fox_fwd(zqkv, fcol, frow, name="od_fox")
    return _mm2(o, w_out, "nn", F32, name="od_out", add=x), (h, zqkv, zf, fcol, frow, o, lse)


def _odd_bwd(x, g, w_qkv, w_f, b_f, w_out, saved, dxo, dxo_lo):
    h, zqkv, zf, fcol, frow, o, lse = saved
    do = _mm2(dxo_lo, w_out, "nt", MXU_DTYPE, name="od_do")
    dw_out = _mm2(o, dxo_lo, "tn", MXU_DTYPE, name="od_dw_out")
    dq, dk, dv, rq, rk = _fox_bwd(zqkv, fcol, frow, lse, o, do, name="od_fox_bwd")
    dfl, db_f = _fox_gate_bwd(rq[:, 0, :], rk[:, 0, :], zf, b_f, name="od_fox_gate_bwd")
    dz = jnp.concatenate([dq, dk, dv], axis=1)
    dw_qkv = _mm2(h, dz, "tn", MXU_DTYPE, name="od_dw_qkv")
    dw_f = _mm2(h, dfl, "tn", MXU_DTYPE, name="od_dw_gate")
    dh = _mm2(dz, w_qkv, "nt", F32, name="od_dh_qkv")
    dh = _mm2(dfl, w_f, "nt", F32, name="od_dh_gate", add=dh)
    dx, dx_lo, dgn = _rms_bwd(x, g, dh, dxo, name="od_norm_bwd")
    return dx, dx_lo, dict(od_norm=dgn, od_b_f=db_f), (dw_qkv, dw_f, dw_out)


def _local_step(x, mem, target, sp, wt):
    b_f = jnp.pad(sp["od_b_f"], ((0, 0), (0, LANES - sp["od_b_f"].shape[1])))
    lb = _lb_fwd(sp["lb_table"], 0, name="lb_fwd")
    fin = sp["final_norm"].reshape(1, -1)
    x1, s_ev = _even_fwd(x, sp["ev_norm"], lb, wt["ev_w_in"], wt["ev_w_pool"], sp["ev_pool_scale"],
                         sp["ev_hg_norm"], wt["ev_w_out"])
    x2, s_xa0 = _xa_fwd(x1, mem, sp["xa_norm"][0:1], sp["xa_mem_norm"][0:1], wt["xa_wq"][0], wt["xa_wkv"][0],
                        wt["xa_wo"][0], 0)
    x3, s_ff0 = _ffn_fwd(x2, sp["ffn_norm"][0:1], wt["ffn_w_gate"][0], wt["ffn_w_up"][0], wt["ffn_w_down"][0], 0)
    x4, s_od = _odd_fwd(x3, sp["od_norm"], wt["od_w_qkv"], wt["od_w_f"], b_f, wt["od_w_out"])
    x5, s_xa1 = _xa_fwd(x4, mem, sp["xa_norm"][1:2], sp["xa_mem_norm"][1:2], wt["xa_wq"][1], wt["xa_wkv"][1],
                        wt["xa_wo"][1], 1)
    x6, s_ff1 = _ffn_fwd(x5, sp["ffn_norm"][1:2], wt["ffn_w_gate"][1], wt["ffn_w_up"][1], wt["ffn_w_down"][1], 1)
    loss, dx, dx_lo, d_fin = _loss_head(x6, fin, target, name="loss_head")
    dx, dx_lo, d_ffn1, dw_ff1 = _ffn_bwd(x5, sp["ffn_norm"][1:2], wt["ffn_w_gate"][1], wt["ffn_w_up"][1],
                                         wt["ffn_w_down"][1], s_ff1, dx, dx_lo, 1)
    dx, dx_lo, d_xa1, d_xm1, dw_xa1 = _xa_bwd(x4, mem, sp["xa_norm"][1:2], sp["xa_mem_norm"][1:2], wt["xa_wq"][1],
                                              wt["xa_wkv"][1], wt["xa_wo"][1], s_xa1, dx, dx_lo, 1)
    dx, dx_lo, d_od, dw_od = _odd_bwd(x3, sp["od_norm"], wt["od_w_qkv"], wt["od_w_f"], b_f, wt["od_w_out"], s_od,
                                      dx, dx_lo)
    dx, dx_lo, d_ffn0, dw_ff0 = _ffn_bwd(x2, sp["ffn_norm"][0:1], wt["ffn_w_gate"][0], wt["ffn_w_up"][0],
                                         wt["ffn_w_down"][0], s_ff0, dx, dx_lo, 0)
    dx, dx_lo, d_xa0, d_xm0, dw_xa0 = _xa_bwd(x1, mem, sp["xa_norm"][0:1], sp["xa_mem_norm"][0:1], wt["xa_wq"][0],
                                              wt["xa_wkv"][0], wt["xa_wo"][0], s_xa0, dx, dx_lo, 0)
    dx, _, d_ev, dw_ev = _even_bwd(x, sp["ev_norm"], lb, wt["ev_w_in"], wt["ev_w_pool"], sp["ev_pool_scale"],
                                   sp["ev_hg_norm"], wt["ev_w_out"], s_ev, dx, dx_lo)
    small = dict(
        lb_table=_lb_bwd(sp["lb_table"], d_ev["lb"], 0, name="lb_bwd"),
        ev_norm=d_ev["ev_norm"], ev_pool_scale=d_ev["ev_pool_scale"], ev_hg_norm=d_ev["ev_hg_norm"],
        od_norm=d_od["od_norm"], od_b_f=d_od["od_b_f"][:, :sp["od_b_f"].shape[1]],
        xa_norm=jnp.concatenate([d_xa0, d_xa1], axis=0), xa_mem_norm=jnp.concatenate([d_xm0, d_xm1], axis=0),
        ffn_norm=jnp.concatenate([d_ffn0, d_ffn1], axis=0), final_norm=d_fin.reshape(-1))
    big = dict(ev=dw_ev, od=dw_od, xa=(dw_xa0, dw_xa1), ffn=(dw_ff0, dw_ff1))
    return loss, dx, small, big


_SMALL = ("lb_table", "ev_norm", "ev_pool_scale", "ev_hg_norm", "od_norm", "od_b_f", "xa_norm", "xa_mem_norm",
          "ffn_norm", "final_norm")
_WEIGHTS = ("lb_table", "ev_norm", "ev_w_in", "ev_w_pool", "ev_pool_scale", "ev_hg_norm", "ev_w_out", "od_norm",
            "od_w_in", "od_b_f", "od_w_out", "xa_norm", "xa_mem_norm", "xa_wq", "xa_wkv", "xa_wo", "ffn_norm",
            "ffn_w_gate", "ffn_w_up", "ffn_w_down", "final_norm")


def _lo(a):
    return a.astype(MXU_DTYPE)


def _rows(v):
    flat = v.reshape(-1)
    return jnp.pad(flat, (0, (-flat.shape[0]) % LANES)).reshape(-1, LANES)


def kernel(x, mem, lb_table, ev_norm, ev_w_in, ev_w_pool, ev_pool_scale, ev_hg_norm, ev_w_out, od_norm, od_w_in, od_b_f, od_w_out, xa_norm, xa_mem_norm, xa_wq, xa_wkv, xa_wo, ffn_norm, ffn_w_gate, ffn_w_up, ffn_w_down, final_norm, loss_target, m_lb_table, m_ev_norm, m_ev_w_in, m_ev_w_pool, m_ev_pool_scale, m_ev_hg_norm, m_ev_w_out, m_od_norm, m_od_w_in, m_od_b_f, m_od_w_out, m_xa_norm, m_xa_mem_norm, m_xa_wq, m_xa_wkv, m_xa_wo, m_ffn_norm, m_ffn_w_gate, m_ffn_w_up, m_ffn_w_down, m_final_norm, v_lb_table, v_ev_norm, v_ev_w_in, v_ev_w_pool, v_ev_pool_scale, v_ev_hg_norm, v_ev_w_out, v_od_norm, v_od_w_in, v_od_b_f, v_od_w_out, v_xa_norm, v_xa_mem_norm, v_xa_wq, v_xa_wkv, v_xa_wo, v_ffn_norm, v_ffn_w_gate, v_ffn_w_up, v_ffn_w_down, v_final_norm):
    arg = dict(locals())
    d = x.shape[-1]
    layers = xa_wq.shape[0]
    me = _slot(_my_place())

    ev_in, ev_pool, ev_out = _all_gather([_lo(ev_w_in[0]), _lo(ev_w_pool[0]), _lo(ev_w_out[0])], name="gather_ev")
    xa_g = [_all_gather([_lo(xa_wq[l]), _lo(xa_wkv[l]), _lo(xa_wo[l])], name=f"gather_xa{l}") for l in range(layers)]
    ff_g = [_all_gather([_lo(ffn_w_gate[l]), _lo(ffn_w_up[l]), _lo(ffn_w_down[l])], name=f"gather_ffn{l}")
            for l in range(layers)]
    od_in, od_out, od_nrm = _all_gather([_lo(od_w_in[0]), _lo(od_w_out[0]), od_norm], name="gather_od")
    od_in_full = jnp.transpose(od_in, (1, 0, 2)).reshape(d, -1)
    n_gate = od_b_f.shape[1]
    wt = dict(
        ev_w_in=ev_in,
        ev_w_pool=jnp.transpose(ev_pool, (1, 0, 2, 3)).reshape(ev_pool.shape[1], -1, ev_pool.shape[3]),
        ev_w_out=ev_out.reshape(d, d),
        od_w_qkv=od_in_full[:, :od_in_full.shape[1] - n_gate],
        od_w_f=jnp.pad(od_in_full[:, od_in_full.shape[1] - n_gate:], ((0, 0), (0, LANES - n_gate))),
        od_w_out=od_out.reshape(d, d),
        xa_wq=[g[0].reshape(d, d) for g in xa_g], xa_wkv=[g[1] for g in xa_g],
        xa_wo=[g[2].reshape(d, d) for g in xa_g],
        ffn_w_gate=[g[0] for g in ff_g], ffn_w_up=[g[1] for g in ff_g], ffn_w_down=[g[2] for g in ff_g])
    sp = {k: arg[k] for k in _SMALL}
    sp["od_norm"] = od_nrm.reshape(1, d)

    loss, dx, small, big = _local_step(x[0], mem[0], loss_target[0], sp, wt)

    def row_parts(g):
        return g.reshape(N_DEV, -1, g.shape[-1])

    dw_in, dw_pool, dw_out = big["ev"]
    gc = dw_pool.shape[1] // N_DEV
    dw_pool = _lo(jnp.transpose(dw_pool.reshape(dw_pool.shape[0], N_DEV, gc, -1), (1, 0, 2, 3)))
    got = {}
    got["ev_w_in"], got["ev_w_pool"], got["ev_w_out"] = _exchange([dw_in, dw_pool, row_parts(dw_out)],
                                                                  name="scatter_ev")
    dw_qkv, dw_f, dw_out = big["od"]
    dw_od_in = jnp.concatenate([dw_qkv, dw_f[:, :n_gate]], axis=1)
    dw_od_in = jnp.transpose(dw_od_in.reshape(d, N_DEV, -1), (1, 0, 2))
    got["od_w_in"], got["od_w_out"] = _exchange([dw_od_in, row_parts(dw_out)], name="scatter_od")
    for l in range(layers):
        dwq, dwkv, dwo = big["xa"][l]
        got["xa_wq", l], got["xa_wkv", l], got["xa_wo", l] = _exchange([row_parts(dwq), dwkv, row_parts(dwo)],
                                                                      name=f"scatter_xa{l}")
        got["ffn_w_gate", l], got["ffn_w_up", l], got["ffn_w_down", l] = _exchange(list(big["ffn"][l]),
                                                                                 name=f"scatter_ffn{l}")

    pieces = [_rows(small[k]) for k in _SMALL]
    packed = jnp.concatenate(pieces, axis=0)
    packed = jnp.pad(packed, ((0, (-packed.shape[0]) % 8), (0, 0)))
    total = _all_reduce_rows(packed, name="all_reduce_small")
    small_g, at = {}, 0
    for k, pc in zip(_SMALL, pieces):
        n = small[k].size
        small_g[k] = total[at:at + pc.shape[0]].reshape(-1)[:n].reshape(small[k].shape)
        at += pc.shape[0]
    small_g["od_norm"] = lax.dynamic_slice_in_dim(small_g["od_norm"], me * od_norm.shape[1], od_norm.shape[1], axis=1)

    res = {}
    for k in _WEIGHTS:
        w, m, v = arg[k], arg["m_" + k], arg["v_" + k]
        if k in _SMALL:
            shp = (1, 1, w.shape[0]) if w.ndim == 1 else (1,) + w.shape
            out = _adamw(w.reshape(shp), m.reshape(shp), v.reshape(shp), small_g[k].reshape(shp), name=f"adamw_{k}")
        elif w.shape[0] == 1:
            shp = (1, -1, w.shape[-1])
            parts = got[k].reshape(N_DEV, -1, w.shape[-1])
            out = _adamw(w.reshape(shp), m.reshape(shp), v.reshape(shp), parts, name=f"adamw_{k}")
        else:
            out = None
            for l in reversed(range(layers)):
                out = _adamw(w, m, v, got[k, l], name=f"adamw_{k}{l}", layer=l, prev=out)
        res[k] = [o.reshape(w.shape) for o in out]

    loss = lax.psum(loss[0, 0], ("x", "y", "c"))
    outs = [loss, dx[None]]
    for j in range(4):
        outs += [res[k][j] for k in _WEIGHTS]
    return tuple(outs)
```

```python
import functools

import jax
import jax.numpy as jnp
from jax import lax
from jax.experimental import pallas as pl
from jax.experimental.pallas import tpu as pltpu

F32 = jnp.float32
MXU_DTYPE = jnp.bfloat16
EPS = 1e-6
N_DEV = 8
V7X_VMEM_BYTES = 64 * 1024 * 1024
VMEM_LIMIT_BYTES = V7X_VMEM_BYTES - 8 * 1024 * 1024
LANES = 128
HIGHEST = lax.Precision.HIGHEST
MESH = pl.DeviceIdType.MESH

HG_HEAD = 128
HG_CHUNK = 16
FOX_HEAD = 128
XA_HEADS = 4
POOL_GROUPS = 4

ADAM_LR = 0.001
ADAM_B1 = 0.9
ADAM_B2 = 0.999
ADAM_EPS = 1e-08
ADAM_WD = 0.01
ADAM_STEP = 10

_NN = ((1,), (0,))
_NT = ((1,), (1,))
_TN = ((0,), (0,))


def _dot(a, b, dims):
    return lax.dot_general(a.astype(MXU_DTYPE), b.astype(MXU_DTYPE), (dims, ((), ())),
                           preferred_element_type=F32)


def _dot_f32(a, b, dims):
    return lax.dot_general(a, b, (dims, ((), ())), preferred_element_type=F32, precision=HIGHEST)


def _cp(*sem):
    return pltpu.CompilerParams(dimension_semantics=sem, vmem_limit_bytes=VMEM_LIMIT_BYTES)


def _tile(n, pref, align=LANES):
    if n <= pref:
        return n
    t = (pref // align) * align
    while t >= align:
        if n % t == 0:
            return t
        t -= align
    return n


def _sigmoid(x):
    return jax.nn.sigmoid(x)


def _mm(a, b, mode, out_dtype, *, name, add=None, dep=None, reduce_b=False, b_split=False, o_split=False,
        tm=1024, tn=1024, tk=512):
    ba, bb = a.shape[0], b.shape[0]
    if mode == "tn":
        kdim, m = a.shape[1], a.shape[2]
    else:
        m, kdim = a.shape[1], a.shape[2]
    if b_split:
        s_cnt, b_rows, w = b.shape
        if mode == "nt":
            n = b_rows
            assert kdim == s_cnt * w
            tk = w
        else:
            n = s_cnt * w
            assert b_rows == kdim
            tn = w
        nb = ba
    else:
        n = b.shape[1] if mode == "nt" else b.shape[2]
        nb = max(ba, bb)
    if not (b_split and mode != "nt"):
        tn = _tile(n, tn)
    if not (b_split and mode == "nt"):
        tk = _tile(kdim, tk)
    tm = _tile(m, tm)
    assert m % tm == 0 and n % tn == 0 and kdim % tk == 0, (name, m, n, kdim, tm, tn, tk)
    nk = kdim // tk
    if reduce_b:
        grid = (m // tm, n // tn, nb, nk)
        unpack = lambda i, j, bi, k: (bi, i, j, k)
        sem = ("parallel", "parallel", "arbitrary", "arbitrary")
        nred = nb * nk
    else:
        grid = (nb, m // tm, n // tn, nk)
        unpack = lambda bi, i, j, k: (bi, i, j, k)
        sem = ("parallel", "parallel", "parallel", "arbitrary")
        nred = nk

    def a_map(*g):
        bi, i, j, k = unpack(*g)
        ab = bi if ba > 1 else 0
        return (ab, k, i) if mode == "tn" else (ab, i, k)

    def b_map(*g):
        bi, i, j, k = unpack(*g)
        if b_split:
            return (k, j, 0) if mode == "nt" else (j, k, 0)
        bq = bi if bb > 1 else 0
        return (bq, j, k) if mode == "nt" else (bq, k, j)

    def o_map(*g):
        bi, i, j, k = unpack(*g)
        if o_split:
            return (j, i, 0)
        return (0 if reduce_b else bi, i, j)

    a_blk = (1, tk, tm) if mode == "tn" else (1, tm, tk)
    b_blk = (1, tn, tk) if mode == "nt" else (1, tk, tn)
    dims = {"nn": _NN, "nt": _NT, "tn": _TN}[mode]
    has_add = add is not None

    def body(*refs):
        a_ref, b_ref = refs[:2]
        o_ref, acc_ref = refs[-2:]
        if has_add:
            add_ref = refs[2]
        if reduce_b:
            step = pl.program_id(2) * nk + pl.program_id(3)
        else:
            step = pl.program_id(3)

        @pl.when(step == 0)
        def _():
            acc_ref[...] = jnp.zeros_like(acc_ref)

        acc_ref[...] += _dot(a_ref[0], b_ref[0], dims)

        @pl.when(step == nred - 1)
        def _():
            r = acc_ref[...]
            if has_add:
                r = r + add_ref[0].astype(F32)
            o_ref[0] = r.astype(o_ref.dtype)

    in_specs = [pl.BlockSpec(a_blk, a_map), pl.BlockSpec(b_blk, b_map)]
    operands = [a, b]
    if has_add:
        in_specs.append(pl.BlockSpec((1, tm, tn), o_map))
        operands.append(add)
    if dep is not None:
        in_specs.append(pl.BlockSpec(memory_space=pl.ANY))
        operands.append(dep)
    if o_split:
        out_shape = jax.ShapeDtypeStruct((n // tn, m, tn), out_dtype)
    else:
        out_shape = jax.ShapeDtypeStruct((1 if reduce_b else nb, m, n), out_dtype)
    return pl.pallas_call(
        body, grid=grid, in_specs=in_specs, out_specs=pl.BlockSpec((1, tm, tn), o_map),
        out_shape=out_shape, scratch_shapes=[pltpu.VMEM((tm, tn), F32)],
        compiler_params=_cp(*sem), name=name)(*operands)


def _mm2(a, b, mode, out_dtype, *, name, add=None, **kw):
    b3 = b if kw.get("b_split") else b[None]
    r = _mm(a[None], b3, mode, out_dtype, name=name, add=None if add is None else add[None], **kw)
    return r if kw.get("o_split") else r[0]


def _rms_fwd(x, g, *, name, tb=512):
    t, d = x.shape
    tb = min(tb, t)

    def body(x_ref, g_ref, o_ref):
        xv = x_ref[...]
        r = lax.rsqrt(jnp.mean(xv * xv, axis=-1, keepdims=True) + EPS)
        o_ref[...] = (xv * r * g_ref[...]).astype(o_ref.dtype)

    return pl.pallas_call(
        body, grid=(t // tb,),
        in_specs=[pl.BlockSpec((tb, d), lambda i: (i, 0)), pl.BlockSpec((1, d), lambda i: (0, 0))],
        out_specs=pl.BlockSpec((tb, d), lambda i: (i, 0)),
        out_shape=jax.ShapeDtypeStruct((t, d), MXU_DTYPE), compiler_params=_cp("parallel"), name=name)(x, g)


def _rms_bwd(x, g, dh, dres, *, name, tb=512):
    t, d = x.shape
    tb = min(tb, t)
    has_res = dres is not None

    def body(*refs):
        if has_res:
            x_ref, g_ref, dh_ref, dres_ref, dx_ref, dxl_ref, dg_ref = refs
        else:
            x_ref, g_ref, dh_ref, dx_ref, dxl_ref, dg_ref = refs
        xv = x_ref[...]
        r = lax.rsqrt(jnp.mean(xv * xv, axis=-1, keepdims=True) + EPS)
        xh = xv * r
        dhv = dh_ref[...].astype(F32)

        @pl.when(pl.program_id(0) == 0)
        def _():
            dg_ref[...] = jnp.zeros_like(dg_ref)

        dg_ref[...] += jnp.sum(dhv * xh, axis=0, keepdims=True)
        dxh = dhv * g_ref[...]
        dx = r * (dxh - xh * jnp.mean(dxh * xh, axis=-1, keepdims=True))
        if has_res:
            dx = dx + dres_ref[...]
        dx_ref[...] = dx
        dxl_ref[...] = dx.astype(dxl_ref.dtype)

    row = pl.BlockSpec((tb, d), lambda i: (i, 0))
    vec = pl.BlockSpec((1, d), lambda i: (0, 0))
    operands = [x, g, dh] + ([dres] if has_res else [])
    return pl.pallas_call(
        body, grid=(t // tb,), in_specs=[row, vec, row] + ([row] if has_res else []),
        out_specs=[row, row, vec],
        out_shape=[jax.ShapeDtypeStruct((t, d), F32), jax.ShapeDtypeStruct((t, d), MXU_DTYPE),
                   jax.ShapeDtypeStruct((1, d), F32)],
        compiler_params=_cp("arbitrary"), name=name)(*operands)


def _loss_head(x, g, target, *, name, tb=512):
    t, d = x.shape
    tb = min(tb, t)

    def body(x_ref, g_ref, t_ref, loss_ref, dx_ref, dxl_ref, dg_ref):
        xv = x_ref[...]
        r = lax.rsqrt(jnp.mean(xv * xv, axis=-1, keepdims=True) + EPS)
        xh = xv * r
        gv = g_ref[...]
        err = xh * gv - t_ref[...]

        @pl.when(pl.program_id(0) == 0)
        def _():
            dg_ref[...] = jnp.zeros_like(dg_ref)
            loss_ref[...] = jnp.zeros_like(loss_ref)

        row_loss = jnp.mean(err * err, axis=-1, keepdims=True)
        loss_ref[...] += 0.5 * jnp.sum(row_loss, axis=0, keepdims=True)
        dy = err * (1.0 / d)
        dg_ref[...] += jnp.sum(dy * xh, axis=0, keepdims=True)
        dxh = dy * gv
        dx = r * (dxh - xh * jnp.mean(dxh * xh, axis=-1, keepdims=True))
        dx_ref[...] = dx
        dxl_ref[...] = dx.astype(dxl_ref.dtype)

    row = pl.BlockSpec((tb, d), lambda i: (i, 0))
    vec = pl.BlockSpec((1, d), lambda i: (0, 0))
    return pl.pallas_call(
        body, grid=(t // tb,), in_specs=[row, vec, row],
        out_specs=[pl.BlockSpec((1, 1), lambda i: (0, 0)), row, row, vec],
        out_shape=[jax.ShapeDtypeStruct((1, 1), F32), jax.ShapeDtypeStruct((t, d), F32),
                   jax.ShapeDtypeStruct((t, d), MXU_DTYPE), jax.ShapeDtypeStruct((1, d), F32)],
        compiler_params=_cp("arbitrary"), name=name)(x, g, target)


def _ffn_up(h, wg, wu, *, name, tb=512):
    t, d = h.shape
    s, _, f = wg.shape
    tb = min(tb, t)

    def body(h_ref, wg_ref, wu_ref, g_ref, u_ref, a_ref):
        hv = h_ref[...]
        gv = _dot(hv, wg_ref[0], _NN)
        uv = _dot(hv, wu_ref[0], _NN)
        g_ref[0] = gv
        u_ref[0] = uv
        a_ref[0] = (gv * _sigmoid(gv) * uv).astype(a_ref.dtype)

    wspec = pl.BlockSpec((1, d, f), lambda j, i: (j, 0, 0))
    ospec = pl.BlockSpec((1, tb, f), lambda j, i: (j, i, 0))
    return pl.pallas_call(
        body, grid=(s, t // tb),
        in_specs=[pl.BlockSpec((tb, d), lambda j, i: (i, 0)), wspec, wspec],
        out_specs=[ospec, ospec, ospec],
        out_shape=[jax.ShapeDtypeStruct((s, t, f), F32), jax.ShapeDtypeStruct((s, t, f), F32),
                   jax.ShapeDtypeStruct((s, t, f), MXU_DTYPE)],
        compiler_params=_cp("parallel", "parallel"), name=name)(h, wg, wu)


def _ffn_dact(dy, wd, gate, up, *, name, tb=512):
    t, d = dy.shape
    s, f, _ = wd.shape
    tb = min(tb, t)

    def body(dy_ref, wd_ref, g_ref, u_ref, dg_ref, du_ref):
        da = _dot(dy_ref[...], wd_ref[0], _NT)
        gv = g_ref[0]
        sg = _sigmoid(gv)
        du_ref[0] = (da * gv * sg).astype(du_ref.dtype)
        dg_ref[0] = (da * u_ref[0] * (sg * (1.0 + gv * (1.0 - sg)))).astype(dg_ref.dtype)

    aspec = pl.BlockSpec((1, tb, f), lambda j, i: (j, i, 0))
    return pl.pallas_call(
        body, grid=(s, t // tb),
        in_specs=[pl.BlockSpec((tb, d), lambda j, i: (i, 0)),
                  pl.BlockSpec((1, f, d), lambda j, i: (j, 0, 0)), aspec, aspec],
        out_specs=[aspec, aspec],
        out_shape=[jax.ShapeDtypeStruct((s, t, f), MXU_DTYPE), jax.ShapeDtypeStruct((s, t, f), MXU_DTYPE)],
        compiler_params=_cp("parallel", "parallel"), name=name)(dy, wd, gate, up)


def _xattn_fwd(q, kv, *, name, tb=512):
    t, d = q.shape
    m = kv.shape[0]
    hd = d // XA_HEADS
    tb = min(tb, t)
    scale = hd ** -0.5

    def body(q_ref, kv_ref, o_ref):
        for hh in range(XA_HEADS):
            cs = slice(hh * hd, (hh + 1) * hd)
            s = _dot(q_ref[:, cs], kv_ref[:, cs], _NT) * scale
            s = s - jnp.max(s, axis=-1, keepdims=True)
            e = jnp.exp(s)
            p = e / jnp.sum(e, axis=-1, keepdims=True)
            o_ref[:, cs] = _dot(p, kv_ref[:, d + hh * hd:d + (hh + 1) * hd], _NN).astype(o_ref.dtype)

    return pl.pallas_call(
        body, grid=(t // tb,),
        in_specs=[pl.BlockSpec((tb, d), lambda i: (i, 0)), pl.BlockSpec((m, 2 * d), lambda i: (0, 0))],
        out_specs=pl.BlockSpec((tb, d), lambda i: (i, 0)),
        out_shape=jax.ShapeDtypeStruct((t, d), MXU_DTYPE), compiler_params=_cp("parallel"), name=name)(q, kv)


def _xattn_bwd(q, kv, do, *, name, tb=512):
    t, d = q.shape
    m = kv.shape[0]
    hd = d // XA_HEADS
    tb = min(tb, t)
    scale = hd ** -0.5

    def body(q_ref, kv_ref, do_ref, dq_ref, dkv_ref):
        @pl.when(pl.program_id(0) == 0)
        def _():
            dkv_ref[...] = jnp.zeros_like(dkv_ref)

        for hh in range(XA_HEADS):
            cs = slice(hh * hd, (hh + 1) * hd)
            vs = slice(d + hh * hd, d + (hh + 1) * hd)
            qv, kk, vv, dov = q_ref[:, cs], kv_ref[:, cs], kv_ref[:, vs], do_ref[:, cs]
            s = _dot(qv, kk, _NT) * scale
            s = s - jnp.max(s, axis=-1, keepdims=True)
            e = jnp.exp(s)
            p = e / jnp.sum(e, axis=-1, keepdims=True)
            dkv_ref[:, vs] += _dot(p, dov, _TN)
            dp = _dot(dov, vv, _NT)
            ds = p * (dp - jnp.sum(p * dp, axis=-1, keepdims=True)) * scale
            dq_ref[:, cs] = _dot(ds, kk, _NN).astype(dq_ref.dtype)
            dkv_ref[:, cs] += _dot(ds, qv, _TN)

    row = pl.BlockSpec((tb, d), lambda i: (i, 0))
    full = pl.BlockSpec((m, 2 * d), lambda i: (0, 0))
    return pl.pallas_call(
        body, grid=(t // tb,), in_specs=[row, full, row], out_specs=[row, full],
        out_shape=[jax.ShapeDtypeStruct((t, d), MXU_DTYPE), jax.ShapeDtypeStruct((m, 2 * d), F32)],
        compiler_params=_cp("arbitrary"), name=name)(q, kv, do)


def _pool_window_stats(u, gi, reverse):
    t = u.shape[0]
    row = lax.broadcasted_iota(jnp.int32, u.shape, 0)
    s = u
    for j in range(POOL_GROUPS):
        sh = 1 << j
        if reverse:
            rolled = jnp.where(row < t - sh, pltpu.roll(s, t - sh, axis=0), 0.0)
        else:
            rolled = jnp.where(row >= sh, pltpu.roll(s, sh, axis=0), 0.0)
        s = jnp.where(j <= gi, s + rolled, s)
    return s, row


def _pool_fwd(z, w_pool, scale, *, name):
    t = z.shape[0]
    g_cnt, c, _ = w_pool.shape

    def body(z_ref, w_ref, s_ref, o_ref):
        gi = pl.program_id(0)
        u = z_ref[...]
        win, row = _pool_window_stats(u, gi, False)
        cnt = jnp.minimum(row + 1, lax.shift_left(jnp.int32(2), gi)).astype(F32)
        p = win / cnt - u
        o_ref[...] = (_dot(p, w_ref[0], _NN) * s_ref[...]).astype(o_ref.dtype)

    return pl.pallas_call(
        body, grid=(g_cnt,),
        in_specs=[pl.BlockSpec((t, c), lambda g: (0, g)), pl.BlockSpec((1, c, c), lambda g: (g, 0, 0)),
                  pl.BlockSpec((1, c), lambda g: (0, g))],
        out_specs=pl.BlockSpec((t, c), lambda g: (0, g)),
        out_shape=jax.ShapeDtypeStruct((t, 2 * g_cnt * c), MXU_DTYPE),
        compiler_params=_cp("parallel"), name=name)(z, w_pool, scale)


def _pool_bwd(z, w_pool, scale, dycat, *, name):
    t = z.shape[0]
    g_cnt, c, _ = w_pool.shape

    def body(z_ref, w_ref, s_ref, dy_ref, du_ref, dw_ref, ds_ref):
        gi = pl.program_id(0)
        u = z_ref[...]
        win, row = _pool_window_stats(u, gi, False)
        cnt = jnp.minimum(row + 1, lax.shift_left(jnp.int32(2), gi)).astype(F32)
        p = win / cnt - u
        y = _dot(p, w_ref[0], _NN)
        dya = dy_ref[...].astype(F32)
        ds_ref[...] = jnp.sum(dya * y, axis=0, keepdims=True)
        dy = dya * s_ref[...]
        dw_ref[0] = _dot(p, dy, _TN)
        dp = _dot(dy, w_ref[0], _NT)
        back, _ = _pool_window_stats(dp / cnt, gi, True)
        du_ref[...] = (back - dp).astype(du_ref.dtype)

    col = pl.BlockSpec((t, c), lambda g: (0, g))
    return pl.pallas_call(
        body, grid=(g_cnt,),
        in_specs=[col, pl.BlockSpec((1, c, c), lambda g: (g, 0, 0)), pl.BlockSpec((1, c), lambda g: (0, g)), col],
        out_specs=[col, pl.BlockSpec((1, c, c), lambda g: (g, 0, 0)), pl.BlockSpec((1, c), lambda g: (0, g))],
        out_shape=[jax.ShapeDtypeStruct((t, g_cnt * c), MXU_DTYPE), jax.ShapeDtypeStruct((g_cnt, c, c), F32),
                   jax.ShapeDtypeStruct((1, g_cnt * c), F32)],
        compiler_params=_cp("parallel"), name=name)(z, w_pool, scale, dycat)


def _chunk_tri(lower):
    r = lax.broadcasted_iota(jnp.int32, (LANES, LANES), 0)
    c = lax.broadcasted_iota(jnp.int32, (LANES, LANES), 1)
    same = (r // HG_CHUNK) == (c // HG_CHUNK)
    return jnp.where(same & ((c <= r) if lower else (c >= r)), 1.0, 0.0).astype(F32)


def _hgrn_prepare(q_ref, f_ref, lb_ref, qh_s, k_s, b_s, qt_s, kt_s, gl_s):
    tb = q_ref.shape[0]
    lb = lb_ref[...]
    sg = _sigmoid(f_ref[...])
    f = lb + (1.0 - lb) * sg
    logf = jnp.log(f)
    qv = q_ref[...]
    qh = qv * _sigmoid(qv) * (HG_HEAD ** -0.5)
    tri = _chunk_tri(True)
    for r in range(tb // LANES):
        rows = slice(r * LANES, (r + 1) * LANES)
        b_s[rows, :] = _dot_f32(tri, logf[rows, :], _NN)
    b = b_s[...]
    b3 = b.reshape(tb // HG_CHUNK, HG_CHUNK, HG_HEAD)
    bl = b3[:, HG_CHUNK - 1:HG_CHUNK, :]
    k = 1.0 - f
    qh_s[...] = qh
    k_s[...] = k
    qt_s[...] = qh * jnp.exp(b)
    kt_s[...] = k * jnp.exp(bl - b3).reshape(tb, HG_HEAD)
    gl_s[...] = jnp.exp(jnp.broadcast_to(bl, b3.shape)).reshape(tb, HG_HEAD)
    return sg, f


def _hgrn_intra(qh, kk, bq, rows_a, rows_b):
    ones = jnp.ones((HG_HEAD, HG_HEAD), MXU_DTYPE)
    es, stack_a, stack_b = [], [], []
    for s in range(HG_CHUNK):
        e = jnp.exp(jnp.minimum(bq - bq[s:s + 1, :], 0.0))
        es.append(e)
        stack_a.append(qh * e * kk[s:s + 1, :])
        if rows_a is not None:
            stack_b.append(rows_a * rows_b[s:s + 1, :])
    a_rep = _dot(jnp.concatenate(stack_a, axis=0), ones, _NN)
    d_rep = _dot(jnp.concatenate(stack_b, axis=0), ones, _NN) if rows_a is not None else None
    return es, a_rep, d_rep


def _hgrn_fwd(z, lb, hg_norm, ycat, *, name, tb=512):
    t = z.shape[0]
    mix_b = lb.shape[1]
    heads = mix_b // HG_HEAD
    off = (z.shape[1] - 4 * mix_b) // HG_HEAD
    tb = min(tb, t)
    ncb = tb // HG_CHUNK

    def body(q_ref, f_ref, i_ref, g_ref, lb_ref, hn_ref, ycat_in, y_ref, o_ref, st_ref,
             state, qh_s, k_s, b_s, qt_s, kt_s, gl_s, o_s):
        del ycat_in

        @pl.when(pl.program_id(1) == 0)
        def _():
            state[...] = jnp.zeros_like(state)

        _hgrn_prepare(q_ref, f_ref, lb_ref, qh_s, k_s, b_s, qt_s, kt_s, gl_s)
        row = lax.broadcasted_iota(jnp.int32, (HG_CHUNK, HG_HEAD), 0)

        def chunk(c, carry):
            rows = pl.ds(pl.multiple_of(c * HG_CHUNK, HG_CHUNK), HG_CHUNK)
            st = state[...]
            st_ref[0, c] = st
            vv = i_ref[rows, :]
            o = _dot(qt_s[rows, :], st, _NT)
            _, a_rep, _ = _hgrn_intra(qh_s[rows, :], k_s[rows, :], b_s[rows, :], None, None)
            for s in range(HG_CHUNK):
                o = o + jnp.where(row >= s, a_rep[s * HG_CHUNK:(s + 1) * HG_CHUNK, :] * vv[s:s + 1, :], 0.0)
            o_s[rows, :] = o
            state[...] = st * gl_s[rows, :][0:1, :] + _dot(vv, kt_s[rows, :], _TN)
            return carry

        lax.fori_loop(0, ncb, chunk, 0)
        o = o_s[...]
        o_ref[...] = o
        r = lax.rsqrt(jnp.mean(o * o, axis=-1, keepdims=True) + EPS)
        gv = g_ref[...]
        y_ref[...] = (o * r * hn_ref[...] * (gv * _sigmoid(gv))).astype(y_ref.dtype)

    def zcol(kind):
        return pl.BlockSpec((tb, HG_HEAD), lambda h, i: (i, off + kind * heads + h))

    scratch = [pltpu.VMEM((HG_HEAD, HG_HEAD), F32)] + [pltpu.VMEM((tb, HG_HEAD), F32)] * 7
    return pl.pallas_call(
        body, grid=(heads, t // tb),
        in_specs=[zcol(0), zcol(1), zcol(2), zcol(3), pl.BlockSpec((1, HG_HEAD), lambda h, i: (0, h)),
                  pl.BlockSpec((1, HG_HEAD), lambda h, i: (0, 0)), pl.BlockSpec(memory_space=pl.ANY)],
        out_specs=[pl.BlockSpec((tb, HG_HEAD), lambda h, i: (i, heads + h)),
                   pl.BlockSpec((tb, HG_HEAD), lambda h, i: (i, h)),
                   pl.BlockSpec((1, ncb, HG_HEAD, HG_HEAD), lambda h, i: (h, i, 0, 0))],
        out_shape=[jax.ShapeDtypeStruct(ycat.shape, ycat.dtype), jax.ShapeDtypeStruct((t, mix_b), F32),
                   jax.ShapeDtypeStruct((heads, t // HG_CHUNK, HG_HEAD, HG_HEAD), F32)],
        scratch_shapes=scratch, input_output_aliases={6: 0},
        compiler_params=_cp("parallel", "arbitrary"), name=name)(z, z, z, z, lb, hg_norm, ycat)


def _hgrn_bwd(z, lb, hg_norm, o_raw, states, dycat, *, name, tb=512):
    t = z.shape[0]
    mix_b = lb.shape[1]
    heads = mix_b // HG_HEAD
    off = (z.shape[1] - 4 * mix_b) // HG_HEAD
    tb = min(tb, t)
    ncb = tb // HG_CHUNK
    nt = t // tb

    def body(q_ref, f_ref, i_ref, g_ref, lb_ref, hn_ref, o_ref, st_ref, dy_ref,
             dq_ref, dfl_ref, di_ref, dg_ref, dlb_ref, dhn_ref,
             dstate, qh_s, k_s, b_s, qt_s, kt_s, gl_s, do_s, dqh_s, dk_s, db_s):
        first = pl.program_id(1) == 0

        @pl.when(first)
        def _():
            dstate[...] = jnp.zeros_like(dstate)
            dlb_ref[...] = jnp.zeros_like(dlb_ref)

        @pl.when(first & (pl.program_id(0) == 0))
        def _():
            dhn_ref[...] = jnp.zeros_like(dhn_ref)

        sg, f = _hgrn_prepare(q_ref, f_ref, lb_ref, qh_s, k_s, b_s, qt_s, kt_s, gl_s)
        o = o_ref[...]
        r = lax.rsqrt(jnp.mean(o * o, axis=-1, keepdims=True) + EPS)
        oh = o * r
        gv = g_ref[...]
        sgg = _sigmoid(gv)
        dy = dy_ref[...].astype(F32)
        hn = hn_ref[...]
        dg_ref[...] = (dy * oh * hn * (sgg * (1.0 + gv * (1.0 - sgg)))).astype(dg_ref.dtype)
        don = dy * (gv * sgg)
        dhn_ref[...] += jnp.sum(don * oh, axis=0, keepdims=True)
        doh = don * hn
        do_s[...] = r * (doh - oh * jnp.mean(doh * oh, axis=-1, keepdims=True))
        row = lax.broadcasted_iota(jnp.int32, (HG_CHUNK, HG_HEAD), 0)

        def chunk(ci, carry):
            c = ncb - 1 - ci
            rows = pl.ds(pl.multiple_of(c * HG_CHUNK, HG_CHUNK), HG_CHUNK)
            st_prev = st_ref[0, c]
            dst = dstate[...]
            qh, kk, bq, vv = qh_s[rows, :], k_s[rows, :], b_s[rows, :], i_ref[rows, :]
            qt, kt, doo = qt_s[rows, :], kt_s[rows, :], do_s[rows, :]
            gl = gl_s[rows, :][0:1, :]
            es, a_rep, d_rep = _hgrn_intra(qh, kk, bq, doo, vv)
            dqh = jnp.exp(bq) * _dot(doo, st_prev, _NN)
            dk = jnp.exp(bq[HG_CHUNK - 1:HG_CHUNK, :] - bq) * _dot(vv, dst, _NN)
            dv = _dot(kt, dst, _NT)
            for s in range(HG_CHUNK):
                blk = slice(s * HG_CHUNK, (s + 1) * HG_CHUNK)
                wgt = jnp.where(row >= s, d_rep[blk, :] * es[s], 0.0)
                dqh = dqh + wgt * kk[s:s + 1, :]
                dk = dk + jnp.where(row == s, jnp.sum(wgt * qh, axis=0, keepdims=True), 0.0)
                dv_row = jnp.sum(jnp.where(row >= s, a_rep[blk, :] * doo, 0.0), axis=0, keepdims=True)
                dv = dv + jnp.where(row == s, dv_row, 0.0)
            st_next = st_prev * gl + _dot(vv, kt, _TN)
            db = qh * dqh - kk * dk
            db = db + jnp.where(row == HG_CHUNK - 1, jnp.sum(st_next * dst, axis=0, keepdims=True), 0.0)
            dstate[...] = dst * gl + _dot(doo, qt, _TN)
            dqh_s[rows, :] = dqh
            dk_s[rows, :] = dk
            db_s[rows, :] = db
            di_ref[rows, :] = dv.astype(di_ref.dtype)
            return carry

        lax.fori_loop(0, ncb, chunk, 0)
        tri = _chunk_tri(False)
        lb_v = lb_ref[...]
        qv = q_ref[...]
        sgq = _sigmoid(qv)
        dq_ref[...] = (dqh_s[...] * (HG_HEAD ** -0.5) * (sgq * (1.0 + qv * (1.0 - sgq)))).astype(dq_ref.dtype)
        dlb = jnp.zeros((1, HG_HEAD), F32)
        for rr in range(tb // LANES):
            rws = slice(rr * LANES, (rr + 1) * LANES)
            dlogf = _dot_f32(tri, db_s[rws, :], _NN)
            df = dlogf / f[rws, :] - dk_s[rws, :]
            sgr = sg[rws, :]
            dfl_ref[rws, :] = (df * (1.0 - lb_v) * sgr * (1.0 - sgr)).astype(dfl_ref.dtype)
            dlb = dlb + jnp.sum(df * (1.0 - sgr), axis=0, keepdims=True)
        dlb_ref[...] += dlb

    def zcol(kind):
        return pl.BlockSpec((tb, HG_HEAD), lambda h, i: (nt - 1 - i, off + kind * heads + h))

    hcol = pl.BlockSpec((tb, HG_HEAD), lambda h, i: (nt - 1 - i, h))
    scratch = [pltpu.VMEM((HG_HEAD, HG_HEAD), F32)] + [pltpu.VMEM((tb, HG_HEAD), F32)] * 10
    out = jax.ShapeDtypeStruct((t, mix_b), MXU_DTYPE)
    return pl.pallas_call(
        body, grid=(heads, nt),
        in_specs=[zcol(0), zcol(1), zcol(2), zcol(3), pl.BlockSpec((1, HG_HEAD), lambda h, i: (0, h)),
                  pl.BlockSpec((1, HG_HEAD), lambda h, i: (0, 0)), hcol,
                  pl.BlockSpec((1, ncb, HG_HEAD, HG_HEAD), lambda h, i: (h, nt - 1 - i, 0, 0)),
                  pl.BlockSpec((tb, HG_HEAD), lambda h, i: (nt - 1 - i, heads + h))],
        out_specs=[hcol, hcol, hcol, hcol, pl.BlockSpec((1, HG_HEAD), lambda h, i: (0, h)),
                   pl.BlockSpec((1, HG_HEAD), lambda h, i: (0, 0))],
        out_shape=[out, out, out, out, jax.ShapeDtypeStruct((1, mix_b), F32),
                   jax.ShapeDtypeStruct((1, HG_HEAD), F32)],
        scratch_shapes=scratch, compiler_params=_cp("arbitrary", "arbitrary"),
        name=name)(z, z, z, z, lb, hg_norm, o_raw, states, dycat)


def _lb_fwd(lb_table, layer, *, name):
    rows, width = lb_table.shape

    def body(t_ref, o_ref):
        tv = t_ref[...]
        e = jnp.exp(tv - jnp.max(tv, axis=0, keepdims=True))
        sm = e / jnp.sum(e, axis=0, keepdims=True)
        o_ref[...] = jnp.sum(sm[1:layer + 2, :], axis=0, keepdims=True)

    return pl.pallas_call(body, out_shape=jax.ShapeDtypeStruct((1, width), F32), name=name)(lb_table)


def _lb_bwd(lb_table, dlb, layer, *, name):
    rows, width = lb_table.shape

    def body(t_ref, d_ref, o_ref):
        tv = t_ref[...]
        e = jnp.exp(tv - jnp.max(tv, axis=0, keepdims=True))
        sm = e / jnp.sum(e, axis=0, keepdims=True)
        ridx = lax.broadcasted_iota(jnp.int32, sm.shape, 0)
        dsm = jnp.where((ridx >= 1) & (ridx <= layer + 1), d_ref[...], 0.0)
        o_ref[...] = sm * (dsm - jnp.sum(sm * dsm, axis=0, keepdims=True))

    return pl.pallas_call(body, out_shape=jax.ShapeDtypeStruct((rows, width), F32), name=name)(lb_table, dlb)


FOX_BLOCK = 512


def _fox_prep(zf, b_f, heads, *, name, blk=256):
    t = zf.shape[0]

    def body(z_ref, b_ref, fc_ref, fr_ref):
        r = lax.broadcasted_iota(jnp.int32, (blk, blk), 0)
        c = lax.broadcasted_iota(jnp.int32, (blk, blk), 1)
        tri = jnp.where(c <= r, 1.0, 0.0).astype(F32)
        carry = jnp.zeros((1, LANES), F32)
        for j in range(t // blk):
            rows = slice(j * blk, (j + 1) * blk)
            ls = jax.nn.log_sigmoid(z_ref[rows, :] + b_ref[...])
            fb = _dot_f32(tri, ls, _NN) + carry
            carry = fb[blk - 1:blk, :]
            fc_ref[rows, :] = fb
            fbt = fb.T
            for hh in range(heads):
                fr_ref[hh, :, rows] = fbt[hh:hh + 1, :]

    return pl.pallas_call(
        body, out_shape=[jax.ShapeDtypeStruct((t, LANES), F32), jax.ShapeDtypeStruct((heads, 1, t), F32)],
        compiler_params=pltpu.CompilerParams(vmem_limit_bytes=VMEM_LIMIT_BYTES), name=name)(zf, b_f)


def _fox_scores(k_blk, q_blk, fcol_blk, frow_blk, head, kj, qi, blk):
    lane = lax.broadcasted_iota(jnp.int32, fcol_blk.shape, 1)
    fcol = jnp.sum(jnp.where(lane == head, fcol_blk, 0.0), axis=1, keepdims=True)
    s = _dot(k_blk, q_blk, _NT) * (FOX_HEAD ** -0.5) + (frow_blk - fcol)
    key = kj * blk + lax.broadcasted_iota(jnp.int32, s.shape, 0)
    qry = qi * blk + lax.broadcasted_iota(jnp.int32, s.shape, 1)
    return jnp.where(key <= qry, s, -jnp.inf)


def _fox_fwd(zqkv, fcol, frow, *, name):
    t = zqkv.shape[0]
    d = zqkv.shape[1] // 3
    heads = d // FOX_HEAD
    blk = min(FOX_BLOCK, t)
    nq = t // blk

    def body(q_ref, k_ref, v_ref, fc_ref, fr_ref, o_ref, lse_ref):
        head = pl.program_id(0)

        def q_block(qi, carry):
            qrows = pl.ds(pl.multiple_of(qi * blk, blk), blk)
            q_blk = q_ref[qrows, :]
            frow_blk = fr_ref[0, :, qrows]

            def k_block(kj, st):
                m, l, acc = st
                krows = pl.ds(pl.multiple_of(kj * blk, blk), blk)
                s = _fox_scores(k_ref[krows, :], q_blk, fc_ref[krows, :], frow_blk, head, kj, qi, blk)
                m_new = jnp.maximum(m, jnp.max(s, axis=0, keepdims=True))
                alpha = jnp.exp(m - m_new)
                p = jnp.exp(s - m_new)
                l = alpha * l + jnp.sum(p, axis=0, keepdims=True)
                acc = acc * alpha + _dot(v_ref[krows, :], p, _TN)
                return m_new, l, acc

            init = (jnp.full((1, blk), -jnp.inf, F32), jnp.zeros((1, blk), F32),
                    jnp.zeros((FOX_HEAD, blk), F32))
            m, l, acc = lax.fori_loop(0, qi + 1, k_block, init)
            o_ref[qrows, :] = (acc / l).T.astype(o_ref.dtype)
            lse_ref[0, :, qrows] = m + jnp.log(l)
            return carry

        lax.fori_loop(0, nq, q_block, 0)

    def col(kind):
        return pl.BlockSpec((t, FOX_HEAD), lambda h: (0, kind * heads + h))

    rowvec = pl.BlockSpec((1, 1, t), lambda h: (h, 0, 0))
    return pl.pallas_call(
        body, grid=(heads,),
        in_specs=[col(0), col(1), col(2), pl.BlockSpec((t, LANES), lambda h: (0, 0)), rowvec],
        out_specs=[pl.BlockSpec((t, FOX_HEAD), lambda h: (0, h)), rowvec],
        out_shape=[jax.ShapeDtypeStruct((t, d), MXU_DTYPE), jax.ShapeDtypeStruct((heads, 1, t), F32)],
        compiler_params=_cp("parallel"), name=name)(zqkv, zqkv, zqkv, fcol, frow)


def _fox_bwd(zqkv, fcol, frow, lse, o, do, *, name):
    t = zqkv.shape[0]
    d = zqkv.shape[1] // 3
    heads = d // FOX_HEAD
    blk = min(FOX_BLOCK, t)
    nq = t // blk
    scale = FOX_HEAD ** -0.5

    def body(q_ref, k_ref, v_ref, fc_ref, fr_ref, lse_ref, o_ref, do_ref,
             dq_ref, dk_ref, dv_ref, rq_ref, rk_ref, dq_s, drow_s):
        head = pl.program_id(0)
        ones = jnp.ones((8, blk), MXU_DTYPE)
        dq_s[...] = jnp.zeros_like(dq_s)
        rq_ref[...] = jnp.zeros_like(rq_ref)
        ones_f = jnp.ones((8, FOX_HEAD), F32)
        for j in range(nq):
            rows = slice(j * blk, (j + 1) * blk)
            prod = do_ref[rows, :].astype(F32) * o_ref[rows, :].astype(F32)
            drow_s[:, rows] = _dot_f32(ones_f, prod, _NT)

        def k_block(kj, carry):
            krows = pl.ds(pl.multiple_of(kj * blk, blk), blk)
            k_blk, v_blk, fc_blk = k_ref[krows, :], v_ref[krows, :], fc_ref[krows, :]

            def q_block(qi, st):
                dk, dv, rk = st
                qrows = pl.ds(pl.multiple_of(qi * blk, blk), blk)
                q_blk, do_blk = q_ref[qrows, :], do_ref[qrows, :]
                s = _fox_scores(k_blk, q_blk, fc_blk, fr_ref[0, :, qrows], head, kj, qi, blk)
                p = jnp.exp(s - lse_ref[0, :, qrows])
                dv = dv + _dot(p, do_blk, _NN)
                dp = _dot(v_blk, do_blk, _NT)
                ds = (p * (dp - drow_s[0:1, qrows])).astype(MXU_DTYPE)
                dk = dk + _dot(ds, q_blk, _NN)
                dq_s[qrows, :] += _dot(ds, k_blk, _TN)
                rq_ref[0, :, qrows] += _dot(ones, ds, _NN)[0:1, :]
                rk = rk + _dot(ones, ds, _NT)
                return dk, dv, rk

            init = (jnp.zeros((blk, FOX_HEAD), F32), jnp.zeros((blk, FOX_HEAD), F32), jnp.zeros((8, blk), F32))
            dk, dv, rk = lax.fori_loop(kj, nq, q_block, init)
            dk_ref[krows, :] = (dk * scale).astype(dk_ref.dtype)
            dv_ref[krows, :] = dv.astype(dv_ref.dtype)
            rk_ref[0, :, krows] = rk[0:1, :]
            return carry

        lax.fori_loop(0, nq, k_block, 0)
        dq_ref[...] = (dq_s[...] * scale).astype(dq_ref.dtype)

    def col(kind):
        return pl.BlockSpec((t, FOX_HEAD), lambda h: (0, kind * heads + h))

    hcol = pl.BlockSpec((t, FOX_HEAD), lambda h: (0, h))
    rowvec = pl.BlockSpec((1, 1, t), lambda h: (h, 0, 0))
    out = jax.ShapeDtypeStruct((t, d), MXU_DTYPE)
    vec = jax.ShapeDtypeStruct((heads, 1, t), F32)
    return pl.pallas_call(
        body, grid=(heads,),
        in_specs=[col(0), col(1), col(2), pl.BlockSpec((t, LANES), lambda h: (0, 0)), rowvec, rowvec, hcol, hcol],
        out_specs=[hcol, hcol, hcol, rowvec, rowvec],
        out_shape=[out, out, out, vec, vec],
        scratch_shapes=[pltpu.VMEM((t, FOX_HEAD), F32), pltpu.VMEM((8, t), F32)],
        compiler_params=_cp("parallel"), name=name)(zqkv, zqkv, zqkv, fcol, frow, lse, o, do)


def _fox_gate_bwd(rq, rk, zf, b_f, *, name, blk=256):
    heads, t = rq.shape

    def body(rq_ref, rk_ref, z_ref, b_ref, dfl_ref, db_ref):
        r = lax.broadcasted_iota(jnp.int32, (blk, blk), 0)
        c = lax.broadcasted_iota(jnp.int32, (blk, blk), 1)
        tri = jnp.where(r >= c, 1.0, 0.0).astype(F32)
        carry = jnp.zeros((heads, 1), F32)
        db = jnp.zeros((1, LANES), F32)
        pad = jnp.zeros((LANES - heads, blk), F32)
        for j in reversed(range(t // blk)):
            cols = slice(j * blk, (j + 1) * blk)
            df = rq_ref[:, cols] - rk_ref[:, cols]
            dls = _dot_f32(df, tri, _NN) + carry
            carry = dls[:, 0:1]
            dls_t = jnp.concatenate([dls, pad], axis=0).T
            dfl = dls_t * _sigmoid(-(z_ref[cols, :] + b_ref[...]))
            dfl_ref[cols, :] = dfl.astype(dfl_ref.dtype)
            db = db + jnp.sum(dfl, axis=0, keepdims=True)
        db_ref[...] = db

    return pl.pallas_call(
        body, out_shape=[jax.ShapeDtypeStruct((t, LANES), MXU_DTYPE), jax.ShapeDtypeStruct((1, LANES), F32)],
        compiler_params=pltpu.CompilerParams(vmem_limit_bytes=VMEM_LIMIT_BYTES), name=name)(rq, rk, zf, b_f)


def _adamw(w, m, v, parts, *, name, layer=None, prev=None, tr=128):
    lcnt, r, c = w.shape
    p = parts.shape[0]
    li = 0 if layer is None else layer
    tr = _tile(r, tr, 16)
    has_prev = prev is not None

    def body(*refs):
        w_ref, m_ref, v_ref, p_ref = refs[:4]
        g_ref, d_ref, nm_ref, nv_ref = refs[-4:]
        g = p_ref[0].astype(F32)
        for j in range(1, p):
            g = g + p_ref[j].astype(F32)
        wv = w_ref[0]
        mn = ADAM_B1 * m_ref[0] + (1.0 - ADAM_B1) * g
        vn = ADAM_B2 * v_ref[0] + (1.0 - ADAM_B2) * (g * g)
        m_hat = mn / (1.0 - ADAM_B1 ** ADAM_STEP)
        v_hat = vn / (1.0 - ADAM_B2 ** ADAM_STEP)
        g_ref[0] = g
        d_ref[0] = -ADAM_LR * (m_hat / (jnp.sqrt(v_hat) + ADAM_EPS) + ADAM_WD * wv)
        nm_ref[0] = mn
        nv_ref[0] = vn

    slab = pl.BlockSpec((1, tr, c), lambda i: (li, i, 0))
    in_specs = [slab, slab, slab, pl.BlockSpec((p, tr, c), lambda i: (0, i, 0))]
    operands = [w, m, v, parts]
    aliases = {}
    if has_prev:
        in_specs += [pl.BlockSpec(memory_space=pl.ANY)] * 4
        operands += list(prev)
        aliases = {4: 0, 5: 1, 6: 2, 7: 3}
    shp = jax.ShapeDtypeStruct((lcnt, r, c), F32)
    return pl.pallas_call(
        body, grid=(r // tr,), in_specs=in_specs, out_specs=[slab] * 4, out_shape=[shp] * 4,
        input_output_aliases=aliases, compiler_params=_cp("parallel"), name=name)(*operands)


def _my_place():
    return lax.axis_index("x"), lax.axis_index("y"), lax.axis_index("c")


def _slot(p):
    return 4 * p[0] + 2 * p[1] + p[2]


def _peer(me, mask):
    x, y, c = me
    return (1 - x if mask & 4 else x, 1 - y if mask & 2 else y, 1 - c if mask & 1 else c)


_HBM = pl.BlockSpec(memory_space=pltpu.HBM)
_SEM = pl.BlockSpec(memory_space=pltpu.SEMAPHORE)
_ANY = pl.BlockSpec(memory_space=pl.ANY)
_EFFECT = pltpu.SideEffectType.DATAFLOW_SIDE_EFFECTING


def _push_copy(src_refs, land_refs, send_sems, recv_sems, a, mask, me, per_peer, outgoing):
    peer = _peer(me, mask)
    src = src_refs[a].at[_slot(peer)] if per_peer else src_refs[a]
    dst = land_refs[a].at[_slot(me) if outgoing else _slot(peer)]
    k = a * (N_DEV - 1) + mask - 1
    return pltpu.make_async_remote_copy(
        src_ref=src, dst_ref=dst, send_sem=send_sems.at[k], recv_sem=recv_sems.at[k],
        device_id=peer, device_id_type=MESH)


def _push_start(srcs, dep, *, per_peer, name):
    n = len(srcs)
    mine = _slot(_my_place())
    lands = []
    for s in srcs:
        own = lax.dynamic_index_in_dim(s, mine, 0, keepdims=True) if per_peer else s[None]
        shape = s.shape if per_peer else (N_DEV,) + s.shape
        lands.append(lax.dynamic_update_slice_in_dim(lax.empty(shape, s.dtype), own, mine, 0))
    has_dep = dep is not None

    def body(*refs):
        src_refs, land_refs = refs[:n], refs[n:2 * n]
        send_sems, recv_sems = refs[2 * n + has_dep], refs[2 * n + has_dep + 1]
        token = refs[-1]
        me = _my_place()
        for a in range(n):
            for mask in range(1, N_DEV):
                _push_copy(src_refs, land_refs, send_sems, recv_sems, a, mask, me, per_peer, True).start()
        token[...] = jnp.zeros_like(token)

    hbm_in = [pltpu.with_memory_space_constraint(v, pltpu.HBM) for v in list(srcs) + lands]
    out = pl.pallas_call(
        body, name=name,
        out_shape=(pltpu.SemaphoreType.DMA((n * (N_DEV - 1),)), pltpu.SemaphoreType.DMA((n * (N_DEV - 1),)),
                   *[pltpu.HBM(v.shape, v.dtype) for v in hbm_in], jax.ShapeDtypeStruct((8, LANES), F32)),
        in_specs=[_HBM] * (2 * n) + ([_ANY] if has_dep else []),
        out_specs=(_SEM, _SEM, *[_HBM] * (2 * n), pl.BlockSpec(memory_space=pltpu.VMEM)),
        input_output_aliases={i: 2 + i for i in range(2 * n)},
        compiler_params=pltpu.CompilerParams(has_side_effects=_EFFECT),
    )(*hbm_in, *([dep] if has_dep else []))
    return (n, per_peer, out[:-1]), out[-1]


def _push_wait(handle, after, *, name):
    n, per_peer, (send_sems, recv_sems, *bufs) = handle

    def body(*refs):
        src_refs, land_refs = refs[:n], refs[n:2 * n]
        send_sems, recv_sems = refs[2 * n], refs[2 * n + 1]
        me = _my_place()
        for a in range(n):
            for mask in range(1, N_DEV):
                cp = _push_copy(src_refs, land_refs, send_sems, recv_sems, a, mask, me, per_peer, False)
                cp.wait_send()
                cp.wait_recv()

    out = pl.pallas_call(
        body, name=name, out_shape=tuple(pltpu.HBM(v.shape, v.dtype) for v in bufs),
        in_specs=[_HBM] * (2 * n) + [_SEM, _SEM, _ANY], out_specs=tuple([_HBM] * (2 * n)),
        input_output_aliases={i: i for i in range(2 * n)},
        compiler_params=pltpu.CompilerParams(has_side_effects=_EFFECT),
    )(*bufs, send_sems, recv_sems, after)
    return list(out[n:])


def _all_reduce_rows(v, *, name):
    r, c = v.shape

    def body(v_ref, o_ref, buf, send_sems, recv_sems):
        me = _my_place()
        mine = _slot(me)
        sends = []
        for mask in range(1, N_DEV):
            peer = _peer(me, mask)
            sends.append(pltpu.make_async_remote_copy(
                src_ref=v_ref, dst_ref=buf.at[mine], send_sem=send_sems.at[mask - 1],
                recv_sem=recv_sems.at[mask - 1], device_id=peer, device_id_type=MESH))
        for cp in sends:
            cp.start()
        buf[mine] = v_ref[...]
        for mask in range(1, N_DEV):
            peer = _peer(me, mask)
            pltpu.make_async_remote_copy(
                src_ref=v_ref, dst_ref=buf.at[_slot(peer)], send_sem=send_sems.at[mask - 1],
                recv_sem=recv_sems.at[mask - 1], device_id=peer, device_id_type=MESH).wait_recv()
        for cp in sends:
            cp.wait_send()
        total = buf[0]
        for j in range(1, N_DEV):
            total = total + buf[j]
        o_ref[...] = total

    vm = pl.BlockSpec(memory_space=pltpu.VMEM)
    return pl.pallas_call(
        body, in_specs=[vm], out_specs=vm, out_shape=jax.ShapeDtypeStruct((r, c), F32),
        scratch_shapes=[pltpu.VMEM((N_DEV, r, c), F32), pltpu.SemaphoreType.DMA((7,)),
                        pltpu.SemaphoreType.DMA((7,))],
        name=name)(v)


def _xa_fwd(x, mem, g_x, g_m, wq, wkv, wo, tag):
    hx = _rms_fwd(x, g_x, name=f"xa{tag}_norm")
    memn = _rms_fwd(mem, g_m, name=f"xa{tag}_mem_norm")
    q = _mm2(hx, wq, "nn", MXU_DTYPE, name=f"xa{tag}_q")
    kv = _mm2(memn, wkv, "nn", MXU_DTYPE, name=f"xa{tag}_kv", b_split=True)
    o = _xattn_fwd(q, kv, name=f"xa{tag}_attn")
    return _mm2(o, wo, "nn", F32, name=f"xa{tag}_out", add=x), (hx, memn, q, kv, o)


def _xa_bwd(x, mem, g_x, g_m, wq, wkv, wo, saved, dxo, dxo_lo, tag, dep=None):
    hx, memn, q, kv, o = saved
    do = _mm2(dxo_lo, wo, "nt", MXU_DTYPE, name=f"xa{tag}_do", dep=dep)
    dwo = _mm2(o, dxo_lo, "tn", MXU_DTYPE, name=f"xa{tag}_dwo")
    dq, dkv = _xattn_bwd(q, kv, do, name=f"xa{tag}_attn_bwd")
    dwq = _mm2(hx, dq, "tn", MXU_DTYPE, name=f"xa{tag}_dwq")
    dhx = _mm2(dq, wq, "nt", F32, name=f"xa{tag}_dh")
    dx, dx_lo, dgx = _rms_bwd(x, g_x, dhx, dxo, name=f"xa{tag}_norm_bwd")
    dwkv = _mm2(memn, dkv, "tn", MXU_DTYPE, name=f"xa{tag}_dwkv", o_split=True, tn=wkv.shape[2])
    dmemn = _mm2(dkv, wkv, "nt", F32, name=f"xa{tag}_dmem", b_split=True)
    _, _, dgm = _rms_bwd(mem, g_m, dmemn, None, name=f"xa{tag}_mem_norm_bwd")
    return dx, dx_lo, dgx, dgm, (dwq, dwkv, dwo)


def _ffn_fwd(x, g, wg, wu, wd, tag):
    h = _rms_fwd(x, g, name=f"ffn{tag}_norm")
    gate, up, act = _ffn_up(h, wg, wu, name=f"ffn{tag}_up")
    x_new = _mm(act, wd, "nn", F32, name=f"ffn{tag}_down", reduce_b=True, add=x[None])[0]
    return x_new, (h, gate, up, act)


def _ffn_bwd(x, g, wg, wu, wd, saved, dxo, dxo_lo, tag, dep=None):
    h, gate, up, act = saved
    dwd = _mm(act, dxo_lo[None], "tn", MXU_DTYPE, name=f"ffn{tag}_dwd", dep=dep)
    dgate, dup = _ffn_dact(dxo_lo, wd, gate, up, name=f"ffn{tag}_dact")
    dwg = _mm(h[None], dgate, "tn", MXU_DTYPE, name=f"ffn{tag}_dwg")
    dwu = _mm(h[None], dup, "tn", MXU_DTYPE, name=f"ffn{tag}_dwu")
    dh = _mm(dgate, wg, "nt", F32, name=f"ffn{tag}_dh_gate", reduce_b=True)
    dh = _mm(dup, wu, "nt", F32, name=f"ffn{tag}_dh_up", reduce_b=True, add=dh)[0]
    dx, dx_lo, dg = _rms_bwd(x, g, dh, dxo, name=f"ffn{tag}_norm_bwd")
    return dx, dx_lo, dg, (dwg, dwu, dwd)


def _even_fwd(x, g, lb, w_in, w_pool, pool_scale, hg_norm, w_out):
    h = _rms_fwd(x, g, name="ev_norm")
    z = _mm2(h, w_in, "nn", F32, name="ev_in", b_split=True)
    ycat = _pool_fwd(z, w_pool, pool_scale, name="ev_pool")
    ycat, o_raw, states = _hgrn_fwd(z, lb, hg_norm, ycat, name="ev_hgrn")
    return _mm2(ycat, w_out, "nn", F32, name="ev_out", add=x), (h, z, ycat, o_raw, states)


def _even_bwd(x, g, lb, w_in, w_pool, pool_scale, hg_norm, w_out, saved, dxo, dxo_lo, dep=None):
    h, z, ycat, o_raw, states = saved
    dycat = _mm2(dxo_lo, w_out, "nt", MXU_DTYPE, name="ev_dy", dep=dep)
    dw_out = _mm2(ycat, dxo_lo, "tn", MXU_DTYPE, name="ev_dw_out")
    du, dw_pool, dscale = _pool_bwd(z, w_pool, pool_scale, dycat, name="ev_pool_bwd")
    dq, dfl, di, dg, dlb, dhn = _hgrn_bwd(z, lb, hg_norm, o_raw, states, dycat, name="ev_hgrn_bwd")
    dz = jnp.concatenate([du, dq, dfl, di, dg], axis=1)
    dw_in = _mm2(h, dz, "tn", MXU_DTYPE, name="ev_dw_in", o_split=True, tn=w_in.shape[2])
    dh = _mm2(dz, w_in, "nt", F32, name="ev_dh", b_split=True)
    dx, dx_lo, dgn = _rms_bwd(x, g, dh, dxo, name="ev_norm_bwd")
    return dx, dx_lo, dict(ev_norm=dgn, ev_pool_scale=dscale, ev_hg_norm=dhn, lb=dlb), (dw_in, dw_pool, dw_out)


def _odd_fwd(x, g, w_qkv, w_f, b_f, w_out):
    heads = w_qkv.shape[1] // (3 * FOX_HEAD)
    h = _rms_fwd(x, g, name="od_norm")
    zqkv = _mm2(h, w_qkv, "nn", MXU_DTYPE, name="od_qkv")
    zf = _mm2(h, w_f, "nn", F32, name="od_gate")
    fcol, frow = _fox_prep(zf, b_f, heads, name="od_fox_prep")
    o, lse = _fox_fwd(zqkv, fcol, frow, name="od_fox")
    return _mm2(o, w_out, "nn", F32, name="od_out", add=x), (h, zqkv, zf, fcol, frow, o, lse)


def _odd_bwd(x, g, w_qkv, w_f, b_f, w_out, saved, dxo, dxo_lo, dep=None):
    h, zqkv, zf, fcol, frow, o, lse = saved
    do = _mm2(dxo_lo, w_out, "nt", MXU_DTYPE, name="od_do", dep=dep)
    dw_out = _mm2(o, dxo_lo, "tn", MXU_DTYPE, name="od_dw_out")
    dq, dk, dv, rq, rk = _fox_bwd(zqkv, fcol, frow, lse, o, do, name="od_fox_bwd")
    dfl, db_f = _fox_gate_bwd(rq[:, 0, :], rk[:, 0, :], zf, b_f, name="od_fox_gate_bwd")
    dz = jnp.concatenate([dq, dk, dv], axis=1)
    dw_qkv = _mm2(h, dz, "tn", MXU_DTYPE, name="od_dw_qkv")
    dw_f = _mm2(h, dfl, "tn", MXU_DTYPE, name="od_dw_gate")
    dh = _mm2(dz, w_qkv, "nt", F32, name="od_dh_qkv")
    dh = _mm2(dfl, w_f, "nt", F32, name="od_dh_gate", add=dh)
    dx, dx_lo, dgn = _rms_bwd(x, g, dh, dxo, name="od_norm_bwd")
    return dx, dx_lo, dict(od_norm=dgn, od_b_f=db_f), (dw_qkv, dw_f, dw_out)


def _local_step(x, mem, target, sp, get_w, put_dw):
    b_f = jnp.pad(sp["od_b_f"], ((0, 0), (0, LANES - sp["od_b_f"].shape[1])))
    lb = _lb_fwd(sp["lb_table"], 0, name="lb_fwd")
    fin = sp["final_norm"].reshape(1, -1)
    xn, xm, fn = sp["xa_norm"], sp["xa_mem_norm"], sp["ffn_norm"]
    w_ev = get_w("ev", None)
    x1, s_ev = _even_fwd(x, sp["ev_norm"], lb, w_ev[0], w_ev[1], sp["ev_pool_scale"], sp["ev_hg_norm"], w_ev[2])
    w_xa0 = get_w("xa0", x1)
    x2, s_xa0 = _xa_fwd(x1, mem, xn[0:1], xm[0:1], *w_xa0, 0)
    w_ff0 = get_w("ffn0", x2)
    x3, s_ff0 = _ffn_fwd(x2, fn[0:1], *w_ff0, 0)
    w_qkv, w_f, w_od_out, od_norm = get_w("od", x3)
    x4, s_od = _odd_fwd(x3, od_norm, w_qkv, w_f, b_f, w_od_out)
    w_xa1 = get_w("xa1", x4)
    x5, s_xa1 = _xa_fwd(x4, mem, xn[1:2], xm[1:2], *w_xa1, 1)
    w_ff1 = get_w("ffn1", x5)
    x6, s_ff1 = _ffn_fwd(x5, fn[1:2], *w_ff1, 1)
    loss, dx, dx_lo, d_fin = _loss_head(x6, fin, target, name="loss_head")
    dx, dx_lo, d_ffn1, dw = _ffn_bwd(x5, fn[1:2], *w_ff1, s_ff1, dx, dx_lo, 1)
    tok = put_dw("ffn1", dw)
    dx, dx_lo, d_xa1, d_xm1, dw = _xa_bwd(x4, mem, xn[1:2], xm[1:2], *w_xa1, s_xa1, dx, dx_lo, 1, dep=tok)
    tok = put_dw("xa1", dw)
    dx, dx_lo, d_od, dw = _odd_bwd(x3, od_norm, w_qkv, w_f, b_f, w_od_out, s_od, dx, dx_lo, dep=tok)
    tok = put_dw("od", dw)
    dx, dx_lo, d_ffn0, dw = _ffn_bwd(x2, fn[0:1], *w_ff0, s_ff0, dx, dx_lo, 0, dep=tok)
    tok = put_dw("ffn0", dw)
    dx, dx_lo, d_xa0, d_xm0, dw = _xa_bwd(x1, mem, xn[0:1], xm[0:1], *w_xa0, s_xa0, dx, dx_lo, 0, dep=tok)
    tok = put_dw("xa0", dw)
    dx, _, d_ev, dw = _even_bwd(x, sp["ev_norm"], lb, w_ev[0], w_ev[1], sp["ev_pool_scale"], sp["ev_hg_norm"],
                                w_ev[2], s_ev, dx, dx_lo, dep=tok)
    put_dw("ev", dw)
    small = dict(
        lb_table=_lb_bwd(sp["lb_table"], d_ev["lb"], 0, name="lb_bwd"),
        ev_norm=d_ev["ev_norm"], ev_pool_scale=d_ev["ev_pool_scale"], ev_hg_norm=d_ev["ev_hg_norm"],
        od_norm=d_od["od_norm"], od_b_f=d_od["od_b_f"][:, :sp["od_b_f"].shape[1]],
        xa_norm=jnp.concatenate([d_xa0, d_xa1], axis=0), xa_mem_norm=jnp.concatenate([d_xm0, d_xm1], axis=0),
        ffn_norm=jnp.concatenate([d_ffn0, d_ffn1], axis=0), final_norm=d_fin.reshape(-1))
    return loss, dx, small


_SMALL = ("lb_table", "ev_norm", "ev_pool_scale", "ev_hg_norm", "od_norm", "od_b_f", "xa_norm", "xa_mem_norm",
          "ffn_norm", "final_norm")
_WEIGHTS = ("lb_table", "ev_norm", "ev_w_in", "ev_w_pool", "ev_pool_scale", "ev_hg_norm", "ev_w_out", "od_norm",
            "od_w_in", "od_b_f", "od_w_out", "xa_norm", "xa_mem_norm", "xa_wq", "xa_wkv", "xa_wo", "ffn_norm",
            "ffn_w_gate", "ffn_w_up", "ffn_w_down", "final_norm")


def _lo(a):
    return a.astype(MXU_DTYPE)


def _rows(v):
    flat = v.reshape(-1)
    return jnp.pad(flat, (0, (-flat.shape[0]) % LANES)).reshape(-1, LANES)


def kernel(x, mem, lb_table, ev_norm, ev_w_in, ev_w_pool, ev_pool_scale, ev_hg_norm, ev_w_out, od_norm, od_w_in, od_b_f, od_w_out, xa_norm, xa_mem_norm, xa_wq, xa_wkv, xa_wo, ffn_norm, ffn_w_gate, ffn_w_up, ffn_w_down, final_norm, loss_target, m_lb_table, m_ev_norm, m_ev_w_in, m_ev_w_pool, m_ev_pool_scale, m_ev_hg_norm, m_ev_w_out, m_od_norm, m_od_w_in, m_od_b_f, m_od_w_out, m_xa_norm, m_xa_mem_norm, m_xa_wq, m_xa_wkv, m_xa_wo, m_ffn_norm, m_ffn_w_gate, m_ffn_w_up, m_ffn_w_down, m_final_norm, v_lb_table, v_ev_norm, v_ev_w_in, v_ev_w_pool, v_ev_pool_scale, v_ev_hg_norm, v_ev_w_out, v_od_norm, v_od_w_in, v_od_b_f, v_od_w_out, v_xa_norm, v_xa_mem_norm, v_xa_wq, v_xa_wkv, v_xa_wo, v_ffn_norm, v_ffn_w_gate, v_ffn_w_up, v_ffn_w_down, v_final_norm):
    arg = dict(locals())
    d = x.shape[-1]
    layers = xa_wq.shape[0]
    me = _slot(_my_place())

    n_gate = od_b_f.shape[1]
    shards = dict(ev=[_lo(ev_w_in[0]), _lo(ev_w_pool[0]), _lo(ev_w_out[0])],
                  od=[_lo(od_w_in[0]), _lo(od_w_out[0]), od_norm])
    for l in range(layers):
        shards[f"xa{l}"] = [_lo(xa_wq[l]), _lo(xa_wkv[l]), _lo(xa_wo[l])]
        shards[f"ffn{l}"] = [_lo(ffn_w_gate[l]), _lo(ffn_w_up[l]), _lo(ffn_w_down[l])]
    gathers, tok = {}, None
    for grp in ("ev", "xa0", "ffn0", "od", "xa1", "ffn1"):
        gathers[grp], tok = _push_start(shards[grp], tok, per_peer=False, name=f"gather_{grp}_start")
    last_start = tok

    def get_w(grp, after):
        got = _push_wait(gathers[grp], last_start if after is None else after, name=f"gather_{grp}_wait")
        if grp == "ev":
            w_in, w_pool, w_out = got
            w_pool = jnp.transpose(w_pool, (1, 0, 2, 3)).reshape(w_pool.shape[1], -1, w_pool.shape[3])
            return w_in, w_pool, w_out.reshape(d, d)
        if grp == "od":
            w_in, w_out, nrm = got
            w_in = jnp.transpose(w_in, (1, 0, 2)).reshape(d, -1)
            w_f = jnp.pad(w_in[:, w_in.shape[1] - n_gate:], ((0, 0), (0, LANES - n_gate)))
            return w_in[:, :w_in.shape[1] - n_gate], w_f, w_out.reshape(d, d), nrm.reshape(1, d)
        if grp.startswith("xa"):
            return got[0].reshape(d, d), got[1], got[2].reshape(d, d)
        return tuple(got)

    def row_parts(g):
        return g.reshape(N_DEV, -1, g.shape[-1])

    scatters = {}

    def put_dw(grp, dws):
        if grp == "ev":
            dw_in, dw_pool, dw_out = dws
            gc = dw_pool.shape[1] // N_DEV
            dw_pool = _lo(jnp.transpose(dw_pool.reshape(dw_pool.shape[0], N_DEV, gc, -1), (1, 0, 2, 3)))
            parts = [dw_in, dw_pool, row_parts(dw_out)]
        elif grp == "od":
            dw_qkv, dw_f, dw_out = dws
            dw_in = jnp.concatenate([dw_qkv, dw_f[:, :n_gate]], axis=1)
            parts = [jnp.transpose(dw_in.reshape(d, N_DEV, -1), (1, 0, 2)), row_parts(dw_out)]
        elif grp.startswith("xa"):
            parts = [row_parts(dws[0]), dws[1], row_parts(dws[2])]
        else:
            parts = list(dws)
        scatters[grp], token = _push_start(parts, None, per_peer=True, name=f"scatter_{grp}_start")
        return token

    sp = {k: arg[k] for k in _SMALL if k != "od_norm"}
    loss, dx, small = _local_step(x[0], mem[0], loss_target[0], sp, get_w, put_dw)

    pieces = [_rows(small[k]) for k in _SMALL]
    packed = jnp.concatenate(pieces, axis=0)
    packed = jnp.pad(packed, ((0, (-packed.shape[0]) % 8), (0, 0)))
    total = _all_reduce_rows(packed, name="all_reduce_small")
    small_g, at = {}, 0
    for k, pc in zip(_SMALL, pieces):
        n = small[k].size
        small_g[k] = total[at:at + pc.shape[0]].reshape(-1)[:n].reshape(small[k].shape)
        at += pc.shape[0]
    small_g["od_norm"] = lax.dynamic_slice_in_dim(small_g["od_norm"], me * od_norm.shape[1], od_norm.shape[1], axis=1)

    res = {}
    for k in _SMALL:
        w, m, v = arg[k], arg["m_" + k], arg["v_" + k]
        shp = (1, 1, w.shape[0]) if w.ndim == 1 else (1,) + w.shape
        out = _adamw(w.reshape(shp), m.reshape(shp), v.reshape(shp), small_g[k].reshape(shp), name=f"adamw_{k}")
        res[k] = [o.reshape(w.shape) for o in out]
    members = dict(ev=("ev_w_in", "ev_w_pool", "ev_w_out"), od=("od_w_in", "od_w_out"),
                   xa=("xa_wq", "xa_wkv", "xa_wo"), ffn=("ffn_w_gate", "ffn_w_up", "ffn_w_down"))
    after, stacked = dx, {}
    for grp in ("ffn1", "xa1", "od", "ffn0", "xa0", "ev"):
        got = _push_wait(scatters[grp], after, name=f"scatter_{grp}_wait")
        kind = grp.rstrip("01")
        for k, parts in zip(members[kind], got):
            w, m, v = arg[k], arg["m_" + k], arg["v_" + k]
            if w.shape[0] == 1:
                shp = (1, -1, w.shape[-1])
                out = _adamw(w.reshape(shp), m.reshape(shp), v.reshape(shp), parts.reshape(N_DEV, -1, w.shape[-1]),
                             name=f"adamw_{k}")
            else:
                out = _adamw(w, m, v, parts, name=f"adamw_{k}{grp[-1]}", layer=int(grp[-1]), prev=stacked.get(k))
                stacked[k] = out
            res[k] = [o.reshape(w.shape) for o in out]
            after = out[3][:1, :8, :LANES]

    loss = lax.psum(loss[0, 0], ("x", "y", "c"))
    outs = [loss, dx[None]]
    for j in range(4):
        outs += [res[k][j] for k in _WEIGHTS]
    return tuple(outs)
```

```python
import functools

import jax
import jax.numpy as jnp
from jax import lax
from jax.experimental import pallas as pl
from jax.experimental.pallas import tpu as pltpu

F32 = jnp.float32
MXU_DTYPE = jnp.bfloat16
EPS = 1e-6
N_DEV = 8
V7X_VMEM_BYTES = 64 * 1024 * 1024
VMEM_LIMIT_BYTES = V7X_VMEM_BYTES - 8 * 1024 * 1024
LANES = 128
HIGHEST = lax.Precision.HIGHEST
MESH = pl.DeviceIdType.MESH

HG_HEAD = 128
HG_CHUNK = 16
FOX_HEAD = 128
XA_HEADS = 4
POOL_GROUPS = 4

ADAM_LR = 0.001
ADAM_B1 = 0.9
ADAM_B2 = 0.999
ADAM_EPS = 1e-08
ADAM_WD = 0.01
ADAM_STEP = 10

_NN = ((1,), (0,))
_NT = ((1,), (1,))
_TN = ((0,), (0,))


def _dot(a, b, dims):
    return lax.dot_general(a.astype(MXU_DTYPE), b.astype(MXU_DTYPE), (dims, ((), ())),
                           preferred_element_type=F32)


def _dot_f32(a, b, dims):
    return lax.dot_general(a, b, (dims, ((), ())), preferred_element_type=F32, precision=HIGHEST)


def _cp(*sem):
    return pltpu.CompilerParams(dimension_semantics=sem, vmem_limit_bytes=VMEM_LIMIT_BYTES)


def _tile(n, pref, align=LANES):
    if n <= pref:
        return n
    t = (pref // align) * align
    while t >= align:
        if n % t == 0:
            return t
        t -= align
    return n


def _sigmoid(x):
    return jax.nn.sigmoid(x)


def _mm(a, b, mode, out_dtype, *, name, add=None, dep=None, reduce_b=False, b_split=False, o_split=False,
        tm=1024, tn=1024, tk=2048):
    ba, bb = a.shape[0], b.shape[0]
    if mode == "tn":
        kdim, m = a.shape[1], a.shape[2]
        tk = 2 * tk
    else:
        m, kdim = a.shape[1], a.shape[2]
    if b_split:
        s_cnt, b_rows, w = b.shape
        if mode == "nt":
            n = b_rows
            assert kdim == s_cnt * w
            tk = w
        else:
            n = s_cnt * w
            assert b_rows == kdim
            tn = w
        nb = ba
    else:
        n = b.shape[1] if mode == "nt" else b.shape[2]
        nb = max(ba, bb)
    if not (b_split and mode != "nt"):
        tn = _tile(n, tn)
    if not (b_split and mode == "nt"):
        tk = _tile(kdim, tk)
    tm = _tile(m, tm)
    assert m % tm == 0 and n % tn == 0 and kdim % tk == 0, (name, m, n, kdim, tm, tn, tk)
    nk = kdim // tk
    if reduce_b:
        grid = (m // tm, n // tn, nb, nk)
        unpack = lambda i, j, bi, k: (bi, i, j, k)
        sem = ("parallel", "parallel", "arbitrary", "arbitrary")
        nred = nb * nk
    else:
        grid = (nb, m // tm, n // tn, nk)
        unpack = lambda bi, i, j, k: (bi, i, j, k)
        sem = ("parallel", "parallel", "parallel", "arbitrary")
        nred = nk

    def a_map(*g):
        bi, i, j, k = unpack(*g)
        ab = bi if ba > 1 else 0
        return (ab, k, i) if mode == "tn" else (ab, i, k)

    def b_map(*g):
        bi, i, j, k = unpack(*g)
        if b_split:
            return (k, j, 0) if mode == "nt" else (j, k, 0)
        bq = bi if bb > 1 else 0
        return (bq, j, k) if mode == "nt" else (bq, k, j)

    def o_map(*g):
        bi, i, j, k = unpack(*g)
        if o_split:
            return (j, i, 0)
        return (0 if reduce_b else bi, i, j)

    a_blk = (1, tk, tm) if mode == "tn" else (1, tm, tk)
    b_blk = (1, tn, tk) if mode == "nt" else (1, tk, tn)
    dims = {"nn": _NN, "nt": _NT, "tn": _TN}[mode]
    has_add = add is not None

    def body(*refs):
        a_ref, b_ref = refs[:2]
        if has_add:
            add_ref = refs[2]
        if nred == 1:
            o_ref = refs[-1]
            r = _dot(a_ref[0], b_ref[0], dims)
            if has_add:
                r = r + add_ref[0].astype(F32)
            o_ref[0] = r.astype(o_ref.dtype)
            return
        o_ref, acc_ref = refs[-2:]
        if reduce_b:
            step = pl.program_id(2) * nk + pl.program_id(3)
        else:
            step = pl.program_id(3)

        @pl.when(step == 0)
        def _():
            acc_ref[...] = _dot(a_ref[0], b_ref[0], dims)

        @pl.when(step > 0)
        def _():
            acc_ref[...] += _dot(a_ref[0], b_ref[0], dims)

        @pl.when(step == nred - 1)
        def _():
            r = acc_ref[...]
            if has_add:
                r = r + add_ref[0].astype(F32)
            o_ref[0] = r.astype(o_ref.dtype)

    in_specs = [pl.BlockSpec(a_blk, a_map), pl.BlockSpec(b_blk, b_map)]
    operands = [a, b]
    if has_add:
        in_specs.append(pl.BlockSpec((1, tm, tn), o_map))
        operands.append(add)
    if dep is not None:
        in_specs.append(pl.BlockSpec(memory_space=pl.ANY))
        operands.append(dep)
    if o_split:
        out_shape = jax.ShapeDtypeStruct((n // tn, m, tn), out_dtype)
    else:
        out_shape = jax.ShapeDtypeStruct((1 if reduce_b else nb, m, n), out_dtype)
    return pl.pallas_call(
        body, grid=grid, in_specs=in_specs, out_specs=pl.BlockSpec((1, tm, tn), o_map),
        out_shape=out_shape, scratch_shapes=[] if nred == 1 else [pltpu.VMEM((tm, tn), F32)],
        compiler_params=_cp(*sem), name=name)(*operands)


def _mm2(a, b, mode, out_dtype, *, name, add=None, **kw):
    b3 = b if kw.get("b_split") else b[None]
    r = _mm(a[None], b3, mode, out_dtype, name=name, add=None if add is None else add[None], **kw)
    return r if kw.get("o_split") else r[0]


def _rms_fwd(x, g, *, name, tb=512):
    t, d = x.shape
    tb = min(tb, t)

    def body(x_ref, g_ref, o_ref):
        xv = x_ref[...]
        r = lax.rsqrt(jnp.mean(xv * xv, axis=-1, keepdims=True) + EPS)
        o_ref[...] = (xv * r * g_ref[...]).astype(o_ref.dtype)

    return pl.pallas_call(
        body, grid=(t // tb,),
        in_specs=[pl.BlockSpec((tb, d), lambda i: (i, 0)), pl.BlockSpec((1, d), lambda i: (0, 0))],
        out_specs=pl.BlockSpec((tb, d), lambda i: (i, 0)),
        out_shape=jax.ShapeDtypeStruct((t, d), MXU_DTYPE), compiler_params=_cp("parallel"), name=name)(x, g)


def _rms_bwd(x, g, dh, dres, *, name, tb=512):
    t, d = x.shape
    tb = min(tb, t)
    has_res = dres is not None

    def body(*refs):
        if has_res:
            x_ref, g_ref, dh_ref, dres_ref, dx_ref, dxl_ref, dg_ref = refs
        else:
            x_ref, g_ref, dh_ref, dx_ref, dxl_ref, dg_ref = refs
        xv = x_ref[...]
        r = lax.rsqrt(jnp.mean(xv * xv, axis=-1, keepdims=True) + EPS)
        xh = xv * r
        dhv = dh_ref[...].astype(F32)

        @pl.when(pl.program_id(0) == 0)
        def _():
            dg_ref[...] = jnp.zeros_like(dg_ref)

        dg_ref[...] += jnp.sum(dhv * xh, axis=0, keepdims=True)
        dxh = dhv * g_ref[...]
        dx = r * (dxh - xh * jnp.mean(dxh * xh, axis=-1, keepdims=True))
        if has_res:
            dx = dx + dres_ref[...]
        dx_ref[...] = dx
        dxl_ref[...] = dx.astype(dxl_ref.dtype)

    row = pl.BlockSpec((tb, d), lambda i: (i, 0))
    vec = pl.BlockSpec((1, d), lambda i: (0, 0))
    operands = [x, g, dh] + ([dres] if has_res else [])
    return pl.pallas_call(
        body, grid=(t // tb,), in_specs=[row, vec, row] + ([row] if has_res else []),
        out_specs=[row, row, vec],
        out_shape=[jax.ShapeDtypeStruct((t, d), F32), jax.ShapeDtypeStruct((t, d), MXU_DTYPE),
                   jax.ShapeDtypeStruct((1, d), F32)],
        compiler_params=_cp("arbitrary"), name=name)(*operands)


def _loss_head(x, g, target, *, name, tb=512):
    t, d = x.shape
    tb = min(tb, t)

    def body(x_ref, g_ref, t_ref, loss_ref, dx_ref, dxl_ref, dg_ref):
        xv = x_ref[...]
        r = lax.rsqrt(jnp.mean(xv * xv, axis=-1, keepdims=True) + EPS)
        xh = xv * r
        gv = g_ref[...]
        err = xh * gv - t_ref[...]

        @pl.when(pl.program_id(0) == 0)
        def _():
            dg_ref[...] = jnp.zeros_like(dg_ref)
            loss_ref[...] = jnp.zeros_like(loss_ref)

        row_loss = jnp.mean(err * err, axis=-1, keepdims=True)
        loss_ref[...] += 0.5 * jnp.sum(row_loss, axis=0, keepdims=True)
        dy = err * (1.0 / d)
        dg_ref[...] += jnp.sum(dy * xh, axis=0, keepdims=True)
        dxh = dy * gv
        dx = r * (dxh - xh * jnp.mean(dxh * xh, axis=-1, keepdims=True))
        dx_ref[...] = dx
        dxl_ref[...] = dx.astype(dxl_ref.dtype)

    row = pl.BlockSpec((tb, d), lambda i: (i, 0))
    vec = pl.BlockSpec((1, d), lambda i: (0, 0))
    return pl.pallas_call(
        body, grid=(t // tb,), in_specs=[row, vec, row],
        out_specs=[pl.BlockSpec((1, 1), lambda i: (0, 0)), row, row, vec],
        out_shape=[jax.ShapeDtypeStruct((1, 1), F32), jax.ShapeDtypeStruct((t, d), F32),
                   jax.ShapeDtypeStruct((t, d), MXU_DTYPE), jax.ShapeDtypeStruct((1, d), F32)],
        compiler_params=_cp("arbitrary"), name=name)(x, g, target)


def _ffn_up(h, wg, wu, *, name, tb=512):
    t, d = h.shape
    s, _, f = wg.shape
    tb = min(tb, t)

    def body(h_ref, wg_ref, wu_ref, g_ref, u_ref, a_ref):
        hv = h_ref[...]
        gv = _dot(hv, wg_ref[0], _NN)
        uv = _dot(hv, wu_ref[0], _NN)
        g_ref[0] = gv
        u_ref[0] = uv
        a_ref[0] = (gv * _sigmoid(gv) * uv).astype(a_ref.dtype)

    wspec = pl.BlockSpec((1, d, f), lambda j, i: (j, 0, 0))
    ospec = pl.BlockSpec((1, tb, f), lambda j, i: (j, i, 0))
    return pl.pallas_call(
        body, grid=(s, t // tb),
        in_specs=[pl.BlockSpec((tb, d), lambda j, i: (i, 0)), wspec, wspec],
        out_specs=[ospec, ospec, ospec],
        out_shape=[jax.ShapeDtypeStruct((s, t, f), F32), jax.ShapeDtypeStruct((s, t, f), F32),
                   jax.ShapeDtypeStruct((s, t, f), MXU_DTYPE)],
        compiler_params=_cp("parallel", "parallel"), name=name)(h, wg, wu)


def _ffn_dact(dy, wd, gate, up, *, name, tb=512):
    t, d = dy.shape
    s, f, _ = wd.shape
    tb = min(tb, t)

    def body(dy_ref, wd_ref, g_ref, u_ref, dg_ref, du_ref):
        da = _dot(dy_ref[...], wd_ref[0], _NT)
        gv = g_ref[0]
        sg = _sigmoid(gv)
        du_ref[0] = (da * gv * sg).astype(du_ref.dtype)
        dg_ref[0] = (da * u_ref[0] * (sg * (1.0 + gv * (1.0 - sg)))).astype(dg_ref.dtype)

    aspec = pl.BlockSpec((1, tb, f), lambda j, i: (j, i, 0))
    return pl.pallas_call(
        body, grid=(s, t // tb),
        in_specs=[pl.BlockSpec((tb, d), lambda j, i: (i, 0)),
                  pl.BlockSpec((1, f, d), lambda j, i: (j, 0, 0)), aspec, aspec],
        out_specs=[aspec, aspec],
        out_shape=[jax.ShapeDtypeStruct((s, t, f), MXU_DTYPE), jax.ShapeDtypeStruct((s, t, f), MXU_DTYPE)],
        compiler_params=_cp("parallel", "parallel"), name=name)(dy, wd, gate, up)


def _xattn_fwd(q, kv, *, name, tb=512):
    t, d = q.shape
    m = kv.shape[0]
    hd = d // XA_HEADS
    tb = min(tb, t)
    scale = hd ** -0.5

    def body(q_ref, kv_ref, o_ref):
        for hh in range(XA_HEADS):
            cs = slice(hh * hd, (hh + 1) * hd)
            s = _dot(q_ref[:, cs], kv_ref[:, cs], _NT) * scale
            s = s - jnp.max(s, axis=-1, keepdims=True)
            e = jnp.exp(s)
            p = e / jnp.sum(e, axis=-1, keepdims=True)
            o_ref[:, cs] = _dot(p, kv_ref[:, d + hh * hd:d + (hh + 1) * hd], _NN).astype(o_ref.dtype)

    return pl.pallas_call(
        body, grid=(t // tb,),
        in_specs=[pl.BlockSpec((tb, d), lambda i: (i, 0)), pl.BlockSpec((m, 2 * d), lambda i: (0, 0))],
        out_specs=pl.BlockSpec((tb, d), lambda i: (i, 0)),
        out_shape=jax.ShapeDtypeStruct((t, d), MXU_DTYPE), compiler_params=_cp("parallel"), name=name)(q, kv)


def _xattn_bwd(q, kv, do, *, name, tb=512):
    t, d = q.shape
    m = kv.shape[0]
    hd = d // XA_HEADS
    tb = min(tb, t)
    scale = hd ** -0.5

    def body(q_ref, kv_ref, do_ref, dq_ref, dkv_ref):
        @pl.when(pl.program_id(0) == 0)
        def _():
            dkv_ref[...] = jnp.zeros_like(dkv_ref)

        for hh in range(XA_HEADS):
            cs = slice(hh * hd, (hh + 1) * hd)
            vs = slice(d + hh * hd, d + (hh + 1) * hd)
            qv, kk, vv, dov = q_ref[:, cs], kv_ref[:, cs], kv_ref[:, vs], do_ref[:, cs]
            s = _dot(qv, kk, _NT) * scale
            s = s - jnp.max(s, axis=-1, keepdims=True)
            e = jnp.exp(s)
            p = e / jnp.sum(e, axis=-1, keepdims=True)
            dkv_ref[:, vs] += _dot(p, dov, _TN)
            dp = _dot(dov, vv, _NT)
            ds = p * (dp - jnp.sum(p * dp, axis=-1, keepdims=True)) * scale
            dq_ref[:, cs] = _dot(ds, kk, _NN).astype(dq_ref.dtype)
            dkv_ref[:, cs] += _dot(ds, qv, _TN)

    row = pl.BlockSpec((tb, d), lambda i: (i, 0))
    full = pl.BlockSpec((m, 2 * d), lambda i: (0, 0))
    return pl.pallas_call(
        body, grid=(t // tb,), in_specs=[row, full, row], out_specs=[row, full],
        out_shape=[jax.ShapeDtypeStruct((t, d), MXU_DTYPE), jax.ShapeDtypeStruct((m, 2 * d), F32)],
        compiler_params=_cp("arbitrary"), name=name)(q, kv, do)


def _pool_window_stats(u, gi, reverse):
    t = u.shape[0]
    row = lax.broadcasted_iota(jnp.int32, u.shape, 0)
    s = u
    for j in range(POOL_GROUPS):
        sh = 1 << j
        if reverse:
            rolled = jnp.where(row < t - sh, pltpu.roll(s, t - sh, axis=0), 0.0)
        else:
            rolled = jnp.where(row >= sh, pltpu.roll(s, sh, axis=0), 0.0)
        s = jnp.where(j <= gi, s + rolled, s)
    return s, row


def _pool_fwd(z, w_pool, scale, *, name):
    t = z.shape[0]
    g_cnt, c, _ = w_pool.shape

    def body(z_ref, w_ref, s_ref, o_ref):
        gi = pl.program_id(0)
        u = z_ref[...]
        win, row = _pool_window_stats(u, gi, False)
        cnt = jnp.minimum(row + 1, lax.shift_left(jnp.int32(2), gi)).astype(F32)
        p = win / cnt - u
        o_ref[...] = (_dot(p, w_ref[0], _NN) * s_ref[...]).astype(o_ref.dtype)

    return pl.pallas_call(
        body, grid=(g_cnt,),
        in_specs=[pl.BlockSpec((t, c), lambda g: (0, g)), pl.BlockSpec((1, c, c), lambda g: (g, 0, 0)),
                  pl.BlockSpec((1, c), lambda g: (0, g))],
        out_specs=pl.BlockSpec((t, c), lambda g: (0, g)),
        out_shape=jax.ShapeDtypeStruct((t, 2 * g_cnt * c), MXU_DTYPE),
        compiler_params=_cp("parallel"), name=name)(z, w_pool, scale)


def _pool_bwd(z, w_pool, scale, dycat, *, name):
    t = z.shape[0]
    g_cnt, c, _ = w_pool.shape

    def body(z_ref, w_ref, s_ref, dy_ref, du_ref, dw_ref, ds_ref):
        gi = pl.program_id(0)
        u = z_ref[...]
        win, row = _pool_window_stats(u, gi, False)
        cnt = jnp.minimum(row + 1, lax.shift_left(jnp.int32(2), gi)).astype(F32)
        p = win / cnt - u
        y = _dot(p, w_ref[0], _NN)
        dya = dy_ref[...].astype(F32)
        ds_ref[...] = jnp.sum(dya * y, axis=0, keepdims=True)
        dy = dya * s_ref[...]
        dw_ref[0] = _dot(p, dy, _TN)
        dp = _dot(dy, w_ref[0], _NT)
        back, _ = _pool_window_stats(dp / cnt, gi, True)
        du_ref[...] = (back - dp).astype(du_ref.dtype)

    col = pl.BlockSpec((t, c), lambda g: (0, g))
    return pl.pallas_call(
        body, grid=(g_cnt,),
        in_specs=[col, pl.BlockSpec((1, c, c), lambda g: (g, 0, 0)), pl.BlockSpec((1, c), lambda g: (0, g)), col],
        out_specs=[col, pl.BlockSpec((1, c, c), lambda g: (g, 0, 0)), pl.BlockSpec((1, c), lambda g: (0, g))],
        out_shape=[jax.ShapeDtypeStruct((t, g_cnt * c), MXU_DTYPE), jax.ShapeDtypeStruct((g_cnt, c, c), F32),
                   jax.ShapeDtypeStruct((1, g_cnt * c), F32)],
        compiler_params=_cp("parallel"), name=name)(z, w_pool, scale, dycat)


def _chunk_tri(lower):
    r = lax.broadcasted_iota(jnp.int32, (LANES, LANES), 0)
    c = lax.broadcasted_iota(jnp.int32, (LANES, LANES), 1)
    same = (r // HG_CHUNK) == (c // HG_CHUNK)
    return jnp.where(same & ((c <= r) if lower else (c >= r)), 1.0, 0.0).astype(F32)


def _hgrn_prepare(q_ref, f_ref, lb_ref, qh_s, k_s, b_s, qt_s, kt_s, gl_s):
    tb = q_ref.shape[0]
    lb = lb_ref[...]
    sg = _sigmoid(f_ref[...])
    f = lb + (1.0 - lb) * sg
    logf = jnp.log(f)
    qv = q_ref[...]
    qh = qv * _sigmoid(qv) * (HG_HEAD ** -0.5)
    tri = _chunk_tri(True)
    for r in range(tb // LANES):
        rows = slice(r * LANES, (r + 1) * LANES)
        b_s[rows, :] = _dot_f32(tri, logf[rows, :], _NN)
    b = b_s[...]
    b3 = b.reshape(tb // HG_CHUNK, HG_CHUNK, HG_HEAD)
    bl = b3[:, HG_CHUNK - 1:HG_CHUNK, :]
    k = 1.0 - f
    qh_s[...] = qh
    k_s[...] = k
    qt_s[...] = qh * jnp.exp(b)
    kt_s[...] = k * jnp.exp(bl - b3).reshape(tb, HG_HEAD)
    gl_s[...] = jnp.exp(jnp.broadcast_to(bl, b3.shape)).reshape(tb, HG_HEAD)
    return sg, f


def _hgrn_intra(qh, kk, bq, rows_a, rows_b):
    ones = jnp.ones((HG_HEAD, HG_HEAD), MXU_DTYPE)
    es, stack_a, stack_b = [], [], []
    for s in range(HG_CHUNK):
        e = jnp.exp(jnp.minimum(bq - bq[s:s + 1, :], 0.0))
        es.append(e)
        stack_a.append(qh * e * kk[s:s + 1, :])
        if rows_a is not None:
            stack_b.append(rows_a * rows_b[s:s + 1, :])
    a_rep = _dot(jnp.concatenate(stack_a, axis=0), ones, _NN)
    d_rep = _dot(jnp.concatenate(stack_b, axis=0), ones, _NN) if rows_a is not None else None
    return es, a_rep, d_rep


def _hgrn_fwd(z, lb, hg_norm, ycat, *, name, tb=512):
    t = z.shape[0]
    mix_b = lb.shape[1]
    heads = mix_b // HG_HEAD
    off = (z.shape[1] - 4 * mix_b) // HG_HEAD
    tb = min(tb, t)
    ncb = tb // HG_CHUNK

    def body(q_ref, f_ref, i_ref, g_ref, lb_ref, hn_ref, ycat_in, y_ref, o_ref, st_ref,
             state, qh_s, k_s, b_s, qt_s, kt_s, gl_s, o_s):
        del ycat_in

        @pl.when(pl.program_id(1) == 0)
        def _():
            state[...] = jnp.zeros_like(state)

        _hgrn_prepare(q_ref, f_ref, lb_ref, qh_s, k_s, b_s, qt_s, kt_s, gl_s)
        row = lax.broadcasted_iota(jnp.int32, (HG_CHUNK, HG_HEAD), 0)

        def chunk(c, carry):
            rows = pl.ds(pl.multiple_of(c * HG_CHUNK, HG_CHUNK), HG_CHUNK)
            st = state[...]
            st_ref[0, c] = st
            vv = i_ref[rows, :]
            o = _dot(qt_s[rows, :], st, _NT)
            _, a_rep, _ = _hgrn_intra(qh_s[rows, :], k_s[rows, :], b_s[rows, :], None, None)
            for s in range(HG_CHUNK):
                o = o + jnp.where(row >= s, a_rep[s * HG_CHUNK:(s + 1) * HG_CHUNK, :] * vv[s:s + 1, :], 0.0)
            o_s[rows, :] = o
            state[...] = st * gl_s[rows, :][0:1, :] + _dot(vv, kt_s[rows, :], _TN)
            return carry

        lax.fori_loop(0, ncb, chunk, 0, unroll=2)
        o = o_s[...]
        o_ref[...] = o
        r = lax.rsqrt(jnp.mean(o * o, axis=-1, keepdims=True) + EPS)
        gv = g_ref[...]
        y_ref[...] = (o * r * hn_ref[...] * (gv * _sigmoid(gv))).astype(y_ref.dtype)

    def zcol(kind):
        return pl.BlockSpec((tb, HG_HEAD), lambda h, i: (i, off + kind * heads + h))

    scratch = [pltpu.VMEM((HG_HEAD, HG_HEAD), F32)] + [pltpu.VMEM((tb, HG_HEAD), F32)] * 7
    return pl.pallas_call(
        body, grid=(heads, t // tb),
        in_specs=[zcol(0), zcol(1), zcol(2), zcol(3), pl.BlockSpec((1, HG_HEAD), lambda h, i: (0, h)),
                  pl.BlockSpec((1, HG_HEAD), lambda h, i: (0, 0)), pl.BlockSpec(memory_space=pl.ANY)],
        out_specs=[pl.BlockSpec((tb, HG_HEAD), lambda h, i: (i, heads + h)),
                   pl.BlockSpec((tb, HG_HEAD), lambda h, i: (i, h)),
                   pl.BlockSpec((1, ncb, HG_HEAD, HG_HEAD), lambda h, i: (h, i, 0, 0))],
        out_shape=[jax.ShapeDtypeStruct(ycat.shape, ycat.dtype), jax.ShapeDtypeStruct((t, mix_b), F32),
                   jax.ShapeDtypeStruct((heads, t // HG_CHUNK, HG_HEAD, HG_HEAD), F32)],
        scratch_shapes=scratch, input_output_aliases={6: 0},
        compiler_params=_cp("parallel", "arbitrary"), name=name)(z, z, z, z, lb, hg_norm, ycat)


def _hgrn_bwd(z, lb, hg_norm, o_raw, states, dycat, *, name, tb=512):
    t = z.shape[0]
    mix_b = lb.shape[1]
    heads = mix_b // HG_HEAD
    off = (z.shape[1] - 4 * mix_b) // HG_HEAD
    tb = min(tb, t)
    ncb = tb // HG_CHUNK
    nt = t // tb

    def body(q_ref, f_ref, i_ref, g_ref, lb_ref, hn_ref, o_ref, st_ref, dy_ref,
             dq_ref, dfl_ref, di_ref, dg_ref, dlb_ref, dhn_ref,
             dstate, qh_s, k_s, b_s, qt_s, kt_s, gl_s, do_s, dqh_s, dk_s, db_s):
        first = pl.program_id(1) == 0

        @pl.when(first)
        def _():
            dstate[...] = jnp.zeros_like(dstate)
            dlb_ref[...] = jnp.zeros_like(dlb_ref)

        @pl.when(first & (pl.program_id(0) == 0))
        def _():
            dhn_ref[...] = jnp.zeros_like(dhn_ref)

        sg, f = _hgrn_prepare(q_ref, f_ref, lb_ref, qh_s, k_s, b_s, qt_s, kt_s, gl_s)
        o = o_ref[...]
        r = lax.rsqrt(jnp.mean(o * o, axis=-1, keepdims=True) + EPS)
        oh = o * r
        gv = g_ref[...]
        sgg = _sigmoid(gv)
        dy = dy_ref[...].astype(F32)
        hn = hn_ref[...]
        dg_ref[...] = (dy * oh * hn * (sgg * (1.0 + gv * (1.0 - sgg)))).astype(dg_ref.dtype)
        don = dy * (gv * sgg)
        dhn_ref[...] += jnp.sum(don * oh, axis=0, keepdims=True)
        doh = don * hn
        do_s[...] = r * (doh - oh * jnp.mean(doh * oh, axis=-1, keepdims=True))
        row = lax.broadcasted_iota(jnp.int32, (HG_CHUNK, HG_HEAD), 0)

        def chunk(ci, carry):
            c = ncb - 1 - ci
            rows = pl.ds(pl.multiple_of(c * HG_CHUNK, HG_CHUNK), HG_CHUNK)
            st_prev = st_ref[0, c]
            dst = dstate[...]
            qh, kk, bq, vv = qh_s[rows, :], k_s[rows, :], b_s[rows, :], i_ref[rows, :]
            qt, kt, doo = qt_s[rows, :], kt_s[rows, :], do_s[rows, :]
            gl = gl_s[rows, :][0:1, :]
            es, a_rep, d_rep = _hgrn_intra(qh, kk, bq, doo, vv)
            dqh = jnp.exp(bq) * _dot(doo, st_prev, _NN)
            dk = jnp.exp(bq[HG_CHUNK - 1:HG_CHUNK, :] - bq) * _dot(vv, dst, _NN)
            dv = _dot(kt, dst, _NT)
            for s in range(HG_CHUNK):
                blk = slice(s * HG_CHUNK, (s + 1) * HG_CHUNK)
                wgt = jnp.where(row >= s, d_rep[blk, :] * es[s], 0.0)
                dqh = dqh + wgt * kk[s:s + 1, :]
                dk = dk + jnp.where(row == s, jnp.sum(wgt * qh, axis=0, keepdims=True), 0.0)
                dv_row = jnp.sum(jnp.where(row >= s, a_rep[blk, :] * doo, 0.0), axis=0, keepdims=True)
                dv = dv + jnp.where(row == s, dv_row, 0.0)
            st_next = st_prev * gl + _dot(vv, kt, _TN)
            db = qh * dqh - kk * dk
            db = db + jnp.where(row == HG_CHUNK - 1, jnp.sum(st_next * dst, axis=0, keepdims=True), 0.0)
            dstate[...] = dst * gl + _dot(doo, qt, _TN)
            dqh_s[rows, :] = dqh
            dk_s[rows, :] = dk
            db_s[rows, :] = db
            di_ref[rows, :] = dv.astype(di_ref.dtype)
            return carry

        lax.fori_loop(0, ncb, chunk, 0, unroll=2)
        tri = _chunk_tri(False)
        lb_v = lb_ref[...]
        qv = q_ref[...]
        sgq = _sigmoid(qv)
        dq_ref[...] = (dqh_s[...] * (HG_HEAD ** -0.5) * (sgq * (1.0 + qv * (1.0 - sgq)))).astype(dq_ref.dtype)
        dlb = jnp.zeros((1, HG_HEAD), F32)
        for rr in range(tb // LANES):
            rws = slice(rr * LANES, (rr + 1) * LANES)
            dlogf = _dot_f32(tri, db_s[rws, :], _NN)
            df = dlogf / f[rws, :] - dk_s[rws, :]
            sgr = sg[rws, :]
            dfl_ref[rws, :] = (df * (1.0 - lb_v) * sgr * (1.0 - sgr)).astype(dfl_ref.dtype)
            dlb = dlb + jnp.sum(df * (1.0 - sgr), axis=0, keepdims=True)
        dlb_ref[...] += dlb

    def zcol(kind):
        return pl.BlockSpec((tb, HG_HEAD), lambda h, i: (nt - 1 - i, off + kind * heads + h))

    hcol = pl.BlockSpec((tb, HG_HEAD), lambda h, i: (nt - 1 - i, h))
    scratch = [pltpu.VMEM((HG_HEAD, HG_HEAD), F32)] + [pltpu.VMEM((tb, HG_HEAD), F32)] * 10
    out = jax.ShapeDtypeStruct((t, mix_b), MXU_DTYPE)
    return pl.pallas_call(
        body, grid=(heads, nt),
        in_specs=[zcol(0), zcol(1), zcol(2), zcol(3), pl.BlockSpec((1, HG_HEAD), lambda h, i: (0, h)),
                  pl.BlockSpec((1, HG_HEAD), lambda h, i: (0, 0)), hcol,
                  pl.BlockSpec((1, ncb, HG_HEAD, HG_HEAD), lambda h, i: (h, nt - 1 - i, 0, 0)),
                  pl.BlockSpec((tb, HG_HEAD), lambda h, i: (nt - 1 - i, heads + h))],
        out_specs=[hcol, hcol, hcol, hcol, pl.BlockSpec((1, HG_HEAD), lambda h, i: (0, h)),
                   pl.BlockSpec((1, HG_HEAD), lambda h, i: (0, 0))],
        out_shape=[out, out, out, out, jax.ShapeDtypeStruct((1, mix_b), F32),
                   jax.ShapeDtypeStruct((1, HG_HEAD), F32)],
        scratch_shapes=scratch, compiler_params=_cp("arbitrary", "arbitrary"),
        name=name)(z, z, z, z, lb, hg_norm, o_raw, states, dycat)


def _lb_fwd(lb_table, layer, *, name):
    rows, width = lb_table.shape

    def body(t_ref, o_ref):
        tv = t_ref[...]
        e = jnp.exp(tv - jnp.max(tv, axis=0, keepdims=True))
        sm = e / jnp.sum(e, axis=0, keepdims=True)
        o_ref[...] = jnp.sum(sm[1:layer + 2, :], axis=0, keepdims=True)

    return pl.pallas_call(body, out_shape=jax.ShapeDtypeStruct((1, width), F32), name=name)(lb_table)


def _lb_bwd(lb_table, dlb, layer, *, name):
    rows, width = lb_table.shape

    def body(t_ref, d_ref, o_ref):
        tv = t_ref[...]
        e = jnp.exp(tv - jnp.max(tv, axis=0, keepdims=True))
        sm = e / jnp.sum(e, axis=0, keepdims=True)
        ridx = lax.broadcasted_iota(jnp.int32, sm.shape, 0)
        dsm = jnp.where((ridx >= 1) & (ridx <= layer + 1), d_ref[...], 0.0)
        o_ref[...] = sm * (dsm - jnp.sum(sm * dsm, axis=0, keepdims=True))

    return pl.pallas_call(body, out_shape=jax.ShapeDtypeStruct((rows, width), F32), name=name)(lb_table, dlb)


FOX_BLOCK = 512


def _fox_prep(zf, b_f, *, name, blk=256):
    t = zf.shape[0]

    def body(z_ref, b_ref, fc_ref):
        r = lax.broadcasted_iota(jnp.int32, (blk, blk), 0)
        c = lax.broadcasted_iota(jnp.int32, (blk, blk), 1)
        tri = jnp.where(c <= r, 1.0, 0.0).astype(F32)
        carry = jnp.zeros((1, LANES), F32)
        for j in range(t // blk):
            rows = slice(j * blk, (j + 1) * blk)
            ls = jax.nn.log_sigmoid(z_ref[rows, :] + b_ref[...])
            fb = _dot_f32(tri, ls, _NN) + carry
            carry = fb[blk - 1:blk, :]
            fc_ref[rows, :] = fb

    return pl.pallas_call(
        body, out_shape=jax.ShapeDtypeStruct((t, LANES), F32),
        compiler_params=pltpu.CompilerParams(vmem_limit_bytes=VMEM_LIMIT_BYTES), name=name)(zf, b_f)


def _fox_head_column(fc_ref, fk_s, head):
    lane = lax.broadcasted_iota(jnp.int32, fc_ref.shape, 1)
    fk_s[...] = jnp.sum(jnp.where(lane == head, fc_ref[...], 0.0), axis=1, keepdims=True)


def _fox_scores(k_blk, q_blk, fk_blk, diagonal):
    s = _dot(k_blk, q_blk, _NT) * (FOX_HEAD ** -0.5) - fk_blk
    if diagonal:
        key = lax.broadcasted_iota(jnp.int32, s.shape, 0)
        qry = lax.broadcasted_iota(jnp.int32, s.shape, 1)
        s = jnp.where(key <= qry, s, -jnp.inf)
    return s


def _fox_fwd(zqkv, fcol, *, name):
    t = zqkv.shape[0]
    d = zqkv.shape[1] // 3
    heads = d // FOX_HEAD
    blk = min(FOX_BLOCK, t)
    nq = t // blk

    def body(q_ref, k_ref, v_ref, fc_ref, o_ref, lse_ref, fk_s):
        _fox_head_column(fc_ref, fk_s, pl.program_id(0))

        def q_block(qi, carry):
            qrows = pl.ds(pl.multiple_of(qi * blk, blk), blk)
            q_blk = q_ref[qrows, :]

            def update(st, krows, diagonal):
                m, l, acc = st
                s = _fox_scores(k_ref[krows, :], q_blk, fk_s[krows, :], diagonal)
                m_new = jnp.maximum(m, jnp.max(s, axis=0, keepdims=True))
                alpha = jnp.exp(m - m_new)
                p = jnp.exp(s - m_new)
                l = alpha * l + jnp.sum(p, axis=0, keepdims=True)
                acc = acc * alpha + _dot(v_ref[krows, :], p, _TN)
                return m_new, l, acc

            def k_block(kj, st):
                return update(st, pl.ds(pl.multiple_of(kj * blk, blk), blk), False)

            init = (jnp.full((1, blk), -jnp.inf, F32), jnp.zeros((1, blk), F32),
                    jnp.zeros((FOX_HEAD, blk), F32))
            m, l, acc = update(lax.fori_loop(0, qi, k_block, init), qrows, True)
            o_ref[qrows, :] = (acc / l).T.astype(o_ref.dtype)
            lse_ref[0, :, qrows] = m + jnp.log(l)
            return carry

        lax.fori_loop(0, nq, q_block, 0)

    def col(kind):
        return pl.BlockSpec((t, FOX_HEAD), lambda h: (0, kind * heads + h))

    rowvec = pl.BlockSpec((1, 1, t), lambda h: (h, 0, 0))
    return pl.pallas_call(
        body, grid=(heads,),
        in_specs=[col(0), col(1), col(2), pl.BlockSpec((t, LANES), lambda h: (0, 0))],
        out_specs=[pl.BlockSpec((t, FOX_HEAD), lambda h: (0, h)), rowvec],
        out_shape=[jax.ShapeDtypeStruct((t, d), MXU_DTYPE), jax.ShapeDtypeStruct((heads, 1, t), F32)],
        scratch_shapes=[pltpu.VMEM((t, 1), F32)],
        compiler_params=_cp("parallel"), name=name)(zqkv, zqkv, zqkv, fcol)


def _fox_bwd(zqkv, fcol, lse, o, do, *, name):
    t = zqkv.shape[0]
    d = zqkv.shape[1] // 3
    heads = d // FOX_HEAD
    blk = min(FOX_BLOCK, t)
    nq = t // blk
    scale = FOX_HEAD ** -0.5

    def body(q_ref, k_ref, v_ref, fc_ref, lse_ref, o_ref, do_ref,
             dq_ref, dk_ref, dv_ref, rq_ref, rk_ref, dq_s, drow_s, fk_s):
        _fox_head_column(fc_ref, fk_s, pl.program_id(0))
        ones = jnp.ones((8, blk), MXU_DTYPE)
        dq_s[...] = jnp.zeros_like(dq_s)
        rq_ref[...] = jnp.zeros_like(rq_ref)
        ones_f = jnp.ones((8, FOX_HEAD), F32)
        for j in range(nq):
            rows = slice(j * blk, (j + 1) * blk)
            prod = do_ref[rows, :].astype(F32) * o_ref[rows, :].astype(F32)
            drow_s[:, rows] = _dot_f32(ones_f, prod, _NT)

        def k_block(kj, carry):
            krows = pl.ds(pl.multiple_of(kj * blk, blk), blk)
            k_blk, v_blk, fk_blk = k_ref[krows, :], v_ref[krows, :], fk_s[krows, :]

            def pair(st, qrows, diagonal):
                dk, dv, rk = st
                q_blk, do_blk = q_ref[qrows, :], do_ref[qrows, :]
                s = _fox_scores(k_blk, q_blk, fk_blk, diagonal)
                p = jnp.exp(s - lse_ref[0, :, qrows])
                dv = dv + _dot(p, do_blk, _NN)
                dp = _dot(v_blk, do_blk, _NT)
                ds = (p * (dp - drow_s[0:1, qrows])).astype(MXU_DTYPE)
                dk = dk + _dot(ds, q_blk, _NN)
                dq_s[qrows, :] += _dot(ds, k_blk, _TN)
                rq_ref[0, :, qrows] += _dot(ones, ds, _NN)[0:1, :]
                rk = rk + _dot(ones, ds, _NT)
                return dk, dv, rk

            def q_block(qi, st):
                return pair(st, pl.ds(pl.multiple_of(qi * blk, blk), blk), False)

            init = (jnp.zeros((blk, FOX_HEAD), F32), jnp.zeros((blk, FOX_HEAD), F32), jnp.zeros((8, blk), F32))
            dk, dv, rk = lax.fori_loop(kj + 1, nq, q_block, pair(init, krows, True))
            dk_ref[krows, :] = (dk * scale).astype(dk_ref.dtype)
            dv_ref[krows, :] = dv.astype(dv_ref.dtype)
            rk_ref[0, :, krows] = rk[0:1, :]
            return carry

        lax.fori_loop(0, nq, k_block, 0)
        dq_ref[...] = (dq_s[...] * scale).astype(dq_ref.dtype)

    def col(kind):
        return pl.BlockSpec((t, FOX_HEAD), lambda h: (0, kind * heads + h))

    hcol = pl.BlockSpec((t, FOX_HEAD), lambda h: (0, h))
    rowvec = pl.BlockSpec((1, 1, t), lambda h: (h, 0, 0))
    out = jax.ShapeDtypeStruct((t, d), MXU_DTYPE)
    vec = jax.ShapeDtypeStruct((heads, 1, t), F32)
    return pl.pallas_call(
        body, grid=(heads,),
        in_specs=[col(0), col(1), col(2), pl.BlockSpec((t, LANES), lambda h: (0, 0)), rowvec, hcol, hcol],
        out_specs=[hcol, hcol, hcol, rowvec, rowvec],
        out_shape=[out, out, out, vec, vec],
        scratch_shapes=[pltpu.VMEM((t, FOX_HEAD), F32), pltpu.VMEM((8, t), F32), pltpu.VMEM((t, 1), F32)],
        compiler_params=_cp("parallel"), name=name)(zqkv, zqkv, zqkv, fcol, lse, o, do)


def _fox_gate_bwd(rq, rk, zf, b_f, *, name, blk=256):
    heads, t = rq.shape

    def body(rq_ref, rk_ref, z_ref, b_ref, dfl_ref, db_ref):
        r = lax.broadcasted_iota(jnp.int32, (blk, blk), 0)
        c = lax.broadcasted_iota(jnp.int32, (blk, blk), 1)
        tri = jnp.where(r >= c, 1.0, 0.0).astype(F32)
        carry = jnp.zeros((heads, 1), F32)
        db = jnp.zeros((1, LANES), F32)
        pad = jnp.zeros((LANES - heads, blk), F32)
        for j in reversed(range(t // blk)):
            cols = slice(j * blk, (j + 1) * blk)
            df = rq_ref[:, cols] - rk_ref[:, cols]
            dls = _dot_f32(df, tri, _NN) + carry
            carry = dls[:, 0:1]
            dls_t = jnp.concatenate([dls, pad], axis=0).T
            dfl = dls_t * _sigmoid(-(z_ref[cols, :] + b_ref[...]))
            dfl_ref[cols, :] = dfl.astype(dfl_ref.dtype)
            db = db + jnp.sum(dfl, axis=0, keepdims=True)
        db_ref[...] = db

    return pl.pallas_call(
        body, out_shape=[jax.ShapeDtypeStruct((t, LANES), MXU_DTYPE), jax.ShapeDtypeStruct((1, LANES), F32)],
        compiler_params=pltpu.CompilerParams(vmem_limit_bytes=VMEM_LIMIT_BYTES), name=name)(rq, rk, zf, b_f)


def _adamw(w, m, v, parts, *, name, layer=None, prev=None, tr=128):
    lcnt, r, c = w.shape
    p = parts.shape[0]
    li = 0 if layer is None else layer
    tr = _tile(r, tr, 16)
    has_prev = prev is not None

    def body(*refs):
        w_ref, m_ref, v_ref, p_ref = refs[:4]
        g_ref, d_ref, nm_ref, nv_ref = refs[-4:]
        g = p_ref[0].astype(F32)
        for j in range(1, p):
            g = g + p_ref[j].astype(F32)
        wv = w_ref[0]
        mn = ADAM_B1 * m_ref[0] + (1.0 - ADAM_B1) * g
        vn = ADAM_B2 * v_ref[0] + (1.0 - ADAM_B2) * (g * g)
        m_hat = mn / (1.0 - ADAM_B1 ** ADAM_STEP)
        v_hat = vn / (1.0 - ADAM_B2 ** ADAM_STEP)
        g_ref[0] = g
        d_ref[0] = -ADAM_LR * (m_hat / (jnp.sqrt(v_hat) + ADAM_EPS) + ADAM_WD * wv)
        nm_ref[0] = mn
        nv_ref[0] = vn

    slab = pl.BlockSpec((1, tr, c), lambda i: (li, i, 0))
    in_specs = [slab, slab, slab, pl.BlockSpec((p, tr, c), lambda i: (0, i, 0))]
    operands = [w, m, v, parts]
    aliases = {}
    if has_prev:
        in_specs += [pl.BlockSpec(memory_space=pl.ANY)] * 4
        operands += list(prev)
        aliases = {4: 0, 5: 1, 6: 2, 7: 3}
    shp = jax.ShapeDtypeStruct((lcnt, r, c), F32)
    return pl.pallas_call(
        body, grid=(r // tr,), in_specs=in_specs, out_specs=[slab] * 4, out_shape=[shp] * 4,
        input_output_aliases=aliases, compiler_params=_cp("parallel"), name=name)(*operands)


def _my_place():
    return lax.axis_index("x"), lax.axis_index("y"), lax.axis_index("c")


def _slot(p):
    return 4 * p[0] + 2 * p[1] + p[2]


def _peer(me, mask):
    x, y, c = me
    return (1 - x if mask & 4 else x, 1 - y if mask & 2 else y, 1 - c if mask & 1 else c)


_HBM = pl.BlockSpec(memory_space=pltpu.HBM)
_SEM = pl.BlockSpec(memory_space=pltpu.SEMAPHORE)
_ANY = pl.BlockSpec(memory_space=pl.ANY)
_EFFECT = pltpu.SideEffectType.DATAFLOW_SIDE_EFFECTING


def _push_copy(src_refs, land_refs, send_sems, recv_sems, a, mask, me, per_peer, outgoing):
    peer = _peer(me, mask)
    src = src_refs[a].at[_slot(peer)] if per_peer else src_refs[a]
    dst = land_refs[a].at[_slot(me) if outgoing else _slot(peer)]
    k = a * (N_DEV - 1) + mask - 1
    return pltpu.make_async_remote_copy(
        src_ref=src, dst_ref=dst, send_sem=send_sems.at[k], recv_sem=recv_sems.at[k],
        device_id=peer, device_id_type=MESH)


def _push_start(srcs, dep, *, per_peer, name):
    n = len(srcs)
    mine = _slot(_my_place())
    lands = []
    for s in srcs:
        own = lax.dynamic_index_in_dim(s, mine, 0, keepdims=True) if per_peer else s[None]
        shape = s.shape if per_peer else (N_DEV,) + s.shape
        lands.append(lax.dynamic_update_slice_in_dim(lax.empty(shape, s.dtype), own, mine, 0))
    has_dep = dep is not None

    def body(*refs):
        src_refs, land_refs = refs[:n], refs[n:2 * n]
        send_sems, recv_sems = refs[2 * n + has_dep], refs[2 * n + has_dep + 1]
        token = refs[-1]
        me = _my_place()
        for a in range(n):
            for mask in range(1, N_DEV):
                _push_copy(src_refs, land_refs, send_sems, recv_sems, a, mask, me, per_peer, True).start()
        token[...] = jnp.zeros_like(token)

    hbm_in = [pltpu.with_memory_space_constraint(v, pltpu.HBM) for v in list(srcs) + lands]
    out = pl.pallas_call(
        body, name=name,
        out_shape=(pltpu.SemaphoreType.DMA((n * (N_DEV - 1),)), pltpu.SemaphoreType.DMA((n * (N_DEV - 1),)),
                   *[pltpu.HBM(v.shape, v.dtype) for v in hbm_in], jax.ShapeDtypeStruct((8, LANES), F32)),
        in_specs=[_HBM] * (2 * n) + ([_ANY] if has_dep else []),
        out_specs=(_SEM, _SEM, *[_HBM] * (2 * n), pl.BlockSpec(memory_space=pltpu.VMEM)),
        input_output_aliases={i: 2 + i for i in range(2 * n)},
        compiler_params=pltpu.CompilerParams(has_side_effects=_EFFECT),
    )(*hbm_in, *([dep] if has_dep else []))
    return (n, per_peer, out[:-1]), out[-1]


def _push_wait(handle, after, *, name):
    n, per_peer, (send_sems, recv_sems, *bufs) = handle

    def body(*refs):
        src_refs, land_refs = refs[:n], refs[n:2 * n]
        send_sems, recv_sems = refs[2 * n], refs[2 * n + 1]
        me = _my_place()
        for a in range(n):
            for mask in range(1, N_DEV):
                cp = _push_copy(src_refs, land_refs, send_sems, recv_sems, a, mask, me, per_peer, False)
                cp.wait_send()
                cp.wait_recv()

    out = pl.pallas_call(
        body, name=name, out_shape=tuple(pltpu.HBM(v.shape, v.dtype) for v in bufs),
        in_specs=[_HBM] * (2 * n) + [_SEM, _SEM, _ANY], out_specs=tuple([_HBM] * (2 * n)),
        input_output_aliases={i: i for i in range(2 * n)},
        compiler_params=pltpu.CompilerParams(has_side_effects=_EFFECT),
    )(*bufs, send_sems, recv_sems, after)
    return list(out[n:])


def _all_reduce_rows(v, *, name):
    r, c = v.shape

    def body(v_ref, o_ref, buf, send_sems, recv_sems):
        me = _my_place()
        mine = _slot(me)
        sends = []
        for mask in range(1, N_DEV):
            peer = _peer(me, mask)
            sends.append(pltpu.make_async_remote_copy(
                src_ref=v_ref, dst_ref=buf.at[mine], send_sem=send_sems.at[mask - 1],
                recv_sem=recv_sems.at[mask - 1], device_id=peer, device_id_type=MESH))
        for cp in sends:
            cp.start()
        buf[mine] = v_ref[...]
        for mask in range(1, N_DEV):
            peer = _peer(me, mask)
            pltpu.make_async_remote_copy(
                src_ref=v_ref, dst_ref=buf.at[_slot(peer)], send_sem=send_sems.at[mask - 1],
                recv_sem=recv_sems.at[mask - 1], device_id=peer, device_id_type=MESH).wait_recv()
        for cp in sends:
            cp.wait_send()
        total = buf[0]
        for j in range(1, N_DEV):
            total = total + buf[j]
        o_ref[...] = total

    vm = pl.BlockSpec(memory_space=pltpu.VMEM)
    return pl.pallas_call(
        body, in_specs=[vm], out_specs=vm, out_shape=jax.ShapeDtypeStruct((r, c), F32),
        scratch_shapes=[pltpu.VMEM((N_DEV, r, c), F32), pltpu.SemaphoreType.DMA((7,)),
                        pltpu.SemaphoreType.DMA((7,))],
        name=name)(v)


def _xa_fwd(x, mem, g_x, g_m, wq, wkv, wo, tag):
    hx = _rms_fwd(x, g_x, name=f"xa{tag}_norm")
    memn = _rms_fwd(mem, g_m, name=f"xa{tag}_mem_norm")
    q = _mm2(hx, wq, "nn", MXU_DTYPE, name=f"xa{tag}_q")
    kv = _mm2(memn, wkv, "nn", MXU_DTYPE, name=f"xa{tag}_kv", b_split=True)
    o = _xattn_fwd(q, kv, name=f"xa{tag}_attn")
    return _mm2(o, wo, "nn", F32, name=f"xa{tag}_out", add=x), (hx, memn, q, kv, o)


def _xa_bwd(x, mem, g_x, g_m, wq, wkv, wo, saved, dxo, dxo_lo, tag, dep=None):
    hx, memn, q, kv, o = saved
    do = _mm2(dxo_lo, wo, "nt", MXU_DTYPE, name=f"xa{tag}_do", dep=dep)
    dwo = _mm2(o, dxo_lo, "tn", MXU_DTYPE, name=f"xa{tag}_dwo")
    dq, dkv = _xattn_bwd(q, kv, do, name=f"xa{tag}_attn_bwd")
    dwq = _mm2(hx, dq, "tn", MXU_DTYPE, name=f"xa{tag}_dwq")
    dhx = _mm2(dq, wq, "nt", F32, name=f"xa{tag}_dh")
    dx, dx_lo, dgx = _rms_bwd(x, g_x, dhx, dxo, name=f"xa{tag}_norm_bwd")
    dwkv = _mm2(memn, dkv, "tn", MXU_DTYPE, name=f"xa{tag}_dwkv", o_split=True, tn=wkv.shape[2])
    dmemn = _mm2(dkv, wkv, "nt", F32, name=f"xa{tag}_dmem", b_split=True)
    _, _, dgm = _rms_bwd(mem, g_m, dmemn, None, name=f"xa{tag}_mem_norm_bwd")
    return dx, dx_lo, dgx, dgm, (dwq, dwkv, dwo)


def _ffn_fwd(x, g, wg, wu, wd, tag):
    h = _rms_fwd(x, g, name=f"ffn{tag}_norm")
    gate, up, act = _ffn_up(h, wg, wu, name=f"ffn{tag}_up")
    x_new = _mm(act, wd, "nn", F32, name=f"ffn{tag}_down", reduce_b=True, add=x[None])[0]
    return x_new, (h, gate, up, act)


def _ffn_bwd(x, g, wg, wu, wd, saved, dxo, dxo_lo, tag, dep=None):
    h, gate, up, act = saved
    dwd = _mm(act, dxo_lo[None], "tn", MXU_DTYPE, name=f"ffn{tag}_dwd", dep=dep)
    dgate, dup = _ffn_dact(dxo_lo, wd, gate, up, name=f"ffn{tag}_dact")
    dwg = _mm(h[None], dgate, "tn", MXU_DTYPE, name=f"ffn{tag}_dwg")
    dwu = _mm(h[None], dup, "tn", MXU_DTYPE, name=f"ffn{tag}_dwu")
    dh = _mm(dgate, wg, "nt", F32, name=f"ffn{tag}_dh_gate", reduce_b=True)
    dh = _mm(dup, wu, "nt", F32, name=f"ffn{tag}_dh_up", reduce_b=True, add=dh)[0]
    dx, dx_lo, dg = _rms_bwd(x, g, dh, dxo, name=f"ffn{tag}_norm_bwd")
    return dx, dx_lo, dg, (dwg, dwu, dwd)


def _even_fwd(x, g, lb, w_in, w_pool, pool_scale, hg_norm, w_out):
    h = _rms_fwd(x, g, name="ev_norm")
    z = _mm2(h, w_in, "nn", F32, name="ev_in", b_split=True)
    ycat = _pool_fwd(z, w_pool, pool_scale, name="ev_pool")
    ycat, o_raw, states = _hgrn_fwd(z, lb, hg_norm, ycat, name="ev_hgrn")
    return _mm2(ycat, w_out, "nn", F32, name="ev_out", add=x), (h, z, ycat, o_raw, states)


def _even_bwd(x, g, lb, w_in, w_pool, pool_scale, hg_norm, w_out, saved, dxo, dxo_lo, dep=None):
    h, z, ycat, o_raw, states = saved
    dycat = _mm2(dxo_lo, w_out, "nt", MXU_DTYPE, name="ev_dy", dep=dep)
    dw_out = _mm2(ycat, dxo_lo, "tn", MXU_DTYPE, name="ev_dw_out")
    du, dw_pool, dscale = _pool_bwd(z, w_pool, pool_scale, dycat, name="ev_pool_bwd")
    dq, dfl, di, dg, dlb, dhn = _hgrn_bwd(z, lb, hg_norm, o_raw, states, dycat, name="ev_hgrn_bwd")
    dz = jnp.concatenate([du, dq, dfl, di, dg], axis=1)
    dw_in = _mm2(h, dz, "tn", MXU_DTYPE, name="ev_dw_in", o_split=True, tn=w_in.shape[2])
    dh = _mm2(dz, w_in, "nt", F32, name="ev_dh", b_split=True)
    dx, dx_lo, dgn = _rms_bwd(x, g, dh, dxo, name="ev_norm_bwd")
    return dx, dx_lo, dict(ev_norm=dgn, ev_pool_scale=dscale, ev_hg_norm=dhn, lb=dlb), (dw_in, dw_pool, dw_out)


def _odd_fwd(x, g, w_qkv, w_f, b_f, w_out):
    h = _rms_fwd(x, g, name="od_norm")
    zqkv = _mm2(h, w_qkv, "nn", MXU_DTYPE, name="od_qkv")
    zf = _mm2(h, w_f, "nn", F32, name="od_gate")
    fcol = _fox_prep(zf, b_f, name="od_fox_prep")
    o, lse = _fox_fwd(zqkv, fcol, name="od_fox")
    return _mm2(o, w_out, "nn", F32, name="od_out", add=x), (h, zqkv, zf, fcol, o, lse)


def _odd_bwd(x, g, w_qkv, w_f, b_f, w_out, saved, dxo, dxo_lo, dep=None):
    h, zqkv, zf, fcol, o, lse = saved
    do = _mm2(dxo_lo, w_out, "nt", MXU_DTYPE, name="od_do", dep=dep)
    dw_out = _mm2(o, dxo_lo, "tn", MXU_DTYPE, name="od_dw_out")
    dq, dk, dv, rq, rk = _fox_bwd(zqkv, fcol, lse, o, do, name="od_fox_bwd")
    dfl, db_f = _fox_gate_bwd(rq[:, 0, :], rk[:, 0, :], zf, b_f, name="od_fox_gate_bwd")
    dz = jnp.concatenate([dq, dk, dv], axis=1)
    dw_qkv = _mm2(h, dz, "tn", MXU_DTYPE, name="od_dw_qkv")
    dw_f = _mm2(h, dfl, "tn", MXU_DTYPE, name="od_dw_gate")
    dh = _mm2(dz, w_qkv, "nt", F32, name="od_dh_qkv")
    dh = _mm2(dfl, w_f, "nt", F32, name="od_dh_gate", add=dh)
    dx, dx_lo, dgn = _rms_bwd(x, g, dh, dxo, name="od_norm_bwd")
    return dx, dx_lo, dict(od_norm=dgn, od_b_f=db_f), (dw_qkv, dw_f, dw_out)


def _local_step(x, mem, target, sp, get_w, put_dw):
    b_f = jnp.pad(sp["od_b_f"], ((0, 0), (0, LANES - sp["od_b_f"].shape[1])))
    lb = _lb_fwd(sp["lb_table"], 0, name="lb_fwd")
    fin = sp["final_norm"].reshape(1, -1)
    xn, xm, fn = sp["xa_norm"], sp["xa_mem_norm"], sp["ffn_norm"]
    w_ev = get_w("ev", None)
    x1, s_ev = _even_fwd(x, sp["ev_norm"], lb, w_ev[0], w_ev[1], sp["ev_pool_scale"], sp["ev_hg_norm"], w_ev[2])
    w_xa0 = get_w("xa0", x1)
    x2, s_xa0 = _xa_fwd(x1, mem, xn[0:1], xm[0:1], *w_xa0, 0)
    w_ff0 = get_w("ffn0", x2)
    x3, s_ff0 = _ffn_fwd(x2, fn[0:1], *w_ff0, 0)
    w_qkv, w_f, w_od_out, od_norm = get_w("od", x3)
    x4, s_od = _odd_fwd(x3, od_norm, w_qkv, w_f, b_f, w_od_out)
    w_xa1 = get_w("xa1", x4)
    x5, s_xa1 = _xa_fwd(x4, mem, xn[1:2], xm[1:2], *w_xa1, 1)
    w_ff1 = get_w("ffn1", x5)
    x6, s_ff1 = _ffn_fwd(x5, fn[1:2], *w_ff1, 1)
    loss, dx, dx_lo, d_fin = _loss_head(x6, fin, target, name="loss_head")
    dx, dx_lo, d_ffn1, dw = _ffn_bwd(x5, fn[1:2], *w_ff1, s_ff1, dx, dx_lo, 1)
    tok = put_dw("ffn1", dw)
    dx, dx_lo, d_xa1, d_xm1, dw = _xa_bwd(x4, mem, xn[1:2], xm[1:2], *w_xa1, s_xa1, dx, dx_lo, 1, dep=tok)
    tok = put_dw("xa1", dw)
    dx, dx_lo, d_od, dw = _odd_bwd(x3, od_norm, w_qkv, w_f, b_f, w_od_out, s_od, dx, dx_lo, dep=tok)
    tok = put_dw("od", dw)
    dx, dx_lo, d_ffn0, dw = _ffn_bwd(x2, fn[0:1], *w_ff0, s_ff0, dx, dx_lo, 0, dep=tok)
    tok = put_dw("ffn0", dw)
    dx, dx_lo, d_xa0, d_xm0, dw = _xa_bwd(x1, mem, xn[0:1], xm[0:1], *w_xa0, s_xa0, dx, dx_lo, 0, dep=tok)
    tok = put_dw("xa0", dw)
    dx, _, d_ev, dw = _even_bwd(x, sp["ev_norm"], lb, w_ev[0], w_ev[1], sp["ev_pool_scale"], sp["ev_hg_norm"],
                                w_ev[2], s_ev, dx, dx_lo, dep=tok)
    put_dw("ev", dw)
    small = dict(
        lb_table=_lb_bwd(sp["lb_table"], d_ev["lb"], 0, name="lb_bwd"),
        ev_norm=d_ev["ev_norm"], ev_pool_scale=d_ev["ev_pool_scale"], ev_hg_norm=d_ev["ev_hg_norm"],
        od_norm=d_od["od_norm"], od_b_f=d_od["od_b_f"][:, :sp["od_b_f"].shape[1]],
        xa_norm=jnp.concatenate([d_xa0, d_xa1], axis=0), xa_mem_norm=jnp.concatenate([d_xm0, d_xm1], axis=0),
        ffn_norm=jnp.concatenate([d_ffn0, d_ffn1], axis=0), final_norm=d_fin.reshape(-1))
    return loss, dx, small


_SMALL = ("lb_table", "ev_norm", "ev_pool_scale", "ev_hg_norm", "od_norm", "od_b_f", "xa_norm", "xa_mem_norm",
          "ffn_norm", "final_norm")
_WEIGHTS = ("lb_table", "ev_norm", "ev_w_in", "ev_w_pool", "ev_pool_scale", "ev_hg_norm", "ev_w_out", "od_norm",
            "od_w_in", "od_b_f", "od_w_out", "xa_norm", "xa_mem_norm", "xa_wq", "xa_wkv", "xa_wo", "ffn_norm",
            "ffn_w_gate", "ffn_w_up", "ffn_w_down", "final_norm")


def _lo(a):
    return a.astype(MXU_DTYPE)


def _rows(v):
    flat = v.reshape(-1)
    return jnp.pad(flat, (0, (-flat.shape[0]) % LANES)).reshape(-1, LANES)


def kernel(x, mem, lb_table, ev_norm, ev_w_in, ev_w_pool, ev_pool_scale, ev_hg_norm, ev_w_out, od_norm, od_w_in, od_b_f, od_w_out, xa_norm, xa_mem_norm, xa_wq, xa_wkv, xa_wo, ffn_norm, ffn_w_gate, ffn_w_up, ffn_w_down, final_norm, loss_target, m_lb_table, m_ev_norm, m_ev_w_in, m_ev_w_pool, m_ev_pool_scale, m_ev_hg_norm, m_ev_w_out, m_od_norm, m_od_w_in, m_od_b_f, m_od_w_out, m_xa_norm, m_xa_mem_norm, m_xa_wq, m_xa_wkv, m_xa_wo, m_ffn_norm, m_ffn_w_gate, m_ffn_w_up, m_ffn_w_down, m_final_norm, v_lb_table, v_ev_norm, v_ev_w_in, v_ev_w_pool, v_ev_pool_scale, v_ev_hg_norm, v_ev_w_out, v_od_norm, v_od_w_in, v_od_b_f, v_od_w_out, v_xa_norm, v_xa_mem_norm, v_xa_wq, v_xa_wkv, v_xa_wo, v_ffn_norm, v_ffn_w_gate, v_ffn_w_up, v_ffn_w_down, v_final_norm):
    arg = dict(locals())
    d = x.shape[-1]
    layers = xa_wq.shape[0]
    me = _slot(_my_place())

    n_gate = od_b_f.shape[1]
    shards = dict(ev=[_lo(ev_w_in[0]), _lo(ev_w_pool[0]), _lo(ev_w_out[0])],
                  od=[_lo(od_w_in[0]), _lo(od_w_out[0]), od_norm])
    for l in range(layers):
        shards[f"xa{l}"] = [_lo(xa_wq[l]), _lo(xa_wkv[l]), _lo(xa_wo[l])]
        shards[f"ffn{l}"] = [_lo(ffn_w_gate[l]), _lo(ffn_w_up[l]), _lo(ffn_w_down[l])]
    gathers, tok = {}, None
    for grp in ("ev", "xa0", "ffn0", "od", "xa1", "ffn1"):
        gathers[grp], tok = _push_start(shards[grp], tok, per_peer=False, name=f"gather_{grp}_start")
    last_start = tok

    def get_w(grp, after):
        got = _push_wait(gathers[grp], last_start if after is None else after, name=f"gather_{grp}_wait")
        if grp == "ev":
            w_in, w_pool, w_out = got
            w_pool = jnp.transpose(w_pool, (1, 0, 2, 3)).reshape(w_pool.shape[1], -1, w_pool.shape[3])
            return w_in, w_pool, w_out.reshape(d, d)
        if grp == "od":
            w_in, w_out, nrm = got
            w_in = jnp.transpose(w_in, (1, 0, 2)).reshape(d, -1)
            w_f = jnp.pad(w_in[:, w_in.shape[1] - n_gate:], ((0, 0), (0, LANES - n_gate)))
            return w_in[:, :w_in.shape[1] - n_gate], w_f, w_out.reshape(d, d), nrm.reshape(1, d)
        if grp.startswith("xa"):
            return got[0].reshape(d, d), got[1], got[2].reshape(d, d)
        return tuple(got)

    def row_parts(g):
        return g.reshape(N_DEV, -1, g.shape[-1])

    scatters = {}

    def put_dw(grp, dws):
        if grp == "ev":
            dw_in, dw_pool, dw_out = dws
            gc = dw_pool.shape[1] // N_DEV
            dw_pool = _lo(jnp.transpose(dw_pool.reshape(dw_pool.shape[0], N_DEV, gc, -1), (1, 0, 2, 3)))
            parts = [dw_in, dw_pool, row_parts(dw_out)]
        elif grp == "od":
            dw_qkv, dw_f, dw_out = dws
            dw_in = jnp.concatenate([dw_qkv, dw_f[:, :n_gate]], axis=1)
            parts = [jnp.transpose(dw_in.reshape(d, N_DEV, -1), (1, 0, 2)), row_parts(dw_out)]
        elif grp.startswith("xa"):
            parts = [row_parts(dws[0]), dws[1], row_parts(dws[2])]
        else:
            parts = list(dws)
        scatters[grp], token = _push_start(parts, None, per_peer=True, name=f"scatter_{grp}_start")
        return token

    sp = {k: arg[k] for k in _SMALL if k != "od_norm"}
    loss, dx, small = _local_step(x[0], mem[0], loss_target[0], sp, get_w, put_dw)

    pieces = [_rows(small[k]) for k in _SMALL]
    packed = jnp.concatenate(pieces, axis=0)
    packed = jnp.pad(packed, ((0, (-packed.shape[0]) % 8), (0, 0)))
    total = _all_reduce_rows(packed, name="all_reduce_small")
    small_g, at = {}, 0
    for k, pc in zip(_SMALL, pieces):
        n = small[k].size
        small_g[k] = total[at:at + pc.shape[0]].reshape(-1)[:n].reshape(small[k].shape)
        at += pc.shape[0]
    small_g["od_norm"] = lax.dynamic_slice_in_dim(small_g["od_norm"], me * od_norm.shape[1], od_norm.shape[1], axis=1)

    res = {}
    for k in _SMALL:
        w, m, v = arg[k], arg["m_" + k], arg["v_" + k]
        shp = (1, 1, w.shape[0]) if w.ndim == 1 else (1,) + w.shape
        out = _adamw(w.reshape(shp), m.reshape(shp), v.reshape(shp), small_g[k].reshape(shp), name=f"adamw_{k}")
        res[k] = [o.reshape(w.shape) for o in out]
    members = dict(ev=("ev_w_in", "ev_w_pool", "ev_w_out"), od=("od_w_in", "od_w_out"),
                   xa=("xa_wq", "xa_wkv", "xa_wo"), ffn=("ffn_w_gate", "ffn_w_up", "ffn_w_down"))
    after, stacked = dx, {}
    for grp in ("ffn1", "xa1", "od", "ffn0", "xa0", "ev"):
        got = _push_wait(scatters[grp], after, name=f"scatter_{grp}_wait")
        kind = grp.rstrip("01")
        for k, parts in zip(members[kind], got):
            w, m, v = arg[k], arg["m_" + k], arg["v_" + k]
            if w.shape[0] == 1:
                shp = (1, -1, w.shape[-1])
                out = _adamw(w.reshape(shp), m.reshape(shp), v.reshape(shp), parts.reshape(N_DEV, -1, w.shape[-1]),
                             name=f"adamw_{k}")
            else:
                out = _adamw(w, m, v, parts, name=f"adamw_{k}{grp[-1]}", layer=int(grp[-1]), prev=stacked.get(k))
                stacked[k] = out
            res[k] = [o.reshape(w.shape) for o in out]
            after = out[3][:1, :8, :LANES]

    loss = lax.psum(loss[0, 0], ("x", "y", "c"))
    outs = [loss, dx[None]]
    for j in range(4):
        outs += [res[k][j] for k in _WEIGHTS]
    return tuple(outs)
```

```python
import functools

import jax
import jax.numpy as jnp
from jax import lax
from jax.experimental import pallas as pl
from jax.experimental.pallas import tpu as pltpu

F32 = jnp.float32
MXU_DTYPE = jnp.bfloat16
EPS = 1e-6
N_DEV = 8
V7X_VMEM_BYTES = 64 * 1024 * 1024
VMEM_LIMIT_BYTES = V7X_VMEM_BYTES - 8 * 1024 * 1024
LANES = 128
HIGHEST = lax.Precision.HIGHEST
MESH = pl.DeviceIdType.MESH

HG_HEAD = 128
HG_CHUNK = 16
FOX_HEAD = 128
XA_HEADS = 4
POOL_GROUPS = 4

ADAM_LR = 0.001
ADAM_B1 = 0.9
ADAM_B2 = 0.999
ADAM_EPS = 1e-08
ADAM_WD = 0.01
ADAM_STEP = 10

_NN = ((1,), (0,))
_NT = ((1,), (1,))
_TN = ((0,), (0,))


def _dot(a, b, dims):
    return lax.dot_general(a.astype(MXU_DTYPE), b.astype(MXU_DTYPE), (dims, ((), ())),
                           preferred_element_type=F32)


def _dot_f32(a, b, dims):
    return lax.dot_general(a, b, (dims, ((), ())), preferred_element_type=F32, precision=HIGHEST)


def _cp(*sem):
    return pltpu.CompilerParams(dimension_semantics=sem, vmem_limit_bytes=VMEM_LIMIT_BYTES)


def _tile(n, pref, align=LANES):
    if n <= pref:
        return n
    t = (pref // align) * align
    while t >= align:
        if n % t == 0:
            return t
        t -= align
    return n


def _sigmoid(x):
    return jax.nn.sigmoid(x)


def _mm(a, b, mode, out_dtype, *, name, add=None, dep=None, reduce_b=False, b_split=False, o_split=False,
        tm=1024, tn=1024, tk=2048):
    ba, bb = a.shape[0], b.shape[0]
    if mode == "tn":
        kdim, m = a.shape[1], a.shape[2]
        tk = 2 * tk
    else:
        m, kdim = a.shape[1], a.shape[2]
    if b_split:
        s_cnt, b_rows, w = b.shape
        if mode == "nt":
            n = b_rows
            assert kdim == s_cnt * w
            tk = w
        else:
            n = s_cnt * w
            assert b_rows == kdim
            tn = w
        nb = ba
    else:
        n = b.shape[1] if mode == "nt" else b.shape[2]
        nb = max(ba, bb)
    if not (b_split and mode != "nt"):
        tn = _tile(n, tn)
    if not (b_split and mode == "nt"):
        tk = _tile(kdim, tk)
    tm = _tile(m, tm)
    assert m % tm == 0 and n % tn == 0 and kdim % tk == 0, (name, m, n, kdim, tm, tn, tk)
    nk = kdim // tk
    if reduce_b:
        grid = (m // tm, n // tn, nb, nk)
        unpack = lambda i, j, bi, k: (bi, i, j, k)
        sem = ("parallel", "parallel", "arbitrary", "arbitrary")
        nred = nb * nk
    else:
        grid = (nb, m // tm, n // tn, nk)
        unpack = lambda bi, i, j, k: (bi, i, j, k)
        sem = ("parallel", "parallel", "parallel", "arbitrary")
        nred = nk

    def a_map(*g):
        bi, i, j, k = unpack(*g)
        ab = bi if ba > 1 else 0
        return (ab, k, i) if mode == "tn" else (ab, i, k)

    def b_map(*g):
        bi, i, j, k = unpack(*g)
        if b_split:
            return (k, j, 0) if mode == "nt" else (j, k, 0)
        bq = bi if bb > 1 else 0
        return (bq, j, k) if mode == "nt" else (bq, k, j)

    def o_map(*g):
        bi, i, j, k = unpack(*g)
        if o_split:
            return (j, i, 0)
        return (0 if reduce_b else bi, i, j)

    a_blk = (1, tk, tm) if mode == "tn" else (1, tm, tk)
    b_blk = (1, tn, tk) if mode == "nt" else (1, tk, tn)
    dims = {"nn": _NN, "nt": _NT, "tn": _TN}[mode]
    has_add = add is not None

    def body(*refs):
        a_ref, b_ref = refs[:2]
        if has_add:
            add_ref = refs[2]
        if nred == 1:
            o_ref = refs[-1]
            r = _dot(a_ref[0], b_ref[0], dims)
            if has_add:
                r = r + add_ref[0].astype(F32)
            o_ref[0] = r.astype(o_ref.dtype)
            return
        o_ref, acc_ref = refs[-2:]
        if reduce_b:
            step = pl.program_id(2) * nk + pl.program_id(3)
        else:
            step = pl.program_id(3)

        @pl.when(step == 0)
        def _():
            acc_ref[...] = _dot(a_ref[0], b_ref[0], dims)

        @pl.when(step > 0)
        def _():
            acc_ref[...] += _dot(a_ref[0], b_ref[0], dims)

        @pl.when(step == nred - 1)
        def _():
            r = acc_ref[...]
            if has_add:
                r = r + add_ref[0].astype(F32)
            o_ref[0] = r.astype(o_ref.dtype)

    in_specs = [pl.BlockSpec(a_blk, a_map), pl.BlockSpec(b_blk, b_map)]
    operands = [a, b]
    if has_add:
        in_specs.append(pl.BlockSpec((1, tm, tn), o_map))
        operands.append(add)
    if dep is not None:
        in_specs.append(pl.BlockSpec(memory_space=pl.ANY))
        operands.append(dep)
    if o_split:
        out_shape = jax.ShapeDtypeStruct((n // tn, m, tn), out_dtype)
    else:
        out_shape = jax.ShapeDtypeStruct((1 if reduce_b else nb, m, n), out_dtype)
    return pl.pallas_call(
        body, grid=grid, in_specs=in_specs, out_specs=pl.BlockSpec((1, tm, tn), o_map),
        out_shape=out_shape, scratch_shapes=[] if nred == 1 else [pltpu.VMEM((tm, tn), F32)],
        compiler_params=_cp(*sem), name=name)(*operands)


def _mm2(a, b, mode, out_dtype, *, name, add=None, **kw):
    b3 = b if kw.get("b_split") else b[None]
    r = _mm(a[None], b3, mode, out_dtype, name=name, add=None if add is None else add[None], **kw)
    return r if kw.get("o_split") else r[0]


def _rms_fwd(x, g, *, name, tb=512):
    t, d = x.shape
    tb = min(tb, t)

    def body(x_ref, g_ref, o_ref):
        xv = x_ref[...]
        r = lax.rsqrt(jnp.mean(xv * xv, axis=-1, keepdims=True) + EPS)
        o_ref[...] = (xv * r * g_ref[...]).astype(o_ref.dtype)

    return pl.pallas_call(
        body, grid=(t // tb,),
        in_specs=[pl.BlockSpec((tb, d), lambda i: (i, 0)), pl.BlockSpec((1, d), lambda i: (0, 0))],
        out_specs=pl.BlockSpec((tb, d), lambda i: (i, 0)),
        out_shape=jax.ShapeDtypeStruct((t, d), MXU_DTYPE), compiler_params=_cp("parallel"), name=name)(x, g)


def _rms_bwd(x, g, dh, dres, *, name, tb=512):
    t, d = x.shape
    tb = min(tb, t)
    has_res = dres is not None

    def body(*refs):
        if has_res:
            x_ref, g_ref, dh_ref, dres_ref, dx_ref, dxl_ref, dg_ref = refs
        else:
            x_ref, g_ref, dh_ref, dx_ref, dxl_ref, dg_ref = refs
        xv = x_ref[...]
        r = lax.rsqrt(jnp.mean(xv * xv, axis=-1, keepdims=True) + EPS)
        xh = xv * r
        dhv = dh_ref[...].astype(F32)

        @pl.when(pl.program_id(0) == 0)
        def _():
            dg_ref[...] = jnp.zeros_like(dg_ref)

        dg_ref[...] += jnp.sum(dhv * xh, axis=0, keepdims=True)
        dxh = dhv * g_ref[...]
        dx = r * (dxh - xh * jnp.mean(dxh * xh, axis=-1, keepdims=True))
        if has_res:
            dx = dx + dres_ref[...]
        dx_ref[...] = dx
        dxl_ref[...] = dx.astype(dxl_ref.dtype)

    row = pl.BlockSpec((tb, d), lambda i: (i, 0))
    vec = pl.BlockSpec((1, d), lambda i: (0, 0))
    operands = [x, g, dh] + ([dres] if has_res else [])
    return pl.pallas_call(
        body, grid=(t // tb,), in_specs=[row, vec, row] + ([row] if has_res else []),
        out_specs=[row, row, vec],
        out_shape=[jax.ShapeDtypeStruct((t, d), F32), jax.ShapeDtypeStruct((t, d), MXU_DTYPE),
                   jax.ShapeDtypeStruct((1, d), F32)],
        compiler_params=_cp("arbitrary"), name=name)(*operands)


def _loss_head(x, g, target, *, name, tb=512):
    t, d = x.shape
    tb = min(tb, t)

    def body(x_ref, g_ref, t_ref, loss_ref, dx_ref, dxl_ref, dg_ref):
        xv = x_ref[...]
        r = lax.rsqrt(jnp.mean(xv * xv, axis=-1, keepdims=True) + EPS)
        xh = xv * r
        gv = g_ref[...]
        err = xh * gv - t_ref[...]

        @pl.when(pl.program_id(0) == 0)
        def _():
            dg_ref[...] = jnp.zeros_like(dg_ref)
            loss_ref[...] = jnp.zeros_like(loss_ref)

        row_loss = jnp.mean(err * err, axis=-1, keepdims=True)
        loss_ref[...] += 0.5 * jnp.sum(row_loss, axis=0, keepdims=True)
        dy = err * (1.0 / d)
        dg_ref[...] += jnp.sum(dy * xh, axis=0, keepdims=True)
        dxh = dy * gv
        dx = r * (dxh - xh * jnp.mean(dxh * xh, axis=-1, keepdims=True))
        dx_ref[...] = dx
        dxl_ref[...] = dx.astype(dxl_ref.dtype)

    row = pl.BlockSpec((tb, d), lambda i: (i, 0))
    vec = pl.BlockSpec((1, d), lambda i: (0, 0))
    return pl.pallas_call(
        body, grid=(t // tb,), in_specs=[row, vec, row],
        out_specs=[pl.BlockSpec((1, 1), lambda i: (0, 0)), row, row, vec],
        out_shape=[jax.ShapeDtypeStruct((1, 1), F32), jax.ShapeDtypeStruct((t, d), F32),
                   jax.ShapeDtypeStruct((t, d), MXU_DTYPE), jax.ShapeDtypeStruct((1, d), F32)],
        compiler_params=_cp("arbitrary"), name=name)(x, g, target)


def _ffn_up(h, wg, wu, *, name, tb=512):
    t, d = h.shape
    s, _, f = wg.shape
    tb = min(tb, t)

    def body(h_ref, wg_ref, wu_ref, g_ref, u_ref, a_ref):
        hv = h_ref[...]
        gv = _dot(hv, wg_ref[0], _NN)
        uv = _dot(hv, wu_ref[0], _NN)
        g_ref[0] = gv.astype(g_ref.dtype)
        u_ref[0] = uv.astype(u_ref.dtype)
        a_ref[0] = (gv * _sigmoid(gv) * uv).astype(a_ref.dtype)

    wspec = pl.BlockSpec((1, d, f), lambda j, i: (j, 0, 0))
    ospec = pl.BlockSpec((1, tb, f), lambda j, i: (j, i, 0))
    return pl.pallas_call(
        body, grid=(s, t // tb),
        in_specs=[pl.BlockSpec((tb, d), lambda j, i: (i, 0)), wspec, wspec],
        out_specs=[ospec, ospec, ospec],
        out_shape=[jax.ShapeDtypeStruct((s, t, f), MXU_DTYPE)] * 3,
        compiler_params=_cp("parallel", "parallel"), name=name)(h, wg, wu)


def _ffn_dact(dy, wd, gate, up, *, name, tb=1024):
    t, d = dy.shape
    s, f, _ = wd.shape
    tb = min(tb, t)

    def body(dy_ref, wd_ref, g_ref, u_ref, dg_ref, du_ref):
        da = _dot(dy_ref[...], wd_ref[0], _NT)
        gv = g_ref[0].astype(F32)
        sg = _sigmoid(gv)
        du_ref[0] = (da * gv * sg).astype(du_ref.dtype)
        dg_ref[0] = (da * u_ref[0].astype(F32) * (sg * (1.0 + gv * (1.0 - sg)))).astype(dg_ref.dtype)

    aspec = pl.BlockSpec((1, tb, f), lambda j, i: (j, i, 0))
    return pl.pallas_call(
        body, grid=(s, t // tb),
        in_specs=[pl.BlockSpec((tb, d), lambda j, i: (i, 0)),
                  pl.BlockSpec((1, f, d), lambda j, i: (j, 0, 0)), aspec, aspec],
        out_specs=[aspec, aspec],
        out_shape=[jax.ShapeDtypeStruct((s, t, f), MXU_DTYPE), jax.ShapeDtypeStruct((s, t, f), MXU_DTYPE)],
        compiler_params=_cp("parallel", "parallel"), name=name)(dy, wd, gate, up)


def _xattn_fwd(q, kv, *, name, tb=512):
    t, d = q.shape
    m = kv.shape[0]
    hd = d // XA_HEADS
    tb = min(tb, t)
    scale = hd ** -0.5

    def body(q_ref, kv_ref, o_ref):
        for hh in range(XA_HEADS):
            cs = slice(hh * hd, (hh + 1) * hd)
            s = _dot(q_ref[:, cs], kv_ref[:, cs], _NT) * scale
            s = s - jnp.max(s, axis=-1, keepdims=True)
            e = jnp.exp(s)
            p = e / jnp.sum(e, axis=-1, keepdims=True)
            o_ref[:, cs] = _dot(p, kv_ref[:, d + hh * hd:d + (hh + 1) * hd], _NN).astype(o_ref.dtype)

    return pl.pallas_call(
        body, grid=(t // tb,),
        in_specs=[pl.BlockSpec((tb, d), lambda i: (i, 0)), pl.BlockSpec((m, 2 * d), lambda i: (0, 0))],
        out_specs=pl.BlockSpec((tb, d), lambda i: (i, 0)),
        out_shape=jax.ShapeDtypeStruct((t, d), MXU_DTYPE), compiler_params=_cp("parallel"), name=name)(q, kv)


def _xattn_bwd(q, kv, do, *, name, tb=512):
    t, d = q.shape
    m = kv.shape[0]
    hd = d // XA_HEADS
    tb = min(tb, t)
    scale = hd ** -0.5

    def body(q_ref, kv_ref, do_ref, dq_ref, dkv_ref):
        @pl.when(pl.program_id(0) == 0)
        def _():
            dkv_ref[...] = jnp.zeros_like(dkv_ref)

        for hh in range(XA_HEADS):
            cs = slice(hh * hd, (hh + 1) * hd)
            vs = slice(d + hh * hd, d + (hh + 1) * hd)
            qv, kk, vv, dov = q_ref[:, cs], kv_ref[:, cs], kv_ref[:, vs], do_ref[:, cs]
            s = _dot(qv, kk, _NT) * scale
            s = s - jnp.max(s, axis=-1, keepdims=True)
            e = jnp.exp(s)
            p = e / jnp.sum(e, axis=-1, keepdims=True)
            dkv_ref[:, vs] += _dot(p, dov, _TN)
            dp = _dot(dov, vv, _NT)
            ds = p * (dp - jnp.sum(p * dp, axis=-1, keepdims=True)) * scale
            dq_ref[:, cs] = _dot(ds, kk, _NN).astype(dq_ref.dtype)
            dkv_ref[:, cs] += _dot(ds, qv, _TN)

    row = pl.BlockSpec((tb, d), lambda i: (i, 0))
    full = pl.BlockSpec((m, 2 * d), lambda i: (0, 0))
    return pl.pallas_call(
        body, grid=(t // tb,), in_specs=[row, full, row], out_specs=[row, full],
        out_shape=[jax.ShapeDtypeStruct((t, d), MXU_DTYPE), jax.ShapeDtypeStruct((m, 2 * d), F32)],
        compiler_params=_cp("arbitrary"), name=name)(q, kv, do)


def _pool_window_stats(u, gi, reverse):
    t = u.shape[0]
    row = lax.broadcasted_iota(jnp.int32, u.shape, 0)
    s = u
    for j in range(POOL_GROUPS):
        sh = 1 << j
        if reverse:
            rolled = jnp.where(row < t - sh, pltpu.roll(s, t - sh, axis=0), 0.0)
        else:
            rolled = jnp.where(row >= sh, pltpu.roll(s, sh, axis=0), 0.0)
        s = jnp.where(j <= gi, s + rolled, s)
    return s, row


def _pool_fwd(z, w_pool, scale, *, name):
    t = z.shape[0]
    g_cnt, c, _ = w_pool.shape

    def body(z_ref, w_ref, s_ref, o_ref):
        gi = pl.program_id(0)
        u = z_ref[...]
        win, row = _pool_window_stats(u, gi, False)
        cnt = jnp.minimum(row + 1, lax.shift_left(jnp.int32(2), gi)).astype(F32)
        p = win / cnt - u
        o_ref[...] = (_dot(p, w_ref[0], _NN) * s_ref[...]).astype(o_ref.dtype)

    return pl.pallas_call(
        body, grid=(g_cnt,),
        in_specs=[pl.BlockSpec((t, c), lambda g: (0, g)), pl.BlockSpec((1, c, c), lambda g: (g, 0, 0)),
                  pl.BlockSpec((1, c), lambda g: (0, g))],
        out_specs=pl.BlockSpec((t, c), lambda g: (0, g)),
        out_shape=jax.ShapeDtypeStruct((t, 2 * g_cnt * c), MXU_DTYPE),
        compiler_params=_cp("parallel"), name=name)(z, w_pool, scale)


def _pool_bwd(z, w_pool, scale, dycat, *, name):
    t = z.shape[0]
    g_cnt, c, _ = w_pool.shape

    def body(z_ref, w_ref, s_ref, dy_ref, du_ref, dw_ref, ds_ref):
        gi = pl.program_id(0)
        u = z_ref[...]
        win, row = _pool_window_stats(u, gi, False)
        cnt = jnp.minimum(row + 1, lax.shift_left(jnp.int32(2), gi)).astype(F32)
        p = win / cnt - u
        y = _dot(p, w_ref[0], _NN)
        dya = dy_ref[...].astype(F32)
        ds_ref[...] = jnp.sum(dya * y, axis=0, keepdims=True)
        dy = dya * s_ref[...]
        dw_ref[0] = _dot(p, dy, _TN)
        dp = _dot(dy, w_ref[0], _NT)
        back, _ = _pool_window_stats(dp / cnt, gi, True)
        du_ref[...] = (back - dp).astype(du_ref.dtype)

    col = pl.BlockSpec((t, c), lambda g: (0, g))
    return pl.pallas_call(
        body, grid=(g_cnt,),
        in_specs=[col, pl.BlockSpec((1, c, c), lambda g: (g, 0, 0)), pl.BlockSpec((1, c), lambda g: (0, g)), col],
        out_specs=[col, pl.BlockSpec((1, c, c), lambda g: (g, 0, 0)), pl.BlockSpec((1, c), lambda g: (0, g))],
        out_shape=[jax.ShapeDtypeStruct((t, g_cnt * c), MXU_DTYPE), jax.ShapeDtypeStruct((g_cnt, c, c), F32),
                   jax.ShapeDtypeStruct((1, g_cnt * c), F32)],
        compiler_params=_cp("parallel"), name=name)(z, w_pool, scale, dycat)


def _chunk_tri(lower):
    r = lax.broadcasted_iota(jnp.int32, (LANES, LANES), 0)
    c = lax.broadcasted_iota(jnp.int32, (LANES, LANES), 1)
    same = (r // HG_CHUNK) == (c // HG_CHUNK)
    return jnp.where(same & ((c <= r) if lower else (c >= r)), 1.0, 0.0).astype(F32)


def _hgrn_prepare(q_ref, f_ref, lb_ref, qh_s, k_s, b_s, qt_s, kt_s, gl_s):
    tb = q_ref.shape[0]
    lb = lb_ref[...]
    sg = _sigmoid(f_ref[...])
    f = lb + (1.0 - lb) * sg
    logf = jnp.log(f)
    qv = q_ref[...]
    qh = qv * _sigmoid(qv) * (HG_HEAD ** -0.5)
    tri = _chunk_tri(True)
    for r in range(tb // LANES):
        rows = slice(r * LANES, (r + 1) * LANES)
        b_s[rows, :] = _dot_f32(tri, logf[rows, :], _NN)
    b = b_s[...]
    b3 = b.reshape(tb // HG_CHUNK, HG_CHUNK, HG_HEAD)
    bl = b3[:, HG_CHUNK - 1:HG_CHUNK, :]
    k = 1.0 - f
    qh_s[...] = qh
    k_s[...] = k
    qt_s[...] = qh * jnp.exp(b)
    kt_s[...] = k * jnp.exp(bl - b3).reshape(tb, HG_HEAD)
    gl_s[...] = jnp.exp(jnp.broadcast_to(bl, b3.shape)).reshape(tb, HG_HEAD)
    return sg, f


def _hgrn_intra(qh, kk, bq, rows_a, rows_b):
    ones = jnp.ones((HG_HEAD, HG_HEAD), MXU_DTYPE)
    es, stack_a, stack_b = [], [], []
    for s in range(HG_CHUNK):
        e = jnp.exp(jnp.minimum(bq - bq[s:s + 1, :], 0.0))
        es.append(e)
        stack_a.append(qh * e * kk[s:s + 1, :])
        if rows_a is not None:
            stack_b.append(rows_a * rows_b[s:s + 1, :])
    a_rep = _dot(jnp.concatenate(stack_a, axis=0), ones, _NN)
    d_rep = _dot(jnp.concatenate(stack_b, axis=0), ones, _NN) if rows_a is not None else None
    return es, a_rep, d_rep


def _hgrn_fwd(z, lb, hg_norm, ycat, *, name, tb=512):
    t = z.shape[0]
    mix_b = lb.shape[1]
    heads = mix_b // HG_HEAD
    off = (z.shape[1] - 4 * mix_b) // HG_HEAD
    tb = min(tb, t)
    ncb = tb // HG_CHUNK

    def body(q_ref, f_ref, i_ref, g_ref, lb_ref, hn_ref, ycat_in, y_ref, o_ref, st_ref,
             state, qh_s, k_s, b_s, qt_s, kt_s, gl_s, o_s):
        del ycat_in

        @pl.when(pl.program_id(1) == 0)
        def _():
            state[...] = jnp.zeros_like(state)

        _hgrn_prepare(q_ref, f_ref, lb_ref, qh_s, k_s, b_s, qt_s, kt_s, gl_s)
        row = lax.broadcasted_iota(jnp.int32, (HG_CHUNK, HG_HEAD), 0)

        def chunk(c, carry):
            rows = pl.ds(pl.multiple_of(c * HG_CHUNK, HG_CHUNK), HG_CHUNK)
            st = state[...]
            st_ref[0, c] = st
            vv = i_ref[rows, :]
            o = _dot(qt_s[rows, :], st, _NT)
            _, a_rep, _ = _hgrn_intra(qh_s[rows, :], k_s[rows, :], b_s[rows, :], None, None)
            for s in range(HG_CHUNK):
                o = o + jnp.where(row >= s, a_rep[s * HG_CHUNK:(s + 1) * HG_CHUNK, :] * vv[s:s + 1, :], 0.0)
            o_s[rows, :] = o
            state[...] = st * gl_s[rows, :][0:1, :] + _dot(vv, kt_s[rows, :], _TN)
            return carry

        lax.fori_loop(0, ncb, chunk, 0, unroll=2)
        o = o_s[...]
        o_ref[...] = o
        r = lax.rsqrt(jnp.mean(o * o, axis=-1, keepdims=True) + EPS)
        gv = g_ref[...]
        y_ref[...] = (o * r * hn_ref[...] * (gv * _sigmoid(gv))).astype(y_ref.dtype)

    def zcol(kind):
        return pl.BlockSpec((tb, HG_HEAD), lambda h, i: (i, off + kind * heads + h))

    scratch = [pltpu.VMEM((HG_HEAD, HG_HEAD), F32)] + [pltpu.VMEM((tb, HG_HEAD), F32)] * 7
    return pl.pallas_call(
        body, grid=(heads, t // tb),
        in_specs=[zcol(0), zcol(1), zcol(2), zcol(3), pl.BlockSpec((1, HG_HEAD), lambda h, i: (0, h)),
                  pl.BlockSpec((1, HG_HEAD), lambda h, i: (0, 0)), pl.BlockSpec(memory_space=pl.ANY)],
        out_specs=[pl.BlockSpec((tb, HG_HEAD), lambda h, i: (i, heads + h)),
                   pl.BlockSpec((tb, HG_HEAD), lambda h, i: (i, h)),
                   pl.BlockSpec((1, ncb, HG_HEAD, HG_HEAD), lambda h, i: (h, i, 0, 0))],
        out_shape=[jax.ShapeDtypeStruct(ycat.shape, ycat.dtype), jax.ShapeDtypeStruct((t, mix_b), F32),
                   jax.ShapeDtypeStruct((heads, t // HG_CHUNK, HG_HEAD, HG_HEAD), F32)],
        scratch_shapes=scratch, input_output_aliases={6: 0},
        compiler_params=_cp("parallel", "arbitrary"), name=name)(z, z, z, z, lb, hg_norm, ycat)


def _hgrn_bwd(z, lb, hg_norm, o_raw, states, dycat, *, name, tb=512):
    t = z.shape[0]
    mix_b = lb.shape[1]
    heads = mix_b // HG_HEAD
    off = (z.shape[1] - 4 * mix_b) // HG_HEAD
    tb = min(tb, t)
    ncb = tb // HG_CHUNK
    nt = t // tb

    def body(q_ref, f_ref, i_ref, g_ref, lb_ref, hn_ref, o_ref, st_ref, dy_ref,
             dq_ref, dfl_ref, di_ref, dg_ref, dlb_ref, dhn_ref,
             dstate, qh_s, k_s, b_s, qt_s, kt_s, gl_s, do_s, dqh_s, dk_s, db_s):
        first = pl.program_id(1) == 0

        @pl.when(first)
        def _():
            dstate[...] = jnp.zeros_like(dstate)
            dlb_ref[...] = jnp.zeros_like(dlb_ref)

        @pl.when(first & (pl.program_id(0) == 0))
        def _():
            dhn_ref[...] = jnp.zeros_like(dhn_ref)

        sg, f = _hgrn_prepare(q_ref, f_ref, lb_ref, qh_s, k_s, b_s, qt_s, kt_s, gl_s)
        o = o_ref[...]
        r = lax.rsqrt(jnp.mean(o * o, axis=-1, keepdims=True) + EPS)
        oh = o * r
        gv = g_ref[...]
        sgg = _sigmoid(gv)
        dy = dy_ref[...].astype(F32)
        hn = hn_ref[...]
        dg_ref[...] = (dy * oh * hn * (sgg * (1.0 + gv * (1.0 - sgg)))).astype(dg_ref.dtype)
        don = dy * (gv * sgg)
        dhn_ref[...] += jnp.sum(don * oh, axis=0, keepdims=True)
        doh = don * hn
        do_s[...] = r * (doh - oh * jnp.mean(doh * oh, axis=-1, keepdims=True))
        row = lax.broadcasted_iota(jnp.int32, (HG_CHUNK, HG_HEAD), 0)

        def chunk(ci, carry):
            c = ncb - 1 - ci
            rows = pl.ds(pl.multiple_of(c * HG_CHUNK, HG_CHUNK), HG_CHUNK)
            st_prev = st_ref[0, c]
            dst = dstate[...]
            qh, kk, bq, vv = qh_s[rows, :], k_s[rows, :], b_s[rows, :], i_ref[rows, :]
            qt, kt, doo = qt_s[rows, :], kt_s[rows, :], do_s[rows, :]
            gl = gl_s[rows, :][0:1, :]
            es, a_rep, d_rep = _hgrn_intra(qh, kk, bq, doo, vv)
            dqh = jnp.exp(bq) * _dot(doo, st_prev, _NN)
            dk = jnp.exp(bq[HG_CHUNK - 1:HG_CHUNK, :] - bq) * _dot(vv, dst, _NN)
            dv = _dot(kt, dst, _NT)
            for s in range(HG_CHUNK):
                blk = slice(s * HG_CHUNK, (s + 1) * HG_CHUNK)
                wgt = jnp.where(row >= s, d_rep[blk, :] * es[s], 0.0)
                dqh = dqh + wgt * kk[s:s + 1, :]
                dk = dk + jnp.where(row == s, jnp.sum(wgt * qh, axis=0, keepdims=True), 0.0)
                dv_row = jnp.sum(jnp.where(row >= s, a_rep[blk, :] * doo, 0.0), axis=0, keepdims=True)
                dv = dv + jnp.where(row == s, dv_row, 0.0)
            st_next = st_prev * gl + _dot(vv, kt, _TN)
            db = qh * dqh - kk * dk
            db = db + jnp.where(row == HG_CHUNK - 1, jnp.sum(st_next * dst, axis=0, keepdims=True), 0.0)
            dstate[...] = dst * gl + _dot(doo, qt, _TN)
            dqh_s[rows, :] = dqh
            dk_s[rows, :] = dk
            db_s[rows, :] = db
            di_ref[rows, :] = dv.astype(di_ref.dtype)
            return carry

        lax.fori_loop(0, ncb, chunk, 0, unroll=2)
        tri = _chunk_tri(False)
        lb_v = lb_ref[...]
        qv = q_ref[...]
        sgq = _sigmoid(qv)
        dq_ref[...] = (dqh_s[...] * (HG_HEAD ** -0.5) * (sgq * (1.0 + qv * (1.0 - sgq)))).astype(dq_ref.dtype)
        dlb = jnp.zeros((1, HG_HEAD), F32)
        for rr in range(tb // LANES):
            rws = slice(rr * LANES, (rr + 1) * LANES)
            dlogf = _dot_f32(tri, db_s[rws, :], _NN)
            df = dlogf / f[rws, :] - dk_s[rws, :]
            sgr = sg[rws, :]
            dfl_ref[rws, :] = (df * (1.0 - lb_v) * sgr * (1.0 - sgr)).astype(dfl_ref.dtype)
            dlb = dlb + jnp.sum(df * (1.0 - sgr), axis=0, keepdims=True)
        dlb_ref[...] += dlb

    def zcol(kind):
        return pl.BlockSpec((tb, HG_HEAD), lambda h, i: (nt - 1 - i, off + kind * heads + h))

    hcol = pl.BlockSpec((tb, HG_HEAD), lambda h, i: (nt - 1 - i, h))
    scratch = [pltpu.VMEM((HG_HEAD, HG_HEAD), F32)] + [pltpu.VMEM((tb, HG_HEAD), F32)] * 10
    out = jax.ShapeDtypeStruct((t, mix_b), MXU_DTYPE)
    return pl.pallas_call(
        body, grid=(heads, nt),
        in_specs=[zcol(0), zcol(1), zcol(2), zcol(3), pl.BlockSpec((1, HG_HEAD), lambda h, i: (0, h)),
                  pl.BlockSpec((1, HG_HEAD), lambda h, i: (0, 0)), hcol,
                  pl.BlockSpec((1, ncb, HG_HEAD, HG_HEAD), lambda h, i: (h, nt - 1 - i, 0, 0)),
                  pl.BlockSpec((tb, HG_HEAD), lambda h, i: (nt - 1 - i, heads + h))],
        out_specs=[hcol, hcol, hcol, hcol, pl.BlockSpec((1, HG_HEAD), lambda h, i: (0, h)),
                   pl.BlockSpec((1, HG_HEAD), lambda h, i: (0, 0))],
        out_shape=[out, out, out, out, jax.ShapeDtypeStruct((1, mix_b), F32),
                   jax.ShapeDtypeStruct((1, HG_HEAD), F32)],
        scratch_shapes=scratch, compiler_params=_cp("arbitrary", "arbitrary"),
        name=name)(z, z, z, z, lb, hg_norm, o_raw, states, dycat)


def _lb_fwd(lb_table, layer, *, name):
    rows, width = lb_table.shape

    def body(t_ref, o_ref):
        tv = t_ref[...]
        e = jnp.exp(tv - jnp.max(tv, axis=0, keepdims=True))
        sm = e / jnp.sum(e, axis=0, keepdims=True)
        o_ref[...] = jnp.sum(sm[1:layer + 2, :], axis=0, keepdims=True)

    return pl.pallas_call(body, out_shape=jax.ShapeDtypeStruct((1, width), F32), name=name)(lb_table)


def _lb_bwd(lb_table, dlb, layer, *, name):
    rows, width = lb_table.shape

    def body(t_ref, d_ref, o_ref):
        tv = t_ref[...]
        e = jnp.exp(tv - jnp.max(tv, axis=0, keepdims=True))
        sm = e / jnp.sum(e, axis=0, keepdims=True)
        ridx = lax.broadcasted_iota(jnp.int32, sm.shape, 0)
        dsm = jnp.where((ridx >= 1) & (ridx <= layer + 1), d_ref[...], 0.0)
        o_ref[...] = sm * (dsm - jnp.sum(sm * dsm, axis=0, keepdims=True))

    return pl.pallas_call(body, out_shape=jax.ShapeDtypeStruct((rows, width), F32), name=name)(lb_table, dlb)


FOX_BLOCK = 512


def _fox_prep(zf, b_f, *, name, blk=256):
    t = zf.shape[0]

    def body(z_ref, b_ref, fc_ref):
        r = lax.broadcasted_iota(jnp.int32, (blk, blk), 0)
        c = lax.broadcasted_iota(jnp.int32, (blk, blk), 1)
        tri = jnp.where(c <= r, 1.0, 0.0).astype(F32)
        carry = jnp.zeros((1, LANES), F32)
        for j in range(t // blk):
            rows = slice(j * blk, (j + 1) * blk)
            ls = jax.nn.log_sigmoid(z_ref[rows, :] + b_ref[...])
            fb = _dot_f32(tri, ls, _NN) + carry
            carry = fb[blk - 1:blk, :]
            fc_ref[rows, :] = fb

    return pl.pallas_call(
        body, out_shape=jax.ShapeDtypeStruct((t, LANES), F32),
        compiler_params=pltpu.CompilerParams(vmem_limit_bytes=VMEM_LIMIT_BYTES), name=name)(zf, b_f)


def _fox_head_column(fc_ref, fk_s, head):
    lane = lax.broadcasted_iota(jnp.int32, fc_ref.shape, 1)
    fk_s[...] = jnp.sum(jnp.where(lane == head, fc_ref[...], 0.0), axis=1, keepdims=True)


def _fox_scores(k_blk, q_blk, fk_blk, diagonal):
    s = _dot(k_blk, q_blk, _NT) * (FOX_HEAD ** -0.5) - fk_blk
    if diagonal:
        key = lax.broadcasted_iota(jnp.int32, s.shape, 0)
        qry = lax.broadcasted_iota(jnp.int32, s.shape, 1)
        s = jnp.where(key <= qry, s, -jnp.inf)
    return s


def _fox_fwd(zqkv, fcol, *, name):
    t = zqkv.shape[0]
    d = zqkv.shape[1] // 3
    heads = d // FOX_HEAD
    blk = min(FOX_BLOCK, t)
    nq = t // blk

    def body(q_ref, k_ref, v_ref, fc_ref, o_ref, lse_ref, fk_s):
        _fox_head_column(fc_ref, fk_s, pl.program_id(0))

        def q_block(qi, carry):
            qrows = pl.ds(pl.multiple_of(qi * blk, blk), blk)
            q_blk = q_ref[qrows, :]

            def update(st, krows, diagonal):
                m, l, acc = st
                s = _fox_scores(k_ref[krows, :], q_blk, fk_s[krows, :], diagonal)
                m_new = jnp.maximum(m, jnp.max(s, axis=0, keepdims=True))
                alpha = jnp.exp(m - m_new)
                p = jnp.exp(s - m_new)
                l = alpha * l + jnp.sum(p, axis=0, keepdims=True)
                acc = acc * alpha + _dot(v_ref[krows, :], p, _TN)
                return m_new, l, acc

            def k_block(kj, st):
                return update(st, pl.ds(pl.multiple_of(kj * blk, blk), blk), False)

            init = (jnp.full((1, blk), -jnp.inf, F32), jnp.zeros((1, blk), F32),
                    jnp.zeros((FOX_HEAD, blk), F32))
            m, l, acc = update(lax.fori_loop(0, qi, k_block, init), qrows, True)
            o_ref[qrows, :] = (acc / l).T.astype(o_ref.dtype)
            lse_ref[0, :, qrows] = m + jnp.log(l)
            return carry

        lax.fori_loop(0, nq, q_block, 0)

    def col(kind):
        return pl.BlockSpec((t, FOX_HEAD), lambda h: (0, kind * heads + h))

    rowvec = pl.BlockSpec((1, 1, t), lambda h: (h, 0, 0))
    return pl.pallas_call(
        body, grid=(heads,),
        in_specs=[col(0), col(1), col(2), pl.BlockSpec((t, LANES), lambda h: (0, 0))],
        out_specs=[pl.BlockSpec((t, FOX_HEAD), lambda h: (0, h)), rowvec],
        out_shape=[jax.ShapeDtypeStruct((t, d), MXU_DTYPE), jax.ShapeDtypeStruct((heads, 1, t), F32)],
        scratch_shapes=[pltpu.VMEM((t, 1), F32)],
        compiler_params=_cp("parallel"), name=name)(zqkv, zqkv, zqkv, fcol)


def _fox_bwd(zqkv, fcol, lse, o, do, *, name):
    t = zqkv.shape[0]
    d = zqkv.shape[1] // 3
    heads = d // FOX_HEAD
    blk = min(FOX_BLOCK, t)
    nq = t // blk
    scale = FOX_HEAD ** -0.5

    def body(q_ref, k_ref, v_ref, fc_ref, lse_ref, o_ref, do_ref,
             dq_ref, dk_ref, dv_ref, rq_ref, rk_ref, dq_s, drow_s, fk_s):
        _fox_head_column(fc_ref, fk_s, pl.program_id(0))
        ones = jnp.ones((8, blk), MXU_DTYPE)
        dq_s[...] = jnp.zeros_like(dq_s)
        rq_ref[...] = jnp.zeros_like(rq_ref)
        ones_f = jnp.ones((8, FOX_HEAD), F32)
        for j in range(nq):
            rows = slice(j * blk, (j + 1) * blk)
            prod = do_ref[rows, :].astype(F32) * o_ref[rows, :].astype(F32)
            drow_s[:, rows] = _dot_f32(ones_f, prod, _NT)

        def k_block(kj, carry):
            krows = pl.ds(pl.multiple_of(kj * blk, blk), blk)
            k_blk, v_blk, fk_blk = k_ref[krows, :], v_ref[krows, :], fk_s[krows, :]

            def pair(st, qrows, diagonal):
                dk, dv, rk = st
                q_blk, do_blk = q_ref[qrows, :], do_ref[qrows, :]
                s = _fox_scores(k_blk, q_blk, fk_blk, diagonal)
                p = jnp.exp(s - lse_ref[0, :, qrows])
                dv = dv + _dot(p, do_blk, _NN)
                dp = _dot(v_blk, do_blk, _NT)
                ds = (p * (dp - drow_s[0:1, qrows])).astype(MXU_DTYPE)
                dk = dk + _dot(ds, q_blk, _NN)
                dq_s[qrows, :] += _dot(ds, k_blk, _TN)
                rq_ref[0, :, qrows] += _dot(ones, ds, _NN)[0:1, :]
                rk = rk + _dot(ones, ds, _NT)
                return dk, dv, rk

            def q_block(qi, st):
                return pair(st, pl.ds(pl.multiple_of(qi * blk, blk), blk), False)

            init = (jnp.zeros((blk, FOX_HEAD), F32), jnp.zeros((blk, FOX_HEAD), F32), jnp.zeros((8, blk), F32))
            dk, dv, rk = lax.fori_loop(kj + 1, nq, q_block, pair(init, krows, True))
            dk_ref[krows, :] = (dk * scale).astype(dk_ref.dtype)
            dv_ref[krows, :] = dv.astype(dv_ref.dtype)
            rk_ref[0, :, krows] = rk[0:1, :]
            return carry

        lax.fori_loop(0, nq, k_block, 0)
        dq_ref[...] = (dq_s[...] * scale).astype(dq_ref.dtype)

    def col(kind):
        return pl.BlockSpec((t, FOX_HEAD), lambda h: (0, kind * heads + h))

    hcol = pl.BlockSpec((t, FOX_HEAD), lambda h: (0, h))
    rowvec = pl.BlockSpec((1, 1, t), lambda h: (h, 0, 0))
    out = jax.ShapeDtypeStruct((t, d), MXU_DTYPE)
    vec = jax.ShapeDtypeStruct((heads, 1, t), F32)
    return pl.pallas_call(
        body, grid=(heads,),
        in_specs=[col(0), col(1), col(2), pl.BlockSpec((t, LANES), lambda h: (0, 0)), rowvec, hcol, hcol],
        out_specs=[hcol, hcol, hcol, rowvec, rowvec],
        out_shape=[out, out, out, vec, vec],
        scratch_shapes=[pltpu.VMEM((t, FOX_HEAD), F32), pltpu.VMEM((8, t), F32), pltpu.VMEM((t, 1), F32)],
        compiler_params=_cp("parallel"), name=name)(zqkv, zqkv, zqkv, fcol, lse, o, do)


def _fox_gate_bwd(rq, rk, zf, b_f, *, name, blk=256):
    heads, t = rq.shape

    def body(rq_ref, rk_ref, z_ref, b_ref, dfl_ref, db_ref):
        r = lax.broadcasted_iota(jnp.int32, (blk, blk), 0)
        c = lax.broadcasted_iota(jnp.int32, (blk, blk), 1)
        tri = jnp.where(r >= c, 1.0, 0.0).astype(F32)
        carry = jnp.zeros((heads, 1), F32)
        db = jnp.zeros((1, LANES), F32)
        pad = jnp.zeros((LANES - heads, blk), F32)
        for j in reversed(range(t // blk)):
            cols = slice(j * blk, (j + 1) * blk)
            df = rq_ref[:, cols] - rk_ref[:, cols]
            dls = _dot_f32(df, tri, _NN) + carry
            carry = dls[:, 0:1]
            dls_t = jnp.concatenate([dls, pad], axis=0).T
            dfl = dls_t * _sigmoid(-(z_ref[cols, :] + b_ref[...]))
            dfl_ref[cols, :] = dfl.astype(dfl_ref.dtype)
            db = db + jnp.sum(dfl, axis=0, keepdims=True)
        db_ref[...] = db

    return pl.pallas_call(
        body, out_shape=[jax.ShapeDtypeStruct((t, LANES), MXU_DTYPE), jax.ShapeDtypeStruct((1, LANES), F32)],
        compiler_params=pltpu.CompilerParams(vmem_limit_bytes=VMEM_LIMIT_BYTES), name=name)(rq, rk, zf, b_f)


def _adamw(w, m, v, parts, *, name, layer=None, prev=None, tr=128):
    lcnt, r, c = w.shape
    p = parts.shape[0]
    li = 0 if layer is None else layer
    tr = _tile(r, tr, 16)
    has_prev = prev is not None

    def body(*refs):
        w_ref, m_ref, v_ref, p_ref = refs[:4]
        g_ref, d_ref, nm_ref, nv_ref = refs[-4:]
        g = p_ref[0].astype(F32)
        for j in range(1, p):
            g = g + p_ref[j].astype(F32)
        wv = w_ref[0]
        mn = ADAM_B1 * m_ref[0] + (1.0 - ADAM_B1) * g
        vn = ADAM_B2 * v_ref[0] + (1.0 - ADAM_B2) * (g * g)
        m_hat = mn / (1.0 - ADAM_B1 ** ADAM_STEP)
        v_hat = vn / (1.0 - ADAM_B2 ** ADAM_STEP)
        g_ref[0] = g
        d_ref[0] = -ADAM_LR * (m_hat / (jnp.sqrt(v_hat) + ADAM_EPS) + ADAM_WD * wv)
        nm_ref[0] = mn
        nv_ref[0] = vn

    slab = pl.BlockSpec((1, tr, c), lambda i: (li, i, 0))
    in_specs = [slab, slab, slab, pl.BlockSpec((p, tr, c), lambda i: (0, i, 0))]
    operands = [w, m, v, parts]
    aliases = {}
    if has_prev:
        in_specs += [pl.BlockSpec(memory_space=pl.ANY)] * 4
        operands += list(prev)
        aliases = {4: 0, 5: 1, 6: 2, 7: 3}
    shp = jax.ShapeDtypeStruct((lcnt, r, c), F32)
    return pl.pallas_call(
        body, grid=(r // tr,), in_specs=in_specs, out_specs=[slab] * 4, out_shape=[shp] * 4,
        input_output_aliases=aliases, compiler_params=_cp("parallel"), name=name)(*operands)


def _my_place():
    return lax.axis_index("x"), lax.axis_index("y"), lax.axis_index("c")


def _slot(p):
    return 4 * p[0] + 2 * p[1] + p[2]


def _peer(me, mask):
    x, y, c = me
    return (1 - x if mask & 4 else x, 1 - y if mask & 2 else y, 1 - c if mask & 1 else c)


_HBM = pl.BlockSpec(memory_space=pltpu.HBM)
_SEM = pl.BlockSpec(memory_space=pltpu.SEMAPHORE)
_ANY = pl.BlockSpec(memory_space=pl.ANY)
_EFFECT = pltpu.SideEffectType.DATAFLOW_SIDE_EFFECTING


def _push_copy(src_refs, land_refs, send_sems, recv_sems, a, mask, me, per_peer, outgoing):
    peer = _peer(me, mask)
    src = src_refs[a].at[_slot(peer)] if per_peer else src_refs[a]
    dst = land_refs[a].at[_slot(me) if outgoing else _slot(peer)]
    k = a * (N_DEV - 1) + mask - 1
    return pltpu.make_async_remote_copy(
        src_ref=src, dst_ref=dst, send_sem=send_sems.at[k], recv_sem=recv_sems.at[k],
        device_id=peer, device_id_type=MESH)


ALL_PEERS = tuple(range(1, N_DEV))
CHIP_PEERS = (2, 4, 6)
FIRST_HOP = (1,) + CHIP_PEERS


def _push_start(srcs, dep, *, per_peer, name, masks=ALL_PEERS):
    n = len(srcs)
    mine = _slot(_my_place())
    lands = []
    for s in srcs:
        own = lax.dynamic_index_in_dim(s, mine, 0, keepdims=True) if per_peer else s[None]
        shape = s.shape if per_peer else (N_DEV,) + s.shape
        lands.append(lax.dynamic_update_slice_in_dim(lax.empty(shape, s.dtype), own, mine, 0))
    has_dep = dep is not None

    def body(*refs):
        src_refs, land_refs = refs[:n], refs[n:2 * n]
        send_sems, recv_sems = refs[2 * n + has_dep], refs[2 * n + has_dep + 1]
        token = refs[-1]
        me = _my_place()
        for a in range(n):
            for mask in masks:
                _push_copy(src_refs, land_refs, send_sems, recv_sems, a, mask, me, per_peer, True).start()
        token[...] = jnp.zeros_like(token)

    hbm_in = [pltpu.with_memory_space_constraint(v, pltpu.HBM) for v in list(srcs) + lands]
    out = pl.pallas_call(
        body, name=name,
        out_shape=(pltpu.SemaphoreType.DMA((n * (N_DEV - 1),)), pltpu.SemaphoreType.DMA((n * (N_DEV - 1),)),
                   *[pltpu.HBM(v.shape, v.dtype) for v in hbm_in], jax.ShapeDtypeStruct((8, LANES), F32)),
        in_specs=[_HBM] * (2 * n) + ([_ANY] if has_dep else []),
        out_specs=(_SEM, _SEM, *[_HBM] * (2 * n), pl.BlockSpec(memory_space=pltpu.VMEM)),
        input_output_aliases={i: 2 + i for i in range(2 * n)},
        compiler_params=pltpu.CompilerParams(has_side_effects=_EFFECT),
    )(*hbm_in, *([dep] if has_dep else []))
    return (n, per_peer, masks, out[:-1]), out[-1]


def _push_wait(handle, after, *, name):
    n, per_peer, masks, (send_sems, recv_sems, *bufs) = handle

    def body(*refs):
        src_refs, land_refs = refs[:n], refs[n:2 * n]
        send_sems, recv_sems = refs[2 * n], refs[2 * n + 1]
        me = _my_place()
        for a in range(n):
            for mask in masks:
                cp = _push_copy(src_refs, land_refs, send_sems, recv_sems, a, mask, me, per_peer, False)
                cp.wait_send()
                cp.wait_recv()

    out = pl.pallas_call(
        body, name=name, out_shape=tuple(pltpu.HBM(v.shape, v.dtype) for v in bufs),
        in_specs=[_HBM] * (2 * n) + [_SEM, _SEM, _ANY], out_specs=tuple([_HBM] * (2 * n)),
        input_output_aliases={i: i for i in range(2 * n)},
        compiler_params=pltpu.CompilerParams(has_side_effects=_EFFECT),
    )(*bufs, send_sems, recv_sems, after)
    return list(out[n:])


def _relay_copy(land_refs, send_sems, recv_sems, a, j, me, outgoing):
    sibling = _peer(me, 1)
    out_slot = _slot(_peer(me, CHIP_PEERS[j]))
    in_slot = _slot(_peer(sibling, CHIP_PEERS[j]))
    k = a * len(CHIP_PEERS) + j
    return pltpu.make_async_remote_copy(
        src_ref=land_refs[a].at[out_slot], dst_ref=land_refs[a].at[out_slot if outgoing else in_slot],
        send_sem=send_sems.at[k], recv_sem=recv_sems.at[k], device_id=sibling, device_id_type=MESH)


def _relay_start(lands, *, name):
    n = len(lands)

    def body(*refs):
        land_refs, send_sems, recv_sems, token = refs[:n], refs[n], refs[n + 1], refs[-1]
        me = _my_place()
        for a in range(n):
            for j in range(len(CHIP_PEERS)):
                _relay_copy(land_refs, send_sems, recv_sems, a, j, me, True).start()
        token[...] = jnp.zeros_like(token)

    hbm_in = [pltpu.with_memory_space_constraint(v, pltpu.HBM) for v in lands]
    n_sem = n * len(CHIP_PEERS)
    out = pl.pallas_call(
        body, name=name,
        out_shape=(pltpu.SemaphoreType.DMA((n_sem,)), pltpu.SemaphoreType.DMA((n_sem,)),
                   *[pltpu.HBM(v.shape, v.dtype) for v in hbm_in], jax.ShapeDtypeStruct((8, LANES), F32)),
        in_specs=[_HBM] * n, out_specs=(_SEM, _SEM, *[_HBM] * n, pl.BlockSpec(memory_space=pltpu.VMEM)),
        input_output_aliases={i: 2 + i for i in range(n)},
        compiler_params=pltpu.CompilerParams(has_side_effects=_EFFECT),
    )(*hbm_in)
    return (n, out[:-1]), out[-1]


def _relay_wait(handle, after, *, name):
    n, (send_sems, recv_sems, *bufs) = handle

    def body(*refs):
        land_refs, send_sems, recv_sems = refs[:n], refs[n], refs[n + 1]
        me = _my_place()
        for a in range(n):
            for j in range(len(CHIP_PEERS)):
                cp = _relay_copy(land_refs, send_sems, recv_sems, a, j, me, False)
                cp.wait_send()
                cp.wait_recv()

    out = pl.pallas_call(
        body, name=name, out_shape=tuple(pltpu.HBM(v.shape, v.dtype) for v in bufs),
        in_specs=[_HBM] * n + [_SEM, _SEM, _ANY], out_specs=tuple([_HBM] * n),
        input_output_aliases={i: i for i in range(n)},
        compiler_params=pltpu.CompilerParams(has_side_effects=_EFFECT),
    )(*bufs, send_sems, recv_sems, after)
    return list(out)


def _all_reduce_rows(v, *, name):
    r, c = v.shape

    def body(v_ref, o_ref, buf, send_sems, recv_sems):
        me = _my_place()
        mine = _slot(me)
        sends = []
        for mask in range(1, N_DEV):
            peer = _peer(me, mask)
            sends.append(pltpu.make_async_remote_copy(
                src_ref=v_ref, dst_ref=buf.at[mine], send_sem=send_sems.at[mask - 1],
                recv_sem=recv_sems.at[mask - 1], device_id=peer, device_id_type=MESH))
        for cp in sends:
            cp.start()
        buf[mine] = v_ref[...]
        for mask in range(1, N_DEV):
            peer = _peer(me, mask)
            pltpu.make_async_remote_copy(
                src_ref=v_ref, dst_ref=buf.at[_slot(peer)], send_sem=send_sems.at[mask - 1],
                recv_sem=recv_sems.at[mask - 1], device_id=peer, device_id_type=MESH).wait_recv()
        for cp in sends:
            cp.wait_send()
        total = buf[0]
        for j in range(1, N_DEV):
            total = total + buf[j]
        o_ref[...] = total

    vm = pl.BlockSpec(memory_space=pltpu.VMEM)
    return pl.pallas_call(
        body, in_specs=[vm], out_specs=vm, out_shape=jax.ShapeDtypeStruct((r, c), F32),
        scratch_shapes=[pltpu.VMEM((N_DEV, r, c), F32), pltpu.SemaphoreType.DMA((7,)),
                        pltpu.SemaphoreType.DMA((7,))],
        name=name)(v)


def _xa_fwd(x, mem, g_x, g_m, wq, wkv, wo, tag):
    hx = _rms_fwd(x, g_x, name=f"xa{tag}_norm")
    memn = _rms_fwd(mem, g_m, name=f"xa{tag}_mem_norm")
    q = _mm2(hx, wq, "nn", MXU_DTYPE, name=f"xa{tag}_q")
    kv = _mm2(memn, wkv, "nn", MXU_DTYPE, name=f"xa{tag}_kv", b_split=True)
    o = _xattn_fwd(q, kv, name=f"xa{tag}_attn")
    return _mm2(o, wo, "nn", F32, name=f"xa{tag}_out", add=x), (hx, memn, q, kv, o)


def _xa_bwd(x, mem, g_x, g_m, wq, wkv, wo, saved, dxo, dxo_lo, tag, put):
    hx, memn, q, kv, o = saved
    do = _mm2(dxo_lo, wo, "nt", MXU_DTYPE, name=f"xa{tag}_do")
    dwo = _mm2(o, dxo_lo, "tn", MXU_DTYPE, name=f"xa{tag}_dwo")
    dq, dkv = _xattn_bwd(q, kv, do, name=f"xa{tag}_attn_bwd")
    dwq = _mm2(hx, dq, "tn", MXU_DTYPE, name=f"xa{tag}_dwq")
    dwkv = _mm2(memn, dkv, "tn", MXU_DTYPE, name=f"xa{tag}_dwkv", o_split=True, tn=wkv.shape[2])
    tok = put((dwq, dwkv, dwo))
    dhx = _mm2(dq, wq, "nt", F32, name=f"xa{tag}_dh", dep=tok)
    dx, dx_lo, dgx = _rms_bwd(x, g_x, dhx, dxo, name=f"xa{tag}_norm_bwd")
    dmemn = _mm2(dkv, wkv, "nt", F32, name=f"xa{tag}_dmem", b_split=True)
    _, _, dgm = _rms_bwd(mem, g_m, dmemn, None, name=f"xa{tag}_mem_norm_bwd")
    return dx, dx_lo, dgx, dgm


def _ffn_fwd(x, g, wg, wu, wd, tag):
    h = _rms_fwd(x, g, name=f"ffn{tag}_norm")
    gate, up, act = _ffn_up(h, wg, wu, name=f"ffn{tag}_up")
    x_new = _mm(act, wd, "nn", F32, name=f"ffn{tag}_down", reduce_b=True, add=x[None])[0]
    return x_new, (h, gate, up, act)


def _ffn_bwd(x, g, wg, wu, wd, saved, dxo, dxo_lo, tag, put):
    h, gate, up, act = saved
    dwd = _mm(act, dxo_lo[None], "tn", MXU_DTYPE, name=f"ffn{tag}_dwd")
    dgate, dup = _ffn_dact(dxo_lo, wd, gate, up, name=f"ffn{tag}_dact")
    dwg = _mm(h[None], dgate, "tn", MXU_DTYPE, name=f"ffn{tag}_dwg")
    dwu = _mm(h[None], dup, "tn", MXU_DTYPE, name=f"ffn{tag}_dwu")
    tok = put((dwg, dwu, dwd))
    dh = _mm(dgate, wg, "nt", F32, name=f"ffn{tag}_dh_gate", reduce_b=True, dep=tok)
    dh = _mm(dup, wu, "nt", F32, name=f"ffn{tag}_dh_up", reduce_b=True, add=dh)[0]
    dx, dx_lo, dg = _rms_bwd(x, g, dh, dxo, name=f"ffn{tag}_norm_bwd")
    return dx, dx_lo, dg


def _even_fwd(x, g, lb, w_in, w_pool, pool_scale, hg_norm, w_out):
    h = _rms_fwd(x, g, name="ev_norm")
    z = _mm2(h, w_in, "nn", F32, name="ev_in", b_split=True)
    ycat = _pool_fwd(z, w_pool, pool_scale, name="ev_pool")
    ycat, o_raw, states = _hgrn_fwd(z, lb, hg_norm, ycat, name="ev_hgrn")
    return _mm2(ycat, w_out, "nn", F32, name="ev_out", add=x), (h, z, ycat, o_raw, states)


def _even_bwd(x, g, lb, w_in, w_pool, pool_scale, hg_norm, w_out, saved, dxo, dxo_lo, put):
    h, z, ycat, o_raw, states = saved
    dycat = _mm2(dxo_lo, w_out, "nt", MXU_DTYPE, name="ev_dy")
    dw_out = _mm2(ycat, dxo_lo, "tn", MXU_DTYPE, name="ev_dw_out")
    du, dw_pool, dscale = _pool_bwd(z, w_pool, pool_scale, dycat, name="ev_pool_bwd")
    dq, dfl, di, dg, dlb, dhn = _hgrn_bwd(z, lb, hg_norm, o_raw, states, dycat, name="ev_hgrn_bwd")
    dz = jnp.concatenate([du, dq, dfl, di, dg], axis=1)
    dw_in = _mm2(h, dz, "tn", MXU_DTYPE, name="ev_dw_in", o_split=True, tn=w_in.shape[2])
    tok = put((dw_in, dw_pool, dw_out))
    dh = _mm2(dz, w_in, "nt", F32, name="ev_dh", b_split=True, dep=tok)
    dx, dx_lo, dgn = _rms_bwd(x, g, dh, dxo, name="ev_norm_bwd")
    return dx, dx_lo, dict(ev_norm=dgn, ev_pool_scale=dscale, ev_hg_norm=dhn, lb=dlb)


def _odd_fwd(x, g, w_qkv, w_f, b_f, w_out):
    h = _rms_fwd(x, g, name="od_norm")
    zqkv = _mm2(h, w_qkv, "nn", MXU_DTYPE, name="od_qkv")
    zf = _mm2(h, w_f, "nn", F32, name="od_gate")
    fcol = _fox_prep(zf, b_f, name="od_fox_prep")
    o, lse = _fox_fwd(zqkv, fcol, name="od_fox")
    return _mm2(o, w_out, "nn", F32, name="od_out", add=x), (h, zqkv, zf, fcol, o, lse)


def _odd_bwd(x, g, w_qkv, w_f, b_f, w_out, saved, dxo, dxo_lo, put):
    h, zqkv, zf, fcol, o, lse = saved
    do = _mm2(dxo_lo, w_out, "nt", MXU_DTYPE, name="od_do")
    dw_out = _mm2(o, dxo_lo, "tn", MXU_DTYPE, name="od_dw_out")
    dq, dk, dv, rq, rk = _fox_bwd(zqkv, fcol, lse, o, do, name="od_fox_bwd")
    dfl, db_f = _fox_gate_bwd(rq[:, 0, :], rk[:, 0, :], zf, b_f, name="od_fox_gate_bwd")
    dz = jnp.concatenate([dq, dk, dv], axis=1)
    dw_qkv = _mm2(h, dz, "tn", MXU_DTYPE, name="od_dw_qkv")
    dw_f = _mm2(h, dfl, "tn", MXU_DTYPE, name="od_dw_gate")
    tok = put((dw_qkv, dw_f, dw_out))
    dh = _mm2(dz, w_qkv, "nt", F32, name="od_dh_qkv", dep=tok)
    dh = _mm2(dfl, w_f, "nt", F32, name="od_dh_gate", add=dh)
    dx, dx_lo, dgn = _rms_bwd(x, g, dh, dxo, name="od_norm_bwd")
    return dx, dx_lo, dict(od_norm=dgn, od_b_f=db_f)


def _local_step(x, mem, target, sp, get_w, put_dw):
    b_f = jnp.pad(sp["od_b_f"], ((0, 0), (0, LANES - sp["od_b_f"].shape[1])))
    lb = _lb_fwd(sp["lb_table"], 0, name="lb_fwd")
    fin = sp["final_norm"].reshape(1, -1)
    xn, xm, fn = sp["xa_norm"], sp["xa_mem_norm"], sp["ffn_norm"]
    w_ev = get_w("ev", None)
    x1, s_ev = _even_fwd(x, sp["ev_norm"], lb, w_ev[0], w_ev[1], sp["ev_pool_scale"], sp["ev_hg_norm"], w_ev[2])
    w_xa0 = get_w("xa0", x1)
    x2, s_xa0 = _xa_fwd(x1, mem, xn[0:1], xm[0:1], *w_xa0, 0)
    w_ff0 = get_w("ffn0", x2)
    x3, s_ff0 = _ffn_fwd(x2, fn[0:1], *w_ff0, 0)
    w_qkv, w_f, w_od_out, od_norm = get_w("od", x3)
    x4, s_od = _odd_fwd(x3, od_norm, w_qkv, w_f, b_f, w_od_out)
    w_xa1 = get_w("xa1", x4)
    x5, s_xa1 = _xa_fwd(x4, mem, xn[1:2], xm[1:2], *w_xa1, 1)
    w_ff1 = get_w("ffn1", x5)
    x6, s_ff1 = _ffn_fwd(x5, fn[1:2], *w_ff1, 1)
    loss, dx, dx_lo, d_fin = _loss_head(x6, fin, target, name="loss_head")
    put = lambda grp: functools.partial(put_dw, grp)
    dx, dx_lo, d_ffn1 = _ffn_bwd(x5, fn[1:2], *w_ff1, s_ff1, dx, dx_lo, 1, put("ffn1"))
    dx, dx_lo, d_xa1, d_xm1 = _xa_bwd(x4, mem, xn[1:2], xm[1:2], *w_xa1, s_xa1, dx, dx_lo, 1, put("xa1"))
    dx, dx_lo, d_od = _odd_bwd(x3, od_norm, w_qkv, w_f, b_f, w_od_out, s_od, dx, dx_lo, put("od"))
    dx, dx_lo, d_ffn0 = _ffn_bwd(x2, fn[0:1], *w_ff0, s_ff0, dx, dx_lo, 0, put("ffn0"))
    dx, dx_lo, d_xa0, d_xm0 = _xa_bwd(x1, mem, xn[0:1], xm[0:1], *w_xa0, s_xa0, dx, dx_lo, 0, put("xa0"))
    dx, _, d_ev = _even_bwd(x, sp["ev_norm"], lb, w_ev[0], w_ev[1], sp["ev_pool_scale"], sp["ev_hg_norm"],
                            w_ev[2], s_ev, dx, dx_lo, put("ev"))
    small = dict(
        lb_table=_lb_bwd(sp["lb_table"], d_ev["lb"], 0, name="lb_bwd"),
        ev_norm=d_ev["ev_norm"], ev_pool_scale=d_ev["ev_pool_scale"], ev_hg_norm=d_ev["ev_hg_norm"],
        od_norm=d_od["od_norm"], od_b_f=d_od["od_b_f"][:, :sp["od_b_f"].shape[1]],
        xa_norm=jnp.concatenate([d_xa0, d_xa1], axis=0), xa_mem_norm=jnp.concatenate([d_xm0, d_xm1], axis=0),
        ffn_norm=jnp.concatenate([d_ffn0, d_ffn1], axis=0), final_norm=d_fin.reshape(-1))
    return loss, dx, small


_SMALL = ("lb_table", "ev_norm", "ev_pool_scale", "ev_hg_norm", "od_norm", "od_b_f", "xa_norm", "xa_mem_norm",
          "ffn_norm", "final_norm")
_WEIGHTS = ("lb_table", "ev_norm", "ev_w_in", "ev_w_pool", "ev_pool_scale", "ev_hg_norm", "ev_w_out", "od_norm",
            "od_w_in", "od_b_f", "od_w_out", "xa_norm", "xa_mem_norm", "xa_wq", "xa_wkv", "xa_wo", "ffn_norm",
            "ffn_w_gate", "ffn_w_up", "ffn_w_down", "final_norm")


def _lo(a):
    return a.astype(MXU_DTYPE)


def _rows(v):
    flat = v.reshape(-1)
    return jnp.pad(flat, (0, (-flat.shape[0]) % LANES)).reshape(-1, LANES)


def kernel(x, mem, lb_table, ev_norm, ev_w_in, ev_w_pool, ev_pool_scale, ev_hg_norm, ev_w_out, od_norm, od_w_in, od_b_f, od_w_out, xa_norm, xa_mem_norm, xa_wq, xa_wkv, xa_wo, ffn_norm, ffn_w_gate, ffn_w_up, ffn_w_down, final_norm, loss_target, m_lb_table, m_ev_norm, m_ev_w_in, m_ev_w_pool, m_ev_pool_scale, m_ev_hg_norm, m_ev_w_out, m_od_norm, m_od_w_in, m_od_b_f, m_od_w_out, m_xa_norm, m_xa_mem_norm, m_xa_wq, m_xa_wkv, m_xa_wo, m_ffn_norm, m_ffn_w_gate, m_ffn_w_up, m_ffn_w_down, m_final_norm, v_lb_table, v_ev_norm, v_ev_w_in, v_ev_w_pool, v_ev_pool_scale, v_ev_hg_norm, v_ev_w_out, v_od_norm, v_od_w_in, v_od_b_f, v_od_w_out, v_xa_norm, v_xa_mem_norm, v_xa_wq, v_xa_wkv, v_xa_wo, v_ffn_norm, v_ffn_w_gate, v_ffn_w_up, v_ffn_w_down, v_final_norm):
    arg = dict(locals())
    d = x.shape[-1]
    layers = xa_wq.shape[0]
    me = _slot(_my_place())

    n_gate = od_b_f.shape[1]
    shards = dict(ev=[_lo(ev_w_in[0]), _lo(ev_w_pool[0]), _lo(ev_w_out[0])],
                  od=[_lo(od_w_in[0]), _lo(od_w_out[0]), od_norm])
    for l in range(layers):
        shards[f"xa{l}"] = [_lo(xa_wq[l]), _lo(xa_wkv[l]), _lo(xa_wo[l])]
        shards[f"ffn{l}"] = [_lo(ffn_w_gate[l]), _lo(ffn_w_up[l]), _lo(ffn_w_down[l])]
    order = ("ev", "xa0", "ffn0", "od", "xa1", "ffn1")
    gathers, relays, tok = {}, {}, None
    for grp in order:
        gathers[grp], tok = _push_start(shards[grp], tok, per_peer=False, masks=FIRST_HOP,
                                        name=f"gather_{grp}_start")
    last_start = tok

    def second_hop(grp, after):
        lands = _push_wait(gathers[grp], after, name=f"gather_{grp}_wait")
        relays[grp], token = _relay_start(lands, name=f"gather_{grp}_relay")
        return token

    def get_w(grp, after):
        i = order.index(grp)
        after = last_start if after is None else after
        if i == 0:
            after = second_hop(grp, after)
        if i + 1 < len(order):
            after = second_hop(order[i + 1], after)
        got = _relay_wait(relays[grp], after, name=f"gather_{grp}_relay_wait")
        if grp == "ev":
            w_in, w_pool, w_out = got
            w_pool = jnp.transpose(w_pool, (1, 0, 2, 3)).reshape(w_pool.shape[1], -1, w_pool.shape[3])
            return w_in, w_pool, w_out.reshape(d, d)
        if grp == "od":
            w_in, w_out, nrm = got
            w_in = jnp.transpose(w_in, (1, 0, 2)).reshape(d, -1)
            w_f = jnp.pad(w_in[:, w_in.shape[1] - n_gate:], ((0, 0), (0, LANES - n_gate)))
            return w_in[:, :w_in.shape[1] - n_gate], w_f, w_out.reshape(d, d), nrm.reshape(1, d)
        if grp.startswith("xa"):
            return got[0].reshape(d, d), got[1], got[2].reshape(d, d)
        return tuple(got)

    def row_parts(g):
        return g.reshape(N_DEV, -1, g.shape[-1])

    scatters = {}

    def put_dw(grp, dws):
        if grp == "ev":
            dw_in, dw_pool, dw_out = dws
            gc = dw_pool.shape[1] // N_DEV
            dw_pool = _lo(jnp.transpose(dw_pool.reshape(dw_pool.shape[0], N_DEV, gc, -1), (1, 0, 2, 3)))
            parts = [dw_in, dw_pool, row_parts(dw_out)]
        elif grp == "od":
            dw_qkv, dw_f, dw_out = dws
            dw_in = jnp.concatenate([dw_qkv, dw_f[:, :n_gate]], axis=1)
            parts = [jnp.transpose(dw_in.reshape(d, N_DEV, -1), (1, 0, 2)), row_parts(dw_out)]
        elif grp.startswith("xa"):
            parts = [row_parts(dws[0]), dws[1], row_parts(dws[2])]
        else:
            parts = list(dws)
        scatters[grp], token = _push_start(parts, None, per_peer=True, name=f"scatter_{grp}_start")
        return token

    sp = {k: arg[k] for k in _SMALL if k != "od_norm"}
    loss, dx, small = _local_step(x[0], mem[0], loss_target[0], sp, get_w, put_dw)

    pieces = [_rows(small[k]) for k in _SMALL]
    packed = jnp.concatenate(pieces, axis=0)
    packed = jnp.pad(packed, ((0, (-packed.shape[0]) % 8), (0, 0)))
    total = _all_reduce_rows(packed, name="all_reduce_small")
    small_g, at = {}, 0
    for k, pc in zip(_SMALL, pieces):
        n = small[k].size
        small_g[k] = total[at:at + pc.shape[0]].reshape(-1)[:n].reshape(small[k].shape)
        at += pc.shape[0]
    small_g["od_norm"] = lax.dynamic_slice_in_dim(small_g["od_norm"], me * od_norm.shape[1], od_norm.shape[1], axis=1)

    res = {}
    for k in _SMALL:
        w, m, v = arg[k], arg["m_" + k], arg["v_" + k]
        shp = (1, 1, w.shape[0]) if w.ndim == 1 else (1,) + w.shape
        out = _adamw(w.reshape(shp), m.reshape(shp), v.reshape(shp), small_g[k].reshape(shp), name=f"adamw_{k}")
        res[k] = [o.reshape(w.shape) for o in out]
    members = dict(ev=("ev_w_in", "ev_w_pool", "ev_w_out"), od=("od_w_in", "od_w_out"),
                   xa=("xa_wq", "xa_wkv", "xa_wo"), ffn=("ffn_w_gate", "ffn_w_up", "ffn_w_down"))
    after, stacked = dx, {}
    for grp in ("ffn1", "xa1", "od", "ffn0", "xa0", "ev"):
        got = _push_wait(scatters[grp], after, name=f"scatter_{grp}_wait")
        kind = grp.rstrip("01")
        for k, parts in zip(members[kind], got):
            w, m, v = arg[k], arg["m_" + k], arg["v_" + k]
            if w.shape[0] == 1:
                shp = (1, -1, w.shape[-1])
                out = _adamw(w.reshape(shp), m.reshape(shp), v.reshape(shp), parts.reshape(N_DEV, -1, w.shape[-1]),
                             name=f"adamw_{k}")
            else:
                out = _adamw(w, m, v, parts, name=f"adamw_{k}{grp[-1]}", layer=int(grp[-1]), prev=stacked.get(k))
                stacked[k] = out
            res[k] = [o.reshape(w.shape) for o in out]
            after = out[3][:1, :8, :LANES]

    loss = lax.psum(loss[0, 0], ("x", "y", "c"))
    outs = [loss, dx[None]]
    for j in range(4):
        outs += [res[k][j] for k in _WEIGHTS]
    return tuple(outs)
```

```python
import functools

import jax
import jax.numpy as jnp
from jax import lax
from jax.experimental import pallas as pl
from jax.experimental.pallas import tpu as pltpu

F32 = jnp.float32
MXU_DTYPE = jnp.bfloat16
EPS = 1e-6
N_DEV = 8
V7X_VMEM_BYTES = 64 * 1024 * 1024
VMEM_LIMIT_BYTES = V7X_VMEM_BYTES - 8 * 1024 * 1024
LANES = 128
HIGHEST = lax.Precision.HIGHEST
MESH = pl.DeviceIdType.MESH

HG_HEAD = 128
HG_CHUNK = 16
FOX_HEAD = 128
XA_HEADS = 4
POOL_GROUPS = 4

ADAM_LR = 0.001
ADAM_B1 = 0.9
ADAM_B2 = 0.999
ADAM_EPS = 1e-08
ADAM_WD = 0.01
ADAM_STEP = 10

_NN = ((1,), (0,))
_NT = ((1,), (1,))
_TN = ((0,), (0,))


def _dot(a, b, dims):
    return lax.dot_general(a.astype(MXU_DTYPE), b.astype(MXU_DTYPE), (dims, ((), ())),
                           preferred_element_type=F32)


def _dot_f32(a, b, dims):
    return lax.dot_general(a, b, (dims, ((), ())), preferred_element_type=F32, precision=HIGHEST)


def _cp(*sem):
    return pltpu.CompilerParams(dimension_semantics=sem, vmem_limit_bytes=VMEM_LIMIT_BYTES)


def _tile(n, pref, align=LANES):
    if n <= pref:
        return n
    t = (pref // align) * align
    while t >= align:
        if n % t == 0:
            return t
        t -= align
    return n


def _sigmoid(x):
    return jax.nn.sigmoid(x)


def _mm(a, b, mode, out_dtype, *, name, add=None, dep=None, reduce_b=False, b_split=False, o_split=False,
        n_b=None, tm=1024, tn=1024, tk=2048):
    ba, bb = a.shape[0], b.shape[0]
    if mode == "tn":
        kdim, m = a.shape[1], a.shape[2]
        tk = 2 * tk
    else:
        m, kdim = a.shape[1], a.shape[2]
    if b_split:
        s_cnt, b_rows, w = b.shape
        if mode == "nt":
            n = b_rows
            assert kdim == s_cnt * w
            tk = w
        else:
            n = s_cnt * w
            assert b_rows == kdim
            tn = w
        nb = ba
    else:
        n = b.shape[1] if mode == "nt" else b.shape[2]
        n = n if n_b is None else n_b
        nb = max(ba, bb)
    if not (b_split and mode != "nt"):
        tn = _tile(n, tn)
    if not (b_split and mode == "nt"):
        tk = _tile(kdim, tk)
    tm = _tile(m, tm)
    assert m % tm == 0 and n % tn == 0 and kdim % tk == 0, (name, m, n, kdim, tm, tn, tk)
    nk = kdim // tk
    if reduce_b:
        grid = (m // tm, n // tn, nb, nk)
        unpack = lambda i, j, bi, k: (bi, i, j, k)
        sem = ("parallel", "parallel", "arbitrary", "arbitrary")
        nred = nb * nk
    else:
        grid = (nb, m // tm, n // tn, nk)
        unpack = lambda bi, i, j, k: (bi, i, j, k)
        sem = ("parallel", "parallel", "parallel", "arbitrary")
        nred = nk

    def a_map(*g):
        bi, i, j, k = unpack(*g)
        ab = bi if ba > 1 else 0
        return (ab, k, i) if mode == "tn" else (ab, i, k)

    def b_map(*g):
        bi, i, j, k = unpack(*g)
        if b_split:
            return (k, j, 0) if mode == "nt" else (j, k, 0)
        bq = bi if bb > 1 else 0
        return (bq, j, k) if mode == "nt" else (bq, k, j)

    def o_map(*g):
        bi, i, j, k = unpack(*g)
        if o_split:
            return (j, i, 0)
        return (0 if reduce_b else bi, i, j)

    a_blk = (1, tk, tm) if mode == "tn" else (1, tm, tk)
    b_blk = (1, tn, tk) if mode == "nt" else (1, tk, tn)
    dims = {"nn": _NN, "nt": _NT, "tn": _TN}[mode]
    has_add = add is not None

    def body(*refs):
        a_ref, b_ref = refs[:2]
        if has_add:
            add_ref = refs[2]
        if nred == 1:
            o_ref = refs[-1]
            r = _dot(a_ref[0], b_ref[0], dims)
            if has_add:
                r = r + add_ref[0].astype(F32)
            o_ref[0] = r.astype(o_ref.dtype)
            return
        o_ref, acc_ref = refs[-2:]
        if reduce_b:
            step = pl.program_id(2) * nk + pl.program_id(3)
        else:
            step = pl.program_id(3)

        @pl.when(step == 0)
        def _():
            acc_ref[...] = _dot(a_ref[0], b_ref[0], dims)

        @pl.when(step > 0)
        def _():
            acc_ref[...] += _dot(a_ref[0], b_ref[0], dims)

        @pl.when(step == nred - 1)
        def _():
            r = acc_ref[...]
            if has_add:
                r = r + add_ref[0].astype(F32)
            o_ref[0] = r.astype(o_ref.dtype)

    in_specs = [pl.BlockSpec(a_blk, a_map), pl.BlockSpec(b_blk, b_map)]
    operands = [a, b]
    if has_add:
        in_specs.append(pl.BlockSpec((1, tm, tn), o_map))
        operands.append(add)
    if dep is not None:
        in_specs.append(pl.BlockSpec(memory_space=pl.ANY))
        operands.append(dep)
    if o_split:
        out_shape = jax.ShapeDtypeStruct((n // tn, m, tn), out_dtype)
    else:
        out_shape = jax.ShapeDtypeStruct((1 if reduce_b else nb, m, n), out_dtype)
    return pl.pallas_call(
        body, grid=grid, in_specs=in_specs, out_specs=pl.BlockSpec((1, tm, tn), o_map),
        out_shape=out_shape, scratch_shapes=[] if nred == 1 else [pltpu.VMEM((tm, tn), F32)],
        compiler_params=_cp(*sem), name=name)(*operands)


def _mm2(a, b, mode, out_dtype, *, name, add=None, **kw):
    b3 = b if kw.get("b_split") else b[None]
    r = _mm(a[None], b3, mode, out_dtype, name=name, add=None if add is None else add[None], **kw)
    return r if kw.get("o_split") else r[0]


def _rms_fwd(x, g, *, name, tb=512):
    t, d = x.shape
    tb = min(tb, t)

    def body(x_ref, g_ref, o_ref):
        xv = x_ref[...]
        r = lax.rsqrt(jnp.mean(xv * xv, axis=-1, keepdims=True) + EPS)
        o_ref[...] = (xv * r * g_ref[...]).astype(o_ref.dtype)

    return pl.pallas_call(
        body, grid=(t // tb,),
        in_specs=[pl.BlockSpec((tb, d), lambda i: (i, 0)), pl.BlockSpec((1, d), lambda i: (0, 0))],
        out_specs=pl.BlockSpec((tb, d), lambda i: (i, 0)),
        out_shape=jax.ShapeDtypeStruct((t, d), MXU_DTYPE), compiler_params=_cp("parallel"), name=name)(x, g)


def _rms_bwd(x, g, dh, dres, *, name, tb=512):
    t, d = x.shape
    tb = min(tb, t)
    has_res = dres is not None

    def body(*refs):
        if has_res:
            x_ref, g_ref, dh_ref, dres_ref, dx_ref, dxl_ref, dg_ref = refs
        else:
            x_ref, g_ref, dh_ref, dx_ref, dxl_ref, dg_ref = refs
        xv = x_ref[...]
        r = lax.rsqrt(jnp.mean(xv * xv, axis=-1, keepdims=True) + EPS)
        xh = xv * r
        dhv = dh_ref[...].astype(F32)

        @pl.when(pl.program_id(0) == 0)
        def _():
            dg_ref[...] = jnp.zeros_like(dg_ref)

        dg_ref[...] += jnp.sum(dhv * xh, axis=0, keepdims=True)
        dxh = dhv * g_ref[...]
        dx = r * (dxh - xh * jnp.mean(dxh * xh, axis=-1, keepdims=True))
        if has_res:
            dx = dx + dres_ref[...]
        dx_ref[...] = dx
        dxl_ref[...] = dx.astype(dxl_ref.dtype)

    row = pl.BlockSpec((tb, d), lambda i: (i, 0))
    vec = pl.BlockSpec((1, d), lambda i: (0, 0))
    operands = [x, g, dh] + ([dres] if has_res else [])
    return pl.pallas_call(
        body, grid=(t // tb,), in_specs=[row, vec, row] + ([row] if has_res else []),
        out_specs=[row, row, vec],
        out_shape=[jax.ShapeDtypeStruct((t, d), F32), jax.ShapeDtypeStruct((t, d), MXU_DTYPE),
                   jax.ShapeDtypeStruct((1, d), F32)],
        compiler_params=_cp("arbitrary"), name=name)(*operands)


def _loss_head(x, g, target, *, name, tb=512):
    t, d = x.shape
    tb = min(tb, t)

    def body(x_ref, g_ref, t_ref, loss_ref, dx_ref, dxl_ref, dg_ref):
        xv = x_ref[...]
        r = lax.rsqrt(jnp.mean(xv * xv, axis=-1, keepdims=True) + EPS)
        xh = xv * r
        gv = g_ref[...]
        err = xh * gv - t_ref[...]

        @pl.when(pl.program_id(0) == 0)
        def _():
            dg_ref[...] = jnp.zeros_like(dg_ref)
            loss_ref[...] = jnp.zeros_like(loss_ref)

        row_loss = jnp.mean(err * err, axis=-1, keepdims=True)
        loss_ref[...] += 0.5 * jnp.sum(row_loss, axis=0, keepdims=True)
        dy = err * (1.0 / d)
        dg_ref[...] += jnp.sum(dy * xh, axis=0, keepdims=True)
        dxh = dy * gv
        dx = r * (dxh - xh * jnp.mean(dxh * xh, axis=-1, keepdims=True))
        dx_ref[...] = dx
        dxl_ref[...] = dx.astype(dxl_ref.dtype)

    row = pl.BlockSpec((tb, d), lambda i: (i, 0))
    vec = pl.BlockSpec((1, d), lambda i: (0, 0))
    return pl.pallas_call(
        body, grid=(t // tb,), in_specs=[row, vec, row],
        out_specs=[pl.BlockSpec((1, 1), lambda i: (0, 0)), row, row, vec],
        out_shape=[jax.ShapeDtypeStruct((1, 1), F32), jax.ShapeDtypeStruct((t, d), F32),
                   jax.ShapeDtypeStruct((t, d), MXU_DTYPE), jax.ShapeDtypeStruct((1, d), F32)],
        compiler_params=_cp("arbitrary"), name=name)(x, g, target)


def _ffn_up(h, wg, wu, *, name, tb=512):
    t, d = h.shape
    s, f, _ = wg.shape
    tb = min(tb, t)

    def body(h_ref, wg_ref, wu_ref, g_ref, u_ref, a_ref):
        hv = h_ref[...]
        gv = _dot(hv, wg_ref[0], _NT)
        uv = _dot(hv, wu_ref[0], _NT)
        g_ref[0] = gv.astype(g_ref.dtype)
        u_ref[0] = uv.astype(u_ref.dtype)
        a_ref[0] = (gv * _sigmoid(gv) * uv).astype(a_ref.dtype)

    wspec = pl.BlockSpec((1, f, d), lambda j, i: (j, 0, 0))
    ospec = pl.BlockSpec((1, tb, f), lambda j, i: (j, i, 0))
    return pl.pallas_call(
        body, grid=(s, t // tb),
        in_specs=[pl.BlockSpec((tb, d), lambda j, i: (i, 0)), wspec, wspec],
        out_specs=[ospec, ospec, ospec],
        out_shape=[jax.ShapeDtypeStruct((s, t, f), MXU_DTYPE)] * 3,
        compiler_params=_cp("parallel", "parallel"), name=name)(h, wg, wu)


def _ffn_dact(dy, wd, gate, up, *, name, tb=1024):
    t, d = dy.shape
    s, f, _ = wd.shape
    tb = min(tb, t)

    def body(dy_ref, wd_ref, g_ref, u_ref, dg_ref, du_ref):
        da = _dot(dy_ref[...], wd_ref[0], _NT)
        gv = g_ref[0].astype(F32)
        sg = _sigmoid(gv)
        du_ref[0] = (da * gv * sg).astype(du_ref.dtype)
        dg_ref[0] = (da * u_ref[0].astype(F32) * (sg * (1.0 + gv * (1.0 - sg)))).astype(dg_ref.dtype)

    aspec = pl.BlockSpec((1, tb, f), lambda j, i: (j, i, 0))
    return pl.pallas_call(
        body, grid=(s, t // tb),
        in_specs=[pl.BlockSpec((tb, d), lambda j, i: (i, 0)),
                  pl.BlockSpec((1, f, d), lambda j, i: (j, 0, 0)), aspec, aspec],
        out_specs=[aspec, aspec],
        out_shape=[jax.ShapeDtypeStruct((s, t, f), MXU_DTYPE), jax.ShapeDtypeStruct((s, t, f), MXU_DTYPE)],
        compiler_params=_cp("parallel", "parallel"), name=name)(dy, wd, gate, up)


def _xattn_fwd(q, kv, *, name, tb=512):
    t, d = q.shape
    m = kv.shape[0]
    hd = d // XA_HEADS
    tb = min(tb, t)
    scale = hd ** -0.5

    def body(q_ref, kv_ref, o_ref):
        for hh in range(XA_HEADS):
            cs = slice(hh * hd, (hh + 1) * hd)
            s = _dot(q_ref[:, cs], kv_ref[:, cs], _NT) * scale
            s = s - jnp.max(s, axis=-1, keepdims=True)
            e = jnp.exp(s)
            p = e / jnp.sum(e, axis=-1, keepdims=True)
            o_ref[:, cs] = _dot(p, kv_ref[:, d + hh * hd:d + (hh + 1) * hd], _NN).astype(o_ref.dtype)

    return pl.pallas_call(
        body, grid=(t // tb,),
        in_specs=[pl.BlockSpec((tb, d), lambda i: (i, 0)), pl.BlockSpec((m, 2 * d), lambda i: (0, 0))],
        out_specs=pl.BlockSpec((tb, d), lambda i: (i, 0)),
        out_shape=jax.ShapeDtypeStruct((t, d), MXU_DTYPE), compiler_params=_cp("parallel"), name=name)(q, kv)


def _xattn_bwd(q, kv, do, *, name, tb=512):
    t, d = q.shape
    m = kv.shape[0]
    hd = d // XA_HEADS
    tb = min(tb, t)
    scale = hd ** -0.5

    def body(q_ref, kv_ref, do_ref, dq_ref, dkv_ref):
        @pl.when(pl.program_id(0) == 0)
        def _():
            dkv_ref[...] = jnp.zeros_like(dkv_ref)

        for hh in range(XA_HEADS):
            cs = slice(hh * hd, (hh + 1) * hd)
            vs = slice(d + hh * hd, d + (hh + 1) * hd)
            qv, kk, vv, dov = q_ref[:, cs], kv_ref[:, cs], kv_ref[:, vs], do_ref[:, cs]
            s = _dot(qv, kk, _NT) * scale
            s = s - jnp.max(s, axis=-1, keepdims=True)
            e = jnp.exp(s)
            p = e / jnp.sum(e, axis=-1, keepdims=True)
            dkv_ref[:, vs] += _dot(p, dov, _TN)
            dp = _dot(dov, vv, _NT)
            ds = p * (dp - jnp.sum(p * dp, axis=-1, keepdims=True)) * scale
            dq_ref[:, cs] = _dot(ds, kk, _NN).astype(dq_ref.dtype)
            dkv_ref[:, cs] += _dot(ds, qv, _TN)

    row = pl.BlockSpec((tb, d), lambda i: (i, 0))
    full = pl.BlockSpec((m, 2 * d), lambda i: (0, 0))
    return pl.pallas_call(
        body, grid=(t // tb,), in_specs=[row, full, row], out_specs=[row, full],
        out_shape=[jax.ShapeDtypeStruct((t, d), MXU_DTYPE), jax.ShapeDtypeStruct((m, 2 * d), F32)],
        compiler_params=_cp("arbitrary"), name=name)(q, kv, do)


def _pool_window_stats(u, gi, reverse):
    t = u.shape[0]
    row = lax.broadcasted_iota(jnp.int32, u.shape, 0)
    s = u
    for j in range(POOL_GROUPS):
        sh = 1 << j
        if reverse:
            rolled = jnp.where(row < t - sh, pltpu.roll(s, t - sh, axis=0), 0.0)
        else:
            rolled = jnp.where(row >= sh, pltpu.roll(s, sh, axis=0), 0.0)
        s = jnp.where(j <= gi, s + rolled, s)
    return s, row


def _pool_fwd(z, w_pool, scale, *, name):
    t = z.shape[0]
    g_cnt, c, _ = w_pool.shape

    def body(z_ref, w_ref, s_ref, o_ref):
        gi = pl.program_id(0)
        u = z_ref[...]
        win, row = _pool_window_stats(u, gi, False)
        cnt = jnp.minimum(row + 1, lax.shift_left(jnp.int32(2), gi)).astype(F32)
        p = win / cnt - u
        o_ref[...] = (_dot(p, w_ref[0], _NN) * s_ref[...]).astype(o_ref.dtype)

    return pl.pallas_call(
        body, grid=(g_cnt,),
        in_specs=[pl.BlockSpec((t, c), lambda g: (0, g)), pl.BlockSpec((1, c, c), lambda g: (g, 0, 0)),
                  pl.BlockSpec((1, c), lambda g: (0, g))],
        out_specs=pl.BlockSpec((t, c), lambda g: (0, g)),
        out_shape=jax.ShapeDtypeStruct((t, 2 * g_cnt * c), MXU_DTYPE),
        compiler_params=_cp("parallel"), name=name)(z, w_pool, scale)


def _pool_bwd(z, w_pool, scale, dycat, *, name):
    t = z.shape[0]
    g_cnt, c, _ = w_pool.shape

    def body(z_ref, w_ref, s_ref, dy_ref, du_ref, dw_ref, ds_ref):
        gi = pl.program_id(0)
        u = z_ref[...]
        win, row = _pool_window_stats(u, gi, False)
        cnt = jnp.minimum(row + 1, lax.shift_left(jnp.int32(2), gi)).astype(F32)
        p = win / cnt - u
        y = _dot(p, w_ref[0], _NN)
        dya = dy_ref[...].astype(F32)
        ds_ref[...] = jnp.sum(dya * y, axis=0, keepdims=True)
        dy = dya * s_ref[...]
        dw_ref[0] = _dot(p, dy, _TN)
        dp = _dot(dy, w_ref[0], _NT)
        back, _ = _pool_window_stats(dp / cnt, gi, True)
        du_ref[...] = (back - dp).astype(du_ref.dtype)

    col = pl.BlockSpec((t, c), lambda g: (0, g))
    return pl.pallas_call(
        body, grid=(g_cnt,),
        in_specs=[col, pl.BlockSpec((1, c, c), lambda g: (g, 0, 0)), pl.BlockSpec((1, c), lambda g: (0, g)), col],
        out_specs=[col, pl.BlockSpec((1, c, c), lambda g: (g, 0, 0)), pl.BlockSpec((1, c), lambda g: (0, g))],
        out_shape=[jax.ShapeDtypeStruct((t, g_cnt * c), MXU_DTYPE), jax.ShapeDtypeStruct((g_cnt, c, c), F32),
                   jax.ShapeDtypeStruct((1, g_cnt * c), F32)],
        compiler_params=_cp("parallel"), name=name)(z, w_pool, scale, dycat)


def _chunk_tri(lower):
    r = lax.broadcasted_iota(jnp.int32, (LANES, LANES), 0)
    c = lax.broadcasted_iota(jnp.int32, (LANES, LANES), 1)
    same = (r // HG_CHUNK) == (c // HG_CHUNK)
    return jnp.where(same & ((c <= r) if lower else (c >= r)), 1.0, 0.0).astype(F32)


def _hgrn_prepare(q_ref, f_ref, lb_ref, qh_s, k_s, b_s, qt_s, kt_s, gl_s):
    tb = q_ref.shape[0]
    lb = lb_ref[...]
    sg = _sigmoid(f_ref[...])
    f = lb + (1.0 - lb) * sg
    logf = jnp.log(f)
    qv = q_ref[...]
    qh = qv * _sigmoid(qv) * (HG_HEAD ** -0.5)
    tri = _chunk_tri(True)
    for r in range(tb // LANES):
        rows = slice(r * LANES, (r + 1) * LANES)
        b_s[rows, :] = _dot_f32(tri, logf[rows, :], _NN)
    b = b_s[...]
    b3 = b.reshape(tb // HG_CHUNK, HG_CHUNK, HG_HEAD)
    bl = b3[:, HG_CHUNK - 1:HG_CHUNK, :]
    k = 1.0 - f
    qh_s[...] = qh
    k_s[...] = k
    qt_s[...] = qh * jnp.exp(b)
    kt_s[...] = k * jnp.exp(bl - b3).reshape(tb, HG_HEAD)
    gl_s[...] = jnp.exp(jnp.broadcast_to(bl, b3.shape)).reshape(tb, HG_HEAD)
    return sg, f


def _hgrn_intra(qh, kk, bq, rows_a, rows_b):
    ones = jnp.ones((HG_HEAD, HG_HEAD), MXU_DTYPE)
    es, stack_a, stack_b = [], [], []
    for s in range(HG_CHUNK):
        e = jnp.exp(jnp.minimum(bq - bq[s:s + 1, :], 0.0))
        es.append(e)
        stack_a.append(qh * e * kk[s:s + 1, :])
        if rows_a is not None:
            stack_b.append(rows_a * rows_b[s:s + 1, :])
    a_rep = _dot(jnp.concatenate(stack_a, axis=0), ones, _NN)
    d_rep = _dot(jnp.concatenate(stack_b, axis=0), ones, _NN) if rows_a is not None else None
    return es, a_rep, d_rep


def _hgrn_fwd(z, lb, hg_norm, ycat, *, name, tb=512):
    t = z.shape[0]
    mix_b = lb.shape[1]
    heads = mix_b // HG_HEAD
    off = (z.shape[1] - 4 * mix_b) // HG_HEAD
    tb = min(tb, t)
    ncb = tb // HG_CHUNK

    def body(q_ref, f_ref, i_ref, g_ref, lb_ref, hn_ref, ycat_in, y_ref, o_ref, st_ref,
             state, qh_s, k_s, b_s, qt_s, kt_s, gl_s, o_s):
        del ycat_in

        @pl.when(pl.program_id(1) == 0)
        def _():
            state[...] = jnp.zeros_like(state)

        _hgrn_prepare(q_ref, f_ref, lb_ref, qh_s, k_s, b_s, qt_s, kt_s, gl_s)
        row = lax.broadcasted_iota(jnp.int32, (HG_CHUNK, HG_HEAD), 0)

        def chunk(c, carry):
            rows = pl.ds(pl.multiple_of(c * HG_CHUNK, HG_CHUNK), HG_CHUNK)
            st = state[...]
            st_ref[0, c] = st
            vv = i_ref[rows, :]
            o = _dot(qt_s[rows, :], st, _NT)
            _, a_rep, _ = _hgrn_intra(qh_s[rows, :], k_s[rows, :], b_s[rows, :], None, None)
            for s in range(HG_CHUNK):
                o = o + jnp.where(row >= s, a_rep[s * HG_CHUNK:(s + 1) * HG_CHUNK, :] * vv[s:s + 1, :], 0.0)
            o_s[rows, :] = o
            state[...] = st * gl_s[rows, :][0:1, :] + _dot(vv, kt_s[rows, :], _TN)
            return carry

        lax.fori_loop(0, ncb, chunk, 0, unroll=2)
        o = o_s[...]
        o_ref[...] = o
        r = lax.rsqrt(jnp.mean(o * o, axis=-1, keepdims=True) + EPS)
        gv = g_ref[...]
        y_ref[...] = (o * r * hn_ref[...] * (gv * _sigmoid(gv))).astype(y_ref.dtype)

    def zcol(kind):
        return pl.BlockSpec((tb, HG_HEAD), lambda h, i: (i, off + kind * heads + h))

    scratch = [pltpu.VMEM((HG_HEAD, HG_HEAD), F32)] + [pltpu.VMEM((tb, HG_HEAD), F32)] * 7
    return pl.pallas_call(
        body, grid=(heads, t // tb),
        in_specs=[zcol(0), zcol(1), zcol(2), zcol(3), pl.BlockSpec((1, HG_HEAD), lambda h, i: (0, h)),
                  pl.BlockSpec((1, HG_HEAD), lambda h, i: (0, 0)), pl.BlockSpec(memory_space=pl.ANY)],
        out_specs=[pl.BlockSpec((tb, HG_HEAD), lambda h, i: (i, heads + h)),
                   pl.BlockSpec((tb, HG_HEAD), lambda h, i: (i, h)),
                   pl.BlockSpec((1, ncb, HG_HEAD, HG_HEAD), lambda h, i: (h, i, 0, 0))],
        out_shape=[jax.ShapeDtypeStruct(ycat.shape, ycat.dtype), jax.ShapeDtypeStruct((t, mix_b), F32),
                   jax.ShapeDtypeStruct((heads, t // HG_CHUNK, HG_HEAD, HG_HEAD), F32)],
        scratch_shapes=scratch, input_output_aliases={6: 0},
        compiler_params=_cp("parallel", "arbitrary"), name=name)(z, z, z, z, lb, hg_norm, ycat)


def _hgrn_bwd(z, lb, hg_norm, o_raw, states, dycat, *, name, tb=512):
    t = z.shape[0]
    mix_b = lb.shape[1]
    heads = mix_b // HG_HEAD
    off = (z.shape[1] - 4 * mix_b) // HG_HEAD
    tb = min(tb, t)
    ncb = tb // HG_CHUNK
    nt = t // tb

    def body(q_ref, f_ref, i_ref, g_ref, lb_ref, hn_ref, o_ref, st_ref, dy_ref,
             dq_ref, dfl_ref, di_ref, dg_ref, dlb_ref, dhn_ref,
             dstate, qh_s, k_s, b_s, qt_s, kt_s, gl_s, do_s, dqh_s, dk_s, db_s):
        first = pl.program_id(1) == 0

        @pl.when(first)
        def _():
            dstate[...] = jnp.zeros_like(dstate)
            dlb_ref[...] = jnp.zeros_like(dlb_ref)

        @pl.when(first & (pl.program_id(0) == 0))
        def _():
            dhn_ref[...] = jnp.zeros_like(dhn_ref)

        sg, f = _hgrn_prepare(q_ref, f_ref, lb_ref, qh_s, k_s, b_s, qt_s, kt_s, gl_s)
        o = o_ref[...]
        r = lax.rsqrt(jnp.mean(o * o, axis=-1, keepdims=True) + EPS)
        oh = o * r
        gv = g_ref[...]
        sgg = _sigmoid(gv)
        dy = dy_ref[...].astype(F32)
        hn = hn_ref[...]
        dg_ref[...] = (dy * oh * hn * (sgg * (1.0 + gv * (1.0 - sgg)))).astype(dg_ref.dtype)
        don = dy * (gv * sgg)
        dhn_ref[...] += jnp.sum(don * oh, axis=0, keepdims=True)
        doh = don * hn
        do_s[...] = r * (doh - oh * jnp.mean(doh * oh, axis=-1, keepdims=True))
        row = lax.broadcasted_iota(jnp.int32, (HG_CHUNK, HG_HEAD), 0)

        def chunk(ci, carry):
            c = ncb - 1 - ci
            rows = pl.ds(pl.multiple_of(c * HG_CHUNK, HG_CHUNK), HG_CHUNK)
            st_prev = st_ref[0, c]
            dst = dstate[...]
            qh, kk, bq, vv = qh_s[rows, :], k_s[rows, :], b_s[rows, :], i_ref[rows, :]
            qt, kt, doo = qt_s[rows, :], kt_s[rows, :], do_s[rows, :]
            gl = gl_s[rows, :][0:1, :]
            es, a_rep, d_rep = _hgrn_intra(qh, kk, bq, doo, vv)
            dqh = jnp.exp(bq) * _dot(doo, st_prev, _NN)
            dk = jnp.exp(bq[HG_CHUNK - 1:HG_CHUNK, :] - bq) * _dot(vv, dst, _NN)
            dv = _dot(kt, dst, _NT)
            for s in range(HG_CHUNK):
                blk = slice(s * HG_CHUNK, (s + 1) * HG_CHUNK)
                wgt = jnp.where(row >= s, d_rep[blk, :] * es[s], 0.0)
                dqh = dqh + wgt * kk[s:s + 1, :]
                dk = dk + jnp.where(row == s, jnp.sum(wgt * qh, axis=0, keepdims=True), 0.0)
                dv_row = jnp.sum(jnp.where(row >= s, a_rep[blk, :] * doo, 0.0), axis=0, keepdims=True)
                dv = dv + jnp.where(row == s, dv_row, 0.0)
            st_next = st_prev * gl + _dot(vv, kt, _TN)
            db = qh * dqh - kk * dk
            db = db + jnp.where(row == HG_CHUNK - 1, jnp.sum(st_next * dst, axis=0, keepdims=True), 0.0)
            dstate[...] = dst * gl + _dot(doo, qt, _TN)
            dqh_s[rows, :] = dqh
            dk_s[rows, :] = dk
            db_s[rows, :] = db
            di_ref[rows, :] = dv.astype(di_ref.dtype)
            return carry

        lax.fori_loop(0, ncb, chunk, 0, unroll=2)
        tri = _chunk_tri(False)
        lb_v = lb_ref[...]
        qv = q_ref[...]
        sgq = _sigmoid(qv)
        dq_ref[...] = (dqh_s[...] * (HG_HEAD ** -0.5) * (sgq * (1.0 + qv * (1.0 - sgq)))).astype(dq_ref.dtype)
        dlb = jnp.zeros((1, HG_HEAD), F32)
        for rr in range(tb // LANES):
            rws = slice(rr * LANES, (rr + 1) * LANES)
            dlogf = _dot_f32(tri, db_s[rws, :], _NN)
            df = dlogf / f[rws, :] - dk_s[rws, :]
            sgr = sg[rws, :]
            dfl_ref[rws, :] = (df * (1.0 - lb_v) * sgr * (1.0 - sgr)).astype(dfl_ref.dtype)
            dlb = dlb + jnp.sum(df * (1.0 - sgr), axis=0, keepdims=True)
        dlb_ref[...] += dlb

    def zcol(kind):
        return pl.BlockSpec((tb, HG_HEAD), lambda h, i: (nt - 1 - i, off + kind * heads + h))

    hcol = pl.BlockSpec((tb, HG_HEAD), lambda h, i: (nt - 1 - i, h))
    scratch = [pltpu.VMEM((HG_HEAD, HG_HEAD), F32)] + [pltpu.VMEM((tb, HG_HEAD), F32)] * 10
    out = jax.ShapeDtypeStruct((t, mix_b), MXU_DTYPE)
    return pl.pallas_call(
        body, grid=(heads, nt),
        in_specs=[zcol(0), zcol(1), zcol(2), zcol(3), pl.BlockSpec((1, HG_HEAD), lambda h, i: (0, h)),
                  pl.BlockSpec((1, HG_HEAD), lambda h, i: (0, 0)), hcol,
                  pl.BlockSpec((1, ncb, HG_HEAD, HG_HEAD), lambda h, i: (h, nt - 1 - i, 0, 0)),
                  pl.BlockSpec((tb, HG_HEAD), lambda h, i: (nt - 1 - i, heads + h))],
        out_specs=[hcol, hcol, hcol, hcol, pl.BlockSpec((1, HG_HEAD), lambda h, i: (0, h)),
                   pl.BlockSpec((1, HG_HEAD), lambda h, i: (0, 0))],
        out_shape=[out, out, out, out, jax.ShapeDtypeStruct((1, mix_b), F32),
                   jax.ShapeDtypeStruct((1, HG_HEAD), F32)],
        scratch_shapes=scratch, compiler_params=_cp("arbitrary", "arbitrary"),
        name=name)(z, z, z, z, lb, hg_norm, o_raw, states, dycat)


def _lb_fwd(lb_table, layer, *, name):
    rows, width = lb_table.shape

    def body(t_ref, o_ref):
        tv = t_ref[...]
        e = jnp.exp(tv - jnp.max(tv, axis=0, keepdims=True))
        sm = e / jnp.sum(e, axis=0, keepdims=True)
        o_ref[...] = jnp.sum(sm[1:layer + 2, :], axis=0, keepdims=True)

    return pl.pallas_call(body, out_shape=jax.ShapeDtypeStruct((1, width), F32), name=name)(lb_table)


def _lb_bwd(lb_table, dlb, layer, *, name):
    rows, width = lb_table.shape

    def body(t_ref, d_ref, o_ref):
        tv = t_ref[...]
        e = jnp.exp(tv - jnp.max(tv, axis=0, keepdims=True))
        sm = e / jnp.sum(e, axis=0, keepdims=True)
        ridx = lax.broadcasted_iota(jnp.int32, sm.shape, 0)
        dsm = jnp.where((ridx >= 1) & (ridx <= layer + 1), d_ref[...], 0.0)
        o_ref[...] = sm * (dsm - jnp.sum(sm * dsm, axis=0, keepdims=True))

    return pl.pallas_call(body, out_shape=jax.ShapeDtypeStruct((rows, width), F32), name=name)(lb_table, dlb)


FOX_BLOCK = 512


def _fox_prep(zf, b_f, *, name, blk=256):
    t = zf.shape[0]

    def body(z_ref, b_ref, fc_ref):
        r = lax.broadcasted_iota(jnp.int32, (blk, blk), 0)
        c = lax.broadcasted_iota(jnp.int32, (blk, blk), 1)
        tri = jnp.where(c <= r, 1.0, 0.0).astype(F32)
        carry = jnp.zeros((1, LANES), F32)
        for j in range(t // blk):
            rows = slice(j * blk, (j + 1) * blk)
            ls = jax.nn.log_sigmoid(z_ref[rows, :] + b_ref[...])
            fb = _dot_f32(tri, ls, _NN) + carry
            carry = fb[blk - 1:blk, :]
            fc_ref[rows, :] = fb

    return pl.pallas_call(
        body, out_shape=jax.ShapeDtypeStruct((t, LANES), F32),
        compiler_params=pltpu.CompilerParams(vmem_limit_bytes=VMEM_LIMIT_BYTES), name=name)(zf, b_f)


def _fox_head_column(fc_ref, fk_s, head):
    lane = lax.broadcasted_iota(jnp.int32, fc_ref.shape, 1)
    fk_s[...] = jnp.sum(jnp.where(lane == head, fc_ref[...], 0.0), axis=1, keepdims=True)


def _fox_scores(k_blk, q_blk, fk_blk, diagonal):
    s = _dot(k_blk, q_blk, _NT) * (FOX_HEAD ** -0.5) - fk_blk
    if diagonal:
        key = lax.broadcasted_iota(jnp.int32, s.shape, 0)
        qry = lax.broadcasted_iota(jnp.int32, s.shape, 1)
        s = jnp.where(key <= qry, s, -jnp.inf)
    return s


def _fox_fwd(zqkv, fcol, *, name):
    t = zqkv.shape[0]
    d = zqkv.shape[1] // 3
    heads = d // FOX_HEAD
    blk = min(FOX_BLOCK, t)
    nq = t // blk

    def body(q_ref, k_ref, v_ref, fc_ref, o_ref, lse_ref, fk_s):
        _fox_head_column(fc_ref, fk_s, pl.program_id(0))

        def q_block(qi, carry):
            qrows = pl.ds(pl.multiple_of(qi * blk, blk), blk)
            q_blk = q_ref[qrows, :]

            def update(st, krows, diagonal):
                m, l, acc = st
                s = _fox_scores(k_ref[krows, :], q_blk, fk_s[krows, :], diagonal)
                m_new = jnp.maximum(m, jnp.max(s, axis=0, keepdims=True))
                alpha = jnp.exp(m - m_new)
                p = jnp.exp(s - m_new)
                l = alpha * l + jnp.sum(p, axis=0, keepdims=True)
                acc = acc * alpha + _dot(v_ref[krows, :], p, _TN)
                return m_new, l, acc

            def k_block(kj, st):
                return update(st, pl.ds(pl.multiple_of(kj * blk, blk), blk), False)

            init = (jnp.full((1, blk), -jnp.inf, F32), jnp.zeros((1, blk), F32),
                    jnp.zeros((FOX_HEAD, blk), F32))
            m, l, acc = update(lax.fori_loop(0, qi, k_block, init), qrows, True)
            o_ref[qrows, :] = (acc / l).T.astype(o_ref.dtype)
            lse_ref[0, :, qrows] = m + jnp.log(l)
            return carry

        lax.fori_loop(0, nq, q_block, 0)

    def col(kind):
        return pl.BlockSpec((t, FOX_HEAD), lambda h: (0, kind * heads + h))

    rowvec = pl.BlockSpec((1, 1, t), lambda h: (h, 0, 0))
    return pl.pallas_call(
        body, grid=(heads,),
        in_specs=[col(0), col(1), col(2), pl.BlockSpec((t, LANES), lambda h: (0, 0))],
        out_specs=[pl.BlockSpec((t, FOX_HEAD), lambda h: (0, h)), rowvec],
        out_shape=[jax.ShapeDtypeStruct((t, d), MXU_DTYPE), jax.ShapeDtypeStruct((heads, 1, t), F32)],
        scratch_shapes=[pltpu.VMEM((t, 1), F32)],
        compiler_params=_cp("parallel"), name=name)(zqkv, zqkv, zqkv, fcol)


def _fox_bwd(zqkv, fcol, lse, o, do, *, name):
    t = zqkv.shape[0]
    d = zqkv.shape[1] // 3
    heads = d // FOX_HEAD
    blk = min(FOX_BLOCK, t)
    nq = t // blk
    scale = FOX_HEAD ** -0.5

    def body(q_ref, k_ref, v_ref, fc_ref, lse_ref, o_ref, do_ref,
             dq_ref, dk_ref, dv_ref, rq_ref, rk_ref, dq_s, drow_s, fk_s):
        _fox_head_column(fc_ref, fk_s, pl.program_id(0))
        ones = jnp.ones((8, blk), MXU_DTYPE)
        dq_s[...] = jnp.zeros_like(dq_s)
        rq_ref[...] = jnp.zeros_like(rq_ref)
        ones_f = jnp.ones((8, FOX_HEAD), F32)
        for j in range(nq):
            rows = slice(j * blk, (j + 1) * blk)
            prod = do_ref[rows, :].astype(F32) * o_ref[rows, :].astype(F32)
            drow_s[:, rows] = _dot_f32(ones_f, prod, _NT)

        def k_block(kj, carry):
            krows = pl.ds(pl.multiple_of(kj * blk, blk), blk)
            k_blk, v_blk, fk_blk = k_ref[krows, :], v_ref[krows, :], fk_s[krows, :]

            def pair(st, qrows, diagonal):
                dk, dv, rk = st
                q_blk, do_blk = q_ref[qrows, :], do_ref[qrows, :]
                s = _fox_scores(k_blk, q_blk, fk_blk, diagonal)
                p = jnp.exp(s - lse_ref[0, :, qrows])
                dv = dv + _dot(p, do_blk, _NN)
                dp = _dot(v_blk, do_blk, _NT)
                ds = (p * (dp - drow_s[0:1, qrows])).astype(MXU_DTYPE)
                dk = dk + _dot(ds, q_blk, _NN)
                dq_s[qrows, :] += _dot(ds, k_blk, _TN)
                rq_ref[0, :, qrows] += _dot(ones, ds, _NN)[0:1, :]
                rk = rk + _dot(ones, ds, _NT)
                return dk, dv, rk

            def q_block(qi, st):
                return pair(st, pl.ds(pl.multiple_of(qi * blk, blk), blk), False)

            init = (jnp.zeros((blk, FOX_HEAD), F32), jnp.zeros((blk, FOX_HEAD), F32), jnp.zeros((8, blk), F32))
            dk, dv, rk = lax.fori_loop(kj + 1, nq, q_block, pair(init, krows, True))
            dk_ref[krows, :] = (dk * scale).astype(dk_ref.dtype)
            dv_ref[krows, :] = dv.astype(dv_ref.dtype)
            rk_ref[0, :, krows] = rk[0:1, :]
            return carry

        lax.fori_loop(0, nq, k_block, 0)
        dq_ref[...] = (dq_s[...] * scale).astype(dq_ref.dtype)

    def col(kind):
        return pl.BlockSpec((t, FOX_HEAD), lambda h: (0, kind * heads + h))

    hcol = pl.BlockSpec((t, FOX_HEAD), lambda h: (0, h))
    rowvec = pl.BlockSpec((1, 1, t), lambda h: (h, 0, 0))
    out = jax.ShapeDtypeStruct((t, d), MXU_DTYPE)
    vec = jax.ShapeDtypeStruct((heads, 1, t), F32)
    return pl.pallas_call(
        body, grid=(heads,),
        in_specs=[col(0), col(1), col(2), pl.BlockSpec((t, LANES), lambda h: (0, 0)), rowvec, hcol, hcol],
        out_specs=[hcol, hcol, hcol, rowvec, rowvec],
        out_shape=[out, out, out, vec, vec],
        scratch_shapes=[pltpu.VMEM((t, FOX_HEAD), F32), pltpu.VMEM((8, t), F32), pltpu.VMEM((t, 1), F32)],
        compiler_params=_cp("parallel"), name=name)(zqkv, zqkv, zqkv, fcol, lse, o, do)


def _fox_gate_bwd(rq, rk, zf, b_f, *, name, blk=256):
    heads, t = rq.shape

    def body(rq_ref, rk_ref, z_ref, b_ref, dfl_ref, db_ref):
        r = lax.broadcasted_iota(jnp.int32, (blk, blk), 0)
        c = lax.broadcasted_iota(jnp.int32, (blk, blk), 1)
        tri = jnp.where(r >= c, 1.0, 0.0).astype(F32)
        carry = jnp.zeros((heads, 1), F32)
        db = jnp.zeros((1, LANES), F32)
        pad = jnp.zeros((LANES - heads, blk), F32)
        for j in reversed(range(t // blk)):
            cols = slice(j * blk, (j + 1) * blk)
            df = rq_ref[:, cols] - rk_ref[:, cols]
            dls = _dot_f32(df, tri, _NN) + carry
            carry = dls[:, 0:1]
            dls_t = jnp.concatenate([dls, pad], axis=0).T
            dfl = dls_t * _sigmoid(-(z_ref[cols, :] + b_ref[...]))
            dfl_ref[cols, :] = dfl.astype(dfl_ref.dtype)
            db = db + jnp.sum(dfl, axis=0, keepdims=True)
        db_ref[...] = db

    return pl.pallas_call(
        body, out_shape=[jax.ShapeDtypeStruct((t, LANES), MXU_DTYPE), jax.ShapeDtypeStruct((1, LANES), F32)],
        compiler_params=pltpu.CompilerParams(vmem_limit_bytes=VMEM_LIMIT_BYTES), name=name)(rq, rk, zf, b_f)


def _adamw(w, m, v, parts, *, name, layer=None, prev=None, tr=128):
    lcnt, r, c = w.shape
    p = parts.shape[0]
    li = 0 if layer is None else layer
    tr = _tile(r, tr, 16)
    tc = c if tr < r or r * c <= 128 * 2048 else _tile(c, 256)
    has_prev = prev is not None

    def body(*refs):
        w_ref, m_ref, v_ref, p_ref = refs[:4]
        g_ref, d_ref, nm_ref, nv_ref = refs[-4:]
        g = p_ref[0].astype(F32)
        for j in range(1, p):
            g = g + p_ref[j].astype(F32)
        wv = w_ref[0]
        mn = ADAM_B1 * m_ref[0] + (1.0 - ADAM_B1) * g
        vn = ADAM_B2 * v_ref[0] + (1.0 - ADAM_B2) * (g * g)
        m_hat = mn / (1.0 - ADAM_B1 ** ADAM_STEP)
        v_hat = vn / (1.0 - ADAM_B2 ** ADAM_STEP)
        g_ref[0] = g
        d_ref[0] = -ADAM_LR * (m_hat / (jnp.sqrt(v_hat) + ADAM_EPS) + ADAM_WD * wv)
        nm_ref[0] = mn
        nv_ref[0] = vn

    slab = pl.BlockSpec((1, tr, tc), lambda i, j: (li, i, j))
    in_specs = [slab, slab, slab, pl.BlockSpec((p, tr, tc), lambda i, j: (0, i, j))]
    operands = [w, m, v, parts]
    aliases = {}
    if has_prev:
        in_specs += [pl.BlockSpec(memory_space=pl.ANY)] * 4
        operands += list(prev)
        aliases = {4: 0, 5: 1, 6: 2, 7: 3}
    shp = jax.ShapeDtypeStruct((lcnt, r, c), F32)
    return pl.pallas_call(
        body, grid=(r // tr, c // tc), in_specs=in_specs, out_specs=[slab] * 4, out_shape=[shp] * 4,
        input_output_aliases=aliases, compiler_params=_cp("parallel", "parallel"), name=name)(*operands)


def _my_place():
    return lax.axis_index("x"), lax.axis_index("y"), lax.axis_index("c")


def _slot(p):
    return 4 * p[0] + 2 * p[1] + p[2]


def _peer(me, mask):
    x, y, c = me
    return (1 - x if mask & 4 else x, 1 - y if mask & 2 else y, 1 - c if mask & 1 else c)


_HBM = pl.BlockSpec(memory_space=pltpu.HBM)
_SEM = pl.BlockSpec(memory_space=pltpu.SEMAPHORE)
_ANY = pl.BlockSpec(memory_space=pl.ANY)
_EFFECT = pltpu.SideEffectType.DATAFLOW_SIDE_EFFECTING


def _push_copy(src_refs, land_refs, send_sems, recv_sems, a, mask, me, per_peer, outgoing):
    peer = _peer(me, mask)
    src = src_refs[a].at[_slot(peer)] if per_peer else src_refs[a]
    dst = land_refs[a].at[_slot(me) if outgoing else _slot(peer)]
    k = a * (N_DEV - 1) + mask - 1
    return pltpu.make_async_remote_copy(
        src_ref=src, dst_ref=dst, send_sem=send_sems.at[k], recv_sem=recv_sems.at[k],
        device_id=peer, device_id_type=MESH)


ALL_PEERS = tuple(range(1, N_DEV))
CHIP_PEERS = (2, 4, 6)
FIRST_HOP = (1,) + CHIP_PEERS


def _push_start(srcs, dep, *, per_peer, name, masks=ALL_PEERS):
    n = len(srcs)
    mine = _slot(_my_place())
    lands = []
    for s in srcs:
        own = lax.dynamic_index_in_dim(s, mine, 0, keepdims=True) if per_peer else s[None]
        shape = s.shape if per_peer else (N_DEV,) + s.shape
        lands.append(lax.dynamic_update_slice_in_dim(lax.empty(shape, s.dtype), own, mine, 0))
    has_dep = dep is not None

    def body(*refs):
        src_refs, land_refs = refs[:n], refs[n:2 * n]
        send_sems, recv_sems = refs[2 * n + has_dep], refs[2 * n + has_dep + 1]
        token = refs[-1]
        me = _my_place()
        for a in range(n):
            for mask in masks:
                _push_copy(src_refs, land_refs, send_sems, recv_sems, a, mask, me, per_peer, True).start()
        token[...] = jnp.zeros_like(token)

    hbm_in = [pltpu.with_memory_space_constraint(v, pltpu.HBM) for v in list(srcs) + lands]
    out = pl.pallas_call(
        body, name=name,
        out_shape=(pltpu.SemaphoreType.DMA((n * (N_DEV - 1),)), pltpu.SemaphoreType.DMA((n * (N_DEV - 1),)),
                   *[pltpu.HBM(v.shape, v.dtype) for v in hbm_in], jax.ShapeDtypeStruct((8, LANES), F32)),
        in_specs=[_HBM] * (2 * n) + ([_ANY] if has_dep else []),
        out_specs=(_SEM, _SEM, *[_HBM] * (2 * n), pl.BlockSpec(memory_space=pltpu.VMEM)),
        input_output_aliases={i: 2 + i for i in range(2 * n)},
        compiler_params=pltpu.CompilerParams(has_side_effects=_EFFECT),
    )(*hbm_in, *([dep] if has_dep else []))
    return (n, per_peer, masks, out[:-1]), out[-1]


def _push_wait(handle, after, *, name):
    n, per_peer, masks, (send_sems, recv_sems, *bufs) = handle

    def body(*refs):
        src_refs, land_refs = refs[:n], refs[n:2 * n]
        send_sems, recv_sems = refs[2 * n], refs[2 * n + 1]
        me = _my_place()
        for a in range(n):
            for mask in masks:
                cp = _push_copy(src_refs, land_refs, send_sems, recv_sems, a, mask, me, per_peer, False)
                cp.wait_send()
                cp.wait_recv()

    out = pl.pallas_call(
        body, name=name, out_shape=tuple(pltpu.HBM(v.shape, v.dtype) for v in bufs),
        in_specs=[_HBM] * (2 * n) + [_SEM, _SEM, _ANY], out_specs=tuple([_HBM] * (2 * n)),
        input_output_aliases={i: i for i in range(2 * n)},
        compiler_params=pltpu.CompilerParams(has_side_effects=_EFFECT),
    )(*bufs, send_sems, recv_sems, after)
    return list(out[n:])


def _relay_copy(land_refs, send_sems, recv_sems, a, j, me, outgoing):
    sibling = _peer(me, 1)
    out_slot = _slot(_peer(me, CHIP_PEERS[j]))
    in_slot = _slot(_peer(sibling, CHIP_PEERS[j]))
    k = a * len(CHIP_PEERS) + j
    return pltpu.make_async_remote_copy(
        src_ref=land_refs[a].at[out_slot], dst_ref=land_refs[a].at[out_slot if outgoing else in_slot],
        send_sem=send_sems.at[k], recv_sem=recv_sems.at[k], device_id=sibling, device_id_type=MESH)


def _relay_start(lands, *, name):
    n = len(lands)

    def body(*refs):
        land_refs, send_sems, recv_sems, token = refs[:n], refs[n], refs[n + 1], refs[-1]
        me = _my_place()
        for a in range(n):
            for j in range(len(CHIP_PEERS)):
                _relay_copy(land_refs, send_sems, recv_sems, a, j, me, True).start()
        token[...] = jnp.zeros_like(token)

    hbm_in = [pltpu.with_memory_space_constraint(v, pltpu.HBM) for v in lands]
    n_sem = n * len(CHIP_PEERS)
    out = pl.pallas_call(
        body, name=name,
        out_shape=(pltpu.SemaphoreType.DMA((n_sem,)), pltpu.SemaphoreType.DMA((n_sem,)),
                   *[pltpu.HBM(v.shape, v.dtype) for v in hbm_in], jax.ShapeDtypeStruct((8, LANES), F32)),
        in_specs=[_HBM] * n, out_specs=(_SEM, _SEM, *[_HBM] * n, pl.BlockSpec(memory_space=pltpu.VMEM)),
        input_output_aliases={i: 2 + i for i in range(n)},
        compiler_params=pltpu.CompilerParams(has_side_effects=_EFFECT),
    )(*hbm_in)
    return (n, out[:-1]), out[-1]


def _relay_wait(handle, after, *, name):
    n, (send_sems, recv_sems, *bufs) = handle

    def body(*refs):
        land_refs, send_sems, recv_sems = refs[:n], refs[n], refs[n + 1]
        me = _my_place()
        for a in range(n):
            for j in range(len(CHIP_PEERS)):
                cp = _relay_copy(land_refs, send_sems, recv_sems, a, j, me, False)
                cp.wait_send()
                cp.wait_recv()

    out = pl.pallas_call(
        body, name=name, out_shape=tuple(pltpu.HBM(v.shape, v.dtype) for v in bufs),
        in_specs=[_HBM] * n + [_SEM, _SEM, _ANY], out_specs=tuple([_HBM] * n),
        input_output_aliases={i: i for i in range(n)},
        compiler_params=pltpu.CompilerParams(has_side_effects=_EFFECT),
    )(*bufs, send_sems, recv_sems, after)
    return list(out)


def _all_reduce_rows(v, *, name):
    r, c = v.shape

    def body(v_ref, o_ref, buf, send_sems, recv_sems):
        me = _my_place()
        mine = _slot(me)
        sends = []
        for mask in range(1, N_DEV):
            peer = _peer(me, mask)
            sends.append(pltpu.make_async_remote_copy(
                src_ref=v_ref, dst_ref=buf.at[mine], send_sem=send_sems.at[mask - 1],
                recv_sem=recv_sems.at[mask - 1], device_id=peer, device_id_type=MESH))
        for cp in sends:
            cp.start()
        buf[mine] = v_ref[...]
        for mask in range(1, N_DEV):
            peer = _peer(me, mask)
            pltpu.make_async_remote_copy(
                src_ref=v_ref, dst_ref=buf.at[_slot(peer)], send_sem=send_sems.at[mask - 1],
                recv_sem=recv_sems.at[mask - 1], device_id=peer, device_id_type=MESH).wait_recv()
        for cp in sends:
            cp.wait_send()
        total = buf[0]
        for j in range(1, N_DEV):
            total = total + buf[j]
        o_ref[...] = total

    vm = pl.BlockSpec(memory_space=pltpu.VMEM)
    return pl.pallas_call(
        body, in_specs=[vm], out_specs=vm, out_shape=jax.ShapeDtypeStruct((r, c), F32),
        scratch_shapes=[pltpu.VMEM((N_DEV, r, c), F32), pltpu.SemaphoreType.DMA((7,)),
                        pltpu.SemaphoreType.DMA((7,))],
        name=name)(v)


def _xa_fwd(x, mem, g_x, g_m, wq, wkv, wo, tag):
    hx = _rms_fwd(x, g_x, name=f"xa{tag}_norm")
    memn = _rms_fwd(mem, g_m, name=f"xa{tag}_mem_norm")
    q = _mm2(hx, wq, "nn", MXU_DTYPE, name=f"xa{tag}_q")
    kv = _mm2(memn, wkv, "nn", MXU_DTYPE, name=f"xa{tag}_kv", b_split=True)
    o = _xattn_fwd(q, kv, name=f"xa{tag}_attn")
    return _mm2(o, wo, "nn", F32, name=f"xa{tag}_out", add=x), (hx, memn, q, kv, o)


def _xa_bwd(x, mem, g_x, g_m, wq, wkv, wo, saved, dxo, dxo_lo, tag, put):
    hx, memn, q, kv, o = saved
    do = _mm2(dxo_lo, wo, "nt", MXU_DTYPE, name=f"xa{tag}_do")
    dwo = _mm2(o, dxo_lo, "tn", MXU_DTYPE, name=f"xa{tag}_dwo")
    dq, dkv = _xattn_bwd(q, kv, do, name=f"xa{tag}_attn_bwd")
    dwq = _mm2(hx, dq, "tn", MXU_DTYPE, name=f"xa{tag}_dwq")
    dwkv = _mm2(memn, dkv, "tn", MXU_DTYPE, name=f"xa{tag}_dwkv", o_split=True, tn=wkv.shape[2])
    tok = put((dwq, dwkv, dwo))
    dhx = _mm2(dq, wq, "nt", F32, name=f"xa{tag}_dh", dep=tok)
    dx, dx_lo, dgx = _rms_bwd(x, g_x, dhx, dxo, name=f"xa{tag}_norm_bwd")
    dmemn = _mm2(dkv, wkv, "nt", F32, name=f"xa{tag}_dmem", b_split=True)
    _, _, dgm = _rms_bwd(mem, g_m, dmemn, None, name=f"xa{tag}_mem_norm_bwd")
    return dx, dx_lo, dgx, dgm


def _ffn_fwd(x, g, wg, wu, wd, tag):
    h = _rms_fwd(x, g, name=f"ffn{tag}_norm")
    gate, up, act = _ffn_up(h, wg, wu, name=f"ffn{tag}_up")
    x_new = _mm(act, wd, "nn", F32, name=f"ffn{tag}_down", reduce_b=True, add=x[None])[0]
    return x_new, (h, gate, up, act)


def _ffn_bwd(x, g, wg, wu, wd, saved, dxo, dxo_lo, tag, put):
    h, gate, up, act = saved
    dwd = _mm(act, dxo_lo[None], "tn", MXU_DTYPE, name=f"ffn{tag}_dwd")
    dgate, dup = _ffn_dact(dxo_lo, wd, gate, up, name=f"ffn{tag}_dact")
    dwg = _mm(dgate, h[None], "tn", MXU_DTYPE, name=f"ffn{tag}_dwg")
    dwu = _mm(dup, h[None], "tn", MXU_DTYPE, name=f"ffn{tag}_dwu")
    tok = put((dwg, dwu, dwd))
    dh = _mm(dgate, wg, "nn", F32, name=f"ffn{tag}_dh_gate", reduce_b=True, dep=tok)
    dh = _mm(dup, wu, "nn", F32, name=f"ffn{tag}_dh_up", reduce_b=True, add=dh)[0]
    dx, dx_lo, dg = _rms_bwd(x, g, dh, dxo, name=f"ffn{tag}_norm_bwd")
    return dx, dx_lo, dg


def _even_fwd(x, g, lb, w_in, w_pool, pool_scale, hg_norm, w_out):
    h = _rms_fwd(x, g, name="ev_norm")
    z = _mm2(h, w_in, "nn", F32, name="ev_in", b_split=True)
    ycat = _pool_fwd(z, w_pool, pool_scale, name="ev_pool")
    ycat, o_raw, states = _hgrn_fwd(z, lb, hg_norm, ycat, name="ev_hgrn")
    return _mm2(ycat, w_out, "nn", F32, name="ev_out", add=x), (h, z, ycat, o_raw, states)


def _even_bwd(x, g, lb, w_in, w_pool, pool_scale, hg_norm, w_out, saved, dxo, dxo_lo, put):
    h, z, ycat, o_raw, states = saved
    dycat = _mm2(dxo_lo, w_out, "nt", MXU_DTYPE, name="ev_dy")
    dw_out = _mm2(ycat, dxo_lo, "tn", MXU_DTYPE, name="ev_dw_out")
    du, dw_pool, dscale = _pool_bwd(z, w_pool, pool_scale, dycat, name="ev_pool_bwd")
    dq, dfl, di, dg, dlb, dhn = _hgrn_bwd(z, lb, hg_norm, o_raw, states, dycat, name="ev_hgrn_bwd")
    dz = jnp.concatenate([du, dq, dfl, di, dg], axis=1)
    dw_in = _mm2(h, dz, "tn", MXU_DTYPE, name="ev_dw_in", o_split=True, tn=w_in.shape[2])
    tok = put((dw_in, dw_pool, dw_out))
    dh = _mm2(dz, w_in, "nt", F32, name="ev_dh", b_split=True, dep=tok)
    dx, dx_lo, dgn = _rms_bwd(x, g, dh, dxo, name="ev_norm_bwd")
    return dx, dx_lo, dict(ev_norm=dgn, ev_pool_scale=dscale, ev_hg_norm=dhn, lb=dlb)


def _odd_fwd(x, g, w_qkv, w_f, b_f, w_out):
    n_qkv = 3 * x.shape[1]
    h = _rms_fwd(x, g, name="od_norm")
    zqkv = _mm2(h, w_qkv, "nt", MXU_DTYPE, name="od_qkv", n_b=n_qkv)
    zf = _mm2(h, w_f, "nt", F32, name="od_gate")
    fcol = _fox_prep(zf, b_f, name="od_fox_prep")
    o, lse = _fox_fwd(zqkv, fcol, name="od_fox")
    return _mm2(o, w_out, "nn", F32, name="od_out", add=x), (h, zqkv, zf, fcol, o, lse)


def _odd_bwd(x, g, w_qkv, w_f, b_f, w_out, saved, dxo, dxo_lo, put):
    h, zqkv, zf, fcol, o, lse = saved
    do = _mm2(dxo_lo, w_out, "nt", MXU_DTYPE, name="od_do")
    dw_out = _mm2(o, dxo_lo, "tn", MXU_DTYPE, name="od_dw_out")
    dq, dk, dv, rq, rk = _fox_bwd(zqkv, fcol, lse, o, do, name="od_fox_bwd")
    dfl, db_f = _fox_gate_bwd(rq[:, 0, :], rk[:, 0, :], zf, b_f, name="od_fox_gate_bwd")
    dz = jnp.concatenate([dq, dk, dv], axis=1)
    dw_qkv = _mm2(dz, h, "tn", MXU_DTYPE, name="od_dw_qkv")
    dw_f = _mm2(dfl, h, "tn", MXU_DTYPE, name="od_dw_gate")
    tok = put((dw_qkv, dw_f, dw_out))
    dh = _mm2(dz, w_qkv, "nn", F32, name="od_dh_qkv", dep=tok)
    dh = _mm2(dfl, w_f, "nn", F32, name="od_dh_gate", add=dh)
    dx, dx_lo, dgn = _rms_bwd(x, g, dh, dxo, name="od_norm_bwd")
    return dx, dx_lo, dict(od_norm=dgn, od_b_f=db_f)


def _local_step(x, mem, target, sp, get_w, put_dw):
    b_f = jnp.pad(sp["od_b_f"], ((0, 0), (0, LANES - sp["od_b_f"].shape[1])))
    lb = _lb_fwd(sp["lb_table"], 0, name="lb_fwd")
    fin = sp["final_norm"].reshape(1, -1)
    xn, xm, fn = sp["xa_norm"], sp["xa_mem_norm"], sp["ffn_norm"]
    w_ev = get_w("ev", None)
    x1, s_ev = _even_fwd(x, sp["ev_norm"], lb, w_ev[0], w_ev[1], sp["ev_pool_scale"], sp["ev_hg_norm"], w_ev[2])
    w_xa0 = get_w("xa0", x1)
    x2, s_xa0 = _xa_fwd(x1, mem, xn[0:1], xm[0:1], *w_xa0, 0)
    w_ff0 = get_w("ffn0", x2)
    x3, s_ff0 = _ffn_fwd(x2, fn[0:1], *w_ff0, 0)
    w_qkv, w_f, w_od_out, od_norm = get_w("od", x3)
    x4, s_od = _odd_fwd(x3, od_norm, w_qkv, w_f, b_f, w_od_out)
    w_xa1 = get_w("xa1", x4)
    x5, s_xa1 = _xa_fwd(x4, mem, xn[1:2], xm[1:2], *w_xa1, 1)
    w_ff1 = get_w("ffn1", x5)
    x6, s_ff1 = _ffn_fwd(x5, fn[1:2], *w_ff1, 1)
    loss, dx, dx_lo, d_fin = _loss_head(x6, fin, target, name="loss_head")
    put = lambda grp: functools.partial(put_dw, grp)
    dx, dx_lo, d_ffn1 = _ffn_bwd(x5, fn[1:2], *w_ff1, s_ff1, dx, dx_lo, 1, put("ffn1"))
    dx, dx_lo, d_xa1, d_xm1 = _xa_bwd(x4, mem, xn[1:2], xm[1:2], *w_xa1, s_xa1, dx, dx_lo, 1, put("xa1"))
    dx, dx_lo, d_od = _odd_bwd(x3, od_norm, w_qkv, w_f, b_f, w_od_out, s_od, dx, dx_lo, put("od"))
    dx, dx_lo, d_ffn0 = _ffn_bwd(x2, fn[0:1], *w_ff0, s_ff0, dx, dx_lo, 0, put("ffn0"))
    dx, dx_lo, d_xa0, d_xm0 = _xa_bwd(x1, mem, xn[0:1], xm[0:1], *w_xa0, s_xa0, dx, dx_lo, 0, put("xa0"))
    dx, _, d_ev = _even_bwd(x, sp["ev_norm"], lb, w_ev[0], w_ev[1], sp["ev_pool_scale"], sp["ev_hg_norm"],
                            w_ev[2], s_ev, dx, dx_lo, put("ev"))
    small = dict(
        lb_table=_lb_bwd(sp["lb_table"], d_ev["lb"], 0, name="lb_bwd"),
        ev_norm=d_ev["ev_norm"], ev_pool_scale=d_ev["ev_pool_scale"], ev_hg_norm=d_ev["ev_hg_norm"],
        od_norm=d_od["od_norm"], od_b_f=d_od["od_b_f"][:, :sp["od_b_f"].shape[1]],
        xa_norm=jnp.concatenate([d_xa0, d_xa1], axis=0), xa_mem_norm=jnp.concatenate([d_xm0, d_xm1], axis=0),
        ffn_norm=jnp.concatenate([d_ffn0, d_ffn1], axis=0), final_norm=d_fin.reshape(-1))
    return loss, dx, small


_SMALL = ("lb_table", "ev_norm", "ev_pool_scale", "ev_hg_norm", "od_norm", "od_b_f", "xa_norm", "xa_mem_norm",
          "ffn_norm", "final_norm")
_WEIGHTS = ("lb_table", "ev_norm", "ev_w_in", "ev_w_pool", "ev_pool_scale", "ev_hg_norm", "ev_w_out", "od_norm",
            "od_w_in", "od_b_f", "od_w_out", "xa_norm", "xa_mem_norm", "xa_wq", "xa_wkv", "xa_wo", "ffn_norm",
            "ffn_w_gate", "ffn_w_up", "ffn_w_down", "final_norm")


def _lo(a):
    return a.astype(MXU_DTYPE)


def _rows(v):
    flat = v.reshape(-1)
    return jnp.pad(flat, (0, (-flat.shape[0]) % LANES)).reshape(-1, LANES)


def kernel(x, mem, lb_table, ev_norm, ev_w_in, ev_w_pool, ev_pool_scale, ev_hg_norm, ev_w_out, od_norm, od_w_in, od_b_f, od_w_out, xa_norm, xa_mem_norm, xa_wq, xa_wkv, xa_wo, ffn_norm, ffn_w_gate, ffn_w_up, ffn_w_down, final_norm, loss_target, m_lb_table, m_ev_norm, m_ev_w_in, m_ev_w_pool, m_ev_pool_scale, m_ev_hg_norm, m_ev_w_out, m_od_norm, m_od_w_in, m_od_b_f, m_od_w_out, m_xa_norm, m_xa_mem_norm, m_xa_wq, m_xa_wkv, m_xa_wo, m_ffn_norm, m_ffn_w_gate, m_ffn_w_up, m_ffn_w_down, m_final_norm, v_lb_table, v_ev_norm, v_ev_w_in, v_ev_w_pool, v_ev_pool_scale, v_ev_hg_norm, v_ev_w_out, v_od_norm, v_od_w_in, v_od_b_f, v_od_w_out, v_xa_norm, v_xa_mem_norm, v_xa_wq, v_xa_wkv, v_xa_wo, v_ffn_norm, v_ffn_w_gate, v_ffn_w_up, v_ffn_w_down, v_final_norm):
    arg = dict(locals())
    d = x.shape[-1]
    layers = xa_wq.shape[0]
    me = _slot(_my_place())

    n_gate = od_b_f.shape[1]
    turned = {k: jnp.swapaxes(arg[k], 1, 2) for k in ("od_w_in", "ffn_w_gate", "ffn_w_up")}
    raw = dict(ev=[ev_w_in[0], ev_w_pool[0], ev_w_out[0]], od=[turned["od_w_in"][0], od_w_out[0], od_norm])
    for l in range(layers):
        raw[f"xa{l}"] = [xa_wq[l], xa_wkv[l], xa_wo[l]]
        raw[f"ffn{l}"] = [turned["ffn_w_gate"][l], turned["ffn_w_up"][l], ffn_w_down[l]]
    order = ("ev", "xa0", "ffn0", "od", "xa1", "ffn1")
    gathers, relays, tok = {}, {}, None
    for grp in order:
        srcs = [w if tok is None else w + tok[0, 0] for w in raw[grp]]
        srcs = [w if grp == "od" and j == 2 else _lo(w) for j, w in enumerate(srcs)]
        gathers[grp], tok = _push_start(srcs, None, per_peer=False, masks=FIRST_HOP, name=f"gather_{grp}_start")
    last_start = tok

    def second_hop(grp, after):
        lands = _push_wait(gathers[grp], after, name=f"gather_{grp}_wait")
        relays[grp], token = _relay_start(lands, name=f"gather_{grp}_relay")
        return token

    def get_w(grp, after):
        i = order.index(grp)
        after = last_start if after is None else after
        if grp not in relays:
            after = second_hop(grp, after)
        if 1 <= i < len(order) - 1:
            after = second_hop(order[i + 1], after)
        got = _relay_wait(relays[grp], after, name=f"gather_{grp}_relay_wait")
        if grp == "ev":
            w_in, w_pool, w_out = got
            w_pool = jnp.transpose(w_pool, (1, 0, 2, 3)).reshape(w_pool.shape[1], -1, w_pool.shape[3])
            return w_in, w_pool, w_out.reshape(d, d)
        if grp == "od":
            w_in, w_out, nrm = got
            w_in = w_in.reshape(-1, d)
            w_f = jnp.pad(w_in[w_in.shape[0] - n_gate:], ((0, LANES - n_gate), (0, 0)))
            return w_in, w_f, w_out.reshape(d, d), nrm.reshape(1, d)
        if grp.startswith("xa"):
            return got[0].reshape(d, d), got[1], got[2].reshape(d, d)
        return tuple(got)

    def row_parts(g):
        return g.reshape(N_DEV, -1, g.shape[-1])

    scatters = {}

    def put_dw(grp, dws):
        if grp == "ev":
            dw_in, dw_pool, dw_out = dws
            gc = dw_pool.shape[1] // N_DEV
            dw_pool = _lo(jnp.transpose(dw_pool.reshape(dw_pool.shape[0], N_DEV, gc, -1), (1, 0, 2, 3)))
            parts = [dw_in, dw_pool, row_parts(dw_out)]
        elif grp == "od":
            dw_qkv, dw_f, dw_out = dws
            parts = [row_parts(jnp.concatenate([dw_qkv, dw_f[:n_gate]], axis=0)), row_parts(dw_out)]
        elif grp.startswith("xa"):
            parts = [row_parts(dws[0]), dws[1], row_parts(dws[2])]
        else:
            parts = list(dws)
        scatters[grp], token = _push_start(parts, None, per_peer=True, name=f"scatter_{grp}_start")
        return token

    sp = {k: arg[k] for k in _SMALL if k != "od_norm"}
    loss, dx, small = _local_step(x[0], mem[0], loss_target[0], sp, get_w, put_dw)

    pieces = [_rows(small[k]) for k in _SMALL]
    packed = jnp.concatenate(pieces + [_rows(loss)], axis=0)
    packed = jnp.pad(packed, ((0, (-packed.shape[0]) % 8), (0, 0)))
    total = _all_reduce_rows(packed, name="all_reduce_small")
    loss = total[sum(pc.shape[0] for pc in pieces), 0]
    small_g, at = {}, 0
    for k, pc in zip(_SMALL, pieces):
        n = small[k].size
        small_g[k] = total[at:at + pc.shape[0]].reshape(-1)[:n].reshape(small[k].shape)
        at += pc.shape[0]
    small_g["od_norm"] = lax.dynamic_slice_in_dim(small_g["od_norm"], me * od_norm.shape[1], od_norm.shape[1], axis=1)

    res = {}
    for k in _SMALL:
        w, m, v = arg[k], arg["m_" + k], arg["v_" + k]
        shp = (1, 1, w.shape[0]) if w.ndim == 1 else (1,) + w.shape
        out = _adamw(w.reshape(shp), m.reshape(shp), v.reshape(shp), small_g[k].reshape(shp), name=f"adamw_{k}")
        res[k] = [o.reshape(w.shape) for o in out]
    members = dict(ev=("ev_w_in", "ev_w_pool", "ev_w_out"), od=("od_w_in", "od_w_out"),
                   xa=("xa_wq", "xa_wkv", "xa_wo"), ffn=("ffn_w_gate", "ffn_w_up", "ffn_w_down"))
    after, stacked = dx, {}
    for grp in ("ffn1", "xa1", "od", "ffn0", "xa0", "ev"):
        got = _push_wait(scatters[grp], after, name=f"scatter_{grp}_wait")
        kind = grp.rstrip("01")
        for k, parts in zip(members[kind], got):
            w, m, v = [jnp.swapaxes(a, 1, 2) if k in turned else a for a in (arg[k], arg["m_" + k], arg["v_" + k])]
            if w.shape[0] == 1:
                shp = (1, -1, w.shape[-1])
                out = _adamw(w.reshape(shp), m.reshape(shp), v.reshape(shp), parts.reshape(N_DEV, -1, w.shape[-1]),
                             name=f"adamw_{k}")
            else:
                out = _adamw(w, m, v, parts, name=f"adamw_{k}{grp[-1]}", layer=int(grp[-1]), prev=stacked.get(k))
                stacked[k] = out
            res[k] = [jnp.swapaxes(o.reshape(w.shape), 1, 2) if k in turned else o.reshape(w.shape) for o in out]
            after = out[3][:1, :8, :LANES]

    outs = [loss, dx[None]]
    for j in range(4):
        outs += [res[k][j] for k in _WEIGHTS]
    return tuple(outs)
```

```python
import functools

import jax
import jax.numpy as jnp
from jax import lax
from jax.experimental import pallas as pl
from jax.experimental.pallas import tpu as pltpu

F32 = jnp.float32
MXU_DTYPE = jnp.bfloat16
EPS = 1e-6
N_DEV = 8
V7X_VMEM_BYTES = 64 * 1024 * 1024
VMEM_LIMIT_BYTES = V7X_VMEM_BYTES - 8 * 1024 * 1024
LANES = 128
HIGHEST = lax.Precision.HIGHEST
MESH = pl.DeviceIdType.MESH

HG_HEAD = 128
HG_CHUNK = 16
FOX_HEAD = 128
XA_HEADS = 4
POOL_GROUPS = 4

ADAM_LR = 0.001
ADAM_B1 = 0.9
ADAM_B2 = 0.999
ADAM_EPS = 1e-08
ADAM_WD = 0.01
ADAM_STEP = 10

_NN = ((1,), (0,))
_NT = ((1,), (1,))
_TN = ((0,), (0,))


def _dot(a, b, dims):
    return lax.dot_general(a.astype(MXU_DTYPE), b.astype(MXU_DTYPE), (dims, ((), ())),
                           preferred_element_type=F32)


def _dot_f32(a, b, dims):
    return lax.dot_general(a, b, (dims, ((), ())), preferred_element_type=F32, precision=HIGHEST)


def _cp(*sem):
    return pltpu.CompilerParams(dimension_semantics=sem, vmem_limit_bytes=VMEM_LIMIT_BYTES)


def _tile(n, pref, align=LANES):
    if n <= pref:
        return n
    t = (pref // align) * align
    while t >= align:
        if n % t == 0:
            return t
        t -= align
    return n


def _sigmoid(x):
    return jax.nn.sigmoid(x)


def _mm(a, b, mode, out_dtype, *, name, add=None, dep=None, reduce_b=False, b_split=False, o_split=False,
        n_b=None, tm=1024, tn=1024, tk=2048):
    ba, bb = a.shape[0], b.shape[0]
    if mode == "tn":
        kdim, m = a.shape[1], a.shape[2]
        tk = 2 * tk
    else:
        m, kdim = a.shape[1], a.shape[2]
    if b_split:
        s_cnt, b_rows, w = b.shape
        if mode == "nt":
            n = b_rows
            assert kdim == s_cnt * w
            tk = w
        else:
            n = s_cnt * w
            assert b_rows == kdim
            tn = w
        nb = ba
    else:
        n = b.shape[1] if mode == "nt" else b.shape[2]
        n = n if n_b is None else n_b
        nb = max(ba, bb)
    if not (b_split and mode != "nt"):
        tn = _tile(n, tn)
    if not (b_split and mode == "nt"):
        tk = _tile(kdim, tk)
    tm = _tile(m, tm)
    assert m % tm == 0 and n % tn == 0 and kdim % tk == 0, (name, m, n, kdim, tm, tn, tk)
    nk = kdim // tk
    if reduce_b:
        grid = (m // tm, n // tn, nb, nk)
        unpack = lambda i, j, bi, k: (bi, i, j, k)
        sem = ("parallel", "parallel", "arbitrary", "arbitrary")
        nred = nb * nk
    else:
        grid = (nb, m // tm, n // tn, nk)
        unpack = lambda bi, i, j, k: (bi, i, j, k)
        sem = ("parallel", "parallel", "parallel", "arbitrary")
        nred = nk

    def a_map(*g):
        bi, i, j, k = unpack(*g)
        ab = bi if ba > 1 else 0
        return (ab, k, i) if mode == "tn" else (ab, i, k)

    def b_map(*g):
        bi, i, j, k = unpack(*g)
        if b_split:
            return (k, j, 0) if mode == "nt" else (j, k, 0)
        bq = bi if bb > 1 else 0
        return (bq, j, k) if mode == "nt" else (bq, k, j)

    def o_map(*g):
        bi, i, j, k = unpack(*g)
        if o_split:
            return (j, i, 0)
        return (0 if reduce_b else bi, i, j)

    a_blk = (1, tk, tm) if mode == "tn" else (1, tm, tk)
    b_blk = (1, tn, tk) if mode == "nt" else (1, tk, tn)
    dims = {"nn": _NN, "nt": _NT, "tn": _TN}[mode]
    has_add = add is not None

    def body(*refs):
        a_ref, b_ref = refs[:2]
        if has_add:
            add_ref = refs[2]
        if nred == 1:
            o_ref = refs[-1]
            r = _dot(a_ref[0], b_ref[0], dims)
            if has_add:
                r = r + add_ref[0].astype(F32)
            o_ref[0] = r.astype(o_ref.dtype)
            return
        o_ref, acc_ref = refs[-2:]
        if reduce_b:
            step = pl.program_id(2) * nk + pl.program_id(3)
        else:
            step = pl.program_id(3)

        @pl.when(step == 0)
        def _():
            acc_ref[...] = _dot(a_ref[0], b_ref[0], dims)

        @pl.when(step > 0)
        def _():
            acc_ref[...] += _dot(a_ref[0], b_ref[0], dims)

        @pl.when(step == nred - 1)
        def _():
            r = acc_ref[...]
            if has_add:
                r = r + add_ref[0].astype(F32)
            o_ref[0] = r.astype(o_ref.dtype)

    in_specs = [pl.BlockSpec(a_blk, a_map), pl.BlockSpec(b_blk, b_map)]
    operands = [a, b]
    if has_add:
        in_specs.append(pl.BlockSpec((1, tm, tn), o_map))
        operands.append(add)
    if dep is not None:
        in_specs.append(pl.BlockSpec(memory_space=pl.ANY))
        operands.append(dep)
    if o_split:
        out_shape = jax.ShapeDtypeStruct((n // tn, m, tn), out_dtype)
    else:
        out_shape = jax.ShapeDtypeStruct((1 if reduce_b else nb, m, n), out_dtype)
    return pl.pallas_call(
        body, grid=grid, in_specs=in_specs, out_specs=pl.BlockSpec((1, tm, tn), o_map),
        out_shape=out_shape, scratch_shapes=[] if nred == 1 else [pltpu.VMEM((tm, tn), F32)],
        compiler_params=_cp(*sem), name=name)(*operands)


def _mm2(a, b, mode, out_dtype, *, name, add=None, **kw):
    b3 = b if kw.get("b_split") else b[None]
    r = _mm(a[None], b3, mode, out_dtype, name=name, add=None if add is None else add[None], **kw)
    return r if kw.get("o_split") else r[0]


def _rms_fwd(x, g, *, name, tb=512):
    t, d = x.shape
    tb = min(tb, t)

    def body(x_ref, g_ref, o_ref):
        xv = x_ref[...]
        r = lax.rsqrt(jnp.mean(xv * xv, axis=-1, keepdims=True) + EPS)
        o_ref[...] = (xv * r * g_ref[...]).astype(o_ref.dtype)

    return pl.pallas_call(
        body, grid=(t // tb,),
        in_specs=[pl.BlockSpec((tb, d), lambda i: (i, 0)), pl.BlockSpec((1, d), lambda i: (0, 0))],
        out_specs=pl.BlockSpec((tb, d), lambda i: (i, 0)),
        out_shape=jax.ShapeDtypeStruct((t, d), MXU_DTYPE), compiler_params=_cp("parallel"), name=name)(x, g)


def _rms_bwd(x, g, dh, dres, *, name, tb=512):
    t, d = x.shape
    tb = min(tb, t)
    has_res = dres is not None

    def body(*refs):
        if has_res:
            x_ref, g_ref, dh_ref, dres_ref, dx_ref, dxl_ref, dg_ref = refs
        else:
            x_ref, g_ref, dh_ref, dx_ref, dxl_ref, dg_ref = refs
        xv = x_ref[...]
        r = lax.rsqrt(jnp.mean(xv * xv, axis=-1, keepdims=True) + EPS)
        xh = xv * r
        dhv = dh_ref[...].astype(F32)

        @pl.when(pl.program_id(0) == 0)
        def _():
            dg_ref[...] = jnp.zeros_like(dg_ref)

        dg_ref[...] += jnp.sum(dhv * xh, axis=0, keepdims=True)
        dxh = dhv * g_ref[...]
        dx = r * (dxh - xh * jnp.mean(dxh * xh, axis=-1, keepdims=True))
        if has_res:
            dx = dx + dres_ref[...]
        dx_ref[...] = dx
        dxl_ref[...] = dx.astype(dxl_ref.dtype)

    row = pl.BlockSpec((tb, d), lambda i: (i, 0))
    vec = pl.BlockSpec((1, d), lambda i: (0, 0))
    operands = [x, g, dh] + ([dres] if has_res else [])
    return pl.pallas_call(
        body, grid=(t // tb,), in_specs=[row, vec, row] + ([row] if has_res else []),
        out_specs=[row, row, vec],
        out_shape=[jax.ShapeDtypeStruct((t, d), F32), jax.ShapeDtypeStruct((t, d), MXU_DTYPE),
                   jax.ShapeDtypeStruct((1, d), F32)],
        compiler_params=_cp("arbitrary"), name=name)(*operands)


def _loss_head(x, g, target, *, name, tb=512):
    t, d = x.shape
    tb = min(tb, t)

    def body(x_ref, g_ref, t_ref, loss_ref, dx_ref, dxl_ref, dg_ref):
        xv = x_ref[...]
        r = lax.rsqrt(jnp.mean(xv * xv, axis=-1, keepdims=True) + EPS)
        xh = xv * r
        gv = g_ref[...]
        err = xh * gv - t_ref[...]

        @pl.when(pl.program_id(0) == 0)
        def _():
            dg_ref[...] = jnp.zeros_like(dg_ref)
            loss_ref[...] = jnp.zeros_like(loss_ref)

        row_loss = jnp.mean(err * err, axis=-1, keepdims=True)
        loss_ref[...] += 0.5 * jnp.sum(row_loss, axis=0, keepdims=True)
        dy = err * (1.0 / d)
        dg_ref[...] += jnp.sum(dy * xh, axis=0, keepdims=True)
        dxh = dy * gv
        dx = r * (dxh - xh * jnp.mean(dxh * xh, axis=-1, keepdims=True))
        dx_ref[...] = dx
        dxl_ref[...] = dx.astype(dxl_ref.dtype)

    row = pl.BlockSpec((tb, d), lambda i: (i, 0))
    vec = pl.BlockSpec((1, d), lambda i: (0, 0))
    return pl.pallas_call(
        body, grid=(t // tb,), in_specs=[row, vec, row],
        out_specs=[pl.BlockSpec((1, 1), lambda i: (0, 0)), row, row, vec],
        out_shape=[jax.ShapeDtypeStruct((1, 1), F32), jax.ShapeDtypeStruct((t, d), F32),
                   jax.ShapeDtypeStruct((t, d), MXU_DTYPE), jax.ShapeDtypeStruct((1, d), F32)],
        compiler_params=_cp("arbitrary"), name=name)(x, g, target)


def _ffn_up(h, wg, wu, *, name, tb=1024):
    t, d = h.shape
    s, f, _ = wg.shape
    tb = min(tb, t)

    def body(h_ref, wg_ref, wu_ref, g_ref, u_ref, a_ref):
        hv = h_ref[...]
        gv = _dot(hv, wg_ref[0], _NT)
        uv = _dot(hv, wu_ref[0], _NT)
        g_ref[0] = gv.astype(g_ref.dtype)
        u_ref[0] = uv.astype(u_ref.dtype)
        a_ref[0] = (gv * _sigmoid(gv) * uv).astype(a_ref.dtype)

    wspec = pl.BlockSpec((1, f, d), lambda j, i: (j, 0, 0))
    ospec = pl.BlockSpec((1, tb, f), lambda j, i: (j, i, 0))
    return pl.pallas_call(
        body, grid=(s, t // tb),
        in_specs=[pl.BlockSpec((tb, d), lambda j, i: (i, 0)), wspec, wspec],
        out_specs=[ospec, ospec, ospec],
        out_shape=[jax.ShapeDtypeStruct((s, t, f), MXU_DTYPE)] * 3,
        compiler_params=_cp("parallel", "parallel"), name=name)(h, wg, wu)


def _ffn_dact(dy, wd, gate, up, *, name, tb=1024):
    t, d = dy.shape
    s, f, _ = wd.shape
    tb = min(tb, t)

    def body(dy_ref, wd_ref, g_ref, u_ref, dg_ref, du_ref):
        da = _dot(dy_ref[...], wd_ref[0], _NT)
        gv = g_ref[0].astype(F32)
        sg = _sigmoid(gv)
        du_ref[0] = (da * gv * sg).astype(du_ref.dtype)
        dg_ref[0] = (da * u_ref[0].astype(F32) * (sg * (1.0 + gv * (1.0 - sg)))).astype(dg_ref.dtype)

    aspec = pl.BlockSpec((1, tb, f), lambda j, i: (j, i, 0))
    return pl.pallas_call(
        body, grid=(s, t // tb),
        in_specs=[pl.BlockSpec((tb, d), lambda j, i: (i, 0)),
                  pl.BlockSpec((1, f, d), lambda j, i: (j, 0, 0)), aspec, aspec],
        out_specs=[aspec, aspec],
        out_shape=[jax.ShapeDtypeStruct((s, t, f), MXU_DTYPE), jax.ShapeDtypeStruct((s, t, f), MXU_DTYPE)],
        compiler_params=_cp("parallel", "parallel"), name=name)(dy, wd, gate, up)


def _ffn_down(act, wd, x, *, name, tm=512, tn=1024):
    s, t, f = act.shape
    d = wd.shape[2]
    tm, tn = _tile(t, tm), _tile(d, tn)

    def body(a_ref, w_ref, x_ref, o_ref):
        r = x_ref[...]
        for j in range(s):
            r = r + _dot(a_ref[j], w_ref[j], _NN)
        o_ref[...] = r

    xspec = pl.BlockSpec((tm, tn), lambda k, i: (i, k))
    return pl.pallas_call(
        body, grid=(d // tn, t // tm),
        in_specs=[pl.BlockSpec((s, tm, f), lambda k, i: (0, i, 0)), pl.BlockSpec((s, f, tn), lambda k, i: (0, 0, k)),
                  xspec],
        out_specs=xspec, out_shape=jax.ShapeDtypeStruct((t, d), F32),
        compiler_params=_cp("parallel", "parallel"), name=name)(act, wd, x)


def _ffn_dh(dgate, dup, wg, wu, dep, *, name, tm=1024, tn=1024):
    s, t, f = dgate.shape
    d = wg.shape[2]
    tm, tn = _tile(t, tm), _tile(d, tn)
    has_dep = dep is not None

    def body(*refs):
        dg_ref, du_ref, wg_ref, wu_ref = refs[:4]
        o_ref, acc_ref = refs[-2:]
        j = pl.program_id(2)
        part = _dot(dg_ref[0], wg_ref[0], _NN) + _dot(du_ref[0], wu_ref[0], _NN)

        @pl.when(j == 0)
        def _():
            acc_ref[...] = part

        @pl.when(j > 0)
        def _():
            acc_ref[...] += part

        @pl.when(j == s - 1)
        def _():
            o_ref[...] = acc_ref[...]

    aspec = pl.BlockSpec((1, tm, f), lambda i, k, j: (j, i, 0))
    wspec = pl.BlockSpec((1, f, tn), lambda i, k, j: (j, 0, k))
    return pl.pallas_call(
        body, grid=(t // tm, d // tn, s),
        in_specs=[aspec, aspec, wspec, wspec] + ([pl.BlockSpec(memory_space=pl.ANY)] if has_dep else []),
        out_specs=pl.BlockSpec((tm, tn), lambda i, k, j: (i, k)),
        out_shape=jax.ShapeDtypeStruct((t, d), F32), scratch_shapes=[pltpu.VMEM((tm, tn), F32)],
        compiler_params=_cp("parallel", "parallel", "arbitrary"),
        name=name)(dgate, dup, wg, wu, *([dep] if has_dep else []))


def _xattn_fwd(q, kv, *, name, tb=512):
    t, d = q.shape
    m = kv.shape[0]
    hd = d // XA_HEADS
    tb = min(tb, t)
    scale = hd ** -0.5

    def body(q_ref, kv_ref, o_ref):
        for hh in range(XA_HEADS):
            cs = slice(hh * hd, (hh + 1) * hd)
            s = _dot(q_ref[:, cs], kv_ref[:, cs], _NT) * scale
            s = s - jnp.max(s, axis=-1, keepdims=True)
            e = jnp.exp(s)
            p = e / jnp.sum(e, axis=-1, keepdims=True)
            o_ref[:, cs] = _dot(p, kv_ref[:, d + hh * hd:d + (hh + 1) * hd], _NN).astype(o_ref.dtype)

    return pl.pallas_call(
        body, grid=(t // tb,),
        in_specs=[pl.BlockSpec((tb, d), lambda i: (i, 0)), pl.BlockSpec((m, 2 * d), lambda i: (0, 0))],
        out_specs=pl.BlockSpec((tb, d), lambda i: (i, 0)),
        out_shape=jax.ShapeDtypeStruct((t, d), MXU_DTYPE), compiler_params=_cp("parallel"), name=name)(q, kv)


def _xattn_bwd(q, kv, do, *, name, tb=512):
    t, d = q.shape
    m = kv.shape[0]
    hd = d // XA_HEADS
    tb = min(tb, t)
    scale = hd ** -0.5

    def body(q_ref, kv_ref, do_ref, dq_ref, dkv_ref):
        @pl.when(pl.program_id(0) == 0)
        def _():
            dkv_ref[...] = jnp.zeros_like(dkv_ref)

        for hh in range(XA_HEADS):
            cs = slice(hh * hd, (hh + 1) * hd)
            vs = slice(d + hh * hd, d + (hh + 1) * hd)
            qv, kk, vv, dov = q_ref[:, cs], kv_ref[:, cs], kv_ref[:, vs], do_ref[:, cs]
            s = _dot(qv, kk, _NT) * scale
            s = s - jnp.max(s, axis=-1, keepdims=True)
            e = jnp.exp(s)
            p = e / jnp.sum(e, axis=-1, keepdims=True)
            dkv_ref[:, vs] += _dot(p, dov, _TN)
            dp = _dot(dov, vv, _NT)
            ds = p * (dp - jnp.sum(p * dp, axis=-1, keepdims=True)) * scale
            dq_ref[:, cs] = _dot(ds, kk, _NN).astype(dq_ref.dtype)
            dkv_ref[:, cs] += _dot(ds, qv, _TN)

    row = pl.BlockSpec((tb, d), lambda i: (i, 0))
    full = pl.BlockSpec((m, 2 * d), lambda i: (0, 0))
    return pl.pallas_call(
        body, grid=(t // tb,), in_specs=[row, full, row], out_specs=[row, full],
        out_shape=[jax.ShapeDtypeStruct((t, d), MXU_DTYPE), jax.ShapeDtypeStruct((m, 2 * d), F32)],
        compiler_params=_cp("arbitrary"), name=name)(q, kv, do)


def _pool_window_stats(u, gi, reverse):
    t = u.shape[0]
    row = lax.broadcasted_iota(jnp.int32, u.shape, 0)
    s = u
    for j in range(POOL_GROUPS):
        sh = 1 << j
        if reverse:
            rolled = jnp.where(row < t - sh, pltpu.roll(s, t - sh, axis=0), 0.0)
        else:
            rolled = jnp.where(row >= sh, pltpu.roll(s, sh, axis=0), 0.0)
        s = jnp.where(j <= gi, s + rolled, s)
    return s, row


def _pool_fwd(z, w_pool, scale, *, name):
    t = z.shape[0]
    g_cnt, c, _ = w_pool.shape

    def body(z_ref, w_ref, s_ref, o_ref):
        gi = pl.program_id(0)
        u = z_ref[...]
        win, row = _pool_window_stats(u, gi, False)
        cnt = jnp.minimum(row + 1, lax.shift_left(jnp.int32(2), gi)).astype(F32)
        p = win / cnt - u
        o_ref[...] = (_dot(p, w_ref[0], _NN) * s_ref[...]).astype(o_ref.dtype)

    return pl.pallas_call(
        body, grid=(g_cnt,),
        in_specs=[pl.BlockSpec((t, c), lambda g: (0, g)), pl.BlockSpec((1, c, c), lambda g: (g, 0, 0)),
                  pl.BlockSpec((1, c), lambda g: (0, g))],
        out_specs=pl.BlockSpec((t, c), lambda g: (0, g)),
        out_shape=jax.ShapeDtypeStruct((t, 2 * g_cnt * c), MXU_DTYPE),
        compiler_params=_cp("parallel"), name=name)(z, w_pool, scale)


def _pool_bwd(z, w_pool, scale, dycat, *, name):
    t = z.shape[0]
    g_cnt, c, _ = w_pool.shape

    def body(z_ref, w_ref, s_ref, dy_ref, du_ref, dw_ref, ds_ref):
        gi = pl.program_id(0)
        u = z_ref[...]
        win, row = _pool_window_stats(u, gi, False)
        cnt = jnp.minimum(row + 1, lax.shift_left(jnp.int32(2), gi)).astype(F32)
        p = win / cnt - u
        y = _dot(p, w_ref[0], _NN)
        dya = dy_ref[...].astype(F32)
        ds_ref[...] = jnp.sum(dya * y, axis=0, keepdims=True)
        dy = dya * s_ref[...]
        dw_ref[0] = _dot(p, dy, _TN)
        dp = _dot(dy, w_ref[0], _NT)
        back, _ = _pool_window_stats(dp / cnt, gi, True)
        du_ref[...] = (back - dp).astype(du_ref.dtype)

    col = pl.BlockSpec((t, c), lambda g: (0, g))
    return pl.pallas_call(
        body, grid=(g_cnt,),
        in_specs=[col, pl.BlockSpec((1, c, c), lambda g: (g, 0, 0)), pl.BlockSpec((1, c), lambda g: (0, g)), col],
        out_specs=[col, pl.BlockSpec((1, c, c), lambda g: (g, 0, 0)), pl.BlockSpec((1, c), lambda g: (0, g))],
        out_shape=[jax.ShapeDtypeStruct((t, g_cnt * c), MXU_DTYPE), jax.ShapeDtypeStruct((g_cnt, c, c), F32),
                   jax.ShapeDtypeStruct((1, g_cnt * c), F32)],
        compiler_params=_cp("parallel"), name=name)(z, w_pool, scale, dycat)


def _chunk_tri(lower):
    r = lax.broadcasted_iota(jnp.int32, (LANES, LANES), 0)
    c = lax.broadcasted_iota(jnp.int32, (LANES, LANES), 1)
    same = (r // HG_CHUNK) == (c // HG_CHUNK)
    return jnp.where(same & ((c <= r) if lower else (c >= r)), 1.0, 0.0).astype(F32)


def _hgrn_prepare(q_ref, f_ref, lb_ref, qh_s, k_s, b_s, qt_s, kt_s, gl_s):
    tb = q_ref.shape[0]
    lb = lb_ref[...]
    sg = _sigmoid(f_ref[...])
    f = lb + (1.0 - lb) * sg
    logf = jnp.log(f)
    qv = q_ref[...]
    qh = qv * _sigmoid(qv) * (HG_HEAD ** -0.5)
    tri = _chunk_tri(True)
    for r in range(tb // LANES):
        rows = slice(r * LANES, (r + 1) * LANES)
        b_s[rows, :] = _dot_f32(tri, logf[rows, :], _NN)
    b = b_s[...]
    b3 = b.reshape(tb // HG_CHUNK, HG_CHUNK, HG_HEAD)
    bl = b3[:, HG_CHUNK - 1:HG_CHUNK, :]
    k = 1.0 - f
    qh_s[...] = qh
    k_s[...] = k
    qt_s[...] = qh * jnp.exp(b)
    kt_s[...] = k * jnp.exp(bl - b3).reshape(tb, HG_HEAD)
    gl_s[...] = jnp.exp(jnp.broadcast_to(bl, b3.shape)).reshape(tb, HG_HEAD)
    return sg, f


def _hgrn_intra(qh, kk, bq, rows_a, rows_b):
    ones = jnp.ones((HG_HEAD, HG_HEAD), MXU_DTYPE)
    es, stack_a, stack_b = [], [], []
    for s in range(HG_CHUNK):
        e = jnp.exp(jnp.minimum(bq - bq[s:s + 1, :], 0.0))
        es.append(e)
        stack_a.append(qh * e * kk[s:s + 1, :])
        if rows_a is not None:
            stack_b.append(rows_a * rows_b[s:s + 1, :])
    a_rep = _dot(jnp.concatenate(stack_a, axis=0), ones, _NN)
    d_rep = _dot(jnp.concatenate(stack_b, axis=0), ones, _NN) if rows_a is not None else None
    return es, a_rep, d_rep


def _hgrn_fwd(z, lb, hg_norm, ycat, *, name, tb=512):
    t = z.shape[0]
    mix_b = lb.shape[1]
    heads = mix_b // HG_HEAD
    off = (z.shape[1] - 4 * mix_b) // HG_HEAD
    tb = min(tb, t)
    ncb = tb // HG_CHUNK

    def body(q_ref, f_ref, i_ref, g_ref, lb_ref, hn_ref, ycat_in, y_ref, o_ref, st_ref,
             state, qh_s, k_s, b_s, qt_s, kt_s, gl_s, o_s):
        del ycat_in

        @pl.when(pl.program_id(1) == 0)
        def _():
            state[...] = jnp.zeros_like(state)

        _hgrn_prepare(q_ref, f_ref, lb_ref, qh_s, k_s, b_s, qt_s, kt_s, gl_s)
        row = lax.broadcasted_iota(jnp.int32, (HG_CHUNK, HG_HEAD), 0)

        def chunk(c, carry):
            rows = pl.ds(pl.multiple_of(c * HG_CHUNK, HG_CHUNK), HG_CHUNK)
            st = state[...]
            st_ref[0, c] = st
            vv = i_ref[rows, :]
            o = _dot(qt_s[rows, :], st, _NT)
            _, a_rep, _ = _hgrn_intra(qh_s[rows, :], k_s[rows, :], b_s[rows, :], None, None)
            for s in range(HG_CHUNK):
                o = o + jnp.where(row >= s, a_rep[s * HG_CHUNK:(s + 1) * HG_CHUNK, :] * vv[s:s + 1, :], 0.0)
            o_s[rows, :] = o
            state[...] = st * gl_s[rows, :][0:1, :] + _dot(vv, kt_s[rows, :], _TN)
            return carry

        lax.fori_loop(0, ncb, chunk, 0, unroll=2)
        o = o_s[...]
        o_ref[...] = o
        r = lax.rsqrt(jnp.mean(o * o, axis=-1, keepdims=True) + EPS)
        gv = g_ref[...]
        y_ref[...] = (o * r * hn_ref[...] * (gv * _sigmoid(gv))).astype(y_ref.dtype)

    def zcol(kind):
        return pl.BlockSpec((tb, HG_HEAD), lambda h, i: (i, off + kind * heads + h))

    scratch = [pltpu.VMEM((HG_HEAD, HG_HEAD), F32)] + [pltpu.VMEM((tb, HG_HEAD), F32)] * 7
    return pl.pallas_call(
        body, grid=(heads, t // tb),
        in_specs=[zcol(0), zcol(1), zcol(2), zcol(3), pl.BlockSpec((1, HG_HEAD), lambda h, i: (0, h)),
                  pl.BlockSpec((1, HG_HEAD), lambda h, i: (0, 0)), pl.BlockSpec(memory_space=pl.ANY)],
        out_specs=[pl.BlockSpec((tb, HG_HEAD), lambda h, i: (i, heads + h)),
                   pl.BlockSpec((tb, HG_HEAD), lambda h, i: (i, h)),
                   pl.BlockSpec((1, ncb, HG_HEAD, HG_HEAD), lambda h, i: (h, i, 0, 0))],
        out_shape=[jax.ShapeDtypeStruct(ycat.shape, ycat.dtype), jax.ShapeDtypeStruct((t, mix_b), F32),
                   jax.ShapeDtypeStruct((heads, t // HG_CHUNK, HG_HEAD, HG_HEAD), F32)],
        scratch_shapes=scratch, input_output_aliases={6: 0},
        compiler_params=_cp("parallel", "arbitrary"), name=name)(z, z, z, z, lb, hg_norm, ycat)


def _hgrn_bwd(z, lb, hg_norm, o_raw, states, dycat, *, name, tb=512):
    t = z.shape[0]
    mix_b = lb.shape[1]
    heads = mix_b // HG_HEAD
    off = (z.shape[1] - 4 * mix_b) // HG_HEAD
    tb = min(tb, t)
    ncb = tb // HG_CHUNK
    nt = t // tb

    def body(q_ref, f_ref, i_ref, g_ref, lb_ref, hn_ref, o_ref, st_ref, dy_ref,
             dq_ref, dfl_ref, di_ref, dg_ref, dlb_ref, dhn_ref,
             dstate, qh_s, k_s, b_s, qt_s, kt_s, gl_s, do_s, dqh_s, dk_s, db_s):
        first = pl.program_id(1) == 0

        @pl.when(first)
        def _():
            dstate[...] = jnp.zeros_like(dstate)
            dlb_ref[...] = jnp.zeros_like(dlb_ref)

        @pl.when(first & (pl.program_id(0) == 0))
        def _():
            dhn_ref[...] = jnp.zeros_like(dhn_ref)

        sg, f = _hgrn_prepare(q_ref, f_ref, lb_ref, qh_s, k_s, b_s, qt_s, kt_s, gl_s)
        o = o_ref[...]
        r = lax.rsqrt(jnp.mean(o * o, axis=-1, keepdims=True) + EPS)
        oh = o * r
        gv = g_ref[...]
        sgg = _sigmoid(gv)
        dy = dy_ref[...].astype(F32)
        hn = hn_ref[...]
        dg_ref[...] = (dy * oh * hn * (sgg * (1.0 + gv * (1.0 - sgg)))).astype(dg_ref.dtype)
        don = dy * (gv * sgg)
        dhn_ref[...] += jnp.sum(don * oh, axis=0, keepdims=True)
        doh = don * hn
        do_s[...] = r * (doh - oh * jnp.mean(doh * oh, axis=-1, keepdims=True))
        row = lax.broadcasted_iota(jnp.int32, (HG_CHUNK, HG_HEAD), 0)

        def chunk(ci, carry):
            c = ncb - 1 - ci
            rows = pl.ds(pl.multiple_of(c * HG_CHUNK, HG_CHUNK), HG_CHUNK)
            st_prev = st_ref[0, c]
            dst = dstate[...]
            qh, kk, bq, vv = qh_s[rows, :], k_s[rows, :], b_s[rows, :], i_ref[rows, :]
            qt, kt, doo = qt_s[rows, :], kt_s[rows, :], do_s[rows, :]
            gl = gl_s[rows, :][0:1, :]
            es, a_rep, d_rep = _hgrn_intra(qh, kk, bq, doo, vv)
            dqh = jnp.exp(bq) * _dot(doo, st_prev, _NN)
            dk = jnp.exp(bq[HG_CHUNK - 1:HG_CHUNK, :] - bq) * _dot(vv, dst, _NN)
            dv = _dot(kt, dst, _NT)
            for s in range(HG_CHUNK):
                blk = slice(s * HG_CHUNK, (s + 1) * HG_CHUNK)
                wgt = jnp.where(row >= s, d_rep[blk, :] * es[s], 0.0)
                dqh = dqh + wgt * kk[s:s + 1, :]
                dk = dk + jnp.where(row == s, jnp.sum(wgt * qh, axis=0, keepdims=True), 0.0)
                dv_row = jnp.sum(jnp.where(row >= s, a_rep[blk, :] * doo, 0.0), axis=0, keepdims=True)
                dv = dv + jnp.where(row == s, dv_row, 0.0)
            st_next = st_prev * gl + _dot(vv, kt, _TN)
            db = qh * dqh - kk * dk
            db = db + jnp.where(row == HG_CHUNK - 1, jnp.sum(st_next * dst, axis=0, keepdims=True), 0.0)
            dstate[...] = dst * gl + _dot(doo, qt, _TN)
            dqh_s[rows, :] = dqh
            dk_s[rows, :] = dk
            db_s[rows, :] = db
            di_ref[rows, :] = dv.astype(di_ref.dtype)
            return carry

        lax.fori_loop(0, ncb, chunk, 0, unroll=2)
        tri = _chunk_tri(False)
        lb_v = lb_ref[...]
        qv = q_ref[...]
        sgq = _sigmoid(qv)
        dq_ref[...] = (dqh_s[...] * (HG_HEAD ** -0.5) * (sgq * (1.0 + qv * (1.0 - sgq)))).astype(dq_ref.dtype)
        dlb = jnp.zeros((1, HG_HEAD), F32)
        for rr in range(tb // LANES):
            rws = slice(rr * LANES, (rr + 1) * LANES)
            dlogf = _dot_f32(tri, db_s[rws, :], _NN)
            df = dlogf / f[rws, :] - dk_s[rws, :]
            sgr = sg[rws, :]
            dfl_ref[rws, :] = (df * (1.0 - lb_v) * sgr * (1.0 - sgr)).astype(dfl_ref.dtype)
            dlb = dlb + jnp.sum(df * (1.0 - sgr), axis=0, keepdims=True)
        dlb_ref[...] += dlb

    def zcol(kind):
        return pl.BlockSpec((tb, HG_HEAD), lambda h, i: (nt - 1 - i, off + kind * heads + h))

    hcol = pl.BlockSpec((tb, HG_HEAD), lambda h, i: (nt - 1 - i, h))
    scratch = [pltpu.VMEM((HG_HEAD, HG_HEAD), F32)] + [pltpu.VMEM((tb, HG_HEAD), F32)] * 10
    out = jax.ShapeDtypeStruct((t, mix_b), MXU_DTYPE)
    return pl.pallas_call(
        body, grid=(heads, nt),
        in_specs=[zcol(0), zcol(1), zcol(2), zcol(3), pl.BlockSpec((1, HG_HEAD), lambda h, i: (0, h)),
                  pl.BlockSpec((1, HG_HEAD), lambda h, i: (0, 0)), hcol,
                  pl.BlockSpec((1, ncb, HG_HEAD, HG_HEAD), lambda h, i: (h, nt - 1 - i, 0, 0)),
                  pl.BlockSpec((tb, HG_HEAD), lambda h, i: (nt - 1 - i, heads + h))],
        out_specs=[hcol, hcol, hcol, hcol, pl.BlockSpec((1, HG_HEAD), lambda h, i: (0, h)),
                   pl.BlockSpec((1, HG_HEAD), lambda h, i: (0, 0))],
        out_shape=[out, out, out, out, jax.ShapeDtypeStruct((1, mix_b), F32),
                   jax.ShapeDtypeStruct((1, HG_HEAD), F32)],
        scratch_shapes=scratch, compiler_params=_cp("arbitrary", "arbitrary"),
        name=name)(z, z, z, z, lb, hg_norm, o_raw, states, dycat)


def _lb_fwd(lb_table, layer, *, name):
    rows, width = lb_table.shape

    def body(t_ref, o_ref):
        tv = t_ref[...]
        e = jnp.exp(tv - jnp.max(tv, axis=0, keepdims=True))
        sm = e / jnp.sum(e, axis=0, keepdims=True)
        o_ref[...] = jnp.sum(sm[1:layer + 2, :], axis=0, keepdims=True)

    return pl.pallas_call(body, out_shape=jax.ShapeDtypeStruct((1, width), F32), name=name)(lb_table)


def _lb_bwd(lb_table, dlb, layer, *, name):
    rows, width = lb_table.shape

    def body(t_ref, d_ref, o_ref):
        tv = t_ref[...]
        e = jnp.exp(tv - jnp.max(tv, axis=0, keepdims=True))
        sm = e / jnp.sum(e, axis=0, keepdims=True)
        ridx = lax.broadcasted_iota(jnp.int32, sm.shape, 0)
        dsm = jnp.where((ridx >= 1) & (ridx <= layer + 1), d_ref[...], 0.0)
        o_ref[...] = sm * (dsm - jnp.sum(sm * dsm, axis=0, keepdims=True))

    return pl.pallas_call(body, out_shape=jax.ShapeDtypeStruct((rows, width), F32), name=name)(lb_table, dlb)


FOX_BLOCK = 512


def _fox_prep(zf, b_f, *, name, blk=256):
    t = zf.shape[0]

    def body(z_ref, b_ref, fc_ref):
        r = lax.broadcasted_iota(jnp.int32, (blk, blk), 0)
        c = lax.broadcasted_iota(jnp.int32, (blk, blk), 1)
        tri = jnp.where(c <= r, 1.0, 0.0).astype(F32)
        carry = jnp.zeros((1, LANES), F32)
        for j in range(t // blk):
            rows = slice(j * blk, (j + 1) * blk)
            ls = jax.nn.log_sigmoid(z_ref[rows, :] + b_ref[...])
            fb = _dot_f32(tri, ls, _NN) + carry
            carry = fb[blk - 1:blk, :]
            fc_ref[rows, :] = fb

    return pl.pallas_call(
        body, out_shape=jax.ShapeDtypeStruct((t, LANES), F32),
        compiler_params=pltpu.CompilerParams(vmem_limit_bytes=VMEM_LIMIT_BYTES), name=name)(zf, b_f)


def _fox_head_column(fc_ref, fk_s, head):
    lane = lax.broadcasted_iota(jnp.int32, fc_ref.shape, 1)
    fk_s[...] = jnp.sum(jnp.where(lane == head, fc_ref[...], 0.0), axis=1, keepdims=True)


def _fox_scores(k_blk, q_blk, fk_blk, diagonal):
    s = _dot(k_blk, q_blk, _NT) * (FOX_HEAD ** -0.5) - fk_blk
    if diagonal:
        key = lax.broadcasted_iota(jnp.int32, s.shape, 0)
        qry = lax.broadcasted_iota(jnp.int32, s.shape, 1)
        s = jnp.where(key <= qry, s, -jnp.inf)
    return s


def _fox_fwd(zqkv, fcol, *, name):
    t = zqkv.shape[0]
    d = zqkv.shape[1] // 3
    heads = d // FOX_HEAD
    blk = min(FOX_BLOCK, t)
    nq = t // blk

    def body(q_ref, k_ref, v_ref, fc_ref, o_ref, lse_ref, fk_s):
        _fox_head_column(fc_ref, fk_s, pl.program_id(0))

        def q_block(qi, carry):
            qrows = pl.ds(pl.multiple_of(qi * blk, blk), blk)
            q_blk = q_ref[qrows, :]

            def update(st, krows, diagonal):
                m, l, acc = st
                s = _fox_scores(k_ref[krows, :], q_blk, fk_s[krows, :], diagonal)
                m_new = jnp.maximum(m, jnp.max(s, axis=0, keepdims=True))
                alpha = jnp.exp(m - m_new)
                p = jnp.exp(s - m_new)
                l = alpha * l + jnp.sum(p, axis=0, keepdims=True)
                acc = acc * alpha + _dot(v_ref[krows, :], p, _TN)
                return m_new, l, acc

            def k_block(kj, st):
                return update(st, pl.ds(pl.multiple_of(kj * blk, blk), blk), False)

            init = (jnp.full((1, blk), -jnp.inf, F32), jnp.zeros((1, blk), F32),
                    jnp.zeros((FOX_HEAD, blk), F32))
            m, l, acc = update(lax.fori_loop(0, qi, k_block, init), qrows, True)
            o_ref[qrows, :] = (acc / l).T.astype(o_ref.dtype)
            lse_ref[0, :, qrows] = m + jnp.log(l)
            return carry

        lax.fori_loop(0, nq, q_block, 0)

    def col(kind):
        return pl.BlockSpec((t, FOX_HEAD), lambda h: (0, kind * heads + h))

    rowvec = pl.BlockSpec((1, 1, t), lambda h: (h, 0, 0))
    return pl.pallas_call(
        body, grid=(heads,),
        in_specs=[col(0), col(1), col(2), pl.BlockSpec((t, LANES), lambda h: (0, 0))],
        out_specs=[pl.BlockSpec((t, FOX_HEAD), lambda h: (0, h)), rowvec],
        out_shape=[jax.ShapeDtypeStruct((t, d), MXU_DTYPE), jax.ShapeDtypeStruct((heads, 1, t), F32)],
        scratch_shapes=[pltpu.VMEM((t, 1), F32)],
        compiler_params=_cp("parallel"), name=name)(zqkv, zqkv, zqkv, fcol)


def _fox_bwd(zqkv, fcol, lse, o, do, *, name):
    t = zqkv.shape[0]
    d = zqkv.shape[1] // 3
    heads = d // FOX_HEAD
    blk = min(FOX_BLOCK, t)
    nq = t // blk
    scale = FOX_HEAD ** -0.5

    def body(q_ref, k_ref, v_ref, fc_ref, lse_ref, o_ref, do_ref,
             dq_ref, dk_ref, dv_ref, rq_ref, rk_ref, dq_s, drow_s, fk_s, acc_s):
        _fox_head_column(fc_ref, fk_s, pl.program_id(0))
        ones = jnp.ones((blk, FOX_HEAD), MXU_DTYPE)
        dq_s[...] = jnp.zeros_like(dq_s)
        rq_ref[...] = jnp.zeros_like(rq_ref)
        ones_f = jnp.ones((8, FOX_HEAD), F32)
        for j in range(nq):
            rows = slice(j * blk, (j + 1) * blk)
            prod = do_ref[rows, :].astype(F32) * o_ref[rows, :].astype(F32)
            drow_s[:, rows] = _dot_f32(ones_f, prod, _NT)

        def k_block(kj, carry):
            krows = pl.ds(pl.multiple_of(kj * blk, blk), blk)
            k_blk, v_blk, fk_blk = k_ref[krows, :], v_ref[krows, :], fk_s[krows, :]

            def pair(qrows, diagonal):
                q_blk, do_blk = q_ref[qrows, :], do_ref[qrows, :]
                s = _fox_scores(k_blk, q_blk, fk_blk, diagonal)
                p = jnp.exp(s - lse_ref[0, :, qrows])
                acc_s[1] += _dot(p, do_blk, _NN)
                dp = _dot(v_blk, do_blk, _NT)
                ds = (p * (dp - drow_s[0:1, qrows])).astype(MXU_DTYPE)
                acc_s[0] += _dot(ds, q_blk, _NN)
                dq_s[qrows, :] += _dot(ds, k_blk, _TN)
                rq_ref[0, :, qrows] += jnp.sum(ds.astype(F32), axis=0, keepdims=True)
                acc_s[2] += _dot(ds, ones, _NN)

            def q_block(qi, carry2):
                pair(pl.ds(pl.multiple_of(qi * blk, blk), blk), False)
                return carry2

            acc_s[...] = jnp.zeros_like(acc_s)
            pair(krows, True)
            lax.fori_loop(kj + 1, nq, q_block, 0)
            dk_ref[krows, :] = (acc_s[0] * scale).astype(dk_ref.dtype)
            dv_ref[krows, :] = acc_s[1].astype(dv_ref.dtype)
            rk_ref[0, :, krows] = acc_s[2].T[0:1, :]
            return carry

        lax.fori_loop(0, nq, k_block, 0)
        dq_ref[...] = (dq_s[...] * scale).astype(dq_ref.dtype)

    def col(kind):
        return pl.BlockSpec((t, FOX_HEAD), lambda h: (0, kind * heads + h))

    hcol = pl.BlockSpec((t, FOX_HEAD), lambda h: (0, h))
    rowvec = pl.BlockSpec((1, 1, t), lambda h: (h, 0, 0))
    out = jax.ShapeDtypeStruct((t, d), MXU_DTYPE)
    vec = jax.ShapeDtypeStruct((heads, 1, t), F32)
    return pl.pallas_call(
        body, grid=(heads,),
        in_specs=[col(0), col(1), col(2), pl.BlockSpec((t, LANES), lambda h: (0, 0)), rowvec, hcol, hcol],
        out_specs=[hcol, hcol, hcol, rowvec, rowvec],
        out_shape=[out, out, out, vec, vec],
        scratch_shapes=[pltpu.VMEM((t, FOX_HEAD), F32), pltpu.VMEM((8, t), F32), pltpu.VMEM((t, 1), F32),
                        pltpu.VMEM((3, blk, FOX_HEAD), F32)],
        compiler_params=_cp("parallel"), name=name)(zqkv, zqkv, zqkv, fcol, lse, o, do)


def _fox_gate_bwd(rq, rk, zf, b_f, *, name, blk=256):
    heads, t = rq.shape

    def body(rq_ref, rk_ref, z_ref, b_ref, dfl_ref, db_ref):
        r = lax.broadcasted_iota(jnp.int32, (blk, blk), 0)
        c = lax.broadcasted_iota(jnp.int32, (blk, blk), 1)
        tri = jnp.where(r >= c, 1.0, 0.0).astype(F32)
        carry = jnp.zeros((heads, 1), F32)
        db = jnp.zeros((1, LANES), F32)
        pad = jnp.zeros((LANES - heads, blk), F32)
        for j in reversed(range(t // blk)):
            cols = slice(j * blk, (j + 1) * blk)
            df = rq_ref[:, cols] - rk_ref[:, cols]
            dls = _dot_f32(df, tri, _NN) + carry
            carry = dls[:, 0:1]
            dls_t = jnp.concatenate([dls, pad], axis=0).T
            dfl = dls_t * _sigmoid(-(z_ref[cols, :] + b_ref[...]))
            dfl_ref[cols, :] = dfl.astype(dfl_ref.dtype)
            db = db + jnp.sum(dfl, axis=0, keepdims=True)
        db_ref[...] = db

    return pl.pallas_call(
        body, out_shape=[jax.ShapeDtypeStruct((t, LANES), MXU_DTYPE), jax.ShapeDtypeStruct((1, LANES), F32)],
        compiler_params=pltpu.CompilerParams(vmem_limit_bytes=VMEM_LIMIT_BYTES), name=name)(rq, rk, zf, b_f)


def _adamw(w, m, v, parts, *, name, layer=None, prev=None, tr=128):
    lcnt, r, c = w.shape
    p = parts.shape[0]
    li = 0 if layer is None else layer
    tr = _tile(r, tr, 16)
    tc = c if tr < r or r * c <= 128 * 2048 else _tile(c, 256)
    has_prev = prev is not None

    def body(*refs):
        w_ref, m_ref, v_ref, p_ref = refs[:4]
        g_ref, d_ref, nm_ref, nv_ref = refs[-4:]
        g = p_ref[0].astype(F32)
        for j in range(1, p):
            g = g + p_ref[j].astype(F32)
        wv = w_ref[0]
        mn = ADAM_B1 * m_ref[0] + (1.0 - ADAM_B1) * g
        vn = ADAM_B2 * v_ref[0] + (1.0 - ADAM_B2) * (g * g)
        m_hat = mn / (1.0 - ADAM_B1 ** ADAM_STEP)
        v_hat = vn / (1.0 - ADAM_B2 ** ADAM_STEP)
        g_ref[0] = g
        d_ref[0] = -ADAM_LR * (m_hat / (jnp.sqrt(v_hat) + ADAM_EPS) + ADAM_WD * wv)
        nm_ref[0] = mn
        nv_ref[0] = vn

    slab = pl.BlockSpec((1, tr, tc), lambda i, j: (li, i, j))
    in_specs = [slab, slab, slab, pl.BlockSpec((p, tr, tc), lambda i, j: (0, i, j))]
    operands = [w, m, v, parts]
    aliases = {}
    if has_prev:
        in_specs += [pl.BlockSpec(memory_space=pl.ANY)] * 4
        operands += list(prev)
        aliases = {4: 0, 5: 1, 6: 2, 7: 3}
    shp = jax.ShapeDtypeStruct((lcnt, r, c), F32)
    return pl.pallas_call(
        body, grid=(r // tr, c // tc), in_specs=in_specs, out_specs=[slab] * 4, out_shape=[shp] * 4,
        input_output_aliases=aliases, compiler_params=_cp("parallel", "parallel"), name=name)(*operands)


def _my_place():
    return lax.axis_index("x"), lax.axis_index("y"), lax.axis_index("c")


def _slot(p):
    return 4 * p[0] + 2 * p[1] + p[2]


def _peer(me, mask):
    x, y, c = me
    return (1 - x if mask & 4 else x, 1 - y if mask & 2 else y, 1 - c if mask & 1 else c)


_HBM = pl.BlockSpec(memory_space=pltpu.HBM)
_SEM = pl.BlockSpec(memory_space=pltpu.SEMAPHORE)
_ANY = pl.BlockSpec(memory_space=pl.ANY)
_EFFECT = pltpu.SideEffectType.DATAFLOW_SIDE_EFFECTING


def _push_copy(src_refs, land_refs, send_sems, recv_sems, a, mask, me, per_peer, outgoing):
    peer = _peer(me, mask)
    src = src_refs[a].at[_slot(peer)] if per_peer else src_refs[a]
    dst = land_refs[a].at[_slot(me) if outgoing else _slot(peer)]
    k = a * (N_DEV - 1) + mask - 1
    return pltpu.make_async_remote_copy(
        src_ref=src, dst_ref=dst, send_sem=send_sems.at[k], recv_sem=recv_sems.at[k],
        device_id=peer, device_id_type=MESH)


ALL_PEERS = tuple(range(1, N_DEV))
CHIP_PEERS = (2, 4, 6)
FIRST_HOP = (1,) + CHIP_PEERS


def _push_start(srcs, dep, *, per_peer, name, masks=ALL_PEERS):
    n = len(srcs)
    mine = _slot(_my_place())
    lands = []
    for s in srcs:
        own = lax.dynamic_index_in_dim(s, mine, 0, keepdims=True) if per_peer else s[None]
        shape = s.shape if per_peer else (N_DEV,) + s.shape
        lands.append(lax.dynamic_update_slice_in_dim(lax.empty(shape, s.dtype), own, mine, 0))
    has_dep = dep is not None

    def body(*refs):
        src_refs, land_refs = refs[:n], refs[n:2 * n]
        send_sems, recv_sems = refs[2 * n + has_dep], refs[2 * n + has_dep + 1]
        token = refs[-1]
        me = _my_place()
        for a in range(n):
            for mask in masks:
                _push_copy(src_refs, land_refs, send_sems, recv_sems, a, mask, me, per_peer, True).start()
        token[...] = jnp.zeros_like(token)

    hbm_in = [pltpu.with_memory_space_constraint(v, pltpu.HBM) for v in list(srcs) + lands]
    out = pl.pallas_call(
        body, name=name,
        out_shape=(pltpu.SemaphoreType.DMA((n * (N_DEV - 1),)), pltpu.SemaphoreType.DMA((n * (N_DEV - 1),)),
                   *[pltpu.HBM(v.shape, v.dtype) for v in hbm_in], jax.ShapeDtypeStruct((8, LANES), F32)),
        in_specs=[_HBM] * (2 * n) + ([_ANY] if has_dep else []),
        out_specs=(_SEM, _SEM, *[_HBM] * (2 * n), pl.BlockSpec(memory_space=pltpu.VMEM)),
        input_output_aliases={i: 2 + i for i in range(2 * n)},
        compiler_params=pltpu.CompilerParams(has_side_effects=_EFFECT),
    )(*hbm_in, *([dep] if has_dep else []))
    return (n, per_peer, masks, out[:-1]), out[-1]


def _push_wait(handle, after, *, name):
    n, per_peer, masks, (send_sems, recv_sems, *bufs) = handle

    def body(*refs):
        src_refs, land_refs = refs[:n], refs[n:2 * n]
        send_sems, recv_sems = refs[2 * n], refs[2 * n + 1]
        me = _my_place()
        for a in range(n):
            for mask in masks:
                cp = _push_copy(src_refs, land_refs, send_sems, recv_sems, a, mask, me, per_peer, False)
                cp.wait_send()
                cp.wait_recv()

    out = pl.pallas_call(
        body, name=name, out_shape=tuple(pltpu.HBM(v.shape, v.dtype) for v in bufs),
        in_specs=[_HBM] * (2 * n) + [_SEM, _SEM, _ANY], out_specs=tuple([_HBM] * (2 * n)),
        input_output_aliases={i: i for i in range(2 * n)},
        compiler_params=pltpu.CompilerParams(has_side_effects=_EFFECT),
    )(*bufs, send_sems, recv_sems, after)
    return list(out[n:])


def _relay_copy(land_refs, send_sems, recv_sems, a, j, me, outgoing):
    sibling = _peer(me, 1)
    out_slot = _slot(_peer(me, CHIP_PEERS[j]))
    in_slot = _slot(_peer(sibling, CHIP_PEERS[j]))
    k = a * len(CHIP_PEERS) + j
    return pltpu.make_async_remote_copy(
        src_ref=land_refs[a].at[out_slot], dst_ref=land_refs[a].at[out_slot if outgoing else in_slot],
        send_sem=send_sems.at[k], recv_sem=recv_sems.at[k], device_id=sibling, device_id_type=MESH)


def _relay_start(lands, *, name):
    n = len(lands)

    def body(*refs):
        land_refs, send_sems, recv_sems, token = refs[:n], refs[n], refs[n + 1], refs[-1]
        me = _my_place()
        for a in range(n):
            for j in range(len(CHIP_PEERS)):
                _relay_copy(land_refs, send_sems, recv_sems, a, j, me, True).start()
        token[...] = jnp.zeros_like(token)

    hbm_in = [pltpu.with_memory_space_constraint(v, pltpu.HBM) for v in lands]
    n_sem = n * len(CHIP_PEERS)
    out = pl.pallas_call(
        body, name=name,
        out_shape=(pltpu.SemaphoreType.DMA((n_sem,)), pltpu.SemaphoreType.DMA((n_sem,)),
                   *[pltpu.HBM(v.shape, v.dtype) for v in hbm_in], jax.ShapeDtypeStruct((8, LANES), F32)),
        in_specs=[_HBM] * n, out_specs=(_SEM, _SEM, *[_HBM] * n, pl.BlockSpec(memory_space=pltpu.VMEM)),
        input_output_aliases={i: 2 + i for i in range(n)},
        compiler_params=pltpu.CompilerParams(has_side_effects=_EFFECT),
    )(*hbm_in)
    return (n, out[:-1]), out[-1]


def _relay_wait(handle, after, *, name):
    n, (send_sems, recv_sems, *bufs) = handle

    def body(*refs):
        land_refs, send_sems, recv_sems = refs[:n], refs[n], refs[n + 1]
        me = _my_place()
        for a in range(n):
            for j in range(len(CHIP_PEERS)):
                cp = _relay_copy(land_refs, send_sems, recv_sems, a, j, me, False)
                cp.wait_send()
                cp.wait_recv()

    out = pl.pallas_call(
        body, name=name, out_shape=tuple(pltpu.HBM(v.shape, v.dtype) for v in bufs),
        in_specs=[_HBM] * n + [_SEM, _SEM, _ANY], out_specs=tuple([_HBM] * n),
        input_output_aliases={i: i for i in range(n)},
        compiler_params=pltpu.CompilerParams(has_side_effects=_EFFECT),
    )(*bufs, send_sems, recv_sems, after)
    return list(out)


def _all_reduce_rows(v, *, name):
    r, c = v.shape

    def body(v_ref, o_ref, buf, send_sems, recv_sems):
        me = _my_place()
        mine = _slot(me)
        sends = []
        for mask in range(1, N_DEV):
            peer = _peer(me, mask)
            sends.append(pltpu.make_async_remote_copy(
                src_ref=v_ref, dst_ref=buf.at[mine], send_sem=send_sems.at[mask - 1],
                recv_sem=recv_sems.at[mask - 1], device_id=peer, device_id_type=MESH))
        for cp in sends:
            cp.start()
        buf[mine] = v_ref[...]
        for mask in range(1, N_DEV):
            peer = _peer(me, mask)
            pltpu.make_async_remote_copy(
                src_ref=v_ref, dst_ref=buf.at[_slot(peer)], send_sem=send_sems.at[mask - 1],
                recv_sem=recv_sems.at[mask - 1], device_id=peer, device_id_type=MESH).wait_recv()
        for cp in sends:
            cp.wait_send()
        total = buf[0]
        for j in range(1, N_DEV):
            total = total + buf[j]
        o_ref[...] = total

    vm = pl.BlockSpec(memory_space=pltpu.VMEM)
    return pl.pallas_call(
        body, in_specs=[vm], out_specs=vm, out_shape=jax.ShapeDtypeStruct((r, c), F32),
        scratch_shapes=[pltpu.VMEM((N_DEV, r, c), F32), pltpu.SemaphoreType.DMA((7,)),
                        pltpu.SemaphoreType.DMA((7,))],
        name=name)(v)


def _xa_fwd(x, mem, g_x, g_m, wq, wkv, wo, tag):
    hx = _rms_fwd(x, g_x, name=f"xa{tag}_norm")
    memn = _rms_fwd(mem, g_m, name=f"xa{tag}_mem_norm")
    q = _mm2(hx, wq, "nn", MXU_DTYPE, name=f"xa{tag}_q")
    kv = _mm2(memn, wkv, "nn", MXU_DTYPE, name=f"xa{tag}_kv", b_split=True)
    o = _xattn_fwd(q, kv, name=f"xa{tag}_attn")
    return _mm2(o, wo, "nn", F32, name=f"xa{tag}_out", add=x), (hx, memn, q, kv, o)


def _xa_bwd(x, mem, g_x, g_m, wq, wkv, wo, saved, dxo, dxo_lo, tag, put):
    hx, memn, q, kv, o = saved
    do = _mm2(dxo_lo, wo, "nt", MXU_DTYPE, name=f"xa{tag}_do")
    dwo = _mm2(o, dxo_lo, "tn", MXU_DTYPE, name=f"xa{tag}_dwo")
    dq, dkv = _xattn_bwd(q, kv, do, name=f"xa{tag}_attn_bwd")
    dwq = _mm2(hx, dq, "tn", MXU_DTYPE, name=f"xa{tag}_dwq")
    dwkv = _mm2(memn, dkv, "tn", MXU_DTYPE, name=f"xa{tag}_dwkv", o_split=True, tn=wkv.shape[2])
    tok = put((dwq, dwkv, dwo))
    dhx = _mm2(dq, wq, "nt", F32, name=f"xa{tag}_dh", dep=tok)
    dx, dx_lo, dgx = _rms_bwd(x, g_x, dhx, dxo, name=f"xa{tag}_norm_bwd")
    dmemn = _mm2(dkv, wkv, "nt", F32, name=f"xa{tag}_dmem", b_split=True)
    _, _, dgm = _rms_bwd(mem, g_m, dmemn, None, name=f"xa{tag}_mem_norm_bwd")
    return dx, dx_lo, dgx, dgm


def _ffn_fwd(x, g, wg, wu, wd, tag):
    h = _rms_fwd(x, g, name=f"ffn{tag}_norm")
    gate, up, act = _ffn_up(h, wg, wu, name=f"ffn{tag}_up")
    return _ffn_down(act, wd, x, name=f"ffn{tag}_down"), (h, gate, up, act)


def _ffn_bwd(x, g, wg, wu, wd, saved, dxo, dxo_lo, tag, put):
    h, gate, up, act = saved
    dwd = _mm(act, dxo_lo[None], "tn", MXU_DTYPE, name=f"ffn{tag}_dwd")
    dgate, dup = _ffn_dact(dxo_lo, wd, gate, up, name=f"ffn{tag}_dact")
    dwg = _mm(dgate, h[None], "tn", MXU_DTYPE, name=f"ffn{tag}_dwg")
    dwu = _mm(dup, h[None], "tn", MXU_DTYPE, name=f"ffn{tag}_dwu")
    tok = put((dwg, dwu, dwd))
    dh = _ffn_dh(dgate, dup, wg, wu, tok, name=f"ffn{tag}_dh")
    dx, dx_lo, dg = _rms_bwd(x, g, dh, dxo, name=f"ffn{tag}_norm_bwd")
    return dx, dx_lo, dg


def _even_fwd(x, g, lb, w_in, w_pool, pool_scale, hg_norm, w_out):
    h = _rms_fwd(x, g, name="ev_norm")
    z = _mm2(h, w_in, "nn", F32, name="ev_in", b_split=True)
    ycat = _pool_fwd(z, w_pool, pool_scale, name="ev_pool")
    ycat, o_raw, states = _hgrn_fwd(z, lb, hg_norm, ycat, name="ev_hgrn")
    return _mm2(ycat, w_out, "nn", F32, name="ev_out", add=x), (h, z, ycat, o_raw, states)


def _even_bwd(x, g, lb, w_in, w_pool, pool_scale, hg_norm, w_out, saved, dxo, dxo_lo, put):
    h, z, ycat, o_raw, states = saved
    dycat = _mm2(dxo_lo, w_out, "nt", MXU_DTYPE, name="ev_dy")
    dw_out = _mm2(ycat, dxo_lo, "tn", MXU_DTYPE, name="ev_dw_out")
    du, dw_pool, dscale = _pool_bwd(z, w_pool, pool_scale, dycat, name="ev_pool_bwd")
    dq, dfl, di, dg, dlb, dhn = _hgrn_bwd(z, lb, hg_norm, o_raw, states, dycat, name="ev_hgrn_bwd")
    dz = jnp.concatenate([du, dq, dfl, di, dg], axis=1)
    dw_in = _mm2(h, dz, "tn", MXU_DTYPE, name="ev_dw_in", o_split=True, tn=w_in.shape[2])
    tok = put((dw_in, dw_pool, dw_out))
    dh = _mm2(dz, w_in, "nt", F32, name="ev_dh", b_split=True, dep=tok)
    dx, dx_lo, dgn = _rms_bwd(x, g, dh, dxo, name="ev_norm_bwd")
    return dx, dx_lo, dict(ev_norm=dgn, ev_pool_scale=dscale, ev_hg_norm=dhn, lb=dlb)


def _odd_fwd(x, g, w_qkv, w_f, b_f, w_out):
    n_qkv = 3 * x.shape[1]
    h = _rms_fwd(x, g, name="od_norm")
    zqkv = _mm2(h, w_qkv, "nt", MXU_DTYPE, name="od_qkv", n_b=n_qkv)
    zf = _mm2(h, w_f, "nt", F32, name="od_gate")
    fcol = _fox_prep(zf, b_f, name="od_fox_prep")
    o, lse = _fox_fwd(zqkv, fcol, name="od_fox")
    return _mm2(o, w_out, "nn", F32, name="od_out", add=x), (h, zqkv, zf, fcol, o, lse)


def _odd_bwd(x, g, w_qkv, w_f, b_f, w_out, saved, dxo, dxo_lo, put):
    h, zqkv, zf, fcol, o, lse = saved
    do = _mm2(dxo_lo, w_out, "nt", MXU_DTYPE, name="od_do")
    dw_out = _mm2(o, dxo_lo, "tn", MXU_DTYPE, name="od_dw_out")
    dq, dk, dv, rq, rk = _fox_bwd(zqkv, fcol, lse, o, do, name="od_fox_bwd")
    dfl, db_f = _fox_gate_bwd(rq[:, 0, :], rk[:, 0, :], zf, b_f, name="od_fox_gate_bwd")
    dz = jnp.concatenate([dq, dk, dv], axis=1)
    dw_qkv = _mm2(dz, h, "tn", MXU_DTYPE, name="od_dw_qkv")
    dw_f = _mm2(dfl, h, "tn", MXU_DTYPE, name="od_dw_gate")
    tok = put((dw_qkv, dw_f, dw_out))
    dh = _mm2(dz, w_qkv, "nn", F32, name="od_dh_qkv", dep=tok)
    dh = _mm2(dfl, w_f, "nn", F32, name="od_dh_gate", add=dh)
    dx, dx_lo, dgn = _rms_bwd(x, g, dh, dxo, name="od_norm_bwd")
    return dx, dx_lo, dict(od_norm=dgn, od_b_f=db_f)


def _local_step(x, mem, target, sp, get_w, put_dw):
    b_f = jnp.pad(sp["od_b_f"], ((0, 0), (0, LANES - sp["od_b_f"].shape[1])))
    lb = _lb_fwd(sp["lb_table"], 0, name="lb_fwd")
    fin = sp["final_norm"].reshape(1, -1)
    xn, xm, fn = sp["xa_norm"], sp["xa_mem_norm"], sp["ffn_norm"]
    w_ev = get_w("ev", None)
    x1, s_ev = _even_fwd(x, sp["ev_norm"], lb, w_ev[0], w_ev[1], sp["ev_pool_scale"], sp["ev_hg_norm"], w_ev[2])
    w_xa0 = get_w("xa0", x1)
    x2, s_xa0 = _xa_fwd(x1, mem, xn[0:1], xm[0:1], *w_xa0, 0)
    w_ff0 = get_w("ffn0", x2)
    x3, s_ff0 = _ffn_fwd(x2, fn[0:1], *w_ff0, 0)
    w_qkv, w_f, w_od_out, od_norm = get_w("od", x3)
    x4, s_od = _odd_fwd(x3, od_norm, w_qkv, w_f, b_f, w_od_out)
    w_xa1 = get_w("xa1", x4)
    x5, s_xa1 = _xa_fwd(x4, mem, xn[1:2], xm[1:2], *w_xa1, 1)
    w_ff1 = get_w("ffn1", x5)
    x6, s_ff1 = _ffn_fwd(x5, fn[1:2], *w_ff1, 1)
    loss, dx, dx_lo, d_fin = _loss_head(x6, fin, target, name="loss_head")
    put = lambda grp: functools.partial(put_dw, grp)
    dx, dx_lo, d_ffn1 = _ffn_bwd(x5, fn[1:2], *w_ff1, s_ff1, dx, dx_lo, 1, put("ffn1"))
    dx, dx_lo, d_xa1, d_xm1 = _xa_bwd(x4, mem, xn[1:2], xm[1:2], *w_xa1, s_xa1, dx, dx_lo, 1, put("xa1"))
    dx, dx_lo, d_od = _odd_bwd(x3, od_norm, w_qkv, w_f, b_f, w_od_out, s_od, dx, dx_lo, put("od"))
    dx, dx_lo, d_ffn0 = _ffn_bwd(x2, fn[0:1], *w_ff0, s_ff0, dx, dx_lo, 0, put("ffn0"))
    dx, dx_lo, d_xa0, d_xm0 = _xa_bwd(x1, mem, xn[0:1], xm[0:1], *w_xa0, s_xa0, dx, dx_lo, 0, put("xa0"))
    dx, _, d_ev = _even_bwd(x, sp["ev_norm"], lb, w_ev[0], w_ev[1], sp["ev_pool_scale"], sp["ev_hg_norm"],
                            w_ev[2], s_ev, dx, dx_lo, put("ev"))
    small = dict(
        lb_table=_lb_bwd(sp["lb_table"], d_ev["lb"], 0, name="lb_bwd"),
        ev_norm=d_ev["ev_norm"], ev_pool_scale=d_ev["ev_pool_scale"], ev_hg_norm=d_ev["ev_hg_norm"],
        od_norm=d_od["od_norm"], od_b_f=d_od["od_b_f"][:, :sp["od_b_f"].shape[1]],
        xa_norm=jnp.concatenate([d_xa0, d_xa1], axis=0), xa_mem_norm=jnp.concatenate([d_xm0, d_xm1], axis=0),
        ffn_norm=jnp.concatenate([d_ffn0, d_ffn1], axis=0), final_norm=d_fin.reshape(-1))
    return loss, dx, small


_SMALL = ("lb_table", "ev_norm", "ev_pool_scale", "ev_hg_norm", "od_norm", "od_b_f", "xa_norm", "xa_mem_norm",
          "ffn_norm", "final_norm")
_WEIGHTS = ("lb_table", "ev_norm", "ev_w_in", "ev_w_pool", "ev_pool_scale", "ev_hg_norm", "ev_w_out", "od_norm",
            "od_w_in", "od_b_f", "od_w_out", "xa_norm", "xa_mem_norm", "xa_wq", "xa_wkv", "xa_wo", "ffn_norm",
            "ffn_w_gate", "ffn_w_up", "ffn_w_down", "final_norm")


def _lo(a):
    return a.astype(MXU_DTYPE)


def _rows(v):
    flat = v.reshape(-1)
    return jnp.pad(flat, (0, (-flat.shape[0]) % LANES)).reshape(-1, LANES)


def kernel(x, mem, lb_table, ev_norm, ev_w_in, ev_w_pool, ev_pool_scale, ev_hg_norm, ev_w_out, od_norm, od_w_in, od_b_f, od_w_out, xa_norm, xa_mem_norm, xa_wq, xa_wkv, xa_wo, ffn_norm, ffn_w_gate, ffn_w_up, ffn_w_down, final_norm, loss_target, m_lb_table, m_ev_norm, m_ev_w_in, m_ev_w_pool, m_ev_pool_scale, m_ev_hg_norm, m_ev_w_out, m_od_norm, m_od_w_in, m_od_b_f, m_od_w_out, m_xa_norm, m_xa_mem_norm, m_xa_wq, m_xa_wkv, m_xa_wo, m_ffn_norm, m_ffn_w_gate, m_ffn_w_up, m_ffn_w_down, m_final_norm, v_lb_table, v_ev_norm, v_ev_w_in, v_ev_w_pool, v_ev_pool_scale, v_ev_hg_norm, v_ev_w_out, v_od_norm, v_od_w_in, v_od_b_f, v_od_w_out, v_xa_norm, v_xa_mem_norm, v_xa_wq, v_xa_wkv, v_xa_wo, v_ffn_norm, v_ffn_w_gate, v_ffn_w_up, v_ffn_w_down, v_final_norm):
    arg = dict(locals())
    d = x.shape[-1]
    layers = xa_wq.shape[0]
    me = _slot(_my_place())

    n_gate = od_b_f.shape[1]
    turned = {k: jnp.swapaxes(arg[k], 1, 2) for k in ("od_w_in", "ffn_w_gate", "ffn_w_up")}
    raw = dict(ev=[ev_w_in[0], ev_w_pool[0], ev_w_out[0]], od=[turned["od_w_in"][0], od_w_out[0], od_norm])
    for l in range(layers):
        raw[f"xa{l}"] = [xa_wq[l], xa_wkv[l], xa_wo[l]]
        raw[f"ffn{l}"] = [turned["ffn_w_gate"][l], turned["ffn_w_up"][l], ffn_w_down[l]]
    order = ("ev", "xa0", "ffn0", "od", "xa1", "ffn1")
    gathers, relays, tok = {}, {}, None
    for grp in order:
        srcs = [w if tok is None else w + tok[0, 0] for w in raw[grp]]
        srcs = [w if grp == "od" and j == 2 else _lo(w) for j, w in enumerate(srcs)]
        gathers[grp], tok = _push_start(srcs, None, per_peer=False, masks=FIRST_HOP, name=f"gather_{grp}_start")
    last_start = tok

    def second_hop(grp, after):
        lands = _push_wait(gathers[grp], after, name=f"gather_{grp}_wait")
        relays[grp], token = _relay_start(lands, name=f"gather_{grp}_relay")
        return token

    def get_w(grp, after):
        i = order.index(grp)
        after = last_start if after is None else after
        if grp not in relays:
            after = second_hop(grp, after)
        if 1 <= i < len(order) - 1:
            after = second_hop(order[i + 1], after)
        got = _relay_wait(relays[grp], after, name=f"gather_{grp}_relay_wait")
        if grp == "ev":
            w_in, w_pool, w_out = got
            w_pool = jnp.transpose(w_pool, (1, 0, 2, 3)).reshape(w_pool.shape[1], -1, w_pool.shape[3])
            return w_in, w_pool, w_out.reshape(d, d)
        if grp == "od":
            w_in, w_out, nrm = got
            w_in = w_in.reshape(-1, d)
            w_f = jnp.pad(w_in[w_in.shape[0] - n_gate:], ((0, LANES - n_gate), (0, 0)))
            return w_in, w_f, w_out.reshape(d, d), nrm.reshape(1, d)
        if grp.startswith("xa"):
            return got[0].reshape(d, d), got[1], got[2].reshape(d, d)
        return tuple(got)

    def row_parts(g):
        return g.reshape(N_DEV, -1, g.shape[-1])

    scatters = {}

    def put_dw(grp, dws):
        if grp == "ev":
            dw_in, dw_pool, dw_out = dws
            gc = dw_pool.shape[1] // N_DEV
            dw_pool = _lo(jnp.transpose(dw_pool.reshape(dw_pool.shape[0], N_DEV, gc, -1), (1, 0, 2, 3)))
            parts = [dw_in, dw_pool, row_parts(dw_out)]
        elif grp == "od":
            dw_qkv, dw_f, dw_out = dws
            parts = [row_parts(jnp.concatenate([dw_qkv, dw_f[:n_gate]], axis=0)), row_parts(dw_out)]
        elif grp.startswith("xa"):
            parts = [row_parts(dws[0]), dws[1], row_parts(dws[2])]
        else:
            parts = list(dws)
        scatters[grp], token = _push_start(parts, None, per_peer=True, name=f"scatter_{grp}_start")
        return token

    sp = {k: arg[k] for k in _SMALL if k != "od_norm"}
    loss, dx, small = _local_step(x[0], mem[0], loss_target[0], sp, get_w, put_dw)

    pieces = [_rows(small[k]) for k in _SMALL]
    packed = jnp.concatenate(pieces + [_rows(loss)], axis=0)
    packed = jnp.pad(packed, ((0, (-packed.shape[0]) % 8), (0, 0)))
    total = _all_reduce_rows(packed, name="all_reduce_small")
    loss = total[sum(pc.shape[0] for pc in pieces), 0]
    small_g, at = {}, 0
    for k, pc in zip(_SMALL, pieces):
        n = small[k].size
        small_g[k] = total[at:at + pc.shape[0]].reshape(-1)[:n].reshape(small[k].shape)
        at += pc.shape[0]
    small_g["od_norm"] = lax.dynamic_slice_in_dim(small_g["od_norm"], me * od_norm.shape[1], od_norm.shape[1], axis=1)

    res = {}
    for k in _SMALL:
        w, m, v = arg[k], arg["m_" + k], arg["v_" + k]
        shp = (1, 1, w.shape[0]) if w.ndim == 1 else (1,) + w.shape
        out = _adamw(w.reshape(shp), m.reshape(shp), v.reshape(shp), small_g[k].reshape(shp), name=f"adamw_{k}")
        res[k] = [o.reshape(w.shape) for o in out]
    members = dict(ev=("ev_w_in", "ev_w_pool", "ev_w_out"), od=("od_w_in", "od_w_out"),
                   xa=("xa_wq", "xa_wkv", "xa_wo"), ffn=("ffn_w_gate", "ffn_w_up", "ffn_w_down"))
    after, stacked = dx, {}
    for grp in ("ffn1", "xa1", "od", "ffn0", "xa0", "ev"):
        got = _push_wait(scatters[grp], after, name=f"scatter_{grp}_wait")
        kind = grp.rstrip("01")
        for k, parts in zip(members[kind], got):
            w, m, v = [jnp.swapaxes(a, 1, 2) if k in turned else a for a in (arg[k], arg["m_" + k], arg["v_" + k])]
            if w.shape[0] == 1:
                shp = (1, -1, w.shape[-1])
                out = _adamw(w.reshape(shp), m.reshape(shp), v.reshape(shp), parts.reshape(N_DEV, -1, w.shape[-1]),
                             name=f"adamw_{k}")
            else:
                out = _adamw(w, m, v, parts, name=f"adamw_{k}{grp[-1]}", layer=int(grp[-1]), prev=stacked.get(k))
                stacked[k] = out
            res[k] = [jnp.swapaxes(o.reshape(w.shape), 1, 2) if k in turned else o.reshape(w.shape) for o in out]
            after = out[3][:1, :8, :LANES]

    outs = [loss, dx[None]]
    for j in range(4):
        outs += [res[k][j] for k in _WEIGHTS]
    return tuple(outs)
```

```python
import functools

import jax
import jax.numpy as jnp
from jax import lax
from jax.experimental import pallas as pl
from jax.experimental.pallas import tpu as pltpu

F32 = jnp.float32
MXU_DTYPE = jnp.bfloat16
EPS = 1e-6
N_DEV = 8
V7X_VMEM_BYTES = 64 * 1024 * 1024
VMEM_LIMIT_BYTES = V7X_VMEM_BYTES - 8 * 1024 * 1024
LANES = 128
HIGHEST = lax.Precision.HIGHEST
MESH = pl.DeviceIdType.MESH

HG_HEAD = 128
HG_CHUNK = 16
FOX_HEAD = 128
XA_HEADS = 4
POOL_GROUPS = 4

ADAM_LR = 0.001
ADAM_B1 = 0.9
ADAM_B2 = 0.999
ADAM_EPS = 1e-08
ADAM_WD = 0.01
ADAM_STEP = 10

_NN = ((1,), (0,))
_NT = ((1,), (1,))
_TN = ((0,), (0,))


def _dot(a, b, dims):
    return lax.dot_general(a.astype(MXU_DTYPE), b.astype(MXU_DTYPE), (dims, ((), ())),
                           preferred_element_type=F32)


def _dot_f32(a, b, dims):
    return lax.dot_general(a, b, (dims, ((), ())), preferred_element_type=F32, precision=HIGHEST)


def _cp(*sem):
    return pltpu.CompilerParams(dimension_semantics=sem, vmem_limit_bytes=VMEM_LIMIT_BYTES)


def _tile(n, pref, align=LANES):
    if n <= pref:
        return n
    t = (pref // align) * align
    while t >= align:
        if n % t == 0:
            return t
        t -= align
    return n


def _sigmoid(x):
    return jax.nn.sigmoid(x)


def _mm(a, b, mode, out_dtype, *, name, add=None, dep=None, reduce_b=False, b_split=False, o_split=False,
        n_b=None, tm=1024, tn=1024, tk=2048):
    ba, bb = a.shape[0], b.shape[0]
    if mode == "tn":
        kdim, m = a.shape[1], a.shape[2]
        tk = 2 * tk
    else:
        m, kdim = a.shape[1], a.shape[2]
    if b_split:
        s_cnt, b_rows, w = b.shape
        if mode == "nt":
            n = b_rows
            assert kdim == s_cnt * w
            tk = w
        else:
            n = s_cnt * w
            assert b_rows == kdim
            tn = w
        nb = ba
    else:
        n = b.shape[1] if mode == "nt" else b.shape[2]
        n = n if n_b is None else n_b
        nb = max(ba, bb)
    if not (b_split and mode != "nt"):
        tn = _tile(n, tn)
    if not (b_split and mode == "nt"):
        tk = _tile(kdim, tk)
    tm = _tile(m, tm)
    assert m % tm == 0 and n % tn == 0 and kdim % tk == 0, (name, m, n, kdim, tm, tn, tk)
    nk = kdim // tk
    if reduce_b:
        grid = (m // tm, n // tn, nb, nk)
        unpack = lambda i, j, bi, k: (bi, i, j, k)
        sem = ("parallel", "parallel", "arbitrary", "arbitrary")
        nred = nb * nk
    else:
        grid = (nb, m // tm, n // tn, nk)
        unpack = lambda bi, i, j, k: (bi, i, j, k)
        sem = ("parallel", "parallel", "parallel", "arbitrary")
        nred = nk

    def a_map(*g):
        bi, i, j, k = unpack(*g)
        ab = bi if ba > 1 else 0
        return (ab, k, i) if mode == "tn" else (ab, i, k)

    def b_map(*g):
        bi, i, j, k = unpack(*g)
        if b_split:
            return (k, j, 0) if mode == "nt" else (j, k, 0)
        bq = bi if bb > 1 else 0
        return (bq, j, k) if mode == "nt" else (bq, k, j)

    def o_map(*g):
        bi, i, j, k = unpack(*g)
        if o_split:
            return (j, i, 0)
        return (0 if reduce_b else bi, i, j)

    a_blk = (1, tk, tm) if mode == "tn" else (1, tm, tk)
    b_blk = (1, tn, tk) if mode == "nt" else (1, tk, tn)
    dims = {"nn": _NN, "nt": _NT, "tn": _TN}[mode]
    has_add = add is not None

    def body(*refs):
        a_ref, b_ref = refs[:2]
        if has_add:
            add_ref = refs[2]
        if nred == 1:
            o_ref = refs[-1]
            r = _dot(a_ref[0], b_ref[0], dims)
            if has_add:
                r = r + add_ref[0].astype(F32)
            o_ref[0] = r.astype(o_ref.dtype)
            return
        o_ref, acc_ref = refs[-2:]
        if reduce_b:
            step = pl.program_id(2) * nk + pl.program_id(3)
        else:
            step = pl.program_id(3)

        @pl.when(step == 0)
        def _():
            acc_ref[...] = _dot(a_ref[0], b_ref[0], dims)

        @pl.when(step > 0)
        def _():
            acc_ref[...] += _dot(a_ref[0], b_ref[0], dims)

        @pl.when(step == nred - 1)
        def _():
            r = acc_ref[...]
            if has_add:
                r = r + add_ref[0].astype(F32)
            o_ref[0] = r.astype(o_ref.dtype)

    in_specs = [pl.BlockSpec(a_blk, a_map), pl.BlockSpec(b_blk, b_map)]
    operands = [a, b]
    if has_add:
        in_specs.append(pl.BlockSpec((1, tm, tn), o_map))
        operands.append(add)
    if dep is not None:
        in_specs.append(pl.BlockSpec(memory_space=pl.ANY))
        operands.append(dep)
    if o_split:
        out_shape = jax.ShapeDtypeStruct((n // tn, m, tn), out_dtype)
    else:
        out_shape = jax.ShapeDtypeStruct((1 if reduce_b else nb, m, n), out_dtype)
    return pl.pallas_call(
        body, grid=grid, in_specs=in_specs, out_specs=pl.BlockSpec((1, tm, tn), o_map),
        out_shape=out_shape, scratch_shapes=[] if nred == 1 else [pltpu.VMEM((tm, tn), F32)],
        compiler_params=_cp(*sem), name=name)(*operands)


def _mm_nt_split(a, b, dep, *, name, tm=512, tn=1024):
    m, k = a.shape
    s, n, w = b.shape
    assert k == s * w and w % LANES == 0
    tm, tn = _tile(m, tm), _tile(n, tn)
    has_dep = dep is not None

    def body(*refs):
        a_ref, b_ref, o_ref = refs[0], refs[1], refs[-1]
        r = _dot(a_ref[:, 0:w], b_ref[0], _NT)
        for q in range(1, s):
            r = r + _dot(a_ref[:, q * w:(q + 1) * w], b_ref[q], _NT)
        o_ref[...] = r

    return pl.pallas_call(
        body, grid=(n // tn, m // tm),
        in_specs=[pl.BlockSpec((tm, k), lambda j, i: (i, 0)), pl.BlockSpec((s, tn, w), lambda j, i: (0, j, 0))]
        + ([pl.BlockSpec(memory_space=pl.ANY)] if has_dep else []),
        out_specs=pl.BlockSpec((tm, tn), lambda j, i: (i, j)), out_shape=jax.ShapeDtypeStruct((m, n), F32),
        compiler_params=_cp("parallel", "parallel"), name=name)(a, b, *([dep] if has_dep else []))


def _mm2(a, b, mode, out_dtype, *, name, add=None, **kw):
    b3 = b if kw.get("b_split") else b[None]
    r = _mm(a[None], b3, mode, out_dtype, name=name, add=None if add is None else add[None], **kw)
    return r if kw.get("o_split") else r[0]


def _rms_fwd(x, g, *, name, tb=512):
    t, d = x.shape
    tb = min(tb, t)

    def body(x_ref, g_ref, o_ref):
        xv = x_ref[...]
        r = lax.rsqrt(jnp.mean(xv * xv, axis=-1, keepdims=True) + EPS)
        o_ref[...] = (xv * r * g_ref[...]).astype(o_ref.dtype)

    return pl.pallas_call(
        body, grid=(t // tb,),
        in_specs=[pl.BlockSpec((tb, d), lambda i: (i, 0)), pl.BlockSpec((1, d), lambda i: (0, 0))],
        out_specs=pl.BlockSpec((tb, d), lambda i: (i, 0)),
        out_shape=jax.ShapeDtypeStruct((t, d), MXU_DTYPE), compiler_params=_cp("parallel"), name=name)(x, g)


def _rms_bwd(x, g, dh, dres, *, name, tb=512):
    t, d = x.shape
    tb = min(tb, t)
    has_res = dres is not None

    def body(*refs):
        if has_res:
            x_ref, g_ref, dh_ref, dres_ref, dx_ref, dxl_ref, dg_ref = refs
        else:
            x_ref, g_ref, dh_ref, dx_ref, dxl_ref, dg_ref = refs
        xv = x_ref[...]
        r = lax.rsqrt(jnp.mean(xv * xv, axis=-1, keepdims=True) + EPS)
        xh = xv * r
        dhv = dh_ref[...].astype(F32)

        @pl.when(pl.program_id(0) == 0)
        def _():
            dg_ref[...] = jnp.zeros_like(dg_ref)

        dg_ref[...] += jnp.sum(dhv * xh, axis=0, keepdims=True)
        dxh = dhv * g_ref[...]
        dx = r * (dxh - xh * jnp.mean(dxh * xh, axis=-1, keepdims=True))
        if has_res:
            dx = dx + dres_ref[...]
        dx_ref[...] = dx
        dxl_ref[...] = dx.astype(dxl_ref.dtype)

    row = pl.BlockSpec((tb, d), lambda i: (i, 0))
    vec = pl.BlockSpec((1, d), lambda i: (0, 0))
    operands = [x, g, dh] + ([dres] if has_res else [])
    return pl.pallas_call(
        body, grid=(t // tb,), in_specs=[row, vec, row] + ([row] if has_res else []),
        out_specs=[row, row, vec],
        out_shape=[jax.ShapeDtypeStruct((t, d), F32), jax.ShapeDtypeStruct((t, d), MXU_DTYPE),
                   jax.ShapeDtypeStruct((1, d), F32)],
        compiler_params=_cp("arbitrary"), name=name)(*operands)


def _loss_head(x, g, target, *, name, tb=512):
    t, d = x.shape
    tb = min(tb, t)

    def body(x_ref, g_ref, t_ref, loss_ref, dx_ref, dxl_ref, dg_ref):
        xv = x_ref[...]
        r = lax.rsqrt(jnp.mean(xv * xv, axis=-1, keepdims=True) + EPS)
        xh = xv * r
        gv = g_ref[...]
        err = xh * gv - t_ref[...]

        @pl.when(pl.program_id(0) == 0)
        def _():
            dg_ref[...] = jnp.zeros_like(dg_ref)
            loss_ref[...] = jnp.zeros_like(loss_ref)

        row_loss = jnp.mean(err * err, axis=-1, keepdims=True)
        loss_ref[...] += 0.5 * jnp.sum(row_loss, axis=0, keepdims=True)
        dy = err * (1.0 / d)
        dg_ref[...] += jnp.sum(dy * xh, axis=0, keepdims=True)
        dxh = dy * gv
        dx = r * (dxh - xh * jnp.mean(dxh * xh, axis=-1, keepdims=True))
        dx_ref[...] = dx
        dxl_ref[...] = dx.astype(dxl_ref.dtype)

    row = pl.BlockSpec((tb, d), lambda i: (i, 0))
    vec = pl.BlockSpec((1, d), lambda i: (0, 0))
    return pl.pallas_call(
        body, grid=(t // tb,), in_specs=[row, vec, row],
        out_specs=[pl.BlockSpec((1, 1), lambda i: (0, 0)), row, row, vec],
        out_shape=[jax.ShapeDtypeStruct((1, 1), F32), jax.ShapeDtypeStruct((t, d), F32),
                   jax.ShapeDtypeStruct((t, d), MXU_DTYPE), jax.ShapeDtypeStruct((1, d), F32)],
        compiler_params=_cp("arbitrary"), name=name)(x, g, target)


def _ffn_up(h, wg, wu, *, name, tb=1024):
    t, d = h.shape
    s, f, _ = wg.shape
    tb = min(tb, t)

    def body(h_ref, wg_ref, wu_ref, g_ref, u_ref, a_ref):
        hv = h_ref[...]
        gv = _dot(hv, wg_ref[0], _NT)
        uv = _dot(hv, wu_ref[0], _NT)
        g_ref[0] = gv.astype(g_ref.dtype)
        u_ref[0] = uv.astype(u_ref.dtype)
        a_ref[0] = (gv * _sigmoid(gv) * uv).astype(a_ref.dtype)

    wspec = pl.BlockSpec((1, f, d), lambda j, i: (j, 0, 0))
    ospec = pl.BlockSpec((1, tb, f), lambda j, i: (j, i, 0))
    return pl.pallas_call(
        body, grid=(s, t // tb),
        in_specs=[pl.BlockSpec((tb, d), lambda j, i: (i, 0)), wspec, wspec],
        out_specs=[ospec, ospec, ospec],
        out_shape=[jax.ShapeDtypeStruct((s, t, f), MXU_DTYPE)] * 3,
        compiler_params=_cp("parallel", "parallel"), name=name)(h, wg, wu)


def _ffn_dact(dy, wd, gate, up, *, name, tb=1024):
    t, d = dy.shape
    s, f, _ = wd.shape
    tb = min(tb, t)

    def body(dy_ref, wd_ref, g_ref, u_ref, dg_ref, du_ref):
        da = _dot(dy_ref[...], wd_ref[0], _NT)
        gv = g_ref[0].astype(F32)
        sg = _sigmoid(gv)
        du_ref[0] = (da * gv * sg).astype(du_ref.dtype)
        dg_ref[0] = (da * u_ref[0].astype(F32) * (sg * (1.0 + gv * (1.0 - sg)))).astype(dg_ref.dtype)

    aspec = pl.BlockSpec((1, tb, f), lambda j, i: (j, i, 0))
    return pl.pallas_call(
        body, grid=(s, t // tb),
        in_specs=[pl.BlockSpec((tb, d), lambda j, i: (i, 0)),
                  pl.BlockSpec((1, f, d), lambda j, i: (j, 0, 0)), aspec, aspec],
        out_specs=[aspec, aspec],
        out_shape=[jax.ShapeDtypeStruct((s, t, f), MXU_DTYPE), jax.ShapeDtypeStruct((s, t, f), MXU_DTYPE)],
        compiler_params=_cp("parallel", "parallel"), name=name)(dy, wd, gate, up)


def _ffn_down(act, wd, x, *, name, tm=512, tn=1024):
    s, t, f = act.shape
    d = wd.shape[2]
    tm, tn = _tile(t, tm), _tile(d, tn)

    def body(a_ref, w_ref, x_ref, o_ref):
        r = x_ref[...]
        for j in range(s):
            r = r + _dot(a_ref[j], w_ref[j], _NN)
        o_ref[...] = r

    xspec = pl.BlockSpec((tm, tn), lambda k, i: (i, k))
    return pl.pallas_call(
        body, grid=(d // tn, t // tm),
        in_specs=[pl.BlockSpec((s, tm, f), lambda k, i: (0, i, 0)), pl.BlockSpec((s, f, tn), lambda k, i: (0, 0, k)),
                  xspec],
        out_specs=xspec, out_shape=jax.ShapeDtypeStruct((t, d), F32),
        compiler_params=_cp("parallel", "parallel"), name=name)(act, wd, x)


def _ffn_dh(dgate, dup, wg, wu, dep, *, name, tm=512, tn=1024, sg=4):
    s, t, f = dgate.shape
    d = wg.shape[2]
    tm, tn = _tile(t, tm), _tile(d, tn)
    steps = s // sg
    has_dep = dep is not None

    def body(*refs):
        dg_ref, du_ref, wg_ref, wu_ref = refs[:4]
        o_ref, acc_ref = refs[-2:]
        j = pl.program_id(2)
        part = _dot(dg_ref[0], wg_ref[0], _NN) + _dot(du_ref[0], wu_ref[0], _NN)
        for q in range(1, sg):
            part = part + _dot(dg_ref[q], wg_ref[q], _NN) + _dot(du_ref[q], wu_ref[q], _NN)

        @pl.when(j == 0)
        def _():
            acc_ref[...] = part

        @pl.when(j > 0)
        def _():
            acc_ref[...] += part

        @pl.when(j == steps - 1)
        def _():
            o_ref[...] = acc_ref[...]

    aspec = pl.BlockSpec((sg, tm, f), lambda i, k, j: (j, i, 0))
    wspec = pl.BlockSpec((sg, f, tn), lambda i, k, j: (j, 0, k))
    return pl.pallas_call(
        body, grid=(t // tm, d // tn, steps),
        in_specs=[aspec, aspec, wspec, wspec] + ([pl.BlockSpec(memory_space=pl.ANY)] if has_dep else []),
        out_specs=pl.BlockSpec((tm, tn), lambda i, k, j: (i, k)),
        out_shape=jax.ShapeDtypeStruct((t, d), F32), scratch_shapes=[pltpu.VMEM((tm, tn), F32)],
        compiler_params=_cp("parallel", "parallel", "arbitrary"),
        name=name)(dgate, dup, wg, wu, *([dep] if has_dep else []))


def _xattn_fwd(q, kv, *, name, tb=512):
    t, d = q.shape
    m = kv.shape[0]
    hd = d // XA_HEADS
    tb = min(tb, t)
    scale = hd ** -0.5

    def body(q_ref, kv_ref, o_ref):
        for hh in range(XA_HEADS):
            cs = slice(hh * hd, (hh + 1) * hd)
            s = _dot(q_ref[:, cs], kv_ref[:, cs], _NT) * scale
            s = s - jnp.max(s, axis=-1, keepdims=True)
            e = jnp.exp(s)
            p = e / jnp.sum(e, axis=-1, keepdims=True)
            o_ref[:, cs] = _dot(p, kv_ref[:, d + hh * hd:d + (hh + 1) * hd], _NN).astype(o_ref.dtype)

    return pl.pallas_call(
        body, grid=(t // tb,),
        in_specs=[pl.BlockSpec((tb, d), lambda i: (i, 0)), pl.BlockSpec((m, 2 * d), lambda i: (0, 0))],
        out_specs=pl.BlockSpec((tb, d), lambda i: (i, 0)),
        out_shape=jax.ShapeDtypeStruct((t, d), MXU_DTYPE), compiler_params=_cp("parallel"), name=name)(q, kv)


def _xattn_bwd(q, kv, do, *, name, tb=512):
    t, d = q.shape
    m = kv.shape[0]
    hd = d // XA_HEADS
    tb = min(tb, t)
    scale = hd ** -0.5

    def body(q_ref, kv_ref, do_ref, dq_ref, dkv_ref):
        @pl.when(pl.program_id(0) == 0)
        def _():
            dkv_ref[...] = jnp.zeros_like(dkv_ref)

        for hh in range(XA_HEADS):
            cs = slice(hh * hd, (hh + 1) * hd)
            vs = slice(d + hh * hd, d + (hh + 1) * hd)
            qv, kk, vv, dov = q_ref[:, cs], kv_ref[:, cs], kv_ref[:, vs], do_ref[:, cs]
            s = _dot(qv, kk, _NT) * scale
            s = s - jnp.max(s, axis=-1, keepdims=True)
            e = jnp.exp(s)
            p = e / jnp.sum(e, axis=-1, keepdims=True)
            dkv_ref[:, vs] += _dot(p, dov, _TN)
            dp = _dot(dov, vv, _NT)
            ds = p * (dp - jnp.sum(p * dp, axis=-1, keepdims=True)) * scale
            dq_ref[:, cs] = _dot(ds, kk, _NN).astype(dq_ref.dtype)
            dkv_ref[:, cs] += _dot(ds, qv, _TN)

    row = pl.BlockSpec((tb, d), lambda i: (i, 0))
    full = pl.BlockSpec((m, 2 * d), lambda i: (0, 0))
    return pl.pallas_call(
        body, grid=(t // tb,), in_specs=[row, full, row], out_specs=[row, full],
        out_shape=[jax.ShapeDtypeStruct((t, d), MXU_DTYPE), jax.ShapeDtypeStruct((m, 2 * d), F32)],
        compiler_params=_cp("arbitrary"), name=name)(q, kv, do)


def _pool_window_stats(u, gi, reverse):
    t = u.shape[0]
    row = lax.broadcasted_iota(jnp.int32, u.shape, 0)
    s = u
    for j in range(POOL_GROUPS):
        sh = 1 << j
        if reverse:
            rolled = jnp.where(row < t - sh, pltpu.roll(s, t - sh, axis=0), 0.0)
        else:
            rolled = jnp.where(row >= sh, pltpu.roll(s, sh, axis=0), 0.0)
        s = jnp.where(j <= gi, s + rolled, s)
    return s, row


def _pool_fwd(z, w_pool, scale, *, name):
    t = z.shape[0]
    g_cnt, c, _ = w_pool.shape

    def body(z_ref, w_ref, s_ref, o_ref):
        gi = pl.program_id(0)
        u = z_ref[...]
        win, row = _pool_window_stats(u, gi, False)
        cnt = jnp.minimum(row + 1, lax.shift_left(jnp.int32(2), gi)).astype(F32)
        p = win / cnt - u
        o_ref[...] = (_dot(p, w_ref[0], _NN) * s_ref[...]).astype(o_ref.dtype)

    return pl.pallas_call(
        body, grid=(g_cnt,),
        in_specs=[pl.BlockSpec((t, c), lambda g: (0, g)), pl.BlockSpec((1, c, c), lambda g: (g, 0, 0)),
                  pl.BlockSpec((1, c), lambda g: (0, g))],
        out_specs=pl.BlockSpec((t, c), lambda g: (0, g)),
        out_shape=jax.ShapeDtypeStruct((t, 2 * g_cnt * c), MXU_DTYPE),
        compiler_params=_cp("parallel"), name=name)(z, w_pool, scale)


def _pool_bwd(z, w_pool, scale, dycat, *, name):
    t = z.shape[0]
    g_cnt, c, _ = w_pool.shape

    def body(z_ref, w_ref, s_ref, dy_ref, du_ref, dw_ref, ds_ref):
        gi = pl.program_id(0)
        u = z_ref[...]
        win, row = _pool_window_stats(u, gi, False)
        cnt = jnp.minimum(row + 1, lax.shift_left(jnp.int32(2), gi)).astype(F32)
        p = win / cnt - u
        y = _dot(p, w_ref[0], _NN)
        dya = dy_ref[...].astype(F32)
        ds_ref[...] = jnp.sum(dya * y, axis=0, keepdims=True)
        dy = dya * s_ref[...]
        dw_ref[0] = _dot(p, dy, _TN)
        dp = _dot(dy, w_ref[0], _NT)
        back, _ = _pool_window_stats(dp / cnt, gi, True)
        du_ref[...] = (back - dp).astype(du_ref.dtype)

    col = pl.BlockSpec((t, c), lambda g: (0, g))
    return pl.pallas_call(
        body, grid=(g_cnt,),
        in_specs=[col, pl.BlockSpec((1, c, c), lambda g: (g, 0, 0)), pl.BlockSpec((1, c), lambda g: (0, g)), col],
        out_specs=[col, pl.BlockSpec((1, c, c), lambda g: (g, 0, 0)), pl.BlockSpec((1, c), lambda g: (0, g))],
        out_shape=[jax.ShapeDtypeStruct((t, g_cnt * c), MXU_DTYPE), jax.ShapeDtypeStruct((g_cnt, c, c), F32),
                   jax.ShapeDtypeStruct((1, g_cnt * c), F32)],
        compiler_params=_cp("parallel"), name=name)(z, w_pool, scale, dycat)


def _chunk_tri(lower):
    r = lax.broadcasted_iota(jnp.int32, (LANES, LANES), 0)
    c = lax.broadcasted_iota(jnp.int32, (LANES, LANES), 1)
    same = (r // HG_CHUNK) == (c // HG_CHUNK)
    return jnp.where(same & ((c <= r) if lower else (c >= r)), 1.0, 0.0).astype(F32)


def _hgrn_prepare(q_ref, f_ref, lb_ref, qh_s, k_s, b_s, qt_s, kt_s, gl_s):
    tb = q_ref.shape[0]
    lb = lb_ref[...]
    sg = _sigmoid(f_ref[...])
    f = lb + (1.0 - lb) * sg
    logf = jnp.log(f)
    qv = q_ref[...]
    qh = qv * _sigmoid(qv) * (HG_HEAD ** -0.5)
    tri = _chunk_tri(True)
    for r in range(tb // LANES):
        rows = slice(r * LANES, (r + 1) * LANES)
        b_s[rows, :] = _dot_f32(tri, logf[rows, :], _NN)
    b = b_s[...]
    b3 = b.reshape(tb // HG_CHUNK, HG_CHUNK, HG_HEAD)
    bl = b3[:, HG_CHUNK - 1:HG_CHUNK, :]
    k = 1.0 - f
    qh_s[...] = qh
    k_s[...] = k
    qt_s[...] = qh * jnp.exp(b)
    kt_s[...] = k * jnp.exp(bl - b3).reshape(tb, HG_HEAD)
    gl_s[...] = jnp.exp(jnp.broadcast_to(bl, b3.shape)).reshape(tb, HG_HEAD)
    return sg, f


def _hgrn_intra(qh, kk, bq, rows_a, rows_b):
    ones = jnp.ones((HG_HEAD, HG_HEAD), MXU_DTYPE)
    es, stack_a, stack_b = [], [], []
    for s in range(HG_CHUNK):
        e = jnp.exp(jnp.minimum(bq - bq[s:s + 1, :], 0.0))
        es.append(e)
        stack_a.append(qh * e * kk[s:s + 1, :])
        if rows_a is not None:
            stack_b.append(rows_a * rows_b[s:s + 1, :])
    a_rep = _dot(jnp.concatenate(stack_a, axis=0), ones, _NN)
    d_rep = _dot(jnp.concatenate(stack_b, axis=0), ones, _NN) if rows_a is not None else None
    return es, a_rep, d_rep


def _hgrn_fwd(z, lb, hg_norm, ycat, *, name, tb=512):
    t = z.shape[0]
    mix_b = lb.shape[1]
    heads = mix_b // HG_HEAD
    off = (z.shape[1] - 4 * mix_b) // HG_HEAD
    tb = min(tb, t)
    ncb = tb // HG_CHUNK

    def body(q_ref, f_ref, i_ref, g_ref, lb_ref, hn_ref, ycat_in, y_ref, o_ref, st_ref,
             state, qh_s, k_s, b_s, qt_s, kt_s, gl_s, o_s):
        del ycat_in

        @pl.when(pl.program_id(1) == 0)
        def _():
            state[...] = jnp.zeros_like(state)

        _hgrn_prepare(q_ref, f_ref, lb_ref, qh_s, k_s, b_s, qt_s, kt_s, gl_s)
        row = lax.broadcasted_iota(jnp.int32, (HG_CHUNK, HG_HEAD), 0)

        def chunk(c, carry):
            rows = pl.ds(pl.multiple_of(c * HG_CHUNK, HG_CHUNK), HG_CHUNK)
            st = state[...]
            st_ref[0, c] = st
            vv = i_ref[rows, :]
            o = _dot(qt_s[rows, :], st, _NT)
            _, a_rep, _ = _hgrn_intra(qh_s[rows, :], k_s[rows, :], b_s[rows, :], None, None)
            for s in range(HG_CHUNK):
                o = o + jnp.where(row >= s, a_rep[s * HG_CHUNK:(s + 1) * HG_CHUNK, :] * vv[s:s + 1, :], 0.0)
            o_s[rows, :] = o
            state[...] = st * gl_s[rows, :][0:1, :] + _dot(vv, kt_s[rows, :], _TN)
            return carry

        lax.fori_loop(0, ncb, chunk, 0, unroll=2)
        o = o_s[...]
        o_ref[...] = o
        r = lax.rsqrt(jnp.mean(o * o, axis=-1, keepdims=True) + EPS)
        gv = g_ref[...]
        y_ref[...] = (o * r * hn_ref[...] * (gv * _sigmoid(gv))).astype(y_ref.dtype)

    def zcol(kind):
        return pl.BlockSpec((tb, HG_HEAD), lambda h, i: (i, off + kind * heads + h))

    scratch = [pltpu.VMEM((HG_HEAD, HG_HEAD), F32)] + [pltpu.VMEM((tb, HG_HEAD), F32)] * 7
    return pl.pallas_call(
        body, grid=(heads, t // tb),
        in_specs=[zcol(0), zcol(1), zcol(2), zcol(3), pl.BlockSpec((1, HG_HEAD), lambda h, i: (0, h)),
                  pl.BlockSpec((1, HG_HEAD), lambda h, i: (0, 0)), pl.BlockSpec(memory_space=pl.ANY)],
        out_specs=[pl.BlockSpec((tb, HG_HEAD), lambda h, i: (i, heads + h)),
                   pl.BlockSpec((tb, HG_HEAD), lambda h, i: (i, h)),
                   pl.BlockSpec((1, ncb, HG_HEAD, HG_HEAD), lambda h, i: (h, i, 0, 0))],
        out_shape=[jax.ShapeDtypeStruct(ycat.shape, ycat.dtype), jax.ShapeDtypeStruct((t, mix_b), F32),
                   jax.ShapeDtypeStruct((heads, t // HG_CHUNK, HG_HEAD, HG_HEAD), F32)],
        scratch_shapes=scratch, input_output_aliases={6: 0},
        compiler_params=_cp("parallel", "arbitrary"), name=name)(z, z, z, z, lb, hg_norm, ycat)


def _hgrn_bwd(z, lb, hg_norm, o_raw, states, dycat, *, name, tb=512):
    t = z.shape[0]
    mix_b = lb.shape[1]
    heads = mix_b // HG_HEAD
    off = (z.shape[1] - 4 * mix_b) // HG_HEAD
    tb = min(tb, t)
    ncb = tb // HG_CHUNK
    nt = t // tb

    def body(q_ref, f_ref, i_ref, g_ref, lb_ref, hn_ref, o_ref, st_ref, dy_ref,
             dq_ref, dfl_ref, di_ref, dg_ref, dlb_ref, dhn_ref,
             dstate, qh_s, k_s, b_s, qt_s, kt_s, gl_s, do_s, dqh_s, dk_s, db_s):
        first = pl.program_id(1) == 0

        @pl.when(first)
        def _():
            dstate[...] = jnp.zeros_like(dstate)
            dlb_ref[...] = jnp.zeros_like(dlb_ref)

        @pl.when(first & (pl.program_id(0) == 0))
        def _():
            dhn_ref[...] = jnp.zeros_like(dhn_ref)

        sg, f = _hgrn_prepare(q_ref, f_ref, lb_ref, qh_s, k_s, b_s, qt_s, kt_s, gl_s)
        o = o_ref[...]
        r = lax.rsqrt(jnp.mean(o * o, axis=-1, keepdims=True) + EPS)
        oh = o * r
        gv = g_ref[...]
        sgg = _sigmoid(gv)
        dy = dy_ref[...].astype(F32)
        hn = hn_ref[...]
        dg_ref[...] = (dy * oh * hn * (sgg * (1.0 + gv * (1.0 - sgg)))).astype(dg_ref.dtype)
        don = dy * (gv * sgg)
        dhn_ref[...] += jnp.sum(don * oh, axis=0, keepdims=True)
        doh = don * hn
        do_s[...] = r * (doh - oh * jnp.mean(doh * oh, axis=-1, keepdims=True))
        row = lax.broadcasted_iota(jnp.int32, (HG_CHUNK, HG_HEAD), 0)

        def chunk(ci, carry):
            c = ncb - 1 - ci
            rows = pl.ds(pl.multiple_of(c * HG_CHUNK, HG_CHUNK), HG_CHUNK)
            st_prev = st_ref[0, c]
            dst = dstate[...]
            qh, kk, bq, vv = qh_s[rows, :], k_s[rows, :], b_s[rows, :], i_ref[rows, :]
            qt, kt, doo = qt_s[rows, :], kt_s[rows, :], do_s[rows, :]
            gl = gl_s[rows, :][0:1, :]
            es, a_rep, d_rep = _hgrn_intra(qh, kk, bq, doo, vv)
            dqh = jnp.exp(bq) * _dot(doo, st_prev, _NN)
            dk = jnp.exp(bq[HG_CHUNK - 1:HG_CHUNK, :] - bq) * _dot(vv, dst, _NN)
            dv = _dot(kt, dst, _NT)
            for s in range(HG_CHUNK):
                blk = slice(s * HG_CHUNK, (s + 1) * HG_CHUNK)
                wgt = jnp.where(row >= s, d_rep[blk, :] * es[s], 0.0)
                dqh = dqh + wgt * kk[s:s + 1, :]
                dk = dk + jnp.where(row == s, jnp.sum(wgt * qh, axis=0, keepdims=True), 0.0)
                dv_row = jnp.sum(jnp.where(row >= s, a_rep[blk, :] * doo, 0.0), axis=0, keepdims=True)
                dv = dv + jnp.where(row == s, dv_row, 0.0)
            st_next = st_prev * gl + _dot(vv, kt, _TN)
            db = qh * dqh - kk * dk
            db = db + jnp.where(row == HG_CHUNK - 1, jnp.sum(st_next * dst, axis=0, keepdims=True), 0.0)
            dstate[...] = dst * gl + _dot(doo, qt, _TN)
            dqh_s[rows, :] = dqh
            dk_s[rows, :] = dk
            db_s[rows, :] = db
            di_ref[rows, :] = dv.astype(di_ref.dtype)
            return carry

        lax.fori_loop(0, ncb, chunk, 0, unroll=2)
        tri = _chunk_tri(False)
        lb_v = lb_ref[...]
        qv = q_ref[...]
        sgq = _sigmoid(qv)
        dq_ref[...] = (dqh_s[...] * (HG_HEAD ** -0.5) * (sgq * (1.0 + qv * (1.0 - sgq)))).astype(dq_ref.dtype)
        dlb = jnp.zeros((1, HG_HEAD), F32)
        for rr in range(tb // LANES):
            rws = slice(rr * LANES, (rr + 1) * LANES)
            dlogf = _dot_f32(tri, db_s[rws, :], _NN)
            df = dlogf / f[rws, :] - dk_s[rws, :]
            sgr = sg[rws, :]
            dfl_ref[rws, :] = (df * (1.0 - lb_v) * sgr * (1.0 - sgr)).astype(dfl_ref.dtype)
            dlb = dlb + jnp.sum(df * (1.0 - sgr), axis=0, keepdims=True)
        dlb_ref[...] += dlb

    def zcol(kind):
        return pl.BlockSpec((tb, HG_HEAD), lambda h, i: (nt - 1 - i, off + kind * heads + h))

    hcol = pl.BlockSpec((tb, HG_HEAD), lambda h, i: (nt - 1 - i, h))
    scratch = [pltpu.VMEM((HG_HEAD, HG_HEAD), F32)] + [pltpu.VMEM((tb, HG_HEAD), F32)] * 10
    out = jax.ShapeDtypeStruct((t, mix_b), MXU_DTYPE)
    return pl.pallas_call(
        body, grid=(heads, nt),
        in_specs=[zcol(0), zcol(1), zcol(2), zcol(3), pl.BlockSpec((1, HG_HEAD), lambda h, i: (0, h)),
                  pl.BlockSpec((1, HG_HEAD), lambda h, i: (0, 0)), hcol,
                  pl.BlockSpec((1, ncb, HG_HEAD, HG_HEAD), lambda h, i: (h, nt - 1 - i, 0, 0)),
                  pl.BlockSpec((tb, HG_HEAD), lambda h, i: (nt - 1 - i, heads + h))],
        out_specs=[hcol, hcol, hcol, hcol, pl.BlockSpec((1, HG_HEAD), lambda h, i: (0, h)),
                   pl.BlockSpec((1, HG_HEAD), lambda h, i: (0, 0))],
        out_shape=[out, out, out, out, jax.ShapeDtypeStruct((1, mix_b), F32),
                   jax.ShapeDtypeStruct((1, HG_HEAD), F32)],
        scratch_shapes=scratch, compiler_params=_cp("arbitrary", "arbitrary"),
        name=name)(z, z, z, z, lb, hg_norm, o_raw, states, dycat)


def _lb_fwd(lb_table, layer, *, name):
    rows, width = lb_table.shape

    def body(t_ref, o_ref):
        tv = t_ref[...]
        e = jnp.exp(tv - jnp.max(tv, axis=0, keepdims=True))
        sm = e / jnp.sum(e, axis=0, keepdims=True)
        o_ref[...] = jnp.sum(sm[1:layer + 2, :], axis=0, keepdims=True)

    return pl.pallas_call(body, out_shape=jax.ShapeDtypeStruct((1, width), F32), name=name)(lb_table)


def _lb_bwd(lb_table, dlb, layer, *, name):
    rows, width = lb_table.shape

    def body(t_ref, d_ref, o_ref):
        tv = t_ref[...]
        e = jnp.exp(tv - jnp.max(tv, axis=0, keepdims=True))
        sm = e / jnp.sum(e, axis=0, keepdims=True)
        ridx = lax.broadcasted_iota(jnp.int32, sm.shape, 0)
        dsm = jnp.where((ridx >= 1) & (ridx <= layer + 1), d_ref[...], 0.0)
        o_ref[...] = sm * (dsm - jnp.sum(sm * dsm, axis=0, keepdims=True))

    return pl.pallas_call(body, out_shape=jax.ShapeDtypeStruct((rows, width), F32), name=name)(lb_table, dlb)


FOX_BLOCK = 1024


def _fox_prep(zf, b_f, *, name, blk=256):
    t = zf.shape[0]

    def body(z_ref, b_ref, fc_ref):
        r = lax.broadcasted_iota(jnp.int32, (blk, blk), 0)
        c = lax.broadcasted_iota(jnp.int32, (blk, blk), 1)
        tri = jnp.where(c <= r, 1.0, 0.0).astype(F32)
        carry = jnp.zeros((1, LANES), F32)
        for j in range(t // blk):
            rows = slice(j * blk, (j + 1) * blk)
            ls = jax.nn.log_sigmoid(z_ref[rows, :] + b_ref[...])
            fb = _dot_f32(tri, ls, _NN) + carry
            carry = fb[blk - 1:blk, :]
            fc_ref[rows, :] = fb

    return pl.pallas_call(
        body, out_shape=jax.ShapeDtypeStruct((t, LANES), F32),
        compiler_params=pltpu.CompilerParams(vmem_limit_bytes=VMEM_LIMIT_BYTES), name=name)(zf, b_f)


def _fox_head_column(fc_ref, fk_s, head):
    lane = lax.broadcasted_iota(jnp.int32, fc_ref.shape, 1)
    fk_s[...] = jnp.sum(jnp.where(lane == head, fc_ref[...], 0.0), axis=1, keepdims=True)


def _fox_scores(k_blk, q_blk, fk_blk, diagonal):
    s = _dot(k_blk, q_blk, _NT) * (FOX_HEAD ** -0.5) - fk_blk
    if diagonal:
        key = lax.broadcasted_iota(jnp.int32, s.shape, 0)
        qry = lax.broadcasted_iota(jnp.int32, s.shape, 1)
        s = jnp.where(key <= qry, s, -jnp.inf)
    return s


def _fox_fwd(zqkv, fcol, *, name):
    t = zqkv.shape[0]
    d = zqkv.shape[1] // 3
    heads = d // FOX_HEAD
    blk = min(FOX_BLOCK, t)
    nq = t // blk

    def body(q_ref, k_ref, v_ref, fc_ref, o_ref, lse_ref, fk_s):
        _fox_head_column(fc_ref, fk_s, pl.program_id(0))

        def q_block(qi, carry):
            qrows = pl.ds(pl.multiple_of(qi * blk, blk), blk)
            q_blk = q_ref[qrows, :]

            def update(st, krows, diagonal):
                m, l, acc = st
                s = _fox_scores(k_ref[krows, :], q_blk, fk_s[krows, :], diagonal)
                m_new = jnp.maximum(m, jnp.max(s, axis=0, keepdims=True))
                alpha = jnp.exp(m - m_new)
                p = jnp.exp(s - m_new)
                l = alpha * l + jnp.sum(p, axis=0, keepdims=True)
                acc = acc * alpha + _dot(v_ref[krows, :], p, _TN)
                return m_new, l, acc

            def k_block(kj, st):
                return update(st, pl.ds(pl.multiple_of(kj * blk, blk), blk), False)

            init = (jnp.full((1, blk), -jnp.inf, F32), jnp.zeros((1, blk), F32),
                    jnp.zeros((FOX_HEAD, blk), F32))
            m, l, acc = update(lax.fori_loop(0, qi, k_block, init), qrows, True)
            o_ref[qrows, :] = (acc / l).T.astype(o_ref.dtype)
            lse_ref[0, :, qrows] = m + jnp.log(l)
            return carry

        lax.fori_loop(0, nq, q_block, 0)

    def col(kind):
        return pl.BlockSpec((t, FOX_HEAD), lambda h: (0, kind * heads + h))

    rowvec = pl.BlockSpec((1, 1, t), lambda h: (h, 0, 0))
    return pl.pallas_call(
        body, grid=(heads,),
        in_specs=[col(0), col(1), col(2), pl.BlockSpec((t, LANES), lambda h: (0, 0))],
        out_specs=[pl.BlockSpec((t, FOX_HEAD), lambda h: (0, h)), rowvec],
        out_shape=[jax.ShapeDtypeStruct((t, d), MXU_DTYPE), jax.ShapeDtypeStruct((heads, 1, t), F32)],
        scratch_shapes=[pltpu.VMEM((t, 1), F32)],
        compiler_params=_cp("parallel"), name=name)(zqkv, zqkv, zqkv, fcol)


def _fox_bwd(zqkv, fcol, lse, o, do, *, name):
    t = zqkv.shape[0]
    d = zqkv.shape[1] // 3
    heads = d // FOX_HEAD
    blk = min(FOX_BLOCK, t)
    nq = t // blk
    scale = FOX_HEAD ** -0.5

    def body(q_ref, k_ref, v_ref, fc_ref, lse_ref, o_ref, do_ref,
             dq_ref, dk_ref, dv_ref, rq_ref, rk_ref, dq_s, drow_s, fk_s, acc_s):
        _fox_head_column(fc_ref, fk_s, pl.program_id(0))
        ones = jnp.ones((blk, FOX_HEAD), MXU_DTYPE)
        dq_s[...] = jnp.zeros_like(dq_s)
        rq_ref[...] = jnp.zeros_like(rq_ref)
        ones_f = jnp.ones((8, FOX_HEAD), F32)
        for j in range(nq):
            rows = slice(j * blk, (j + 1) * blk)
            prod = do_ref[rows, :].astype(F32) * o_ref[rows, :].astype(F32)
            drow_s[:, rows] = _dot_f32(ones_f, prod, _NT)

        def k_block(kj, carry):
            krows = pl.ds(pl.multiple_of(kj * blk, blk), blk)
            k_blk, v_blk, fk_blk = k_ref[krows, :], v_ref[krows, :], fk_s[krows, :]

            def pair(qrows, diagonal):
                q_blk, do_blk = q_ref[qrows, :], do_ref[qrows, :]
                s = _fox_scores(k_blk, q_blk, fk_blk, diagonal)
                p = jnp.exp(s - lse_ref[0, :, qrows])
                acc_s[1] += _dot(p, do_blk, _NN)
                dp = _dot(v_blk, do_blk, _NT)
                ds = (p * (dp - drow_s[0:1, qrows])).astype(MXU_DTYPE)
                acc_s[0] += _dot(ds, q_blk, _NN)
                dq_s[qrows, :] += _dot(ds, k_blk, _TN)
                rq_ref[0, :, qrows] += jnp.sum(ds.astype(F32), axis=0, keepdims=True)
                acc_s[2] += _dot(ds, ones, _NN)

            def q_block(qi, carry2):
                pair(pl.ds(pl.multiple_of(qi * blk, blk), blk), False)
                return carry2

            acc_s[...] = jnp.zeros_like(acc_s)
            pair(krows, True)
            lax.fori_loop(kj + 1, nq, q_block, 0)
            dk_ref[krows, :] = (acc_s[0] * scale).astype(dk_ref.dtype)
            dv_ref[krows, :] = acc_s[1].astype(dv_ref.dtype)
            rk_ref[0, :, krows] = acc_s[2].T[0:1, :]
            return carry

        lax.fori_loop(0, nq, k_block, 0)
        dq_ref[...] = (dq_s[...] * scale).astype(dq_ref.dtype)

    def col(kind):
        return pl.BlockSpec((t, FOX_HEAD), lambda h: (0, kind * heads + h))

    hcol = pl.BlockSpec((t, FOX_HEAD), lambda h: (0, h))
    rowvec = pl.BlockSpec((1, 1, t), lambda h: (h, 0, 0))
    out = jax.ShapeDtypeStruct((t, d), MXU_DTYPE)
    vec = jax.ShapeDtypeStruct((heads, 1, t), F32)
    return pl.pallas_call(
        body, grid=(heads,),
        in_specs=[col(0), col(1), col(2), pl.BlockSpec((t, LANES), lambda h: (0, 0)), rowvec, hcol, hcol],
        out_specs=[hcol, hcol, hcol, rowvec, rowvec],
        out_shape=[out, out, out, vec, vec],
        scratch_shapes=[pltpu.VMEM((t, FOX_HEAD), F32), pltpu.VMEM((8, t), F32), pltpu.VMEM((t, 1), F32),
                        pltpu.VMEM((3, blk, FOX_HEAD), F32)],
        compiler_params=_cp("parallel"), name=name)(zqkv, zqkv, zqkv, fcol, lse, o, do)


def _fox_gate_bwd(rq, rk, zf, b_f, *, name, blk=256):
    heads, t = rq.shape

    def body(rq_ref, rk_ref, z_ref, b_ref, dfl_ref, db_ref):
        r = lax.broadcasted_iota(jnp.int32, (blk, blk), 0)
        c = lax.broadcasted_iota(jnp.int32, (blk, blk), 1)
        tri = jnp.where(r >= c, 1.0, 0.0).astype(F32)
        carry = jnp.zeros((heads, 1), F32)
        db = jnp.zeros((1, LANES), F32)
        pad = jnp.zeros((LANES - heads, blk), F32)
        for j in reversed(range(t // blk)):
            cols = slice(j * blk, (j + 1) * blk)
            df = rq_ref[:, cols] - rk_ref[:, cols]
            dls = _dot_f32(df, tri, _NN) + carry
            carry = dls[:, 0:1]
            dls_t = jnp.concatenate([dls, pad], axis=0).T
            dfl = dls_t * _sigmoid(-(z_ref[cols, :] + b_ref[...]))
            dfl_ref[cols, :] = dfl.astype(dfl_ref.dtype)
            db = db + jnp.sum(dfl, axis=0, keepdims=True)
        db_ref[...] = db

    return pl.pallas_call(
        body, out_shape=[jax.ShapeDtypeStruct((t, LANES), MXU_DTYPE), jax.ShapeDtypeStruct((1, LANES), F32)],
        compiler_params=pltpu.CompilerParams(vmem_limit_bytes=VMEM_LIMIT_BYTES), name=name)(rq, rk, zf, b_f)


def _adamw(w, m, v, parts, *, name, layer=None, prev=None, tr=128):
    lcnt, r, c = w.shape
    p = parts.shape[0]
    li = 0 if layer is None else layer
    tr = _tile(r, tr, 16)
    tc = c if tr < r or r * c <= 128 * 2048 else _tile(c, 256)
    has_prev = prev is not None

    def body(*refs):
        w_ref, m_ref, v_ref, p_ref = refs[:4]
        g_ref, d_ref, nm_ref, nv_ref = refs[-4:]
        g = p_ref[0].astype(F32)
        for j in range(1, p):
            g = g + p_ref[j].astype(F32)
        wv = w_ref[0]
        mn = ADAM_B1 * m_ref[0] + (1.0 - ADAM_B1) * g
        vn = ADAM_B2 * v_ref[0] + (1.0 - ADAM_B2) * (g * g)
        m_hat = mn / (1.0 - ADAM_B1 ** ADAM_STEP)
        v_hat = vn / (1.0 - ADAM_B2 ** ADAM_STEP)
        g_ref[0] = g
        d_ref[0] = -ADAM_LR * (m_hat / (jnp.sqrt(v_hat) + ADAM_EPS) + ADAM_WD * wv)
        nm_ref[0] = mn
        nv_ref[0] = vn

    slab = pl.BlockSpec((1, tr, tc), lambda i, j: (li, i, j))
    in_specs = [slab, slab, slab, pl.BlockSpec((p, tr, tc), lambda i, j: (0, i, j))]
    operands = [w, m, v, parts]
    aliases = {}
    if has_prev:
        in_specs += [pl.BlockSpec(memory_space=pl.ANY)] * 4
        operands += list(prev)
        aliases = {4: 0, 5: 1, 6: 2, 7: 3}
    shp = jax.ShapeDtypeStruct((lcnt, r, c), F32)
    return pl.pallas_call(
        body, grid=(r // tr, c // tc), in_specs=in_specs, out_specs=[slab] * 4, out_shape=[shp] * 4,
        input_output_aliases=aliases, compiler_params=_cp("parallel", "parallel"), name=name)(*operands)


def _my_place():
    return lax.axis_index("x"), lax.axis_index("y"), lax.axis_index("c")


def _slot(p):
    return 4 * p[0] + 2 * p[1] + p[2]


def _peer(me, mask):
    x, y, c = me
    return (1 - x if mask & 4 else x, 1 - y if mask & 2 else y, 1 - c if mask & 1 else c)


_HBM = pl.BlockSpec(memory_space=pltpu.HBM)
_SEM = pl.BlockSpec(memory_space=pltpu.SEMAPHORE)
_ANY = pl.BlockSpec(memory_space=pl.ANY)
_EFFECT = pltpu.SideEffectType.DATAFLOW_SIDE_EFFECTING


def _push_copy(src_refs, land_refs, send_sems, recv_sems, a, mask, me, per_peer, outgoing):
    peer = _peer(me, mask)
    src = src_refs[a].at[_slot(peer)] if per_peer else src_refs[a]
    dst = land_refs[a].at[_slot(me) if outgoing else _slot(peer)]
    k = a * (N_DEV - 1) + mask - 1
    return pltpu.make_async_remote_copy(
        src_ref=src, dst_ref=dst, send_sem=send_sems.at[k], recv_sem=recv_sems.at[k],
        device_id=peer, device_id_type=MESH)


ALL_PEERS = tuple(range(1, N_DEV))
CHIP_PEERS = (2, 4, 6)
FIRST_HOP = (1,) + CHIP_PEERS


def _push_start(srcs, dep, *, per_peer, name, masks=ALL_PEERS):
    n = len(srcs)
    mine = _slot(_my_place())
    lands = []
    for s in srcs:
        own = lax.dynamic_index_in_dim(s, mine, 0, keepdims=True) if per_peer else s[None]
        shape = s.shape if per_peer else (N_DEV,) + s.shape
        lands.append(lax.dynamic_update_slice_in_dim(lax.empty(shape, s.dtype), own, mine, 0))
    has_dep = dep is not None

    def body(*refs):
        src_refs, land_refs = refs[:n], refs[n:2 * n]
        send_sems, recv_sems = refs[2 * n + has_dep], refs[2 * n + has_dep + 1]
        token = refs[-1]
        me = _my_place()
        for a in range(n):
            for mask in masks:
                _push_copy(src_refs, land_refs, send_sems, recv_sems, a, mask, me, per_peer, True).start()
        token[...] = jnp.zeros_like(token)

    hbm_in = [pltpu.with_memory_space_constraint(v, pltpu.HBM) for v in list(srcs) + lands]
    out = pl.pallas_call(
        body, name=name,
        out_shape=(pltpu.SemaphoreType.DMA((n * (N_DEV - 1),)), pltpu.SemaphoreType.DMA((n * (N_DEV - 1),)),
                   *[pltpu.HBM(v.shape, v.dtype) for v in hbm_in], jax.ShapeDtypeStruct((8, LANES), F32)),
        in_specs=[_HBM] * (2 * n) + ([_ANY] if has_dep else []),
        out_specs=(_SEM, _SEM, *[_HBM] * (2 * n), pl.BlockSpec(memory_space=pltpu.VMEM)),
        input_output_aliases={i: 2 + i for i in range(2 * n)},
        compiler_params=pltpu.CompilerParams(has_side_effects=_EFFECT),
    )(*hbm_in, *([dep] if has_dep else []))
    return (n, per_peer, masks, out[:-1]), out[-1]


def _push_wait(handle, after, *, name):
    n, per_peer, masks, (send_sems, recv_sems, *bufs) = handle

    def body(*refs):
        src_refs, land_refs = refs[:n], refs[n:2 * n]
        send_sems, recv_sems = refs[2 * n], refs[2 * n + 1]
        me = _my_place()
        for a in range(n):
            for mask in masks:
                cp = _push_copy(src_refs, land_refs, send_sems, recv_sems, a, mask, me, per_peer, False)
                cp.wait_send()
                cp.wait_recv()

    out = pl.pallas_call(
        body, name=name, out_shape=tuple(pltpu.HBM(v.shape, v.dtype) for v in bufs),
        in_specs=[_HBM] * (2 * n) + [_SEM, _SEM, _ANY], out_specs=tuple([_HBM] * (2 * n)),
        input_output_aliases={i: i for i in range(2 * n)},
        compiler_params=pltpu.CompilerParams(has_side_effects=_EFFECT),
    )(*bufs, send_sems, recv_sems, after)
    return list(out[n:])


def _relay_copy(land_refs, send_sems, recv_sems, a, j, me, outgoing):
    sibling = _peer(me, 1)
    out_slot = _slot(_peer(me, CHIP_PEERS[j]))
    in_slot = _slot(_peer(sibling, CHIP_PEERS[j]))
    k = a * len(CHIP_PEERS) + j
    return pltpu.make_async_remote_copy(
        src_ref=land_refs[a].at[out_slot], dst_ref=land_refs[a].at[out_slot if outgoing else in_slot],
        send_sem=send_sems.at[k], recv_sem=recv_sems.at[k], device_id=sibling, device_id_type=MESH)


def _relay_start(lands, *, name):
    n = len(lands)

    def body(*refs):
        land_refs, send_sems, recv_sems, token = refs[:n], refs[n], refs[n + 1], refs[-1]
        me = _my_place()
        for a in range(n):
            for j in range(len(CHIP_PEERS)):
                _relay_copy(land_refs, send_sems, recv_sems, a, j, me, True).start()
        token[...] = jnp.zeros_like(token)

    hbm_in = [pltpu.with_memory_space_constraint(v, pltpu.HBM) for v in lands]
    n_sem = n * len(CHIP_PEERS)
    out = pl.pallas_call(
        body, name=name,
        out_shape=(pltpu.SemaphoreType.DMA((n_sem,)), pltpu.SemaphoreType.DMA((n_sem,)),
                   *[pltpu.HBM(v.shape, v.dtype) for v in hbm_in], jax.ShapeDtypeStruct((8, LANES), F32)),
        in_specs=[_HBM] * n, out_specs=(_SEM, _SEM, *[_HBM] * n, pl.BlockSpec(memory_space=pltpu.VMEM)),
        input_output_aliases={i: 2 + i for i in range(n)},
        compiler_params=pltpu.CompilerParams(has_side_effects=_EFFECT),
    )(*hbm_in)
    return (n, out[:-1]), out[-1]


def _relay_wait(handle, after, *, name):
    n, (send_sems, recv_sems, *bufs) = handle

    def body(*refs):
        land_refs, send_sems, recv_sems = refs[:n], refs[n], refs[n + 1]
        me = _my_place()
        for a in range(n):
            for j in range(len(CHIP_PEERS)):
                cp = _relay_copy(land_refs, send_sems, recv_sems, a, j, me, False)
                cp.wait_send()
                cp.wait_recv()

    out = pl.pallas_call(
        body, name=name, out_shape=tuple(pltpu.HBM(v.shape, v.dtype) for v in bufs),
        in_specs=[_HBM] * n + [_SEM, _SEM, _ANY], out_specs=tuple([_HBM] * n),
        input_output_aliases={i: i for i in range(n)},
        compiler_params=pltpu.CompilerParams(has_side_effects=_EFFECT),
    )(*bufs, send_sems, recv_sems, after)
    return list(out)


def _all_reduce_rows(v, *, name):
    r, c = v.shape

    def body(v_ref, o_ref, buf, send_sems, recv_sems):
        me = _my_place()
        mine = _slot(me)
        sends = []
        for mask in range(1, N_DEV):
            peer = _peer(me, mask)
            sends.append(pltpu.make_async_remote_copy(
                src_ref=v_ref, dst_ref=buf.at[mine], send_sem=send_sems.at[mask - 1],
                recv_sem=recv_sems.at[mask - 1], device_id=peer, device_id_type=MESH))
        for cp in sends:
            cp.start()
        buf[mine] = v_ref[...]
        for mask in range(1, N_DEV):
            peer = _peer(me, mask)
            pltpu.make_async_remote_copy(
                src_ref=v_ref, dst_ref=buf.at[_slot(peer)], send_sem=send_sems.at[mask - 1],
                recv_sem=recv_sems.at[mask - 1], device_id=peer, device_id_type=MESH).wait_recv()
        for cp in sends:
            cp.wait_send()
        total = buf[0]
        for j in range(1, N_DEV):
            total = total + buf[j]
        o_ref[...] = total

    vm = pl.BlockSpec(memory_space=pltpu.VMEM)
    return pl.pallas_call(
        body, in_specs=[vm], out_specs=vm, out_shape=jax.ShapeDtypeStruct((r, c), F32),
        scratch_shapes=[pltpu.VMEM((N_DEV, r, c), F32), pltpu.SemaphoreType.DMA((7,)),
                        pltpu.SemaphoreType.DMA((7,))],
        name=name)(v)


def _xa_fwd(x, mem, g_x, g_m, wq, wkv, wo, tag):
    hx = _rms_fwd(x, g_x, name=f"xa{tag}_norm")
    memn = _rms_fwd(mem, g_m, name=f"xa{tag}_mem_norm")
    q = _mm2(hx, wq, "nn", MXU_DTYPE, name=f"xa{tag}_q")
    kv = _mm2(memn, wkv, "nn", MXU_DTYPE, name=f"xa{tag}_kv", b_split=True)
    o = _xattn_fwd(q, kv, name=f"xa{tag}_attn")
    return _mm2(o, wo, "nn", F32, name=f"xa{tag}_out", add=x), (hx, memn, q, kv, o)


def _xa_bwd(x, mem, g_x, g_m, wq, wkv, wo, saved, dxo, dxo_lo, tag, put):
    hx, memn, q, kv, o = saved
    do = _mm2(dxo_lo, wo, "nt", MXU_DTYPE, name=f"xa{tag}_do")
    dwo = _mm2(o, dxo_lo, "tn", MXU_DTYPE, name=f"xa{tag}_dwo")
    dq, dkv = _xattn_bwd(q, kv, do, name=f"xa{tag}_attn_bwd")
    dwq = _mm2(hx, dq, "tn", MXU_DTYPE, name=f"xa{tag}_dwq")
    dwkv = _mm2(memn, dkv, "tn", MXU_DTYPE, name=f"xa{tag}_dwkv", o_split=True, tn=wkv.shape[2])
    tok = put((dwq, dwkv, dwo))
    dhx = _mm2(dq, wq, "nt", F32, name=f"xa{tag}_dh", dep=tok)
    dx, dx_lo, dgx = _rms_bwd(x, g_x, dhx, dxo, name=f"xa{tag}_norm_bwd")
    dmemn = _mm2(dkv, wkv, "nt", F32, name=f"xa{tag}_dmem", b_split=True)
    _, _, dgm = _rms_bwd(mem, g_m, dmemn, None, name=f"xa{tag}_mem_norm_bwd")
    return dx, dx_lo, dgx, dgm


def _ffn_fwd(x, g, wg, wu, wd, tag):
    h = _rms_fwd(x, g, name=f"ffn{tag}_norm")
    gate, up, act = _ffn_up(h, wg, wu, name=f"ffn{tag}_up")
    return _ffn_down(act, wd, x, name=f"ffn{tag}_down"), (h, gate, up, act)


def _ffn_bwd(x, g, wg, wu, wd, saved, dxo, dxo_lo, tag, put):
    h, gate, up, act = saved
    dwd = _mm(act, dxo_lo[None], "tn", MXU_DTYPE, name=f"ffn{tag}_dwd")
    dgate, dup = _ffn_dact(dxo_lo, wd, gate, up, name=f"ffn{tag}_dact")
    dwg = _mm(dgate, h[None], "tn", MXU_DTYPE, name=f"ffn{tag}_dwg")
    dwu = _mm(dup, h[None], "tn", MXU_DTYPE, name=f"ffn{tag}_dwu")
    tok = put((dwg, dwu, dwd))
    dh = _ffn_dh(dgate, dup, wg, wu, tok, name=f"ffn{tag}_dh")
    dx, dx_lo, dg = _rms_bwd(x, g, dh, dxo, name=f"ffn{tag}_norm_bwd")
    return dx, dx_lo, dg


def _even_fwd(x, g, lb, w_in, w_pool, pool_scale, hg_norm, w_out):
    h = _rms_fwd(x, g, name="ev_norm")
    z = _mm2(h, w_in, "nn", F32, name="ev_in", b_split=True)
    ycat = _pool_fwd(z, w_pool, pool_scale, name="ev_pool")
    ycat, o_raw, states = _hgrn_fwd(z, lb, hg_norm, ycat, name="ev_hgrn")
    return _mm2(ycat, w_out, "nn", F32, name="ev_out", add=x), (h, z, ycat, o_raw, states)


def _even_bwd(x, g, lb, w_in, w_pool, pool_scale, hg_norm, w_out, saved, dxo, dxo_lo, put):
    h, z, ycat, o_raw, states = saved
    dycat = _mm2(dxo_lo, w_out, "nt", MXU_DTYPE, name="ev_dy")
    dw_out = _mm2(ycat, dxo_lo, "tn", MXU_DTYPE, name="ev_dw_out")
    du, dw_pool, dscale = _pool_bwd(z, w_pool, pool_scale, dycat, name="ev_pool_bwd")
    dq, dfl, di, dg, dlb, dhn = _hgrn_bwd(z, lb, hg_norm, o_raw, states, dycat, name="ev_hgrn_bwd")
    dz = jnp.concatenate([du, dq, dfl, di, dg], axis=1)
    dw_in = _mm2(h, dz, "tn", MXU_DTYPE, name="ev_dw_in", o_split=True, tn=w_in.shape[2])
    tok = put((dw_in, dw_pool, dw_out))
    dh = _mm_nt_split(dz, w_in, tok, name="ev_dh")
    dx, dx_lo, dgn = _rms_bwd(x, g, dh, dxo, name="ev_norm_bwd")
    return dx, dx_lo, dict(ev_norm=dgn, ev_pool_scale=dscale, ev_hg_norm=dhn, lb=dlb)


def _odd_fwd(x, g, w_qkv, w_f, b_f, w_out):
    n_qkv = 3 * x.shape[1]
    h = _rms_fwd(x, g, name="od_norm")
    zqkv = _mm2(h, w_qkv, "nt", MXU_DTYPE, name="od_qkv", n_b=n_qkv)
    zf = _mm2(h, w_f, "nt", F32, name="od_gate")
    fcol = _fox_prep(zf, b_f, name="od_fox_prep")
    o, lse = _fox_fwd(zqkv, fcol, name="od_fox")
    return _mm2(o, w_out, "nn", F32, name="od_out", add=x), (h, zqkv, zf, fcol, o, lse)


def _odd_bwd(x, g, w_qkv, w_f, b_f, w_out, saved, dxo, dxo_lo, put):
    h, zqkv, zf, fcol, o, lse = saved
    do = _mm2(dxo_lo, w_out, "nt", MXU_DTYPE, name="od_do")
    dw_out = _mm2(o, dxo_lo, "tn", MXU_DTYPE, name="od_dw_out")
    dq, dk, dv, rq, rk = _fox_bwd(zqkv, fcol, lse, o, do, name="od_fox_bwd")
    dfl, db_f = _fox_gate_bwd(rq[:, 0, :], rk[:, 0, :], zf, b_f, name="od_fox_gate_bwd")
    dz = jnp.concatenate([dq, dk, dv], axis=1)
    dw_qkv = _mm2(dz, h, "tn", MXU_DTYPE, name="od_dw_qkv")
    dw_f = _mm2(dfl, h, "tn", MXU_DTYPE, name="od_dw_gate")
    tok = put((dw_qkv, dw_f, dw_out))
    dh = _mm2(dz, w_qkv, "nn", F32, name="od_dh_qkv", dep=tok)
    dh = _mm2(dfl, w_f, "nn", F32, name="od_dh_gate", add=dh)
    dx, dx_lo, dgn = _rms_bwd(x, g, dh, dxo, name="od_norm_bwd")
    return dx, dx_lo, dict(od_norm=dgn, od_b_f=db_f)


def _local_step(x, mem, target, sp, get_w, put_dw):
    b_f = jnp.pad(sp["od_b_f"], ((0, 0), (0, LANES - sp["od_b_f"].shape[1])))
    lb = _lb_fwd(sp["lb_table"], 0, name="lb_fwd")
    fin = sp["final_norm"].reshape(1, -1)
    xn, xm, fn = sp["xa_norm"], sp["xa_mem_norm"], sp["ffn_norm"]
    w_ev = get_w("ev", None)
    x1, s_ev = _even_fwd(x, sp["ev_norm"], lb, w_ev[0], w_ev[1], sp["ev_pool_scale"], sp["ev_hg_norm"], w_ev[2])
    w_xa0 = get_w("xa0", x1)
    x2, s_xa0 = _xa_fwd(x1, mem, xn[0:1], xm[0:1], *w_xa0, 0)
    w_ff0 = get_w("ffn0", x2)
    x3, s_ff0 = _ffn_fwd(x2, fn[0:1], *w_ff0, 0)
    w_qkv, w_f, w_od_out, od_norm = get_w("od", x3)
    x4, s_od = _odd_fwd(x3, od_norm, w_qkv, w_f, b_f, w_od_out)
    w_xa1 = get_w("xa1", x4)
    x5, s_xa1 = _xa_fwd(x4, mem, xn[1:2], xm[1:2], *w_xa1, 1)
    w_ff1 = get_w("ffn1", x5)
    x6, s_ff1 = _ffn_fwd(x5, fn[1:2], *w_ff1, 1)
    loss, dx, dx_lo, d_fin = _loss_head(x6, fin, target, name="loss_head")
    put = lambda grp: functools.partial(put_dw, grp)
    dx, dx_lo, d_ffn1 = _ffn_bwd(x5, fn[1:2], *w_ff1, s_ff1, dx, dx_lo, 1, put("ffn1"))
    dx, dx_lo, d_xa1, d_xm1 = _xa_bwd(x4, mem, xn[1:2], xm[1:2], *w_xa1, s_xa1, dx, dx_lo, 1, put("xa1"))
    dx, dx_lo, d_od = _odd_bwd(x3, od_norm, w_qkv, w_f, b_f, w_od_out, s_od, dx, dx_lo, put("od"))
    dx, dx_lo, d_ffn0 = _ffn_bwd(x2, fn[0:1], *w_ff0, s_ff0, dx, dx_lo, 0, put("ffn0"))
    dx, dx_lo, d_xa0, d_xm0 = _xa_bwd(x1, mem, xn[0:1], xm[0:1], *w_xa0, s_xa0, dx, dx_lo, 0, put("xa0"))
    dx, _, d_ev = _even_bwd(x, sp["ev_norm"], lb, w_ev[0], w_ev[1], sp["ev_pool_scale"], sp["ev_hg_norm"],
                            w_ev[2], s_ev, dx, dx_lo, put("ev"))
    small = dict(
        lb_table=_lb_bwd(sp["lb_table"], d_ev["lb"], 0, name="lb_bwd"),
        ev_norm=d_ev["ev_norm"], ev_pool_scale=d_ev["ev_pool_scale"], ev_hg_norm=d_ev["ev_hg_norm"],
        od_norm=d_od["od_norm"], od_b_f=d_od["od_b_f"][:, :sp["od_b_f"].shape[1]],
        xa_norm=jnp.concatenate([d_xa0, d_xa1], axis=0), xa_mem_norm=jnp.concatenate([d_xm0, d_xm1], axis=0),
        ffn_norm=jnp.concatenate([d_ffn0, d_ffn1], axis=0), final_norm=d_fin.reshape(-1))
    return loss, dx, small


_SMALL = ("lb_table", "ev_norm", "ev_pool_scale", "ev_hg_norm", "od_norm", "od_b_f", "xa_norm", "xa_mem_norm",
          "ffn_norm", "final_norm")
_WEIGHTS = ("lb_table", "ev_norm", "ev_w_in", "ev_w_pool", "ev_pool_scale", "ev_hg_norm", "ev_w_out", "od_norm",
            "od_w_in", "od_b_f", "od_w_out", "xa_norm", "xa_mem_norm", "xa_wq", "xa_wkv", "xa_wo", "ffn_norm",
            "ffn_w_gate", "ffn_w_up", "ffn_w_down", "final_norm")


def _lo(a):
    return a.astype(MXU_DTYPE)


def _rows(v):
    flat = v.reshape(-1)
    return jnp.pad(flat, (0, (-flat.shape[0]) % LANES)).reshape(-1, LANES)


def kernel(x, mem, lb_table, ev_norm, ev_w_in, ev_w_pool, ev_pool_scale, ev_hg_norm, ev_w_out, od_norm, od_w_in, od_b_f, od_w_out, xa_norm, xa_mem_norm, xa_wq, xa_wkv, xa_wo, ffn_norm, ffn_w_gate, ffn_w_up, ffn_w_down, final_norm, loss_target, m_lb_table, m_ev_norm, m_ev_w_in, m_ev_w_pool, m_ev_pool_scale, m_ev_hg_norm, m_ev_w_out, m_od_norm, m_od_w_in, m_od_b_f, m_od_w_out, m_xa_norm, m_xa_mem_norm, m_xa_wq, m_xa_wkv, m_xa_wo, m_ffn_norm, m_ffn_w_gate, m_ffn_w_up, m_ffn_w_down, m_final_norm, v_lb_table, v_ev_norm, v_ev_w_in, v_ev_w_pool, v_ev_pool_scale, v_ev_hg_norm, v_ev_w_out, v_od_norm, v_od_w_in, v_od_b_f, v_od_w_out, v_xa_norm, v_xa_mem_norm, v_xa_wq, v_xa_wkv, v_xa_wo, v_ffn_norm, v_ffn_w_gate, v_ffn_w_up, v_ffn_w_down, v_final_norm):
    arg = dict(locals())
    d = x.shape[-1]
    layers = xa_wq.shape[0]
    me = _slot(_my_place())

    n_gate = od_b_f.shape[1]
    turned = {k: jnp.swapaxes(arg[k], 1, 2) for k in ("od_w_in", "ffn_w_gate", "ffn_w_up")}
    raw = dict(ev=[ev_w_in[0], ev_w_pool[0], ev_w_out[0]], od=[turned["od_w_in"][0], od_w_out[0], od_norm])
    for l in range(layers):
        raw[f"xa{l}"] = [xa_wq[l], xa_wkv[l], xa_wo[l]]
        raw[f"ffn{l}"] = [turned["ffn_w_gate"][l], turned["ffn_w_up"][l], ffn_w_down[l]]
    order = ("ev", "xa0", "ffn0", "od", "xa1", "ffn1")
    gathers, relays, tok = {}, {}, None
    for grp in order:
        srcs = [w if tok is None else w + tok[0, 0] for w in raw[grp]]
        srcs = [w if grp == "od" and j == 2 else _lo(w) for j, w in enumerate(srcs)]
        gathers[grp], tok = _push_start(srcs, None, per_peer=False, masks=FIRST_HOP, name=f"gather_{grp}_start")
    last_start = tok

    def second_hop(grp, after):
        lands = _push_wait(gathers[grp], after, name=f"gather_{grp}_wait")
        relays[grp], token = _relay_start(lands, name=f"gather_{grp}_relay")
        return token

    def get_w(grp, after):
        i = order.index(grp)
        after = last_start if after is None else after
        if grp not in relays:
            after = second_hop(grp, after)
        if 1 <= i < len(order) - 1:
            after = second_hop(order[i + 1], after)
        got = _relay_wait(relays[grp], after, name=f"gather_{grp}_relay_wait")
        if grp == "ev":
            w_in, w_pool, w_out = got
            w_pool = jnp.transpose(w_pool, (1, 0, 2, 3)).reshape(w_pool.shape[1], -1, w_pool.shape[3])
            return w_in, w_pool, w_out.reshape(d, d)
        if grp == "od":
            w_in, w_out, nrm = got
            w_in = w_in.reshape(-1, d)
            w_f = jnp.pad(w_in[w_in.shape[0] - n_gate:], ((0, LANES - n_gate), (0, 0)))
            return w_in, w_f, w_out.reshape(d, d), nrm.reshape(1, d)
        if grp.startswith("xa"):
            return got[0].reshape(d, d), got[1], got[2].reshape(d, d)
        return tuple(got)

    def row_parts(g):
        return g.reshape(N_DEV, -1, g.shape[-1])

    scatters = {}

    def put_dw(grp, dws):
        if grp == "ev":
            dw_in, dw_pool, dw_out = dws
            gc = dw_pool.shape[1] // N_DEV
            dw_pool = _lo(jnp.transpose(dw_pool.reshape(dw_pool.shape[0], N_DEV, gc, -1), (1, 0, 2, 3)))
            parts = [dw_in, dw_pool, row_parts(dw_out)]
        elif grp == "od":
            dw_qkv, dw_f, dw_out = dws
            parts = [row_parts(jnp.concatenate([dw_qkv, dw_f[:n_gate]], axis=0)), row_parts(dw_out)]
        elif grp.startswith("xa"):
            parts = [row_parts(dws[0]), dws[1], row_parts(dws[2])]
        else:
            parts = list(dws)
        scatters[grp], token = _push_start(parts, None, per_peer=True, name=f"scatter_{grp}_start")
        return token

    sp = {k: arg[k] for k in _SMALL if k != "od_norm"}
    loss, dx, small = _local_step(x[0], mem[0], loss_target[0], sp, get_w, put_dw)

    pieces = [_rows(small[k]) for k in _SMALL]
    packed = jnp.concatenate(pieces + [_rows(loss)], axis=0)
    packed = jnp.pad(packed, ((0, (-packed.shape[0]) % 8), (0, 0)))
    total = _all_reduce_rows(packed, name="all_reduce_small")
    loss = total[sum(pc.shape[0] for pc in pieces), 0]
    small_g, at = {}, 0
    for k, pc in zip(_SMALL, pieces):
        n = small[k].size
        small_g[k] = total[at:at + pc.shape[0]].reshape(-1)[:n].reshape(small[k].shape)
        at += pc.shape[0]
    small_g["od_norm"] = lax.dynamic_slice_in_dim(small_g["od_norm"], me * od_norm.shape[1], od_norm.shape[1], axis=1)

    res = {}
    for k in _SMALL:
        w, m, v = arg[k], arg["m_" + k], arg["v_" + k]
        shp = (1, 1, w.shape[0]) if w.ndim == 1 else (1,) + w.shape
        out = _adamw(w.reshape(shp), m.reshape(shp), v.reshape(shp), small_g[k].reshape(shp), name=f"adamw_{k}")
        res[k] = [o.reshape(w.shape) for o in out]
    members = dict(ev=("ev_w_in", "ev_w_pool", "ev_w_out"), od=("od_w_in", "od_w_out"),
                   xa=("xa_wq", "xa_wkv", "xa_wo"), ffn=("ffn_w_gate", "ffn_w_up", "ffn_w_down"))
    after, stacked = dx, {}
    for grp in ("ffn1", "xa1", "od", "ffn0", "xa0", "ev"):
        got = _push_wait(scatters[grp], after, name=f"scatter_{grp}_wait")
        kind = grp.rstrip("01")
        for k, parts in zip(members[kind], got):
            w, m, v = [jnp.swapaxes(a, 1, 2) if k in turned else a for a in (arg[k], arg["m_" + k], arg["v_" + k])]
            if w.shape[0] == 1:
                shp = (1, -1, w.shape[-1])
                out = _adamw(w.reshape(shp), m.reshape(shp), v.reshape(shp), parts.reshape(N_DEV, -1, w.shape[-1]),
                             name=f"adamw_{k}")
            else:
                out = _adamw(w, m, v, parts, name=f"adamw_{k}{grp[-1]}", layer=int(grp[-1]), prev=stacked.get(k))
                stacked[k] = out
            res[k] = [jnp.swapaxes(o.reshape(w.shape), 1, 2) if k in turned else o.reshape(w.shape) for o in out]
            after = out[3][:1, :8, :LANES]

    outs = [loss, dx[None]]
    for j in range(4):
        outs += [res[k][j] for k in _WEIGHTS]
    return tuple(outs)
```

```python
import functools

import jax
import jax.numpy as jnp
from jax import lax
from jax.experimental import pallas as pl
from jax.experimental.pallas import tpu as pltpu

F32 = jnp.float32
MXU_DTYPE = jnp.bfloat16
EPS = 1e-6
N_DEV = 8
V7X_VMEM_BYTES = 64 * 1024 * 1024
VMEM_LIMIT_BYTES = V7X_VMEM_BYTES - 8 * 1024 * 1024
LANES = 128
HIGHEST = lax.Precision.HIGHEST
MESH = pl.DeviceIdType.MESH

HG_HEAD = 128
HG_CHUNK = 32
FOX_HEAD = 128
XA_HEADS = 4
POOL_GROUPS = 4

ADAM_LR = 0.001
ADAM_B1 = 0.9
ADAM_B2 = 0.999
ADAM_EPS = 1e-08
ADAM_WD = 0.01
ADAM_STEP = 10

_NN = ((1,), (0,))
_NT = ((1,), (1,))
_TN = ((0,), (0,))


def _dot(a, b, dims):
    return lax.dot_general(a.astype(MXU_DTYPE), b.astype(MXU_DTYPE), (dims, ((), ())),
                           preferred_element_type=F32)


def _dot_f32(a, b, dims):
    return lax.dot_general(a, b, (dims, ((), ())), preferred_element_type=F32, precision=HIGHEST)


def _cp(*sem):
    return pltpu.CompilerParams(dimension_semantics=sem, vmem_limit_bytes=VMEM_LIMIT_BYTES)


def _tile(n, pref, align=LANES):
    if n <= pref:
        return n
    t = (pref // align) * align
    while t >= align:
        if n % t == 0:
            return t
        t -= align
    return n


def _sigmoid(x):
    return jax.nn.sigmoid(x)


def _mm(a, b, mode, out_dtype, *, name, add=None, dep=None, reduce_b=False, b_split=False, o_split=False,
        n_b=None, tm=1024, tn=1024, tk=2048):
    ba, bb = a.shape[0], b.shape[0]
    if mode == "tn":
        kdim, m = a.shape[1], a.shape[2]
        tk = 2 * tk
    else:
        m, kdim = a.shape[1], a.shape[2]
    if b_split:
        s_cnt, b_rows, w = b.shape
        if mode == "nt":
            n = b_rows
            assert kdim == s_cnt * w
            tk = w
        else:
            n = s_cnt * w
            assert b_rows == kdim
            tn = w
        nb = ba
    else:
        n = b.shape[1] if mode == "nt" else b.shape[2]
        n = n if n_b is None else n_b
        nb = max(ba, bb)
    if not (b_split and mode != "nt"):
        tn = _tile(n, tn)
    if not (b_split and mode == "nt"):
        tk = _tile(kdim, tk)
    tm = _tile(m, tm)
    assert m % tm == 0 and n % tn == 0 and kdim % tk == 0, (name, m, n, kdim, tm, tn, tk)
    nk = kdim // tk
    if reduce_b:
        grid = (m // tm, n // tn, nb, nk)
        unpack = lambda i, j, bi, k: (bi, i, j, k)
        sem = ("parallel", "parallel", "arbitrary", "arbitrary")
        nred = nb * nk
    else:
        grid = (nb, m // tm, n // tn, nk)
        unpack = lambda bi, i, j, k: (bi, i, j, k)
        sem = ("parallel", "parallel", "parallel", "arbitrary")
        nred = nk

    def a_map(*g):
        bi, i, j, k = unpack(*g)
        ab = bi if ba > 1 else 0
        return (ab, k, i) if mode == "tn" else (ab, i, k)

    def b_map(*g):
        bi, i, j, k = unpack(*g)
        if b_split:
            return (k, j, 0) if mode == "nt" else (j, k, 0)
        bq = bi if bb > 1 else 0
        return (bq, j, k) if mode == "nt" else (bq, k, j)

    def o_map(*g):
        bi, i, j, k = unpack(*g)
        if o_split:
            return (j, i, 0)
        return (0 if reduce_b else bi, i, j)

    a_blk = (1, tk, tm) if mode == "tn" else (1, tm, tk)
    b_blk = (1, tn, tk) if mode == "nt" else (1, tk, tn)
    dims = {"nn": _NN, "nt": _NT, "tn": _TN}[mode]
    has_add = add is not None

    def body(*refs):
        a_ref, b_ref = refs[:2]
        if has_add:
            add_ref = refs[2]
        if nred == 1:
            o_ref = refs[-1]
            r = _dot(a_ref[0], b_ref[0], dims)
            if has_add:
                r = r + add_ref[0].astype(F32)
            o_ref[0] = r.astype(o_ref.dtype)
            return
        o_ref, acc_ref = refs[-2:]
        if reduce_b:
            step = pl.program_id(2) * nk + pl.program_id(3)
        else:
            step = pl.program_id(3)

        @pl.when(step == 0)
        def _():
            acc_ref[...] = _dot(a_ref[0], b_ref[0], dims)

        @pl.when(step > 0)
        def _():
            acc_ref[...] += _dot(a_ref[0], b_ref[0], dims)

        @pl.when(step == nred - 1)
        def _():
            r = acc_ref[...]
            if has_add:
                r = r + add_ref[0].astype(F32)
            o_ref[0] = r.astype(o_ref.dtype)

    in_specs = [pl.BlockSpec(a_blk, a_map), pl.BlockSpec(b_blk, b_map)]
    operands = [a, b]
    if has_add:
        in_specs.append(pl.BlockSpec((1, tm, tn), o_map))
        operands.append(add)
    if dep is not None:
        in_specs.append(pl.BlockSpec(memory_space=pl.ANY))
        operands.append(dep)
    if o_split:
        out_shape = jax.ShapeDtypeStruct((n // tn, m, tn), out_dtype)
    else:
        out_shape = jax.ShapeDtypeStruct((1 if reduce_b else nb, m, n), out_dtype)
    return pl.pallas_call(
        body, grid=grid, in_specs=in_specs, out_specs=pl.BlockSpec((1, tm, tn), o_map),
        out_shape=out_shape, scratch_shapes=[] if nred == 1 else [pltpu.VMEM((tm, tn), F32)],
        compiler_params=_cp(*sem), name=name)(*operands)


def _mm_nt_split(a, b, dep, *, name, tm=512, tn=1024):
    m, k = a.shape
    s, n, w = b.shape
    assert k == s * w and w % LANES == 0
    tm, tn = _tile(m, tm), _tile(n, tn)
    has_dep = dep is not None

    def body(*refs):
        a_ref, b_ref, o_ref = refs[0], refs[1], refs[-1]
        r = _dot(a_ref[:, 0:w], b_ref[0], _NT)
        for q in range(1, s):
            r = r + _dot(a_ref[:, q * w:(q + 1) * w], b_ref[q], _NT)
        o_ref[...] = r

    return pl.pallas_call(
        body, grid=(n // tn, m // tm),
        in_specs=[pl.BlockSpec((tm, k), lambda j, i: (i, 0)), pl.BlockSpec((s, tn, w), lambda j, i: (0, j, 0))]
        + ([pl.BlockSpec(memory_space=pl.ANY)] if has_dep else []),
        out_specs=pl.BlockSpec((tm, tn), lambda j, i: (i, j)), out_shape=jax.ShapeDtypeStruct((m, n), F32),
        compiler_params=_cp("parallel", "parallel"), name=name)(a, b, *([dep] if has_dep else []))


def _mm2(a, b, mode, out_dtype, *, name, add=None, **kw):
    b3 = b if kw.get("b_split") else b[None]
    r = _mm(a[None], b3, mode, out_dtype, name=name, add=None if add is None else add[None], **kw)
    return r if kw.get("o_split") else r[0]


def _rms_fwd(x, g, *, name, tb=512):
    t, d = x.shape
    tb = min(tb, t)

    def body(x_ref, g_ref, o_ref):
        xv = x_ref[...]
        r = lax.rsqrt(jnp.mean(xv * xv, axis=-1, keepdims=True) + EPS)
        o_ref[...] = (xv * r * g_ref[...]).astype(o_ref.dtype)

    return pl.pallas_call(
        body, grid=(t // tb,),
        in_specs=[pl.BlockSpec((tb, d), lambda i: (i, 0)), pl.BlockSpec((1, d), lambda i: (0, 0))],
        out_specs=pl.BlockSpec((tb, d), lambda i: (i, 0)),
        out_shape=jax.ShapeDtypeStruct((t, d), MXU_DTYPE), compiler_params=_cp("parallel"), name=name)(x, g)


def _rms_bwd(x, g, dh, dres, *, name, tb=512):
    t, d = x.shape
    tb = min(tb, t)
    has_res = dres is not None

    def body(*refs):
        if has_res:
            x_ref, g_ref, dh_ref, dres_ref, dx_ref, dxl_ref, dg_ref = refs
        else:
            x_ref, g_ref, dh_ref, dx_ref, dxl_ref, dg_ref = refs
        xv = x_ref[...]
        r = lax.rsqrt(jnp.mean(xv * xv, axis=-1, keepdims=True) + EPS)
        xh = xv * r
        dhv = dh_ref[...].astype(F32)

        @pl.when(pl.program_id(0) == 0)
        def _():
            dg_ref[...] = jnp.zeros_like(dg_ref)

        dg_ref[...] += jnp.sum(dhv * xh, axis=0, keepdims=True)
        dxh = dhv * g_ref[...]
        dx = r * (dxh - xh * jnp.mean(dxh * xh, axis=-1, keepdims=True))
        if has_res:
            dx = dx + dres_ref[...]
        dx_ref[...] = dx
        dxl_ref[...] = dx.astype(dxl_ref.dtype)

    row = pl.BlockSpec((tb, d), lambda i: (i, 0))
    vec = pl.BlockSpec((1, d), lambda i: (0, 0))
    operands = [x, g, dh] + ([dres] if has_res else [])
    return pl.pallas_call(
        body, grid=(t // tb,), in_specs=[row, vec, row] + ([row] if has_res else []),
        out_specs=[row, row, vec],
        out_shape=[jax.ShapeDtypeStruct((t, d), F32), jax.ShapeDtypeStruct((t, d), MXU_DTYPE),
                   jax.ShapeDtypeStruct((1, d), F32)],
        compiler_params=_cp("arbitrary"), name=name)(*operands)


def _loss_head(x, g, target, *, name, tb=512):
    t, d = x.shape
    tb = min(tb, t)

    def body(x_ref, g_ref, t_ref, loss_ref, dx_ref, dxl_ref, dg_ref):
        xv = x_ref[...]
        r = lax.rsqrt(jnp.mean(xv * xv, axis=-1, keepdims=True) + EPS)
        xh = xv * r
        gv = g_ref[...]
        err = xh * gv - t_ref[...]

        @pl.when(pl.program_id(0) == 0)
        def _():
            dg_ref[...] = jnp.zeros_like(dg_ref)
            loss_ref[...] = jnp.zeros_like(loss_ref)

        row_loss = jnp.mean(err * err, axis=-1, keepdims=True)
        loss_ref[...] += 0.5 * jnp.sum(row_loss, axis=0, keepdims=True)
        dy = err * (1.0 / d)
        dg_ref[...] += jnp.sum(dy * xh, axis=0, keepdims=True)
        dxh = dy * gv
        dx = r * (dxh - xh * jnp.mean(dxh * xh, axis=-1, keepdims=True))
        dx_ref[...] = dx
        dxl_ref[...] = dx.astype(dxl_ref.dtype)

    row = pl.BlockSpec((tb, d), lambda i: (i, 0))
    vec = pl.BlockSpec((1, d), lambda i: (0, 0))
    return pl.pallas_call(
        body, grid=(t // tb,), in_specs=[row, vec, row],
        out_specs=[pl.BlockSpec((1, 1), lambda i: (0, 0)), row, row, vec],
        out_shape=[jax.ShapeDtypeStruct((1, 1), F32), jax.ShapeDtypeStruct((t, d), F32),
                   jax.ShapeDtypeStruct((t, d), MXU_DTYPE), jax.ShapeDtypeStruct((1, d), F32)],
        compiler_params=_cp("arbitrary"), name=name)(x, g, target)


def _ffn_up(h, wg, wu, *, name, tb=1024):
    t, d = h.shape
    s, f, _ = wg.shape
    tb = min(tb, t)

    def body(h_ref, wg_ref, wu_ref, g_ref, u_ref, a_ref):
        hv = h_ref[...]
        gv = _dot(hv, wg_ref[0], _NT)
        uv = _dot(hv, wu_ref[0], _NT)
        g_ref[0] = gv.astype(g_ref.dtype)
        u_ref[0] = uv.astype(u_ref.dtype)
        a_ref[0] = (gv * _sigmoid(gv) * uv).astype(a_ref.dtype)

    wspec = pl.BlockSpec((1, f, d), lambda j, i: (j, 0, 0))
    ospec = pl.BlockSpec((1, tb, f), lambda j, i: (j, i, 0))
    return pl.pallas_call(
        body, grid=(s, t // tb),
        in_specs=[pl.BlockSpec((tb, d), lambda j, i: (i, 0)), wspec, wspec],
        out_specs=[ospec, ospec, ospec],
        out_shape=[jax.ShapeDtypeStruct((s, t, f), MXU_DTYPE)] * 3,
        compiler_params=_cp("parallel", "parallel"), name=name)(h, wg, wu)


def _ffn_dact(dy, wd, gate, up, *, name, tb=1024):
    t, d = dy.shape
    s, f, _ = wd.shape
    tb = min(tb, t)

    def body(dy_ref, wd_ref, g_ref, u_ref, dg_ref, du_ref):
        da = _dot(dy_ref[...], wd_ref[0], _NT)
        gv = g_ref[0].astype(F32)
        sg = _sigmoid(gv)
        du_ref[0] = (da * gv * sg).astype(du_ref.dtype)
        dg_ref[0] = (da * u_ref[0].astype(F32) * (sg * (1.0 + gv * (1.0 - sg)))).astype(dg_ref.dtype)

    aspec = pl.BlockSpec((1, tb, f), lambda j, i: (j, i, 0))
    return pl.pallas_call(
        body, grid=(s, t // tb),
        in_specs=[pl.BlockSpec((tb, d), lambda j, i: (i, 0)),
                  pl.BlockSpec((1, f, d), lambda j, i: (j, 0, 0)), aspec, aspec],
        out_specs=[aspec, aspec],
        out_shape=[jax.ShapeDtypeStruct((s, t, f), MXU_DTYPE), jax.ShapeDtypeStruct((s, t, f), MXU_DTYPE)],
        compiler_params=_cp("parallel", "parallel"), name=name)(dy, wd, gate, up)


def _ffn_down(act, wd, x, *, name, tm=512, tn=1024):
    s, t, f = act.shape
    d = wd.shape[2]
    tm, tn = _tile(t, tm), _tile(d, tn)

    def body(a_ref, w_ref, x_ref, o_ref):
        r = x_ref[...]
        for j in range(s):
            r = r + _dot(a_ref[j], w_ref[j], _NN)
        o_ref[...] = r

    xspec = pl.BlockSpec((tm, tn), lambda k, i: (i, k))
    return pl.pallas_call(
        body, grid=(d // tn, t // tm),
        in_specs=[pl.BlockSpec((s, tm, f), lambda k, i: (0, i, 0)), pl.BlockSpec((s, f, tn), lambda k, i: (0, 0, k)),
                  xspec],
        out_specs=xspec, out_shape=jax.ShapeDtypeStruct((t, d), F32),
        compiler_params=_cp("parallel", "parallel"), name=name)(act, wd, x)


def _ffn_dh(dgate, dup, wg, wu, dep, *, name, tm=512, tn=1024, sg=4):
    s, t, f = dgate.shape
    d = wg.shape[2]
    tm, tn = _tile(t, tm), _tile(d, tn)
    steps = s // sg
    has_dep = dep is not None

    def body(*refs):
        dg_ref, du_ref, wg_ref, wu_ref = refs[:4]
        o_ref, acc_ref = refs[-2:]
        j = pl.program_id(2)
        part = _dot(dg_ref[0], wg_ref[0], _NN) + _dot(du_ref[0], wu_ref[0], _NN)
        for q in range(1, sg):
            part = part + _dot(dg_ref[q], wg_ref[q], _NN) + _dot(du_ref[q], wu_ref[q], _NN)

        @pl.when(j == 0)
        def _():
            acc_ref[...] = part

        @pl.when(j > 0)
        def _():
            acc_ref[...] += part

        @pl.when(j == steps - 1)
        def _():
            o_ref[...] = acc_ref[...]

    aspec = pl.BlockSpec((sg, tm, f), lambda i, k, j: (j, i, 0))
    wspec = pl.BlockSpec((sg, f, tn), lambda i, k, j: (j, 0, k))
    return pl.pallas_call(
        body, grid=(t // tm, d // tn, steps),
        in_specs=[aspec, aspec, wspec, wspec] + ([pl.BlockSpec(memory_space=pl.ANY)] if has_dep else []),
        out_specs=pl.BlockSpec((tm, tn), lambda i, k, j: (i, k)),
        out_shape=jax.ShapeDtypeStruct((t, d), F32), scratch_shapes=[pltpu.VMEM((tm, tn), F32)],
        compiler_params=_cp("parallel", "parallel", "arbitrary"),
        name=name)(dgate, dup, wg, wu, *([dep] if has_dep else []))


def _xattn_fwd(q, kv, *, name, tb=512):
    t, d = q.shape
    m = kv.shape[0]
    hd = d // XA_HEADS
    tb = min(tb, t)
    scale = hd ** -0.5

    def body(q_ref, kv_ref, o_ref):
        for hh in range(XA_HEADS):
            cs = slice(hh * hd, (hh + 1) * hd)
            s = _dot(q_ref[:, cs], kv_ref[:, cs], _NT) * scale
            s = s - jnp.max(s, axis=-1, keepdims=True)
            e = jnp.exp(s)
            p = e / jnp.sum(e, axis=-1, keepdims=True)
            o_ref[:, cs] = _dot(p, kv_ref[:, d + hh * hd:d + (hh + 1) * hd], _NN).astype(o_ref.dtype)

    return pl.pallas_call(
        body, grid=(t // tb,),
        in_specs=[pl.BlockSpec((tb, d), lambda i: (i, 0)), pl.BlockSpec((m, 2 * d), lambda i: (0, 0))],
        out_specs=pl.BlockSpec((tb, d), lambda i: (i, 0)),
        out_shape=jax.ShapeDtypeStruct((t, d), MXU_DTYPE), compiler_params=_cp("parallel"), name=name)(q, kv)


def _xattn_bwd(q, kv, do, *, name, tb=512):
    t, d = q.shape
    m = kv.shape[0]
    hd = d // XA_HEADS
    tb = min(tb, t)
    scale = hd ** -0.5

    def body(q_ref, kv_ref, do_ref, dq_ref, dkv_ref):
        @pl.when(pl.program_id(0) == 0)
        def _():
            dkv_ref[...] = jnp.zeros_like(dkv_ref)

        for hh in range(XA_HEADS):
            cs = slice(hh * hd, (hh + 1) * hd)
            vs = slice(d + hh * hd, d + (hh + 1) * hd)
            qv, kk, vv, dov = q_ref[:, cs], kv_ref[:, cs], kv_ref[:, vs], do_ref[:, cs]
            s = _dot(qv, kk, _NT) * scale
            s = s - jnp.max(s, axis=-1, keepdims=True)
            e = jnp.exp(s)
            p = e / jnp.sum(e, axis=-1, keepdims=True)
            dkv_ref[:, vs] += _dot(p, dov, _TN)
            dp = _dot(dov, vv, _NT)
            ds = p * (dp - jnp.sum(p * dp, axis=-1, keepdims=True)) * scale
            dq_ref[:, cs] = _dot(ds, kk, _NN).astype(dq_ref.dtype)
            dkv_ref[:, cs] += _dot(ds, qv, _TN)

    row = pl.BlockSpec((tb, d), lambda i: (i, 0))
    full = pl.BlockSpec((m, 2 * d), lambda i: (0, 0))
    return pl.pallas_call(
        body, grid=(t // tb,), in_specs=[row, full, row], out_specs=[row, full],
        out_shape=[jax.ShapeDtypeStruct((t, d), MXU_DTYPE), jax.ShapeDtypeStruct((m, 2 * d), F32)],
        compiler_params=_cp("arbitrary"), name=name)(q, kv, do)


def _pool_window_stats(u, gi, reverse):
    t = u.shape[0]
    row = lax.broadcasted_iota(jnp.int32, u.shape, 0)
    s = u
    for j in range(POOL_GROUPS):
        sh = 1 << j
        if reverse:
            rolled = jnp.where(row < t - sh, pltpu.roll(s, t - sh, axis=0), 0.0)
        else:
            rolled = jnp.where(row >= sh, pltpu.roll(s, sh, axis=0), 0.0)
        s = jnp.where(j <= gi, s + rolled, s)
    return s, row


def _pool_fwd(z, w_pool, scale, *, name):
    t = z.shape[0]
    g_cnt, c, _ = w_pool.shape

    def body(z_ref, w_ref, s_ref, o_ref):
        gi = pl.program_id(0)
        u = z_ref[...]
        win, row = _pool_window_stats(u, gi, False)
        cnt = jnp.minimum(row + 1, lax.shift_left(jnp.int32(2), gi)).astype(F32)
        p = win / cnt - u
        o_ref[...] = (_dot(p, w_ref[0], _NN) * s_ref[...]).astype(o_ref.dtype)

    return pl.pallas_call(
        body, grid=(g_cnt,),
        in_specs=[pl.BlockSpec((t, c), lambda g: (0, g)), pl.BlockSpec((1, c, c), lambda g: (g, 0, 0)),
                  pl.BlockSpec((1, c), lambda g: (0, g))],
        out_specs=pl.BlockSpec((t, c), lambda g: (0, g)),
        out_shape=jax.ShapeDtypeStruct((t, 2 * g_cnt * c), MXU_DTYPE),
        compiler_params=_cp("parallel"), name=name)(z, w_pool, scale)


def _pool_bwd(z, w_pool, scale, dycat, *, name):
    t = z.shape[0]
    g_cnt, c, _ = w_pool.shape

    def body(z_ref, w_ref, s_ref, dy_ref, du_ref, dw_ref, ds_ref):
        gi = pl.program_id(0)
        u = z_ref[...]
        win, row = _pool_window_stats(u, gi, False)
        cnt = jnp.minimum(row + 1, lax.shift_left(jnp.int32(2), gi)).astype(F32)
        p = win / cnt - u
        y = _dot(p, w_ref[0], _NN)
        dya = dy_ref[...].astype(F32)
        ds_ref[...] = jnp.sum(dya * y, axis=0, keepdims=True)
        dy = dya * s_ref[...]
        dw_ref[0] = _dot(p, dy, _TN)
        dp = _dot(dy, w_ref[0], _NT)
        back, _ = _pool_window_stats(dp / cnt, gi, True)
        du_ref[...] = (back - dp).astype(du_ref.dtype)

    col = pl.BlockSpec((t, c), lambda g: (0, g))
    return pl.pallas_call(
        body, grid=(g_cnt,),
        in_specs=[col, pl.BlockSpec((1, c, c), lambda g: (g, 0, 0)), pl.BlockSpec((1, c), lambda g: (0, g)), col],
        out_specs=[col, pl.BlockSpec((1, c, c), lambda g: (g, 0, 0)), pl.BlockSpec((1, c), lambda g: (0, g))],
        out_shape=[jax.ShapeDtypeStruct((t, g_cnt * c), MXU_DTYPE), jax.ShapeDtypeStruct((g_cnt, c, c), F32),
                   jax.ShapeDtypeStruct((1, g_cnt * c), F32)],
        compiler_params=_cp("parallel"), name=name)(z, w_pool, scale, dycat)


def _chunk_tri(lower):
    r = lax.broadcasted_iota(jnp.int32, (LANES, LANES), 0)
    c = lax.broadcasted_iota(jnp.int32, (LANES, LANES), 1)
    same = (r // HG_CHUNK) == (c // HG_CHUNK)
    return jnp.where(same & ((c <= r) if lower else (c >= r)), 1.0, 0.0).astype(F32)


def _hgrn_prepare(q_ref, f_ref, lb_ref, qh_s, k_s, b_s, qt_s, kt_s, gl_s):
    tb = q_ref.shape[0]
    lb = lb_ref[...]
    sg = _sigmoid(f_ref[...])
    f = lb + (1.0 - lb) * sg
    logf = jnp.log(f)
    qv = q_ref[...]
    qh = qv * _sigmoid(qv) * (HG_HEAD ** -0.5)
    tri = _chunk_tri(True)
    for r in range(tb // LANES):
        rows = slice(r * LANES, (r + 1) * LANES)
        b_s[rows, :] = _dot_f32(tri, logf[rows, :], _NN)
    b = b_s[...]
    b3 = b.reshape(tb // HG_CHUNK, HG_CHUNK, HG_HEAD)
    bl = b3[:, HG_CHUNK - 1:HG_CHUNK, :]
    k = 1.0 - f
    qh_s[...] = qh
    k_s[...] = k
    qt_s[...] = qh * jnp.exp(b)
    kt_s[...] = k * jnp.exp(bl - b3).reshape(tb, HG_HEAD)
    gl_s[...] = jnp.exp(jnp.broadcast_to(bl, b3.shape)).reshape(tb, HG_HEAD)
    return sg, f


SUBLANES = 8
HG_GROUPS = HG_CHUNK // SUBLANES


def _hgrn_intra(qh, kk, bq, rows_a, rows_b):
    ones = jnp.ones((HG_HEAD, HG_HEAD), MXU_DTYPE)
    es, stack_a, stack_b, starts, at = [], [], [], [], 0
    for s in range(HG_CHUNK):
        lo = (s // SUBLANES) * SUBLANES
        e = jnp.exp(jnp.minimum(bq[lo:, :] - bq[s:s + 1, :], 0.0))
        es.append(e)
        stack_a.append(qh[lo:, :] * e * kk[s:s + 1, :])
        if rows_a is not None:
            stack_b.append(rows_a[lo:, :] * rows_b[s:s + 1, :])
        starts.append(at)
        at += HG_CHUNK - lo
    a_rep = _dot(jnp.concatenate(stack_a, axis=0), ones, _NN)
    d_rep = _dot(jnp.concatenate(stack_b, axis=0), ones, _NN) if rows_a is not None else None
    return es, a_rep, d_rep, starts


def _groups(v):
    return [v[g * SUBLANES:(g + 1) * SUBLANES, :] for g in range(HG_GROUPS)]


def _hgrn_fwd(z, lb, hg_norm, ycat, *, name, tb=512):
    t = z.shape[0]
    mix_b = lb.shape[1]
    heads = mix_b // HG_HEAD
    off = (z.shape[1] - 4 * mix_b) // HG_HEAD
    tb = min(tb, t)
    ncb = tb // HG_CHUNK

    def body(q_ref, f_ref, i_ref, g_ref, lb_ref, hn_ref, ycat_in, y_ref, o_ref, st_ref,
             state, qh_s, k_s, b_s, qt_s, kt_s, gl_s, o_s):
        del ycat_in

        @pl.when(pl.program_id(1) == 0)
        def _():
            state[...] = jnp.zeros_like(state)

        _hgrn_prepare(q_ref, f_ref, lb_ref, qh_s, k_s, b_s, qt_s, kt_s, gl_s)
        row = lax.broadcasted_iota(jnp.int32, (SUBLANES, HG_HEAD), 0)

        def chunk(c, carry):
            rows = pl.ds(pl.multiple_of(c * HG_CHUNK, HG_CHUNK), HG_CHUNK)
            st = state[...]
            st_ref[0, c] = st
            vv = i_ref[rows, :]
            o = _groups(_dot(qt_s[rows, :], st, _NT))
            _, a_rep, _, starts = _hgrn_intra(qh_s[rows, :], k_s[rows, :], b_s[rows, :], None, None)
            for s in range(HG_CHUNK):
                g0 = s // SUBLANES
                for g in range(g0, HG_GROUPS):
                    at = starts[s] + (g - g0) * SUBLANES
                    piece = a_rep[at:at + SUBLANES, :] * vv[s:s + 1, :]
                    o[g] = o[g] + (jnp.where(row >= s - g0 * SUBLANES, piece, 0.0) if g == g0 else piece)
            o_s[rows, :] = jnp.concatenate(o, axis=0)
            state[...] = st * gl_s[rows, :][0:1, :] + _dot(vv, kt_s[rows, :], _TN)
            return carry

        lax.fori_loop(0, ncb, chunk, 0, unroll=2)
        o = o_s[...]
        o_ref[...] = o
        r = lax.rsqrt(jnp.mean(o * o, axis=-1, keepdims=True) + EPS)
        gv = g_ref[...]
        y_ref[...] = (o * r * hn_ref[...] * (gv * _sigmoid(gv))).astype(y_ref.dtype)

    def zcol(kind):
        return pl.BlockSpec((tb, HG_HEAD), lambda h, i: (i, off + kind * heads + h))

    scratch = [pltpu.VMEM((HG_HEAD, HG_HEAD), F32)] + [pltpu.VMEM((tb, HG_HEAD), F32)] * 7
    return pl.pallas_call(
        body, grid=(heads, t // tb),
        in_specs=[zcol(0), zcol(1), zcol(2), zcol(3), pl.BlockSpec((1, HG_HEAD), lambda h, i: (0, h)),
                  pl.BlockSpec((1, HG_HEAD), lambda h, i: (0, 0)), pl.BlockSpec(memory_space=pl.ANY)],
        out_specs=[pl.BlockSpec((tb, HG_HEAD), lambda h, i: (i, heads + h)),
                   pl.BlockSpec((tb, HG_HEAD), lambda h, i: (i, h)),
                   pl.BlockSpec((1, ncb, HG_HEAD, HG_HEAD), lambda h, i: (h, i, 0, 0))],
        out_shape=[jax.ShapeDtypeStruct(ycat.shape, ycat.dtype), jax.ShapeDtypeStruct((t, mix_b), F32),
                   jax.ShapeDtypeStruct((heads, t // HG_CHUNK, HG_HEAD, HG_HEAD), F32)],
        scratch_shapes=scratch, input_output_aliases={6: 0},
        compiler_params=_cp("parallel", "arbitrary"), name=name)(z, z, z, z, lb, hg_norm, ycat)


def _hgrn_bwd(z, lb, hg_norm, o_raw, states, dycat, *, name, tb=512):
    t = z.shape[0]
    mix_b = lb.shape[1]
    heads = mix_b // HG_HEAD
    off = (z.shape[1] - 4 * mix_b) // HG_HEAD
    tb = min(tb, t)
    ncb = tb // HG_CHUNK
    nt = t // tb

    def body(q_ref, f_ref, i_ref, g_ref, lb_ref, hn_ref, o_ref, st_ref, dy_ref,
             dq_ref, dfl_ref, di_ref, dg_ref, dlb_ref, dhn_ref,
             dstate, qh_s, k_s, b_s, qt_s, kt_s, gl_s, do_s, dqh_s, dk_s, db_s):
        first = pl.program_id(1) == 0

        @pl.when(first)
        def _():
            dstate[...] = jnp.zeros_like(dstate)
            dlb_ref[...] = jnp.zeros_like(dlb_ref)

        @pl.when(first & (pl.program_id(0) == 0))
        def _():
            dhn_ref[...] = jnp.zeros_like(dhn_ref)

        sg, f = _hgrn_prepare(q_ref, f_ref, lb_ref, qh_s, k_s, b_s, qt_s, kt_s, gl_s)
        o = o_ref[...]
        r = lax.rsqrt(jnp.mean(o * o, axis=-1, keepdims=True) + EPS)
        oh = o * r
        gv = g_ref[...]
        sgg = _sigmoid(gv)
        dy = dy_ref[...].astype(F32)
        hn = hn_ref[...]
        dg_ref[...] = (dy * oh * hn * (sgg * (1.0 + gv * (1.0 - sgg)))).astype(dg_ref.dtype)
        don = dy * (gv * sgg)
        dhn_ref[...] += jnp.sum(don * oh, axis=0, keepdims=True)
        doh = don * hn
        do_s[...] = r * (doh - oh * jnp.mean(doh * oh, axis=-1, keepdims=True))
        row = lax.broadcasted_iota(jnp.int32, (SUBLANES, HG_HEAD), 0)

        def chunk(ci, carry):
            c = ncb - 1 - ci
            rows = pl.ds(pl.multiple_of(c * HG_CHUNK, HG_CHUNK), HG_CHUNK)
            st_prev = st_ref[0, c]
            dst = dstate[...]
            qh, kk, bq, vv = qh_s[rows, :], k_s[rows, :], b_s[rows, :], i_ref[rows, :]
            qt, kt, doo = qt_s[rows, :], kt_s[rows, :], do_s[rows, :]
            gl = gl_s[rows, :][0:1, :]
            es, a_rep, d_rep, starts = _hgrn_intra(qh, kk, bq, doo, vv)
            dqh = _groups(jnp.exp(bq) * _dot(doo, st_prev, _NN))
            dk = _groups(jnp.exp(bq[HG_CHUNK - 1:HG_CHUNK, :] - bq) * _dot(vv, dst, _NN))
            dv = _groups(_dot(kt, dst, _NT))
            qh_g, do_g = _groups(qh), _groups(doo)
            for s in range(HG_CHUNK):
                g0 = s // SUBLANES
                local = s - g0 * SUBLANES
                dk_acc = dv_acc = None
                for g in range(g0, HG_GROUPS):
                    at = (g - g0) * SUBLANES
                    wgt = d_rep[starts[s] + at:starts[s] + at + SUBLANES, :] * es[s][at:at + SUBLANES, :]
                    avo = a_rep[starts[s] + at:starts[s] + at + SUBLANES, :] * do_g[g]
                    if g == g0:
                        wgt = jnp.where(row >= local, wgt, 0.0)
                        avo = jnp.where(row >= local, avo, 0.0)
                    dqh[g] = dqh[g] + wgt * kk[s:s + 1, :]
                    dk_acc = wgt * qh_g[g] if dk_acc is None else dk_acc + wgt * qh_g[g]
                    dv_acc = avo if dv_acc is None else dv_acc + avo
                dk[g0] = dk[g0] + jnp.where(row == local, jnp.sum(dk_acc, axis=0, keepdims=True), 0.0)
                dv[g0] = dv[g0] + jnp.where(row == local, jnp.sum(dv_acc, axis=0, keepdims=True), 0.0)
            dqh, dk, dv = [jnp.concatenate(p, axis=0) for p in (dqh, dk, dv)]
            row_c = lax.broadcasted_iota(jnp.int32, (HG_CHUNK, HG_HEAD), 0)
            st_next = st_prev * gl + _dot(vv, kt, _TN)
            db = qh * dqh - kk * dk
            db = db + jnp.where(row_c == HG_CHUNK - 1, jnp.sum(st_next * dst, axis=0, keepdims=True), 0.0)
            dstate[...] = dst * gl + _dot(doo, qt, _TN)
            dqh_s[rows, :] = dqh
            dk_s[rows, :] = dk
            db_s[rows, :] = db
            di_ref[rows, :] = dv.astype(di_ref.dtype)
            return carry

        lax.fori_loop(0, ncb, chunk, 0, unroll=2)
        tri = _chunk_tri(False)
        lb_v = lb_ref[...]
        qv = q_ref[...]
        sgq = _sigmoid(qv)
        dq_ref[...] = (dqh_s[...] * (HG_HEAD ** -0.5) * (sgq * (1.0 + qv * (1.0 - sgq)))).astype(dq_ref.dtype)
        dlb = jnp.zeros((1, HG_HEAD), F32)
        for rr in range(tb // LANES):
            rws = slice(rr * LANES, (rr + 1) * LANES)
            dlogf = _dot_f32(tri, db_s[rws, :], _NN)
            df = dlogf / f[rws, :] - dk_s[rws, :]
            sgr = sg[rws, :]
            dfl_ref[rws, :] = (df * (1.0 - lb_v) * sgr * (1.0 - sgr)).astype(dfl_ref.dtype)
            dlb = dlb + jnp.sum(df * (1.0 - sgr), axis=0, keepdims=True)
        dlb_ref[...] += dlb

    def zcol(kind):
        return pl.BlockSpec((tb, HG_HEAD), lambda h, i: (nt - 1 - i, off + kind * heads + h))

    hcol = pl.BlockSpec((tb, HG_HEAD), lambda h, i: (nt - 1 - i, h))
    scratch = [pltpu.VMEM((HG_HEAD, HG_HEAD), F32)] + [pltpu.VMEM((tb, HG_HEAD), F32)] * 10
    out = jax.ShapeDtypeStruct((t, mix_b), MXU_DTYPE)
    return pl.pallas_call(
        body, grid=(heads, nt),
        in_specs=[zcol(0), zcol(1), zcol(2), zcol(3), pl.BlockSpec((1, HG_HEAD), lambda h, i: (0, h)),
                  pl.BlockSpec((1, HG_HEAD), lambda h, i: (0, 0)), hcol,
                  pl.BlockSpec((1, ncb, HG_HEAD, HG_HEAD), lambda h, i: (h, nt - 1 - i, 0, 0)),
                  pl.BlockSpec((tb, HG_HEAD), lambda h, i: (nt - 1 - i, heads + h))],
        out_specs=[hcol, hcol, hcol, hcol, pl.BlockSpec((1, HG_HEAD), lambda h, i: (0, h)),
                   pl.BlockSpec((1, HG_HEAD), lambda h, i: (0, 0))],
        out_shape=[out, out, out, out, jax.ShapeDtypeStruct((1, mix_b), F32),
                   jax.ShapeDtypeStruct((1, HG_HEAD), F32)],
        scratch_shapes=scratch, compiler_params=_cp("arbitrary", "arbitrary"),
        name=name)(z, z, z, z, lb, hg_norm, o_raw, states, dycat)


def _lb_fwd(lb_table, layer, *, name):
    rows, width = lb_table.shape

    def body(t_ref, o_ref):
        tv = t_ref[...]
        e = jnp.exp(tv - jnp.max(tv, axis=0, keepdims=True))
        sm = e / jnp.sum(e, axis=0, keepdims=True)
        o_ref[...] = jnp.sum(sm[1:layer + 2, :], axis=0, keepdims=True)

    return pl.pallas_call(body, out_shape=jax.ShapeDtypeStruct((1, width), F32), name=name)(lb_table)


def _lb_bwd(lb_table, dlb, layer, *, name):
    rows, width = lb_table.shape

    def body(t_ref, d_ref, o_ref):
        tv = t_ref[...]
        e = jnp.exp(tv - jnp.max(tv, axis=0, keepdims=True))
        sm = e / jnp.sum(e, axis=0, keepdims=True)
        ridx = lax.broadcasted_iota(jnp.int32, sm.shape, 0)
        dsm = jnp.where((ridx >= 1) & (ridx <= layer + 1), d_ref[...], 0.0)
        o_ref[...] = sm * (dsm - jnp.sum(sm * dsm, axis=0, keepdims=True))

    return pl.pallas_call(body, out_shape=jax.ShapeDtypeStruct((rows, width), F32), name=name)(lb_table, dlb)


FOX_BLOCK = 1024


def _fox_prep(zf, b_f, *, name, blk=256):
    t = zf.shape[0]

    def body(z_ref, b_ref, fc_ref):
        r = lax.broadcasted_iota(jnp.int32, (blk, blk), 0)
        c = lax.broadcasted_iota(jnp.int32, (blk, blk), 1)
        tri = jnp.where(c <= r, 1.0, 0.0).astype(F32)
        carry = jnp.zeros((1, LANES), F32)
        for j in range(t // blk):
            rows = slice(j * blk, (j + 1) * blk)
            ls = jax.nn.log_sigmoid(z_ref[rows, :] + b_ref[...])
            fb = _dot_f32(tri, ls, _NN) + carry
            carry = fb[blk - 1:blk, :]
            fc_ref[rows, :] = fb

    return pl.pallas_call(
        body, out_shape=jax.ShapeDtypeStruct((t, LANES), F32),
        compiler_params=pltpu.CompilerParams(vmem_limit_bytes=VMEM_LIMIT_BYTES), name=name)(zf, b_f)


def _fox_head_column(fc_ref, fk_s, head):
    lane = lax.broadcasted_iota(jnp.int32, fc_ref.shape, 1)
    fk_s[...] = jnp.sum(jnp.where(lane == head, fc_ref[...], 0.0), axis=1, keepdims=True)


def _fox_scores(k_blk, q_blk, fk_blk, diagonal):
    s = _dot(k_blk, q_blk, _NT) * (FOX_HEAD ** -0.5) - fk_blk
    if diagonal:
        key = lax.broadcasted_iota(jnp.int32, s.shape, 0)
        qry = lax.broadcasted_iota(jnp.int32, s.shape, 1)
        s = jnp.where(key <= qry, s, -jnp.inf)
    return s


def _fox_fwd(zqkv, fcol, *, name):
    t = zqkv.shape[0]
    d = zqkv.shape[1] // 3
    heads = d // FOX_HEAD
    blk = min(FOX_BLOCK, t)
    nq = t // blk

    def body(q_ref, k_ref, v_ref, fc_ref, o_ref, lse_ref, fk_s):
        _fox_head_column(fc_ref, fk_s, pl.program_id(0))

        def q_block(qi, carry):
            qrows = pl.ds(pl.multiple_of(qi * blk, blk), blk)
            q_blk = q_ref[qrows, :]

            def update(st, krows, diagonal):
                m, l, acc = st
                s = _fox_scores(k_ref[krows, :], q_blk, fk_s[krows, :], diagonal)
                m_new = jnp.maximum(m, jnp.max(s, axis=0, keepdims=True))
                alpha = jnp.exp(m - m_new)
                p = jnp.exp(s - m_new)
                l = alpha * l + jnp.sum(p, axis=0, keepdims=True)
                acc = acc * alpha + _dot(v_ref[krows, :], p, _TN)
                return m_new, l, acc

            def k_block(kj, st):
                return update(st, pl.ds(pl.multiple_of(kj * blk, blk), blk), False)

            init = (jnp.full((1, blk), -jnp.inf, F32), jnp.zeros((1, blk), F32),
                    jnp.zeros((FOX_HEAD, blk), F32))
            m, l, acc = update(lax.fori_loop(0, qi, k_block, init), qrows, True)
            o_ref[qrows, :] = (acc / l).T.astype(o_ref.dtype)
            lse_ref[0, :, qrows] = m + jnp.log(l)
            return carry

        lax.fori_loop(0, nq, q_block, 0)

    def col(kind):
        return pl.BlockSpec((t, FOX_HEAD), lambda h: (0, kind * heads + h))

    rowvec = pl.BlockSpec((1, 1, t), lambda h: (h, 0, 0))
    return pl.pallas_call(
        body, grid=(heads,),
        in_specs=[col(0), col(1), col(2), pl.BlockSpec((t, LANES), lambda h: (0, 0))],
        out_specs=[pl.BlockSpec((t, FOX_HEAD), lambda h: (0, h)), rowvec],
        out_shape=[jax.ShapeDtypeStruct((t, d), MXU_DTYPE), jax.ShapeDtypeStruct((heads, 1, t), F32)],
        scratch_shapes=[pltpu.VMEM((t, 1), F32)],
        compiler_params=_cp("parallel"), name=name)(zqkv, zqkv, zqkv, fcol)


def _fox_bwd(zqkv, fcol, lse, o, do, *, name):
    t = zqkv.shape[0]
    d = zqkv.shape[1] // 3
    heads = d // FOX_HEAD
    blk = min(FOX_BLOCK, t)
    nq = t // blk
    scale = FOX_HEAD ** -0.5

    def body(q_ref, k_ref, v_ref, fc_ref, lse_ref, o_ref, do_ref,
             dq_ref, dk_ref, dv_ref, rq_ref, rk_ref, dq_s, drow_s, fk_s, acc_s):
        _fox_head_column(fc_ref, fk_s, pl.program_id(0))
        ones = jnp.ones((blk, FOX_HEAD), MXU_DTYPE)
        dq_s[...] = jnp.zeros_like(dq_s)
        rq_ref[...] = jnp.zeros_like(rq_ref)
        ones_f = jnp.ones((8, FOX_HEAD), F32)
        for j in range(nq):
            rows = slice(j * blk, (j + 1) * blk)
            prod = do_ref[rows, :].astype(F32) * o_ref[rows, :].astype(F32)
            drow_s[:, rows] = _dot_f32(ones_f, prod, _NT)

        def k_block(kj, carry):
            krows = pl.ds(pl.multiple_of(kj * blk, blk), blk)
            k_blk, v_blk, fk_blk = k_ref[krows, :], v_ref[krows, :], fk_s[krows, :]

            def pair(qrows, diagonal):
                q_blk, do_blk = q_ref[qrows, :], do_ref[qrows, :]
                s = _fox_scores(k_blk, q_blk, fk_blk, diagonal)
                p = jnp.exp(s - lse_ref[0, :, qrows])
                acc_s[1] += _dot(p, do_blk, _NN)
                dp = _dot(v_blk, do_blk, _NT)
                ds = (p * (dp - drow_s[0:1, qrows])).astype(MXU_DTYPE)
                acc_s[0] += _dot(ds, q_blk, _NN)
                dq_s[qrows, :] += _dot(ds, k_blk, _TN)
                rq_ref[0, :, qrows] += jnp.sum(ds.astype(F32), axis=0, keepdims=True)
                acc_s[2] += _dot(ds, ones, _NN)

            def q_block(qi, carry2):
                pair(pl.ds(pl.multiple_of(qi * blk, blk), blk), False)
                return carry2

            acc_s[...] = jnp.zeros_like(acc_s)
            pair(krows, True)
            lax.fori_loop(kj + 1, nq, q_block, 0)
            dk_ref[krows, :] = (acc_s[0] * scale).astype(dk_ref.dtype)
            dv_ref[krows, :] = acc_s[1].astype(dv_ref.dtype)
            rk_ref[0, :, krows] = acc_s[2].T[0:1, :]
            return carry

        lax.fori_loop(0, nq, k_block, 0)
        dq_ref[...] = (dq_s[...] * scale).astype(dq_ref.dtype)

    def col(kind):
        return pl.BlockSpec((t, FOX_HEAD), lambda h: (0, kind * heads + h))

    hcol = pl.BlockSpec((t, FOX_HEAD), lambda h: (0, h))
    rowvec = pl.BlockSpec((1, 1, t), lambda h: (h, 0, 0))
    out = jax.ShapeDtypeStruct((t, d), MXU_DTYPE)
    vec = jax.ShapeDtypeStruct((heads, 1, t), F32)
    return pl.pallas_call(
        body, grid=(heads,),
        in_specs=[col(0), col(1), col(2), pl.BlockSpec((t, LANES), lambda h: (0, 0)), rowvec, hcol, hcol],
        out_specs=[hcol, hcol, hcol, rowvec, rowvec],
        out_shape=[out, out, out, vec, vec],
        scratch_shapes=[pltpu.VMEM((t, FOX_HEAD), F32), pltpu.VMEM((8, t), F32), pltpu.VMEM((t, 1), F32),
                        pltpu.VMEM((3, blk, FOX_HEAD), F32)],
        compiler_params=_cp("parallel"), name=name)(zqkv, zqkv, zqkv, fcol, lse, o, do)


def _fox_gate_bwd(rq, rk, zf, b_f, *, name, blk=256):
    heads, t = rq.shape

    def body(rq_ref, rk_ref, z_ref, b_ref, dfl_ref, db_ref):
        r = lax.broadcasted_iota(jnp.int32, (blk, blk), 0)
        c = lax.broadcasted_iota(jnp.int32, (blk, blk), 1)
        tri = jnp.where(r >= c, 1.0, 0.0).astype(F32)
        carry = jnp.zeros((heads, 1), F32)
        db = jnp.zeros((1, LANES), F32)
        pad = jnp.zeros((LANES - heads, blk), F32)
        for j in reversed(range(t // blk)):
            cols = slice(j * blk, (j + 1) * blk)
            df = rq_ref[:, cols] - rk_ref[:, cols]
            dls = _dot_f32(df, tri, _NN) + carry
            carry = dls[:, 0:1]
            dls_t = jnp.concatenate([dls, pad], axis=0).T
            dfl = dls_t * _sigmoid(-(z_ref[cols, :] + b_ref[...]))
            dfl_ref[cols, :] = dfl.astype(dfl_ref.dtype)
            db = db + jnp.sum(dfl, axis=0, keepdims=True)
        db_ref[...] = db

    return pl.pallas_call(
        body, out_shape=[jax.ShapeDtypeStruct((t, LANES), MXU_DTYPE), jax.ShapeDtypeStruct((1, LANES), F32)],
        compiler_params=pltpu.CompilerParams(vmem_limit_bytes=VMEM_LIMIT_BYTES), name=name)(rq, rk, zf, b_f)


def _adamw(w, m, v, parts, *, name, layer=None, prev=None, tr=128):
    lcnt, r, c = w.shape
    p = parts.shape[0]
    li = 0 if layer is None else layer
    tr = _tile(r, tr, 16)
    tc = c if tr < r or r * c <= 128 * 2048 else _tile(c, 256)
    has_prev = prev is not None

    def body(*refs):
        w_ref, m_ref, v_ref, p_ref = refs[:4]
        g_ref, d_ref, nm_ref, nv_ref = refs[-4:]
        g = p_ref[0].astype(F32)
        for j in range(1, p):
            g = g + p_ref[j].astype(F32)
        wv = w_ref[0]
        mn = ADAM_B1 * m_ref[0] + (1.0 - ADAM_B1) * g
        vn = ADAM_B2 * v_ref[0] + (1.0 - ADAM_B2) * (g * g)
        m_hat = mn / (1.0 - ADAM_B1 ** ADAM_STEP)
        v_hat = vn / (1.0 - ADAM_B2 ** ADAM_STEP)
        g_ref[0] = g
        d_ref[0] = -ADAM_LR * (m_hat / (jnp.sqrt(v_hat) + ADAM_EPS) + ADAM_WD * wv)
        nm_ref[0] = mn
        nv_ref[0] = vn

    slab = pl.BlockSpec((1, tr, tc), lambda i, j: (li, i, j))
    in_specs = [slab, slab, slab, pl.BlockSpec((p, tr, tc), lambda i, j: (0, i, j))]
    operands = [w, m, v, parts]
    aliases = {}
    if has_prev:
        in_specs += [pl.BlockSpec(memory_space=pl.ANY)] * 4
        operands += list(prev)
        aliases = {4: 0, 5: 1, 6: 2, 7: 3}
    shp = jax.ShapeDtypeStruct((lcnt, r, c), F32)
    return pl.pallas_call(
        body, grid=(r // tr, c // tc), in_specs=in_specs, out_specs=[slab] * 4, out_shape=[shp] * 4,
        input_output_aliases=aliases, compiler_params=_cp("parallel", "parallel"), name=name)(*operands)


def _my_place():
    return lax.axis_index("x"), lax.axis_index("y"), lax.axis_index("c")


def _slot(p):
    return 4 * p[0] + 2 * p[1] + p[2]


def _peer(me, mask):
    x, y, c = me
    return (1 - x if mask & 4 else x, 1 - y if mask & 2 else y, 1 - c if mask & 1 else c)


_HBM = pl.BlockSpec(memory_space=pltpu.HBM)
_SEM = pl.BlockSpec(memory_space=pltpu.SEMAPHORE)
_ANY = pl.BlockSpec(memory_space=pl.ANY)
_EFFECT = pltpu.SideEffectType.DATAFLOW_SIDE_EFFECTING


def _push_copy(src_refs, land_refs, send_sems, recv_sems, a, mask, me, per_peer, outgoing):
    peer = _peer(me, mask)
    src = src_refs[a].at[_slot(peer)] if per_peer else src_refs[a]
    dst = land_refs[a].at[_slot(me) if outgoing else _slot(peer)]
    k = a * (N_DEV - 1) + mask - 1
    return pltpu.make_async_remote_copy(
        src_ref=src, dst_ref=dst, send_sem=send_sems.at[k], recv_sem=recv_sems.at[k],
        device_id=peer, device_id_type=MESH)


ALL_PEERS = tuple(range(1, N_DEV))
CHIP_PEERS = (2, 4, 6)
FIRST_HOP = (1,) + CHIP_PEERS


def _push_start(srcs, dep, *, per_peer, name, masks=ALL_PEERS):
    n = len(srcs)
    mine = _slot(_my_place())
    lands = []
    for s in srcs:
        own = lax.dynamic_index_in_dim(s, mine, 0, keepdims=True) if per_peer else s[None]
        shape = s.shape if per_peer else (N_DEV,) + s.shape
        lands.append(lax.dynamic_update_slice_in_dim(lax.empty(shape, s.dtype), own, mine, 0))
    has_dep = dep is not None

    def body(*refs):
        src_refs, land_refs = refs[:n], refs[n:2 * n]
        send_sems, recv_sems = refs[2 * n + has_dep], refs[2 * n + has_dep + 1]
        token = refs[-1]
        me = _my_place()
        for a in range(n):
            for mask in masks:
                _push_copy(src_refs, land_refs, send_sems, recv_sems, a, mask, me, per_peer, True).start()
        token[...] = jnp.zeros_like(token)

    hbm_in = [pltpu.with_memory_space_constraint(v, pltpu.HBM) for v in list(srcs) + lands]
    out = pl.pallas_call(
        body, name=name,
        out_shape=(pltpu.SemaphoreType.DMA((n * (N_DEV - 1),)), pltpu.SemaphoreType.DMA((n * (N_DEV - 1),)),
                   *[pltpu.HBM(v.shape, v.dtype) for v in hbm_in], jax.ShapeDtypeStruct((8, LANES), F32)),
        in_specs=[_HBM] * (2 * n) + ([_ANY] if has_dep else []),
        out_specs=(_SEM, _SEM, *[_HBM] * (2 * n), pl.BlockSpec(memory_space=pltpu.VMEM)),
        input_output_aliases={i: 2 + i for i in range(2 * n)},
        compiler_params=pltpu.CompilerParams(has_side_effects=_EFFECT),
    )(*hbm_in, *([dep] if has_dep else []))
    return (n, per_peer, masks, out[:-1]), out[-1]


def _push_wait(handle, after, *, name):
    n, per_peer, masks, (send_sems, recv_sems, *bufs) = handle

    def body(*refs):
        src_refs, land_refs = refs[:n], refs[n:2 * n]
        send_sems, recv_sems = refs[2 * n], refs[2 * n + 1]
        me = _my_place()
        for a in range(n):
            for mask in masks:
                cp = _push_copy(src_refs, land_refs, send_sems, recv_sems, a, mask, me, per_peer, False)
                cp.wait_send()
                cp.wait_recv()

    out = pl.pallas_call(
        body, name=name, out_shape=tuple(pltpu.HBM(v.shape, v.dtype) for v in bufs),
        in_specs=[_HBM] * (2 * n) + [_SEM, _SEM, _ANY], out_specs=tuple([_HBM] * (2 * n)),
        input_output_aliases={i: i for i in range(2 * n)},
        compiler_params=pltpu.CompilerParams(has_side_effects=_EFFECT),
    )(*bufs, send_sems, recv_sems, after)
    return list(out[n:])


def _relay_copy(land_refs, send_sems, recv_sems, a, j, me, outgoing):
    sibling = _peer(me, 1)
    out_slot = _slot(_peer(me, CHIP_PEERS[j]))
    in_slot = _slot(_peer(sibling, CHIP_PEERS[j]))
    k = a * len(CHIP_PEERS) + j
    return pltpu.make_async_remote_copy(
        src_ref=land_refs[a].at[out_slot], dst_ref=land_refs[a].at[out_slot if outgoing else in_slot],
        send_sem=send_sems.at[k], recv_sem=recv_sems.at[k], device_id=sibling, device_id_type=MESH)


def _relay_start(lands, *, name):
    n = len(lands)

    def body(*refs):
        land_refs, send_sems, recv_sems, token = refs[:n], refs[n], refs[n + 1], refs[-1]
        me = _my_place()
        for a in range(n):
            for j in range(len(CHIP_PEERS)):
                _relay_copy(land_refs, send_sems, recv_sems, a, j, me, True).start()
        token[...] = jnp.zeros_like(token)

    hbm_in = [pltpu.with_memory_space_constraint(v, pltpu.HBM) for v in lands]
    n_sem = n * len(CHIP_PEERS)
    out = pl.pallas_call(
        body, name=name,
        out_shape=(pltpu.SemaphoreType.DMA((n_sem,)), pltpu.SemaphoreType.DMA((n_sem,)),
                   *[pltpu.HBM(v.shape, v.dtype) for v in hbm_in], jax.ShapeDtypeStruct((8, LANES), F32)),
        in_specs=[_HBM] * n, out_specs=(_SEM, _SEM, *[_HBM] * n, pl.BlockSpec(memory_space=pltpu.VMEM)),
        input_output_aliases={i: 2 + i for i in range(n)},
        compiler_params=pltpu.CompilerParams(has_side_effects=_EFFECT),
    )(*hbm_in)
    return (n, out[:-1]), out[-1]


def _relay_wait(handle, after, *, name):
    n, (send_sems, recv_sems, *bufs) = handle

    def body(*refs):
        land_refs, send_sems, recv_sems = refs[:n], refs[n], refs[n + 1]
        me = _my_place()
        for a in range(n):
            for j in range(len(CHIP_PEERS)):
                cp = _relay_copy(land_refs, send_sems, recv_sems, a, j, me, False)
                cp.wait_send()
                cp.wait_recv()

    out = pl.pallas_call(
        body, name=name, out_shape=tuple(pltpu.HBM(v.shape, v.dtype) for v in bufs),
        in_specs=[_HBM] * n + [_SEM, _SEM, _ANY], out_specs=tuple([_HBM] * n),
        input_output_aliases={i: i for i in range(n)},
        compiler_params=pltpu.CompilerParams(has_side_effects=_EFFECT),
    )(*bufs, send_sems, recv_sems, after)
    return list(out)


def _all_reduce_rows(v, *, name):
    r, c = v.shape

    def body(v_ref, o_ref, buf, send_sems, recv_sems):
        me = _my_place()
        mine = _slot(me)
        sends = []
        for mask in range(1, N_DEV):
            peer = _peer(me, mask)
            sends.append(pltpu.make_async_remote_copy(
                src_ref=v_ref, dst_ref=buf.at[mine], send_sem=send_sems.at[mask - 1],
                recv_sem=recv_sems.at[mask - 1], device_id=peer, device_id_type=MESH))
        for cp in sends:
            cp.start()
        buf[mine] = v_ref[...]
        for mask in range(1, N_DEV):
            peer = _peer(me, mask)
            pltpu.make_async_remote_copy(
                src_ref=v_ref, dst_ref=buf.at[_slot(peer)], send_sem=send_sems.at[mask - 1],
                recv_sem=recv_sems.at[mask - 1], device_id=peer, device_id_type=MESH).wait_recv()
        for cp in sends:
            cp.wait_send()
        total = buf[0]
        for j in range(1, N_DEV):
            total = total + buf[j]
        o_ref[...] = total

    vm = pl.BlockSpec(memory_space=pltpu.VMEM)
    return pl.pallas_call(
        body, in_specs=[vm], out_specs=vm, out_shape=jax.ShapeDtypeStruct((r, c), F32),
        scratch_shapes=[pltpu.VMEM((N_DEV, r, c), F32), pltpu.SemaphoreType.DMA((7,)),
                        pltpu.SemaphoreType.DMA((7,))],
        name=name)(v)


def _xa_fwd(x, mem, g_x, g_m, wq, wkv, wo, tag):
    hx = _rms_fwd(x, g_x, name=f"xa{tag}_norm")
    memn = _rms_fwd(mem, g_m, name=f"xa{tag}_mem_norm")
    q = _mm2(hx, wq, "nn", MXU_DTYPE, name=f"xa{tag}_q")
    kv = _mm2(memn, wkv, "nn", MXU_DTYPE, name=f"xa{tag}_kv", b_split=True)
    o = _xattn_fwd(q, kv, name=f"xa{tag}_attn")
    return _mm2(o, wo, "nn", F32, name=f"xa{tag}_out", add=x), (hx, memn, q, kv, o)


def _xa_bwd(x, mem, g_x, g_m, wq, wkv, wo, saved, dxo, dxo_lo, tag, put):
    hx, memn, q, kv, o = saved
    do = _mm2(dxo_lo, wo, "nt", MXU_DTYPE, name=f"xa{tag}_do")
    dwo = _mm2(o, dxo_lo, "tn", MXU_DTYPE, name=f"xa{tag}_dwo")
    dq, dkv = _xattn_bwd(q, kv, do, name=f"xa{tag}_attn_bwd")
    dwq = _mm2(hx, dq, "tn", MXU_DTYPE, name=f"xa{tag}_dwq")
    dwkv = _mm2(memn, dkv, "tn", MXU_DTYPE, name=f"xa{tag}_dwkv", o_split=True, tn=wkv.shape[2])
    tok = put((dwq, dwkv, dwo))
    dhx = _mm2(dq, wq, "nt", F32, name=f"xa{tag}_dh", dep=tok)
    dx, dx_lo, dgx = _rms_bwd(x, g_x, dhx, dxo, name=f"xa{tag}_norm_bwd")
    dmemn = _mm2(dkv, wkv, "nt", F32, name=f"xa{tag}_dmem", b_split=True)
    _, _, dgm = _rms_bwd(mem, g_m, dmemn, None, name=f"xa{tag}_mem_norm_bwd")
    return dx, dx_lo, dgx, dgm


def _ffn_fwd(x, g, wg, wu, wd, tag):
    h = _rms_fwd(x, g, name=f"ffn{tag}_norm")
    gate, up, act = _ffn_up(h, wg, wu, name=f"ffn{tag}_up")
    return _ffn_down(act, wd, x, name=f"ffn{tag}_down"), (h, gate, up, act)


def _ffn_bwd(x, g, wg, wu, wd, saved, dxo, dxo_lo, tag, put):
    h, gate, up, act = saved
    dwd = _mm(act, dxo_lo[None], "tn", MXU_DTYPE, name=f"ffn{tag}_dwd")
    dgate, dup = _ffn_dact(dxo_lo, wd, gate, up, name=f"ffn{tag}_dact")
    dwg = _mm(dgate, h[None], "tn", MXU_DTYPE, name=f"ffn{tag}_dwg")
    dwu = _mm(dup, h[None], "tn", MXU_DTYPE, name=f"ffn{tag}_dwu")
    tok = put((dwg, dwu, dwd))
    dh = _ffn_dh(dgate, dup, wg, wu, tok, name=f"ffn{tag}_dh")
    dx, dx_lo, dg = _rms_bwd(x, g, dh, dxo, name=f"ffn{tag}_norm_bwd")
    return dx, dx_lo, dg


def _even_fwd(x, g, lb, w_in, w_pool, pool_scale, hg_norm, w_out):
    h = _rms_fwd(x, g, name="ev_norm")
    z = _mm2(h, w_in, "nn", F32, name="ev_in", b_split=True)
    ycat = _pool_fwd(z, w_pool, pool_scale, name="ev_pool")
    ycat, o_raw, states = _hgrn_fwd(z, lb, hg_norm, ycat, name="ev_hgrn")
    return _mm2(ycat, w_out, "nn", F32, name="ev_out", add=x), (h, z, ycat, o_raw, states)


def _even_bwd(x, g, lb, w_in, w_pool, pool_scale, hg_norm, w_out, saved, dxo, dxo_lo, put):
    h, z, ycat, o_raw, states = saved
    dycat = _mm2(dxo_lo, w_out, "nt", MXU_DTYPE, name="ev_dy")
    dw_out = _mm2(ycat, dxo_lo, "tn", MXU_DTYPE, name="ev_dw_out")
    du, dw_pool, dscale = _pool_bwd(z, w_pool, pool_scale, dycat, name="ev_pool_bwd")
    dq, dfl, di, dg, dlb, dhn = _hgrn_bwd(z, lb, hg_norm, o_raw, states, dycat, name="ev_hgrn_bwd")
    dz = jnp.concatenate([du, dq, dfl, di, dg], axis=1)
    dw_in = _mm2(h, dz, "tn", MXU_DTYPE, name="ev_dw_in", o_split=True, tn=w_in.shape[2])
    tok = put((dw_in, dw_pool, dw_out))
    dh = _mm_nt_split(dz, w_in, tok, name="ev_dh")
    dx, dx_lo, dgn = _rms_bwd(x, g, dh, dxo, name="ev_norm_bwd")
    return dx, dx_lo, dict(ev_norm=dgn, ev_pool_scale=dscale, ev_hg_norm=dhn, lb=dlb)


def _odd_fwd(x, g, w_qkv, w_f, b_f, w_out):
    n_qkv = 3 * x.shape[1]
    h = _rms_fwd(x, g, name="od_norm")
    zqkv = _mm2(h, w_qkv, "nt", MXU_DTYPE, name="od_qkv", n_b=n_qkv)
    zf = _mm2(h, w_f, "nt", F32, name="od_gate")
    fcol = _fox_prep(zf, b_f, name="od_fox_prep")
    o, lse = _fox_fwd(zqkv, fcol, name="od_fox")
    return _mm2(o, w_out, "nn", F32, name="od_out", add=x), (h, zqkv, zf, fcol, o, lse)


def _odd_bwd(x, g, w_qkv, w_f, b_f, w_out, saved, dxo, dxo_lo, put):
    h, zqkv, zf, fcol, o, lse = saved
    do = _mm2(dxo_lo, w_out, "nt", MXU_DTYPE, name="od_do")
    dw_out = _mm2(o, dxo_lo, "tn", MXU_DTYPE, name="od_dw_out")
    dq, dk, dv, rq, rk = _fox_bwd(zqkv, fcol, lse, o, do, name="od_fox_bwd")
    dfl, db_f = _fox_gate_bwd(rq[:, 0, :], rk[:, 0, :], zf, b_f, name="od_fox_gate_bwd")
    dz = jnp.concatenate([dq, dk, dv], axis=1)
    dw_qkv = _mm2(dz, h, "tn", MXU_DTYPE, name="od_dw_qkv")
    dw_f = _mm2(dfl, h, "tn", MXU_DTYPE, name="od_dw_gate")
    tok = put((dw_qkv, dw_f, dw_out))
    dh = _mm2(dz, w_qkv, "nn", F32, name="od_dh_qkv", dep=tok)
    dh = _mm2(dfl, w_f, "nn", F32, name="od_dh_gate", add=dh)
    dx, dx_lo, dgn = _rms_bwd(x, g, dh, dxo, name="od_norm_bwd")
    return dx, dx_lo, dict(od_norm=dgn, od_b_f=db_f)


def _local_step(x, mem, target, sp, get_w, put_dw):
    b_f = jnp.pad(sp["od_b_f"], ((0, 0), (0, LANES - sp["od_b_f"].shape[1])))
    lb = _lb_fwd(sp["lb_table"], 0, name="lb_fwd")
    fin = sp["final_norm"].reshape(1, -1)
    xn, xm, fn = sp["xa_norm"], sp["xa_mem_norm"], sp["ffn_norm"]
    w_ev = get_w("ev", None)
    x1, s_ev = _even_fwd(x, sp["ev_norm"], lb, w_ev[0], w_ev[1], sp["ev_pool_scale"], sp["ev_hg_norm"], w_ev[2])
    w_xa0 = get_w("xa0", x1)
    x2, s_xa0 = _xa_fwd(x1, mem, xn[0:1], xm[0:1], *w_xa0, 0)
    w_ff0 = get_w("ffn0", x2)
    x3, s_ff0 = _ffn_fwd(x2, fn[0:1], *w_ff0, 0)
    w_qkv, w_f, w_od_out, od_norm = get_w("od", x3)
    x4, s_od = _odd_fwd(x3, od_norm, w_qkv, w_f, b_f, w_od_out)
    w_xa1 = get_w("xa1", x4)
    x5, s_xa1 = _xa_fwd(x4, mem, xn[1:2], xm[1:2], *w_xa1, 1)
    w_ff1 = get_w("ffn1", x5)
    x6, s_ff1 = _ffn_fwd(x5, fn[1:2], *w_ff1, 1)
    loss, dx, dx_lo, d_fin = _loss_head(x6, fin, target, name="loss_head")
    put = lambda grp: functools.partial(put_dw, grp)
    dx, dx_lo, d_ffn1 = _ffn_bwd(x5, fn[1:2], *w_ff1, s_ff1, dx, dx_lo, 1, put("ffn1"))
    dx, dx_lo, d_xa1, d_xm1 = _xa_bwd(x4, mem, xn[1:2], xm[1:2], *w_xa1, s_xa1, dx, dx_lo, 1, put("xa1"))
    dx, dx_lo, d_od = _odd_bwd(x3, od_norm, w_qkv, w_f, b_f, w_od_out, s_od, dx, dx_lo, put("od"))
    dx, dx_lo, d_ffn0 = _ffn_bwd(x2, fn[0:1], *w_ff0, s_ff0, dx, dx_lo, 0, put("ffn0"))
    dx, dx_lo, d_xa0, d_xm0 = _xa_bwd(x1, mem, xn[0:1], xm[0:1], *w_xa0, s_xa0, dx, dx_lo, 0, put("xa0"))
    dx, _, d_ev = _even_bwd(x, sp["ev_norm"], lb, w_ev[0], w_ev[1], sp["ev_pool_scale"], sp["ev_hg_norm"],
                            w_ev[2], s_ev, dx, dx_lo, put("ev"))
    small = dict(
        lb_table=_lb_bwd(sp["lb_table"], d_ev["lb"], 0, name="lb_bwd"),
        ev_norm=d_ev["ev_norm"], ev_pool_scale=d_ev["ev_pool_scale"], ev_hg_norm=d_ev["ev_hg_norm"],
        od_norm=d_od["od_norm"], od_b_f=d_od["od_b_f"][:, :sp["od_b_f"].shape[1]],
        xa_norm=jnp.concatenate([d_xa0, d_xa1], axis=0), xa_mem_norm=jnp.concatenate([d_xm0, d_xm1], axis=0),
        ffn_norm=jnp.concatenate([d_ffn0, d_ffn1], axis=0), final_norm=d_fin.reshape(-1))
    return loss, dx, small


_SMALL = ("lb_table", "ev_norm", "ev_pool_scale", "ev_hg_norm", "od_norm", "od_b_f", "xa_norm", "xa_mem_norm",
          "ffn_norm", "final_norm")
_WEIGHTS = ("lb_table", "ev_norm", "ev_w_in", "ev_w_pool", "ev_pool_scale", "ev_hg_norm", "ev_w_out", "od_norm",
            "od_w_in", "od_b_f", "od_w_out", "xa_norm", "xa_mem_norm", "xa_wq", "xa_wkv", "xa_wo", "ffn_norm",
            "ffn_w_gate", "ffn_w_up", "ffn_w_down", "final_norm")


def _lo(a):
    return a.astype(MXU_DTYPE)


def _rows(v):
    flat = v.reshape(-1)
    return jnp.pad(flat, (0, (-flat.shape[0]) % LANES)).reshape(-1, LANES)


def kernel(x, mem, lb_table, ev_norm, ev_w_in, ev_w_pool, ev_pool_scale, ev_hg_norm, ev_w_out, od_norm, od_w_in, od_b_f, od_w_out, xa_norm, xa_mem_norm, xa_wq, xa_wkv, xa_wo, ffn_norm, ffn_w_gate, ffn_w_up, ffn_w_down, final_norm, loss_target, m_lb_table, m_ev_norm, m_ev_w_in, m_ev_w_pool, m_ev_pool_scale, m_ev_hg_norm, m_ev_w_out, m_od_norm, m_od_w_in, m_od_b_f, m_od_w_out, m_xa_norm, m_xa_mem_norm, m_xa_wq, m_xa_wkv, m_xa_wo, m_ffn_norm, m_ffn_w_gate, m_ffn_w_up, m_ffn_w_down, m_final_norm, v_lb_table, v_ev_norm, v_ev_w_in, v_ev_w_pool, v_ev_pool_scale, v_ev_hg_norm, v_ev_w_out, v_od_norm, v_od_w_in, v_od_b_f, v_od_w_out, v_xa_norm, v_xa_mem_norm, v_xa_wq, v_xa_wkv, v_xa_wo, v_ffn_norm, v_ffn_w_gate, v_ffn_w_up, v_ffn_w_down, v_final_norm):
    arg = dict(locals())
    d = x.shape[-1]
    layers = xa_wq.shape[0]
    me = _slot(_my_place())

    n_gate = od_b_f.shape[1]
    turned = {k: jnp.swapaxes(arg[k], 1, 2) for k in ("od_w_in", "ffn_w_gate", "ffn_w_up")}
    raw = dict(ev=[ev_w_in[0], ev_w_pool[0], ev_w_out[0]], od=[turned["od_w_in"][0], od_w_out[0], od_norm])
    for l in range(layers):
        raw[f"xa{l}"] = [xa_wq[l], xa_wkv[l], xa_wo[l]]
        raw[f"ffn{l}"] = [turned["ffn_w_gate"][l], turned["ffn_w_up"][l], ffn_w_down[l]]
    order = ("ev", "xa0", "ffn0", "od", "xa1", "ffn1")
    gathers, relays, tok = {}, {}, None
    for grp in order:
        srcs = [w if tok is None else w + tok[0, 0] for w in raw[grp]]
        srcs = [w if grp == "od" and j == 2 else _lo(w) for j, w in enumerate(srcs)]
        gathers[grp], tok = _push_start(srcs, None, per_peer=False, masks=FIRST_HOP, name=f"gather_{grp}_start")
    last_start = tok

    def second_hop(grp, after):
        lands = _push_wait(gathers[grp], after, name=f"gather_{grp}_wait")
        relays[grp], token = _relay_start(lands, name=f"gather_{grp}_relay")
        return token

    def get_w(grp, after):
        i = order.index(grp)
        after = last_start if after is None else after
        if grp not in relays:
            after = second_hop(grp, after)
        if 1 <= i < len(order) - 1:
            after = second_hop(order[i + 1], after)
        got = _relay_wait(relays[grp], after, name=f"gather_{grp}_relay_wait")
        if grp == "ev":
            w_in, w_pool, w_out = got
            w_pool = jnp.transpose(w_pool, (1, 0, 2, 3)).reshape(w_pool.shape[1], -1, w_pool.shape[3])
            return w_in, w_pool, w_out.reshape(d, d)
        if grp == "od":
            w_in, w_out, nrm = got
            w_in = w_in.reshape(-1, d)
            w_f = jnp.pad(w_in[w_in.shape[0] - n_gate:], ((0, LANES - n_gate), (0, 0)))
            return w_in, w_f, w_out.reshape(d, d), nrm.reshape(1, d)
        if grp.startswith("xa"):
            return got[0].reshape(d, d), got[1], got[2].reshape(d, d)
        return tuple(got)

    def row_parts(g):
        return g.reshape(N_DEV, -1, g.shape[-1])

    scatters = {}

    def put_dw(grp, dws):
        if grp == "ev":
            dw_in, dw_pool, dw_out = dws
            gc = dw_pool.shape[1] // N_DEV
            dw_pool = _lo(jnp.transpose(dw_pool.reshape(dw_pool.shape[0], N_DEV, gc, -1), (1, 0, 2, 3)))
            parts = [dw_in, dw_pool, row_parts(dw_out)]
        elif grp == "od":
            dw_qkv, dw_f, dw_out = dws
            parts = [row_parts(jnp.concatenate([dw_qkv, dw_f[:n_gate]], axis=0)), row_parts(dw_out)]
        elif grp.startswith("xa"):
            parts = [row_parts(dws[0]), dws[1], row_parts(dws[2])]
        else:
            parts = list(dws)
        scatters[grp], token = _push_start(parts, None, per_peer=True, name=f"scatter_{grp}_start")
        return token

    sp = {k: arg[k] for k in _SMALL if k != "od_norm"}
    loss, dx, small = _local_step(x[0], mem[0], loss_target[0], sp, get_w, put_dw)

    pieces = [_rows(small[k]) for k in _SMALL]
    packed = jnp.concatenate(pieces + [_rows(loss)], axis=0)
    packed = jnp.pad(packed, ((0, (-packed.shape[0]) % 8), (0, 0)))
    total = _all_reduce_rows(packed, name="all_reduce_small")
    loss = total[sum(pc.shape[0] for pc in pieces), 0]
    small_g, at = {}, 0
    for k, pc in zip(_SMALL, pieces):
        n = small[k].size
        small_g[k] = total[at:at + pc.shape[0]].reshape(-1)[:n].reshape(small[k].shape)
        at += pc.shape[0]
    small_g["od_norm"] = lax.dynamic_slice_in_dim(small_g["od_norm"], me * od_norm.shape[1], od_norm.shape[1], axis=1)

    res = {}
    for k in _SMALL:
        w, m, v = arg[k], arg["m_" + k], arg["v_" + k]
        shp = (1, 1, w.shape[0]) if w.ndim == 1 else (1,) + w.shape
        out = _adamw(w.reshape(shp), m.reshape(shp), v.reshape(shp), small_g[k].reshape(shp), name=f"adamw_{k}")
        res[k] = [o.reshape(w.shape) for o in out]
    members = dict(ev=("ev_w_in", "ev_w_pool", "ev_w_out"), od=("od_w_in", "od_w_out"),
                   xa=("xa_wq", "xa_wkv", "xa_wo"), ffn=("ffn_w_gate", "ffn_w_up", "ffn_w_down"))
    after, stacked = dx, {}
    for grp in ("ffn1", "xa1", "od", "ffn0", "xa0", "ev"):
        got = _push_wait(scatters[grp], after, name=f"scatter_{grp}_wait")
        kind = grp.rstrip("01")
        for k, parts in zip(members[kind], got):
            w, m, v = [jnp.swapaxes(a, 1, 2) if k in turned else a for a in (arg[k], arg["m_" + k], arg["v_" + k])]
            if w.shape[0] == 1:
                shp = (1, -1, w.shape[-1])
                out = _adamw(w.reshape(shp), m.reshape(shp), v.reshape(shp), parts.reshape(N_DEV, -1, w.shape[-1]),
                             name=f"adamw_{k}")
            else:
                out = _adamw(w, m, v, parts, name=f"adamw_{k}{grp[-1]}", layer=int(grp[-1]), prev=stacked.get(k))
                stacked[k] = out
            res[k] = [jnp.swapaxes(o.reshape(w.shape), 1, 2) if k in turned else o.reshape(w.shape) for o in out]
            after = out[3][:1, :8, :LANES]

    outs = [loss, dx[None]]
    for j in range(4):
        outs += [res[k][j] for k in _WEIGHTS]
    return tuple(outs)
```

```python
import functools

import jax
import jax.numpy as jnp
from jax import lax
from jax.experimental import pallas as pl
from jax.experimental.pallas import tpu as pltpu

F32 = jnp.float32
MXU_DTYPE = jnp.bfloat16
EPS = 1e-6
N_DEV = 8
V7X_VMEM_BYTES = 64 * 1024 * 1024
VMEM_LIMIT_BYTES = V7X_VMEM_BYTES - 8 * 1024 * 1024
LANES = 128
HIGHEST = lax.Precision.HIGHEST
MESH = pl.DeviceIdType.MESH

HG_HEAD = 128
HG_CHUNK = 32
FOX_HEAD = 128
XA_HEADS = 4
POOL_GROUPS = 4

ADAM_LR = 0.001
ADAM_B1 = 0.9
ADAM_B2 = 0.999
ADAM_EPS = 1e-08
ADAM_WD = 0.01
ADAM_STEP = 10

_NN = ((1,), (0,))
_NT = ((1,), (1,))
_TN = ((0,), (0,))


def _dot(a, b, dims):
    return lax.dot_general(a.astype(MXU_DTYPE), b.astype(MXU_DTYPE), (dims, ((), ())),
                           preferred_element_type=F32)


def _dot_f32(a, b, dims):
    return lax.dot_general(a, b, (dims, ((), ())), preferred_element_type=F32, precision=HIGHEST)


def _cp(*sem):
    return pltpu.CompilerParams(dimension_semantics=sem, vmem_limit_bytes=VMEM_LIMIT_BYTES)


def _tile(n, pref, align=LANES):
    if n <= pref:
        return n
    t = (pref // align) * align
    while t >= align:
        if n % t == 0:
            return t
        t -= align
    return n


def _sigmoid(x):
    return jax.nn.sigmoid(x)


def _mm(a, b, mode, out_dtype, *, name, add=None, dep=None, reduce_b=False, b_split=False, o_split=False,
        n_b=None, tm=1024, tn=1024, tk=2048):
    ba, bb = a.shape[0], b.shape[0]
    if mode == "tn":
        kdim, m = a.shape[1], a.shape[2]
        tk = 2 * tk
    else:
        m, kdim = a.shape[1], a.shape[2]
    if b_split:
        s_cnt, b_rows, w = b.shape
        if mode == "nt":
            n = b_rows
            assert kdim == s_cnt * w
            tk = w
        else:
            n = s_cnt * w
            assert b_rows == kdim
            tn = w
        nb = ba
    else:
        n = b.shape[1] if mode == "nt" else b.shape[2]
        n = n if n_b is None else n_b
        nb = max(ba, bb)
    if not (b_split and mode != "nt"):
        tn = _tile(n, tn)
    if not (b_split and mode == "nt"):
        tk = _tile(kdim, tk)
    tm = _tile(m, tm)
    assert m % tm == 0 and n % tn == 0 and kdim % tk == 0, (name, m, n, kdim, tm, tn, tk)
    nk = kdim // tk
    if reduce_b:
        grid = (m // tm, n // tn, nb, nk)
        unpack = lambda i, j, bi, k: (bi, i, j, k)
        sem = ("parallel", "parallel", "arbitrary", "arbitrary")
        nred = nb * nk
    else:
        grid = (nb, m // tm, n // tn, nk)
        unpack = lambda bi, i, j, k: (bi, i, j, k)
        sem = ("parallel", "parallel", "parallel", "arbitrary")
        nred = nk

    def a_map(*g):
        bi, i, j, k = unpack(*g)
        ab = bi if ba > 1 else 0
        return (ab, k, i) if mode == "tn" else (ab, i, k)

    def b_map(*g):
        bi, i, j, k = unpack(*g)
        if b_split:
            return (k, j, 0) if mode == "nt" else (j, k, 0)
        bq = bi if bb > 1 else 0
        return (bq, j, k) if mode == "nt" else (bq, k, j)

    def o_map(*g):
        bi, i, j, k = unpack(*g)
        if o_split:
            return (j, i, 0)
        return (0 if reduce_b else bi, i, j)

    a_blk = (1, tk, tm) if mode == "tn" else (1, tm, tk)
    b_blk = (1, tn, tk) if mode == "nt" else (1, tk, tn)
    dims = {"nn": _NN, "nt": _NT, "tn": _TN}[mode]
    has_add = add is not None

    def body(*refs):
        a_ref, b_ref = refs[:2]
        if has_add:
            add_ref = refs[2]
        if nred == 1:
            o_ref = refs[-1]
            r = _dot(a_ref[0], b_ref[0], dims)
            if has_add:
                r = r + add_ref[0].astype(F32)
            o_ref[0] = r.astype(o_ref.dtype)
            return
        o_ref, acc_ref = refs[-2:]
        if reduce_b:
            step = pl.program_id(2) * nk + pl.program_id(3)
        else:
            step = pl.program_id(3)

        @pl.when(step == 0)
        def _():
            acc_ref[...] = _dot(a_ref[0], b_ref[0], dims)

        @pl.when(step > 0)
        def _():
            acc_ref[...] += _dot(a_ref[0], b_ref[0], dims)

        @pl.when(step == nred - 1)
        def _():
            r = acc_ref[...]
            if has_add:
                r = r + add_ref[0].astype(F32)
            o_ref[0] = r.astype(o_ref.dtype)

    in_specs = [pl.BlockSpec(a_blk, a_map), pl.BlockSpec(b_blk, b_map)]
    operands = [a, b]
    if has_add:
        in_specs.append(pl.BlockSpec((1, tm, tn), o_map))
        operands.append(add)
    if dep is not None:
        in_specs.append(pl.BlockSpec(memory_space=pl.ANY))
        operands.append(dep)
    if o_split:
        out_shape = jax.ShapeDtypeStruct((n // tn, m, tn), out_dtype)
    else:
        out_shape = jax.ShapeDtypeStruct((1 if reduce_b else nb, m, n), out_dtype)
    return pl.pallas_call(
        body, grid=grid, in_specs=in_specs, out_specs=pl.BlockSpec((1, tm, tn), o_map),
        out_shape=out_shape, scratch_shapes=[] if nred == 1 else [pltpu.VMEM((tm, tn), F32)],
        compiler_params=_cp(*sem), name=name)(*operands)


def _mm_nt_split(a, b, dep, *, name, tm=512, tn=1024):
    m, k = a.shape
    s, n, w = b.shape
    assert k == s * w and w % LANES == 0
    tm, tn = _tile(m, tm), _tile(n, tn)
    has_dep = dep is not None

    def body(*refs):
        a_ref, b_ref, o_ref = refs[0], refs[1], refs[-1]
        r = _dot(a_ref[:, 0:w], b_ref[0], _NT)
        for q in range(1, s):
            r = r + _dot(a_ref[:, q * w:(q + 1) * w], b_ref[q], _NT)
        o_ref[...] = r

    return pl.pallas_call(
        body, grid=(n // tn, m // tm),
        in_specs=[pl.BlockSpec((tm, k), lambda j, i: (i, 0)), pl.BlockSpec((s, tn, w), lambda j, i: (0, j, 0))]
        + ([pl.BlockSpec(memory_space=pl.ANY)] if has_dep else []),
        out_specs=pl.BlockSpec((tm, tn), lambda j, i: (i, j)), out_shape=jax.ShapeDtypeStruct((m, n), F32),
        compiler_params=_cp("parallel", "parallel"), name=name)(a, b, *([dep] if has_dep else []))


def _mm2(a, b, mode, out_dtype, *, name, add=None, **kw):
    b3 = b if kw.get("b_split") else b[None]
    r = _mm(a[None], b3, mode, out_dtype, name=name, add=None if add is None else add[None], **kw)
    return r if kw.get("o_split") else r[0]


def _rms_fwd(x, g, *, name, tb=512):
    t, d = x.shape
    tb = min(tb, t)

    def body(x_ref, g_ref, o_ref):
        xv = x_ref[...]
        r = lax.rsqrt(jnp.mean(xv * xv, axis=-1, keepdims=True) + EPS)
        o_ref[...] = (xv * r * g_ref[...]).astype(o_ref.dtype)

    return pl.pallas_call(
        body, grid=(t // tb,),
        in_specs=[pl.BlockSpec((tb, d), lambda i: (i, 0)), pl.BlockSpec((1, d), lambda i: (0, 0))],
        out_specs=pl.BlockSpec((tb, d), lambda i: (i, 0)),
        out_shape=jax.ShapeDtypeStruct((t, d), MXU_DTYPE), compiler_params=_cp("parallel"), name=name)(x, g)


def _rms_bwd(x, g, dh, dres, *, name, tb=512):
    t, d = x.shape
    tb = min(tb, t)
    has_res = dres is not None

    def body(*refs):
        if has_res:
            x_ref, g_ref, dh_ref, dres_ref, dx_ref, dxl_ref, dg_ref = refs
        else:
            x_ref, g_ref, dh_ref, dx_ref, dxl_ref, dg_ref = refs
        xv = x_ref[...]
        r = lax.rsqrt(jnp.mean(xv * xv, axis=-1, keepdims=True) + EPS)
        xh = xv * r
        dhv = dh_ref[...].astype(F32)

        @pl.when(pl.program_id(0) == 0)
        def _():
            dg_ref[...] = jnp.zeros_like(dg_ref)

        dg_ref[...] += jnp.sum(dhv * xh, axis=0, keepdims=True)
        dxh = dhv * g_ref[...]
        dx = r * (dxh - xh * jnp.mean(dxh * xh, axis=-1, keepdims=True))
        if has_res:
            dx = dx + dres_ref[...]
        dx_ref[...] = dx
        dxl_ref[...] = dx.astype(dxl_ref.dtype)

    row = pl.BlockSpec((tb, d), lambda i: (i, 0))
    vec = pl.BlockSpec((1, d), lambda i: (0, 0))
    operands = [x, g, dh] + ([dres] if has_res else [])
    return pl.pallas_call(
        body, grid=(t // tb,), in_specs=[row, vec, row] + ([row] if has_res else []),
        out_specs=[row, row, vec],
        out_shape=[jax.ShapeDtypeStruct((t, d), F32), jax.ShapeDtypeStruct((t, d), MXU_DTYPE),
                   jax.ShapeDtypeStruct((1, d), F32)],
        compiler_params=_cp("arbitrary"), name=name)(*operands)


def _loss_head(x, g, target, *, name, tb=512):
    t, d = x.shape
    tb = min(tb, t)

    def body(x_ref, g_ref, t_ref, loss_ref, dx_ref, dxl_ref, dg_ref):
        xv = x_ref[...]
        r = lax.rsqrt(jnp.mean(xv * xv, axis=-1, keepdims=True) + EPS)
        xh = xv * r
        gv = g_ref[...]
        err = xh * gv - t_ref[...]

        @pl.when(pl.program_id(0) == 0)
        def _():
            dg_ref[...] = jnp.zeros_like(dg_ref)
            loss_ref[...] = jnp.zeros_like(loss_ref)

        row_loss = jnp.mean(err * err, axis=-1, keepdims=True)
        loss_ref[...] += 0.5 * jnp.sum(row_loss, axis=0, keepdims=True)
        dy = err * (1.0 / d)
        dg_ref[...] += jnp.sum(dy * xh, axis=0, keepdims=True)
        dxh = dy * gv
        dx = r * (dxh - xh * jnp.mean(dxh * xh, axis=-1, keepdims=True))
        dx_ref[...] = dx
        dxl_ref[...] = dx.astype(dxl_ref.dtype)

    row = pl.BlockSpec((tb, d), lambda i: (i, 0))
    vec = pl.BlockSpec((1, d), lambda i: (0, 0))
    return pl.pallas_call(
        body, grid=(t // tb,), in_specs=[row, vec, row],
        out_specs=[pl.BlockSpec((1, 1), lambda i: (0, 0)), row, row, vec],
        out_shape=[jax.ShapeDtypeStruct((1, 1), F32), jax.ShapeDtypeStruct((t, d), F32),
                   jax.ShapeDtypeStruct((t, d), MXU_DTYPE), jax.ShapeDtypeStruct((1, d), F32)],
        compiler_params=_cp("arbitrary"), name=name)(x, g, target)


def _ffn_up(h, wg, wu, *, name, tb=1024):
    t, d = h.shape
    s, f, _ = wg.shape
    tb = min(tb, t)

    def body(h_ref, wg_ref, wu_ref, g_ref, u_ref, a_ref):
        hv = h_ref[...]
        gv = _dot(hv, wg_ref[0], _NT)
        uv = _dot(hv, wu_ref[0], _NT)
        g_ref[0] = gv.astype(g_ref.dtype)
        u_ref[0] = uv.astype(u_ref.dtype)
        a_ref[0] = (gv * _sigmoid(gv) * uv).astype(a_ref.dtype)

    wspec = pl.BlockSpec((1, f, d), lambda j, i: (j, 0, 0))
    ospec = pl.BlockSpec((1, tb, f), lambda j, i: (j, i, 0))
    return pl.pallas_call(
        body, grid=(s, t // tb),
        in_specs=[pl.BlockSpec((tb, d), lambda j, i: (i, 0)), wspec, wspec],
        out_specs=[ospec, ospec, ospec],
        out_shape=[jax.ShapeDtypeStruct((s, t, f), MXU_DTYPE)] * 3,
        compiler_params=_cp("parallel", "parallel"), name=name)(h, wg, wu)


def _ffn_dact(dy, wd, gate, up, *, name, tb=1024):
    t, d = dy.shape
    s, f, _ = wd.shape
    tb = min(tb, t)

    def body(dy_ref, wd_ref, g_ref, u_ref, dg_ref, du_ref):
        da = _dot(dy_ref[...], wd_ref[0], _NT)
        gv = g_ref[0].astype(F32)
        sg = _sigmoid(gv)
        du_ref[0] = (da * gv * sg).astype(du_ref.dtype)
        dg_ref[0] = (da * u_ref[0].astype(F32) * (sg * (1.0 + gv * (1.0 - sg)))).astype(dg_ref.dtype)

    aspec = pl.BlockSpec((1, tb, f), lambda i, j: (j, i, 0))
    return pl.pallas_call(
        body, grid=(t // tb, s),
        in_specs=[pl.BlockSpec((tb, d), lambda i, j: (i, 0)),
                  pl.BlockSpec((1, f, d), lambda i, j: (j, 0, 0)), aspec, aspec],
        out_specs=[aspec, aspec],
        out_shape=[jax.ShapeDtypeStruct((s, t, f), MXU_DTYPE), jax.ShapeDtypeStruct((s, t, f), MXU_DTYPE)],
        compiler_params=_cp("parallel", "parallel"), name=name)(dy, wd, gate, up)


def _ffn_down(act, wd, x, *, name, tm=512, tn=1024):
    s, t, f = act.shape
    d = wd.shape[2]
    tm, tn = _tile(t, tm), _tile(d, tn)

    def body(a_ref, w_ref, x_ref, o_ref):
        r = x_ref[...]
        for j in range(s):
            r = r + _dot(a_ref[j], w_ref[j], _NN)
        o_ref[...] = r

    xspec = pl.BlockSpec((tm, tn), lambda k, i: (i, k))
    return pl.pallas_call(
        body, grid=(d // tn, t // tm),
        in_specs=[pl.BlockSpec((s, tm, f), lambda k, i: (0, i, 0)), pl.BlockSpec((s, f, tn), lambda k, i: (0, 0, k)),
                  xspec],
        out_specs=xspec, out_shape=jax.ShapeDtypeStruct((t, d), F32),
        compiler_params=_cp("parallel", "parallel"), name=name)(act, wd, x)


def _ffn_dh(dgate, dup, wg, wu, dep, *, name, tm=512, tn=1024, sg=4):
    s, t, f = dgate.shape
    d = wg.shape[2]
    tm, tn = _tile(t, tm), _tile(d, tn)
    steps = s // sg
    has_dep = dep is not None

    def body(*refs):
        dg_ref, du_ref, wg_ref, wu_ref = refs[:4]
        o_ref, acc_ref = refs[-2:]
        j = pl.program_id(2)
        part = _dot(dg_ref[0], wg_ref[0], _NN) + _dot(du_ref[0], wu_ref[0], _NN)
        for q in range(1, sg):
            part = part + _dot(dg_ref[q], wg_ref[q], _NN) + _dot(du_ref[q], wu_ref[q], _NN)

        @pl.when(j == 0)
        def _():
            acc_ref[...] = part

        @pl.when(j > 0)
        def _():
            acc_ref[...] += part

        @pl.when(j == steps - 1)
        def _():
            o_ref[...] = acc_ref[...]

    aspec = pl.BlockSpec((sg, tm, f), lambda i, k, j: (j, i, 0))
    wspec = pl.BlockSpec((sg, f, tn), lambda i, k, j: (j, 0, k))
    return pl.pallas_call(
        body, grid=(t // tm, d // tn, steps),
        in_specs=[aspec, aspec, wspec, wspec] + ([pl.BlockSpec(memory_space=pl.ANY)] if has_dep else []),
        out_specs=pl.BlockSpec((tm, tn), lambda i, k, j: (i, k)),
        out_shape=jax.ShapeDtypeStruct((t, d), F32), scratch_shapes=[pltpu.VMEM((tm, tn), F32)],
        compiler_params=_cp("parallel", "parallel", "arbitrary"),
        name=name)(dgate, dup, wg, wu, *([dep] if has_dep else []))


def _xattn_fwd(q, kv, *, name, tb=512):
    t, d = q.shape
    m = kv.shape[0]
    hd = d // XA_HEADS
    tb = min(tb, t)
    scale = hd ** -0.5

    def body(q_ref, kv_ref, o_ref):
        for hh in range(XA_HEADS):
            cs = slice(hh * hd, (hh + 1) * hd)
            s = _dot(q_ref[:, cs], kv_ref[:, cs], _NT) * scale
            s = s - jnp.max(s, axis=-1, keepdims=True)
            e = jnp.exp(s)
            p = e / jnp.sum(e, axis=-1, keepdims=True)
            o_ref[:, cs] = _dot(p, kv_ref[:, d + hh * hd:d + (hh + 1) * hd], _NN).astype(o_ref.dtype)

    return pl.pallas_call(
        body, grid=(t // tb,),
        in_specs=[pl.BlockSpec((tb, d), lambda i: (i, 0)), pl.BlockSpec((m, 2 * d), lambda i: (0, 0))],
        out_specs=pl.BlockSpec((tb, d), lambda i: (i, 0)),
        out_shape=jax.ShapeDtypeStruct((t, d), MXU_DTYPE), compiler_params=_cp("parallel"), name=name)(q, kv)


def _xattn_bwd(q, kv, do, *, name, tb=512):
    t, d = q.shape
    m = kv.shape[0]
    hd = d // XA_HEADS
    tb = min(tb, t)
    scale = hd ** -0.5

    def body(q_ref, kv_ref, do_ref, dq_ref, dkv_ref):
        @pl.when(pl.program_id(0) == 0)
        def _():
            dkv_ref[...] = jnp.zeros_like(dkv_ref)

        for hh in range(XA_HEADS):
            cs = slice(hh * hd, (hh + 1) * hd)
            vs = slice(d + hh * hd, d + (hh + 1) * hd)
            qv, kk, vv, dov = q_ref[:, cs], kv_ref[:, cs], kv_ref[:, vs], do_ref[:, cs]
            s = _dot(qv, kk, _NT) * scale
            s = s - jnp.max(s, axis=-1, keepdims=True)
            e = jnp.exp(s)
            p = e / jnp.sum(e, axis=-1, keepdims=True)
            dkv_ref[:, vs] += _dot(p, dov, _TN)
            dp = _dot(dov, vv, _NT)
            ds = p * (dp - jnp.sum(p * dp, axis=-1, keepdims=True)) * scale
            dq_ref[:, cs] = _dot(ds, kk, _NN).astype(dq_ref.dtype)
            dkv_ref[:, cs] += _dot(ds, qv, _TN)

    row = pl.BlockSpec((tb, d), lambda i: (i, 0))
    full = pl.BlockSpec((m, 2 * d), lambda i: (0, 0))
    return pl.pallas_call(
        body, grid=(t // tb,), in_specs=[row, full, row], out_specs=[row, full],
        out_shape=[jax.ShapeDtypeStruct((t, d), MXU_DTYPE), jax.ShapeDtypeStruct((m, 2 * d), F32)],
        compiler_params=_cp("arbitrary"), name=name)(q, kv, do)


def _pool_window_stats(u, gi, reverse):
    t = u.shape[0]
    row = lax.broadcasted_iota(jnp.int32, u.shape, 0)
    s = u
    for j in range(POOL_GROUPS):
        sh = 1 << j
        if reverse:
            rolled = jnp.where(row < t - sh, pltpu.roll(s, t - sh, axis=0), 0.0)
        else:
            rolled = jnp.where(row >= sh, pltpu.roll(s, sh, axis=0), 0.0)
        s = jnp.where(j <= gi, s + rolled, s)
    return s, row


def _pool_fwd(z, w_pool, scale, *, name):
    t = z.shape[0]
    g_cnt, c, _ = w_pool.shape

    def body(z_ref, w_ref, s_ref, o_ref):
        gi = pl.program_id(0)
        u = z_ref[...]
        win, row = _pool_window_stats(u, gi, False)
        cnt = jnp.minimum(row + 1, lax.shift_left(jnp.int32(2), gi)).astype(F32)
        p = win / cnt - u
        o_ref[...] = (_dot(p, w_ref[0], _NN) * s_ref[...]).astype(o_ref.dtype)

    return pl.pallas_call(
        body, grid=(g_cnt,),
        in_specs=[pl.BlockSpec((t, c), lambda g: (0, g)), pl.BlockSpec((1, c, c), lambda g: (g, 0, 0)),
                  pl.BlockSpec((1, c), lambda g: (0, g))],
        out_specs=pl.BlockSpec((t, c), lambda g: (0, g)),
        out_shape=jax.ShapeDtypeStruct((t, 2 * g_cnt * c), MXU_DTYPE),
        compiler_params=_cp("parallel"), name=name)(z, w_pool, scale)


def _pool_bwd(z, w_pool, scale, dycat, *, name):
    t = z.shape[0]
    g_cnt, c, _ = w_pool.shape

    def body(z_ref, w_ref, s_ref, dy_ref, du_ref, dw_ref, ds_ref):
        gi = pl.program_id(0)
        u = z_ref[...]
        win, row = _pool_window_stats(u, gi, False)
        cnt = jnp.minimum(row + 1, lax.shift_left(jnp.int32(2), gi)).astype(F32)
        p = win / cnt - u
        y = _dot(p, w_ref[0], _NN)
        dya = dy_ref[...].astype(F32)
        ds_ref[...] = jnp.sum(dya * y, axis=0, keepdims=True)
        dy = dya * s_ref[...]
        dw_ref[0] = _dot(p, dy, _TN)
        dp = _dot(dy, w_ref[0], _NT)
        back, _ = _pool_window_stats(dp / cnt, gi, True)
        du_ref[...] = (back - dp).astype(du_ref.dtype)

    col = pl.BlockSpec((t, c), lambda g: (0, g))
    return pl.pallas_call(
        body, grid=(g_cnt,),
        in_specs=[col, pl.BlockSpec((1, c, c), lambda g: (g, 0, 0)), pl.BlockSpec((1, c), lambda g: (0, g)), col],
        out_specs=[col, pl.BlockSpec((1, c, c), lambda g: (g, 0, 0)), pl.BlockSpec((1, c), lambda g: (0, g))],
        out_shape=[jax.ShapeDtypeStruct((t, g_cnt * c), MXU_DTYPE), jax.ShapeDtypeStruct((g_cnt, c, c), F32),
                   jax.ShapeDtypeStruct((1, g_cnt * c), F32)],
        compiler_params=_cp("parallel"), name=name)(z, w_pool, scale, dycat)


def _chunk_tri(lower):
    r = lax.broadcasted_iota(jnp.int32, (LANES, LANES), 0)
    c = lax.broadcasted_iota(jnp.int32, (LANES, LANES), 1)
    same = (r // HG_CHUNK) == (c // HG_CHUNK)
    return jnp.where(same & ((c <= r) if lower else (c >= r)), 1.0, 0.0).astype(F32)


def _hgrn_prepare(q_ref, f_ref, lb_ref, qh_s, k_s, b_s, qt_s, kt_s, gl_s):
    tb = q_ref.shape[0]
    lb = lb_ref[...]
    sg = _sigmoid(f_ref[...])
    f = lb + (1.0 - lb) * sg
    logf = jnp.log(f)
    qv = q_ref[...]
    qh = qv * _sigmoid(qv) * (HG_HEAD ** -0.5)
    tri = _chunk_tri(True)
    for r in range(tb // LANES):
        rows = slice(r * LANES, (r + 1) * LANES)
        b_s[rows, :] = _dot_f32(tri, logf[rows, :], _NN)
    b = b_s[...]
    b3 = b.reshape(tb // HG_CHUNK, HG_CHUNK, HG_HEAD)
    bl = b3[:, HG_CHUNK - 1:HG_CHUNK, :]
    k = 1.0 - f
    qh_s[...] = qh
    k_s[...] = k
    qt_s[...] = qh * jnp.exp(b)
    kt_s[...] = k * jnp.exp(bl - b3).reshape(tb, HG_HEAD)
    gl_s[...] = jnp.exp(jnp.broadcast_to(bl, b3.shape)).reshape(tb, HG_HEAD)
    return sg, f


SUBLANES = 8
HG_GROUPS = HG_CHUNK // SUBLANES


def _hgrn_intra(qh, kk, bq, rows_a, rows_b):
    ones = jnp.ones((HG_HEAD, HG_HEAD), MXU_DTYPE)
    es, stack_a, stack_b, starts, at = [], [], [], [], 0
    for s in range(HG_CHUNK):
        lo = (s // SUBLANES) * SUBLANES
        e = jnp.exp(jnp.minimum(bq[lo:, :] - bq[s:s + 1, :], 0.0))
        es.append(e)
        stack_a.append(qh[lo:, :] * e * kk[s:s + 1, :])
        if rows_a is not None:
            stack_b.append(rows_a[lo:, :] * rows_b[s:s + 1, :])
        starts.append(at)
        at += HG_CHUNK - lo
    a_rep = _dot(jnp.concatenate(stack_a, axis=0), ones, _NN)
    d_rep = _dot(jnp.concatenate(stack_b, axis=0), ones, _NN) if rows_a is not None else None
    return es, a_rep, d_rep, starts


def _groups(v):
    return [v[g * SUBLANES:(g + 1) * SUBLANES, :] for g in range(HG_GROUPS)]


def _hgrn_fwd(z, lb, hg_norm, ycat, *, name, tb=512):
    t = z.shape[0]
    mix_b = lb.shape[1]
    heads = mix_b // HG_HEAD
    off = (z.shape[1] - 4 * mix_b) // HG_HEAD
    tb = min(tb, t)
    ncb = tb // HG_CHUNK

    def body(q_ref, f_ref, i_ref, g_ref, lb_ref, hn_ref, ycat_in, y_ref, o_ref, st_ref,
             state, qh_s, k_s, b_s, qt_s, kt_s, gl_s, o_s):
        del ycat_in

        @pl.when(pl.program_id(1) == 0)
        def _():
            state[...] = jnp.zeros_like(state)

        _hgrn_prepare(q_ref, f_ref, lb_ref, qh_s, k_s, b_s, qt_s, kt_s, gl_s)
        row = lax.broadcasted_iota(jnp.int32, (SUBLANES, HG_HEAD), 0)

        def chunk(c, carry):
            rows = pl.ds(pl.multiple_of(c * HG_CHUNK, HG_CHUNK), HG_CHUNK)
            st = state[...]
            st_ref[0, c] = st
            vv = i_ref[rows, :]
            o = _groups(_dot(qt_s[rows, :], st, _NT))
            _, a_rep, _, starts = _hgrn_intra(qh_s[rows, :], k_s[rows, :], b_s[rows, :], None, None)
            for s in range(HG_CHUNK):
                g0 = s // SUBLANES
                for g in range(g0, HG_GROUPS):
                    at = starts[s] + (g - g0) * SUBLANES
                    piece = a_rep[at:at + SUBLANES, :] * vv[s:s + 1, :]
                    o[g] = o[g] + (jnp.where(row >= s - g0 * SUBLANES, piece, 0.0) if g == g0 else piece)
            o_s[rows, :] = jnp.concatenate(o, axis=0)
            state[...] = st * gl_s[rows, :][0:1, :] + _dot(vv, kt_s[rows, :], _TN)
            return carry

        lax.fori_loop(0, ncb, chunk, 0, unroll=2)
        o = o_s[...]
        o_ref[...] = o
        r = lax.rsqrt(jnp.mean(o * o, axis=-1, keepdims=True) + EPS)
        gv = g_ref[...]
        y_ref[...] = (o * r * hn_ref[...] * (gv * _sigmoid(gv))).astype(y_ref.dtype)

    def zcol(kind):
        return pl.BlockSpec((tb, HG_HEAD), lambda h, i: (i, off + kind * heads + h))

    scratch = [pltpu.VMEM((HG_HEAD, HG_HEAD), F32)] + [pltpu.VMEM((tb, HG_HEAD), F32)] * 7
    return pl.pallas_call(
        body, grid=(heads, t // tb),
        in_specs=[zcol(0), zcol(1), zcol(2), zcol(3), pl.BlockSpec((1, HG_HEAD), lambda h, i: (0, h)),
                  pl.BlockSpec((1, HG_HEAD), lambda h, i: (0, 0)), pl.BlockSpec(memory_space=pl.ANY)],
        out_specs=[pl.BlockSpec((tb, HG_HEAD), lambda h, i: (i, heads + h)),
                   pl.BlockSpec((tb, HG_HEAD), lambda h, i: (i, h)),
                   pl.BlockSpec((1, ncb, HG_HEAD, HG_HEAD), lambda h, i: (h, i, 0, 0))],
        out_shape=[jax.ShapeDtypeStruct(ycat.shape, ycat.dtype), jax.ShapeDtypeStruct((t, mix_b), F32),
                   jax.ShapeDtypeStruct((heads, t // HG_CHUNK, HG_HEAD, HG_HEAD), F32)],
        scratch_shapes=scratch, input_output_aliases={6: 0},
        compiler_params=_cp("parallel", "arbitrary"), name=name)(z, z, z, z, lb, hg_norm, ycat)


def _hgrn_bwd(z, lb, hg_norm, o_raw, states, dycat, *, name, tb=512):
    t = z.shape[0]
    mix_b = lb.shape[1]
    heads = mix_b // HG_HEAD
    off = (z.shape[1] - 4 * mix_b) // HG_HEAD
    tb = min(tb, t)
    ncb = tb // HG_CHUNK
    nt = t // tb

    def body(q_ref, f_ref, i_ref, g_ref, lb_ref, hn_ref, o_ref, st_ref, dy_ref,
             dq_ref, dfl_ref, di_ref, dg_ref, dlb_ref, dhn_ref,
             dstate, qh_s, k_s, b_s, qt_s, kt_s, gl_s, do_s, dqh_s, dk_s, db_s):
        first = pl.program_id(1) == 0

        @pl.when(first)
        def _():
            dstate[...] = jnp.zeros_like(dstate)
            dlb_ref[...] = jnp.zeros_like(dlb_ref)

        @pl.when(first & (pl.program_id(0) == 0))
        def _():
            dhn_ref[...] = jnp.zeros_like(dhn_ref)

        sg, f = _hgrn_prepare(q_ref, f_ref, lb_ref, qh_s, k_s, b_s, qt_s, kt_s, gl_s)
        o = o_ref[...]
        r = lax.rsqrt(jnp.mean(o * o, axis=-1, keepdims=True) + EPS)
        oh = o * r
        gv = g_ref[...]
        sgg = _sigmoid(gv)
        dy = dy_ref[...].astype(F32)
        hn = hn_ref[...]
        dg_ref[...] = (dy * oh * hn * (sgg * (1.0 + gv * (1.0 - sgg)))).astype(dg_ref.dtype)
        don = dy * (gv * sgg)
        dhn_ref[...] += jnp.sum(don * oh, axis=0, keepdims=True)
        doh = don * hn
        do_s[...] = r * (doh - oh * jnp.mean(doh * oh, axis=-1, keepdims=True))
        row = lax.broadcasted_iota(jnp.int32, (SUBLANES, HG_HEAD), 0)

        def chunk(ci, carry):
            c = ncb - 1 - ci
            rows = pl.ds(pl.multiple_of(c * HG_CHUNK, HG_CHUNK), HG_CHUNK)
            st_prev = st_ref[0, c]
            dst = dstate[...]
            qh, kk, bq, vv = qh_s[rows, :], k_s[rows, :], b_s[rows, :], i_ref[rows, :]
            qt, kt, doo = qt_s[rows, :], kt_s[rows, :], do_s[rows, :]
            gl = gl_s[rows, :][0:1, :]
            es, a_rep, d_rep, starts = _hgrn_intra(qh, kk, bq, doo, vv)
            dqh = _groups(jnp.exp(bq) * _dot(doo, st_prev, _NN))
            dk = _groups(jnp.exp(bq[HG_CHUNK - 1:HG_CHUNK, :] - bq) * _dot(vv, dst, _NN))
            dv = _groups(_dot(kt, dst, _NT))
            qh_g, do_g = _groups(qh), _groups(doo)
            for s in range(HG_CHUNK):
                g0 = s // SUBLANES
                local = s - g0 * SUBLANES
                dk_acc = dv_acc = None
                for g in range(g0, HG_GROUPS):
                    at = (g - g0) * SUBLANES
                    wgt = d_rep[starts[s] + at:starts[s] + at + SUBLANES, :] * es[s][at:at + SUBLANES, :]
                    avo = a_rep[starts[s] + at:starts[s] + at + SUBLANES, :] * do_g[g]
                    if g == g0:
                        wgt = jnp.where(row >= local, wgt, 0.0)
                        avo = jnp.where(row >= local, avo, 0.0)
                    dqh[g] = dqh[g] + wgt * kk[s:s + 1, :]
                    dk_acc = wgt * qh_g[g] if dk_acc is None else dk_acc + wgt * qh_g[g]
                    dv_acc = avo if dv_acc is None else dv_acc + avo
                dk[g0] = dk[g0] + jnp.where(row == local, jnp.sum(dk_acc, axis=0, keepdims=True), 0.0)
                dv[g0] = dv[g0] + jnp.where(row == local, jnp.sum(dv_acc, axis=0, keepdims=True), 0.0)
            dqh, dk, dv = [jnp.concatenate(p, axis=0) for p in (dqh, dk, dv)]
            row_c = lax.broadcasted_iota(jnp.int32, (HG_CHUNK, HG_HEAD), 0)
            st_next = st_prev * gl + _dot(vv, kt, _TN)
            db = qh * dqh - kk * dk
            db = db + jnp.where(row_c == HG_CHUNK - 1, jnp.sum(st_next * dst, axis=0, keepdims=True), 0.0)
            dstate[...] = dst * gl + _dot(doo, qt, _TN)
            dqh_s[rows, :] = dqh
            dk_s[rows, :] = dk
            db_s[rows, :] = db
            di_ref[rows, :] = dv.astype(di_ref.dtype)
            return carry

        lax.fori_loop(0, ncb, chunk, 0, unroll=2)
        tri = _chunk_tri(False)
        lb_v = lb_ref[...]
        qv = q_ref[...]
        sgq = _sigmoid(qv)
        dq_ref[...] = (dqh_s[...] * (HG_HEAD ** -0.5) * (sgq * (1.0 + qv * (1.0 - sgq)))).astype(dq_ref.dtype)
        dlb = jnp.zeros((1, HG_HEAD), F32)
        for rr in range(tb // LANES):
            rws = slice(rr * LANES, (rr + 1) * LANES)
            dlogf = _dot_f32(tri, db_s[rws, :], _NN)
            df = dlogf / f[rws, :] - dk_s[rws, :]
            sgr = sg[rws, :]
            dfl_ref[rws, :] = (df * (1.0 - lb_v) * sgr * (1.0 - sgr)).astype(dfl_ref.dtype)
            dlb = dlb + jnp.sum(df * (1.0 - sgr), axis=0, keepdims=True)
        dlb_ref[...] += dlb

    def zcol(kind):
        return pl.BlockSpec((tb, HG_HEAD), lambda h, i: (nt - 1 - i, off + kind * heads + h))

    hcol = pl.BlockSpec((tb, HG_HEAD), lambda h, i: (nt - 1 - i, h))
    scratch = [pltpu.VMEM((HG_HEAD, HG_HEAD), F32)] + [pltpu.VMEM((tb, HG_HEAD), F32)] * 10
    out = jax.ShapeDtypeStruct((t, mix_b), MXU_DTYPE)
    return pl.pallas_call(
        body, grid=(heads, nt),
        in_specs=[zcol(0), zcol(1), zcol(2), zcol(3), pl.BlockSpec((1, HG_HEAD), lambda h, i: (0, h)),
                  pl.BlockSpec((1, HG_HEAD), lambda h, i: (0, 0)), hcol,
                  pl.BlockSpec((1, ncb, HG_HEAD, HG_HEAD), lambda h, i: (h, nt - 1 - i, 0, 0)),
                  pl.BlockSpec((tb, HG_HEAD), lambda h, i: (nt - 1 - i, heads + h))],
        out_specs=[hcol, hcol, hcol, hcol, pl.BlockSpec((1, HG_HEAD), lambda h, i: (0, h)),
                   pl.BlockSpec((1, HG_HEAD), lambda h, i: (0, 0))],
        out_shape=[out, out, out, out, jax.ShapeDtypeStruct((1, mix_b), F32),
                   jax.ShapeDtypeStruct((1, HG_HEAD), F32)],
        scratch_shapes=scratch, compiler_params=_cp("arbitrary", "arbitrary"),
        name=name)(z, z, z, z, lb, hg_norm, o_raw, states, dycat)


def _lb_fwd(lb_table, layer, *, name):
    rows, width = lb_table.shape

    def body(t_ref, o_ref):
        tv = t_ref[...]
        e = jnp.exp(tv - jnp.max(tv, axis=0, keepdims=True))
        sm = e / jnp.sum(e, axis=0, keepdims=True)
        o_ref[...] = jnp.sum(sm[1:layer + 2, :], axis=0, keepdims=True)

    return pl.pallas_call(body, out_shape=jax.ShapeDtypeStruct((1, width), F32), name=name)(lb_table)


def _lb_bwd(lb_table, dlb, layer, *, name):
    rows, width = lb_table.shape

    def body(t_ref, d_ref, o_ref):
        tv = t_ref[...]
        e = jnp.exp(tv - jnp.max(tv, axis=0, keepdims=True))
        sm = e / jnp.sum(e, axis=0, keepdims=True)
        ridx = lax.broadcasted_iota(jnp.int32, sm.shape, 0)
        dsm = jnp.where((ridx >= 1) & (ridx <= layer + 1), d_ref[...], 0.0)
        o_ref[...] = sm * (dsm - jnp.sum(sm * dsm, axis=0, keepdims=True))

    return pl.pallas_call(body, out_shape=jax.ShapeDtypeStruct((rows, width), F32), name=name)(lb_table, dlb)


FOX_BLOCK = 1024


def _fox_prep(zf, b_f, *, name, blk=256):
    t = zf.shape[0]

    def body(z_ref, b_ref, fc_ref):
        r = lax.broadcasted_iota(jnp.int32, (blk, blk), 0)
        c = lax.broadcasted_iota(jnp.int32, (blk, blk), 1)
        tri = jnp.where(c <= r, 1.0, 0.0).astype(F32)
        carry = jnp.zeros((1, LANES), F32)
        for j in range(t // blk):
            rows = slice(j * blk, (j + 1) * blk)
            ls = jax.nn.log_sigmoid(z_ref[rows, :] + b_ref[...])
            fb = _dot_f32(tri, ls, _NN) + carry
            carry = fb[blk - 1:blk, :]
            fc_ref[rows, :] = fb

    return pl.pallas_call(
        body, out_shape=jax.ShapeDtypeStruct((t, LANES), F32),
        compiler_params=pltpu.CompilerParams(vmem_limit_bytes=VMEM_LIMIT_BYTES), name=name)(zf, b_f)


def _fox_head_column(fc_ref, fk_s, head):
    lane = lax.broadcasted_iota(jnp.int32, fc_ref.shape, 1)
    fk_s[...] = jnp.sum(jnp.where(lane == head, fc_ref[...], 0.0), axis=1, keepdims=True)


def _fox_scores(k_blk, q_blk, fk_blk, diagonal):
    s = _dot(k_blk, q_blk, _NT) * (FOX_HEAD ** -0.5) - fk_blk
    if diagonal:
        key = lax.broadcasted_iota(jnp.int32, s.shape, 0)
        qry = lax.broadcasted_iota(jnp.int32, s.shape, 1)
        s = jnp.where(key <= qry, s, -jnp.inf)
    return s


def _fox_fwd(zqkv, fcol, *, name):
    t = zqkv.shape[0]
    d = zqkv.shape[1] // 3
    heads = d // FOX_HEAD
    blk = min(FOX_BLOCK, t)
    nq = t // blk

    def body(q_ref, k_ref, v_ref, fc_ref, o_ref, lse_ref, fk_s):
        _fox_head_column(fc_ref, fk_s, pl.program_id(0))

        def q_block(qi, carry):
            qrows = pl.ds(pl.multiple_of(qi * blk, blk), blk)
            q_blk = q_ref[qrows, :]

            def update(st, krows, diagonal):
                m, l, acc = st
                s = _fox_scores(k_ref[krows, :], q_blk, fk_s[krows, :], diagonal)
                m_new = jnp.maximum(m, jnp.max(s, axis=0, keepdims=True))
                alpha = jnp.exp(m - m_new)
                p = jnp.exp(s - m_new)
                l = alpha * l + jnp.sum(p, axis=0, keepdims=True)
                acc = acc * alpha + _dot(v_ref[krows, :], p, _TN)
                return m_new, l, acc

            def k_block(kj, st):
                return update(st, pl.ds(pl.multiple_of(kj * blk, blk), blk), False)

            init = (jnp.full((1, blk), -jnp.inf, F32), jnp.zeros((1, blk), F32),
                    jnp.zeros((FOX_HEAD, blk), F32))
            m, l, acc = update(lax.fori_loop(0, qi, k_block, init), qrows, True)
            o_ref[qrows, :] = (acc / l).T.astype(o_ref.dtype)
            lse_ref[0, :, qrows] = m + jnp.log(l)
            return carry

        lax.fori_loop(0, nq, q_block, 0)

    def col(kind):
        return pl.BlockSpec((t, FOX_HEAD), lambda h: (0, kind * heads + h))

    rowvec = pl.BlockSpec((1, 1, t), lambda h: (h, 0, 0))
    return pl.pallas_call(
        body, grid=(heads,),
        in_specs=[col(0), col(1), col(2), pl.BlockSpec((t, LANES), lambda h: (0, 0))],
        out_specs=[pl.BlockSpec((t, FOX_HEAD), lambda h: (0, h)), rowvec],
        out_shape=[jax.ShapeDtypeStruct((t, d), MXU_DTYPE), jax.ShapeDtypeStruct((heads, 1, t), F32)],
        scratch_shapes=[pltpu.VMEM((t, 1), F32)],
        compiler_params=_cp("parallel"), name=name)(zqkv, zqkv, zqkv, fcol)


def _fox_bwd(zqkv, fcol, lse, o, do, *, name):
    t = zqkv.shape[0]
    d = zqkv.shape[1] // 3
    heads = d // FOX_HEAD
    blk = min(FOX_BLOCK, t)
    nq = t // blk
    scale = FOX_HEAD ** -0.5

    def body(q_ref, k_ref, v_ref, fc_ref, lse_ref, o_ref, do_ref,
             dq_ref, dk_ref, dv_ref, rq_ref, rk_ref, dq_s, drow_s, fk_s, acc_s, rk_s):
        _fox_head_column(fc_ref, fk_s, pl.program_id(0))
        dq_s[...] = jnp.zeros_like(dq_s)
        rq_ref[...] = jnp.zeros_like(rq_ref)
        ones_f = jnp.ones((8, FOX_HEAD), F32)
        for j in range(nq):
            rows = slice(j * blk, (j + 1) * blk)
            prod = do_ref[rows, :].astype(F32) * o_ref[rows, :].astype(F32)
            drow_s[:, rows] = _dot_f32(ones_f, prod, _NT)

        def k_block(kj, carry):
            krows = pl.ds(pl.multiple_of(kj * blk, blk), blk)
            k_blk, v_blk, fk_blk = k_ref[krows, :], v_ref[krows, :], fk_s[krows, :]

            def pair(qrows, diagonal):
                q_blk, do_blk = q_ref[qrows, :], do_ref[qrows, :]
                s = _fox_scores(k_blk, q_blk, fk_blk, diagonal)
                p = jnp.exp(s - lse_ref[0, :, qrows])
                acc_s[1] += _dot(p, do_blk, _NN)
                dp = _dot(v_blk, do_blk, _NT)
                ds = (p * (dp - drow_s[0:1, qrows])).astype(MXU_DTYPE)
                acc_s[0] += _dot(ds, q_blk, _NN)
                dq_s[qrows, :] += _dot(ds, k_blk, _TN)
                ds_f = ds.astype(F32)
                rq_ref[0, :, qrows] += jnp.sum(ds_f, axis=0, keepdims=True)
                rk_s[...] += jnp.sum(ds_f, axis=1, keepdims=True)

            def q_block(qi, carry2):
                pair(pl.ds(pl.multiple_of(qi * blk, blk), blk), False)
                return carry2

            acc_s[...] = jnp.zeros_like(acc_s)
            rk_s[...] = jnp.zeros_like(rk_s)
            pair(krows, True)
            lax.fori_loop(kj + 1, nq, q_block, 0)
            dk_ref[krows, :] = (acc_s[0] * scale).astype(dk_ref.dtype)
            dv_ref[krows, :] = acc_s[1].astype(dv_ref.dtype)
            rk_ref[0, :, krows] = jnp.broadcast_to(rk_s[...], (blk, FOX_HEAD)).T[0:1, :]
            return carry

        lax.fori_loop(0, nq, k_block, 0)
        dq_ref[...] = (dq_s[...] * scale).astype(dq_ref.dtype)

    def col(kind):
        return pl.BlockSpec((t, FOX_HEAD), lambda h: (0, kind * heads + h))

    hcol = pl.BlockSpec((t, FOX_HEAD), lambda h: (0, h))
    rowvec = pl.BlockSpec((1, 1, t), lambda h: (h, 0, 0))
    out = jax.ShapeDtypeStruct((t, d), MXU_DTYPE)
    vec = jax.ShapeDtypeStruct((heads, 1, t), F32)
    return pl.pallas_call(
        body, grid=(heads,),
        in_specs=[col(0), col(1), col(2), pl.BlockSpec((t, LANES), lambda h: (0, 0)), rowvec, hcol, hcol],
        out_specs=[hcol, hcol, hcol, rowvec, rowvec],
        out_shape=[out, out, out, vec, vec],
        scratch_shapes=[pltpu.VMEM((t, FOX_HEAD), F32), pltpu.VMEM((8, t), F32), pltpu.VMEM((t, 1), F32),
                        pltpu.VMEM((2, blk, FOX_HEAD), F32), pltpu.VMEM((blk, 1), F32)],
        compiler_params=_cp("parallel"), name=name)(zqkv, zqkv, zqkv, fcol, lse, o, do)


def _fox_gate_bwd(rq, rk, zf, b_f, *, name, blk=256):
    heads, t = rq.shape

    def body(rq_ref, rk_ref, z_ref, b_ref, dfl_ref, db_ref):
        r = lax.broadcasted_iota(jnp.int32, (blk, blk), 0)
        c = lax.broadcasted_iota(jnp.int32, (blk, blk), 1)
        tri = jnp.where(r >= c, 1.0, 0.0).astype(F32)
        carry = jnp.zeros((heads, 1), F32)
        db = jnp.zeros((1, LANES), F32)
        pad = jnp.zeros((LANES - heads, blk), F32)
        for j in reversed(range(t // blk)):
            cols = slice(j * blk, (j + 1) * blk)
            df = rq_ref[:, cols] - rk_ref[:, cols]
            dls = _dot_f32(df, tri, _NN) + carry
            carry = dls[:, 0:1]
            dls_t = jnp.concatenate([dls, pad], axis=0).T
            dfl = dls_t * _sigmoid(-(z_ref[cols, :] + b_ref[...]))
            dfl_ref[cols, :] = dfl.astype(dfl_ref.dtype)
            db = db + jnp.sum(dfl, axis=0, keepdims=True)
        db_ref[...] = db

    return pl.pallas_call(
        body, out_shape=[jax.ShapeDtypeStruct((t, LANES), MXU_DTYPE), jax.ShapeDtypeStruct((1, LANES), F32)],
        compiler_params=pltpu.CompilerParams(vmem_limit_bytes=VMEM_LIMIT_BYTES), name=name)(rq, rk, zf, b_f)


def _adamw(w, m, v, parts, *, name, layer=None, prev=None, tr=256):
    lcnt, r, c = w.shape
    p = parts.shape[0]
    li = 0 if layer is None else layer
    tr = _tile(r, tr, 16)
    tc = c if tr * c <= 256 * 2048 else _tile(c, 256)
    has_prev = prev is not None

    def body(*refs):
        w_ref, m_ref, v_ref, p_ref = refs[:4]
        g_ref, d_ref, nm_ref, nv_ref = refs[-4:]
        g = p_ref[0].astype(F32)
        for j in range(1, p):
            g = g + p_ref[j].astype(F32)
        wv = w_ref[0]
        mn = ADAM_B1 * m_ref[0] + (1.0 - ADAM_B1) * g
        vn = ADAM_B2 * v_ref[0] + (1.0 - ADAM_B2) * (g * g)
        m_hat = mn / (1.0 - ADAM_B1 ** ADAM_STEP)
        v_hat = vn / (1.0 - ADAM_B2 ** ADAM_STEP)
        g_ref[0] = g
        d_ref[0] = -ADAM_LR * (m_hat / (jnp.sqrt(v_hat) + ADAM_EPS) + ADAM_WD * wv)
        nm_ref[0] = mn
        nv_ref[0] = vn

    slab = pl.BlockSpec((1, tr, tc), lambda i, j: (li, i, j))
    in_specs = [slab, slab, slab, pl.BlockSpec((p, tr, tc), lambda i, j: (0, i, j))]
    operands = [w, m, v, parts]
    aliases = {}
    if has_prev:
        in_specs += [pl.BlockSpec(memory_space=pl.ANY)] * 4
        operands += list(prev)
        aliases = {4: 0, 5: 1, 6: 2, 7: 3}
    shp = jax.ShapeDtypeStruct((lcnt, r, c), F32)
    return pl.pallas_call(
        body, grid=(r // tr, c // tc), in_specs=in_specs, out_specs=[slab] * 4, out_shape=[shp] * 4,
        input_output_aliases=aliases, compiler_params=_cp("parallel", "parallel"), name=name)(*operands)


def _my_place():
    return lax.axis_index("x"), lax.axis_index("y"), lax.axis_index("c")


def _slot(p):
    return 4 * p[0] + 2 * p[1] + p[2]


def _peer(me, mask):
    x, y, c = me
    return (1 - x if mask & 4 else x, 1 - y if mask & 2 else y, 1 - c if mask & 1 else c)


_HBM = pl.BlockSpec(memory_space=pltpu.HBM)
_SEM = pl.BlockSpec(memory_space=pltpu.SEMAPHORE)
_ANY = pl.BlockSpec(memory_space=pl.ANY)
_EFFECT = pltpu.SideEffectType.DATAFLOW_SIDE_EFFECTING


def _push_copy(src_refs, land_refs, send_sems, recv_sems, a, mask, me, per_peer, outgoing):
    peer = _peer(me, mask)
    src = src_refs[a].at[_slot(peer)] if per_peer else src_refs[a]
    dst = land_refs[a].at[_slot(me) if outgoing else _slot(peer)]
    k = a * (N_DEV - 1) + mask - 1
    return pltpu.make_async_remote_copy(
        src_ref=src, dst_ref=dst, send_sem=send_sems.at[k], recv_sem=recv_sems.at[k],
        device_id=peer, device_id_type=MESH)


ALL_PEERS = tuple(range(1, N_DEV))
CHIP_PEERS = (2, 4, 6)
FIRST_HOP = (1,) + CHIP_PEERS


def _push_start(srcs, dep, *, per_peer, name, masks=ALL_PEERS):
    n = len(srcs)
    mine = _slot(_my_place())
    lands = []
    for s in srcs:
        own = lax.dynamic_index_in_dim(s, mine, 0, keepdims=True) if per_peer else s[None]
        shape = s.shape if per_peer else (N_DEV,) + s.shape
        lands.append(lax.dynamic_update_slice_in_dim(lax.empty(shape, s.dtype), own, mine, 0))
    has_dep = dep is not None

    def body(*refs):
        src_refs, land_refs = refs[:n], refs[n:2 * n]
        send_sems, recv_sems = refs[2 * n + has_dep], refs[2 * n + has_dep + 1]
        token = refs[-1]
        me = _my_place()
        for a in range(n):
            for mask in masks:
                _push_copy(src_refs, land_refs, send_sems, recv_sems, a, mask, me, per_peer, True).start()
        token[...] = jnp.zeros_like(token)

    hbm_in = [pltpu.with_memory_space_constraint(v, pltpu.HBM) for v in list(srcs) + lands]
    out = pl.pallas_call(
        body, name=name,
        out_shape=(pltpu.SemaphoreType.DMA((n * (N_DEV - 1),)), pltpu.SemaphoreType.DMA((n * (N_DEV - 1),)),
                   *[pltpu.HBM(v.shape, v.dtype) for v in hbm_in], jax.ShapeDtypeStruct((8, LANES), F32)),
        in_specs=[_HBM] * (2 * n) + ([_ANY] if has_dep else []),
        out_specs=(_SEM, _SEM, *[_HBM] * (2 * n), pl.BlockSpec(memory_space=pltpu.VMEM)),
        input_output_aliases={i: 2 + i for i in range(2 * n)},
        compiler_params=pltpu.CompilerParams(has_side_effects=_EFFECT),
    )(*hbm_in, *([dep] if has_dep else []))
    return (n, per_peer, masks, out[:-1]), out[-1]


def _push_wait(handle, after, *, name):
    n, per_peer, masks, (send_sems, recv_sems, *bufs) = handle

    def body(*refs):
        src_refs, land_refs = refs[:n], refs[n:2 * n]
        send_sems, recv_sems = refs[2 * n], refs[2 * n + 1]
        me = _my_place()
        for a in range(n):
            for mask in masks:
                cp = _push_copy(src_refs, land_refs, send_sems, recv_sems, a, mask, me, per_peer, False)
                cp.wait_send()
                cp.wait_recv()

    out = pl.pallas_call(
        body, name=name, out_shape=tuple(pltpu.HBM(v.shape, v.dtype) for v in bufs),
        in_specs=[_HBM] * (2 * n) + [_SEM, _SEM, _ANY], out_specs=tuple([_HBM] * (2 * n)),
        input_output_aliases={i: i for i in range(2 * n)},
        compiler_params=pltpu.CompilerParams(has_side_effects=_EFFECT),
    )(*bufs, send_sems, recv_sems, after)
    return list(out[n:])


def _relay_copy(land_refs, send_sems, recv_sems, a, j, me, outgoing):
    sibling = _peer(me, 1)
    out_slot = _slot(_peer(me, CHIP_PEERS[j]))
    in_slot = _slot(_peer(sibling, CHIP_PEERS[j]))
    k = a * len(CHIP_PEERS) + j
    return pltpu.make_async_remote_copy(
        src_ref=land_refs[a].at[out_slot], dst_ref=land_refs[a].at[out_slot if outgoing else in_slot],
        send_sem=send_sems.at[k], recv_sem=recv_sems.at[k], device_id=sibling, device_id_type=MESH)


def _relay_start(lands, *, name):
    n = len(lands)

    def body(*refs):
        land_refs, send_sems, recv_sems, token = refs[:n], refs[n], refs[n + 1], refs[-1]
        me = _my_place()
        for a in range(n):
            for j in range(len(CHIP_PEERS)):
                _relay_copy(land_refs, send_sems, recv_sems, a, j, me, True).start()
        token[...] = jnp.zeros_like(token)

    hbm_in = [pltpu.with_memory_space_constraint(v, pltpu.HBM) for v in lands]
    n_sem = n * len(CHIP_PEERS)
    out = pl.pallas_call(
        body, name=name,
        out_shape=(pltpu.SemaphoreType.DMA((n_sem,)), pltpu.SemaphoreType.DMA((n_sem,)),
                   *[pltpu.HBM(v.shape, v.dtype) for v in hbm_in], jax.ShapeDtypeStruct((8, LANES), F32)),
        in_specs=[_HBM] * n, out_specs=(_SEM, _SEM, *[_HBM] * n, pl.BlockSpec(memory_space=pltpu.VMEM)),
        input_output_aliases={i: 2 + i for i in range(n)},
        compiler_params=pltpu.CompilerParams(has_side_effects=_EFFECT),
    )(*hbm_in)
    return (n, out[:-1]), out[-1]


def _relay_wait(handle, after, *, name):
    n, (send_sems, recv_sems, *bufs) = handle

    def body(*refs):
        land_refs, send_sems, recv_sems = refs[:n], refs[n], refs[n + 1]
        me = _my_place()
        for a in range(n):
            for j in range(len(CHIP_PEERS)):
                cp = _relay_copy(land_refs, send_sems, recv_sems, a, j, me, False)
                cp.wait_send()
                cp.wait_recv()

    out = pl.pallas_call(
        body, name=name, out_shape=tuple(pltpu.HBM(v.shape, v.dtype) for v in bufs),
        in_specs=[_HBM] * n + [_SEM, _SEM, _ANY], out_specs=tuple([_HBM] * n),
        input_output_aliases={i: i for i in range(n)},
        compiler_params=pltpu.CompilerParams(has_side_effects=_EFFECT),
    )(*bufs, send_sems, recv_sems, after)
    return list(out)


def _all_reduce_rows(v, *, name):
    r, c = v.shape

    def body(v_ref, o_ref, buf, send_sems, recv_sems):
        me = _my_place()
        mine = _slot(me)
        sends = []
        for mask in range(1, N_DEV):
            peer = _peer(me, mask)
            sends.append(pltpu.make_async_remote_copy(
                src_ref=v_ref, dst_ref=buf.at[mine], send_sem=send_sems.at[mask - 1],
                recv_sem=recv_sems.at[mask - 1], device_id=peer, device_id_type=MESH))
        for cp in sends:
            cp.start()
        buf[mine] = v_ref[...]
        for mask in range(1, N_DEV):
            peer = _peer(me, mask)
            pltpu.make_async_remote_copy(
                src_ref=v_ref, dst_ref=buf.at[_slot(peer)], send_sem=send_sems.at[mask - 1],
                recv_sem=recv_sems.at[mask - 1], device_id=peer, device_id_type=MESH).wait_recv()
        for cp in sends:
            cp.wait_send()
        total = buf[0]
        for j in range(1, N_DEV):
            total = total + buf[j]
        o_ref[...] = total

    vm = pl.BlockSpec(memory_space=pltpu.VMEM)
    return pl.pallas_call(
        body, in_specs=[vm], out_specs=vm, out_shape=jax.ShapeDtypeStruct((r, c), F32),
        scratch_shapes=[pltpu.VMEM((N_DEV, r, c), F32), pltpu.SemaphoreType.DMA((7,)),
                        pltpu.SemaphoreType.DMA((7,))],
        name=name)(v)


def _xa_fwd(x, mem, g_x, g_m, wq, wkv, wo, tag):
    hx = _rms_fwd(x, g_x, name=f"xa{tag}_norm")
    memn = _rms_fwd(mem, g_m, name=f"xa{tag}_mem_norm")
    q = _mm2(hx, wq, "nn", MXU_DTYPE, name=f"xa{tag}_q")
    kv = _mm2(memn, wkv, "nn", MXU_DTYPE, name=f"xa{tag}_kv", b_split=True)
    o = _xattn_fwd(q, kv, name=f"xa{tag}_attn")
    return _mm2(o, wo, "nn", F32, name=f"xa{tag}_out", add=x), (hx, memn, q, kv, o)


def _xa_bwd(x, mem, g_x, g_m, wq, wkv, wo, saved, dxo, dxo_lo, tag, put):
    hx, memn, q, kv, o = saved
    do = _mm2(dxo_lo, wo, "nt", MXU_DTYPE, name=f"xa{tag}_do")
    dwo = _mm2(o, dxo_lo, "tn", MXU_DTYPE, name=f"xa{tag}_dwo")
    dq, dkv = _xattn_bwd(q, kv, do, name=f"xa{tag}_attn_bwd")
    dwq = _mm2(hx, dq, "tn", MXU_DTYPE, name=f"xa{tag}_dwq")
    dwkv = _mm2(memn, dkv, "tn", MXU_DTYPE, name=f"xa{tag}_dwkv", o_split=True, tn=wkv.shape[2])
    tok = put((dwq, dwkv, dwo))
    dhx = _mm2(dq, wq, "nt", F32, name=f"xa{tag}_dh", dep=tok)
    dx, dx_lo, dgx = _rms_bwd(x, g_x, dhx, dxo, name=f"xa{tag}_norm_bwd")
    dmemn = _mm2(dkv, wkv, "nt", F32, name=f"xa{tag}_dmem", b_split=True)
    _, _, dgm = _rms_bwd(mem, g_m, dmemn, None, name=f"xa{tag}_mem_norm_bwd")
    return dx, dx_lo, dgx, dgm


def _ffn_fwd(x, g, wg, wu, wd, tag):
    h = _rms_fwd(x, g, name=f"ffn{tag}_norm")
    gate, up, act = _ffn_up(h, wg, wu, name=f"ffn{tag}_up")
    return _ffn_down(act, wd, x, name=f"ffn{tag}_down"), (h, gate, up, act)


def _ffn_bwd(x, g, wg, wu, wd, saved, dxo, dxo_lo, tag, put):
    h, gate, up, act = saved
    dwd = _mm(act, dxo_lo[None], "tn", MXU_DTYPE, name=f"ffn{tag}_dwd")
    dgate, dup = _ffn_dact(dxo_lo, wd, gate, up, name=f"ffn{tag}_dact")
    dwg = _mm(dgate, h[None], "tn", MXU_DTYPE, name=f"ffn{tag}_dwg")
    dwu = _mm(dup, h[None], "tn", MXU_DTYPE, name=f"ffn{tag}_dwu")
    tok = put((dwg, dwu, dwd))
    dh = _ffn_dh(dgate, dup, wg, wu, tok, name=f"ffn{tag}_dh")
    dx, dx_lo, dg = _rms_bwd(x, g, dh, dxo, name=f"ffn{tag}_norm_bwd")
    return dx, dx_lo, dg


def _even_fwd(x, g, lb, w_in, w_pool, pool_scale, hg_norm, w_out):
    h = _rms_fwd(x, g, name="ev_norm")
    z = _mm2(h, w_in, "nn", F32, name="ev_in", b_split=True)
    ycat = _pool_fwd(z, w_pool, pool_scale, name="ev_pool")
    ycat, o_raw, states = _hgrn_fwd(z, lb, hg_norm, ycat, name="ev_hgrn")
    return _mm2(ycat, w_out, "nn", F32, name="ev_out", add=x), (h, z, ycat, o_raw, states)


def _even_bwd(x, g, lb, w_in, w_pool, pool_scale, hg_norm, w_out, saved, dxo, dxo_lo, put):
    h, z, ycat, o_raw, states = saved
    dycat = _mm2(dxo_lo, w_out, "nt", MXU_DTYPE, name="ev_dy")
    dw_out = _mm2(ycat, dxo_lo, "tn", MXU_DTYPE, name="ev_dw_out")
    du, dw_pool, dscale = _pool_bwd(z, w_pool, pool_scale, dycat, name="ev_pool_bwd")
    dq, dfl, di, dg, dlb, dhn = _hgrn_bwd(z, lb, hg_norm, o_raw, states, dycat, name="ev_hgrn_bwd")
    dz = jnp.concatenate([du, dq, dfl, di, dg], axis=1)
    dw_in = _mm2(h, dz, "tn", MXU_DTYPE, name="ev_dw_in", o_split=True, tn=w_in.shape[2])
    tok = put((dw_in, dw_pool, dw_out))
    dh = _mm_nt_split(dz, w_in, tok, name="ev_dh")
    dx, dx_lo, dgn = _rms_bwd(x, g, dh, dxo, name="ev_norm_bwd")
    return dx, dx_lo, dict(ev_norm=dgn, ev_pool_scale=dscale, ev_hg_norm=dhn, lb=dlb)


def _odd_fwd(x, g, w_qkv, w_f, b_f, w_out):
    n_qkv = 3 * x.shape[1]
    h = _rms_fwd(x, g, name="od_norm")
    zqkv = _mm2(h, w_qkv, "nt", MXU_DTYPE, name="od_qkv", n_b=n_qkv)
    zf = _mm2(h, w_f, "nt", F32, name="od_gate")
    fcol = _fox_prep(zf, b_f, name="od_fox_prep")
    o, lse = _fox_fwd(zqkv, fcol, name="od_fox")
    return _mm2(o, w_out, "nn", F32, name="od_out", add=x), (h, zqkv, zf, fcol, o, lse)


def _odd_bwd(x, g, w_qkv, w_f, b_f, w_out, saved, dxo, dxo_lo, put):
    h, zqkv, zf, fcol, o, lse = saved
    do = _mm2(dxo_lo, w_out, "nt", MXU_DTYPE, name="od_do")
    dw_out = _mm2(o, dxo_lo, "tn", MXU_DTYPE, name="od_dw_out")
    dq, dk, dv, rq, rk = _fox_bwd(zqkv, fcol, lse, o, do, name="od_fox_bwd")
    dfl, db_f = _fox_gate_bwd(rq[:, 0, :], rk[:, 0, :], zf, b_f, name="od_fox_gate_bwd")
    dz = jnp.concatenate([dq, dk, dv], axis=1)
    dw_qkv = _mm2(dz, h, "tn", MXU_DTYPE, name="od_dw_qkv")
    dw_f = _mm2(dfl, h, "tn", MXU_DTYPE, name="od_dw_gate")
    tok = put((dw_qkv, dw_f, dw_out))
    dh = _mm2(dz, w_qkv, "nn", F32, name="od_dh_qkv", dep=tok)
    dh = _mm2(dfl, w_f, "nn", F32, name="od_dh_gate", add=dh)
    dx, dx_lo, dgn = _rms_bwd(x, g, dh, dxo, name="od_norm_bwd")
    return dx, dx_lo, dict(od_norm=dgn, od_b_f=db_f)


def _local_step(x, mem, target, sp, get_w, put_dw):
    b_f = jnp.pad(sp["od_b_f"], ((0, 0), (0, LANES - sp["od_b_f"].shape[1])))
    lb = _lb_fwd(sp["lb_table"], 0, name="lb_fwd")
    fin = sp["final_norm"].reshape(1, -1)
    xn, xm, fn = sp["xa_norm"], sp["xa_mem_norm"], sp["ffn_norm"]
    w_ev = get_w("ev", None)
    x1, s_ev = _even_fwd(x, sp["ev_norm"], lb, w_ev[0], w_ev[1], sp["ev_pool_scale"], sp["ev_hg_norm"], w_ev[2])
    w_xa0 = get_w("xa0", x1)
    x2, s_xa0 = _xa_fwd(x1, mem, xn[0:1], xm[0:1], *w_xa0, 0)
    w_ff0 = get_w("ffn0", x2)
    x3, s_ff0 = _ffn_fwd(x2, fn[0:1], *w_ff0, 0)
    w_qkv, w_f, w_od_out, od_norm = get_w("od", x3)
    x4, s_od = _odd_fwd(x3, od_norm, w_qkv, w_f, b_f, w_od_out)
    w_xa1 = get_w("xa1", x4)
    x5, s_xa1 = _xa_fwd(x4, mem, xn[1:2], xm[1:2], *w_xa1, 1)
    w_ff1 = get_w("ffn1", x5)
    x6, s_ff1 = _ffn_fwd(x5, fn[1:2], *w_ff1, 1)
    loss, dx, dx_lo, d_fin = _loss_head(x6, fin, target, name="loss_head")
    put = lambda grp: functools.partial(put_dw, grp)
    dx, dx_lo, d_ffn1 = _ffn_bwd(x5, fn[1:2], *w_ff1, s_ff1, dx, dx_lo, 1, put("ffn1"))
    dx, dx_lo, d_xa1, d_xm1 = _xa_bwd(x4, mem, xn[1:2], xm[1:2], *w_xa1, s_xa1, dx, dx_lo, 1, put("xa1"))
    dx, dx_lo, d_od = _odd_bwd(x3, od_norm, w_qkv, w_f, b_f, w_od_out, s_od, dx, dx_lo, put("od"))
    dx, dx_lo, d_ffn0 = _ffn_bwd(x2, fn[0:1], *w_ff0, s_ff0, dx, dx_lo, 0, put("ffn0"))
    dx, dx_lo, d_xa0, d_xm0 = _xa_bwd(x1, mem, xn[0:1], xm[0:1], *w_xa0, s_xa0, dx, dx_lo, 0, put("xa0"))
    dx, _, d_ev = _even_bwd(x, sp["ev_norm"], lb, w_ev[0], w_ev[1], sp["ev_pool_scale"], sp["ev_hg_norm"],
                            w_ev[2], s_ev, dx, dx_lo, put("ev"))
    small = dict(
        lb_table=_lb_bwd(sp["lb_table"], d_ev["lb"], 0, name="lb_bwd"),
        ev_norm=d_ev["ev_norm"], ev_pool_scale=d_ev["ev_pool_scale"], ev_hg_norm=d_ev["ev_hg_norm"],
        od_norm=d_od["od_norm"], od_b_f=d_od["od_b_f"][:, :sp["od_b_f"].shape[1]],
        xa_norm=jnp.concatenate([d_xa0, d_xa1], axis=0), xa_mem_norm=jnp.concatenate([d_xm0, d_xm1], axis=0),
        ffn_norm=jnp.concatenate([d_ffn0, d_ffn1], axis=0), final_norm=d_fin.reshape(-1))
    return loss, dx, small


_SMALL = ("lb_table", "ev_norm", "ev_pool_scale", "ev_hg_norm", "od_norm", "od_b_f", "xa_norm", "xa_mem_norm",
          "ffn_norm", "final_norm")
_WEIGHTS = ("lb_table", "ev_norm", "ev_w_in", "ev_w_pool", "ev_pool_scale", "ev_hg_norm", "ev_w_out", "od_norm",
            "od_w_in", "od_b_f", "od_w_out", "xa_norm", "xa_mem_norm", "xa_wq", "xa_wkv", "xa_wo", "ffn_norm",
            "ffn_w_gate", "ffn_w_up", "ffn_w_down", "final_norm")


def _lo(a):
    return a.astype(MXU_DTYPE)


def _rows(v):
    flat = v.reshape(-1)
    return jnp.pad(flat, (0, (-flat.shape[0]) % LANES)).reshape(-1, LANES)


def kernel(x, mem, lb_table, ev_norm, ev_w_in, ev_w_pool, ev_pool_scale, ev_hg_norm, ev_w_out, od_norm, od_w_in, od_b_f, od_w_out, xa_norm, xa_mem_norm, xa_wq, xa_wkv, xa_wo, ffn_norm, ffn_w_gate, ffn_w_up, ffn_w_down, final_norm, loss_target, m_lb_table, m_ev_norm, m_ev_w_in, m_ev_w_pool, m_ev_pool_scale, m_ev_hg_norm, m_ev_w_out, m_od_norm, m_od_w_in, m_od_b_f, m_od_w_out, m_xa_norm, m_xa_mem_norm, m_xa_wq, m_xa_wkv, m_xa_wo, m_ffn_norm, m_ffn_w_gate, m_ffn_w_up, m_ffn_w_down, m_final_norm, v_lb_table, v_ev_norm, v_ev_w_in, v_ev_w_pool, v_ev_pool_scale, v_ev_hg_norm, v_ev_w_out, v_od_norm, v_od_w_in, v_od_b_f, v_od_w_out, v_xa_norm, v_xa_mem_norm, v_xa_wq, v_xa_wkv, v_xa_wo, v_ffn_norm, v_ffn_w_gate, v_ffn_w_up, v_ffn_w_down, v_final_norm):
    arg = dict(locals())
    d = x.shape[-1]
    layers = xa_wq.shape[0]
    me = _slot(_my_place())

    n_gate = od_b_f.shape[1]
    turned = {k: jnp.swapaxes(arg[k], 1, 2) for k in ("od_w_in", "ffn_w_gate", "ffn_w_up")}
    raw = dict(ev=[ev_w_in[0], ev_w_pool[0], ev_w_out[0]], od=[turned["od_w_in"][0], od_w_out[0], od_norm])
    for l in range(layers):
        raw[f"xa{l}"] = [xa_wq[l], xa_wkv[l], xa_wo[l]]
        raw[f"ffn{l}"] = [turned["ffn_w_gate"][l], turned["ffn_w_up"][l], ffn_w_down[l]]
    order = ("ev", "xa0", "ffn0", "od", "xa1", "ffn1")
    gathers, relays, tok = {}, {}, None
    for grp in order:
        srcs = [w if tok is None else w + tok[0, 0] for w in raw[grp]]
        srcs = [w if grp == "od" and j == 2 else _lo(w) for j, w in enumerate(srcs)]
        gathers[grp], tok = _push_start(srcs, None, per_peer=False, masks=FIRST_HOP, name=f"gather_{grp}_start")
    last_start = tok

    def second_hop(grp, after):
        lands = _push_wait(gathers[grp], after, name=f"gather_{grp}_wait")
        relays[grp], token = _relay_start(lands, name=f"gather_{grp}_relay")
        return token

    def get_w(grp, after):
        i = order.index(grp)
        after = last_start if after is None else after
        if grp not in relays:
            after = second_hop(grp, after)
        if 1 <= i < len(order) - 1:
            after = second_hop(order[i + 1], after)
        got = _relay_wait(relays[grp], after, name=f"gather_{grp}_relay_wait")
        if grp == "ev":
            w_in, w_pool, w_out = got
            w_pool = jnp.transpose(w_pool, (1, 0, 2, 3)).reshape(w_pool.shape[1], -1, w_pool.shape[3])
            return w_in, w_pool, w_out.reshape(d, d)
        if grp == "od":
            w_in, w_out, nrm = got
            assert n_gate <= w_in.shape[1]
            w_f = jnp.pad(w_in[N_DEV - 1, w_in.shape[1] - n_gate:], ((0, LANES - n_gate), (0, 0)))
            return w_in.reshape(-1, d), w_f, w_out.reshape(d, d), nrm.reshape(1, d)
        if grp.startswith("xa"):
            return got[0].reshape(d, d), got[1], got[2].reshape(d, d)
        return tuple(got)

    def row_parts(g):
        return g.reshape(N_DEV, -1, g.shape[-1])

    scatters = {}

    def put_dw(grp, dws):
        if grp == "ev":
            dw_in, dw_pool, dw_out = dws
            gc = dw_pool.shape[1] // N_DEV
            dw_pool = _lo(jnp.transpose(dw_pool.reshape(dw_pool.shape[0], N_DEV, gc, -1), (1, 0, 2, 3)))
            parts = [dw_in, dw_pool, row_parts(dw_out)]
        elif grp == "od":
            dw_qkv, dw_f, dw_out = dws
            parts = [row_parts(jnp.concatenate([dw_qkv, dw_f[:n_gate]], axis=0)), row_parts(dw_out)]
        elif grp.startswith("xa"):
            parts = [row_parts(dws[0]), dws[1], row_parts(dws[2])]
        else:
            parts = list(dws)
        scatters[grp], token = _push_start(parts, None, per_peer=True, name=f"scatter_{grp}_start")
        return token

    sp = {k: arg[k] for k in _SMALL if k != "od_norm"}
    loss, dx, small = _local_step(x[0], mem[0], loss_target[0], sp, get_w, put_dw)

    pieces = [_rows(small[k]) for k in _SMALL]
    packed = jnp.concatenate(pieces + [_rows(loss)], axis=0)
    packed = jnp.pad(packed, ((0, (-packed.shape[0]) % 8), (0, 0)))
    total = _all_reduce_rows(packed, name="all_reduce_small")
    loss = total[sum(pc.shape[0] for pc in pieces), 0]
    small_g, at = {}, 0
    for k, pc in zip(_SMALL, pieces):
        n = small[k].size
        small_g[k] = total[at:at + pc.shape[0]].reshape(-1)[:n].reshape(small[k].shape)
        at += pc.shape[0]
    small_g["od_norm"] = lax.dynamic_slice_in_dim(small_g["od_norm"], me * od_norm.shape[1], od_norm.shape[1], axis=1)

    res = {}
    for k in _SMALL:
        w, m, v = arg[k], arg["m_" + k], arg["v_" + k]
        shp = (1, 1, w.shape[0]) if w.ndim == 1 else (1,) + w.shape
        out = _adamw(w.reshape(shp), m.reshape(shp), v.reshape(shp), small_g[k].reshape(shp), name=f"adamw_{k}")
        res[k] = [o.reshape(w.shape) for o in out]
    members = dict(ev=("ev_w_in", "ev_w_pool", "ev_w_out"), od=("od_w_in", "od_w_out"),
                   xa=("xa_wq", "xa_wkv", "xa_wo"), ffn=("ffn_w_gate", "ffn_w_up", "ffn_w_down"))
    after, stacked = dx, {}
    for grp in ("ffn1", "xa1", "od", "ffn0", "xa0", "ev"):
        got = _push_wait(scatters[grp], after, name=f"scatter_{grp}_wait")
        kind = grp.rstrip("01")
        for k, parts in zip(members[kind], got):
            w, m, v = [jnp.swapaxes(a, 1, 2) if k in turned else a for a in (arg[k], arg["m_" + k], arg["v_" + k])]
            if w.shape[0] == 1:
                shp = (1, -1, w.shape[-1])
                out = _adamw(w.reshape(shp), m.reshape(shp), v.reshape(shp), parts.reshape(N_DEV, -1, w.shape[-1]),
                             name=f"adamw_{k}")
            else:
                out = _adamw(w, m, v, parts, name=f"adamw_{k}{grp[-1]}", layer=int(grp[-1]), prev=stacked.get(k))
                stacked[k] = out
            res[k] = [jnp.swapaxes(o.reshape(w.shape), 1, 2) if k in turned else o.reshape(w.shape) for o in out]
            after = out[3][:1, :8, :LANES]

    outs = [loss, dx[None]]
    for j in range(4):
        outs += [res[k][j] for k in _WEIGHTS]
    return tuple(outs)
```

```python
import functools

import jax
import jax.numpy as jnp
from jax import lax
from jax.experimental import pallas as pl
from jax.experimental.pallas import tpu as pltpu

F32 = jnp.float32
MXU_DTYPE = jnp.bfloat16
EPS = 1e-6
N_DEV = 8
V7X_VMEM_BYTES = 64 * 1024 * 1024
VMEM_LIMIT_BYTES = V7X_VMEM_BYTES - 8 * 1024 * 1024
LANES = 128
HIGHEST = lax.Precision.HIGHEST
MESH = pl.DeviceIdType.MESH

HG_HEAD = 128
HG_CHUNK = 32
FOX_HEAD = 128
XA_HEADS = 4
POOL_GROUPS = 4

ADAM_LR = 0.001
ADAM_B1 = 0.9
ADAM_B2 = 0.999
ADAM_EPS = 1e-08
ADAM_WD = 0.01
ADAM_STEP = 10

_NN = ((1,), (0,))
_NT = ((1,), (1,))
_TN = ((0,), (0,))


def _dot(a, b, dims):
    return lax.dot_general(a.astype(MXU_DTYPE), b.astype(MXU_DTYPE), (dims, ((), ())),
                           preferred_element_type=F32)


def _dot_f32(a, b, dims):
    return lax.dot_general(a, b, (dims, ((), ())), preferred_element_type=F32, precision=HIGHEST)


def _cp(*sem):
    return pltpu.CompilerParams(dimension_semantics=sem, vmem_limit_bytes=VMEM_LIMIT_BYTES)


def _tile(n, pref, align=LANES):
    if n <= pref:
        return n
    t = (pref // align) * align
    while t >= align:
        if n % t == 0:
            return t
        t -= align
    return n


def _sigmoid(x):
    return jax.nn.sigmoid(x)


def _mm(a, b, mode, out_dtype, *, name, add=None, dep=None, reduce_b=False, b_split=False, o_split=False,
        n_b=None, tm=1024, tn=1024, tk=2048):
    ba, bb = a.shape[0], b.shape[0]
    if mode == "tn":
        kdim, m = a.shape[1], a.shape[2]
        tk = 2 * tk
    else:
        m, kdim = a.shape[1], a.shape[2]
    if b_split:
        s_cnt, b_rows, w = b.shape
        if mode == "nt":
            n = b_rows
            assert kdim == s_cnt * w
            tk = w
        else:
            n = s_cnt * w
            assert b_rows == kdim
            tn = w
        nb = ba
    else:
        n = b.shape[1] if mode == "nt" else b.shape[2]
        n = n if n_b is None else n_b
        nb = max(ba, bb)
    if not (b_split and mode != "nt"):
        tn = _tile(n, tn)
    if not (b_split and mode == "nt"):
        tk = _tile(kdim, tk)
    tm = _tile(m, tm)
    assert m % tm == 0 and n % tn == 0 and kdim % tk == 0, (name, m, n, kdim, tm, tn, tk)
    nk = kdim // tk
    if reduce_b:
        grid = (m // tm, n // tn, nb, nk)
        unpack = lambda i, j, bi, k: (bi, i, j, k)
        sem = ("parallel", "parallel", "arbitrary", "arbitrary")
        nred = nb * nk
    else:
        grid = (nb, m // tm, n // tn, nk)
        unpack = lambda bi, i, j, k: (bi, i, j, k)
        sem = ("parallel", "parallel", "parallel", "arbitrary")
        nred = nk

    def a_map(*g):
        bi, i, j, k = unpack(*g)
        ab = bi if ba > 1 else 0
        return (ab, k, i) if mode == "tn" else (ab, i, k)

    def b_map(*g):
        bi, i, j, k = unpack(*g)
        if b_split:
            return (k, j, 0) if mode == "nt" else (j, k, 0)
        bq = bi if bb > 1 else 0
        return (bq, j, k) if mode == "nt" else (bq, k, j)

    def o_map(*g):
        bi, i, j, k = unpack(*g)
        if o_split:
            return (j, i, 0)
        return (0 if reduce_b else bi, i, j)

    a_blk = (1, tk, tm) if mode == "tn" else (1, tm, tk)
    b_blk = (1, tn, tk) if mode == "nt" else (1, tk, tn)
    dims = {"nn": _NN, "nt": _NT, "tn": _TN}[mode]
    has_add = add is not None

    def body(*refs):
        a_ref, b_ref = refs[:2]
        if has_add:
            add_ref = refs[2]
        if nred == 1:
            o_ref = refs[-1]
            r = _dot(a_ref[0], b_ref[0], dims)
            if has_add:
                r = r + add_ref[0].astype(F32)
            o_ref[0] = r.astype(o_ref.dtype)
            return
        o_ref, acc_ref = refs[-2:]
        if reduce_b:
            step = pl.program_id(2) * nk + pl.program_id(3)
        else:
            step = pl.program_id(3)

        @pl.when(step == 0)
        def _():
            acc_ref[...] = _dot(a_ref[0], b_ref[0], dims)

        @pl.when(step > 0)
        def _():
            acc_ref[...] += _dot(a_ref[0], b_ref[0], dims)

        @pl.when(step == nred - 1)
        def _():
            r = acc_ref[...]
            if has_add:
                r = r + add_ref[0].astype(F32)
            o_ref[0] = r.astype(o_ref.dtype)

    in_specs = [pl.BlockSpec(a_blk, a_map), pl.BlockSpec(b_blk, b_map)]
    operands = [a, b]
    if has_add:
        in_specs.append(pl.BlockSpec((1, tm, tn), o_map))
        operands.append(add)
    if dep is not None:
        in_specs.append(pl.BlockSpec(memory_space=pl.ANY))
        operands.append(dep)
    if o_split:
        out_shape = jax.ShapeDtypeStruct((n // tn, m, tn), out_dtype)
    else:
        out_shape = jax.ShapeDtypeStruct((1 if reduce_b else nb, m, n), out_dtype)
    return pl.pallas_call(
        body, grid=grid, in_specs=in_specs, out_specs=pl.BlockSpec((1, tm, tn), o_map),
        out_shape=out_shape, scratch_shapes=[] if nred == 1 else [pltpu.VMEM((tm, tn), F32)],
        compiler_params=_cp(*sem), name=name)(*operands)


def _mm_nt_split(a, b, dep, *, name, tm=512, tn=1024):
    m, k = a.shape
    s, n, w = b.shape
    assert k == s * w and w % LANES == 0
    tm, tn = _tile(m, tm), _tile(n, tn)
    has_dep = dep is not None

    def body(*refs):
        a_ref, b_ref, o_ref = refs[0], refs[1], refs[-1]
        r = _dot(a_ref[:, 0:w], b_ref[0], _NT)
        for q in range(1, s):
            r = r + _dot(a_ref[:, q * w:(q + 1) * w], b_ref[q], _NT)
        o_ref[...] = r

    return pl.pallas_call(
        body, grid=(n // tn, m // tm),
        in_specs=[pl.BlockSpec((tm, k), lambda j, i: (i, 0)), pl.BlockSpec((s, tn, w), lambda j, i: (0, j, 0))]
        + ([pl.BlockSpec(memory_space=pl.ANY)] if has_dep else []),
        out_specs=pl.BlockSpec((tm, tn), lambda j, i: (i, j)), out_shape=jax.ShapeDtypeStruct((m, n), F32),
        compiler_params=_cp("parallel", "parallel"), name=name)(a, b, *([dep] if has_dep else []))


def _mm2(a, b, mode, out_dtype, *, name, add=None, **kw):
    b3 = b if kw.get("b_split") else b[None]
    r = _mm(a[None], b3, mode, out_dtype, name=name, add=None if add is None else add[None], **kw)
    return r if kw.get("o_split") else r[0]


def _rms_fwd(x, g, *, name, tb=512):
    t, d = x.shape
    tb = min(tb, t)

    def body(x_ref, g_ref, o_ref):
        xv = x_ref[...]
        r = lax.rsqrt(jnp.mean(xv * xv, axis=-1, keepdims=True) + EPS)
        o_ref[...] = (xv * r * g_ref[...]).astype(o_ref.dtype)

    return pl.pallas_call(
        body, grid=(t // tb,),
        in_specs=[pl.BlockSpec((tb, d), lambda i: (i, 0)), pl.BlockSpec((1, d), lambda i: (0, 0))],
        out_specs=pl.BlockSpec((tb, d), lambda i: (i, 0)),
        out_shape=jax.ShapeDtypeStruct((t, d), MXU_DTYPE), compiler_params=_cp("parallel"), name=name)(x, g)


def _rms_bwd(x, g, dh, dres, *, name, tb=512):
    t, d = x.shape
    tb = min(tb, t)
    has_res = dres is not None

    def body(*refs):
        if has_res:
            x_ref, g_ref, dh_ref, dres_ref, dx_ref, dxl_ref, dg_ref = refs
        else:
            x_ref, g_ref, dh_ref, dx_ref, dxl_ref, dg_ref = refs
        xv = x_ref[...]
        r = lax.rsqrt(jnp.mean(xv * xv, axis=-1, keepdims=True) + EPS)
        xh = xv * r
        dhv = dh_ref[...].astype(F32)

        @pl.when(pl.program_id(0) == 0)
        def _():
            dg_ref[...] = jnp.zeros_like(dg_ref)

        dg_ref[...] += jnp.sum(dhv * xh, axis=0, keepdims=True)
        dxh = dhv * g_ref[...]
        dx = r * (dxh - xh * jnp.mean(dxh * xh, axis=-1, keepdims=True))
        if has_res:
            dx = dx + dres_ref[...]
        dx_ref[...] = dx
        dxl_ref[...] = dx.astype(dxl_ref.dtype)

    row = pl.BlockSpec((tb, d), lambda i: (i, 0))
    vec = pl.BlockSpec((1, d), lambda i: (0, 0))
    operands = [x, g, dh] + ([dres] if has_res else [])
    return pl.pallas_call(
        body, grid=(t // tb,), in_specs=[row, vec, row] + ([row] if has_res else []),
        out_specs=[row, row, vec],
        out_shape=[jax.ShapeDtypeStruct((t, d), F32), jax.ShapeDtypeStruct((t, d), MXU_DTYPE),
                   jax.ShapeDtypeStruct((1, d), F32)],
        compiler_params=_cp("arbitrary"), name=name)(*operands)


def _loss_head(x, g, target, *, name, tb=512):
    t, d = x.shape
    tb = min(tb, t)

    def body(x_ref, g_ref, t_ref, loss_ref, dx_ref, dxl_ref, dg_ref):
        xv = x_ref[...]
        r = lax.rsqrt(jnp.mean(xv * xv, axis=-1, keepdims=True) + EPS)
        xh = xv * r
        gv = g_ref[...]
        err = xh * gv - t_ref[...]

        @pl.when(pl.program_id(0) == 0)
        def _():
            dg_ref[...] = jnp.zeros_like(dg_ref)
            loss_ref[...] = jnp.zeros_like(loss_ref)

        row_loss = jnp.mean(err * err, axis=-1, keepdims=True)
        loss_ref[...] += 0.5 * jnp.sum(row_loss, axis=0, keepdims=True)
        dy = err * (1.0 / d)
        dg_ref[...] += jnp.sum(dy * xh, axis=0, keepdims=True)
        dxh = dy * gv
        dx = r * (dxh - xh * jnp.mean(dxh * xh, axis=-1, keepdims=True))
        dx_ref[...] = dx
        dxl_ref[...] = dx.astype(dxl_ref.dtype)

    row = pl.BlockSpec((tb, d), lambda i: (i, 0))
    vec = pl.BlockSpec((1, d), lambda i: (0, 0))
    return pl.pallas_call(
        body, grid=(t // tb,), in_specs=[row, vec, row],
        out_specs=[pl.BlockSpec((1, 1), lambda i: (0, 0)), row, row, vec],
        out_shape=[jax.ShapeDtypeStruct((1, 1), F32), jax.ShapeDtypeStruct((t, d), F32),
                   jax.ShapeDtypeStruct((t, d), MXU_DTYPE), jax.ShapeDtypeStruct((1, d), F32)],
        compiler_params=_cp("arbitrary"), name=name)(x, g, target)


def _ffn_up(h, wg, wu, *, name, tb=1024):
    t, d = h.shape
    s, f, _ = wg.shape
    tb = min(tb, t)

    def body(h_ref, wg_ref, wu_ref, g_ref, u_ref, a_ref):
        hv = h_ref[...]
        gv = _dot(hv, wg_ref[0], _NT)
        uv = _dot(hv, wu_ref[0], _NT)
        g_ref[0] = gv.astype(g_ref.dtype)
        u_ref[0] = uv.astype(u_ref.dtype)
        a_ref[0] = (gv * _sigmoid(gv) * uv).astype(a_ref.dtype)

    wspec = pl.BlockSpec((1, f, d), lambda j, i: (j, 0, 0))
    ospec = pl.BlockSpec((1, tb, f), lambda j, i: (j, i, 0))
    return pl.pallas_call(
        body, grid=(s, t // tb),
        in_specs=[pl.BlockSpec((tb, d), lambda j, i: (i, 0)), wspec, wspec],
        out_specs=[ospec, ospec, ospec],
        out_shape=[jax.ShapeDtypeStruct((s, t, f), MXU_DTYPE)] * 3,
        compiler_params=_cp("parallel", "parallel"), name=name)(h, wg, wu)


def _ffn_dact(dy, wd, gate, up, *, name, tb=1024):
    t, d = dy.shape
    s, f, _ = wd.shape
    tb = min(tb, t)

    def body(dy_ref, wd_ref, g_ref, u_ref, dg_ref, du_ref):
        da = _dot(dy_ref[...], wd_ref[0], _NT)
        gv = g_ref[0].astype(F32)
        sg = _sigmoid(gv)
        du_ref[0] = (da * gv * sg).astype(du_ref.dtype)
        dg_ref[0] = (da * u_ref[0].astype(F32) * (sg * (1.0 + gv * (1.0 - sg)))).astype(dg_ref.dtype)

    aspec = pl.BlockSpec((1, tb, f), lambda i, j: (j, i, 0))
    return pl.pallas_call(
        body, grid=(t // tb, s),
        in_specs=[pl.BlockSpec((tb, d), lambda i, j: (i, 0)),
                  pl.BlockSpec((1, f, d), lambda i, j: (j, 0, 0)), aspec, aspec],
        out_specs=[aspec, aspec],
        out_shape=[jax.ShapeDtypeStruct((s, t, f), MXU_DTYPE), jax.ShapeDtypeStruct((s, t, f), MXU_DTYPE)],
        compiler_params=_cp("parallel", "parallel"), name=name)(dy, wd, gate, up)


def _ffn_down(act, wd, x, *, name, tm=512, tn=1024):
    s, t, f = act.shape
    d = wd.shape[2]
    tm, tn = _tile(t, tm), _tile(d, tn)

    def body(a_ref, w_ref, x_ref, o_ref):
        r = x_ref[...]
        for j in range(s):
            r = r + _dot(a_ref[j], w_ref[j], _NN)
        o_ref[...] = r

    xspec = pl.BlockSpec((tm, tn), lambda k, i: (i, k))
    return pl.pallas_call(
        body, grid=(d // tn, t // tm),
        in_specs=[pl.BlockSpec((s, tm, f), lambda k, i: (0, i, 0)), pl.BlockSpec((s, f, tn), lambda k, i: (0, 0, k)),
                  xspec],
        out_specs=xspec, out_shape=jax.ShapeDtypeStruct((t, d), F32),
        compiler_params=_cp("parallel", "parallel"), name=name)(act, wd, x)


def _ffn_dh(dgate, dup, wg, wu, dep, *, name, tm=512, tn=1024, sg=4):
    s, t, f = dgate.shape
    d = wg.shape[2]
    tm, tn = _tile(t, tm), _tile(d, tn)
    steps = s // sg
    has_dep = dep is not None

    def body(*refs):
        dg_ref, du_ref, wg_ref, wu_ref = refs[:4]
        o_ref, acc_ref = refs[-2:]
        j = pl.program_id(2)
        part = _dot(dg_ref[0], wg_ref[0], _NN) + _dot(du_ref[0], wu_ref[0], _NN)
        for q in range(1, sg):
            part = part + _dot(dg_ref[q], wg_ref[q], _NN) + _dot(du_ref[q], wu_ref[q], _NN)

        @pl.when(j == 0)
        def _():
            acc_ref[...] = part

        @pl.when(j > 0)
        def _():
            acc_ref[...] += part

        @pl.when(j == steps - 1)
        def _():
            o_ref[...] = acc_ref[...]

    aspec = pl.BlockSpec((sg, tm, f), lambda i, k, j: (j, i, 0))
    wspec = pl.BlockSpec((sg, f, tn), lambda i, k, j: (j, 0, k))
    return pl.pallas_call(
        body, grid=(t // tm, d // tn, steps),
        in_specs=[aspec, aspec, wspec, wspec] + ([pl.BlockSpec(memory_space=pl.ANY)] if has_dep else []),
        out_specs=pl.BlockSpec((tm, tn), lambda i, k, j: (i, k)),
        out_shape=jax.ShapeDtypeStruct((t, d), F32), scratch_shapes=[pltpu.VMEM((tm, tn), F32)],
        compiler_params=_cp("parallel", "parallel", "arbitrary"),
        name=name)(dgate, dup, wg, wu, *([dep] if has_dep else []))


def _xattn_fwd(q, kv, *, name, tb=512):
    t, d = q.shape
    m = kv.shape[0]
    hd = d // XA_HEADS
    tb = min(tb, t)
    scale = hd ** -0.5

    def body(q_ref, kv_ref, o_ref):
        for hh in range(XA_HEADS):
            cs = slice(hh * hd, (hh + 1) * hd)
            s = _dot(q_ref[:, cs], kv_ref[:, cs], _NT) * scale
            s = s - jnp.max(s, axis=-1, keepdims=True)
            e = jnp.exp(s)
            p = e / jnp.sum(e, axis=-1, keepdims=True)
            o_ref[:, cs] = _dot(p, kv_ref[:, d + hh * hd:d + (hh + 1) * hd], _NN).astype(o_ref.dtype)

    return pl.pallas_call(
        body, grid=(t // tb,),
        in_specs=[pl.BlockSpec((tb, d), lambda i: (i, 0)), pl.BlockSpec((m, 2 * d), lambda i: (0, 0))],
        out_specs=pl.BlockSpec((tb, d), lambda i: (i, 0)),
        out_shape=jax.ShapeDtypeStruct((t, d), MXU_DTYPE), compiler_params=_cp("parallel"), name=name)(q, kv)


def _xattn_bwd(q, kv, do, *, name, tb=512):
    t, d = q.shape
    m = kv.shape[0]
    hd = d // XA_HEADS
    tb = min(tb, t)
    scale = hd ** -0.5

    def body(q_ref, kv_ref, do_ref, dq_ref, dkv_ref):
        @pl.when(pl.program_id(0) == 0)
        def _():
            dkv_ref[...] = jnp.zeros_like(dkv_ref)

        for hh in range(XA_HEADS):
            cs = slice(hh * hd, (hh + 1) * hd)
            vs = slice(d + hh * hd, d + (hh + 1) * hd)
            qv, kk, vv, dov = q_ref[:, cs], kv_ref[:, cs], kv_ref[:, vs], do_ref[:, cs]
            s = _dot(qv, kk, _NT) * scale
            s = s - jnp.max(s, axis=-1, keepdims=True)
            e = jnp.exp(s)
            p = e / jnp.sum(e, axis=-1, keepdims=True)
            dkv_ref[:, vs] += _dot(p, dov, _TN)
            dp = _dot(dov, vv, _NT)
            ds = p * (dp - jnp.sum(p * dp, axis=-1, keepdims=True)) * scale
            dq_ref[:, cs] = _dot(ds, kk, _NN).astype(dq_ref.dtype)
            dkv_ref[:, cs] += _dot(ds, qv, _TN)

    row = pl.BlockSpec((tb, d), lambda i: (i, 0))
    full = pl.BlockSpec((m, 2 * d), lambda i: (0, 0))
    return pl.pallas_call(
        body, grid=(t // tb,), in_specs=[row, full, row], out_specs=[row, full],
        out_shape=[jax.ShapeDtypeStruct((t, d), MXU_DTYPE), jax.ShapeDtypeStruct((m, 2 * d), F32)],
        compiler_params=_cp("arbitrary"), name=name)(q, kv, do)


def _pool_window_stats(u, gi, reverse):
    t = u.shape[0]
    row = lax.broadcasted_iota(jnp.int32, u.shape, 0)
    s = u
    for j in range(POOL_GROUPS):
        sh = 1 << j
        if reverse:
            rolled = jnp.where(row < t - sh, pltpu.roll(s, t - sh, axis=0), 0.0)
        else:
            rolled = jnp.where(row >= sh, pltpu.roll(s, sh, axis=0), 0.0)
        s = jnp.where(j <= gi, s + rolled, s)
    return s, row


def _pool_fwd(z, w_pool, scale, *, name):
    t = z.shape[0]
    g_cnt, c, _ = w_pool.shape

    def body(z_ref, w_ref, s_ref, o_ref):
        gi = pl.program_id(0)
        u = z_ref[...]
        win, row = _pool_window_stats(u, gi, False)
        cnt = jnp.minimum(row + 1, lax.shift_left(jnp.int32(2), gi)).astype(F32)
        p = win / cnt - u
        o_ref[...] = (_dot(p, w_ref[0], _NN) * s_ref[...]).astype(o_ref.dtype)

    return pl.pallas_call(
        body, grid=(g_cnt,),
        in_specs=[pl.BlockSpec((t, c), lambda g: (0, g)), pl.BlockSpec((1, c, c), lambda g: (g, 0, 0)),
                  pl.BlockSpec((1, c), lambda g: (0, g))],
        out_specs=pl.BlockSpec((t, c), lambda g: (0, g)),
        out_shape=jax.ShapeDtypeStruct((t, 2 * g_cnt * c), MXU_DTYPE),
        compiler_params=_cp("parallel"), name=name)(z, w_pool, scale)


def _pool_bwd(z, w_pool, scale, dycat, *, name):
    t = z.shape[0]
    g_cnt, c, _ = w_pool.shape

    def body(z_ref, w_ref, s_ref, dy_ref, du_ref, dw_ref, ds_ref):
        gi = pl.program_id(0)
        u = z_ref[...]
        win, row = _pool_window_stats(u, gi, False)
        cnt = jnp.minimum(row + 1, lax.shift_left(jnp.int32(2), gi)).astype(F32)
        p = win / cnt - u
        y = _dot(p, w_ref[0], _NN)
        dya = dy_ref[...].astype(F32)
        ds_ref[...] = jnp.sum(dya * y, axis=0, keepdims=True)
        dy = dya * s_ref[...]
        dw_ref[0] = _dot(p, dy, _TN)
        dp = _dot(dy, w_ref[0], _NT)
        back, _ = _pool_window_stats(dp / cnt, gi, True)
        du_ref[...] = (back - dp).astype(du_ref.dtype)

    col = pl.BlockSpec((t, c), lambda g: (0, g))
    return pl.pallas_call(
        body, grid=(g_cnt,),
        in_specs=[col, pl.BlockSpec((1, c, c), lambda g: (g, 0, 0)), pl.BlockSpec((1, c), lambda g: (0, g)), col],
        out_specs=[col, pl.BlockSpec((1, c, c), lambda g: (g, 0, 0)), pl.BlockSpec((1, c), lambda g: (0, g))],
        out_shape=[jax.ShapeDtypeStruct((t, g_cnt * c), MXU_DTYPE), jax.ShapeDtypeStruct((g_cnt, c, c), F32),
                   jax.ShapeDtypeStruct((1, g_cnt * c), F32)],
        compiler_params=_cp("parallel"), name=name)(z, w_pool, scale, dycat)


def _chunk_tri(lower):
    r = lax.broadcasted_iota(jnp.int32, (LANES, LANES), 0)
    c = lax.broadcasted_iota(jnp.int32, (LANES, LANES), 1)
    same = (r // HG_CHUNK) == (c // HG_CHUNK)
    return jnp.where(same & ((c <= r) if lower else (c >= r)), 1.0, 0.0).astype(F32)


def _hgrn_prepare(q_ref, f_ref, lb_ref, qh_s, k_s, b_s, qt_s, kt_s, gl_s):
    tb = q_ref.shape[0]
    lb = lb_ref[...]
    sg = _sigmoid(f_ref[...])
    f = lb + (1.0 - lb) * sg
    logf = jnp.log(f)
    qv = q_ref[...]
    qh = qv * _sigmoid(qv) * (HG_HEAD ** -0.5)
    tri = _chunk_tri(True)
    for r in range(tb // LANES):
        rows = slice(r * LANES, (r + 1) * LANES)
        b_s[rows, :] = _dot_f32(tri, logf[rows, :], _NN)
    b = b_s[...]
    b3 = b.reshape(tb // HG_CHUNK, HG_CHUNK, HG_HEAD)
    bl = b3[:, HG_CHUNK - 1:HG_CHUNK, :]
    k = 1.0 - f
    qh_s[...] = qh
    k_s[...] = k
    qt_s[...] = qh * jnp.exp(b)
    kt_s[...] = k * jnp.exp(bl - b3).reshape(tb, HG_HEAD)
    gl_s[...] = jnp.exp(jnp.broadcast_to(bl, b3.shape)).reshape(tb, HG_HEAD)
    return sg, f


SUBLANES = 8
HG_GROUPS = HG_CHUNK // SUBLANES


def _hgrn_intra(qh, kk, bq, rows_a, rows_b):
    ones = jnp.ones((HG_HEAD, HG_HEAD), MXU_DTYPE)
    es, stack_a, stack_b, starts, at = [], [], [], [], 0
    for s in range(HG_CHUNK):
        lo = (s // SUBLANES) * SUBLANES
        e = jnp.exp(jnp.minimum(bq[lo:, :] - bq[s:s + 1, :], 0.0))
        es.append(e)
        stack_a.append(qh[lo:, :] * e * kk[s:s + 1, :])
        if rows_a is not None:
            stack_b.append(rows_a[lo:, :] * rows_b[s:s + 1, :])
        starts.append(at)
        at += HG_CHUNK - lo
    a_rep = _dot(jnp.concatenate(stack_a, axis=0), ones, _NN)
    d_rep = _dot(jnp.concatenate(stack_b, axis=0), ones, _NN) if rows_a is not None else None
    return es, a_rep, d_rep, starts


def _groups(v):
    return [v[g * SUBLANES:(g + 1) * SUBLANES, :] for g in range(HG_GROUPS)]


def _hgrn_fwd(z, lb, hg_norm, ycat, *, name, tb=512):
    t = z.shape[0]
    mix_b = lb.shape[1]
    heads = mix_b // HG_HEAD
    off = (z.shape[1] - 4 * mix_b) // HG_HEAD
    tb = min(tb, t)
    ncb = tb // HG_CHUNK

    def body(q_ref, f_ref, i_ref, g_ref, lb_ref, hn_ref, ycat_in, y_ref, o_ref, st_ref,
             state, qh_s, k_s, b_s, qt_s, kt_s, gl_s, o_s):
        del ycat_in

        @pl.when(pl.program_id(1) == 0)
        def _():
            state[...] = jnp.zeros_like(state)

        _hgrn_prepare(q_ref, f_ref, lb_ref, qh_s, k_s, b_s, qt_s, kt_s, gl_s)
        row = lax.broadcasted_iota(jnp.int32, (SUBLANES, HG_HEAD), 0)

        def chunk(c, carry):
            rows = pl.ds(pl.multiple_of(c * HG_CHUNK, HG_CHUNK), HG_CHUNK)
            st = state[...]
            st_ref[0, c] = st
            vv = i_ref[rows, :]
            o = _groups(_dot(qt_s[rows, :], st, _NT))
            _, a_rep, _, starts = _hgrn_intra(qh_s[rows, :], k_s[rows, :], b_s[rows, :], None, None)
            for s in range(HG_CHUNK):
                g0 = s // SUBLANES
                for g in range(g0, HG_GROUPS):
                    at = starts[s] + (g - g0) * SUBLANES
                    piece = a_rep[at:at + SUBLANES, :] * vv[s:s + 1, :]
                    o[g] = o[g] + (jnp.where(row >= s - g0 * SUBLANES, piece, 0.0) if g == g0 else piece)
            o_s[rows, :] = jnp.concatenate(o, axis=0)
            state[...] = st * gl_s[rows, :][0:1, :] + _dot(vv, kt_s[rows, :], _TN)
            return carry

        lax.fori_loop(0, ncb, chunk, 0, unroll=2)
        o = o_s[...]
        o_ref[...] = o
        r = lax.rsqrt(jnp.mean(o * o, axis=-1, keepdims=True) + EPS)
        gv = g_ref[...]
        y_ref[...] = (o * r * hn_ref[...] * (gv * _sigmoid(gv))).astype(y_ref.dtype)

    def zcol(kind):
        return pl.BlockSpec((tb, HG_HEAD), lambda h, i: (i, off + kind * heads + h))

    scratch = [pltpu.VMEM((HG_HEAD, HG_HEAD), F32)] + [pltpu.VMEM((tb, HG_HEAD), F32)] * 7
    return pl.pallas_call(
        body, grid=(heads, t // tb),
        in_specs=[zcol(0), zcol(1), zcol(2), zcol(3), pl.BlockSpec((1, HG_HEAD), lambda h, i: (0, h)),
                  pl.BlockSpec((1, HG_HEAD), lambda h, i: (0, 0)), pl.BlockSpec(memory_space=pl.ANY)],
        out_specs=[pl.BlockSpec((tb, HG_HEAD), lambda h, i: (i, heads + h)),
                   pl.BlockSpec((tb, HG_HEAD), lambda h, i: (i, h)),
                   pl.BlockSpec((1, ncb, HG_HEAD, HG_HEAD), lambda h, i: (h, i, 0, 0))],
        out_shape=[jax.ShapeDtypeStruct(ycat.shape, ycat.dtype), jax.ShapeDtypeStruct((t, mix_b), F32),
                   jax.ShapeDtypeStruct((heads, t // HG_CHUNK, HG_HEAD, HG_HEAD), F32)],
        scratch_shapes=scratch, input_output_aliases={6: 0},
        compiler_params=_cp("parallel", "arbitrary"), name=name)(z, z, z, z, lb, hg_norm, ycat)


def _hgrn_bwd(z, lb, hg_norm, o_raw, states, dycat, *, name, tb=512):
    t = z.shape[0]
    mix_b = lb.shape[1]
    heads = mix_b // HG_HEAD
    off = (z.shape[1] - 4 * mix_b) // HG_HEAD
    tb = min(tb, t)
    ncb = tb // HG_CHUNK
    nt = t // tb

    def body(q_ref, f_ref, i_ref, g_ref, lb_ref, hn_ref, o_ref, st_ref, dy_ref,
             dq_ref, dfl_ref, di_ref, dg_ref, dlb_ref, dhn_ref,
             dstate, qh_s, k_s, b_s, qt_s, kt_s, gl_s, do_s, dqh_s, dk_s, db_s):
        first = pl.program_id(1) == 0

        @pl.when(first)
        def _():
            dstate[...] = jnp.zeros_like(dstate)
            dlb_ref[...] = jnp.zeros_like(dlb_ref)

        @pl.when(first & (pl.program_id(0) == 0))
        def _():
            dhn_ref[...] = jnp.zeros_like(dhn_ref)

        sg, f = _hgrn_prepare(q_ref, f_ref, lb_ref, qh_s, k_s, b_s, qt_s, kt_s, gl_s)
        o = o_ref[...]
        r = lax.rsqrt(jnp.mean(o * o, axis=-1, keepdims=True) + EPS)
        oh = o * r
        gv = g_ref[...]
        sgg = _sigmoid(gv)
        dy = dy_ref[...].astype(F32)
        hn = hn_ref[...]
        dg_ref[...] = (dy * oh * hn * (sgg * (1.0 + gv * (1.0 - sgg)))).astype(dg_ref.dtype)
        don = dy * (gv * sgg)
        dhn_ref[...] += jnp.sum(don * oh, axis=0, keepdims=True)
        doh = don * hn
        do_s[...] = r * (doh - oh * jnp.mean(doh * oh, axis=-1, keepdims=True))
        row = lax.broadcasted_iota(jnp.int32, (SUBLANES, HG_HEAD), 0)

        def chunk(ci, carry):
            c = ncb - 1 - ci
            rows = pl.ds(pl.multiple_of(c * HG_CHUNK, HG_CHUNK), HG_CHUNK)
            st_prev = st_ref[0, c]
            dst = dstate[...]
            qh, kk, bq, vv = qh_s[rows, :], k_s[rows, :], b_s[rows, :], i_ref[rows, :]
            qt, kt, doo = qt_s[rows, :], kt_s[rows, :], do_s[rows, :]
            gl = gl_s[rows, :][0:1, :]
            es, a_rep, d_rep, starts = _hgrn_intra(qh, kk, bq, doo, vv)
            dqh = _groups(jnp.exp(bq) * _dot(doo, st_prev, _NN))
            dk = _groups(jnp.exp(bq[HG_CHUNK - 1:HG_CHUNK, :] - bq) * _dot(vv, dst, _NN))
            dv = _groups(_dot(kt, dst, _NT))
            qh_g, do_g = _groups(qh), _groups(doo)
            for s in range(HG_CHUNK):
                g0 = s // SUBLANES
                local = s - g0 * SUBLANES
                dk_acc = dv_acc = None
                for g in range(g0, HG_GROUPS):
                    at = (g - g0) * SUBLANES
                    wgt = d_rep[starts[s] + at:starts[s] + at + SUBLANES, :] * es[s][at:at + SUBLANES, :]
                    avo = a_rep[starts[s] + at:starts[s] + at + SUBLANES, :] * do_g[g]
                    if g == g0:
                        wgt = jnp.where(row >= local, wgt, 0.0)
                        avo = jnp.where(row >= local, avo, 0.0)
                    dqh[g] = dqh[g] + wgt * kk[s:s + 1, :]
                    dk_acc = wgt * qh_g[g] if dk_acc is None else dk_acc + wgt * qh_g[g]
                    dv_acc = avo if dv_acc is None else dv_acc + avo
                dk[g0] = dk[g0] + jnp.where(row == local, jnp.sum(dk_acc, axis=0, keepdims=True), 0.0)
                dv[g0] = dv[g0] + jnp.where(row == local, jnp.sum(dv_acc, axis=0, keepdims=True), 0.0)
            dqh, dk, dv = [jnp.concatenate(p, axis=0) for p in (dqh, dk, dv)]
            row_c = lax.broadcasted_iota(jnp.int32, (HG_CHUNK, HG_HEAD), 0)
            st_next = st_prev * gl + _dot(vv, kt, _TN)
            db = qh * dqh - kk * dk
            db = db + jnp.where(row_c == HG_CHUNK - 1, jnp.sum(st_next * dst, axis=0, keepdims=True), 0.0)
            dstate[...] = dst * gl + _dot(doo, qt, _TN)
            dqh_s[rows, :] = dqh
            dk_s[rows, :] = dk
            db_s[rows, :] = db
            di_ref[rows, :] = dv.astype(di_ref.dtype)
            return carry

        lax.fori_loop(0, ncb, chunk, 0, unroll=2)
        tri = _chunk_tri(False)
        lb_v = lb_ref[...]
        qv = q_ref[...]
        sgq = _sigmoid(qv)
        dq_ref[...] = (dqh_s[...] * (HG_HEAD ** -0.5) * (sgq * (1.0 + qv * (1.0 - sgq)))).astype(dq_ref.dtype)
        dlb = jnp.zeros((1, HG_HEAD), F32)
        for rr in range(tb // LANES):
            rws = slice(rr * LANES, (rr + 1) * LANES)
            dlogf = _dot_f32(tri, db_s[rws, :], _NN)
            df = dlogf / f[rws, :] - dk_s[rws, :]
            sgr = sg[rws, :]
            dfl_ref[rws, :] = (df * (1.0 - lb_v) * sgr * (1.0 - sgr)).astype(dfl_ref.dtype)
            dlb = dlb + jnp.sum(df * (1.0 - sgr), axis=0, keepdims=True)
        dlb_ref[...] += dlb

    def zcol(kind):
        return pl.BlockSpec((tb, HG_HEAD), lambda h, i: (nt - 1 - i, off + kind * heads + h))

    hcol = pl.BlockSpec((tb, HG_HEAD), lambda h, i: (nt - 1 - i, h))
    scratch = [pltpu.VMEM((HG_HEAD, HG_HEAD), F32)] + [pltpu.VMEM((tb, HG_HEAD), F32)] * 10
    out = jax.ShapeDtypeStruct((t, mix_b), MXU_DTYPE)
    return pl.pallas_call(
        body, grid=(heads, nt),
        in_specs=[zcol(0), zcol(1), zcol(2), zcol(3), pl.BlockSpec((1, HG_HEAD), lambda h, i: (0, h)),
                  pl.BlockSpec((1, HG_HEAD), lambda h, i: (0, 0)), hcol,
                  pl.BlockSpec((1, ncb, HG_HEAD, HG_HEAD), lambda h, i: (h, nt - 1 - i, 0, 0)),
                  pl.BlockSpec((tb, HG_HEAD), lambda h, i: (nt - 1 - i, heads + h))],
        out_specs=[hcol, hcol, hcol, hcol, pl.BlockSpec((1, HG_HEAD), lambda h, i: (0, h)),
                   pl.BlockSpec((1, HG_HEAD), lambda h, i: (0, 0))],
        out_shape=[out, out, out, out, jax.ShapeDtypeStruct((1, mix_b), F32),
                   jax.ShapeDtypeStruct((1, HG_HEAD), F32)],
        scratch_shapes=scratch, compiler_params=_cp("arbitrary", "arbitrary"),
        name=name)(z, z, z, z, lb, hg_norm, o_raw, states, dycat)


def _lb_fwd(lb_table, layer, *, name):
    rows, width = lb_table.shape

    def body(t_ref, o_ref):
        tv = t_ref[...]
        e = jnp.exp(tv - jnp.max(tv, axis=0, keepdims=True))
        sm = e / jnp.sum(e, axis=0, keepdims=True)
        o_ref[...] = jnp.sum(sm[1:layer + 2, :], axis=0, keepdims=True)

    return pl.pallas_call(body, out_shape=jax.ShapeDtypeStruct((1, width), F32), name=name)(lb_table)


def _lb_bwd(lb_table, dlb, layer, *, name):
    rows, width = lb_table.shape

    def body(t_ref, d_ref, o_ref):
        tv = t_ref[...]
        e = jnp.exp(tv - jnp.max(tv, axis=0, keepdims=True))
        sm = e / jnp.sum(e, axis=0, keepdims=True)
        ridx = lax.broadcasted_iota(jnp.int32, sm.shape, 0)
        dsm = jnp.where((ridx >= 1) & (ridx <= layer + 1), d_ref[...], 0.0)
        o_ref[...] = sm * (dsm - jnp.sum(sm * dsm, axis=0, keepdims=True))

    return pl.pallas_call(body, out_shape=jax.ShapeDtypeStruct((rows, width), F32), name=name)(lb_table, dlb)


FOX_BLOCK = 1024


def _fox_prep(zf, b_f, *, name, blk=256):
    t = zf.shape[0]

    def body(z_ref, b_ref, fc_ref):
        r = lax.broadcasted_iota(jnp.int32, (blk, blk), 0)
        c = lax.broadcasted_iota(jnp.int32, (blk, blk), 1)
        tri = jnp.where(c <= r, 1.0, 0.0).astype(F32)
        carry = jnp.zeros((1, LANES), F32)
        for j in range(t // blk):
            rows = slice(j * blk, (j + 1) * blk)
            ls = jax.nn.log_sigmoid(z_ref[rows, :] + b_ref[...])
            fb = _dot_f32(tri, ls, _NN) + carry
            carry = fb[blk - 1:blk, :]
            fc_ref[rows, :] = fb

    return pl.pallas_call(
        body, out_shape=jax.ShapeDtypeStruct((t, LANES), F32),
        compiler_params=pltpu.CompilerParams(vmem_limit_bytes=VMEM_LIMIT_BYTES), name=name)(zf, b_f)


def _fox_head_column(fc_ref, fk_s, head):
    lane = lax.broadcasted_iota(jnp.int32, fc_ref.shape, 1)
    fk_s[...] = jnp.sum(jnp.where(lane == head, fc_ref[...], 0.0), axis=1, keepdims=True)


def _fox_scores(k_blk, q_blk, fk_blk, diagonal):
    s = _dot(k_blk, q_blk, _NT) * (FOX_HEAD ** -0.5) - fk_blk
    if diagonal:
        key = lax.broadcasted_iota(jnp.int32, s.shape, 0)
        qry = lax.broadcasted_iota(jnp.int32, s.shape, 1)
        s = jnp.where(key <= qry, s, -jnp.inf)
    return s


def _fox_fwd(zqkv, fcol, *, name):
    t = zqkv.shape[0]
    d = zqkv.shape[1] // 3
    heads = d // FOX_HEAD
    blk = min(FOX_BLOCK, t)
    nq = t // blk

    def body(q_ref, k_ref, v_ref, fc_ref, o_ref, lse_ref, fk_s):
        _fox_head_column(fc_ref, fk_s, pl.program_id(0))

        def q_block(qi, carry):
            qrows = pl.ds(pl.multiple_of(qi * blk, blk), blk)
            q_blk = q_ref[qrows, :]

            def update(st, krows, diagonal):
                m, l, acc = st
                s = _fox_scores(k_ref[krows, :], q_blk, fk_s[krows, :], diagonal)
                m_new = jnp.maximum(m, jnp.max(s, axis=0, keepdims=True))
                alpha = jnp.exp(m - m_new)
                p = jnp.exp(s - m_new)
                l = alpha * l + jnp.sum(p, axis=0, keepdims=True)
                acc = acc * alpha + _dot(v_ref[krows, :], p, _TN)
                return m_new, l, acc

            def k_block(kj, st):
                return update(st, pl.ds(pl.multiple_of(kj * blk, blk), blk), False)

            init = (jnp.full((1, blk), -jnp.inf, F32), jnp.zeros((1, blk), F32),
                    jnp.zeros((FOX_HEAD, blk), F32))
            m, l, acc = update(lax.fori_loop(0, qi, k_block, init), qrows, True)
            o_ref[qrows, :] = (acc / l).T.astype(o_ref.dtype)
            lse_ref[0, :, qrows] = m + jnp.log(l)
            return carry

        lax.fori_loop(0, nq, q_block, 0)

    def col(kind):
        return pl.BlockSpec((t, FOX_HEAD), lambda h: (0, kind * heads + h))

    rowvec = pl.BlockSpec((1, 1, t), lambda h: (h, 0, 0))
    return pl.pallas_call(
        body, grid=(heads,),
        in_specs=[col(0), col(1), col(2), pl.BlockSpec((t, LANES), lambda h: (0, 0))],
        out_specs=[pl.BlockSpec((t, FOX_HEAD), lambda h: (0, h)), rowvec],
        out_shape=[jax.ShapeDtypeStruct((t, d), MXU_DTYPE), jax.ShapeDtypeStruct((heads, 1, t), F32)],
        scratch_shapes=[pltpu.VMEM((t, 1), F32)],
        compiler_params=_cp("parallel"), name=name)(zqkv, zqkv, zqkv, fcol)


def _fox_bwd(zqkv, fcol, lse, o, do, *, name):
    t = zqkv.shape[0]
    d = zqkv.shape[1] // 3
    heads = d // FOX_HEAD
    blk = min(FOX_BLOCK, t)
    nq = t // blk
    scale = FOX_HEAD ** -0.5

    def body(q_ref, k_ref, v_ref, fc_ref, lse_ref, o_ref, do_ref,
             dq_ref, dk_ref, dv_ref, rq_ref, rk_ref, dq_s, drow_s, fk_s, acc_s, rk_s):
        _fox_head_column(fc_ref, fk_s, pl.program_id(0))
        dq_s[...] = jnp.zeros_like(dq_s)
        rq_ref[...] = jnp.zeros_like(rq_ref)
        ones_f = jnp.ones((8, FOX_HEAD), F32)
        for j in range(nq):
            rows = slice(j * blk, (j + 1) * blk)
            prod = do_ref[rows, :].astype(F32) * o_ref[rows, :].astype(F32)
            drow_s[:, rows] = _dot_f32(ones_f, prod, _NT)

        def k_block(kj, carry):
            krows = pl.ds(pl.multiple_of(kj * blk, blk), blk)
            k_blk, v_blk, fk_blk = k_ref[krows, :], v_ref[krows, :], fk_s[krows, :]

            def pair(qrows, diagonal):
                q_blk, do_blk = q_ref[qrows, :], do_ref[qrows, :]
                s = _fox_scores(k_blk, q_blk, fk_blk, diagonal)
                p = jnp.exp(s - lse_ref[0, :, qrows])
                acc_s[1] += _dot(p, do_blk, _NN)
                dp = _dot(v_blk, do_blk, _NT)
                ds = (p * (dp - drow_s[0:1, qrows])).astype(MXU_DTYPE)
                acc_s[0] += _dot(ds, q_blk, _NN)
                dq_s[qrows, :] += _dot(ds, k_blk, _TN)
                ds_f = ds.astype(F32)
                rq_ref[0, :, qrows] += jnp.sum(ds_f, axis=0, keepdims=True)
                rk_s[...] += jnp.sum(ds_f, axis=1, keepdims=True)

            def q_block(qi, carry2):
                pair(pl.ds(pl.multiple_of(qi * blk, blk), blk), False)
                return carry2

            acc_s[...] = jnp.zeros_like(acc_s)
            rk_s[...] = jnp.zeros_like(rk_s)
            pair(krows, True)
            lax.fori_loop(kj + 1, nq, q_block, 0)
            dk_ref[krows, :] = (acc_s[0] * scale).astype(dk_ref.dtype)
            dv_ref[krows, :] = acc_s[1].astype(dv_ref.dtype)
            rk_ref[0, :, krows] = jnp.broadcast_to(rk_s[...], (blk, FOX_HEAD)).T[0:1, :]
            return carry

        lax.fori_loop(0, nq, k_block, 0)
        dq_ref[...] = (dq_s[...] * scale).astype(dq_ref.dtype)

    def col(kind):
        return pl.BlockSpec((t, FOX_HEAD), lambda h: (0, kind * heads + h))

    hcol = pl.BlockSpec((t, FOX_HEAD), lambda h: (0, h))
    rowvec = pl.BlockSpec((1, 1, t), lambda h: (h, 0, 0))
    out = jax.ShapeDtypeStruct((t, d), MXU_DTYPE)
    vec = jax.ShapeDtypeStruct((heads, 1, t), F32)
    return pl.pallas_call(
        body, grid=(heads,),
        in_specs=[col(0), col(1), col(2), pl.BlockSpec((t, LANES), lambda h: (0, 0)), rowvec, hcol, hcol],
        out_specs=[hcol, hcol, hcol, rowvec, rowvec],
        out_shape=[out, out, out, vec, vec],
        scratch_shapes=[pltpu.VMEM((t, FOX_HEAD), F32), pltpu.VMEM((8, t), F32), pltpu.VMEM((t, 1), F32),
                        pltpu.VMEM((2, blk, FOX_HEAD), F32), pltpu.VMEM((blk, 1), F32)],
        compiler_params=_cp("parallel"), name=name)(zqkv, zqkv, zqkv, fcol, lse, o, do)


def _fox_gate_bwd(rq, rk, zf, b_f, *, name, blk=256):
    heads, t = rq.shape

    def body(rq_ref, rk_ref, z_ref, b_ref, dfl_ref, db_ref):
        r = lax.broadcasted_iota(jnp.int32, (blk, blk), 0)
        c = lax.broadcasted_iota(jnp.int32, (blk, blk), 1)
        tri = jnp.where(r >= c, 1.0, 0.0).astype(F32)
        carry = jnp.zeros((heads, 1), F32)
        db = jnp.zeros((1, LANES), F32)
        pad = jnp.zeros((LANES - heads, blk), F32)
        for j in reversed(range(t // blk)):
            cols = slice(j * blk, (j + 1) * blk)
            df = rq_ref[:, cols] - rk_ref[:, cols]
            dls = _dot_f32(df, tri, _NN) + carry
            carry = dls[:, 0:1]
            dls_t = jnp.concatenate([dls, pad], axis=0).T
            dfl = dls_t * _sigmoid(-(z_ref[cols, :] + b_ref[...]))
            dfl_ref[cols, :] = dfl.astype(dfl_ref.dtype)
            db = db + jnp.sum(dfl, axis=0, keepdims=True)
        db_ref[...] = db

    return pl.pallas_call(
        body, out_shape=[jax.ShapeDtypeStruct((t, LANES), MXU_DTYPE), jax.ShapeDtypeStruct((1, LANES), F32)],
        compiler_params=pltpu.CompilerParams(vmem_limit_bytes=VMEM_LIMIT_BYTES), name=name)(rq, rk, zf, b_f)


def _adamw(w, m, v, parts, *, name, layer=None, prev=None, tr=256):
    lcnt, r, c = w.shape
    p = parts.shape[0]
    li = 0 if layer is None else layer
    tr = _tile(r, tr, 16)
    tc = c if tr * c <= 256 * 2048 else _tile(c, 256)
    has_prev = prev is not None

    def body(*refs):
        w_ref, m_ref, v_ref, p_ref = refs[:4]
        g_ref, d_ref, nm_ref, nv_ref = refs[-4:]
        g = p_ref[0].astype(F32)
        for j in range(1, p):
            g = g + p_ref[j].astype(F32)
        wv = w_ref[0]
        mn = ADAM_B1 * m_ref[0] + (1.0 - ADAM_B1) * g
        vn = ADAM_B2 * v_ref[0] + (1.0 - ADAM_B2) * (g * g)
        m_hat = mn / (1.0 - ADAM_B1 ** ADAM_STEP)
        v_hat = vn / (1.0 - ADAM_B2 ** ADAM_STEP)
        g_ref[0] = g
        d_ref[0] = -ADAM_LR * (m_hat / (jnp.sqrt(v_hat) + ADAM_EPS) + ADAM_WD * wv)
        nm_ref[0] = mn
        nv_ref[0] = vn

    slab = pl.BlockSpec((1, tr, tc), lambda i, j: (li, i, j))
    in_specs = [slab, slab, slab, pl.BlockSpec((p, tr, tc), lambda i, j: (0, i, j))]
    operands = [w, m, v, parts]
    aliases = {}
    if has_prev:
        in_specs += [pl.BlockSpec(memory_space=pl.ANY)] * 4
        operands += list(prev)
        aliases = {4: 0, 5: 1, 6: 2, 7: 3}
    shp = jax.ShapeDtypeStruct((lcnt, r, c), F32)
    return pl.pallas_call(
        body, grid=(r // tr, c // tc), in_specs=in_specs, out_specs=[slab] * 4, out_shape=[shp] * 4,
        input_output_aliases=aliases, compiler_params=_cp("parallel", "parallel"), name=name)(*operands)


def _my_place():
    return lax.axis_index("x"), lax.axis_index("y"), lax.axis_index("c")


def _slot(p):
    return 4 * p[0] + 2 * p[1] + p[2]


def _peer(me, mask):
    x, y, c = me
    return (1 - x if mask & 4 else x, 1 - y if mask & 2 else y, 1 - c if mask & 1 else c)


_HBM = pl.BlockSpec(memory_space=pltpu.HBM)
_SEM = pl.BlockSpec(memory_space=pltpu.SEMAPHORE)
_ANY = pl.BlockSpec(memory_space=pl.ANY)
_EFFECT = pltpu.SideEffectType.DATAFLOW_SIDE_EFFECTING


def _push_copy(src_refs, land_refs, send_sems, recv_sems, a, mask, me, per_peer, outgoing):
    peer = _peer(me, mask)
    src = src_refs[a].at[_slot(peer)] if per_peer else src_refs[a]
    dst = land_refs[a].at[_slot(me) if outgoing else _slot(peer)]
    k = a * (N_DEV - 1) + mask - 1
    return pltpu.make_async_remote_copy(
        src_ref=src, dst_ref=dst, send_sem=send_sems.at[k], recv_sem=recv_sems.at[k],
        device_id=peer, device_id_type=MESH)


ALL_PEERS = tuple(range(1, N_DEV))
CHIP_PEERS = (2, 4, 6)
FIRST_HOP = (1,) + CHIP_PEERS


def _push_start(srcs, dep, *, per_peer, name, masks=ALL_PEERS):
    n = len(srcs)
    mine = _slot(_my_place())
    lands = []
    for s in srcs:
        own = lax.dynamic_index_in_dim(s, mine, 0, keepdims=True) if per_peer else s[None]
        shape = s.shape if per_peer else (N_DEV,) + s.shape
        lands.append(lax.dynamic_update_slice_in_dim(lax.empty(shape, s.dtype), own, mine, 0))
    has_dep = dep is not None

    def body(*refs):
        src_refs, land_refs = refs[:n], refs[n:2 * n]
        send_sems, recv_sems = refs[2 * n + has_dep], refs[2 * n + has_dep + 1]
        token = refs[-1]
        me = _my_place()
        for a in range(n):
            for mask in masks:
                _push_copy(src_refs, land_refs, send_sems, recv_sems, a, mask, me, per_peer, True).start()
        token[...] = jnp.zeros_like(token)

    hbm_in = [pltpu.with_memory_space_constraint(v, pltpu.HBM) for v in list(srcs) + lands]
    out = pl.pallas_call(
        body, name=name,
        out_shape=(pltpu.SemaphoreType.DMA((n * (N_DEV - 1),)), pltpu.SemaphoreType.DMA((n * (N_DEV - 1),)),
                   *[pltpu.HBM(v.shape, v.dtype) for v in hbm_in], jax.ShapeDtypeStruct((8, LANES), F32)),
        in_specs=[_HBM] * (2 * n) + ([_ANY] if has_dep else []),
        out_specs=(_SEM, _SEM, *[_HBM] * (2 * n), pl.BlockSpec(memory_space=pltpu.VMEM)),
        input_output_aliases={i: 2 + i for i in range(2 * n)},
        compiler_params=pltpu.CompilerParams(has_side_effects=_EFFECT),
    )(*hbm_in, *([dep] if has_dep else []))
    return (n, per_peer, masks, out[:-1]), out[-1]


def _push_wait(handle, after, *, name):
    n, per_peer, masks, (send_sems, recv_sems, *bufs) = handle

    def body(*refs):
        src_refs, land_refs = refs[:n], refs[n:2 * n]
        send_sems, recv_sems = refs[2 * n], refs[2 * n + 1]
        me = _my_place()
        for a in range(n):
            for mask in masks:
                cp = _push_copy(src_refs, land_refs, send_sems, recv_sems, a, mask, me, per_peer, False)
                cp.wait_send()
                cp.wait_recv()

    out = pl.pallas_call(
        body, name=name, out_shape=tuple(pltpu.HBM(v.shape, v.dtype) for v in bufs),
        in_specs=[_HBM] * (2 * n) + [_SEM, _SEM, _ANY], out_specs=tuple([_HBM] * (2 * n)),
        input_output_aliases={i: i for i in range(2 * n)},
        compiler_params=pltpu.CompilerParams(has_side_effects=_EFFECT),
    )(*bufs, send_sems, recv_sems, after)
    return list(out[n:])


def _relay_copy(land_refs, send_sems, recv_sems, a, j, me, outgoing):
    sibling = _peer(me, 1)
    out_slot = _slot(_peer(me, CHIP_PEERS[j]))
    in_slot = _slot(_peer(sibling, CHIP_PEERS[j]))
    k = a * len(CHIP_PEERS) + j
    return pltpu.make_async_remote_copy(
        src_ref=land_refs[a].at[out_slot], dst_ref=land_refs[a].at[out_slot if outgoing else in_slot],
        send_sem=send_sems.at[k], recv_sem=recv_sems.at[k], device_id=sibling, device_id_type=MESH)


def _relay_start(lands, *, name):
    n = len(lands)

    def body(*refs):
        land_refs, send_sems, recv_sems, token = refs[:n], refs[n], refs[n + 1], refs[-1]
        me = _my_place()
        for a in range(n):
            for j in range(len(CHIP_PEERS)):
                _relay_copy(land_refs, send_sems, recv_sems, a, j, me, True).start()
        token[...] = jnp.zeros_like(token)

    hbm_in = [pltpu.with_memory_space_constraint(v, pltpu.HBM) for v in lands]
    n_sem = n * len(CHIP_PEERS)
    out = pl.pallas_call(
        body, name=name,
        out_shape=(pltpu.SemaphoreType.DMA((n_sem,)), pltpu.SemaphoreType.DMA((n_sem,)),
                   *[pltpu.HBM(v.shape, v.dtype) for v in hbm_in], jax.ShapeDtypeStruct((8, LANES), F32)),
        in_specs=[_HBM] * n, out_specs=(_SEM, _SEM, *[_HBM] * n, pl.BlockSpec(memory_space=pltpu.VMEM)),
        input_output_aliases={i: 2 + i for i in range(n)},
        compiler_params=pltpu.CompilerParams(has_side_effects=_EFFECT),
    )(*hbm_in)
    return (n, out[:-1]), out[-1]


def _relay_wait(handle, after, *, name):
    n, (send_sems, recv_sems, *bufs) = handle

    def body(*refs):
        land_refs, send_sems, recv_sems = refs[:n], refs[n], refs[n + 1]
        me = _my_place()
        for a in range(n):
            for j in range(len(CHIP_PEERS)):
                cp = _relay_copy(land_refs, send_sems, recv_sems, a, j, me, False)
                cp.wait_send()
                cp.wait_recv()

    out = pl.pallas_call(
        body, name=name, out_shape=tuple(pltpu.HBM(v.shape, v.dtype) for v in bufs),
        in_specs=[_HBM] * n + [_SEM, _SEM, _ANY], out_specs=tuple([_HBM] * n),
        input_output_aliases={i: i for i in range(n)},
        compiler_params=pltpu.CompilerParams(has_side_effects=_EFFECT),
    )(*bufs, send_sems, recv_sems, after)
    return list(out)


def _all_reduce_rows(v, *, name):
    r, c = v.shape

    def body(v_ref, o_ref, buf, send_sems, recv_sems):
        me = _my_place()
        mine = _slot(me)
        sends = []
        for mask in range(1, N_DEV):
            peer = _peer(me, mask)
            sends.append(pltpu.make_async_remote_copy(
                src_ref=v_ref, dst_ref=buf.at[mine], send_sem=send_sems.at[mask - 1],
                recv_sem=recv_sems.at[mask - 1], device_id=peer, device_id_type=MESH))
        for cp in sends:
            cp.start()
        buf[mine] = v_ref[...]
        for mask in range(1, N_DEV):
            peer = _peer(me, mask)
            pltpu.make_async_remote_copy(
                src_ref=v_ref, dst_ref=buf.at[_slot(peer)], send_sem=send_sems.at[mask - 1],
                recv_sem=recv_sems.at[mask - 1], device_id=peer, device_id_type=MESH).wait_recv()
        for cp in sends:
            cp.wait_send()
        total = buf[0]
        for j in range(1, N_DEV):
            total = total + buf[j]
        o_ref[...] = total

    vm = pl.BlockSpec(memory_space=pltpu.VMEM)
    return pl.pallas_call(
        body, in_specs=[vm], out_specs=vm, out_shape=jax.ShapeDtypeStruct((r, c), F32),
        scratch_shapes=[pltpu.VMEM((N_DEV, r, c), F32), pltpu.SemaphoreType.DMA((7,)),
                        pltpu.SemaphoreType.DMA((7,))],
        name=name)(v)


def _xa_fwd(x, memn, g_x, wq, wkv, wo, tag):
    hx = _rms_fwd(x, g_x, name=f"xa{tag}_norm")
    q = _mm2(hx, wq, "nn", MXU_DTYPE, name=f"xa{tag}_q")
    kv = _mm2(memn, wkv, "nn", MXU_DTYPE, name=f"xa{tag}_kv", b_split=True)
    o = _xattn_fwd(q, kv, name=f"xa{tag}_attn")
    return _mm2(o, wo, "nn", F32, name=f"xa{tag}_out", add=x), (hx, memn, q, kv, o)


def _xa_bwd(x, mem, g_x, g_m, wq, wkv, wo, saved, dxo, dxo_lo, tag, put):
    hx, memn, q, kv, o = saved
    do = _mm2(dxo_lo, wo, "nt", MXU_DTYPE, name=f"xa{tag}_do")
    dwo = _mm2(o, dxo_lo, "tn", MXU_DTYPE, name=f"xa{tag}_dwo")
    dq, dkv = _xattn_bwd(q, kv, do, name=f"xa{tag}_attn_bwd")
    dwq = _mm2(hx, dq, "tn", MXU_DTYPE, name=f"xa{tag}_dwq")
    dwkv = _mm2(memn, dkv, "tn", MXU_DTYPE, name=f"xa{tag}_dwkv", o_split=True, tn=wkv.shape[2])
    tok = put((dwq, dwkv, dwo))
    dhx = _mm2(dq, wq, "nt", F32, name=f"xa{tag}_dh", dep=tok)
    dx, dx_lo, dgx = _rms_bwd(x, g_x, dhx, dxo, name=f"xa{tag}_norm_bwd")
    dmemn = _mm2(dkv, wkv, "nt", F32, name=f"xa{tag}_dmem", b_split=True)
    _, _, dgm = _rms_bwd(mem, g_m, dmemn, None, name=f"xa{tag}_mem_norm_bwd")
    return dx, dx_lo, dgx, dgm


def _ffn_fwd(x, g, wg, wu, wd, tag):
    h = _rms_fwd(x, g, name=f"ffn{tag}_norm")
    gate, up, act = _ffn_up(h, wg, wu, name=f"ffn{tag}_up")
    return _ffn_down(act, wd, x, name=f"ffn{tag}_down"), (h, gate, up, act)


def _ffn_bwd(x, g, wg, wu, wd, saved, dxo, dxo_lo, tag, put):
    h, gate, up, act = saved
    dwd = _mm(act, dxo_lo[None], "tn", MXU_DTYPE, name=f"ffn{tag}_dwd")
    dgate, dup = _ffn_dact(dxo_lo, wd, gate, up, name=f"ffn{tag}_dact")
    dwg = _mm(dgate, h[None], "tn", MXU_DTYPE, name=f"ffn{tag}_dwg")
    dwu = _mm(dup, h[None], "tn", MXU_DTYPE, name=f"ffn{tag}_dwu")
    tok = put((dwg, dwu, dwd))
    dh = _ffn_dh(dgate, dup, wg, wu, tok, name=f"ffn{tag}_dh")
    dx, dx_lo, dg = _rms_bwd(x, g, dh, dxo, name=f"ffn{tag}_norm_bwd")
    return dx, dx_lo, dg


def _even_fwd(x, h, lb, w_in, w_pool, pool_scale, hg_norm, w_out):
    z = _mm2(h, w_in, "nn", F32, name="ev_in", b_split=True)
    ycat = _pool_fwd(z, w_pool, pool_scale, name="ev_pool")
    ycat, o_raw, states = _hgrn_fwd(z, lb, hg_norm, ycat, name="ev_hgrn")
    return _mm2(ycat, w_out, "nn", F32, name="ev_out", add=x), (h, z, ycat, o_raw, states)


def _even_bwd(x, g, lb, w_in, w_pool, pool_scale, hg_norm, w_out, saved, dxo, dxo_lo, put):
    h, z, ycat, o_raw, states = saved
    dycat = _mm2(dxo_lo, w_out, "nt", MXU_DTYPE, name="ev_dy")
    dw_out = _mm2(ycat, dxo_lo, "tn", MXU_DTYPE, name="ev_dw_out")
    du, dw_pool, dscale = _pool_bwd(z, w_pool, pool_scale, dycat, name="ev_pool_bwd")
    dq, dfl, di, dg, dlb, dhn = _hgrn_bwd(z, lb, hg_norm, o_raw, states, dycat, name="ev_hgrn_bwd")
    dz = jnp.concatenate([du, dq, dfl, di, dg], axis=1)
    dw_in = _mm2(h, dz, "tn", MXU_DTYPE, name="ev_dw_in", o_split=True, tn=w_in.shape[2])
    tok = put((dw_in, dw_pool, dw_out))
    dh = _mm_nt_split(dz, w_in, tok, name="ev_dh")
    dx, dx_lo, dgn = _rms_bwd(x, g, dh, dxo, name="ev_norm_bwd")
    return dx, dx_lo, dict(ev_norm=dgn, ev_pool_scale=dscale, ev_hg_norm=dhn, lb=dlb)


def _odd_fwd(x, g, w_qkv, w_f, b_f, w_out):
    n_qkv = 3 * x.shape[1]
    h = _rms_fwd(x, g, name="od_norm")
    zqkv = _mm2(h, w_qkv, "nt", MXU_DTYPE, name="od_qkv", n_b=n_qkv)
    zf = _mm2(h, w_f, "nt", F32, name="od_gate")
    fcol = _fox_prep(zf, b_f, name="od_fox_prep")
    o, lse = _fox_fwd(zqkv, fcol, name="od_fox")
    return _mm2(o, w_out, "nn", F32, name="od_out", add=x), (h, zqkv, zf, fcol, o, lse)


def _odd_bwd(x, g, w_qkv, w_f, b_f, w_out, saved, dxo, dxo_lo, put):
    h, zqkv, zf, fcol, o, lse = saved
    do = _mm2(dxo_lo, w_out, "nt", MXU_DTYPE, name="od_do")
    dw_out = _mm2(o, dxo_lo, "tn", MXU_DTYPE, name="od_dw_out")
    dq, dk, dv, rq, rk = _fox_bwd(zqkv, fcol, lse, o, do, name="od_fox_bwd")
    dfl, db_f = _fox_gate_bwd(rq[:, 0, :], rk[:, 0, :], zf, b_f, name="od_fox_gate_bwd")
    dz = jnp.concatenate([dq, dk, dv], axis=1)
    dw_qkv = _mm2(dz, h, "tn", MXU_DTYPE, name="od_dw_qkv")
    dw_f = _mm2(dfl, h, "tn", MXU_DTYPE, name="od_dw_gate")
    tok = put((dw_qkv, dw_f, dw_out))
    dh = _mm2(dz, w_qkv, "nn", F32, name="od_dh_qkv", dep=tok)
    dh = _mm2(dfl, w_f, "nn", F32, name="od_dh_gate", add=dh)
    dx, dx_lo, dgn = _rms_bwd(x, g, dh, dxo, name="od_norm_bwd")
    return dx, dx_lo, dict(od_norm=dgn, od_b_f=db_f)


def _local_step(x, mem, target, sp, get_w, put_dw):
    b_f = jnp.pad(sp["od_b_f"], ((0, 0), (0, LANES - sp["od_b_f"].shape[1])))
    lb = _lb_fwd(sp["lb_table"], 0, name="lb_fwd")
    fin = sp["final_norm"].reshape(1, -1)
    xn, xm, fn = sp["xa_norm"], sp["xa_mem_norm"], sp["ffn_norm"]
    h0 = _rms_fwd(x, sp["ev_norm"], name="ev_norm")
    memn = [_rms_fwd(mem, xm[l:l + 1], name=f"xa{l}_mem_norm") for l in range(xm.shape[0])]
    w_ev = get_w("ev", h0)
    x1, s_ev = _even_fwd(x, h0, lb, w_ev[0], w_ev[1], sp["ev_pool_scale"], sp["ev_hg_norm"], w_ev[2])
    w_xa0 = get_w("xa0", x1)
    x2, s_xa0 = _xa_fwd(x1, memn[0], xn[0:1], *w_xa0, 0)
    w_ff0 = get_w("ffn0", x2)
    x3, s_ff0 = _ffn_fwd(x2, fn[0:1], *w_ff0, 0)
    w_qkv, w_f, w_od_out, od_norm = get_w("od", x3)
    x4, s_od = _odd_fwd(x3, od_norm, w_qkv, w_f, b_f, w_od_out)
    w_xa1 = get_w("xa1", x4)
    x5, s_xa1 = _xa_fwd(x4, memn[1], xn[1:2], *w_xa1, 1)
    w_ff1 = get_w("ffn1", x5)
    x6, s_ff1 = _ffn_fwd(x5, fn[1:2], *w_ff1, 1)
    loss, dx, dx_lo, d_fin = _loss_head(x6, fin, target, name="loss_head")
    put = lambda grp: functools.partial(put_dw, grp)
    dx, dx_lo, d_ffn1 = _ffn_bwd(x5, fn[1:2], *w_ff1, s_ff1, dx, dx_lo, 1, put("ffn1"))
    dx, dx_lo, d_xa1, d_xm1 = _xa_bwd(x4, mem, xn[1:2], xm[1:2], *w_xa1, s_xa1, dx, dx_lo, 1, put("xa1"))
    dx, dx_lo, d_od = _odd_bwd(x3, od_norm, w_qkv, w_f, b_f, w_od_out, s_od, dx, dx_lo, put("od"))
    dx, dx_lo, d_ffn0 = _ffn_bwd(x2, fn[0:1], *w_ff0, s_ff0, dx, dx_lo, 0, put("ffn0"))
    dx, dx_lo, d_xa0, d_xm0 = _xa_bwd(x1, mem, xn[0:1], xm[0:1], *w_xa0, s_xa0, dx, dx_lo, 0, put("xa0"))
    dx, _, d_ev = _even_bwd(x, sp["ev_norm"], lb, w_ev[0], w_ev[1], sp["ev_pool_scale"], sp["ev_hg_norm"],
                            w_ev[2], s_ev, dx, dx_lo, put("ev"))
    small = dict(
        lb_table=_lb_bwd(sp["lb_table"], d_ev["lb"], 0, name="lb_bwd"),
        ev_norm=d_ev["ev_norm"], ev_pool_scale=d_ev["ev_pool_scale"], ev_hg_norm=d_ev["ev_hg_norm"],
        od_norm=d_od["od_norm"], od_b_f=d_od["od_b_f"][:, :sp["od_b_f"].shape[1]],
        xa_norm=jnp.concatenate([d_xa0, d_xa1], axis=0), xa_mem_norm=jnp.concatenate([d_xm0, d_xm1], axis=0),
        ffn_norm=jnp.concatenate([d_ffn0, d_ffn1], axis=0), final_norm=d_fin.reshape(-1))
    return loss, dx, small


_SMALL = ("lb_table", "ev_norm", "ev_pool_scale", "ev_hg_norm", "od_norm", "od_b_f", "xa_norm", "xa_mem_norm",
          "ffn_norm", "final_norm")
_WEIGHTS = ("lb_table", "ev_norm", "ev_w_in", "ev_w_pool", "ev_pool_scale", "ev_hg_norm", "ev_w_out", "od_norm",
            "od_w_in", "od_b_f", "od_w_out", "xa_norm", "xa_mem_norm", "xa_wq", "xa_wkv", "xa_wo", "ffn_norm",
            "ffn_w_gate", "ffn_w_up", "ffn_w_down", "final_norm")


def _lo(a):
    return a.astype(MXU_DTYPE)


def _rows(v):
    flat = v.reshape(-1)
    return jnp.pad(flat, (0, (-flat.shape[0]) % LANES)).reshape(-1, LANES)


def kernel(x, mem, lb_table, ev_norm, ev_w_in, ev_w_pool, ev_pool_scale, ev_hg_norm, ev_w_out, od_norm, od_w_in, od_b_f, od_w_out, xa_norm, xa_mem_norm, xa_wq, xa_wkv, xa_wo, ffn_norm, ffn_w_gate, ffn_w_up, ffn_w_down, final_norm, loss_target, m_lb_table, m_ev_norm, m_ev_w_in, m_ev_w_pool, m_ev_pool_scale, m_ev_hg_norm, m_ev_w_out, m_od_norm, m_od_w_in, m_od_b_f, m_od_w_out, m_xa_norm, m_xa_mem_norm, m_xa_wq, m_xa_wkv, m_xa_wo, m_ffn_norm, m_ffn_w_gate, m_ffn_w_up, m_ffn_w_down, m_final_norm, v_lb_table, v_ev_norm, v_ev_w_in, v_ev_w_pool, v_ev_pool_scale, v_ev_hg_norm, v_ev_w_out, v_od_norm, v_od_w_in, v_od_b_f, v_od_w_out, v_xa_norm, v_xa_mem_norm, v_xa_wq, v_xa_wkv, v_xa_wo, v_ffn_norm, v_ffn_w_gate, v_ffn_w_up, v_ffn_w_down, v_final_norm):
    arg = dict(locals())
    d = x.shape[-1]
    layers = xa_wq.shape[0]
    me = _slot(_my_place())

    n_gate = od_b_f.shape[1]
    turned = {k: jnp.swapaxes(arg[k], 1, 2) for k in ("od_w_in", "ffn_w_gate", "ffn_w_up")}
    raw = dict(ev=[ev_w_in[0], ev_w_pool[0], ev_w_out[0]], od=[turned["od_w_in"][0], od_w_out[0], od_norm])
    for l in range(layers):
        raw[f"xa{l}"] = [xa_wq[l], xa_wkv[l], xa_wo[l]]
        raw[f"ffn{l}"] = [turned["ffn_w_gate"][l], turned["ffn_w_up"][l], ffn_w_down[l]]
    order = ("ev", "xa0", "ffn0", "od", "xa1", "ffn1")
    gathers, relays, tok = {}, {}, None
    for grp in order:
        srcs = [w if tok is None else w + tok[0, 0] for w in raw[grp]]
        srcs = [w if grp == "od" and j == 2 else _lo(w) for j, w in enumerate(srcs)]
        gathers[grp], tok = _push_start(srcs, None, per_peer=False, masks=FIRST_HOP, name=f"gather_{grp}_start")
    last_start = tok

    def second_hop(grp, after):
        lands = _push_wait(gathers[grp], after, name=f"gather_{grp}_wait")
        relays[grp], token = _relay_start(lands, name=f"gather_{grp}_relay")
        return token

    def get_w(grp, after):
        i = order.index(grp)
        if i == 0:
            after = last_start if after is None else last_start + after[:8, :LANES].astype(F32)
        if grp not in relays:
            after = second_hop(grp, after)
        if 1 <= i < len(order) - 1:
            after = second_hop(order[i + 1], after)
        got = _relay_wait(relays[grp], after, name=f"gather_{grp}_relay_wait")
        if grp == "ev":
            w_in, w_pool, w_out = got
            w_pool = jnp.transpose(w_pool, (1, 0, 2, 3)).reshape(w_pool.shape[1], -1, w_pool.shape[3])
            return w_in, w_pool, w_out.reshape(d, d)
        if grp == "od":
            w_in, w_out, nrm = got
            assert n_gate <= w_in.shape[1]
            w_f = jnp.pad(w_in[N_DEV - 1, w_in.shape[1] - n_gate:], ((0, LANES - n_gate), (0, 0)))
            return w_in.reshape(-1, d), w_f, w_out.reshape(d, d), nrm.reshape(1, d)
        if grp.startswith("xa"):
            return got[0].reshape(d, d), got[1], got[2].reshape(d, d)
        return tuple(got)

    def row_parts(g):
        return g.reshape(N_DEV, -1, g.shape[-1])

    scatters = {}

    def put_dw(grp, dws):
        if grp == "ev":
            dw_in, dw_pool, dw_out = dws
            gc = dw_pool.shape[1] // N_DEV
            dw_pool = _lo(jnp.transpose(dw_pool.reshape(dw_pool.shape[0], N_DEV, gc, -1), (1, 0, 2, 3)))
            parts = [dw_in, dw_pool, row_parts(dw_out)]
        elif grp == "od":
            dw_qkv, dw_f, dw_out = dws
            parts = [row_parts(jnp.concatenate([dw_qkv, dw_f[:n_gate]], axis=0)), row_parts(dw_out)]
        elif grp.startswith("xa"):
            parts = [row_parts(dws[0]), dws[1], row_parts(dws[2])]
        else:
            parts = list(dws)
        scatters[grp], token = _push_start(parts, None, per_peer=True, name=f"scatter_{grp}_start")
        return token

    sp = {k: arg[k] for k in _SMALL if k != "od_norm"}
    loss, dx, small = _local_step(x[0], mem[0], loss_target[0], sp, get_w, put_dw)

    pieces = [_rows(small[k]) for k in _SMALL]
    packed = jnp.concatenate(pieces + [_rows(loss)], axis=0)
    packed = jnp.pad(packed, ((0, (-packed.shape[0]) % 8), (0, 0)))
    total = _all_reduce_rows(packed, name="all_reduce_small")
    loss = total[sum(pc.shape[0] for pc in pieces), 0]
    small_g, at = {}, 0
    for k, pc in zip(_SMALL, pieces):
        n = small[k].size
        small_g[k] = total[at:at + pc.shape[0]].reshape(-1)[:n].reshape(small[k].shape)
        at += pc.shape[0]
    small_g["od_norm"] = lax.dynamic_slice_in_dim(small_g["od_norm"], me * od_norm.shape[1], od_norm.shape[1], axis=1)

    res = {}
    for k in _SMALL:
        w, m, v = arg[k], arg["m_" + k], arg["v_" + k]
        shp = (1, 1, w.shape[0]) if w.ndim == 1 else (1,) + w.shape
        out = _adamw(w.reshape(shp), m.reshape(shp), v.reshape(shp), small_g[k].reshape(shp), name=f"adamw_{k}")
        res[k] = [o.reshape(w.shape) for o in out]
    members = dict(ev=("ev_w_in", "ev_w_pool", "ev_w_out"), od=("od_w_in", "od_w_out"),
                   xa=("xa_wq", "xa_wkv", "xa_wo"), ffn=("ffn_w_gate", "ffn_w_up", "ffn_w_down"))
    after, stacked = dx, {}
    for grp in ("ffn1", "xa1", "od", "ffn0", "xa0", "ev"):
        got = _push_wait(scatters[grp], after, name=f"scatter_{grp}_wait")
        kind = grp.rstrip("01")
        for k, parts in zip(members[kind], got):
            w, m, v = [jnp.swapaxes(a, 1, 2) if k in turned else a for a in (arg[k], arg["m_" + k], arg["v_" + k])]
            if w.shape[0] == 1:
                shp = (1, -1, w.shape[-1])
                out = _adamw(w.reshape(shp), m.reshape(shp), v.reshape(shp), parts.reshape(N_DEV, -1, w.shape[-1]),
                             name=f"adamw_{k}")
            else:
                out = _adamw(w, m, v, parts, name=f"adamw_{k}{grp[-1]}", layer=int(grp[-1]), prev=stacked.get(k))
                stacked[k] = out
            res[k] = [jnp.swapaxes(o.reshape(w.shape), 1, 2) if k in turned else o.reshape(w.shape) for o in out]
            after = out[3][:1, :8, :LANES]

    outs = [loss, dx[None]]
    for j in range(4):
        outs += [res[k][j] for k in _WEIGHTS]
    return tuple(outs)
```

```python
import functools

import jax
import jax.numpy as jnp
from jax import lax
from jax.experimental import pallas as pl
from jax.experimental.pallas import tpu as pltpu

F32 = jnp.float32
MXU_DTYPE = jnp.bfloat16
EPS = 1e-6
N_DEV = 8
V7X_VMEM_BYTES = 64 * 1024 * 1024
VMEM_LIMIT_BYTES = V7X_VMEM_BYTES - 8 * 1024 * 1024
LANES = 128
HIGHEST = lax.Precision.HIGHEST
MESH = pl.DeviceIdType.MESH

HG_HEAD = 128
HG_CHUNK = 32
FOX_HEAD = 128
XA_HEADS = 4
POOL_GROUPS = 4

ADAM_LR = 0.001
ADAM_B1 = 0.9
ADAM_B2 = 0.999
ADAM_EPS = 1e-08
ADAM_WD = 0.01
ADAM_STEP = 10

_NN = ((1,), (0,))
_NT = ((1,), (1,))
_TN = ((0,), (0,))


def _dot(a, b, dims):
    return lax.dot_general(a.astype(MXU_DTYPE), b.astype(MXU_DTYPE), (dims, ((), ())),
                           preferred_element_type=F32)


def _dot_f32(a, b, dims):
    return lax.dot_general(a, b, (dims, ((), ())), preferred_element_type=F32, precision=HIGHEST)


def _cp(*sem):
    return pltpu.CompilerParams(dimension_semantics=sem, vmem_limit_bytes=VMEM_LIMIT_BYTES)


def _tile(n, pref, align=LANES):
    if n <= pref:
        return n
    t = (pref // align) * align
    while t >= align:
        if n % t == 0:
            return t
        t -= align
    return n


def _sigmoid(x):
    return jax.nn.sigmoid(x)


def _mm(a, b, mode, out_dtype, *, name, add=None, dep=None, reduce_b=False, b_split=False, o_split=False,
        n_b=None, tm=1024, tn=1024, tk=2048):
    ba, bb = a.shape[0], b.shape[0]
    if mode == "tn":
        kdim, m = a.shape[1], a.shape[2]
        tk = 2 * tk
    else:
        m, kdim = a.shape[1], a.shape[2]
    if b_split:
        s_cnt, b_rows, w = b.shape
        if mode == "nt":
            n = b_rows
            assert kdim == s_cnt * w
            tk = w
        else:
            n = s_cnt * w
            assert b_rows == kdim
            tn = w
        nb = ba
    else:
        n = b.shape[1] if mode == "nt" else b.shape[2]
        n = n if n_b is None else n_b
        nb = max(ba, bb)
    if not (b_split and mode != "nt"):
        tn = _tile(n, tn)
    if not (b_split and mode == "nt"):
        tk = _tile(kdim, tk)
    tm = _tile(m, tm)
    assert m % tm == 0 and n % tn == 0 and kdim % tk == 0, (name, m, n, kdim, tm, tn, tk)
    nk = kdim // tk
    if reduce_b:
        grid = (m // tm, n // tn, nb, nk)
        unpack = lambda i, j, bi, k: (bi, i, j, k)
        sem = ("parallel", "parallel", "arbitrary", "arbitrary")
        nred = nb * nk
    else:
        grid = (nb, m // tm, n // tn, nk)
        unpack = lambda bi, i, j, k: (bi, i, j, k)
        sem = ("parallel", "parallel", "parallel", "arbitrary")
        nred = nk

    def a_map(*g):
        bi, i, j, k = unpack(*g)
        ab = bi if ba > 1 else 0
        return (ab, k, i) if mode == "tn" else (ab, i, k)

    def b_map(*g):
        bi, i, j, k = unpack(*g)
        if b_split:
            return (k, j, 0) if mode == "nt" else (j, k, 0)
        bq = bi if bb > 1 else 0
        return (bq, j, k) if mode == "nt" else (bq, k, j)

    def o_map(*g):
        bi, i, j, k = unpack(*g)
        if o_split:
            return (j, i, 0)
        return (0 if reduce_b else bi, i, j)

    a_blk = (1, tk, tm) if mode == "tn" else (1, tm, tk)
    b_blk = (1, tn, tk) if mode == "nt" else (1, tk, tn)
    dims = {"nn": _NN, "nt": _NT, "tn": _TN}[mode]
    has_add = add is not None

    def body(*refs):
        a_ref, b_ref = refs[:2]
        if has_add:
            add_ref = refs[2]
        if nred == 1:
            o_ref = refs[-1]
            r = _dot(a_ref[0], b_ref[0], dims)
            if has_add:
                r = r + add_ref[0].astype(F32)
            o_ref[0] = r.astype(o_ref.dtype)
            return
        o_ref, acc_ref = refs[-2:]
        if reduce_b:
            step = pl.program_id(2) * nk + pl.program_id(3)
        else:
            step = pl.program_id(3)

        @pl.when(step == 0)
        def _():
            acc_ref[...] = _dot(a_ref[0], b_ref[0], dims)

        @pl.when(step > 0)
        def _():
            acc_ref[...] += _dot(a_ref[0], b_ref[0], dims)

        @pl.when(step == nred - 1)
        def _():
            r = acc_ref[...]
            if has_add:
                r = r + add_ref[0].astype(F32)
            o_ref[0] = r.astype(o_ref.dtype)

    in_specs = [pl.BlockSpec(a_blk, a_map), pl.BlockSpec(b_blk, b_map)]
    operands = [a, b]
    if has_add:
        in_specs.append(pl.BlockSpec((1, tm, tn), o_map))
        operands.append(add)
    if dep is not None:
        in_specs.append(pl.BlockSpec(memory_space=pl.ANY))
        operands.append(dep)
    if o_split:
        out_shape = jax.ShapeDtypeStruct((n // tn, m, tn), out_dtype)
    else:
        out_shape = jax.ShapeDtypeStruct((1 if reduce_b else nb, m, n), out_dtype)
    return pl.pallas_call(
        body, grid=grid, in_specs=in_specs, out_specs=pl.BlockSpec((1, tm, tn), o_map),
        out_shape=out_shape, scratch_shapes=[] if nred == 1 else [pltpu.VMEM((tm, tn), F32)],
        compiler_params=_cp(*sem), name=name)(*operands)


def _mm_nt_split(a, b, dep, *, name, tm=512, tn=1024):
    m, k = a.shape
    s, n, w = b.shape
    assert k == s * w and w % LANES == 0
    tm, tn = _tile(m, tm), _tile(n, tn)
    has_dep = dep is not None

    def body(*refs):
        a_ref, b_ref, o_ref = refs[0], refs[1], refs[-1]
        r = _dot(a_ref[:, 0:w], b_ref[0], _NT)
        for q in range(1, s):
            r = r + _dot(a_ref[:, q * w:(q + 1) * w], b_ref[q], _NT)
        o_ref[...] = r

    return pl.pallas_call(
        body, grid=(n // tn, m // tm),
        in_specs=[pl.BlockSpec((tm, k), lambda j, i: (i, 0)), pl.BlockSpec((s, tn, w), lambda j, i: (0, j, 0))]
        + ([pl.BlockSpec(memory_space=pl.ANY)] if has_dep else []),
        out_specs=pl.BlockSpec((tm, tn), lambda j, i: (i, j)), out_shape=jax.ShapeDtypeStruct((m, n), F32),
        compiler_params=_cp("parallel", "parallel"), name=name)(a, b, *([dep] if has_dep else []))


def _mm2(a, b, mode, out_dtype, *, name, add=None, **kw):
    b3 = b if kw.get("b_split") else b[None]
    r = _mm(a[None], b3, mode, out_dtype, name=name, add=None if add is None else add[None], **kw)
    return r if kw.get("o_split") else r[0]


def _rms_fwd(x, g, *, name, tb=512):
    t, d = x.shape
    tb = min(tb, t)

    def body(x_ref, g_ref, o_ref):
        xv = x_ref[...]
        r = lax.rsqrt(jnp.mean(xv * xv, axis=-1, keepdims=True) + EPS)
        o_ref[...] = (xv * r * g_ref[...]).astype(o_ref.dtype)

    return pl.pallas_call(
        body, grid=(t // tb,),
        in_specs=[pl.BlockSpec((tb, d), lambda i: (i, 0)), pl.BlockSpec((1, d), lambda i: (0, 0))],
        out_specs=pl.BlockSpec((tb, d), lambda i: (i, 0)),
        out_shape=jax.ShapeDtypeStruct((t, d), MXU_DTYPE), compiler_params=_cp("parallel"), name=name)(x, g)


def _rms_bwd(x, g, dh, dres, *, name, tb=512):
    t, d = x.shape
    tb = min(tb, t)
    has_res = dres is not None

    def body(*refs):
        if has_res:
            x_ref, g_ref, dh_ref, dres_ref, dx_ref, dxl_ref, dg_ref = refs
        else:
            x_ref, g_ref, dh_ref, dx_ref, dxl_ref, dg_ref = refs
        xv = x_ref[...]
        r = lax.rsqrt(jnp.mean(xv * xv, axis=-1, keepdims=True) + EPS)
        xh = xv * r
        dhv = dh_ref[...].astype(F32)

        @pl.when(pl.program_id(0) == 0)
        def _():
            dg_ref[...] = jnp.zeros_like(dg_ref)

        dg_ref[...] += jnp.sum(dhv * xh, axis=0, keepdims=True)
        dxh = dhv * g_ref[...]
        dx = r * (dxh - xh * jnp.mean(dxh * xh, axis=-1, keepdims=True))
        if has_res:
            dx = dx + dres_ref[...]
        dx_ref[...] = dx
        dxl_ref[...] = dx.astype(dxl_ref.dtype)

    row = pl.BlockSpec((tb, d), lambda i: (i, 0))
    vec = pl.BlockSpec((1, d), lambda i: (0, 0))
    operands = [x, g, dh] + ([dres] if has_res else [])
    return pl.pallas_call(
        body, grid=(t // tb,), in_specs=[row, vec, row] + ([row] if has_res else []),
        out_specs=[row, row, vec],
        out_shape=[jax.ShapeDtypeStruct((t, d), F32), jax.ShapeDtypeStruct((t, d), MXU_DTYPE),
                   jax.ShapeDtypeStruct((1, d), F32)],
        compiler_params=_cp("arbitrary"), name=name)(*operands)


def _loss_head(x, g, target, *, name, tb=512):
    t, d = x.shape
    tb = min(tb, t)

    def body(x_ref, g_ref, t_ref, loss_ref, dx_ref, dxl_ref, dg_ref):
        xv = x_ref[...]
        r = lax.rsqrt(jnp.mean(xv * xv, axis=-1, keepdims=True) + EPS)
        xh = xv * r
        gv = g_ref[...]
        err = xh * gv - t_ref[...]

        @pl.when(pl.program_id(0) == 0)
        def _():
            dg_ref[...] = jnp.zeros_like(dg_ref)
            loss_ref[...] = jnp.zeros_like(loss_ref)

        row_loss = jnp.mean(err * err, axis=-1, keepdims=True)
        loss_ref[...] += 0.5 * jnp.sum(row_loss, axis=0, keepdims=True)
        dy = err * (1.0 / d)
        dg_ref[...] += jnp.sum(dy * xh, axis=0, keepdims=True)
        dxh = dy * gv
        dx = r * (dxh - xh * jnp.mean(dxh * xh, axis=-1, keepdims=True))
        dx_ref[...] = dx
        dxl_ref[...] = dx.astype(dxl_ref.dtype)

    row = pl.BlockSpec((tb, d), lambda i: (i, 0))
    vec = pl.BlockSpec((1, d), lambda i: (0, 0))
    return pl.pallas_call(
        body, grid=(t // tb,), in_specs=[row, vec, row],
        out_specs=[pl.BlockSpec((1, 1), lambda i: (0, 0)), row, row, vec],
        out_shape=[jax.ShapeDtypeStruct((1, 1), F32), jax.ShapeDtypeStruct((t, d), F32),
                   jax.ShapeDtypeStruct((t, d), MXU_DTYPE), jax.ShapeDtypeStruct((1, d), F32)],
        compiler_params=_cp("arbitrary"), name=name)(x, g, target)


def _ffn_up(h, wg, wu, *, name, tb=1024):
    t, d = h.shape
    s, f, _ = wg.shape
    tb = min(tb, t)

    def body(h_ref, wg_ref, wu_ref, g_ref, u_ref, a_ref):
        hv = h_ref[...]
        gv = _dot(hv, wg_ref[0], _NT)
        uv = _dot(hv, wu_ref[0], _NT)
        g_ref[0] = gv.astype(g_ref.dtype)
        u_ref[0] = uv.astype(u_ref.dtype)
        a_ref[0] = (gv * _sigmoid(gv) * uv).astype(a_ref.dtype)

    wspec = pl.BlockSpec((1, f, d), lambda j, i: (j, 0, 0))
    ospec = pl.BlockSpec((1, tb, f), lambda j, i: (j, i, 0))
    return pl.pallas_call(
        body, grid=(s, t // tb),
        in_specs=[pl.BlockSpec((tb, d), lambda j, i: (i, 0)), wspec, wspec],
        out_specs=[ospec, ospec, ospec],
        out_shape=[jax.ShapeDtypeStruct((s, t, f), MXU_DTYPE)] * 3,
        compiler_params=_cp("parallel", "parallel"), name=name)(h, wg, wu)


def _ffn_dact(dy, wd, gate, up, *, name, tb=1024):
    t, d = dy.shape
    s, f, _ = wd.shape
    tb = min(tb, t)

    def body(dy_ref, wd_ref, g_ref, u_ref, dg_ref, du_ref):
        da = _dot(dy_ref[...], wd_ref[0], _NT)
        gv = g_ref[0].astype(F32)
        sg = _sigmoid(gv)
        du_ref[0] = (da * gv * sg).astype(du_ref.dtype)
        dg_ref[0] = (da * u_ref[0].astype(F32) * (sg * (1.0 + gv * (1.0 - sg)))).astype(dg_ref.dtype)

    aspec = pl.BlockSpec((1, tb, f), lambda i, j: (j, i, 0))
    return pl.pallas_call(
        body, grid=(t // tb, s),
        in_specs=[pl.BlockSpec((tb, d), lambda i, j: (i, 0)),
                  pl.BlockSpec((1, f, d), lambda i, j: (j, 0, 0)), aspec, aspec],
        out_specs=[aspec, aspec],
        out_shape=[jax.ShapeDtypeStruct((s, t, f), MXU_DTYPE), jax.ShapeDtypeStruct((s, t, f), MXU_DTYPE)],
        compiler_params=_cp("parallel", "parallel"), name=name)(dy, wd, gate, up)


def _ffn_down(act, wd, x, *, name, tm=512, tn=1024):
    s, t, f = act.shape
    d = wd.shape[2]
    tm, tn = _tile(t, tm), _tile(d, tn)

    def body(a_ref, w_ref, x_ref, o_ref):
        r = x_ref[...]
        for j in range(s):
            r = r + _dot(a_ref[j], w_ref[j], _NN)
        o_ref[...] = r

    xspec = pl.BlockSpec((tm, tn), lambda k, i: (i, k))
    return pl.pallas_call(
        body, grid=(d // tn, t // tm),
        in_specs=[pl.BlockSpec((s, tm, f), lambda k, i: (0, i, 0)), pl.BlockSpec((s, f, tn), lambda k, i: (0, 0, k)),
                  xspec],
        out_specs=xspec, out_shape=jax.ShapeDtypeStruct((t, d), F32),
        compiler_params=_cp("parallel", "parallel"), name=name)(act, wd, x)


def _ffn_dh(dgate, dup, wg, wu, dep, *, name, tm=512, tn=1024, sg=4):
    s, t, f = dgate.shape
    d = wg.shape[2]
    tm, tn = _tile(t, tm), _tile(d, tn)
    steps = s // sg
    has_dep = dep is not None

    def body(*refs):
        dg_ref, du_ref, wg_ref, wu_ref = refs[:4]
        o_ref, acc_ref = refs[-2:]
        j = pl.program_id(2)
        part = _dot(dg_ref[0], wg_ref[0], _NN) + _dot(du_ref[0], wu_ref[0], _NN)
        for q in range(1, sg):
            part = part + _dot(dg_ref[q], wg_ref[q], _NN) + _dot(du_ref[q], wu_ref[q], _NN)

        @pl.when(j == 0)
        def _():
            acc_ref[...] = part

        @pl.when(j > 0)
        def _():
            acc_ref[...] += part

        @pl.when(j == steps - 1)
        def _():
            o_ref[...] = acc_ref[...]

    aspec = pl.BlockSpec((sg, tm, f), lambda i, k, j: (j, i, 0))
    wspec = pl.BlockSpec((sg, f, tn), lambda i, k, j: (j, 0, k))
    return pl.pallas_call(
        body, grid=(t // tm, d // tn, steps),
        in_specs=[aspec, aspec, wspec, wspec] + ([pl.BlockSpec(memory_space=pl.ANY)] if has_dep else []),
        out_specs=pl.BlockSpec((tm, tn), lambda i, k, j: (i, k)),
        out_shape=jax.ShapeDtypeStruct((t, d), F32), scratch_shapes=[pltpu.VMEM((tm, tn), F32)],
        compiler_params=_cp("parallel", "parallel", "arbitrary"),
        name=name)(dgate, dup, wg, wu, *([dep] if has_dep else []))


def _xattn_fwd(q, kv, *, name, tb=512):
    t, d = q.shape
    m = kv.shape[0]
    hd = d // XA_HEADS
    tb = min(tb, t)
    scale = hd ** -0.5

    def body(q_ref, kv_ref, o_ref):
        for hh in range(XA_HEADS):
            cs = slice(hh * hd, (hh + 1) * hd)
            s = _dot(q_ref[:, cs], kv_ref[:, cs], _NT) * scale
            s = s - jnp.max(s, axis=-1, keepdims=True)
            e = jnp.exp(s)
            p = e / jnp.sum(e, axis=-1, keepdims=True)
            o_ref[:, cs] = _dot(p, kv_ref[:, d + hh * hd:d + (hh + 1) * hd], _NN).astype(o_ref.dtype)

    return pl.pallas_call(
        body, grid=(t // tb,),
        in_specs=[pl.BlockSpec((tb, d), lambda i: (i, 0)), pl.BlockSpec((m, 2 * d), lambda i: (0, 0))],
        out_specs=pl.BlockSpec((tb, d), lambda i: (i, 0)),
        out_shape=jax.ShapeDtypeStruct((t, d), MXU_DTYPE), compiler_params=_cp("parallel"), name=name)(q, kv)


def _xattn_bwd(q, kv, do, *, name, tb=512):
    t, d = q.shape
    m = kv.shape[0]
    hd = d // XA_HEADS
    tb = min(tb, t)
    scale = hd ** -0.5

    def body(q_ref, kv_ref, do_ref, dq_ref, dkv_ref):
        @pl.when(pl.program_id(0) == 0)
        def _():
            dkv_ref[...] = jnp.zeros_like(dkv_ref)

        for hh in range(XA_HEADS):
            cs = slice(hh * hd, (hh + 1) * hd)
            vs = slice(d + hh * hd, d + (hh + 1) * hd)
            qv, kk, vv, dov = q_ref[:, cs], kv_ref[:, cs], kv_ref[:, vs], do_ref[:, cs]
            s = _dot(qv, kk, _NT) * scale
            s = s - jnp.max(s, axis=-1, keepdims=True)
            e = jnp.exp(s)
            p = e / jnp.sum(e, axis=-1, keepdims=True)
            dkv_ref[:, vs] += _dot(p, dov, _TN)
            dp = _dot(dov, vv, _NT)
            ds = p * (dp - jnp.sum(p * dp, axis=-1, keepdims=True)) * scale
            dq_ref[:, cs] = _dot(ds, kk, _NN).astype(dq_ref.dtype)
            dkv_ref[:, cs] += _dot(ds, qv, _TN)

    row = pl.BlockSpec((tb, d), lambda i: (i, 0))
    full = pl.BlockSpec((m, 2 * d), lambda i: (0, 0))
    return pl.pallas_call(
        body, grid=(t // tb,), in_specs=[row, full, row], out_specs=[row, full],
        out_shape=[jax.ShapeDtypeStruct((t, d), MXU_DTYPE), jax.ShapeDtypeStruct((m, 2 * d), F32)],
        compiler_params=_cp("arbitrary"), name=name)(q, kv, do)


def _pool_window_stats(u, gi, reverse):
    t = u.shape[0]
    row = lax.broadcasted_iota(jnp.int32, u.shape, 0)
    s = u
    for j in range(POOL_GROUPS):
        sh = 1 << j
        if reverse:
            rolled = jnp.where(row < t - sh, pltpu.roll(s, t - sh, axis=0), 0.0)
        else:
            rolled = jnp.where(row >= sh, pltpu.roll(s, sh, axis=0), 0.0)
        s = jnp.where(j <= gi, s + rolled, s)
    return s, row


def _pool_fwd(z, w_pool, scale, *, name):
    t = z.shape[0]
    g_cnt, c, _ = w_pool.shape

    def body(z_ref, w_ref, s_ref, o_ref):
        gi = pl.program_id(0)
        u = z_ref[...]
        win, row = _pool_window_stats(u, gi, False)
        cnt = jnp.minimum(row + 1, lax.shift_left(jnp.int32(2), gi)).astype(F32)
        p = win / cnt - u
        o_ref[...] = (_dot(p, w_ref[0], _NN) * s_ref[...]).astype(o_ref.dtype)

    return pl.pallas_call(
        body, grid=(g_cnt,),
        in_specs=[pl.BlockSpec((t, c), lambda g: (0, g)), pl.BlockSpec((1, c, c), lambda g: (g, 0, 0)),
                  pl.BlockSpec((1, c), lambda g: (0, g))],
        out_specs=pl.BlockSpec((t, c), lambda g: (0, g)),
        out_shape=jax.ShapeDtypeStruct((t, 2 * g_cnt * c), MXU_DTYPE),
        compiler_params=_cp("parallel"), name=name)(z, w_pool, scale)


def _pool_bwd(z, w_pool, scale, dycat, *, name):
    t = z.shape[0]
    g_cnt, c, _ = w_pool.shape

    def body(z_ref, w_ref, s_ref, dy_ref, du_ref, dw_ref, ds_ref):
        gi = pl.program_id(0)
        u = z_ref[...]
        win, row = _pool_window_stats(u, gi, False)
        cnt = jnp.minimum(row + 1, lax.shift_left(jnp.int32(2), gi)).astype(F32)
        p = win / cnt - u
        y = _dot(p, w_ref[0], _NN)
        dya = dy_ref[...].astype(F32)
        ds_ref[...] = jnp.sum(dya * y, axis=0, keepdims=True)
        dy = dya * s_ref[...]
        dw_ref[0] = _dot(p, dy, _TN)
        dp = _dot(dy, w_ref[0], _NT)
        back, _ = _pool_window_stats(dp / cnt, gi, True)
        du_ref[...] = (back - dp).astype(du_ref.dtype)

    col = pl.BlockSpec((t, c), lambda g: (0, g))
    return pl.pallas_call(
        body, grid=(g_cnt,),
        in_specs=[col, pl.BlockSpec((1, c, c), lambda g: (g, 0, 0)), pl.BlockSpec((1, c), lambda g: (0, g)), col],
        out_specs=[col, pl.BlockSpec((1, c, c), lambda g: (g, 0, 0)), pl.BlockSpec((1, c), lambda g: (0, g))],
        out_shape=[jax.ShapeDtypeStruct((t, g_cnt * c), MXU_DTYPE), jax.ShapeDtypeStruct((g_cnt, c, c), F32),
                   jax.ShapeDtypeStruct((1, g_cnt * c), F32)],
        compiler_params=_cp("parallel"), name=name)(z, w_pool, scale, dycat)


def _chunk_tri(lower):
    r = lax.broadcasted_iota(jnp.int32, (LANES, LANES), 0)
    c = lax.broadcasted_iota(jnp.int32, (LANES, LANES), 1)
    same = (r // HG_CHUNK) == (c // HG_CHUNK)
    return jnp.where(same & ((c <= r) if lower else (c >= r)), 1.0, 0.0).astype(F32)


def _hgrn_prepare(q_ref, f_ref, lb_ref, qh_s, k_s, b_s, qt_s, kt_s, gl_s):
    tb = q_ref.shape[0]
    lb = lb_ref[...]
    sg = _sigmoid(f_ref[...])
    f = lb + (1.0 - lb) * sg
    logf = jnp.log(f)
    qv = q_ref[...]
    qh = qv * _sigmoid(qv) * (HG_HEAD ** -0.5)
    tri = _chunk_tri(True)
    for r in range(tb // LANES):
        rows = slice(r * LANES, (r + 1) * LANES)
        b_s[rows, :] = _dot_f32(tri, logf[rows, :], _NN)
    b = b_s[...]
    b3 = b.reshape(tb // HG_CHUNK, HG_CHUNK, HG_HEAD)
    bl = b3[:, HG_CHUNK - 1:HG_CHUNK, :]
    k = 1.0 - f
    qh_s[...] = qh
    k_s[...] = k
    qt_s[...] = qh * jnp.exp(b)
    kt_s[...] = k * jnp.exp(bl - b3).reshape(tb, HG_HEAD)
    gl_s[...] = jnp.exp(jnp.broadcast_to(bl, b3.shape)).reshape(tb, HG_HEAD)
    return sg, f


SUBLANES = 8
HG_GROUPS = HG_CHUNK // SUBLANES


def _hgrn_intra(qh, kk, bq, rows_a, rows_b):
    ones = jnp.ones((HG_HEAD, HG_HEAD), MXU_DTYPE)
    es, stack_a, stack_b, starts, at = [], [], [], [], 0
    for s in range(HG_CHUNK):
        lo = (s // SUBLANES) * SUBLANES
        e = jnp.exp(jnp.minimum(bq[lo:, :] - bq[s:s + 1, :], 0.0))
        es.append(e)
        stack_a.append(qh[lo:, :] * e * kk[s:s + 1, :])
        if rows_a is not None:
            stack_b.append(rows_a[lo:, :] * rows_b[s:s + 1, :])
        starts.append(at)
        at += HG_CHUNK - lo
    a_rep = _dot(jnp.concatenate(stack_a, axis=0), ones, _NN)
    d_rep = _dot(jnp.concatenate(stack_b, axis=0), ones, _NN) if rows_a is not None else None
    return es, a_rep, d_rep, starts


def _groups(v):
    return [v[g * SUBLANES:(g + 1) * SUBLANES, :] for g in range(HG_GROUPS)]


def _hgrn_fwd(z, lb, hg_norm, ycat, *, name, tb=512):
    t = z.shape[0]
    mix_b = lb.shape[1]
    heads = mix_b // HG_HEAD
    off = (z.shape[1] - 4 * mix_b) // HG_HEAD
    tb = min(tb, t)
    ncb = tb // HG_CHUNK

    def body(q_ref, f_ref, i_ref, g_ref, lb_ref, hn_ref, ycat_in, y_ref, o_ref, st_ref,
             state, qh_s, k_s, b_s, qt_s, kt_s, gl_s, o_s):
        del ycat_in

        @pl.when(pl.program_id(1) == 0)
        def _():
            state[...] = jnp.zeros_like(state)

        _hgrn_prepare(q_ref, f_ref, lb_ref, qh_s, k_s, b_s, qt_s, kt_s, gl_s)
        row = lax.broadcasted_iota(jnp.int32, (SUBLANES, HG_HEAD), 0)

        def chunk(c, carry):
            rows = pl.ds(pl.multiple_of(c * HG_CHUNK, HG_CHUNK), HG_CHUNK)
            st = state[...]
            st_ref[0, c] = st
            vv = i_ref[rows, :]
            o = _groups(_dot(qt_s[rows, :], st, _NT))
            _, a_rep, _, starts = _hgrn_intra(qh_s[rows, :], k_s[rows, :], b_s[rows, :], None, None)
            for s in range(HG_CHUNK):
                g0 = s // SUBLANES
                for g in range(g0, HG_GROUPS):
                    at = starts[s] + (g - g0) * SUBLANES
                    piece = a_rep[at:at + SUBLANES, :] * vv[s:s + 1, :]
                    o[g] = o[g] + (jnp.where(row >= s - g0 * SUBLANES, piece, 0.0) if g == g0 else piece)
            o_s[rows, :] = jnp.concatenate(o, axis=0)
            state[...] = st * gl_s[rows, :][0:1, :] + _dot(vv, kt_s[rows, :], _TN)
            return carry

        lax.fori_loop(0, ncb, chunk, 0, unroll=2)
        o = o_s[...]
        o_ref[...] = o
        r = lax.rsqrt(jnp.mean(o * o, axis=-1, keepdims=True) + EPS)
        gv = g_ref[...]
        y_ref[...] = (o * r * hn_ref[...] * (gv * _sigmoid(gv))).astype(y_ref.dtype)

    def zcol(kind):
        return pl.BlockSpec((tb, HG_HEAD), lambda h, i: (i, off + kind * heads + h))

    scratch = [pltpu.VMEM((HG_HEAD, HG_HEAD), F32)] + [pltpu.VMEM((tb, HG_HEAD), F32)] * 7
    return pl.pallas_call(
        body, grid=(heads, t // tb),
        in_specs=[zcol(0), zcol(1), zcol(2), zcol(3), pl.BlockSpec((1, HG_HEAD), lambda h, i: (0, h)),
                  pl.BlockSpec((1, HG_HEAD), lambda h, i: (0, 0)), pl.BlockSpec(memory_space=pl.ANY)],
        out_specs=[pl.BlockSpec((tb, HG_HEAD), lambda h, i: (i, heads + h)),
                   pl.BlockSpec((tb, HG_HEAD), lambda h, i: (i, h)),
                   pl.BlockSpec((1, ncb, HG_HEAD, HG_HEAD), lambda h, i: (h, i, 0, 0))],
        out_shape=[jax.ShapeDtypeStruct(ycat.shape, ycat.dtype), jax.ShapeDtypeStruct((t, mix_b), F32),
                   jax.ShapeDtypeStruct((heads, t // HG_CHUNK, HG_HEAD, HG_HEAD), F32)],
        scratch_shapes=scratch, input_output_aliases={6: 0},
        compiler_params=_cp("parallel", "arbitrary"), name=name)(z, z, z, z, lb, hg_norm, ycat)


def _hgrn_bwd(z, lb, hg_norm, o_raw, states, dycat, *, name, tb=512):
    t = z.shape[0]
    mix_b = lb.shape[1]
    heads = mix_b // HG_HEAD
    off = (z.shape[1] - 4 * mix_b) // HG_HEAD
    tb = min(tb, t)
    ncb = tb // HG_CHUNK
    nt = t // tb

    def body(q_ref, f_ref, i_ref, g_ref, lb_ref, hn_ref, o_ref, st_ref, dy_ref,
             dq_ref, dfl_ref, di_ref, dg_ref, dlb_ref, dhn_ref,
             dstate, qh_s, k_s, b_s, qt_s, kt_s, gl_s, do_s, dqh_s, dk_s, db_s):
        first = pl.program_id(1) == 0

        @pl.when(first)
        def _():
            dstate[...] = jnp.zeros_like(dstate)
            dlb_ref[...] = jnp.zeros_like(dlb_ref)

        @pl.when(first & (pl.program_id(0) == 0))
        def _():
            dhn_ref[...] = jnp.zeros_like(dhn_ref)

        sg, f = _hgrn_prepare(q_ref, f_ref, lb_ref, qh_s, k_s, b_s, qt_s, kt_s, gl_s)
        o = o_ref[...]
        r = lax.rsqrt(jnp.mean(o * o, axis=-1, keepdims=True) + EPS)
        oh = o * r
        gv = g_ref[...]
        sgg = _sigmoid(gv)
        dy = dy_ref[...].astype(F32)
        hn = hn_ref[...]
        dg_ref[...] = (dy * oh * hn * (sgg * (1.0 + gv * (1.0 - sgg)))).astype(dg_ref.dtype)
        don = dy * (gv * sgg)
        dhn_ref[...] += jnp.sum(don * oh, axis=0, keepdims=True)
        doh = don * hn
        do_s[...] = r * (doh - oh * jnp.mean(doh * oh, axis=-1, keepdims=True))
        row = lax.broadcasted_iota(jnp.int32, (SUBLANES, HG_HEAD), 0)

        def chunk(ci, carry):
            c = ncb - 1 - ci
            rows = pl.ds(pl.multiple_of(c * HG_CHUNK, HG_CHUNK), HG_CHUNK)
            st_prev = st_ref[0, c]
            dst = dstate[...]
            qh, kk, bq, vv = qh_s[rows, :], k_s[rows, :], b_s[rows, :], i_ref[rows, :]
            qt, kt, doo = qt_s[rows, :], kt_s[rows, :], do_s[rows, :]
            gl = gl_s[rows, :][0:1, :]
            es, a_rep, d_rep, starts = _hgrn_intra(qh, kk, bq, doo, vv)
            dqh = _groups(jnp.exp(bq) * _dot(doo, st_prev, _NN))
            dk = _groups(jnp.exp(bq[HG_CHUNK - 1:HG_CHUNK, :] - bq) * _dot(vv, dst, _NN))
            dv = _groups(_dot(kt, dst, _NT))
            qh_g, do_g = _groups(qh), _groups(doo)
            for s in range(HG_CHUNK):
                g0 = s // SUBLANES
                local = s - g0 * SUBLANES
                dk_acc = dv_acc = None
                for g in range(g0, HG_GROUPS):
                    at = (g - g0) * SUBLANES
                    wgt = d_rep[starts[s] + at:starts[s] + at + SUBLANES, :] * es[s][at:at + SUBLANES, :]
                    avo = a_rep[starts[s] + at:starts[s] + at + SUBLANES, :] * do_g[g]
                    if g == g0:
                        wgt = jnp.where(row >= local, wgt, 0.0)
                        avo = jnp.where(row >= local, avo, 0.0)
                    dqh[g] = dqh[g] + wgt * kk[s:s + 1, :]
                    dk_acc = wgt * qh_g[g] if dk_acc is None else dk_acc + wgt * qh_g[g]
                    dv_acc = avo if dv_acc is None else dv_acc + avo
                dk[g0] = dk[g0] + jnp.where(row == local, jnp.sum(dk_acc, axis=0, keepdims=True), 0.0)
                dv[g0] = dv[g0] + jnp.where(row == local, jnp.sum(dv_acc, axis=0, keepdims=True), 0.0)
            dqh, dk, dv = [jnp.concatenate(p, axis=0) for p in (dqh, dk, dv)]
            row_c = lax.broadcasted_iota(jnp.int32, (HG_CHUNK, HG_HEAD), 0)
            st_next = st_prev * gl + _dot(vv, kt, _TN)
            db = qh * dqh - kk * dk
            db = db + jnp.where(row_c == HG_CHUNK - 1, jnp.sum(st_next * dst, axis=0, keepdims=True), 0.0)
            dstate[...] = dst * gl + _dot(doo, qt, _TN)
            dqh_s[rows, :] = dqh
            dk_s[rows, :] = dk
            db_s[rows, :] = db
            di_ref[rows, :] = dv.astype(di_ref.dtype)
            return carry

        lax.fori_loop(0, ncb, chunk, 0, unroll=2)
        tri = _chunk_tri(False)
        lb_v = lb_ref[...]
        qv = q_ref[...]
        sgq = _sigmoid(qv)
        dq_ref[...] = (dqh_s[...] * (HG_HEAD ** -0.5) * (sgq * (1.0 + qv * (1.0 - sgq)))).astype(dq_ref.dtype)
        dlb = jnp.zeros((1, HG_HEAD), F32)
        for rr in range(tb // LANES):
            rws = slice(rr * LANES, (rr + 1) * LANES)
            dlogf = _dot_f32(tri, db_s[rws, :], _NN)
            df = dlogf / f[rws, :] - dk_s[rws, :]
            sgr = sg[rws, :]
            dfl_ref[rws, :] = (df * (1.0 - lb_v) * sgr * (1.0 - sgr)).astype(dfl_ref.dtype)
            dlb = dlb + jnp.sum(df * (1.0 - sgr), axis=0, keepdims=True)
        dlb_ref[...] += dlb

    def zcol(kind):
        return pl.BlockSpec((tb, HG_HEAD), lambda h, i: (nt - 1 - i, off + kind * heads + h))

    hcol = pl.BlockSpec((tb, HG_HEAD), lambda h, i: (nt - 1 - i, h))
    scratch = [pltpu.VMEM((HG_HEAD, HG_HEAD), F32)] + [pltpu.VMEM((tb, HG_HEAD), F32)] * 10
    out = jax.ShapeDtypeStruct((t, mix_b), MXU_DTYPE)
    return pl.pallas_call(
        body, grid=(heads, nt),
        in_specs=[zcol(0), zcol(1), zcol(2), zcol(3), pl.BlockSpec((1, HG_HEAD), lambda h, i: (0, h)),
                  pl.BlockSpec((1, HG_HEAD), lambda h, i: (0, 0)), hcol,
                  pl.BlockSpec((1, ncb, HG_HEAD, HG_HEAD), lambda h, i: (h, nt - 1 - i, 0, 0)),
                  pl.BlockSpec((tb, HG_HEAD), lambda h, i: (nt - 1 - i, heads + h))],
        out_specs=[hcol, hcol, hcol, hcol, pl.BlockSpec((1, HG_HEAD), lambda h, i: (0, h)),
                   pl.BlockSpec((1, HG_HEAD), lambda h, i: (0, 0))],
        out_shape=[out, out, out, out, jax.ShapeDtypeStruct((1, mix_b), F32),
                   jax.ShapeDtypeStruct((1, HG_HEAD), F32)],
        scratch_shapes=scratch, compiler_params=_cp("arbitrary", "arbitrary"),
        name=name)(z, z, z, z, lb, hg_norm, o_raw, states, dycat)


def _lb_fwd(lb_table, layer, *, name):
    rows, width = lb_table.shape

    def body(t_ref, o_ref):
        tv = t_ref[...]
        e = jnp.exp(tv - jnp.max(tv, axis=0, keepdims=True))
        sm = e / jnp.sum(e, axis=0, keepdims=True)
        o_ref[...] = jnp.sum(sm[1:layer + 2, :], axis=0, keepdims=True)

    return pl.pallas_call(body, out_shape=jax.ShapeDtypeStruct((1, width), F32), name=name)(lb_table)


def _lb_bwd(lb_table, dlb, layer, *, name):
    rows, width = lb_table.shape

    def body(t_ref, d_ref, o_ref):
        tv = t_ref[...]
        e = jnp.exp(tv - jnp.max(tv, axis=0, keepdims=True))
        sm = e / jnp.sum(e, axis=0, keepdims=True)
        ridx = lax.broadcasted_iota(jnp.int32, sm.shape, 0)
        dsm = jnp.where((ridx >= 1) & (ridx <= layer + 1), d_ref[...], 0.0)
        o_ref[...] = sm * (dsm - jnp.sum(sm * dsm, axis=0, keepdims=True))

    return pl.pallas_call(body, out_shape=jax.ShapeDtypeStruct((rows, width), F32), name=name)(lb_table, dlb)


FOX_BLOCK = 1024
FOX_DIAGONAL_STRIPS = 4


def _fox_prep(zf, b_f, *, name, blk=256):
    t = zf.shape[0]

    def body(z_ref, b_ref, fc_ref):
        r = lax.broadcasted_iota(jnp.int32, (blk, blk), 0)
        c = lax.broadcasted_iota(jnp.int32, (blk, blk), 1)
        tri = jnp.where(c <= r, 1.0, 0.0).astype(F32)
        carry = jnp.zeros((1, LANES), F32)
        for j in range(t // blk):
            rows = slice(j * blk, (j + 1) * blk)
            ls = jax.nn.log_sigmoid(z_ref[rows, :] + b_ref[...])
            fb = _dot_f32(tri, ls, _NN) + carry
            carry = fb[blk - 1:blk, :]
            fc_ref[rows, :] = fb

    return pl.pallas_call(
        body, out_shape=jax.ShapeDtypeStruct((t, LANES), F32),
        compiler_params=pltpu.CompilerParams(vmem_limit_bytes=VMEM_LIMIT_BYTES), name=name)(zf, b_f)


def _fox_head_column(fc_ref, fk_s, head):
    lane = lax.broadcasted_iota(jnp.int32, fc_ref.shape, 1)
    fk_s[...] = jnp.sum(jnp.where(lane == head, fc_ref[...], 0.0), axis=1, keepdims=True)


def _fox_scores(k_blk, q_blk, fk_blk, diagonal):
    s = _dot(k_blk, q_blk, _NT) * (FOX_HEAD ** -0.5) - fk_blk
    if diagonal:
        key = lax.broadcasted_iota(jnp.int32, s.shape, 0)
        qry = lax.broadcasted_iota(jnp.int32, s.shape, 1)
        s = jnp.where(key <= qry, s, -jnp.inf)
    return s


def _fox_fwd(zqkv, fcol, *, name):
    t = zqkv.shape[0]
    d = zqkv.shape[1] // 3
    heads = d // FOX_HEAD
    blk = min(FOX_BLOCK, t)
    nq = t // blk

    def body(q_ref, k_ref, v_ref, fc_ref, o_ref, lse_ref, fk_s):
        _fox_head_column(fc_ref, fk_s, pl.program_id(0))

        def q_block(qi, carry):
            qrows = pl.ds(pl.multiple_of(qi * blk, blk), blk)
            q_blk = q_ref[qrows, :]

            def update(st, krows, diagonal):
                m, l, acc = st
                s = _fox_scores(k_ref[krows, :], q_blk, fk_s[krows, :], diagonal)
                m_new = jnp.maximum(m, jnp.max(s, axis=0, keepdims=True))
                alpha = jnp.exp(m - m_new)
                p = jnp.exp(s - m_new)
                l = alpha * l + jnp.sum(p, axis=0, keepdims=True)
                acc = acc * alpha + _dot(v_ref[krows, :], p, _TN)
                return m_new, l, acc

            def k_block(kj, st):
                return update(st, pl.ds(pl.multiple_of(kj * blk, blk), blk), False)

            init = (jnp.full((1, blk), -jnp.inf, F32), jnp.zeros((1, blk), F32),
                    jnp.zeros((FOX_HEAD, blk), F32))
            m, l, acc = update(lax.fori_loop(0, qi, k_block, init), qrows, True)
            o_ref[qrows, :] = (acc / l).T.astype(o_ref.dtype)
            lse_ref[0, :, qrows] = m + jnp.log(l)
            return carry

        lax.fori_loop(0, nq, q_block, 0)

    def col(kind):
        return pl.BlockSpec((t, FOX_HEAD), lambda h: (0, kind * heads + h))

    rowvec = pl.BlockSpec((1, 1, t), lambda h: (h, 0, 0))
    return pl.pallas_call(
        body, grid=(heads,),
        in_specs=[col(0), col(1), col(2), pl.BlockSpec((t, LANES), lambda h: (0, 0))],
        out_specs=[pl.BlockSpec((t, FOX_HEAD), lambda h: (0, h)), rowvec],
        out_shape=[jax.ShapeDtypeStruct((t, d), MXU_DTYPE), jax.ShapeDtypeStruct((heads, 1, t), F32)],
        scratch_shapes=[pltpu.VMEM((t, 1), F32)],
        compiler_params=_cp("parallel"), name=name)(zqkv, zqkv, zqkv, fcol)


def _fox_bwd(zqkv, fcol, lse, o, do, *, name):
    t = zqkv.shape[0]
    d = zqkv.shape[1] // 3
    heads = d // FOX_HEAD
    blk = min(FOX_BLOCK, t)
    nq = t // blk
    strip = blk // FOX_DIAGONAL_STRIPS
    scale = FOX_HEAD ** -0.5

    def body(q_ref, k_ref, v_ref, fc_ref, lse_ref, o_ref, do_ref,
             dq_ref, dk_ref, dv_ref, rq_ref, rk_ref, dq_s, drow_s, fk_s, acc_s, rk_s):
        _fox_head_column(fc_ref, fk_s, pl.program_id(0))
        dq_s[...] = jnp.zeros_like(dq_s)
        rq_ref[...] = jnp.zeros_like(rq_ref)
        ones_f = jnp.ones((8, FOX_HEAD), F32)
        for j in range(nq):
            rows = slice(j * blk, (j + 1) * blk)
            prod = do_ref[rows, :].astype(F32) * o_ref[rows, :].astype(F32)
            drow_s[:, rows] = _dot_f32(ones_f, prod, _NT)

        def k_block(kj, carry):
            krows = pl.ds(pl.multiple_of(kj * blk, blk), blk)
            k_blk, v_blk, fk_blk = k_ref[krows, :], v_ref[krows, :], fk_s[krows, :]

            def pair(qrows, diagonal, keys=slice(0, blk)):
                q_blk, do_blk = q_ref[qrows, :], do_ref[qrows, :]
                s = _fox_scores(k_blk[keys, :], q_blk, fk_blk[keys, :], diagonal)
                p = jnp.exp(s - lse_ref[0, :, qrows])
                acc_s[1, keys, :] += _dot(p, do_blk, _NN)
                dp = _dot(v_blk[keys, :], do_blk, _NT)
                ds = (p * (dp - drow_s[0:1, qrows])).astype(MXU_DTYPE)
                acc_s[0, keys, :] += _dot(ds, q_blk, _NN)
                dq_s[qrows, :] += _dot(ds, k_blk[keys, :], _TN)
                ds_f = ds.astype(F32)
                rq_ref[0, :, qrows] += jnp.sum(ds_f, axis=0, keepdims=True)
                rk_s[keys, :] += jnp.sum(ds_f, axis=1, keepdims=True)

            def q_block(qi, carry2):
                pair(pl.ds(pl.multiple_of(qi * blk, blk), blk), False)
                return carry2

            acc_s[...] = jnp.zeros_like(acc_s)
            rk_s[...] = jnp.zeros_like(rk_s)
            for j in range(FOX_DIAGONAL_STRIPS):
                pair(pl.ds(pl.multiple_of(kj * blk + j * strip, strip), blk - j * strip), True,
                     slice(j * strip, (j + 1) * strip))
            lax.fori_loop(kj + 1, nq, q_block, 0)
            dk_ref[krows, :] = (acc_s[0] * scale).astype(dk_ref.dtype)
            dv_ref[krows, :] = acc_s[1].astype(dv_ref.dtype)
            rk_ref[0, :, krows] = jnp.broadcast_to(rk_s[...], (blk, FOX_HEAD)).T[0:1, :]
            return carry

        lax.fori_loop(0, nq, k_block, 0)
        dq_ref[...] = (dq_s[...] * scale).astype(dq_ref.dtype)

    def col(kind):
        return pl.BlockSpec((t, FOX_HEAD), lambda h: (0, kind * heads + h))

    hcol = pl.BlockSpec((t, FOX_HEAD), lambda h: (0, h))
    rowvec = pl.BlockSpec((1, 1, t), lambda h: (h, 0, 0))
    out = jax.ShapeDtypeStruct((t, d), MXU_DTYPE)
    vec = jax.ShapeDtypeStruct((heads, 1, t), F32)
    return pl.pallas_call(
        body, grid=(heads,),
        in_specs=[col(0), col(1), col(2), pl.BlockSpec((t, LANES), lambda h: (0, 0)), rowvec, hcol, hcol],
        out_specs=[hcol, hcol, hcol, rowvec, rowvec],
        out_shape=[out, out, out, vec, vec],
        scratch_shapes=[pltpu.VMEM((t, FOX_HEAD), F32), pltpu.VMEM((8, t), F32), pltpu.VMEM((t, 1), F32),
                        pltpu.VMEM((2, blk, FOX_HEAD), F32), pltpu.VMEM((blk, 1), F32)],
        compiler_params=_cp("parallel"), name=name)(zqkv, zqkv, zqkv, fcol, lse, o, do)


def _fox_gate_bwd(rq, rk, zf, b_f, *, name, blk=256):
    heads, t = rq.shape

    def body(rq_ref, rk_ref, z_ref, b_ref, dfl_ref, db_ref):
        r = lax.broadcasted_iota(jnp.int32, (blk, blk), 0)
        c = lax.broadcasted_iota(jnp.int32, (blk, blk), 1)
        tri = jnp.where(r >= c, 1.0, 0.0).astype(F32)
        carry = jnp.zeros((heads, 1), F32)
        db = jnp.zeros((1, LANES), F32)
        pad = jnp.zeros((LANES - heads, blk), F32)
        for j in reversed(range(t // blk)):
            cols = slice(j * blk, (j + 1) * blk)
            df = rq_ref[:, cols] - rk_ref[:, cols]
            dls = _dot_f32(df, tri, _NN) + carry
            carry = dls[:, 0:1]
            dls_t = jnp.concatenate([dls, pad], axis=0).T
            dfl = dls_t * _sigmoid(-(z_ref[cols, :] + b_ref[...]))
            dfl_ref[cols, :] = dfl.astype(dfl_ref.dtype)
            db = db + jnp.sum(dfl, axis=0, keepdims=True)
        db_ref[...] = db

    return pl.pallas_call(
        body, out_shape=[jax.ShapeDtypeStruct((t, LANES), MXU_DTYPE), jax.ShapeDtypeStruct((1, LANES), F32)],
        compiler_params=pltpu.CompilerParams(vmem_limit_bytes=VMEM_LIMIT_BYTES), name=name)(rq, rk, zf, b_f)


def _adamw(w, m, v, parts, *, name, layer=None, prev=None, tr=256):
    lcnt, r, c = w.shape
    p = parts.shape[0]
    li = 0 if layer is None else layer
    tr = _tile(r, tr, 16)
    tc = c if tr * c <= 256 * 2048 else _tile(c, 256)
    has_prev = prev is not None

    def body(*refs):
        w_ref, m_ref, v_ref, p_ref = refs[:4]
        g_ref, d_ref, nm_ref, nv_ref = refs[-4:]
        g = p_ref[0].astype(F32)
        for j in range(1, p):
            g = g + p_ref[j].astype(F32)
        wv = w_ref[0]
        mn = ADAM_B1 * m_ref[0] + (1.0 - ADAM_B1) * g
        vn = ADAM_B2 * v_ref[0] + (1.0 - ADAM_B2) * (g * g)
        m_hat = mn / (1.0 - ADAM_B1 ** ADAM_STEP)
        v_hat = vn / (1.0 - ADAM_B2 ** ADAM_STEP)
        g_ref[0] = g
        d_ref[0] = -ADAM_LR * (m_hat / (jnp.sqrt(v_hat) + ADAM_EPS) + ADAM_WD * wv)
        nm_ref[0] = mn
        nv_ref[0] = vn

    slab = pl.BlockSpec((1, tr, tc), lambda i, j: (li, i, j))
    in_specs = [slab, slab, slab, pl.BlockSpec((p, tr, tc), lambda i, j: (0, i, j))]
    operands = [w, m, v, parts]
    aliases = {}
    if has_prev:
        in_specs += [pl.BlockSpec(memory_space=pl.ANY)] * 4
        operands += list(prev)
        aliases = {4: 0, 5: 1, 6: 2, 7: 3}
    shp = jax.ShapeDtypeStruct((lcnt, r, c), F32)
    return pl.pallas_call(
        body, grid=(r // tr, c // tc), in_specs=in_specs, out_specs=[slab] * 4, out_shape=[shp] * 4,
        input_output_aliases=aliases, compiler_params=_cp("parallel", "parallel"), name=name)(*operands)


def _my_place():
    return lax.axis_index("x"), lax.axis_index("y"), lax.axis_index("c")


def _slot(p):
    return 4 * p[0] + 2 * p[1] + p[2]


def _peer(me, mask):
    x, y, c = me
    return (1 - x if mask & 4 else x, 1 - y if mask & 2 else y, 1 - c if mask & 1 else c)


_HBM = pl.BlockSpec(memory_space=pltpu.HBM)
_SEM = pl.BlockSpec(memory_space=pltpu.SEMAPHORE)
_ANY = pl.BlockSpec(memory_space=pl.ANY)
_EFFECT = pltpu.SideEffectType.DATAFLOW_SIDE_EFFECTING


def _push_copy(src_refs, land_refs, send_sems, recv_sems, a, mask, me, per_peer, outgoing):
    peer = _peer(me, mask)
    src = src_refs[a].at[_slot(peer)] if per_peer else src_refs[a]
    dst = land_refs[a].at[_slot(me) if outgoing else _slot(peer)]
    k = a * (N_DEV - 1) + mask - 1
    return pltpu.make_async_remote_copy(
        src_ref=src, dst_ref=dst, send_sem=send_sems.at[k], recv_sem=recv_sems.at[k],
        device_id=peer, device_id_type=MESH)


ALL_PEERS = tuple(range(1, N_DEV))
CHIP_PEERS = (2, 4, 6)
FIRST_HOP = (1,) + CHIP_PEERS


def _push_start(srcs, dep, *, per_peer, name, masks=ALL_PEERS):
    n = len(srcs)
    mine = _slot(_my_place())
    lands = []
    for s in srcs:
        own = lax.dynamic_index_in_dim(s, mine, 0, keepdims=True) if per_peer else s[None]
        shape = s.shape if per_peer else (N_DEV,) + s.shape
        lands.append(lax.dynamic_update_slice_in_dim(lax.empty(shape, s.dtype), own, mine, 0))
    has_dep = dep is not None

    def body(*refs):
        src_refs, land_refs = refs[:n], refs[n:2 * n]
        send_sems, recv_sems = refs[2 * n + has_dep], refs[2 * n + has_dep + 1]
        token = refs[-1]
        me = _my_place()
        for a in range(n):
            for mask in masks:
                _push_copy(src_refs, land_refs, send_sems, recv_sems, a, mask, me, per_peer, True).start()
        token[...] = jnp.zeros_like(token)

    hbm_in = [pltpu.with_memory_space_constraint(v, pltpu.HBM) for v in list(srcs) + lands]
    out = pl.pallas_call(
        body, name=name,
        out_shape=(pltpu.SemaphoreType.DMA((n * (N_DEV - 1),)), pltpu.SemaphoreType.DMA((n * (N_DEV - 1),)),
                   *[pltpu.HBM(v.shape, v.dtype) for v in hbm_in], jax.ShapeDtypeStruct((8, LANES), F32)),
        in_specs=[_HBM] * (2 * n) + ([_ANY] if has_dep else []),
        out_specs=(_SEM, _SEM, *[_HBM] * (2 * n), pl.BlockSpec(memory_space=pltpu.VMEM)),
        input_output_aliases={i: 2 + i for i in range(2 * n)},
        compiler_params=pltpu.CompilerParams(has_side_effects=_EFFECT),
    )(*hbm_in, *([dep] if has_dep else []))
    return (n, per_peer, masks, out[:-1]), out[-1]


def _push_wait(handle, after, *, name):
    n, per_peer, masks, (send_sems, recv_sems, *bufs) = handle

    def body(*refs):
        src_refs, land_refs = refs[:n], refs[n:2 * n]
        send_sems, recv_sems = refs[2 * n], refs[2 * n + 1]
        me = _my_place()
        for a in range(n):
            for mask in masks:
                cp = _push_copy(src_refs, land_refs, send_sems, recv_sems, a, mask, me, per_peer, False)
                cp.wait_send()
                cp.wait_recv()

    out = pl.pallas_call(
        body, name=name, out_shape=tuple(pltpu.HBM(v.shape, v.dtype) for v in bufs),
        in_specs=[_HBM] * (2 * n) + [_SEM, _SEM, _ANY], out_specs=tuple([_HBM] * (2 * n)),
        input_output_aliases={i: i for i in range(2 * n)},
        compiler_params=pltpu.CompilerParams(has_side_effects=_EFFECT),
    )(*bufs, send_sems, recv_sems, after)
    return list(out[n:])


def _relay_copy(land_refs, send_sems, recv_sems, a, j, me, outgoing):
    sibling = _peer(me, 1)
    out_slot = _slot(_peer(me, CHIP_PEERS[j]))
    in_slot = _slot(_peer(sibling, CHIP_PEERS[j]))
    k = a * len(CHIP_PEERS) + j
    return pltpu.make_async_remote_copy(
        src_ref=land_refs[a].at[out_slot], dst_ref=land_refs[a].at[out_slot if outgoing else in_slot],
        send_sem=send_sems.at[k], recv_sem=recv_sems.at[k], device_id=sibling, device_id_type=MESH)


def _relay_start(lands, *, name):
    n = len(lands)

    def body(*refs):
        land_refs, send_sems, recv_sems, token = refs[:n], refs[n], refs[n + 1], refs[-1]
        me = _my_place()
        for a in range(n):
            for j in range(len(CHIP_PEERS)):
                _relay_copy(land_refs, send_sems, recv_sems, a, j, me, True).start()
        token[...] = jnp.zeros_like(token)

    hbm_in = [pltpu.with_memory_space_constraint(v, pltpu.HBM) for v in lands]
    n_sem = n * len(CHIP_PEERS)
    out = pl.pallas_call(
        body, name=name,
        out_shape=(pltpu.SemaphoreType.DMA((n_sem,)), pltpu.SemaphoreType.DMA((n_sem,)),
                   *[pltpu.HBM(v.shape, v.dtype) for v in hbm_in], jax.ShapeDtypeStruct((8, LANES), F32)),
        in_specs=[_HBM] * n, out_specs=(_SEM, _SEM, *[_HBM] * n, pl.BlockSpec(memory_space=pltpu.VMEM)),
        input_output_aliases={i: 2 + i for i in range(n)},
        compiler_params=pltpu.CompilerParams(has_side_effects=_EFFECT),
    )(*hbm_in)
    return (n, out[:-1]), out[-1]


def _relay_wait(handle, after, *, name):
    n, (send_sems, recv_sems, *bufs) = handle

    def body(*refs):
        land_refs, send_sems, recv_sems = refs[:n], refs[n], refs[n + 1]
        me = _my_place()
        for a in range(n):
            for j in range(len(CHIP_PEERS)):
                cp = _relay_copy(land_refs, send_sems, recv_sems, a, j, me, False)
                cp.wait_send()
                cp.wait_recv()

    out = pl.pallas_call(
        body, name=name, out_shape=tuple(pltpu.HBM(v.shape, v.dtype) for v in bufs),
        in_specs=[_HBM] * n + [_SEM, _SEM, _ANY], out_specs=tuple([_HBM] * n),
        input_output_aliases={i: i for i in range(n)},
        compiler_params=pltpu.CompilerParams(has_side_effects=_EFFECT),
    )(*bufs, send_sems, recv_sems, after)
    return list(out)


def _all_reduce_rows(v, *, name):
    r, c = v.shape

    def body(v_ref, o_ref, buf, send_sems, recv_sems):
        me = _my_place()
        mine = _slot(me)
        sends = []
        for mask in range(1, N_DEV):
            peer = _peer(me, mask)
            sends.append(pltpu.make_async_remote_copy(
                src_ref=v_ref, dst_ref=buf.at[mine], send_sem=send_sems.at[mask - 1],
                recv_sem=recv_sems.at[mask - 1], device_id=peer, device_id_type=MESH))
        for cp in sends:
            cp.start()
        buf[mine] = v_ref[...]
        for mask in range(1, N_DEV):
            peer = _peer(me, mask)
            pltpu.make_async_remote_copy(
                src_ref=v_ref, dst_ref=buf.at[_slot(peer)], send_sem=send_sems.at[mask - 1],
                recv_sem=recv_sems.at[mask - 1], device_id=peer, device_id_type=MESH).wait_recv()
        for cp in sends:
            cp.wait_send()
        total = buf[0]
        for j in range(1, N_DEV):
            total = total + buf[j]
        o_ref[...] = total

    vm = pl.BlockSpec(memory_space=pltpu.VMEM)
    return pl.pallas_call(
        body, in_specs=[vm], out_specs=vm, out_shape=jax.ShapeDtypeStruct((r, c), F32),
        scratch_shapes=[pltpu.VMEM((N_DEV, r, c), F32), pltpu.SemaphoreType.DMA((7,)),
                        pltpu.SemaphoreType.DMA((7,))],
        name=name)(v)


def _xa_fwd(x, memn, g_x, wq, wkv, wo, tag):
    hx = _rms_fwd(x, g_x, name=f"xa{tag}_norm")
    q = _mm2(hx, wq, "nn", MXU_DTYPE, name=f"xa{tag}_q", tm=2048)
    kv = _mm2(memn, wkv, "nn", MXU_DTYPE, name=f"xa{tag}_kv", b_split=True)
    o = _xattn_fwd(q, kv, name=f"xa{tag}_attn")
    return _mm2(o, wo, "nn", F32, name=f"xa{tag}_out", add=x), (hx, memn, q, kv, o)


def _xa_bwd(x, mem, g_x, g_m, wq, wkv, wo, saved, dxo, dxo_lo, tag, put):
    hx, memn, q, kv, o = saved
    do = _mm2(dxo_lo, wo, "nt", MXU_DTYPE, name=f"xa{tag}_do", tm=2048)
    dwo = _mm2(o, dxo_lo, "tn", MXU_DTYPE, name=f"xa{tag}_dwo")
    dq, dkv = _xattn_bwd(q, kv, do, name=f"xa{tag}_attn_bwd")
    dwq = _mm2(hx, dq, "tn", MXU_DTYPE, name=f"xa{tag}_dwq")
    dwkv = _mm2(memn, dkv, "tn", MXU_DTYPE, name=f"xa{tag}_dwkv", o_split=True, tn=wkv.shape[2])
    tok = put((dwq, dwkv, dwo))
    dhx = _mm2(dq, wq, "nt", F32, name=f"xa{tag}_dh", dep=tok)
    dx, dx_lo, dgx = _rms_bwd(x, g_x, dhx, dxo, name=f"xa{tag}_norm_bwd")
    dmemn = _mm2(dkv, wkv, "nt", F32, name=f"xa{tag}_dmem", b_split=True)
    _, _, dgm = _rms_bwd(mem, g_m, dmemn, None, name=f"xa{tag}_mem_norm_bwd")
    return dx, dx_lo, dgx, dgm


def _ffn_fwd(x, g, wg, wu, wd, tag):
    h = _rms_fwd(x, g, name=f"ffn{tag}_norm")
    gate, up, act = _ffn_up(h, wg, wu, name=f"ffn{tag}_up")
    return _ffn_down(act, wd, x, name=f"ffn{tag}_down"), (h, gate, up, act)


def _ffn_bwd(x, g, wg, wu, wd, saved, dxo, dxo_lo, tag, put):
    h, gate, up, act = saved
    dwd = _mm(act, dxo_lo[None], "tn", MXU_DTYPE, name=f"ffn{tag}_dwd")
    dgate, dup = _ffn_dact(dxo_lo, wd, gate, up, name=f"ffn{tag}_dact")
    dwg = _mm(dgate, h[None], "tn", MXU_DTYPE, name=f"ffn{tag}_dwg")
    dwu = _mm(dup, h[None], "tn", MXU_DTYPE, name=f"ffn{tag}_dwu")
    tok = put((dwg, dwu, dwd))
    dh = _ffn_dh(dgate, dup, wg, wu, tok, name=f"ffn{tag}_dh")
    dx, dx_lo, dg = _rms_bwd(x, g, dh, dxo, name=f"ffn{tag}_norm_bwd")
    return dx, dx_lo, dg


def _even_fwd(x, h, lb, w_in, w_pool, pool_scale, hg_norm, w_out):
    z = _mm2(h, w_in, "nn", F32, name="ev_in", b_split=True)
    ycat = _pool_fwd(z, w_pool, pool_scale, name="ev_pool")
    ycat, o_raw, states = _hgrn_fwd(z, lb, hg_norm, ycat, name="ev_hgrn")
    return _mm2(ycat, w_out, "nn", F32, name="ev_out", add=x), (h, z, ycat, o_raw, states)


def _even_bwd(x, g, lb, w_in, w_pool, pool_scale, hg_norm, w_out, saved, dxo, dxo_lo, put):
    h, z, ycat, o_raw, states = saved
    dycat = _mm2(dxo_lo, w_out, "nt", MXU_DTYPE, name="ev_dy", tm=2048)
    dw_out = _mm2(ycat, dxo_lo, "tn", MXU_DTYPE, name="ev_dw_out")
    du, dw_pool, dscale = _pool_bwd(z, w_pool, pool_scale, dycat, name="ev_pool_bwd")
    dq, dfl, di, dg, dlb, dhn = _hgrn_bwd(z, lb, hg_norm, o_raw, states, dycat, name="ev_hgrn_bwd")
    dz = jnp.concatenate([du, dq, dfl, di, dg], axis=1)
    dw_in = _mm2(h, dz, "tn", MXU_DTYPE, name="ev_dw_in", o_split=True, tn=w_in.shape[2])
    tok = put((dw_in, dw_pool, dw_out))
    dh = _mm_nt_split(dz, w_in, tok, name="ev_dh")
    dx, dx_lo, dgn = _rms_bwd(x, g, dh, dxo, name="ev_norm_bwd")
    return dx, dx_lo, dict(ev_norm=dgn, ev_pool_scale=dscale, ev_hg_norm=dhn, lb=dlb)


def _odd_fwd(x, g, w_qkv, w_f, b_f, w_out):
    n_qkv = 3 * x.shape[1]
    h = _rms_fwd(x, g, name="od_norm")
    zqkv = _mm2(h, w_qkv, "nt", MXU_DTYPE, name="od_qkv", n_b=n_qkv)
    zf = _mm2(h, w_f, "nt", F32, name="od_gate")
    fcol = _fox_prep(zf, b_f, name="od_fox_prep")
    o, lse = _fox_fwd(zqkv, fcol, name="od_fox")
    return _mm2(o, w_out, "nn", F32, name="od_out", add=x), (h, zqkv, zf, fcol, o, lse)


def _odd_bwd(x, g, w_qkv, w_f, b_f, w_out, saved, dxo, dxo_lo, put):
    h, zqkv, zf, fcol, o, lse = saved
    do = _mm2(dxo_lo, w_out, "nt", MXU_DTYPE, name="od_do", tm=2048)
    dw_out = _mm2(o, dxo_lo, "tn", MXU_DTYPE, name="od_dw_out")
    dq, dk, dv, rq, rk = _fox_bwd(zqkv, fcol, lse, o, do, name="od_fox_bwd")
    dfl, db_f = _fox_gate_bwd(rq[:, 0, :], rk[:, 0, :], zf, b_f, name="od_fox_gate_bwd")
    dz = jnp.concatenate([dq, dk, dv], axis=1)
    dw_qkv = _mm2(dz, h, "tn", MXU_DTYPE, name="od_dw_qkv")
    dw_f = _mm2(dfl, h, "tn", MXU_DTYPE, name="od_dw_gate")
    tok = put((dw_qkv, dw_f, dw_out))
    dh = _mm2(dz, w_qkv, "nn", F32, name="od_dh_qkv", dep=tok)
    dh = _mm2(dfl, w_f, "nn", F32, name="od_dh_gate", add=dh)
    dx, dx_lo, dgn = _rms_bwd(x, g, dh, dxo, name="od_norm_bwd")
    return dx, dx_lo, dict(od_norm=dgn, od_b_f=db_f)


def _local_step(x, mem, target, sp, get_w, put_dw):
    b_f = jnp.pad(sp["od_b_f"], ((0, 0), (0, LANES - sp["od_b_f"].shape[1])))
    lb = _lb_fwd(sp["lb_table"], 0, name="lb_fwd")
    fin = sp["final_norm"].reshape(1, -1)
    xn, xm, fn = sp["xa_norm"], sp["xa_mem_norm"], sp["ffn_norm"]
    h0 = _rms_fwd(x, sp["ev_norm"], name="ev_norm")
    memn = [_rms_fwd(mem, xm[l:l + 1], name=f"xa{l}_mem_norm") for l in range(xm.shape[0])]
    w_ev = get_w("ev", h0)
    x1, s_ev = _even_fwd(x, h0, lb, w_ev[0], w_ev[1], sp["ev_pool_scale"], sp["ev_hg_norm"], w_ev[2])
    w_xa0 = get_w("xa0", x1)
    x2, s_xa0 = _xa_fwd(x1, memn[0], xn[0:1], *w_xa0, 0)
    w_ff0 = get_w("ffn0", x2)
    x3, s_ff0 = _ffn_fwd(x2, fn[0:1], *w_ff0, 0)
    w_qkv, w_f, w_od_out, od_norm = get_w("od", x3)
    x4, s_od = _odd_fwd(x3, od_norm, w_qkv, w_f, b_f, w_od_out)
    w_xa1 = get_w("xa1", x4)
    x5, s_xa1 = _xa_fwd(x4, memn[1], xn[1:2], *w_xa1, 1)
    w_ff1 = get_w("ffn1", x5)
    x6, s_ff1 = _ffn_fwd(x5, fn[1:2], *w_ff1, 1)
    loss, dx, dx_lo, d_fin = _loss_head(x6, fin, target, name="loss_head")
    put = lambda grp: functools.partial(put_dw, grp)
    dx, dx_lo, d_ffn1 = _ffn_bwd(x5, fn[1:2], *w_ff1, s_ff1, dx, dx_lo, 1, put("ffn1"))
    dx, dx_lo, d_xa1, d_xm1 = _xa_bwd(x4, mem, xn[1:2], xm[1:2], *w_xa1, s_xa1, dx, dx_lo, 1, put("xa1"))
    dx, dx_lo, d_od = _odd_bwd(x3, od_norm, w_qkv, w_f, b_f, w_od_out, s_od, dx, dx_lo, put("od"))
    dx, dx_lo, d_ffn0 = _ffn_bwd(x2, fn[0:1], *w_ff0, s_ff0, dx, dx_lo, 0, put("ffn0"))
    dx, dx_lo, d_xa0, d_xm0 = _xa_bwd(x1, mem, xn[0:1], xm[0:1], *w_xa0, s_xa0, dx, dx_lo, 0, put("xa0"))
    dx, _, d_ev = _even_bwd(x, sp["ev_norm"], lb, w_ev[0], w_ev[1], sp["ev_pool_scale"], sp["ev_hg_norm"],
                            w_ev[2], s_ev, dx, dx_lo, put("ev"))
    small = dict(
        lb_table=_lb_bwd(sp["lb_table"], d_ev["lb"], 0, name="lb_bwd"),
        ev_norm=d_ev["ev_norm"], ev_pool_scale=d_ev["ev_pool_scale"], ev_hg_norm=d_ev["ev_hg_norm"],
        od_norm=d_od["od_norm"], od_b_f=d_od["od_b_f"][:, :sp["od_b_f"].shape[1]],
        xa_norm=jnp.concatenate([d_xa0, d_xa1], axis=0), xa_mem_norm=jnp.concatenate([d_xm0, d_xm1], axis=0),
        ffn_norm=jnp.concatenate([d_ffn0, d_ffn1], axis=0), final_norm=d_fin.reshape(-1))
    return loss, dx, small


_SMALL = ("lb_table", "ev_norm", "ev_pool_scale", "ev_hg_norm", "od_norm", "od_b_f", "xa_norm", "xa_mem_norm",
          "ffn_norm", "final_norm")
_WEIGHTS = ("lb_table", "ev_norm", "ev_w_in", "ev_w_pool", "ev_pool_scale", "ev_hg_norm", "ev_w_out", "od_norm",
            "od_w_in", "od_b_f", "od_w_out", "xa_norm", "xa_mem_norm", "xa_wq", "xa_wkv", "xa_wo", "ffn_norm",
            "ffn_w_gate", "ffn_w_up", "ffn_w_down", "final_norm")


def _lo(a):
    return a.astype(MXU_DTYPE)


def _rows(v):
    flat = v.reshape(-1)
    return jnp.pad(flat, (0, (-flat.shape[0]) % LANES)).reshape(-1, LANES)


def kernel(x, mem, lb_table, ev_norm, ev_w_in, ev_w_pool, ev_pool_scale, ev_hg_norm, ev_w_out, od_norm, od_w_in, od_b_f, od_w_out, xa_norm, xa_mem_norm, xa_wq, xa_wkv, xa_wo, ffn_norm, ffn_w_gate, ffn_w_up, ffn_w_down, final_norm, loss_target, m_lb_table, m_ev_norm, m_ev_w_in, m_ev_w_pool, m_ev_pool_scale, m_ev_hg_norm, m_ev_w_out, m_od_norm, m_od_w_in, m_od_b_f, m_od_w_out, m_xa_norm, m_xa_mem_norm, m_xa_wq, m_xa_wkv, m_xa_wo, m_ffn_norm, m_ffn_w_gate, m_ffn_w_up, m_ffn_w_down, m_final_norm, v_lb_table, v_ev_norm, v_ev_w_in, v_ev_w_pool, v_ev_pool_scale, v_ev_hg_norm, v_ev_w_out, v_od_norm, v_od_w_in, v_od_b_f, v_od_w_out, v_xa_norm, v_xa_mem_norm, v_xa_wq, v_xa_wkv, v_xa_wo, v_ffn_norm, v_ffn_w_gate, v_ffn_w_up, v_ffn_w_down, v_final_norm):
    arg = dict(locals())
    d = x.shape[-1]
    layers = xa_wq.shape[0]
    me = _slot(_my_place())

    n_gate = od_b_f.shape[1]
    turned = {k: jnp.swapaxes(arg[k], 1, 2) for k in ("od_w_in", "ffn_w_gate", "ffn_w_up")}
    raw = dict(ev=[ev_w_in[0], ev_w_pool[0], ev_w_out[0]], od=[turned["od_w_in"][0], od_w_out[0], od_norm])
    for l in range(layers):
        raw[f"xa{l}"] = [xa_wq[l], xa_wkv[l], xa_wo[l]]
        raw[f"ffn{l}"] = [turned["ffn_w_gate"][l], turned["ffn_w_up"][l], ffn_w_down[l]]
    order = ("ev", "xa0", "ffn0", "od", "xa1", "ffn1")
    gathers, relays, tok = {}, {}, None
    for grp in order:
        srcs = [w if tok is None else w + tok[0, 0] for w in raw[grp]]
        srcs = [w if grp == "od" and j == 2 else _lo(w) for j, w in enumerate(srcs)]
        gathers[grp], tok = _push_start(srcs, None, per_peer=False, masks=FIRST_HOP, name=f"gather_{grp}_start")
    last_start = tok

    def second_hop(grp, after):
        lands = _push_wait(gathers[grp], after, name=f"gather_{grp}_wait")
        relays[grp], token = _relay_start(lands, name=f"gather_{grp}_relay")
        return token

    def get_w(grp, after):
        i = order.index(grp)
        if i == 0:
            after = last_start if after is None else last_start + after[:8, :LANES].astype(F32)
        if grp not in relays:
            after = second_hop(grp, after)
        if 1 <= i < len(order) - 1:
            after = second_hop(order[i + 1], after)
        got = _relay_wait(relays[grp], after, name=f"gather_{grp}_relay_wait")
        if grp == "ev":
            w_in, w_pool, w_out = got
            w_pool = jnp.transpose(w_pool, (1, 0, 2, 3)).reshape(w_pool.shape[1], -1, w_pool.shape[3])
            return w_in, w_pool, w_out.reshape(d, d)
        if grp == "od":
            w_in, w_out, nrm = got
            assert n_gate <= w_in.shape[1]
            w_f = jnp.pad(w_in[N_DEV - 1, w_in.shape[1] - n_gate:], ((0, LANES - n_gate), (0, 0)))
            return w_in.reshape(-1, d), w_f, w_out.reshape(d, d), nrm.reshape(1, d)
        if grp.startswith("xa"):
            return got[0].reshape(d, d), got[1], got[2].reshape(d, d)
        return tuple(got)

    def row_parts(g):
        return g.reshape(N_DEV, -1, g.shape[-1])

    scatters = {}

    def put_dw(grp, dws):
        if grp == "ev":
            dw_in, dw_pool, dw_out = dws
            gc = dw_pool.shape[1] // N_DEV
            dw_pool = _lo(jnp.transpose(dw_pool.reshape(dw_pool.shape[0], N_DEV, gc, -1), (1, 0, 2, 3)))
            parts = [dw_in, dw_pool, row_parts(dw_out)]
        elif grp == "od":
            dw_qkv, dw_f, dw_out = dws
            parts = [row_parts(jnp.concatenate([dw_qkv, dw_f[:n_gate]], axis=0)), row_parts(dw_out)]
        elif grp.startswith("xa"):
            parts = [row_parts(dws[0]), dws[1], row_parts(dws[2])]
        else:
            parts = list(dws)
        scatters[grp], token = _push_start(parts, None, per_peer=True, name=f"scatter_{grp}_start")
        return token

    sp = {k: arg[k] for k in _SMALL if k != "od_norm"}
    loss, dx, small = _local_step(x[0], mem[0], loss_target[0], sp, get_w, put_dw)

    pieces = [_rows(small[k]) for k in _SMALL]
    packed = jnp.concatenate(pieces + [_rows(loss)], axis=0)
    packed = jnp.pad(packed, ((0, (-packed.shape[0]) % 8), (0, 0)))
    total = _all_reduce_rows(packed, name="all_reduce_small")
    loss = total[sum(pc.shape[0] for pc in pieces), 0]
    small_g, at = {}, 0
    for k, pc in zip(_SMALL, pieces):
        n = small[k].size
        small_g[k] = total[at:at + pc.shape[0]].reshape(-1)[:n].reshape(small[k].shape)
        at += pc.shape[0]
    small_g["od_norm"] = lax.dynamic_slice_in_dim(small_g["od_norm"], me * od_norm.shape[1], od_norm.shape[1], axis=1)

    res = {}
    for k in _SMALL:
        w, m, v = arg[k], arg["m_" + k], arg["v_" + k]
        shp = (1, 1, w.shape[0]) if w.ndim == 1 else (1,) + w.shape
        out = _adamw(w.reshape(shp), m.reshape(shp), v.reshape(shp), small_g[k].reshape(shp), name=f"adamw_{k}")
        res[k] = [o.reshape(w.shape) for o in out]
    members = dict(ev=("ev_w_in", "ev_w_pool", "ev_w_out"), od=("od_w_in", "od_w_out"),
                   xa=("xa_wq", "xa_wkv", "xa_wo"), ffn=("ffn_w_gate", "ffn_w_up", "ffn_w_down"))
    after, stacked = dx, {}
    for grp in ("ffn1", "xa1", "od", "ffn0", "xa0", "ev"):
        got = _push_wait(scatters[grp], after, name=f"scatter_{grp}_wait")
        kind = grp.rstrip("01")
        for k, parts in zip(members[kind], got):
            w, m, v = [jnp.swapaxes(a, 1, 2) if k in turned else a for a in (arg[k], arg["m_" + k], arg["v_" + k])]
            if w.shape[0] == 1:
                shp = (1, -1, w.shape[-1])
                out = _adamw(w.reshape(shp), m.reshape(shp), v.reshape(shp), parts.reshape(N_DEV, -1, w.shape[-1]),
                             name=f"adamw_{k}")
            else:
                out = _adamw(w, m, v, parts, name=f"adamw_{k}{grp[-1]}", layer=int(grp[-1]), prev=stacked.get(k))
                stacked[k] = out
            res[k] = [jnp.swapaxes(o.reshape(w.shape), 1, 2) if k in turned else o.reshape(w.shape) for o in out]
            after = out[3][:1, :8, :LANES]

    outs = [loss, dx[None]]
    for j in range(4):
        outs += [res[k][j] for k in _WEIGHTS]
    return tuple(outs)
```

```python
import functools

import jax
import jax.numpy as jnp
from jax import lax
from jax.experimental import pallas as pl
from jax.experimental.pallas import tpu as pltpu

F32 = jnp.float32
MXU_DTYPE = jnp.bfloat16
EPS = 1e-6
N_DEV = 8
V7X_VMEM_BYTES = 64 * 1024 * 1024
VMEM_LIMIT_BYTES = V7X_VMEM_BYTES - 8 * 1024 * 1024
LANES = 128
HIGHEST = lax.Precision.HIGHEST
MESH = pl.DeviceIdType.MESH

HG_HEAD = 128
HG_CHUNK = 32
FOX_HEAD = 128
XA_HEADS = 4
POOL_GROUPS = 4

ADAM_LR = 0.001
ADAM_B1 = 0.9
ADAM_B2 = 0.999
ADAM_EPS = 1e-08
ADAM_WD = 0.01
ADAM_STEP = 10

_NN = ((1,), (0,))
_NT = ((1,), (1,))
_TN = ((0,), (0,))


def _dot(a, b, dims):
    return lax.dot_general(a.astype(MXU_DTYPE), b.astype(MXU_DTYPE), (dims, ((), ())),
                           preferred_element_type=F32)


def _dot_f32(a, b, dims):
    return lax.dot_general(a, b, (dims, ((), ())), preferred_element_type=F32, precision=HIGHEST)


def _cp(*sem):
    return pltpu.CompilerParams(dimension_semantics=sem, vmem_limit_bytes=VMEM_LIMIT_BYTES)


def _tile(n, pref, align=LANES):
    if n <= pref:
        return n
    t = (pref // align) * align
    while t >= align:
        if n % t == 0:
            return t
        t -= align
    return n


def _sigmoid(x):
    return jax.nn.sigmoid(x)


def _mm(a, b, mode, out_dtype, *, name, add=None, dep=None, reduce_b=False, b_split=False, o_split=False,
        n_b=None, tm=1024, tn=1024, tk=2048):
    ba, bb = a.shape[0], b.shape[0]
    if mode == "tn":
        kdim, m = a.shape[1], a.shape[2]
        tk = 2 * tk
    else:
        m, kdim = a.shape[1], a.shape[2]
    if b_split:
        s_cnt, b_rows, w = b.shape
        if mode == "nt":
            n = b_rows
            assert kdim == s_cnt * w
            tk = w
        else:
            n = s_cnt * w
            assert b_rows == kdim
            tn = w
        nb = ba
    else:
        n = b.shape[1] if mode == "nt" else b.shape[2]
        n = n if n_b is None else n_b
        nb = max(ba, bb)
    if not (b_split and mode != "nt"):
        tn = _tile(n, tn)
    if not (b_split and mode == "nt"):
        tk = _tile(kdim, tk)
    tm = _tile(m, tm)
    assert m % tm == 0 and n % tn == 0 and kdim % tk == 0, (name, m, n, kdim, tm, tn, tk)
    nk = kdim // tk
    if reduce_b:
        grid = (m // tm, n // tn, nb, nk)
        unpack = lambda i, j, bi, k: (bi, i, j, k)
        sem = ("parallel", "parallel", "arbitrary", "arbitrary")
        nred = nb * nk
    else:
        grid = (nb, m // tm, n // tn, nk)
        unpack = lambda bi, i, j, k: (bi, i, j, k)
        sem = ("parallel", "parallel", "parallel", "arbitrary")
        nred = nk

    def a_map(*g):
        bi, i, j, k = unpack(*g)
        ab = bi if ba > 1 else 0
        return (ab, k, i) if mode == "tn" else (ab, i, k)

    def b_map(*g):
        bi, i, j, k = unpack(*g)
        if b_split:
            return (k, j, 0) if mode == "nt" else (j, k, 0)
        bq = bi if bb > 1 else 0
        return (bq, j, k) if mode == "nt" else (bq, k, j)

    def o_map(*g):
        bi, i, j, k = unpack(*g)
        if o_split:
            return (j, i, 0)
        return (0 if reduce_b else bi, i, j)

    a_blk = (1, tk, tm) if mode == "tn" else (1, tm, tk)
    b_blk = (1, tn, tk) if mode == "nt" else (1, tk, tn)
    dims = {"nn": _NN, "nt": _NT, "tn": _TN}[mode]
    has_add = add is not None

    def body(*refs):
        a_ref, b_ref = refs[:2]
        if has_add:
            add_ref = refs[2]
        if nred == 1:
            o_ref = refs[-1]
            r = _dot(a_ref[0], b_ref[0], dims)
            if has_add:
                r = r + add_ref[0].astype(F32)
            o_ref[0] = r.astype(o_ref.dtype)
            return
        o_ref, acc_ref = refs[-2:]
        if reduce_b:
            step = pl.program_id(2) * nk + pl.program_id(3)
        else:
            step = pl.program_id(3)

        @pl.when(step == 0)
        def _():
            acc_ref[...] = _dot(a_ref[0], b_ref[0], dims)

        @pl.when(step > 0)
        def _():
            acc_ref[...] += _dot(a_ref[0], b_ref[0], dims)

        @pl.when(step == nred - 1)
        def _():
            r = acc_ref[...]
            if has_add:
                r = r + add_ref[0].astype(F32)
            o_ref[0] = r.astype(o_ref.dtype)

    in_specs = [pl.BlockSpec(a_blk, a_map), pl.BlockSpec(b_blk, b_map)]
    operands = [a, b]
    if has_add:
        in_specs.append(pl.BlockSpec((1, tm, tn), o_map))
        operands.append(add)
    if dep is not None:
        in_specs.append(pl.BlockSpec(memory_space=pl.ANY))
        operands.append(dep)
    if o_split:
        out_shape = jax.ShapeDtypeStruct((n // tn, m, tn), out_dtype)
    else:
        out_shape = jax.ShapeDtypeStruct((1 if reduce_b else nb, m, n), out_dtype)
    return pl.pallas_call(
        body, grid=grid, in_specs=in_specs, out_specs=pl.BlockSpec((1, tm, tn), o_map),
        out_shape=out_shape, scratch_shapes=[] if nred == 1 else [pltpu.VMEM((tm, tn), F32)],
        compiler_params=_cp(*sem), name=name)(*operands)


def _mm_nt_split(a, b, dep, *, name, tm=512, tn=1024):
    m, k = a.shape
    s, n, w = b.shape
    assert k == s * w and w % LANES == 0
    tm, tn = _tile(m, tm), _tile(n, tn)
    has_dep = dep is not None

    def body(*refs):
        a_ref, b_ref, o_ref = refs[0], refs[1], refs[-1]
        r = _dot(a_ref[:, 0:w], b_ref[0], _NT)
        for q in range(1, s):
            r = r + _dot(a_ref[:, q * w:(q + 1) * w], b_ref[q], _NT)
        o_ref[...] = r

    return pl.pallas_call(
        body, grid=(n // tn, m // tm),
        in_specs=[pl.BlockSpec((tm, k), lambda j, i: (i, 0)), pl.BlockSpec((s, tn, w), lambda j, i: (0, j, 0))]
        + ([pl.BlockSpec(memory_space=pl.ANY)] if has_dep else []),
        out_specs=pl.BlockSpec((tm, tn), lambda j, i: (i, j)), out_shape=jax.ShapeDtypeStruct((m, n), F32),
        compiler_params=_cp("parallel", "parallel"), name=name)(a, b, *([dep] if has_dep else []))


def _mm2(a, b, mode, out_dtype, *, name, add=None, **kw):
    b3 = b if kw.get("b_split") else b[None]
    r = _mm(a[None], b3, mode, out_dtype, name=name, add=None if add is None else add[None], **kw)
    return r if kw.get("o_split") else r[0]


def _rms_fwd(x, g, *, name, tb=512):
    t, d = x.shape
    tb = min(tb, t)

    def body(x_ref, g_ref, o_ref):
        xv = x_ref[...]
        r = lax.rsqrt(jnp.mean(xv * xv, axis=-1, keepdims=True) + EPS)
        o_ref[...] = (xv * r * g_ref[...]).astype(o_ref.dtype)

    return pl.pallas_call(
        body, grid=(t // tb,),
        in_specs=[pl.BlockSpec((tb, d), lambda i: (i, 0)), pl.BlockSpec((1, d), lambda i: (0, 0))],
        out_specs=pl.BlockSpec((tb, d), lambda i: (i, 0)),
        out_shape=jax.ShapeDtypeStruct((t, d), MXU_DTYPE), compiler_params=_cp("parallel"), name=name)(x, g)


def _rms_bwd(x, g, dh, dres, *, name, tb=512):
    t, d = x.shape
    tb = min(tb, t)
    has_res = dres is not None

    def body(*refs):
        if has_res:
            x_ref, g_ref, dh_ref, dres_ref, dx_ref, dxl_ref, dg_ref = refs
        else:
            x_ref, g_ref, dh_ref, dx_ref, dxl_ref, dg_ref = refs
        xv = x_ref[...]
        r = lax.rsqrt(jnp.mean(xv * xv, axis=-1, keepdims=True) + EPS)
        xh = xv * r
        dhv = dh_ref[...].astype(F32)

        @pl.when(pl.program_id(0) == 0)
        def _():
            dg_ref[...] = jnp.zeros_like(dg_ref)

        dg_ref[...] += jnp.sum(dhv * xh, axis=0, keepdims=True)
        dxh = dhv * g_ref[...]
        dx = r * (dxh - xh * jnp.mean(dxh * xh, axis=-1, keepdims=True))
        if has_res:
            dx = dx + dres_ref[...]
        dx_ref[...] = dx
        dxl_ref[...] = dx.astype(dxl_ref.dtype)

    row = pl.BlockSpec((tb, d), lambda i: (i, 0))
    vec = pl.BlockSpec((1, d), lambda i: (0, 0))
    operands = [x, g, dh] + ([dres] if has_res else [])
    return pl.pallas_call(
        body, grid=(t // tb,), in_specs=[row, vec, row] + ([row] if has_res else []),
        out_specs=[row, row, vec],
        out_shape=[jax.ShapeDtypeStruct((t, d), F32), jax.ShapeDtypeStruct((t, d), MXU_DTYPE),
                   jax.ShapeDtypeStruct((1, d), F32)],
        compiler_params=_cp("arbitrary"), name=name)(*operands)


def _loss_head(x, g, target, *, name, tb=512):
    t, d = x.shape
    tb = min(tb, t)

    def body(x_ref, g_ref, t_ref, loss_ref, dx_ref, dxl_ref, dg_ref):
        xv = x_ref[...]
        r = lax.rsqrt(jnp.mean(xv * xv, axis=-1, keepdims=True) + EPS)
        xh = xv * r
        gv = g_ref[...]
        err = xh * gv - t_ref[...]

        @pl.when(pl.program_id(0) == 0)
        def _():
            dg_ref[...] = jnp.zeros_like(dg_ref)
            loss_ref[...] = jnp.zeros_like(loss_ref)

        row_loss = jnp.mean(err * err, axis=-1, keepdims=True)
        loss_ref[...] += 0.5 * jnp.sum(row_loss, axis=0, keepdims=True)
        dy = err * (1.0 / d)
        dg_ref[...] += jnp.sum(dy * xh, axis=0, keepdims=True)
        dxh = dy * gv
        dx = r * (dxh - xh * jnp.mean(dxh * xh, axis=-1, keepdims=True))
        dx_ref[...] = dx
        dxl_ref[...] = dx.astype(dxl_ref.dtype)

    row = pl.BlockSpec((tb, d), lambda i: (i, 0))
    vec = pl.BlockSpec((1, d), lambda i: (0, 0))
    return pl.pallas_call(
        body, grid=(t // tb,), in_specs=[row, vec, row],
        out_specs=[pl.BlockSpec((1, 1), lambda i: (0, 0)), row, row, vec],
        out_shape=[jax.ShapeDtypeStruct((1, 1), F32), jax.ShapeDtypeStruct((t, d), F32),
                   jax.ShapeDtypeStruct((t, d), MXU_DTYPE), jax.ShapeDtypeStruct((1, d), F32)],
        compiler_params=_cp("arbitrary"), name=name)(x, g, target)


def _ffn_up(h, wg, wu, *, name, tb=1024):
    t, d = h.shape
    s, f, _ = wg.shape
    tb = min(tb, t)

    def body(h_ref, wg_ref, wu_ref, g_ref, u_ref, a_ref):
        hv = h_ref[...]
        gv = _dot(hv, wg_ref[0], _NT)
        uv = _dot(hv, wu_ref[0], _NT)
        g_ref[0] = gv.astype(g_ref.dtype)
        u_ref[0] = uv.astype(u_ref.dtype)
        a_ref[0] = (gv * _sigmoid(gv) * uv).astype(a_ref.dtype)

    wspec = pl.BlockSpec((1, f, d), lambda j, i: (j, 0, 0))
    ospec = pl.BlockSpec((1, tb, f), lambda j, i: (j, i, 0))
    return pl.pallas_call(
        body, grid=(s, t // tb),
        in_specs=[pl.BlockSpec((tb, d), lambda j, i: (i, 0)), wspec, wspec],
        out_specs=[ospec, ospec, ospec],
        out_shape=[jax.ShapeDtypeStruct((s, t, f), MXU_DTYPE)] * 3,
        compiler_params=_cp("parallel", "parallel"), name=name)(h, wg, wu)


def _ffn_dact(dy, wd, gate, up, *, name, tb=1024):
    t, d = dy.shape
    s, f, _ = wd.shape
    tb = min(tb, t)

    def body(dy_ref, wd_ref, g_ref, u_ref, dg_ref, du_ref):
        da = _dot(dy_ref[...], wd_ref[0], _NT)
        gv = g_ref[0].astype(F32)
        sg = _sigmoid(gv)
        du_ref[0] = (da * gv * sg).astype(du_ref.dtype)
        dg_ref[0] = (da * u_ref[0].astype(F32) * (sg * (1.0 + gv * (1.0 - sg)))).astype(dg_ref.dtype)

    aspec = pl.BlockSpec((1, tb, f), lambda i, j: (j, i, 0))
    return pl.pallas_call(
        body, grid=(t // tb, s),
        in_specs=[pl.BlockSpec((tb, d), lambda i, j: (i, 0)),
                  pl.BlockSpec((1, f, d), lambda i, j: (j, 0, 0)), aspec, aspec],
        out_specs=[aspec, aspec],
        out_shape=[jax.ShapeDtypeStruct((s, t, f), MXU_DTYPE), jax.ShapeDtypeStruct((s, t, f), MXU_DTYPE)],
        compiler_params=_cp("parallel", "parallel"), name=name)(dy, wd, gate, up)


def _ffn_down(act, wd, x, *, name, tm=512, tn=1024):
    s, t, f = act.shape
    d = wd.shape[2]
    tm, tn = _tile(t, tm), _tile(d, tn)

    def body(a_ref, w_ref, x_ref, o_ref):
        r = x_ref[...]
        for j in range(s):
            r = r + _dot(a_ref[j], w_ref[j], _NN)
        o_ref[...] = r

    xspec = pl.BlockSpec((tm, tn), lambda k, i: (i, k))
    return pl.pallas_call(
        body, grid=(d // tn, t // tm),
        in_specs=[pl.BlockSpec((s, tm, f), lambda k, i: (0, i, 0)), pl.BlockSpec((s, f, tn), lambda k, i: (0, 0, k)),
                  xspec],
        out_specs=xspec, out_shape=jax.ShapeDtypeStruct((t, d), F32),
        compiler_params=_cp("parallel", "parallel"), name=name)(act, wd, x)


def _ffn_dh(dgate, dup, wg, wu, dep, *, name, tm=512, tn=2048, sg=2):
    s, t, f = dgate.shape
    d = wg.shape[2]
    tm, tn = _tile(t, tm), _tile(d, tn)
    steps = s // sg
    has_dep = dep is not None

    def body(*refs):
        dg_ref, du_ref, wg_ref, wu_ref = refs[:4]
        o_ref, acc_ref = refs[-2:]
        j = pl.program_id(2)
        part = _dot(dg_ref[0], wg_ref[0], _NN) + _dot(du_ref[0], wu_ref[0], _NN)
        for q in range(1, sg):
            part = part + _dot(dg_ref[q], wg_ref[q], _NN) + _dot(du_ref[q], wu_ref[q], _NN)

        @pl.when(j == 0)
        def _():
            acc_ref[...] = part

        @pl.when(j > 0)
        def _():
            acc_ref[...] += part

        @pl.when(j == steps - 1)
        def _():
            o_ref[...] = acc_ref[...]

    aspec = pl.BlockSpec((sg, tm, f), lambda i, k, j: (j, i, 0))
    wspec = pl.BlockSpec((sg, f, tn), lambda i, k, j: (j, 0, k))
    return pl.pallas_call(
        body, grid=(t // tm, d // tn, steps),
        in_specs=[aspec, aspec, wspec, wspec] + ([pl.BlockSpec(memory_space=pl.ANY)] if has_dep else []),
        out_specs=pl.BlockSpec((tm, tn), lambda i, k, j: (i, k)),
        out_shape=jax.ShapeDtypeStruct((t, d), F32), scratch_shapes=[pltpu.VMEM((tm, tn), F32)],
        compiler_params=_cp("parallel", "parallel", "arbitrary"),
        name=name)(dgate, dup, wg, wu, *([dep] if has_dep else []))


def _xattn_fwd(q, kv, *, name, tb=512):
    t, d = q.shape
    m = kv.shape[0]
    hd = d // XA_HEADS
    tb = min(tb, t)
    scale = hd ** -0.5

    def body(q_ref, kv_ref, o_ref):
        for hh in range(XA_HEADS):
            cs = slice(hh * hd, (hh + 1) * hd)
            s = _dot(q_ref[:, cs], kv_ref[:, cs], _NT) * scale
            s = s - jnp.max(s, axis=-1, keepdims=True)
            e = jnp.exp(s)
            p = e / jnp.sum(e, axis=-1, keepdims=True)
            o_ref[:, cs] = _dot(p, kv_ref[:, d + hh * hd:d + (hh + 1) * hd], _NN).astype(o_ref.dtype)

    return pl.pallas_call(
        body, grid=(t // tb,),
        in_specs=[pl.BlockSpec((tb, d), lambda i: (i, 0)), pl.BlockSpec((m, 2 * d), lambda i: (0, 0))],
        out_specs=pl.BlockSpec((tb, d), lambda i: (i, 0)),
        out_shape=jax.ShapeDtypeStruct((t, d), MXU_DTYPE), compiler_params=_cp("parallel"), name=name)(q, kv)


def _xattn_bwd(q, kv, do, *, name, tb=512):
    t, d = q.shape
    m = kv.shape[0]
    hd = d // XA_HEADS
    tb = min(tb, t)
    scale = hd ** -0.5

    def body(q_ref, kv_ref, do_ref, dq_ref, dkv_ref):
        @pl.when(pl.program_id(0) == 0)
        def _():
            dkv_ref[...] = jnp.zeros_like(dkv_ref)

        for hh in range(XA_HEADS):
            cs = slice(hh * hd, (hh + 1) * hd)
            vs = slice(d + hh * hd, d + (hh + 1) * hd)
            qv, kk, vv, dov = q_ref[:, cs], kv_ref[:, cs], kv_ref[:, vs], do_ref[:, cs]
            s = _dot(qv, kk, _NT) * scale
            s = s - jnp.max(s, axis=-1, keepdims=True)
            e = jnp.exp(s)
            p = e / jnp.sum(e, axis=-1, keepdims=True)
            dkv_ref[:, vs] += _dot(p, dov, _TN)
            dp = _dot(dov, vv, _NT)
            ds = p * (dp - jnp.sum(p * dp, axis=-1, keepdims=True)) * scale
            dq_ref[:, cs] = _dot(ds, kk, _NN).astype(dq_ref.dtype)
            dkv_ref[:, cs] += _dot(ds, qv, _TN)

    row = pl.BlockSpec((tb, d), lambda i: (i, 0))
    full = pl.BlockSpec((m, 2 * d), lambda i: (0, 0))
    return pl.pallas_call(
        body, grid=(t // tb,), in_specs=[row, full, row], out_specs=[row, full],
        out_shape=[jax.ShapeDtypeStruct((t, d), MXU_DTYPE), jax.ShapeDtypeStruct((m, 2 * d), F32)],
        compiler_params=_cp("arbitrary"), name=name)(q, kv, do)


def _pool_window_stats(u, gi, reverse):
    t = u.shape[0]
    row = lax.broadcasted_iota(jnp.int32, u.shape, 0)
    s = u
    for j in range(POOL_GROUPS):
        sh = 1 << j
        if reverse:
            rolled = jnp.where(row < t - sh, pltpu.roll(s, t - sh, axis=0), 0.0)
        else:
            rolled = jnp.where(row >= sh, pltpu.roll(s, sh, axis=0), 0.0)
        s = jnp.where(j <= gi, s + rolled, s)
    return s, row


def _pool_fwd(z, w_pool, scale, *, name):
    t = z.shape[0]
    g_cnt, c, _ = w_pool.shape

    def body(z_ref, w_ref, s_ref, o_ref):
        gi = pl.program_id(0)
        u = z_ref[...]
        win, row = _pool_window_stats(u, gi, False)
        cnt = jnp.minimum(row + 1, lax.shift_left(jnp.int32(2), gi)).astype(F32)
        p = win / cnt - u
        o_ref[...] = (_dot(p, w_ref[0], _NN) * s_ref[...]).astype(o_ref.dtype)

    return pl.pallas_call(
        body, grid=(g_cnt,),
        in_specs=[pl.BlockSpec((t, c), lambda g: (0, g)), pl.BlockSpec((1, c, c), lambda g: (g, 0, 0)),
                  pl.BlockSpec((1, c), lambda g: (0, g))],
        out_specs=pl.BlockSpec((t, c), lambda g: (0, g)),
        out_shape=jax.ShapeDtypeStruct((t, 2 * g_cnt * c), MXU_DTYPE),
        compiler_params=_cp("parallel"), name=name)(z, w_pool, scale)


def _pool_bwd(z, w_pool, scale, dycat, *, name):
    t = z.shape[0]
    g_cnt, c, _ = w_pool.shape

    def body(z_ref, w_ref, s_ref, dy_ref, du_ref, dw_ref, ds_ref):
        gi = pl.program_id(0)
        u = z_ref[...]
        win, row = _pool_window_stats(u, gi, False)
        cnt = jnp.minimum(row + 1, lax.shift_left(jnp.int32(2), gi)).astype(F32)
        p = win / cnt - u
        y = _dot(p, w_ref[0], _NN)
        dya = dy_ref[...].astype(F32)
        ds_ref[...] = jnp.sum(dya * y, axis=0, keepdims=True)
        dy = dya * s_ref[...]
        dw_ref[0] = _dot(p, dy, _TN)
        dp = _dot(dy, w_ref[0], _NT)
        back, _ = _pool_window_stats(dp / cnt, gi, True)
        du_ref[...] = (back - dp).astype(du_ref.dtype)

    col = pl.BlockSpec((t, c), lambda g: (0, g))
    return pl.pallas_call(
        body, grid=(g_cnt,),
        in_specs=[col, pl.BlockSpec((1, c, c), lambda g: (g, 0, 0)), pl.BlockSpec((1, c), lambda g: (0, g)), col],
        out_specs=[col, pl.BlockSpec((1, c, c), lambda g: (g, 0, 0)), pl.BlockSpec((1, c), lambda g: (0, g))],
        out_shape=[jax.ShapeDtypeStruct((t, g_cnt * c), MXU_DTYPE), jax.ShapeDtypeStruct((g_cnt, c, c), F32),
                   jax.ShapeDtypeStruct((1, g_cnt * c), F32)],
        compiler_params=_cp("parallel"), name=name)(z, w_pool, scale, dycat)


def _chunk_tri(lower):
    r = lax.broadcasted_iota(jnp.int32, (LANES, LANES), 0)
    c = lax.broadcasted_iota(jnp.int32, (LANES, LANES), 1)
    same = (r // HG_CHUNK) == (c // HG_CHUNK)
    return jnp.where(same & ((c <= r) if lower else (c >= r)), 1.0, 0.0).astype(F32)


def _hgrn_prepare(q_ref, f_ref, lb_ref, qh_s, k_s, b_s, qt_s, kt_s, gl_s):
    tb = q_ref.shape[0]
    lb = lb_ref[...]
    sg = _sigmoid(f_ref[...])
    f = lb + (1.0 - lb) * sg
    logf = jnp.log(f)
    qv = q_ref[...]
    qh = qv * _sigmoid(qv) * (HG_HEAD ** -0.5)
    tri = _chunk_tri(True)
    for r in range(tb // LANES):
        rows = slice(r * LANES, (r + 1) * LANES)
        b_s[rows, :] = _dot_f32(tri, logf[rows, :], _NN)
    b = b_s[...]
    b3 = b.reshape(tb // HG_CHUNK, HG_CHUNK, HG_HEAD)
    bl = b3[:, HG_CHUNK - 1:HG_CHUNK, :]
    k = 1.0 - f
    qh_s[...] = qh
    k_s[...] = k
    qt_s[...] = qh * jnp.exp(b)
    kt_s[...] = k * jnp.exp(bl - b3).reshape(tb, HG_HEAD)
    gl_s[...] = jnp.exp(jnp.broadcast_to(bl, b3.shape)).reshape(tb, HG_HEAD)
    return sg, f


SUBLANES = 8
HG_GROUPS = HG_CHUNK // SUBLANES


def _hgrn_intra(qh, kk, bq, rows_a, rows_b):
    ones = jnp.ones((HG_HEAD, HG_HEAD), MXU_DTYPE)
    es, stack_a, stack_b, starts, at = [], [], [], [], 0
    for s in range(HG_CHUNK):
        lo = (s // SUBLANES) * SUBLANES
        e = jnp.exp(jnp.minimum(bq[lo:, :] - bq[s:s + 1, :], 0.0))
        es.append(e)
        stack_a.append(qh[lo:, :] * e * kk[s:s + 1, :])
        if rows_a is not None:
            stack_b.append(rows_a[lo:, :] * rows_b[s:s + 1, :])
        starts.append(at)
        at += HG_CHUNK - lo
    a_rep = _dot(jnp.concatenate(stack_a, axis=0), ones, _NN)
    d_rep = _dot(jnp.concatenate(stack_b, axis=0), ones, _NN) if rows_a is not None else None
    return es, a_rep, d_rep, starts


def _groups(v):
    return [v[g * SUBLANES:(g + 1) * SUBLANES, :] for g in range(HG_GROUPS)]


def _hgrn_fwd(z, lb, hg_norm, ycat, *, name, tb=1024):
    t = z.shape[0]
    mix_b = lb.shape[1]
    heads = mix_b // HG_HEAD
    off = (z.shape[1] - 4 * mix_b) // HG_HEAD
    tb = min(tb, t)
    ncb = tb // HG_CHUNK

    def body(q_ref, f_ref, i_ref, g_ref, lb_ref, hn_ref, ycat_in, y_ref, o_ref, st_ref,
             state, qh_s, k_s, b_s, qt_s, kt_s, gl_s, o_s):
        del ycat_in

        @pl.when(pl.program_id(1) == 0)
        def _():
            state[...] = jnp.zeros_like(state)

        _hgrn_prepare(q_ref, f_ref, lb_ref, qh_s, k_s, b_s, qt_s, kt_s, gl_s)
        row = lax.broadcasted_iota(jnp.int32, (SUBLANES, HG_HEAD), 0)

        def chunk(c, carry):
            rows = pl.ds(pl.multiple_of(c * HG_CHUNK, HG_CHUNK), HG_CHUNK)
            st = state[...]
            st_ref[0, c] = st
            vv = i_ref[rows, :]
            o = _groups(_dot(qt_s[rows, :], st, _NT))
            _, a_rep, _, starts = _hgrn_intra(qh_s[rows, :], k_s[rows, :], b_s[rows, :], None, None)
            for s in range(HG_CHUNK):
                g0 = s // SUBLANES
                for g in range(g0, HG_GROUPS):
                    at = starts[s] + (g - g0) * SUBLANES
                    piece = a_rep[at:at + SUBLANES, :] * vv[s:s + 1, :]
                    o[g] = o[g] + (jnp.where(row >= s - g0 * SUBLANES, piece, 0.0) if g == g0 else piece)
            o_s[rows, :] = jnp.concatenate(o, axis=0)
            state[...] = st * gl_s[rows, :][0:1, :] + _dot(vv, kt_s[rows, :], _TN)
            return carry

        lax.fori_loop(0, ncb, chunk, 0, unroll=2)
        o = o_s[...]
        o_ref[...] = o
        r = lax.rsqrt(jnp.mean(o * o, axis=-1, keepdims=True) + EPS)
        gv = g_ref[...]
        y_ref[...] = (o * r * hn_ref[...] * (gv * _sigmoid(gv))).astype(y_ref.dtype)

    def zcol(kind):
        return pl.BlockSpec((tb, HG_HEAD), lambda h, i: (i, off + kind * heads + h))

    scratch = [pltpu.VMEM((HG_HEAD, HG_HEAD), F32)] + [pltpu.VMEM((tb, HG_HEAD), F32)] * 7
    return pl.pallas_call(
        body, grid=(heads, t // tb),
        in_specs=[zcol(0), zcol(1), zcol(2), zcol(3), pl.BlockSpec((1, HG_HEAD), lambda h, i: (0, h)),
                  pl.BlockSpec((1, HG_HEAD), lambda h, i: (0, 0)), pl.BlockSpec(memory_space=pl.ANY)],
        out_specs=[pl.BlockSpec((tb, HG_HEAD), lambda h, i: (i, heads + h)),
                   pl.BlockSpec((tb, HG_HEAD), lambda h, i: (i, h)),
                   pl.BlockSpec((1, ncb, HG_HEAD, HG_HEAD), lambda h, i: (h, i, 0, 0))],
        out_shape=[jax.ShapeDtypeStruct(ycat.shape, ycat.dtype), jax.ShapeDtypeStruct((t, mix_b), F32),
                   jax.ShapeDtypeStruct((heads, t // HG_CHUNK, HG_HEAD, HG_HEAD), F32)],
        scratch_shapes=scratch, input_output_aliases={6: 0},
        compiler_params=_cp("parallel", "arbitrary"), name=name)(z, z, z, z, lb, hg_norm, ycat)


def _hgrn_bwd(z, lb, hg_norm, o_raw, states, dycat, *, name, tb=1024):
    t = z.shape[0]
    mix_b = lb.shape[1]
    heads = mix_b // HG_HEAD
    off = (z.shape[1] - 4 * mix_b) // HG_HEAD
    tb = min(tb, t)
    ncb = tb // HG_CHUNK
    nt = t // tb

    def body(q_ref, f_ref, i_ref, g_ref, lb_ref, hn_ref, o_ref, st_ref, dy_ref,
             dq_ref, dfl_ref, di_ref, dg_ref, dlb_ref, dhn_ref,
             dstate, qh_s, k_s, b_s, qt_s, kt_s, gl_s, do_s, dqh_s, dk_s, db_s):
        first = pl.program_id(1) == 0

        @pl.when(first)
        def _():
            dstate[...] = jnp.zeros_like(dstate)
            dlb_ref[...] = jnp.zeros_like(dlb_ref)

        @pl.when(first & (pl.program_id(0) == 0))
        def _():
            dhn_ref[...] = jnp.zeros_like(dhn_ref)

        sg, f = _hgrn_prepare(q_ref, f_ref, lb_ref, qh_s, k_s, b_s, qt_s, kt_s, gl_s)
        o = o_ref[...]
        r = lax.rsqrt(jnp.mean(o * o, axis=-1, keepdims=True) + EPS)
        oh = o * r
        gv = g_ref[...]
        sgg = _sigmoid(gv)
        dy = dy_ref[...].astype(F32)
        hn = hn_ref[...]
        dg_ref[...] = (dy * oh * hn * (sgg * (1.0 + gv * (1.0 - sgg)))).astype(dg_ref.dtype)
        don = dy * (gv * sgg)
        dhn_ref[...] += jnp.sum(don * oh, axis=0, keepdims=True)
        doh = don * hn
        do_s[...] = r * (doh - oh * jnp.mean(doh * oh, axis=-1, keepdims=True))
        row = lax.broadcasted_iota(jnp.int32, (SUBLANES, HG_HEAD), 0)

        def chunk(ci, carry):
            c = ncb - 1 - ci
            rows = pl.ds(pl.multiple_of(c * HG_CHUNK, HG_CHUNK), HG_CHUNK)
            st_prev = st_ref[0, c]
            dst = dstate[...]
            qh, kk, bq, vv = qh_s[rows, :], k_s[rows, :], b_s[rows, :], i_ref[rows, :]
            qt, kt, doo = qt_s[rows, :], kt_s[rows, :], do_s[rows, :]
            gl = gl_s[rows, :][0:1, :]
            es, a_rep, d_rep, starts = _hgrn_intra(qh, kk, bq, doo, vv)
            dqh = _groups(jnp.exp(bq) * _dot(doo, st_prev, _NN))
            dk = _groups(jnp.exp(bq[HG_CHUNK - 1:HG_CHUNK, :] - bq) * _dot(vv, dst, _NN))
            dv = _groups(_dot(kt, dst, _NT))
            qh_g, do_g = _groups(qh), _groups(doo)
            for s in range(HG_CHUNK):
                g0 = s // SUBLANES
                local = s - g0 * SUBLANES
                dk_acc = dv_acc = None
                for g in range(g0, HG_GROUPS):
                    at = (g - g0) * SUBLANES
                    wgt = d_rep[starts[s] + at:starts[s] + at + SUBLANES, :] * es[s][at:at + SUBLANES, :]
                    avo = a_rep[starts[s] + at:starts[s] + at + SUBLANES, :] * do_g[g]
                    if g == g0:
                        wgt = jnp.where(row >= local, wgt, 0.0)
                        avo = jnp.where(row >= local, avo, 0.0)
                    dqh[g] = dqh[g] + wgt * kk[s:s + 1, :]
                    dk_acc = wgt * qh_g[g] if dk_acc is None else dk_acc + wgt * qh_g[g]
                    dv_acc = avo if dv_acc is None else dv_acc + avo
                dk[g0] = dk[g0] + jnp.where(row == local, jnp.sum(dk_acc, axis=0, keepdims=True), 0.0)
                dv[g0] = dv[g0] + jnp.where(row == local, jnp.sum(dv_acc, axis=0, keepdims=True), 0.0)
            dqh, dk, dv = [jnp.concatenate(p, axis=0) for p in (dqh, dk, dv)]
            row_c = lax.broadcasted_iota(jnp.int32, (HG_CHUNK, HG_HEAD), 0)
            st_next = st_prev * gl + _dot(vv, kt, _TN)
            db = qh * dqh - kk * dk
            db = db + jnp.where(row_c == HG_CHUNK - 1, jnp.sum(st_next * dst, axis=0, keepdims=True), 0.0)
            dstate[...] = dst * gl + _dot(doo, qt, _TN)
            dqh_s[rows, :] = dqh
            dk_s[rows, :] = dk
            db_s[rows, :] = db
            di_ref[rows, :] = dv.astype(di_ref.dtype)
            return carry

        lax.fori_loop(0, ncb, chunk, 0, unroll=2)
        tri = _chunk_tri(False)
        lb_v = lb_ref[...]
        qv = q_ref[...]
        sgq = _sigmoid(qv)
        dq_ref[...] = (dqh_s[...] * (HG_HEAD ** -0.5) * (sgq * (1.0 + qv * (1.0 - sgq)))).astype(dq_ref.dtype)
        dlb = jnp.zeros((1, HG_HEAD), F32)
        for rr in range(tb // LANES):
            rws = slice(rr * LANES, (rr + 1) * LANES)
            dlogf = _dot_f32(tri, db_s[rws, :], _NN)
            df = dlogf / f[rws, :] - dk_s[rws, :]
            sgr = sg[rws, :]
            dfl_ref[rws, :] = (df * (1.0 - lb_v) * sgr * (1.0 - sgr)).astype(dfl_ref.dtype)
            dlb = dlb + jnp.sum(df * (1.0 - sgr), axis=0, keepdims=True)
        dlb_ref[...] += dlb

    def zcol(kind):
        return pl.BlockSpec((tb, HG_HEAD), lambda h, i: (nt - 1 - i, off + kind * heads + h))

    hcol = pl.BlockSpec((tb, HG_HEAD), lambda h, i: (nt - 1 - i, h))
    scratch = [pltpu.VMEM((HG_HEAD, HG_HEAD), F32)] + [pltpu.VMEM((tb, HG_HEAD), F32)] * 10
    out = jax.ShapeDtypeStruct((t, mix_b), MXU_DTYPE)
    return pl.pallas_call(
        body, grid=(heads, nt),
        in_specs=[zcol(0), zcol(1), zcol(2), zcol(3), pl.BlockSpec((1, HG_HEAD), lambda h, i: (0, h)),
                  pl.BlockSpec((1, HG_HEAD), lambda h, i: (0, 0)), hcol,
                  pl.BlockSpec((1, ncb, HG_HEAD, HG_HEAD), lambda h, i: (h, nt - 1 - i, 0, 0)),
                  pl.BlockSpec((tb, HG_HEAD), lambda h, i: (nt - 1 - i, heads + h))],
        out_specs=[hcol, hcol, hcol, hcol, pl.BlockSpec((1, HG_HEAD), lambda h, i: (0, h)),
                   pl.BlockSpec((1, HG_HEAD), lambda h, i: (0, 0))],
        out_shape=[out, out, out, out, jax.ShapeDtypeStruct((1, mix_b), F32),
                   jax.ShapeDtypeStruct((1, HG_HEAD), F32)],
        scratch_shapes=scratch, compiler_params=_cp("arbitrary", "arbitrary"),
        name=name)(z, z, z, z, lb, hg_norm, o_raw, states, dycat)


def _lb_fwd(lb_table, layer, *, name):
    rows, width = lb_table.shape

    def body(t_ref, o_ref):
        tv = t_ref[...]
        e = jnp.exp(tv - jnp.max(tv, axis=0, keepdims=True))
        sm = e / jnp.sum(e, axis=0, keepdims=True)
        o_ref[...] = jnp.sum(sm[1:layer + 2, :], axis=0, keepdims=True)

    return pl.pallas_call(body, out_shape=jax.ShapeDtypeStruct((1, width), F32), name=name)(lb_table)


def _lb_bwd(lb_table, dlb, layer, *, name):
    rows, width = lb_table.shape

    def body(t_ref, d_ref, o_ref):
        tv = t_ref[...]
        e = jnp.exp(tv - jnp.max(tv, axis=0, keepdims=True))
        sm = e / jnp.sum(e, axis=0, keepdims=True)
        ridx = lax.broadcasted_iota(jnp.int32, sm.shape, 0)
        dsm = jnp.where((ridx >= 1) & (ridx <= layer + 1), d_ref[...], 0.0)
        o_ref[...] = sm * (dsm - jnp.sum(sm * dsm, axis=0, keepdims=True))

    return pl.pallas_call(body, out_shape=jax.ShapeDtypeStruct((rows, width), F32), name=name)(lb_table, dlb)


FOX_BLOCK = 1024
FOX_DIAGONAL_STRIPS = 4


def _fox_prep(zf, b_f, *, name, blk=256):
    t = zf.shape[0]

    def body(z_ref, b_ref, fc_ref):
        r = lax.broadcasted_iota(jnp.int32, (blk, blk), 0)
        c = lax.broadcasted_iota(jnp.int32, (blk, blk), 1)
        tri = jnp.where(c <= r, 1.0, 0.0).astype(F32)
        carry = jnp.zeros((1, LANES), F32)
        for j in range(t // blk):
            rows = slice(j * blk, (j + 1) * blk)
            ls = jax.nn.log_sigmoid(z_ref[rows, :] + b_ref[...])
            fb = _dot_f32(tri, ls, _NN) + carry
            carry = fb[blk - 1:blk, :]
            fc_ref[rows, :] = fb

    return pl.pallas_call(
        body, out_shape=jax.ShapeDtypeStruct((t, LANES), F32),
        compiler_params=pltpu.CompilerParams(vmem_limit_bytes=VMEM_LIMIT_BYTES), name=name)(zf, b_f)


def _fox_head_column(fc_ref, fk_s, head):
    lane = lax.broadcasted_iota(jnp.int32, fc_ref.shape, 1)
    fk_s[...] = jnp.sum(jnp.where(lane == head, fc_ref[...], 0.0), axis=1, keepdims=True)


def _fox_scores(k_blk, q_blk, fk_blk, diagonal):
    s = _dot(k_blk, q_blk, _NT) * (FOX_HEAD ** -0.5) - fk_blk
    if diagonal:
        key = lax.broadcasted_iota(jnp.int32, s.shape, 0)
        qry = lax.broadcasted_iota(jnp.int32, s.shape, 1)
        s = jnp.where(key <= qry, s, -jnp.inf)
    return s


def _fox_fwd(zqkv, fcol, *, name):
    t = zqkv.shape[0]
    d = zqkv.shape[1] // 3
    heads = d // FOX_HEAD
    blk = min(FOX_BLOCK, t)
    nq = t // blk

    def body(q_ref, k_ref, v_ref, fc_ref, o_ref, lse_ref, fk_s):
        _fox_head_column(fc_ref, fk_s, pl.program_id(0))

        def q_block(qi, carry):
            qrows = pl.ds(pl.multiple_of(qi * blk, blk), blk)
            q_blk = q_ref[qrows, :]

            def update(st, krows, diagonal):
                m, l, acc = st
                s = _fox_scores(k_ref[krows, :], q_blk, fk_s[krows, :], diagonal)
                m_new = jnp.maximum(m, jnp.max(s, axis=0, keepdims=True))
                alpha = jnp.exp(m - m_new)
                p = jnp.exp(s - m_new)
                l = alpha * l + jnp.sum(p, axis=0, keepdims=True)
                acc = acc * alpha + _dot(v_ref[krows, :], p, _TN)
                return m_new, l, acc

            def k_block(kj, st):
                return update(st, pl.ds(pl.multiple_of(kj * blk, blk), blk), False)

            init = (jnp.full((1, blk), -jnp.inf, F32), jnp.zeros((1, blk), F32),
                    jnp.zeros((FOX_HEAD, blk), F32))
            m, l, acc = update(lax.fori_loop(0, qi, k_block, init), qrows, True)
            o_ref[qrows, :] = (acc / l).T.astype(o_ref.dtype)
            lse_ref[0, :, qrows] = m + jnp.log(l)
            return carry

        lax.fori_loop(0, nq, q_block, 0)

    def col(kind):
        return pl.BlockSpec((t, FOX_HEAD), lambda h: (0, kind * heads + h))

    rowvec = pl.BlockSpec((1, 1, t), lambda h: (h, 0, 0))
    return pl.pallas_call(
        body, grid=(heads,),
        in_specs=[col(0), col(1), col(2), pl.BlockSpec((t, LANES), lambda h: (0, 0))],
        out_specs=[pl.BlockSpec((t, FOX_HEAD), lambda h: (0, h)), rowvec],
        out_shape=[jax.ShapeDtypeStruct((t, d), MXU_DTYPE), jax.ShapeDtypeStruct((heads, 1, t), F32)],
        scratch_shapes=[pltpu.VMEM((t, 1), F32)],
        compiler_params=_cp("parallel"), name=name)(zqkv, zqkv, zqkv, fcol)


def _fox_bwd(zqkv, fcol, lse, o, do, *, name):
    t = zqkv.shape[0]
    d = zqkv.shape[1] // 3
    heads = d // FOX_HEAD
    blk = min(FOX_BLOCK, t)
    nq = t // blk
    strip = blk // FOX_DIAGONAL_STRIPS
    scale = FOX_HEAD ** -0.5

    def body(q_ref, k_ref, v_ref, fc_ref, lse_ref, o_ref, do_ref,
             dq_ref, dk_ref, dv_ref, rq_ref, rk_ref, dq_s, drow_s, fk_s, acc_s, rk_s):
        _fox_head_column(fc_ref, fk_s, pl.program_id(0))
        dq_s[...] = jnp.zeros_like(dq_s)
        rq_ref[...] = jnp.zeros_like(rq_ref)
        ones_f = jnp.ones((8, FOX_HEAD), F32)
        for j in range(nq):
            rows = slice(j * blk, (j + 1) * blk)
            prod = do_ref[rows, :].astype(F32) * o_ref[rows, :].astype(F32)
            drow_s[:, rows] = _dot_f32(ones_f, prod, _NT)

        def k_block(kj, carry):
            krows = pl.ds(pl.multiple_of(kj * blk, blk), blk)
            k_blk, v_blk, fk_blk = k_ref[krows, :], v_ref[krows, :], fk_s[krows, :]

            def pair(qrows, diagonal, keys=slice(0, blk)):
                q_blk, do_blk = q_ref[qrows, :], do_ref[qrows, :]
                s = _fox_scores(k_blk[keys, :], q_blk, fk_blk[keys, :], diagonal)
                p = jnp.exp(s - lse_ref[0, :, qrows])
                acc_s[1, keys, :] += _dot(p, do_blk, _NN)
                dp = _dot(v_blk[keys, :], do_blk, _NT)
                ds = (p * (dp - drow_s[0:1, qrows])).astype(MXU_DTYPE)
                acc_s[0, keys, :] += _dot(ds, q_blk, _NN)
                dq_s[qrows, :] += _dot(ds, k_blk[keys, :], _TN)
                ds_f = ds.astype(F32)
                rq_ref[0, :, qrows] += jnp.sum(ds_f, axis=0, keepdims=True)
                rk_s[keys, :] += jnp.sum(ds_f, axis=1, keepdims=True)

            def q_block(qi, carry2):
                pair(pl.ds(pl.multiple_of(qi * blk, blk), blk), False)
                return carry2

            acc_s[...] = jnp.zeros_like(acc_s)
            rk_s[...] = jnp.zeros_like(rk_s)
            for j in range(FOX_DIAGONAL_STRIPS):
                pair(pl.ds(pl.multiple_of(kj * blk + j * strip, strip), blk - j * strip), True,
                     slice(j * strip, (j + 1) * strip))
            lax.fori_loop(kj + 1, nq, q_block, 0)
            dk_ref[krows, :] = (acc_s[0] * scale).astype(dk_ref.dtype)
            dv_ref[krows, :] = acc_s[1].astype(dv_ref.dtype)
            rk_ref[0, :, krows] = jnp.broadcast_to(rk_s[...], (blk, FOX_HEAD)).T[0:1, :]
            return carry

        lax.fori_loop(0, nq, k_block, 0)
        dq_ref[...] = (dq_s[...] * scale).astype(dq_ref.dtype)

    def col(kind):
        return pl.BlockSpec((t, FOX_HEAD), lambda h: (0, kind * heads + h))

    hcol = pl.BlockSpec((t, FOX_HEAD), lambda h: (0, h))
    rowvec = pl.BlockSpec((1, 1, t), lambda h: (h, 0, 0))
    out = jax.ShapeDtypeStruct((t, d), MXU_DTYPE)
    vec = jax.ShapeDtypeStruct((heads, 1, t), F32)
    return pl.pallas_call(
        body, grid=(heads,),
        in_specs=[col(0), col(1), col(2), pl.BlockSpec((t, LANES), lambda h: (0, 0)), rowvec, hcol, hcol],
        out_specs=[hcol, hcol, hcol, rowvec, rowvec],
        out_shape=[out, out, out, vec, vec],
        scratch_shapes=[pltpu.VMEM((t, FOX_HEAD), F32), pltpu.VMEM((8, t), F32), pltpu.VMEM((t, 1), F32),
                        pltpu.VMEM((2, blk, FOX_HEAD), F32), pltpu.VMEM((blk, 1), F32)],
        compiler_params=_cp("parallel"), name=name)(zqkv, zqkv, zqkv, fcol, lse, o, do)


def _fox_gate_bwd(rq, rk, zf, b_f, *, name, blk=256):
    heads, t = rq.shape

    def body(rq_ref, rk_ref, z_ref, b_ref, dfl_ref, db_ref):
        r = lax.broadcasted_iota(jnp.int32, (blk, blk), 0)
        c = lax.broadcasted_iota(jnp.int32, (blk, blk), 1)
        tri = jnp.where(r >= c, 1.0, 0.0).astype(F32)
        carry = jnp.zeros((heads, 1), F32)
        db = jnp.zeros((1, LANES), F32)
        pad = jnp.zeros((LANES - heads, blk), F32)
        for j in reversed(range(t // blk)):
            cols = slice(j * blk, (j + 1) * blk)
            df = rq_ref[:, cols] - rk_ref[:, cols]
            dls = _dot_f32(df, tri, _NN) + carry
            carry = dls[:, 0:1]
            dls_t = jnp.concatenate([dls, pad], axis=0).T
            dfl = dls_t * _sigmoid(-(z_ref[cols, :] + b_ref[...]))
            dfl_ref[cols, :] = dfl.astype(dfl_ref.dtype)
            db = db + jnp.sum(dfl, axis=0, keepdims=True)
        db_ref[...] = db

    return pl.pallas_call(
        body, out_shape=[jax.ShapeDtypeStruct((t, LANES), MXU_DTYPE), jax.ShapeDtypeStruct((1, LANES), F32)],
        compiler_params=pltpu.CompilerParams(vmem_limit_bytes=VMEM_LIMIT_BYTES), name=name)(rq, rk, zf, b_f)


def _adamw(w, m, v, parts, *, name, layer=None, prev=None, tr=256):
    lcnt, r, c = w.shape
    p = parts.shape[0]
    li = 0 if layer is None else layer
    tr = _tile(r, tr, 16)
    tc = c if tr * c <= 256 * 2048 else _tile(c, 256)
    has_prev = prev is not None

    def body(*refs):
        w_ref, m_ref, v_ref, p_ref = refs[:4]
        g_ref, d_ref, nm_ref, nv_ref = refs[-4:]
        g = p_ref[0].astype(F32)
        for j in range(1, p):
            g = g + p_ref[j].astype(F32)
        wv = w_ref[0]
        mn = ADAM_B1 * m_ref[0] + (1.0 - ADAM_B1) * g
        vn = ADAM_B2 * v_ref[0] + (1.0 - ADAM_B2) * (g * g)
        m_hat = mn / (1.0 - ADAM_B1 ** ADAM_STEP)
        v_hat = vn / (1.0 - ADAM_B2 ** ADAM_STEP)
        g_ref[0] = g
        d_ref[0] = -ADAM_LR * (m_hat / (jnp.sqrt(v_hat) + ADAM_EPS) + ADAM_WD * wv)
        nm_ref[0] = mn
        nv_ref[0] = vn

    slab = pl.BlockSpec((1, tr, tc), lambda i, j: (li, i, j))
    in_specs = [slab, slab, slab, pl.BlockSpec((p, tr, tc), lambda i, j: (0, i, j))]
    operands = [w, m, v, parts]
    aliases = {}
    if has_prev:
        in_specs += [pl.BlockSpec(memory_space=pl.ANY)] * 4
        operands += list(prev)
        aliases = {4: 0, 5: 1, 6: 2, 7: 3}
    shp = jax.ShapeDtypeStruct((lcnt, r, c), F32)
    return pl.pallas_call(
        body, grid=(r // tr, c // tc), in_specs=in_specs, out_specs=[slab] * 4, out_shape=[shp] * 4,
        input_output_aliases=aliases, compiler_params=_cp("parallel", "parallel"), name=name)(*operands)


def _my_place():
    return lax.axis_index("x"), lax.axis_index("y"), lax.axis_index("c")


def _slot(p):
    return 4 * p[0] + 2 * p[1] + p[2]


def _peer(me, mask):
    x, y, c = me
    return (1 - x if mask & 4 else x, 1 - y if mask & 2 else y, 1 - c if mask & 1 else c)


_HBM = pl.BlockSpec(memory_space=pltpu.HBM)
_SEM = pl.BlockSpec(memory_space=pltpu.SEMAPHORE)
_ANY = pl.BlockSpec(memory_space=pl.ANY)
_EFFECT = pltpu.SideEffectType.DATAFLOW_SIDE_EFFECTING


def _push_copy(src_refs, land_refs, send_sems, recv_sems, a, mask, me, per_peer, outgoing):
    peer = _peer(me, mask)
    src = src_refs[a].at[_slot(peer)] if per_peer else src_refs[a]
    dst = land_refs[a].at[_slot(me) if outgoing else _slot(peer)]
    k = a * (N_DEV - 1) + mask - 1
    return pltpu.make_async_remote_copy(
        src_ref=src, dst_ref=dst, send_sem=send_sems.at[k], recv_sem=recv_sems.at[k],
        device_id=peer, device_id_type=MESH)


ALL_PEERS = tuple(range(1, N_DEV))
CHIP_PEERS = (2, 4, 6)
FIRST_HOP = (1,) + CHIP_PEERS


def _push_start(srcs, dep, *, per_peer, name, masks=ALL_PEERS):
    n = len(srcs)
    mine = _slot(_my_place())
    lands = []
    for s in srcs:
        own = lax.dynamic_index_in_dim(s, mine, 0, keepdims=True) if per_peer else s[None]
        shape = s.shape if per_peer else (N_DEV,) + s.shape
        lands.append(lax.dynamic_update_slice_in_dim(lax.empty(shape, s.dtype), own, mine, 0))
    has_dep = dep is not None

    def body(*refs):
        src_refs, land_refs = refs[:n], refs[n:2 * n]
        send_sems, recv_sems = refs[2 * n + has_dep], refs[2 * n + has_dep + 1]
        token = refs[-1]
        me = _my_place()
        for a in range(n):
            for mask in masks:
                _push_copy(src_refs, land_refs, send_sems, recv_sems, a, mask, me, per_peer, True).start()
        token[...] = jnp.zeros_like(token)

    hbm_in = [pltpu.with_memory_space_constraint(v, pltpu.HBM) for v in list(srcs) + lands]
    out = pl.pallas_call(
        body, name=name,
        out_shape=(pltpu.SemaphoreType.DMA((n * (N_DEV - 1),)), pltpu.SemaphoreType.DMA((n * (N_DEV - 1),)),
                   *[pltpu.HBM(v.shape, v.dtype) for v in hbm_in], jax.ShapeDtypeStruct((8, LANES), F32)),
        in_specs=[_HBM] * (2 * n) + ([_ANY] if has_dep else []),
        out_specs=(_SEM, _SEM, *[_HBM] * (2 * n), pl.BlockSpec(memory_space=pltpu.VMEM)),
        input_output_aliases={i: 2 + i for i in range(2 * n)},
        compiler_params=pltpu.CompilerParams(has_side_effects=_EFFECT),
    )(*hbm_in, *([dep] if has_dep else []))
    return (n, per_peer, masks, out[:-1]), out[-1]


def _push_wait(handle, after, *, name):
    n, per_peer, masks, (send_sems, recv_sems, *bufs) = handle

    def body(*refs):
        src_refs, land_refs = refs[:n], refs[n:2 * n]
        send_sems, recv_sems = refs[2 * n], refs[2 * n + 1]
        me = _my_place()
        for a in range(n):
            for mask in masks:
                cp = _push_copy(src_refs, land_refs, send_sems, recv_sems, a, mask, me, per_peer, False)
                cp.wait_send()
                cp.wait_recv()

    out = pl.pallas_call(
        body, name=name, out_shape=tuple(pltpu.HBM(v.shape, v.dtype) for v in bufs),
        in_specs=[_HBM] * (2 * n) + [_SEM, _SEM, _ANY], out_specs=tuple([_HBM] * (2 * n)),
        input_output_aliases={i: i for i in range(2 * n)},
        compiler_params=pltpu.CompilerParams(has_side_effects=_EFFECT),
    )(*bufs, send_sems, recv_sems, after)
    return list(out[n:])


def _relay_copy(land_refs, send_sems, recv_sems, a, j, me, outgoing):
    sibling = _peer(me, 1)
    out_slot = _slot(_peer(me, CHIP_PEERS[j]))
    in_slot = _slot(_peer(sibling, CHIP_PEERS[j]))
    k = a * len(CHIP_PEERS) + j
    return pltpu.make_async_remote_copy(
        src_ref=land_refs[a].at[out_slot], dst_ref=land_refs[a].at[out_slot if outgoing else in_slot],
        send_sem=send_sems.at[k], recv_sem=recv_sems.at[k], device_id=sibling, device_id_type=MESH)


def _relay_start(lands, *, name):
    n = len(lands)

    def body(*refs):
        land_refs, send_sems, recv_sems, token = refs[:n], refs[n], refs[n + 1], refs[-1]
        me = _my_place()
        for a in range(n):
            for j in range(len(CHIP_PEERS)):
                _relay_copy(land_refs, send_sems, recv_sems, a, j, me, True).start()
        token[...] = jnp.zeros_like(token)

    hbm_in = [pltpu.with_memory_space_constraint(v, pltpu.HBM) for v in lands]
    n_sem = n * len(CHIP_PEERS)
    out = pl.pallas_call(
        body, name=name,
        out_shape=(pltpu.SemaphoreType.DMA((n_sem,)), pltpu.SemaphoreType.DMA((n_sem,)),
                   *[pltpu.HBM(v.shape, v.dtype) for v in hbm_in], jax.ShapeDtypeStruct((8, LANES), F32)),
        in_specs=[_HBM] * n, out_specs=(_SEM, _SEM, *[_HBM] * n, pl.BlockSpec(memory_space=pltpu.VMEM)),
        input_output_aliases={i: 2 + i for i in range(n)},
        compiler_params=pltpu.CompilerParams(has_side_effects=_EFFECT),
    )(*hbm_in)
    return (n, out[:-1]), out[-1]


def _relay_wait(handle, after, *, name):
    n, (send_sems, recv_sems, *bufs) = handle

    def body(*refs):
        land_refs, send_sems, recv_sems = refs[:n], refs[n], refs[n + 1]
        me = _my_place()
        for a in range(n):
            for j in range(len(CHIP_PEERS)):
                cp = _relay_copy(land_refs, send_sems, recv_sems, a, j, me, False)
                cp.wait_send()
                cp.wait_recv()

    out = pl.pallas_call(
        body, name=name, out_shape=tuple(pltpu.HBM(v.shape, v.dtype) for v in bufs),
        in_specs=[_HBM] * n + [_SEM, _SEM, _ANY], out_specs=tuple([_HBM] * n),
        input_output_aliases={i: i for i in range(n)},
        compiler_params=pltpu.CompilerParams(has_side_effects=_EFFECT),
    )(*bufs, send_sems, recv_sems, after)
    return list(out)


def _all_reduce_rows(v, *, name):
    r, c = v.shape

    def body(v_ref, o_ref, buf, send_sems, recv_sems):
        me = _my_place()
        mine = _slot(me)
        sends = []
        for mask in range(1, N_DEV):
            peer = _peer(me, mask)
            sends.append(pltpu.make_async_remote_copy(
                src_ref=v_ref, dst_ref=buf.at[mine], send_sem=send_sems.at[mask - 1],
                recv_sem=recv_sems.at[mask - 1], device_id=peer, device_id_type=MESH))
        for cp in sends:
            cp.start()
        buf[mine] = v_ref[...]
        for mask in range(1, N_DEV):
            peer = _peer(me, mask)
            pltpu.make_async_remote_copy(
                src_ref=v_ref, dst_ref=buf.at[_slot(peer)], send_sem=send_sems.at[mask - 1],
                recv_sem=recv_sems.at[mask - 1], device_id=peer, device_id_type=MESH).wait_recv()
        for cp in sends:
            cp.wait_send()
        total = buf[0]
        for j in range(1, N_DEV):
            total = total + buf[j]
        o_ref[...] = total

    vm = pl.BlockSpec(memory_space=pltpu.VMEM)
    return pl.pallas_call(
        body, in_specs=[vm], out_specs=vm, out_shape=jax.ShapeDtypeStruct((r, c), F32),
        scratch_shapes=[pltpu.VMEM((N_DEV, r, c), F32), pltpu.SemaphoreType.DMA((7,)),
                        pltpu.SemaphoreType.DMA((7,))],
        name=name)(v)


def _xa_fwd(x, memn, g_x, wq, wkv, wo, tag):
    hx = _rms_fwd(x, g_x, name=f"xa{tag}_norm")
    q = _mm2(hx, wq, "nn", MXU_DTYPE, name=f"xa{tag}_q", tm=2048)
    kv = _mm2(memn, wkv, "nn", MXU_DTYPE, name=f"xa{tag}_kv", b_split=True)
    o = _xattn_fwd(q, kv, name=f"xa{tag}_attn")
    return _mm2(o, wo, "nn", F32, name=f"xa{tag}_out", add=x), (hx, memn, q, kv, o)


def _xa_bwd(x, mem, g_x, g_m, wq, wkv, wo, saved, dxo, dxo_lo, tag, put):
    hx, memn, q, kv, o = saved
    do = _mm2(dxo_lo, wo, "nt", MXU_DTYPE, name=f"xa{tag}_do")
    dwo = _mm2(o, dxo_lo, "tn", MXU_DTYPE, name=f"xa{tag}_dwo")
    dq, dkv = _xattn_bwd(q, kv, do, name=f"xa{tag}_attn_bwd")
    dwq = _mm2(hx, dq, "tn", MXU_DTYPE, name=f"xa{tag}_dwq")
    dwkv = _mm2(memn, dkv, "tn", MXU_DTYPE, name=f"xa{tag}_dwkv", o_split=True, tn=wkv.shape[2])
    tok = put((dwq, dwkv, dwo))
    dhx = _mm2(dq, wq, "nt", F32, name=f"xa{tag}_dh", dep=tok)
    dx, dx_lo, dgx = _rms_bwd(x, g_x, dhx, dxo, name=f"xa{tag}_norm_bwd")
    dmemn = _mm2(dkv, wkv, "nt", F32, name=f"xa{tag}_dmem", b_split=True)
    _, _, dgm = _rms_bwd(mem, g_m, dmemn, None, name=f"xa{tag}_mem_norm_bwd")
    return dx, dx_lo, dgx, dgm


def _ffn_fwd(x, g, wg, wu, wd, tag):
    h = _rms_fwd(x, g, name=f"ffn{tag}_norm")
    gate, up, act = _ffn_up(h, wg, wu, name=f"ffn{tag}_up")
    return _ffn_down(act, wd, x, name=f"ffn{tag}_down"), (h, gate, up, act)


def _ffn_bwd(x, g, wg, wu, wd, saved, dxo, dxo_lo, tag, put):
    h, gate, up, act = saved
    dwd = _mm(act, dxo_lo[None], "tn", MXU_DTYPE, name=f"ffn{tag}_dwd")
    dgate, dup = _ffn_dact(dxo_lo, wd, gate, up, name=f"ffn{tag}_dact")
    dwg = _mm(dgate, h[None], "tn", MXU_DTYPE, name=f"ffn{tag}_dwg")
    dwu = _mm(dup, h[None], "tn", MXU_DTYPE, name=f"ffn{tag}_dwu")
    tok = put((dwg, dwu, dwd))
    dh = _ffn_dh(dgate, dup, wg, wu, tok, name=f"ffn{tag}_dh")
    dx, dx_lo, dg = _rms_bwd(x, g, dh, dxo, name=f"ffn{tag}_norm_bwd")
    return dx, dx_lo, dg


def _even_fwd(x, h, lb, w_in, w_pool, pool_scale, hg_norm, w_out):
    z = _mm2(h, w_in, "nn", F32, name="ev_in", b_split=True)
    ycat = _pool_fwd(z, w_pool, pool_scale, name="ev_pool")
    ycat, o_raw, states = _hgrn_fwd(z, lb, hg_norm, ycat, name="ev_hgrn")
    return _mm2(ycat, w_out, "nn", F32, name="ev_out", add=x), (h, z, ycat, o_raw, states)


def _even_bwd(x, g, lb, w_in, w_pool, pool_scale, hg_norm, w_out, saved, dxo, dxo_lo, put):
    h, z, ycat, o_raw, states = saved
    dycat = _mm2(dxo_lo, w_out, "nt", MXU_DTYPE, name="ev_dy")
    dw_out = _mm2(ycat, dxo_lo, "tn", MXU_DTYPE, name="ev_dw_out")
    du, dw_pool, dscale = _pool_bwd(z, w_pool, pool_scale, dycat, name="ev_pool_bwd")
    dq, dfl, di, dg, dlb, dhn = _hgrn_bwd(z, lb, hg_norm, o_raw, states, dycat, name="ev_hgrn_bwd")
    dz = jnp.concatenate([du, dq, dfl, di, dg], axis=1)
    dw_in = _mm2(h, dz, "tn", MXU_DTYPE, name="ev_dw_in", o_split=True, tn=w_in.shape[2])
    tok = put((dw_in, dw_pool, dw_out))
    dh = _mm_nt_split(dz, w_in, tok, name="ev_dh")
    dx, dx_lo, dgn = _rms_bwd(x, g, dh, dxo, name="ev_norm_bwd")
    return dx, dx_lo, dict(ev_norm=dgn, ev_pool_scale=dscale, ev_hg_norm=dhn, lb=dlb)


def _odd_fwd(x, g, w_qkv, w_f, b_f, w_out):
    n_qkv = 3 * x.shape[1]
    h = _rms_fwd(x, g, name="od_norm")
    zqkv = _mm2(h, w_qkv, "nt", MXU_DTYPE, name="od_qkv", n_b=n_qkv)
    zf = _mm2(h, w_f, "nt", F32, name="od_gate")
    fcol = _fox_prep(zf, b_f, name="od_fox_prep")
    o, lse = _fox_fwd(zqkv, fcol, name="od_fox")
    return _mm2(o, w_out, "nn", F32, name="od_out", add=x), (h, zqkv, zf, fcol, o, lse)


def _odd_bwd(x, g, w_qkv, w_f, b_f, w_out, saved, dxo, dxo_lo, put):
    h, zqkv, zf, fcol, o, lse = saved
    do = _mm2(dxo_lo, w_out, "nt", MXU_DTYPE, name="od_do")
    dw_out = _mm2(o, dxo_lo, "tn", MXU_DTYPE, name="od_dw_out")
    dq, dk, dv, rq, rk = _fox_bwd(zqkv, fcol, lse, o, do, name="od_fox_bwd")
    dfl, db_f = _fox_gate_bwd(rq[:, 0, :], rk[:, 0, :], zf, b_f, name="od_fox_gate_bwd")
    dz = jnp.concatenate([dq, dk, dv], axis=1)
    dw_qkv = _mm2(dz, h, "tn", MXU_DTYPE, name="od_dw_qkv")
    dw_f = _mm2(dfl, h, "tn", MXU_DTYPE, name="od_dw_gate")
    tok = put((dw_qkv, dw_f, dw_out))
    dh = _mm2(dz, w_qkv, "nn", F32, name="od_dh_qkv", dep=tok)
    dh = _mm2(dfl, w_f, "nn", F32, name="od_dh_gate", add=dh)
    dx, dx_lo, dgn = _rms_bwd(x, g, dh, dxo, name="od_norm_bwd")
    return dx, dx_lo, dict(od_norm=dgn, od_b_f=db_f)


def _local_step(x, mem, target, sp, get_w, put_dw):
    b_f = jnp.pad(sp["od_b_f"], ((0, 0), (0, LANES - sp["od_b_f"].shape[1])))
    lb = _lb_fwd(sp["lb_table"], 0, name="lb_fwd")
    fin = sp["final_norm"].reshape(1, -1)
    xn, xm, fn = sp["xa_norm"], sp["xa_mem_norm"], sp["ffn_norm"]
    h0 = _rms_fwd(x, sp["ev_norm"], name="ev_norm")
    memn = [_rms_fwd(mem, xm[l:l + 1], name=f"xa{l}_mem_norm") for l in range(xm.shape[0])]
    w_ev = get_w("ev", h0)
    x1, s_ev = _even_fwd(x, h0, lb, w_ev[0], w_ev[1], sp["ev_pool_scale"], sp["ev_hg_norm"], w_ev[2])
    w_xa0 = get_w("xa0", x1)
    x2, s_xa0 = _xa_fwd(x1, memn[0], xn[0:1], *w_xa0, 0)
    w_ff0 = get_w("ffn0", x2)
    x3, s_ff0 = _ffn_fwd(x2, fn[0:1], *w_ff0, 0)
    w_qkv, w_f, w_od_out, od_norm = get_w("od", x3)
    x4, s_od = _odd_fwd(x3, od_norm, w_qkv, w_f, b_f, w_od_out)
    w_xa1 = get_w("xa1", x4)
    x5, s_xa1 = _xa_fwd(x4, memn[1], xn[1:2], *w_xa1, 1)
    w_ff1 = get_w("ffn1", x5)
    x6, s_ff1 = _ffn_fwd(x5, fn[1:2], *w_ff1, 1)
    loss, dx, dx_lo, d_fin = _loss_head(x6, fin, target, name="loss_head")
    put = lambda grp: functools.partial(put_dw, grp)
    dx, dx_lo, d_ffn1 = _ffn_bwd(x5, fn[1:2], *w_ff1, s_ff1, dx, dx_lo, 1, put("ffn1"))
    dx, dx_lo, d_xa1, d_xm1 = _xa_bwd(x4, mem, xn[1:2], xm[1:2], *w_xa1, s_xa1, dx, dx_lo, 1, put("xa1"))
    dx, dx_lo, d_od = _odd_bwd(x3, od_norm, w_qkv, w_f, b_f, w_od_out, s_od, dx, dx_lo, put("od"))
    dx, dx_lo, d_ffn0 = _ffn_bwd(x2, fn[0:1], *w_ff0, s_ff0, dx, dx_lo, 0, put("ffn0"))
    dx, dx_lo, d_xa0, d_xm0 = _xa_bwd(x1, mem, xn[0:1], xm[0:1], *w_xa0, s_xa0, dx, dx_lo, 0, put("xa0"))
    dx, _, d_ev = _even_bwd(x, sp["ev_norm"], lb, w_ev[0], w_ev[1], sp["ev_pool_scale"], sp["ev_hg_norm"],
                            w_ev[2], s_ev, dx, dx_lo, put("ev"))
    small = dict(
        lb_table=_lb_bwd(sp["lb_table"], d_ev["lb"], 0, name="lb_bwd"),
        ev_norm=d_ev["ev_norm"], ev_pool_scale=d_ev["ev_pool_scale"], ev_hg_norm=d_ev["ev_hg_norm"],
        od_norm=d_od["od_norm"], od_b_f=d_od["od_b_f"][:, :sp["od_b_f"].shape[1]],
        xa_norm=jnp.concatenate([d_xa0, d_xa1], axis=0), xa_mem_norm=jnp.concatenate([d_xm0, d_xm1], axis=0),
        ffn_norm=jnp.concatenate([d_ffn0, d_ffn1], axis=0), final_norm=d_fin.reshape(-1))
    return loss, dx, small


_SMALL = ("lb_table", "ev_norm", "ev_pool_scale", "ev_hg_norm", "od_norm", "od_b_f", "xa_norm", "xa_mem_norm",
          "ffn_norm", "final_norm")
_WEIGHTS = ("lb_table", "ev_norm", "ev_w_in", "ev_w_pool", "ev_pool_scale", "ev_hg_norm", "ev_w_out", "od_norm",
            "od_w_in", "od_b_f", "od_w_out", "xa_norm", "xa_mem_norm", "xa_wq", "xa_wkv", "xa_wo", "ffn_norm",
            "ffn_w_gate", "ffn_w_up", "ffn_w_down", "final_norm")


def _lo(a):
    return a.astype(MXU_DTYPE)


def _rows(v):
    flat = v.reshape(-1)
    return jnp.pad(flat, (0, (-flat.shape[0]) % LANES)).reshape(-1, LANES)


def kernel(x, mem, lb_table, ev_norm, ev_w_in, ev_w_pool, ev_pool_scale, ev_hg_norm, ev_w_out, od_norm, od_w_in, od_b_f, od_w_out, xa_norm, xa_mem_norm, xa_wq, xa_wkv, xa_wo, ffn_norm, ffn_w_gate, ffn_w_up, ffn_w_down, final_norm, loss_target, m_lb_table, m_ev_norm, m_ev_w_in, m_ev_w_pool, m_ev_pool_scale, m_ev_hg_norm, m_ev_w_out, m_od_norm, m_od_w_in, m_od_b_f, m_od_w_out, m_xa_norm, m_xa_mem_norm, m_xa_wq, m_xa_wkv, m_xa_wo, m_ffn_norm, m_ffn_w_gate, m_ffn_w_up, m_ffn_w_down, m_final_norm, v_lb_table, v_ev_norm, v_ev_w_in, v_ev_w_pool, v_ev_pool_scale, v_ev_hg_norm, v_ev_w_out, v_od_norm, v_od_w_in, v_od_b_f, v_od_w_out, v_xa_norm, v_xa_mem_norm, v_xa_wq, v_xa_wkv, v_xa_wo, v_ffn_norm, v_ffn_w_gate, v_ffn_w_up, v_ffn_w_down, v_final_norm):
    arg = dict(locals())
    d = x.shape[-1]
    layers = xa_wq.shape[0]
    me = _slot(_my_place())

    n_gate = od_b_f.shape[1]
    turned = {k: jnp.swapaxes(arg[k], 1, 2) for k in ("od_w_in", "ffn_w_gate", "ffn_w_up")}
    raw = dict(ev=[ev_w_in[0], ev_w_pool[0], ev_w_out[0]], od=[turned["od_w_in"][0], od_w_out[0], od_norm])
    for l in range(layers):
        raw[f"xa{l}"] = [xa_wq[l], xa_wkv[l], xa_wo[l]]
        raw[f"ffn{l}"] = [turned["ffn_w_gate"][l], turned["ffn_w_up"][l], ffn_w_down[l]]
    order = ("ev", "xa0", "ffn0", "od", "xa1", "ffn1")
    gathers, relays, tok = {}, {}, None
    for grp in order:
        srcs = [w if tok is None else w + tok[0, 0] for w in raw[grp]]
        srcs = [w if grp == "od" and j == 2 else _lo(w) for j, w in enumerate(srcs)]
        gathers[grp], tok = _push_start(srcs, None, per_peer=False, masks=FIRST_HOP, name=f"gather_{grp}_start")
    last_start = tok

    def second_hop(grp, after):
        lands = _push_wait(gathers[grp], after, name=f"gather_{grp}_wait")
        relays[grp], token = _relay_start(lands, name=f"gather_{grp}_relay")
        return token

    def get_w(grp, after):
        i = order.index(grp)
        if i == 0:
            after = last_start if after is None else last_start + after[:8, :LANES].astype(F32)
        if grp not in relays:
            after = second_hop(grp, after)
        if 1 <= i < len(order) - 1:
            after = second_hop(order[i + 1], after)
        got = _relay_wait(relays[grp], after, name=f"gather_{grp}_relay_wait")
        if grp == "ev":
            w_in, w_pool, w_out = got
            w_pool = jnp.transpose(w_pool, (1, 0, 2, 3)).reshape(w_pool.shape[1], -1, w_pool.shape[3])
            return w_in, w_pool, w_out.reshape(d, d)
        if grp == "od":
            w_in, w_out, nrm = got
            assert n_gate <= w_in.shape[1]
            w_f = jnp.pad(w_in[N_DEV - 1, w_in.shape[1] - n_gate:], ((0, LANES - n_gate), (0, 0)))
            return w_in.reshape(-1, d), w_f, w_out.reshape(d, d), nrm.reshape(1, d)
        if grp.startswith("xa"):
            return got[0].reshape(d, d), got[1], got[2].reshape(d, d)
        return tuple(got)

    def row_parts(g):
        return g.reshape(N_DEV, -1, g.shape[-1])

    scatters = {}

    def put_dw(grp, dws):
        if grp == "ev":
            dw_in, dw_pool, dw_out = dws
            gc = dw_pool.shape[1] // N_DEV
            dw_pool = _lo(jnp.transpose(dw_pool.reshape(dw_pool.shape[0], N_DEV, gc, -1), (1, 0, 2, 3)))
            parts = [dw_in, dw_pool, row_parts(dw_out)]
        elif grp == "od":
            dw_qkv, dw_f, dw_out = dws
            parts = [row_parts(jnp.concatenate([dw_qkv, dw_f[:n_gate]], axis=0)), row_parts(dw_out)]
        elif grp.startswith("xa"):
            parts = [row_parts(dws[0]), dws[1], row_parts(dws[2])]
        else:
            parts = list(dws)
        scatters[grp], token = _push_start(parts, None, per_peer=True, name=f"scatter_{grp}_start")
        return token

    sp = {k: arg[k] for k in _SMALL if k != "od_norm"}
    loss, dx, small = _local_step(x[0], mem[0], loss_target[0], sp, get_w, put_dw)

    pieces = [_rows(small[k]) for k in _SMALL]
    packed = jnp.concatenate(pieces + [_rows(loss)], axis=0)
    packed = jnp.pad(packed, ((0, (-packed.shape[0]) % 8), (0, 0)))
    total = _all_reduce_rows(packed, name="all_reduce_small")
    loss = total[sum(pc.shape[0] for pc in pieces), 0]
    small_g, at = {}, 0
    for k, pc in zip(_SMALL, pieces):
        n = small[k].size
        small_g[k] = total[at:at + pc.shape[0]].reshape(-1)[:n].reshape(small[k].shape)
        at += pc.shape[0]
    small_g["od_norm"] = lax.dynamic_slice_in_dim(small_g["od_norm"], me * od_norm.shape[1], od_norm.shape[1], axis=1)

    res = {}
    for k in _SMALL:
        w, m, v = arg[k], arg["m_" + k], arg["v_" + k]
        shp = (1, 1, w.shape[0]) if w.ndim == 1 else (1,) + w.shape
        out = _adamw(w.reshape(shp), m.reshape(shp), v.reshape(shp), small_g[k].reshape(shp), name=f"adamw_{k}")
        res[k] = [o.reshape(w.shape) for o in out]
    members = dict(ev=("ev_w_in", "ev_w_pool", "ev_w_out"), od=("od_w_in", "od_w_out"),
                   xa=("xa_wq", "xa_wkv", "xa_wo"), ffn=("ffn_w_gate", "ffn_w_up", "ffn_w_down"))
    after, stacked = dx, {}
    for grp in ("ffn1", "xa1", "od", "ffn0", "xa0", "ev"):
        got = _push_wait(scatters[grp], after, name=f"scatter_{grp}_wait")
        kind = grp.rstrip("01")
        for k, parts in zip(members[kind], got):
            w, m, v = [jnp.swapaxes(a, 1, 2) if k in turned else a for a in (arg[k], arg["m_" + k], arg["v_" + k])]
            if w.shape[0] == 1:
                shp = (1, -1, w.shape[-1])
                out = _adamw(w.reshape(shp), m.reshape(shp), v.reshape(shp), parts.reshape(N_DEV, -1, w.shape[-1]),
                             name=f"adamw_{k}")
            else:
                out = _adamw(w, m, v, parts, name=f"adamw_{k}{grp[-1]}", layer=int(grp[-1]), prev=stacked.get(k))
                stacked[k] = out
            res[k] = [jnp.swapaxes(o.reshape(w.shape), 1, 2) if k in turned else o.reshape(w.shape) for o in out]
            after = out[3][:1, :8, :LANES]

    outs = [loss, dx[None]]
    for j in range(4):
        outs += [res[k][j] for k in _WEIGHTS]
    return tuple(outs)
```

```python
import functools

import jax
import jax.numpy as jnp
from jax import lax
from jax.experimental import pallas as pl
from jax.experimental.pallas import tpu as pltpu

F32 = jnp.float32
MXU_DTYPE = jnp.bfloat16
EPS = 1e-6
N_DEV = 8
V7X_VMEM_BYTES = 64 * 1024 * 1024
VMEM_LIMIT_BYTES = V7X_VMEM_BYTES - 8 * 1024 * 1024
LANES = 128
HIGHEST = lax.Precision.HIGHEST
MESH = pl.DeviceIdType.MESH

HG_HEAD = 128
HG_CHUNK = 32
FOX_HEAD = 128
XA_HEADS = 4
POOL_GROUPS = 4

ADAM_LR = 0.001
ADAM_B1 = 0.9
ADAM_B2 = 0.999
ADAM_EPS = 1e-08
ADAM_WD = 0.01
ADAM_STEP = 10

_NN = ((1,), (0,))
_NT = ((1,), (1,))
_TN = ((0,), (0,))


def _dot(a, b, dims):
    return lax.dot_general(a.astype(MXU_DTYPE), b.astype(MXU_DTYPE), (dims, ((), ())),
                           preferred_element_type=F32)


def _dot_f32(a, b, dims):
    return lax.dot_general(a, b, (dims, ((), ())), preferred_element_type=F32, precision=HIGHEST)


def _cp(*sem):
    return pltpu.CompilerParams(dimension_semantics=sem, vmem_limit_bytes=VMEM_LIMIT_BYTES)


def _tile(n, pref, align=LANES):
    if n <= pref:
        return n
    t = (pref // align) * align
    while t >= align:
        if n % t == 0:
            return t
        t -= align
    return n


def _sigmoid(x):
    return jax.nn.sigmoid(x)


def _mm(a, b, mode, out_dtype, *, name, add=None, dep=None, reduce_b=False, b_split=False, o_split=False,
        n_b=None, tm=1024, tn=1024, tk=2048):
    ba, bb = a.shape[0], b.shape[0]
    if mode == "tn":
        kdim, m = a.shape[1], a.shape[2]
        tk = 2 * tk
    else:
        m, kdim = a.shape[1], a.shape[2]
    if b_split:
        s_cnt, b_rows, w = b.shape
        if mode == "nt":
            n = b_rows
            assert kdim == s_cnt * w
            tk = w
        else:
            n = s_cnt * w
            assert b_rows == kdim
            tn = w
        nb = ba
    else:
        n = b.shape[1] if mode == "nt" else b.shape[2]
        n = n if n_b is None else n_b
        nb = max(ba, bb)
    if not (b_split and mode != "nt"):
        tn = _tile(n, tn)
    if not (b_split and mode == "nt"):
        tk = _tile(kdim, tk)
    tm = _tile(m, tm)
    assert m % tm == 0 and n % tn == 0 and kdim % tk == 0, (name, m, n, kdim, tm, tn, tk)
    nk = kdim // tk
    if reduce_b:
        grid = (m // tm, n // tn, nb, nk)
        unpack = lambda i, j, bi, k: (bi, i, j, k)
        sem = ("parallel", "parallel", "arbitrary", "arbitrary")
        nred = nb * nk
    else:
        grid = (nb, m // tm, n // tn, nk)
        unpack = lambda bi, i, j, k: (bi, i, j, k)
        sem = ("parallel", "parallel", "parallel", "arbitrary")
        nred = nk

    def a_map(*g):
        bi, i, j, k = unpack(*g)
        ab = bi if ba > 1 else 0
        return (ab, k, i) if mode == "tn" else (ab, i, k)

    def b_map(*g):
        bi, i, j, k = unpack(*g)
        if b_split:
            return (k, j, 0) if mode == "nt" else (j, k, 0)
        bq = bi if bb > 1 else 0
        return (bq, j, k) if mode == "nt" else (bq, k, j)

    def o_map(*g):
        bi, i, j, k = unpack(*g)
        if o_split:
            return (j, i, 0)
        return (0 if reduce_b else bi, i, j)

    a_blk = (1, tk, tm) if mode == "tn" else (1, tm, tk)
    b_blk = (1, tn, tk) if mode == "nt" else (1, tk, tn)
    dims = {"nn": _NN, "nt": _NT, "tn": _TN}[mode]
    has_add = add is not None

    def body(*refs):
        a_ref, b_ref = refs[:2]
        if has_add:
            add_ref = refs[2]
        if nred == 1:
            o_ref = refs[-1]
            r = _dot(a_ref[0], b_ref[0], dims)
            if has_add:
                r = r + add_ref[0].astype(F32)
            o_ref[0] = r.astype(o_ref.dtype)
            return
        o_ref, acc_ref = refs[-2:]
        if reduce_b:
            step = pl.program_id(2) * nk + pl.program_id(3)
        else:
            step = pl.program_id(3)

        @pl.when(step == 0)
        def _():
            acc_ref[...] = _dot(a_ref[0], b_ref[0], dims)

        @pl.when(step > 0)
        def _():
            acc_ref[...] += _dot(a_ref[0], b_ref[0], dims)

        @pl.when(step == nred - 1)
        def _():
            r = acc_ref[...]
            if has_add:
                r = r + add_ref[0].astype(F32)
            o_ref[0] = r.astype(o_ref.dtype)

    in_specs = [pl.BlockSpec(a_blk, a_map), pl.BlockSpec(b_blk, b_map)]
    operands = [a, b]
    if has_add:
        in_specs.append(pl.BlockSpec((1, tm, tn), o_map))
        operands.append(add)
    if dep is not None:
        in_specs.append(pl.BlockSpec(memory_space=pl.ANY))
        operands.append(dep)
    if o_split:
        out_shape = jax.ShapeDtypeStruct((n // tn, m, tn), out_dtype)
    else:
        out_shape = jax.ShapeDtypeStruct((1 if reduce_b else nb, m, n), out_dtype)
    return pl.pallas_call(
        body, grid=grid, in_specs=in_specs, out_specs=pl.BlockSpec((1, tm, tn), o_map),
        out_shape=out_shape, scratch_shapes=[] if nred == 1 else [pltpu.VMEM((tm, tn), F32)],
        compiler_params=_cp(*sem), name=name)(*operands)


def _mm_nt_split(a, b, dep, *, name, tm=512, tn=1024):
    m, k = a.shape
    s, n, w = b.shape
    assert k == s * w and w % LANES == 0
    tm, tn = _tile(m, tm), _tile(n, tn)
    has_dep = dep is not None

    def body(*refs):
        a_ref, b_ref, o_ref = refs[0], refs[1], refs[-1]
        r = _dot(a_ref[:, 0:w], b_ref[0], _NT)
        for q in range(1, s):
            r = r + _dot(a_ref[:, q * w:(q + 1) * w], b_ref[q], _NT)
        o_ref[...] = r

    return pl.pallas_call(
        body, grid=(n // tn, m // tm),
        in_specs=[pl.BlockSpec((tm, k), lambda j, i: (i, 0)), pl.BlockSpec((s, tn, w), lambda j, i: (0, j, 0))]
        + ([pl.BlockSpec(memory_space=pl.ANY)] if has_dep else []),
        out_specs=pl.BlockSpec((tm, tn), lambda j, i: (i, j)), out_shape=jax.ShapeDtypeStruct((m, n), F32),
        compiler_params=_cp("parallel", "parallel"), name=name)(a, b, *([dep] if has_dep else []))


def _mm2(a, b, mode, out_dtype, *, name, add=None, **kw):
    b3 = b if kw.get("b_split") else b[None]
    r = _mm(a[None], b3, mode, out_dtype, name=name, add=None if add is None else add[None], **kw)
    return r if kw.get("o_split") else r[0]


def _rms_fwd(x, g, *, name, tb=512):
    t, d = x.shape
    tb = min(tb, t)

    def body(x_ref, g_ref, o_ref):
        xv = x_ref[...]
        r = lax.rsqrt(jnp.mean(xv * xv, axis=-1, keepdims=True) + EPS)
        o_ref[...] = (xv * r * g_ref[...]).astype(o_ref.dtype)

    return pl.pallas_call(
        body, grid=(t // tb,),
        in_specs=[pl.BlockSpec((tb, d), lambda i: (i, 0)), pl.BlockSpec((1, d), lambda i: (0, 0))],
        out_specs=pl.BlockSpec((tb, d), lambda i: (i, 0)),
        out_shape=jax.ShapeDtypeStruct((t, d), MXU_DTYPE), compiler_params=_cp("parallel"), name=name)(x, g)


def _rms_bwd(x, g, dh, dres, *, name, tb=512):
    t, d = x.shape
    tb = min(tb, t)
    has_res = dres is not None

    def body(*refs):
        if has_res:
            x_ref, g_ref, dh_ref, dres_ref, dx_ref, dxl_ref, dg_ref = refs
        else:
            x_ref, g_ref, dh_ref, dx_ref, dxl_ref, dg_ref = refs
        xv = x_ref[...]
        r = lax.rsqrt(jnp.mean(xv * xv, axis=-1, keepdims=True) + EPS)
        xh = xv * r
        dhv = dh_ref[...].astype(F32)

        @pl.when(pl.program_id(0) == 0)
        def _():
            dg_ref[...] = jnp.zeros_like(dg_ref)

        dg_ref[...] += jnp.sum(dhv * xh, axis=0, keepdims=True)
        dxh = dhv * g_ref[...]
        dx = r * (dxh - xh * jnp.mean(dxh * xh, axis=-1, keepdims=True))
        if has_res:
            dx = dx + dres_ref[...]
        dx_ref[...] = dx
        dxl_ref[...] = dx.astype(dxl_ref.dtype)

    row = pl.BlockSpec((tb, d), lambda i: (i, 0))
    vec = pl.BlockSpec((1, d), lambda i: (0, 0))
    operands = [x, g, dh] + ([dres] if has_res else [])
    return pl.pallas_call(
        body, grid=(t // tb,), in_specs=[row, vec, row] + ([row] if has_res else []),
        out_specs=[row, row, vec],
        out_shape=[jax.ShapeDtypeStruct((t, d), F32), jax.ShapeDtypeStruct((t, d), MXU_DTYPE),
                   jax.ShapeDtypeStruct((1, d), F32)],
        compiler_params=_cp("arbitrary"), name=name)(*operands)


def _loss_head(x, g, target, *, name, tb=512):
    t, d = x.shape
    tb = min(tb, t)

    def body(x_ref, g_ref, t_ref, loss_ref, dx_ref, dxl_ref, dg_ref):
        xv = x_ref[...]
        r = lax.rsqrt(jnp.mean(xv * xv, axis=-1, keepdims=True) + EPS)
        xh = xv * r
        gv = g_ref[...]
        err = xh * gv - t_ref[...]

        @pl.when(pl.program_id(0) == 0)
        def _():
            dg_ref[...] = jnp.zeros_like(dg_ref)
            loss_ref[...] = jnp.zeros_like(loss_ref)

        row_loss = jnp.mean(err * err, axis=-1, keepdims=True)
        loss_ref[...] += 0.5 * jnp.sum(row_loss, axis=0, keepdims=True)
        dy = err * (1.0 / d)
        dg_ref[...] += jnp.sum(dy * xh, axis=0, keepdims=True)
        dxh = dy * gv
        dx = r * (dxh - xh * jnp.mean(dxh * xh, axis=-1, keepdims=True))
        dx_ref[...] = dx
        dxl_ref[...] = dx.astype(dxl_ref.dtype)

    row = pl.BlockSpec((tb, d), lambda i: (i, 0))
    vec = pl.BlockSpec((1, d), lambda i: (0, 0))
    return pl.pallas_call(
        body, grid=(t // tb,), in_specs=[row, vec, row],
        out_specs=[pl.BlockSpec((1, 1), lambda i: (0, 0)), row, row, vec],
        out_shape=[jax.ShapeDtypeStruct((1, 1), F32), jax.ShapeDtypeStruct((t, d), F32),
                   jax.ShapeDtypeStruct((t, d), MXU_DTYPE), jax.ShapeDtypeStruct((1, d), F32)],
        compiler_params=_cp("arbitrary"), name=name)(x, g, target)


def _ffn_up(h, wg, wu, *, name, tb=1024):
    t, d = h.shape
    s, f, _ = wg.shape
    tb = min(tb, t)

    def body(h_ref, wg_ref, wu_ref, g_ref, u_ref, a_ref):
        hv = h_ref[...]
        gv = _dot(hv, wg_ref[0], _NT)
        uv = _dot(hv, wu_ref[0], _NT)
        g_ref[0] = gv.astype(g_ref.dtype)
        u_ref[0] = uv.astype(u_ref.dtype)
        a_ref[0] = (gv * _sigmoid(gv) * uv).astype(a_ref.dtype)

    wspec = pl.BlockSpec((1, f, d), lambda j, i: (j, 0, 0))
    ospec = pl.BlockSpec((1, tb, f), lambda j, i: (j, i, 0))
    return pl.pallas_call(
        body, grid=(s, t // tb),
        in_specs=[pl.BlockSpec((tb, d), lambda j, i: (i, 0)), wspec, wspec],
        out_specs=[ospec, ospec, ospec],
        out_shape=[jax.ShapeDtypeStruct((s, t, f), MXU_DTYPE)] * 3,
        compiler_params=_cp("parallel", "parallel"), name=name)(h, wg, wu)


def _ffn_dact(dy, wd, gate, up, *, name, tb=1024):
    t, d = dy.shape
    s, f, _ = wd.shape
    tb = min(tb, t)

    def body(dy_ref, wd_ref, g_ref, u_ref, dg_ref, du_ref):
        da = _dot(dy_ref[...], wd_ref[0], _NT)
        gv = g_ref[0].astype(F32)
        sg = _sigmoid(gv)
        du_ref[0] = (da * gv * sg).astype(du_ref.dtype)
        dg_ref[0] = (da * u_ref[0].astype(F32) * (sg * (1.0 + gv * (1.0 - sg)))).astype(dg_ref.dtype)

    aspec = pl.BlockSpec((1, tb, f), lambda i, j: (j, i, 0))
    return pl.pallas_call(
        body, grid=(t // tb, s),
        in_specs=[pl.BlockSpec((tb, d), lambda i, j: (i, 0)),
                  pl.BlockSpec((1, f, d), lambda i, j: (j, 0, 0)), aspec, aspec],
        out_specs=[aspec, aspec],
        out_shape=[jax.ShapeDtypeStruct((s, t, f), MXU_DTYPE), jax.ShapeDtypeStruct((s, t, f), MXU_DTYPE)],
        compiler_params=_cp("parallel", "parallel"), name=name)(dy, wd, gate, up)


def _ffn_down(act, wd, x, *, name, tm=512, tn=1024):
    s, t, f = act.shape
    d = wd.shape[2]
    tm, tn = _tile(t, tm), _tile(d, tn)

    def body(a_ref, w_ref, x_ref, o_ref):
        r = x_ref[...]
        for j in range(s):
            r = r + _dot(a_ref[j], w_ref[j], _NN)
        o_ref[...] = r

    xspec = pl.BlockSpec((tm, tn), lambda k, i: (i, k))
    return pl.pallas_call(
        body, grid=(d // tn, t // tm),
        in_specs=[pl.BlockSpec((s, tm, f), lambda k, i: (0, i, 0)), pl.BlockSpec((s, f, tn), lambda k, i: (0, 0, k)),
                  xspec],
        out_specs=xspec, out_shape=jax.ShapeDtypeStruct((t, d), F32),
        compiler_params=_cp("parallel", "parallel"), name=name)(act, wd, x)


def _ffn_dh(dgate, dup, wg, wu, dep, *, name, tm=512, tn=2048, sg=2):
    s, t, f = dgate.shape
    d = wg.shape[2]
    tm, tn = _tile(t, tm), _tile(d, tn)
    steps = s // sg
    has_dep = dep is not None

    def body(*refs):
        dg_ref, du_ref, wg_ref, wu_ref = refs[:4]
        o_ref, acc_ref = refs[-2:]
        j = pl.program_id(2)
        part = _dot(dg_ref[0], wg_ref[0], _NN) + _dot(du_ref[0], wu_ref[0], _NN)
        for q in range(1, sg):
            part = part + _dot(dg_ref[q], wg_ref[q], _NN) + _dot(du_ref[q], wu_ref[q], _NN)

        @pl.when(j == 0)
        def _():
            acc_ref[...] = part

        @pl.when(j > 0)
        def _():
            acc_ref[...] += part

        @pl.when(j == steps - 1)
        def _():
            o_ref[...] = acc_ref[...]

    aspec = pl.BlockSpec((sg, tm, f), lambda i, k, j: (j, i, 0))
    wspec = pl.BlockSpec((sg, f, tn), lambda i, k, j: (j, 0, k))
    return pl.pallas_call(
        body, grid=(t // tm, d // tn, steps),
        in_specs=[aspec, aspec, wspec, wspec] + ([pl.BlockSpec(memory_space=pl.ANY)] if has_dep else []),
        out_specs=pl.BlockSpec((tm, tn), lambda i, k, j: (i, k)),
        out_shape=jax.ShapeDtypeStruct((t, d), F32), scratch_shapes=[pltpu.VMEM((tm, tn), F32)],
        compiler_params=_cp("parallel", "parallel", "arbitrary"),
        name=name)(dgate, dup, wg, wu, *([dep] if has_dep else []))


def _xattn_fwd(q, kv, *, name, tb=512):
    t, d = q.shape
    m = kv.shape[0]
    hd = d // XA_HEADS
    tb = min(tb, t)
    scale = hd ** -0.5

    def body(q_ref, kv_ref, o_ref):
        for hh in range(XA_HEADS):
            cs = slice(hh * hd, (hh + 1) * hd)
            s = _dot(q_ref[:, cs], kv_ref[:, cs], _NT) * scale
            s = s - jnp.max(s, axis=-1, keepdims=True)
            e = jnp.exp(s)
            p = e / jnp.sum(e, axis=-1, keepdims=True)
            o_ref[:, cs] = _dot(p, kv_ref[:, d + hh * hd:d + (hh + 1) * hd], _NN).astype(o_ref.dtype)

    return pl.pallas_call(
        body, grid=(t // tb,),
        in_specs=[pl.BlockSpec((tb, d), lambda i: (i, 0)), pl.BlockSpec((m, 2 * d), lambda i: (0, 0))],
        out_specs=pl.BlockSpec((tb, d), lambda i: (i, 0)),
        out_shape=jax.ShapeDtypeStruct((t, d), MXU_DTYPE), compiler_params=_cp("parallel"), name=name)(q, kv)


def _xattn_bwd(q, kv, do, *, name, tb=512):
    t, d = q.shape
    m = kv.shape[0]
    hd = d // XA_HEADS
    tb = min(tb, t)
    scale = hd ** -0.5

    def body(q_ref, kv_ref, do_ref, dq_ref, dkv_ref):
        @pl.when(pl.program_id(0) == 0)
        def _():
            dkv_ref[...] = jnp.zeros_like(dkv_ref)

        for hh in range(XA_HEADS):
            cs = slice(hh * hd, (hh + 1) * hd)
            vs = slice(d + hh * hd, d + (hh + 1) * hd)
            qv, kk, vv, dov = q_ref[:, cs], kv_ref[:, cs], kv_ref[:, vs], do_ref[:, cs]
            s = _dot(qv, kk, _NT) * scale
            s = s - jnp.max(s, axis=-1, keepdims=True)
            e = jnp.exp(s)
            p = e / jnp.sum(e, axis=-1, keepdims=True)
            dkv_ref[:, vs] += _dot(p, dov, _TN)
            dp = _dot(dov, vv, _NT)
            ds = p * (dp - jnp.sum(p * dp, axis=-1, keepdims=True)) * scale
            dq_ref[:, cs] = _dot(ds, kk, _NN).astype(dq_ref.dtype)
            dkv_ref[:, cs] += _dot(ds, qv, _TN)

    row = pl.BlockSpec((tb, d), lambda i: (i, 0))
    full = pl.BlockSpec((m, 2 * d), lambda i: (0, 0))
    return pl.pallas_call(
        body, grid=(t // tb,), in_specs=[row, full, row], out_specs=[row, full],
        out_shape=[jax.ShapeDtypeStruct((t, d), MXU_DTYPE), jax.ShapeDtypeStruct((m, 2 * d), F32)],
        compiler_params=_cp("arbitrary"), name=name)(q, kv, do)


def _pool_window_stats(u, gi, reverse):
    t = u.shape[0]
    row = lax.broadcasted_iota(jnp.int32, u.shape, 0)
    s = u
    for j in range(POOL_GROUPS):
        sh = 1 << j
        if reverse:
            rolled = jnp.where(row < t - sh, pltpu.roll(s, t - sh, axis=0), 0.0)
        else:
            rolled = jnp.where(row >= sh, pltpu.roll(s, sh, axis=0), 0.0)
        s = jnp.where(j <= gi, s + rolled, s)
    return s, row


def _pool_fwd(z, w_pool, scale, *, name):
    t = z.shape[0]
    g_cnt, c, _ = w_pool.shape

    def body(z_ref, w_ref, s_ref, o_ref):
        gi = pl.program_id(0)
        u = z_ref[...]
        win, row = _pool_window_stats(u, gi, False)
        cnt = jnp.minimum(row + 1, lax.shift_left(jnp.int32(2), gi)).astype(F32)
        p = win / cnt - u
        o_ref[...] = (_dot(p, w_ref[0], _NN) * s_ref[...]).astype(o_ref.dtype)

    return pl.pallas_call(
        body, grid=(g_cnt,),
        in_specs=[pl.BlockSpec((t, c), lambda g: (0, g)), pl.BlockSpec((1, c, c), lambda g: (g, 0, 0)),
                  pl.BlockSpec((1, c), lambda g: (0, g))],
        out_specs=pl.BlockSpec((t, c), lambda g: (0, g)),
        out_shape=jax.ShapeDtypeStruct((t, 2 * g_cnt * c), MXU_DTYPE),
        compiler_params=_cp("parallel"), name=name)(z, w_pool, scale)


def _pool_bwd(z, w_pool, scale, dycat, *, name):
    t = z.shape[0]
    g_cnt, c, _ = w_pool.shape

    def body(z_ref, w_ref, s_ref, dy_ref, du_ref, dw_ref, ds_ref):
        gi = pl.program_id(0)
        u = z_ref[...]
        win, row = _pool_window_stats(u, gi, False)
        cnt = jnp.minimum(row + 1, lax.shift_left(jnp.int32(2), gi)).astype(F32)
        p = win / cnt - u
        y = _dot(p, w_ref[0], _NN)
        dya = dy_ref[...].astype(F32)
        ds_ref[...] = jnp.sum(dya * y, axis=0, keepdims=True)
        dy = dya * s_ref[...]
        dw_ref[0] = _dot(p, dy, _TN)
        dp = _dot(dy, w_ref[0], _NT)
        back, _ = _pool_window_stats(dp / cnt, gi, True)
        du_ref[...] = (back - dp).astype(du_ref.dtype)

    col = pl.BlockSpec((t, c), lambda g: (0, g))
    return pl.pallas_call(
        body, grid=(g_cnt,),
        in_specs=[col, pl.BlockSpec((1, c, c), lambda g: (g, 0, 0)), pl.BlockSpec((1, c), lambda g: (0, g)), col],
        out_specs=[col, pl.BlockSpec((1, c, c), lambda g: (g, 0, 0)), pl.BlockSpec((1, c), lambda g: (0, g))],
        out_shape=[jax.ShapeDtypeStruct((t, g_cnt * c), MXU_DTYPE), jax.ShapeDtypeStruct((g_cnt, c, c), F32),
                   jax.ShapeDtypeStruct((1, g_cnt * c), F32)],
        compiler_params=_cp("parallel"), name=name)(z, w_pool, scale, dycat)


def _chunk_tri(lower):
    r = lax.broadcasted_iota(jnp.int32, (LANES, LANES), 0)
    c = lax.broadcasted_iota(jnp.int32, (LANES, LANES), 1)
    same = (r // HG_CHUNK) == (c // HG_CHUNK)
    return jnp.where(same & ((c <= r) if lower else (c >= r)), 1.0, 0.0).astype(F32)


def _hgrn_prepare(q_ref, f_ref, lb_ref, qh_s, k_s, b_s, qt_s, kt_s, gl_s):
    tb = q_ref.shape[0]
    lb = lb_ref[...]
    sg = _sigmoid(f_ref[...])
    f = lb + (1.0 - lb) * sg
    logf = jnp.log(f)
    qv = q_ref[...]
    qh = qv * _sigmoid(qv) * (HG_HEAD ** -0.5)
    tri = _chunk_tri(True)
    for r in range(tb // LANES):
        rows = slice(r * LANES, (r + 1) * LANES)
        b_s[rows, :] = _dot_f32(tri, logf[rows, :], _NN)
    b = b_s[...]
    b3 = b.reshape(tb // HG_CHUNK, HG_CHUNK, HG_HEAD)
    bl = b3[:, HG_CHUNK - 1:HG_CHUNK, :]
    k = 1.0 - f
    qh_s[...] = qh
    k_s[...] = k
    qt_s[...] = qh * jnp.exp(b)
    kt_s[...] = k * jnp.exp(bl - b3).reshape(tb, HG_HEAD)
    gl_s[...] = jnp.exp(jnp.broadcast_to(bl, b3.shape)).reshape(tb, HG_HEAD)
    return sg, f


SUBLANES = 8
HG_GROUPS = HG_CHUNK // SUBLANES


def _hgrn_intra(qh, kk, bq, rows_a, rows_b):
    ones = jnp.ones((HG_HEAD, HG_HEAD), MXU_DTYPE)
    es, stack_a, stack_b, starts, at = [], [], [], [], 0
    for s in range(HG_CHUNK):
        lo = (s // SUBLANES) * SUBLANES
        e = jnp.exp(jnp.minimum(bq[lo:, :] - bq[s:s + 1, :], 0.0))
        es.append(e)
        stack_a.append(qh[lo:, :] * e * kk[s:s + 1, :])
        if rows_a is not None:
            stack_b.append(rows_a[lo:, :] * rows_b[s:s + 1, :])
        starts.append(at)
        at += HG_CHUNK - lo
    a_rep = _dot(jnp.concatenate(stack_a, axis=0), ones, _NN)
    d_rep = _dot(jnp.concatenate(stack_b, axis=0), ones, _NN) if rows_a is not None else None
    return es, a_rep, d_rep, starts


def _groups(v):
    return [v[g * SUBLANES:(g + 1) * SUBLANES, :] for g in range(HG_GROUPS)]


def _hgrn_fwd(z, lb, hg_norm, ycat, *, name, tb=1024):
    t = z.shape[0]
    mix_b = lb.shape[1]
    heads = mix_b // HG_HEAD
    off = (z.shape[1] - 4 * mix_b) // HG_HEAD
    tb = min(tb, t)
    ncb = tb // HG_CHUNK

    def body(q_ref, f_ref, i_ref, g_ref, lb_ref, hn_ref, ycat_in, y_ref, o_ref, st_ref,
             state, qh_s, k_s, b_s, qt_s, kt_s, gl_s, o_s):
        del ycat_in

        @pl.when(pl.program_id(1) == 0)
        def _():
            state[...] = jnp.zeros_like(state)

        _hgrn_prepare(q_ref, f_ref, lb_ref, qh_s, k_s, b_s, qt_s, kt_s, gl_s)
        row = lax.broadcasted_iota(jnp.int32, (SUBLANES, HG_HEAD), 0)

        def chunk(c, carry):
            rows = pl.ds(pl.multiple_of(c * HG_CHUNK, HG_CHUNK), HG_CHUNK)
            st = state[...]
            st_ref[0, c] = st
            vv = i_ref[rows, :]
            o = _groups(_dot(qt_s[rows, :], st, _NT))
            _, a_rep, _, starts = _hgrn_intra(qh_s[rows, :], k_s[rows, :], b_s[rows, :], None, None)
            for s in range(HG_CHUNK):
                g0 = s // SUBLANES
                for g in range(g0, HG_GROUPS):
                    at = starts[s] + (g - g0) * SUBLANES
                    piece = a_rep[at:at + SUBLANES, :] * vv[s:s + 1, :]
                    o[g] = o[g] + (jnp.where(row >= s - g0 * SUBLANES, piece, 0.0) if g == g0 else piece)
            o_s[rows, :] = jnp.concatenate(o, axis=0)
            state[...] = st * gl_s[rows, :][0:1, :] + _dot(vv, kt_s[rows, :], _TN)
            return carry

        lax.fori_loop(0, ncb, chunk, 0, unroll=2)
        o = o_s[...]
        o_ref[...] = o
        r = lax.rsqrt(jnp.mean(o * o, axis=-1, keepdims=True) + EPS)
        gv = g_ref[...]
        y_ref[...] = (o * r * hn_ref[...] * (gv * _sigmoid(gv))).astype(y_ref.dtype)

    def zcol(kind):
        return pl.BlockSpec((tb, HG_HEAD), lambda h, i: (i, off + kind * heads + h))

    scratch = [pltpu.VMEM((HG_HEAD, HG_HEAD), F32)] + [pltpu.VMEM((tb, HG_HEAD), F32)] * 7
    return pl.pallas_call(
        body, grid=(heads, t // tb),
        in_specs=[zcol(0), zcol(1), zcol(2), zcol(3), pl.BlockSpec((1, HG_HEAD), lambda h, i: (0, h)),
                  pl.BlockSpec((1, HG_HEAD), lambda h, i: (0, 0)), pl.BlockSpec(memory_space=pl.ANY)],
        out_specs=[pl.BlockSpec((tb, HG_HEAD), lambda h, i: (i, heads + h)),
                   pl.BlockSpec((tb, HG_HEAD), lambda h, i: (i, h)),
                   pl.BlockSpec((1, ncb, HG_HEAD, HG_HEAD), lambda h, i: (h, i, 0, 0))],
        out_shape=[jax.ShapeDtypeStruct(ycat.shape, ycat.dtype), jax.ShapeDtypeStruct((t, mix_b), F32),
                   jax.ShapeDtypeStruct((heads, t // HG_CHUNK, HG_HEAD, HG_HEAD), F32)],
        scratch_shapes=scratch, input_output_aliases={6: 0},
        compiler_params=_cp("parallel", "arbitrary"), name=name)(z, z, z, z, lb, hg_norm, ycat)


def _hgrn_bwd(z, lb, hg_norm, o_raw, states, dycat, *, name, tb=1024):
    t = z.shape[0]
    mix_b = lb.shape[1]
    heads = mix_b // HG_HEAD
    off = (z.shape[1] - 4 * mix_b) // HG_HEAD
    tb = min(tb, t)
    ncb = tb // HG_CHUNK
    nt = t // tb

    def body(q_ref, f_ref, i_ref, g_ref, lb_ref, hn_ref, o_ref, st_ref, dy_ref,
             dq_ref, dfl_ref, di_ref, dg_ref, dlb_ref, dhn_ref,
             dstate, qh_s, k_s, b_s, qt_s, kt_s, gl_s, do_s, dqh_s, dk_s, db_s):
        first = pl.program_id(1) == 0

        @pl.when(first)
        def _():
            dstate[...] = jnp.zeros_like(dstate)
            dlb_ref[...] = jnp.zeros_like(dlb_ref)

        @pl.when(first & (pl.program_id(0) == 0))
        def _():
            dhn_ref[...] = jnp.zeros_like(dhn_ref)

        sg, f = _hgrn_prepare(q_ref, f_ref, lb_ref, qh_s, k_s, b_s, qt_s, kt_s, gl_s)
        o = o_ref[...]
        r = lax.rsqrt(jnp.mean(o * o, axis=-1, keepdims=True) + EPS)
        oh = o * r
        gv = g_ref[...]
        sgg = _sigmoid(gv)
        dy = dy_ref[...].astype(F32)
        hn = hn_ref[...]
        dg_ref[...] = (dy * oh * hn * (sgg * (1.0 + gv * (1.0 - sgg)))).astype(dg_ref.dtype)
        don = dy * (gv * sgg)
        dhn_ref[...] += jnp.sum(don * oh, axis=0, keepdims=True)
        doh = don * hn
        do_s[...] = r * (doh - oh * jnp.mean(doh * oh, axis=-1, keepdims=True))
        row = lax.broadcasted_iota(jnp.int32, (SUBLANES, HG_HEAD), 0)

        def chunk(ci, carry):
            c = ncb - 1 - ci
            rows = pl.ds(pl.multiple_of(c * HG_CHUNK, HG_CHUNK), HG_CHUNK)
            st_prev = st_ref[0, c]
            dst = dstate[...]
            qh, kk, bq, vv = qh_s[rows, :], k_s[rows, :], b_s[rows, :], i_ref[rows, :]
            qt, kt, doo = qt_s[rows, :], kt_s[rows, :], do_s[rows, :]
            gl = gl_s[rows, :][0:1, :]
            es, a_rep, d_rep, starts = _hgrn_intra(qh, kk, bq, doo, vv)
            dqh = _groups(jnp.exp(bq) * _dot(doo, st_prev, _NN))
            dk = _groups(jnp.exp(bq[HG_CHUNK - 1:HG_CHUNK, :] - bq) * _dot(vv, dst, _NN))
            dv = _groups(_dot(kt, dst, _NT))
            qh_g, do_g = _groups(qh), _groups(doo)
            for s in range(HG_CHUNK):
                g0 = s // SUBLANES
                local = s - g0 * SUBLANES
                dk_acc = dv_acc = None
                for g in range(g0, HG_GROUPS):
                    at = (g - g0) * SUBLANES
                    wgt = d_rep[starts[s] + at:starts[s] + at + SUBLANES, :] * es[s][at:at + SUBLANES, :]
                    avo = a_rep[starts[s] + at:starts[s] + at + SUBLANES, :] * do_g[g]
                    if g == g0:
                        wgt = jnp.where(row >= local, wgt, 0.0)
                        avo = jnp.where(row >= local, avo, 0.0)
                    dqh[g] = dqh[g] + wgt * kk[s:s + 1, :]
                    dk_acc = wgt * qh_g[g] if dk_acc is None else dk_acc + wgt * qh_g[g]
                    dv_acc = avo if dv_acc is None else dv_acc + avo
                dk[g0] = dk[g0] + jnp.where(row == local, jnp.sum(dk_acc, axis=0, keepdims=True), 0.0)
                dv[g0] = dv[g0] + jnp.where(row == local, jnp.sum(dv_acc, axis=0, keepdims=True), 0.0)
            dqh, dk, dv = [jnp.concatenate(p, axis=0) for p in (dqh, dk, dv)]
            row_c = lax.broadcasted_iota(jnp.int32, (HG_CHUNK, HG_HEAD), 0)
            st_next = st_prev * gl + _dot(vv, kt, _TN)
            db = qh * dqh - kk * dk
            db = db + jnp.where(row_c == HG_CHUNK - 1, jnp.sum(st_next * dst, axis=0, keepdims=True), 0.0)
            dstate[...] = dst * gl + _dot(doo, qt, _TN)
            dqh_s[rows, :] = dqh
            dk_s[rows, :] = dk
            db_s[rows, :] = db
            di_ref[rows, :] = dv.astype(di_ref.dtype)
            return carry

        lax.fori_loop(0, ncb, chunk, 0, unroll=2)
        tri = _chunk_tri(False)
        lb_v = lb_ref[...]
        qv = q_ref[...]
        sgq = _sigmoid(qv)
        dq_ref[...] = (dqh_s[...] * (HG_HEAD ** -0.5) * (sgq * (1.0 + qv * (1.0 - sgq)))).astype(dq_ref.dtype)
        dlb = jnp.zeros((1, HG_HEAD), F32)
        for rr in range(tb // LANES):
            rws = slice(rr * LANES, (rr + 1) * LANES)
            dlogf = _dot_f32(tri, db_s[rws, :], _NN)
            df = dlogf / f[rws, :] - dk_s[rws, :]
            sgr = sg[rws, :]
            dfl_ref[rws, :] = (df * (1.0 - lb_v) * sgr * (1.0 - sgr)).astype(dfl_ref.dtype)
            dlb = dlb + jnp.sum(df * (1.0 - sgr), axis=0, keepdims=True)
        dlb_ref[...] += dlb

    def zcol(kind):
        return pl.BlockSpec((tb, HG_HEAD), lambda h, i: (nt - 1 - i, off + kind * heads + h))

    hcol = pl.BlockSpec((tb, HG_HEAD), lambda h, i: (nt - 1 - i, h))
    scratch = [pltpu.VMEM((HG_HEAD, HG_HEAD), F32)] + [pltpu.VMEM((tb, HG_HEAD), F32)] * 10
    out = jax.ShapeDtypeStruct((t, mix_b), MXU_DTYPE)
    return pl.pallas_call(
        body, grid=(heads, nt),
        in_specs=[zcol(0), zcol(1), zcol(2), zcol(3), pl.BlockSpec((1, HG_HEAD), lambda h, i: (0, h)),
                  pl.BlockSpec((1, HG_HEAD), lambda h, i: (0, 0)), hcol,
                  pl.BlockSpec((1, ncb, HG_HEAD, HG_HEAD), lambda h, i: (h, nt - 1 - i, 0, 0)),
                  pl.BlockSpec((tb, HG_HEAD), lambda h, i: (nt - 1 - i, heads + h))],
        out_specs=[hcol, hcol, hcol, hcol, pl.BlockSpec((1, HG_HEAD), lambda h, i: (0, h)),
                   pl.BlockSpec((1, HG_HEAD), lambda h, i: (0, 0))],
        out_shape=[out, out, out, out, jax.ShapeDtypeStruct((1, mix_b), F32),
                   jax.ShapeDtypeStruct((1, HG_HEAD), F32)],
        scratch_shapes=scratch, compiler_params=_cp("arbitrary", "arbitrary"),
        name=name)(z, z, z, z, lb, hg_norm, o_raw, states, dycat)


def _lb_fwd(lb_table, layer, *, name):
    rows, width = lb_table.shape

    def body(t_ref, o_ref):
        tv = t_ref[...]
        e = jnp.exp(tv - jnp.max(tv, axis=0, keepdims=True))
        sm = e / jnp.sum(e, axis=0, keepdims=True)
        o_ref[...] = jnp.sum(sm[1:layer + 2, :], axis=0, keepdims=True)

    return pl.pallas_call(body, out_shape=jax.ShapeDtypeStruct((1, width), F32), name=name)(lb_table)


def _lb_bwd(lb_table, dlb, layer, *, name):
    rows, width = lb_table.shape

    def body(t_ref, d_ref, o_ref):
        tv = t_ref[...]
        e = jnp.exp(tv - jnp.max(tv, axis=0, keepdims=True))
        sm = e / jnp.sum(e, axis=0, keepdims=True)
        ridx = lax.broadcasted_iota(jnp.int32, sm.shape, 0)
        dsm = jnp.where((ridx >= 1) & (ridx <= layer + 1), d_ref[...], 0.0)
        o_ref[...] = sm * (dsm - jnp.sum(sm * dsm, axis=0, keepdims=True))

    return pl.pallas_call(body, out_shape=jax.ShapeDtypeStruct((rows, width), F32), name=name)(lb_table, dlb)


FOX_BLOCK = 1024
FOX_DIAGONAL_STRIPS = 4


def _fox_prep(zf, b_f, *, name, blk=256):
    t = zf.shape[0]

    def body(z_ref, b_ref, fc_ref):
        r = lax.broadcasted_iota(jnp.int32, (blk, blk), 0)
        c = lax.broadcasted_iota(jnp.int32, (blk, blk), 1)
        tri = jnp.where(c <= r, 1.0, 0.0).astype(F32)
        carry = jnp.zeros((1, LANES), F32)
        for j in range(t // blk):
            rows = slice(j * blk, (j + 1) * blk)
            ls = jax.nn.log_sigmoid(z_ref[rows, :] + b_ref[...])
            fb = _dot_f32(tri, ls, _NN) + carry
            carry = fb[blk - 1:blk, :]
            fc_ref[rows, :] = fb

    return pl.pallas_call(
        body, out_shape=jax.ShapeDtypeStruct((t, LANES), F32),
        compiler_params=pltpu.CompilerParams(vmem_limit_bytes=VMEM_LIMIT_BYTES), name=name)(zf, b_f)


def _fox_head_column(fc_ref, fk_s, head):
    lane = lax.broadcasted_iota(jnp.int32, fc_ref.shape, 1)
    fk_s[...] = jnp.sum(jnp.where(lane == head, fc_ref[...], 0.0), axis=1, keepdims=True)


def _fox_scores(k_blk, q_blk, fk_blk, diagonal):
    s = _dot(k_blk, q_blk, _NT) * (FOX_HEAD ** -0.5) - fk_blk
    if diagonal:
        key = lax.broadcasted_iota(jnp.int32, s.shape, 0)
        qry = lax.broadcasted_iota(jnp.int32, s.shape, 1)
        s = jnp.where(key <= qry, s, -jnp.inf)
    return s


def _fox_fwd(zqkv, fcol, *, name):
    t = zqkv.shape[0]
    d = zqkv.shape[1] // 3
    heads = d // FOX_HEAD
    blk = min(FOX_BLOCK, t)
    nq = t // blk

    def body(q_ref, k_ref, v_ref, fc_ref, o_ref, lse_ref, fk_s):
        _fox_head_column(fc_ref, fk_s, pl.program_id(0))

        def q_block(qi, carry):
            qrows = pl.ds(pl.multiple_of(qi * blk, blk), blk)
            q_blk = q_ref[qrows, :]

            def update(st, krows, diagonal):
                m, l, acc = st
                s = _fox_scores(k_ref[krows, :], q_blk, fk_s[krows, :], diagonal)
                m_new = jnp.maximum(m, jnp.max(s, axis=0, keepdims=True))
                alpha = jnp.exp(m - m_new)
                p = jnp.exp(s - m_new)
                l = alpha * l + jnp.sum(p, axis=0, keepdims=True)
                acc = acc * alpha + _dot(v_ref[krows, :], p, _TN)
                return m_new, l, acc

            def k_block(kj, st):
                return update(st, pl.ds(pl.multiple_of(kj * blk, blk), blk), False)

            init = (jnp.full((1, blk), -jnp.inf, F32), jnp.zeros((1, blk), F32),
                    jnp.zeros((FOX_HEAD, blk), F32))
            m, l, acc = update(lax.fori_loop(0, qi, k_block, init), qrows, True)
            o_ref[qrows, :] = (acc / l).T.astype(o_ref.dtype)
            lse_ref[0, :, qrows] = m + jnp.log(l)
            return carry

        lax.fori_loop(0, nq, q_block, 0)

    def col(kind):
        return pl.BlockSpec((t, FOX_HEAD), lambda h: (0, kind * heads + h))

    rowvec = pl.BlockSpec((1, 1, t), lambda h: (h, 0, 0))
    return pl.pallas_call(
        body, grid=(heads,),
        in_specs=[col(0), col(1), col(2), pl.BlockSpec((t, LANES), lambda h: (0, 0))],
        out_specs=[pl.BlockSpec((t, FOX_HEAD), lambda h: (0, h)), rowvec],
        out_shape=[jax.ShapeDtypeStruct((t, d), MXU_DTYPE), jax.ShapeDtypeStruct((heads, 1, t), F32)],
        scratch_shapes=[pltpu.VMEM((t, 1), F32)],
        compiler_params=_cp("parallel"), name=name)(zqkv, zqkv, zqkv, fcol)


def _fox_bwd(zqkv, fcol, lse, o, do, *, name):
    t = zqkv.shape[0]
    d = zqkv.shape[1] // 3
    heads = d // FOX_HEAD
    blk = min(FOX_BLOCK, t)
    nq = t // blk
    strip = blk // FOX_DIAGONAL_STRIPS
    scale = FOX_HEAD ** -0.5

    def body(q_ref, k_ref, v_ref, fc_ref, lse_ref, o_ref, do_ref,
             dq_ref, dk_ref, dv_ref, rq_ref, rk_ref, dq_s, drow_s, fk_s, acc_s, rk_s):
        _fox_head_column(fc_ref, fk_s, pl.program_id(0))
        dq_s[...] = jnp.zeros_like(dq_s)
        rq_ref[...] = jnp.zeros_like(rq_ref)
        ones_f = jnp.ones((8, FOX_HEAD), F32)
        for j in range(nq):
            rows = slice(j * blk, (j + 1) * blk)
            prod = do_ref[rows, :].astype(F32) * o_ref[rows, :].astype(F32)
            drow_s[:, rows] = _dot_f32(ones_f, prod, _NT)

        def k_block(kj, carry):
            krows = pl.ds(pl.multiple_of(kj * blk, blk), blk)
            k_blk, v_blk, fk_blk = k_ref[krows, :], v_ref[krows, :], fk_s[krows, :]

            def pair(qrows, diagonal, keys=slice(0, blk)):
                q_blk, do_blk = q_ref[qrows, :], do_ref[qrows, :]
                s = _fox_scores(k_blk[keys, :], q_blk, fk_blk[keys, :], diagonal)
                p = jnp.exp(s - lse_ref[0, :, qrows])
                acc_s[1, keys, :] += _dot(p, do_blk, _NN)
                dp = _dot(v_blk[keys, :], do_blk, _NT)
                ds = (p * (dp - drow_s[0:1, qrows])).astype(MXU_DTYPE)
                acc_s[0, keys, :] += _dot(ds, q_blk, _NN)
                dq_s[qrows, :] += _dot(ds, k_blk[keys, :], _TN)
                ds_f = ds.astype(F32)
                rq_ref[0, :, qrows] += jnp.sum(ds_f, axis=0, keepdims=True)
                rk_s[keys, :] += jnp.sum(ds_f, axis=1, keepdims=True)

            def q_block(qi, carry2):
                pair(pl.ds(pl.multiple_of(qi * blk, blk), blk), False)
                return carry2

            acc_s[...] = jnp.zeros_like(acc_s)
            rk_s[...] = jnp.zeros_like(rk_s)
            for j in range(FOX_DIAGONAL_STRIPS):
                pair(pl.ds(pl.multiple_of(kj * blk + j * strip, strip), blk - j * strip), True,
                     slice(j * strip, (j + 1) * strip))
            lax.fori_loop(kj + 1, nq, q_block, 0)
            dk_ref[krows, :] = (acc_s[0] * scale).astype(dk_ref.dtype)
            dv_ref[krows, :] = acc_s[1].astype(dv_ref.dtype)
            rk_ref[0, :, krows] = jnp.broadcast_to(rk_s[...], (blk, FOX_HEAD)).T[0:1, :]
            return carry

        lax.fori_loop(0, nq, k_block, 0)
        dq_ref[...] = (dq_s[...] * scale).astype(dq_ref.dtype)

    def col(kind):
        return pl.BlockSpec((t, FOX_HEAD), lambda h: (0, kind * heads + h))

    hcol = pl.BlockSpec((t, FOX_HEAD), lambda h: (0, h))
    rowvec = pl.BlockSpec((1, 1, t), lambda h: (h, 0, 0))
    out = jax.ShapeDtypeStruct((t, d), MXU_DTYPE)
    vec = jax.ShapeDtypeStruct((heads, 1, t), F32)
    return pl.pallas_call(
        body, grid=(heads,),
        in_specs=[col(0), col(1), col(2), pl.BlockSpec((t, LANES), lambda h: (0, 0)), rowvec, hcol, hcol],
        out_specs=[hcol, hcol, hcol, rowvec, rowvec],
        out_shape=[out, out, out, vec, vec],
        scratch_shapes=[pltpu.VMEM((t, FOX_HEAD), F32), pltpu.VMEM((8, t), F32), pltpu.VMEM((t, 1), F32),
                        pltpu.VMEM((2, blk, FOX_HEAD), F32), pltpu.VMEM((blk, 1), F32)],
        compiler_params=_cp("parallel"), name=name)(zqkv, zqkv, zqkv, fcol, lse, o, do)


def _fox_gate_bwd(rq, rk, zf, b_f, *, name, blk=256):
    heads, t = rq.shape

    def body(rq_ref, rk_ref, z_ref, b_ref, dfl_ref, db_ref):
        r = lax.broadcasted_iota(jnp.int32, (blk, blk), 0)
        c = lax.broadcasted_iota(jnp.int32, (blk, blk), 1)
        tri = jnp.where(r >= c, 1.0, 0.0).astype(F32)
        carry = jnp.zeros((heads, 1), F32)
        db = jnp.zeros((1, LANES), F32)
        pad = jnp.zeros((LANES - heads, blk), F32)
        for j in reversed(range(t // blk)):
            cols = slice(j * blk, (j + 1) * blk)
            df = rq_ref[:, cols] - rk_ref[:, cols]
            dls = _dot_f32(df, tri, _NN) + carry
            carry = dls[:, 0:1]
            dls_t = jnp.concatenate([dls, pad], axis=0).T
            dfl = dls_t * _sigmoid(-(z_ref[cols, :] + b_ref[...]))
            dfl_ref[cols, :] = dfl.astype(dfl_ref.dtype)
            db = db + jnp.sum(dfl, axis=0, keepdims=True)
        db_ref[...] = db

    return pl.pallas_call(
        body, out_shape=[jax.ShapeDtypeStruct((t, LANES), MXU_DTYPE), jax.ShapeDtypeStruct((1, LANES), F32)],
        compiler_params=pltpu.CompilerParams(vmem_limit_bytes=VMEM_LIMIT_BYTES), name=name)(rq, rk, zf, b_f)


def _adamw(w, m, v, parts, *, name, layer=None, prev=None, tr=256):
    lcnt, r, c = w.shape
    p = parts.shape[0]
    li = 0 if layer is None else layer
    tr = _tile(r, tr, 16)
    tc = c if tr * c <= 256 * 2048 else _tile(c, 256)
    has_prev = prev is not None

    def body(*refs):
        w_ref, m_ref, v_ref, p_ref = refs[:4]
        g_ref, d_ref, nm_ref, nv_ref = refs[-4:]
        g = p_ref[0].astype(F32)
        for j in range(1, p):
            g = g + p_ref[j].astype(F32)
        wv = w_ref[0]
        mn = ADAM_B1 * m_ref[0] + (1.0 - ADAM_B1) * g
        vn = ADAM_B2 * v_ref[0] + (1.0 - ADAM_B2) * (g * g)
        m_hat = mn / (1.0 - ADAM_B1 ** ADAM_STEP)
        v_hat = vn / (1.0 - ADAM_B2 ** ADAM_STEP)
        g_ref[0] = g
        d_ref[0] = -ADAM_LR * (m_hat / (jnp.sqrt(v_hat) + ADAM_EPS) + ADAM_WD * wv)
        nm_ref[0] = mn
        nv_ref[0] = vn

    slab = pl.BlockSpec((1, tr, tc), lambda i, j: (li, i, j))
    in_specs = [slab, slab, slab, pl.BlockSpec((p, tr, tc), lambda i, j: (0, i, j))]
    operands = [w, m, v, parts]
    aliases = {}
    if has_prev:
        in_specs += [pl.BlockSpec(memory_space=pl.ANY)] * 4
        operands += list(prev)
        aliases = {4: 0, 5: 1, 6: 2, 7: 3}
    shp = jax.ShapeDtypeStruct((lcnt, r, c), F32)
    return pl.pallas_call(
        body, grid=(r // tr, c // tc), in_specs=in_specs, out_specs=[slab] * 4, out_shape=[shp] * 4,
        input_output_aliases=aliases, compiler_params=_cp("parallel", "parallel"), name=name)(*operands)


def _my_place():
    return lax.axis_index("x"), lax.axis_index("y"), lax.axis_index("c")


def _slot(p):
    return 4 * p[0] + 2 * p[1] + p[2]


def _peer(me, mask):
    x, y, c = me
    return (1 - x if mask & 4 else x, 1 - y if mask & 2 else y, 1 - c if mask & 1 else c)


_HBM = pl.BlockSpec(memory_space=pltpu.HBM)
_SEM = pl.BlockSpec(memory_space=pltpu.SEMAPHORE)
_ANY = pl.BlockSpec(memory_space=pl.ANY)
_EFFECT = pltpu.SideEffectType.DATAFLOW_SIDE_EFFECTING


def _push_copy(src_refs, land_refs, send_sems, recv_sems, a, mask, me, per_peer, outgoing):
    peer = _peer(me, mask)
    src = src_refs[a].at[_slot(peer)] if per_peer else land_refs[a].at[_slot(me)]
    dst = land_refs[a].at[_slot(me) if outgoing else _slot(peer)]
    k = a * (N_DEV - 1) + mask - 1
    return pltpu.make_async_remote_copy(
        src_ref=src, dst_ref=dst, send_sem=send_sems.at[k], recv_sem=recv_sems.at[k],
        device_id=peer, device_id_type=MESH)


ALL_PEERS = tuple(range(1, N_DEV))
CHIP_PEERS = (2, 4, 6)
FIRST_HOP = (1,) + CHIP_PEERS


def _push_start(srcs, dep, *, per_peer, name, masks=ALL_PEERS):
    n = len(srcs)
    mine = _slot(_my_place())
    lands = []
    for s in srcs:
        own = lax.dynamic_index_in_dim(s, mine, 0, keepdims=True) if per_peer else s[None]
        shape = s.shape if per_peer else (N_DEV,) + s.shape
        lands.append(lax.dynamic_update_slice_in_dim(lax.empty(shape, s.dtype), own, mine, 0))
    has_dep = dep is not None
    n_src = n if per_peer else 0
    n_buf = n_src + n

    def body(*refs):
        src_refs, land_refs = refs[:n_src], refs[n_src:n_buf]
        send_sems, recv_sems = refs[n_buf + has_dep], refs[n_buf + has_dep + 1]
        token = refs[-1]
        me = _my_place()
        for a in range(n):
            for mask in masks:
                _push_copy(src_refs, land_refs, send_sems, recv_sems, a, mask, me, per_peer, True).start()
        token[...] = jnp.zeros_like(token)

    hbm_in = [pltpu.with_memory_space_constraint(v, pltpu.HBM) for v in (list(srcs) if per_peer else []) + lands]
    out = pl.pallas_call(
        body, name=name,
        out_shape=(pltpu.SemaphoreType.DMA((n * (N_DEV - 1),)), pltpu.SemaphoreType.DMA((n * (N_DEV - 1),)),
                   *[pltpu.HBM(v.shape, v.dtype) for v in hbm_in], jax.ShapeDtypeStruct((8, LANES), F32)),
        in_specs=[_HBM] * n_buf + ([_ANY] if has_dep else []),
        out_specs=(_SEM, _SEM, *[_HBM] * n_buf, pl.BlockSpec(memory_space=pltpu.VMEM)),
        input_output_aliases={i: 2 + i for i in range(n_buf)},
        compiler_params=pltpu.CompilerParams(has_side_effects=_EFFECT),
    )(*hbm_in, *([dep] if has_dep else []))
    return (n, per_peer, masks, out[:-1]), out[-1]


def _push_wait(handle, after, *, name):
    n, per_peer, masks, (send_sems, recv_sems, *bufs) = handle
    n_src = n if per_peer else 0
    n_buf = n_src + n

    def body(*refs):
        src_refs, land_refs = refs[:n_src], refs[n_src:n_buf]
        send_sems, recv_sems = refs[n_buf], refs[n_buf + 1]
        me = _my_place()
        for a in range(n):
            for mask in masks:
                cp = _push_copy(src_refs, land_refs, send_sems, recv_sems, a, mask, me, per_peer, False)
                cp.wait_send()
                cp.wait_recv()

    out = pl.pallas_call(
        body, name=name, out_shape=tuple(pltpu.HBM(v.shape, v.dtype) for v in bufs),
        in_specs=[_HBM] * n_buf + [_SEM, _SEM, _ANY], out_specs=tuple([_HBM] * n_buf),
        input_output_aliases={i: i for i in range(n_buf)},
        compiler_params=pltpu.CompilerParams(has_side_effects=_EFFECT),
    )(*bufs, send_sems, recv_sems, after)
    return list(out[n_src:])


def _relay_copy(land_refs, send_sems, recv_sems, a, j, me, outgoing):
    sibling = _peer(me, 1)
    out_slot = _slot(_peer(me, CHIP_PEERS[j]))
    in_slot = _slot(_peer(sibling, CHIP_PEERS[j]))
    k = a * len(CHIP_PEERS) + j
    return pltpu.make_async_remote_copy(
        src_ref=land_refs[a].at[out_slot], dst_ref=land_refs[a].at[out_slot if outgoing else in_slot],
        send_sem=send_sems.at[k], recv_sem=recv_sems.at[k], device_id=sibling, device_id_type=MESH)


def _relay_start(lands, *, name):
    n = len(lands)

    def body(*refs):
        land_refs, send_sems, recv_sems, token = refs[:n], refs[n], refs[n + 1], refs[-1]
        me = _my_place()
        for a in range(n):
            for j in range(len(CHIP_PEERS)):
                _relay_copy(land_refs, send_sems, recv_sems, a, j, me, True).start()
        token[...] = jnp.zeros_like(token)

    hbm_in = [pltpu.with_memory_space_constraint(v, pltpu.HBM) for v in lands]
    n_sem = n * len(CHIP_PEERS)
    out = pl.pallas_call(
        body, name=name,
        out_shape=(pltpu.SemaphoreType.DMA((n_sem,)), pltpu.SemaphoreType.DMA((n_sem,)),
                   *[pltpu.HBM(v.shape, v.dtype) for v in hbm_in], jax.ShapeDtypeStruct((8, LANES), F32)),
        in_specs=[_HBM] * n, out_specs=(_SEM, _SEM, *[_HBM] * n, pl.BlockSpec(memory_space=pltpu.VMEM)),
        input_output_aliases={i: 2 + i for i in range(n)},
        compiler_params=pltpu.CompilerParams(has_side_effects=_EFFECT),
    )(*hbm_in)
    return (n, out[:-1]), out[-1]


def _relay_wait(handle, after, *, name):
    n, (send_sems, recv_sems, *bufs) = handle

    def body(*refs):
        land_refs, send_sems, recv_sems = refs[:n], refs[n], refs[n + 1]
        me = _my_place()
        for a in range(n):
            for j in range(len(CHIP_PEERS)):
                cp = _relay_copy(land_refs, send_sems, recv_sems, a, j, me, False)
                cp.wait_send()
                cp.wait_recv()

    out = pl.pallas_call(
        body, name=name, out_shape=tuple(pltpu.HBM(v.shape, v.dtype) for v in bufs),
        in_specs=[_HBM] * n + [_SEM, _SEM, _ANY], out_specs=tuple([_HBM] * n),
        input_output_aliases={i: i for i in range(n)},
        compiler_params=pltpu.CompilerParams(has_side_effects=_EFFECT),
    )(*bufs, send_sems, recv_sems, after)
    return list(out)


def _all_reduce_rows(v, *, name):
    r, c = v.shape

    def body(v_ref, o_ref, buf, send_sems, recv_sems):
        me = _my_place()
        mine = _slot(me)
        sends = []
        for mask in range(1, N_DEV):
            peer = _peer(me, mask)
            sends.append(pltpu.make_async_remote_copy(
                src_ref=v_ref, dst_ref=buf.at[mine], send_sem=send_sems.at[mask - 1],
                recv_sem=recv_sems.at[mask - 1], device_id=peer, device_id_type=MESH))
        for cp in sends:
            cp.start()
        buf[mine] = v_ref[...]
        for mask in range(1, N_DEV):
            peer = _peer(me, mask)
            pltpu.make_async_remote_copy(
                src_ref=v_ref, dst_ref=buf.at[_slot(peer)], send_sem=send_sems.at[mask - 1],
                recv_sem=recv_sems.at[mask - 1], device_id=peer, device_id_type=MESH).wait_recv()
        for cp in sends:
            cp.wait_send()
        total = buf[0]
        for j in range(1, N_DEV):
            total = total + buf[j]
        o_ref[...] = total

    vm = pl.BlockSpec(memory_space=pltpu.VMEM)
    return pl.pallas_call(
        body, in_specs=[vm], out_specs=vm, out_shape=jax.ShapeDtypeStruct((r, c), F32),
        scratch_shapes=[pltpu.VMEM((N_DEV, r, c), F32), pltpu.SemaphoreType.DMA((7,)),
                        pltpu.SemaphoreType.DMA((7,))],
        name=name)(v)


def _xa_fwd(x, memn, g_x, wq, wkv, wo, tag):
    hx = _rms_fwd(x, g_x, name=f"xa{tag}_norm")
    q = _mm2(hx, wq, "nn", MXU_DTYPE, name=f"xa{tag}_q", tm=2048)
    kv = _mm2(memn, wkv, "nn", MXU_DTYPE, name=f"xa{tag}_kv", b_split=True)
    o = _xattn_fwd(q, kv, name=f"xa{tag}_attn")
    return _mm2(o, wo, "nn", F32, name=f"xa{tag}_out", add=x), (hx, memn, q, kv, o)


def _xa_bwd(x, mem, g_x, g_m, wq, wkv, wo, saved, dxo, dxo_lo, tag, put):
    hx, memn, q, kv, o = saved
    do = _mm2(dxo_lo, wo, "nt", MXU_DTYPE, name=f"xa{tag}_do")
    dwo = _mm2(o, dxo_lo, "tn", MXU_DTYPE, name=f"xa{tag}_dwo")
    dq, dkv = _xattn_bwd(q, kv, do, name=f"xa{tag}_attn_bwd")
    dwq = _mm2(hx, dq, "tn", MXU_DTYPE, name=f"xa{tag}_dwq")
    dwkv = _mm2(memn, dkv, "tn", MXU_DTYPE, name=f"xa{tag}_dwkv", o_split=True, tn=wkv.shape[2])
    tok = put((dwq, dwkv, dwo))
    dhx = _mm2(dq, wq, "nt", F32, name=f"xa{tag}_dh", dep=tok)
    dx, dx_lo, dgx = _rms_bwd(x, g_x, dhx, dxo, name=f"xa{tag}_norm_bwd")
    dmemn = _mm2(dkv, wkv, "nt", F32, name=f"xa{tag}_dmem", b_split=True)
    _, _, dgm = _rms_bwd(mem, g_m, dmemn, None, name=f"xa{tag}_mem_norm_bwd")
    return dx, dx_lo, dgx, dgm


def _ffn_fwd(x, g, wg, wu, wd, tag):
    h = _rms_fwd(x, g, name=f"ffn{tag}_norm")
    gate, up, act = _ffn_up(h, wg, wu, name=f"ffn{tag}_up")
    wd = wd(act) if callable(wd) else wd
    return _ffn_down(act, wd, x, name=f"ffn{tag}_down"), (h, gate, up, act), wd


def _ffn_bwd(x, g, wg, wu, wd, saved, dxo, dxo_lo, tag, put):
    h, gate, up, act = saved
    dwd = _mm(act, dxo_lo[None], "tn", MXU_DTYPE, name=f"ffn{tag}_dwd")
    dgate, dup = _ffn_dact(dxo_lo, wd, gate, up, name=f"ffn{tag}_dact")
    dwg = _mm(dgate, h[None], "tn", MXU_DTYPE, name=f"ffn{tag}_dwg")
    dwu = _mm(dup, h[None], "tn", MXU_DTYPE, name=f"ffn{tag}_dwu")
    tok = put((dwg, dwu, dwd))
    dh = _ffn_dh(dgate, dup, wg, wu, tok, name=f"ffn{tag}_dh")
    dx, dx_lo, dg = _rms_bwd(x, g, dh, dxo, name=f"ffn{tag}_norm_bwd")
    return dx, dx_lo, dg


def _even_fwd(x, h, lb, w_in, w_pool, pool_scale, hg_norm, w_out):
    z = _mm2(h, w_in, "nn", F32, name="ev_in", b_split=True)
    ycat = _pool_fwd(z, w_pool, pool_scale, name="ev_pool")
    ycat, o_raw, states = _hgrn_fwd(z, lb, hg_norm, ycat, name="ev_hgrn")
    w_out = w_out(ycat) if callable(w_out) else w_out
    return _mm2(ycat, w_out, "nn", F32, name="ev_out", add=x), (h, z, ycat, o_raw, states), w_out


def _even_bwd(x, g, lb, w_in, w_pool, pool_scale, hg_norm, w_out, saved, dxo, dxo_lo, put):
    h, z, ycat, o_raw, states = saved
    dycat = _mm2(dxo_lo, w_out, "nt", MXU_DTYPE, name="ev_dy")
    dw_out = _mm2(ycat, dxo_lo, "tn", MXU_DTYPE, name="ev_dw_out")
    du, dw_pool, dscale = _pool_bwd(z, w_pool, pool_scale, dycat, name="ev_pool_bwd")
    dq, dfl, di, dg, dlb, dhn = _hgrn_bwd(z, lb, hg_norm, o_raw, states, dycat, name="ev_hgrn_bwd")
    dz = jnp.concatenate([du, dq, dfl, di, dg], axis=1)
    dw_in = _mm2(h, dz, "tn", MXU_DTYPE, name="ev_dw_in", o_split=True, tn=w_in.shape[2])
    tok = put((dw_in, dw_pool, dw_out))
    dh = _mm_nt_split(dz, w_in, tok, name="ev_dh")
    dx, dx_lo, dgn = _rms_bwd(x, g, dh, dxo, name="ev_norm_bwd")
    return dx, dx_lo, dict(ev_norm=dgn, ev_pool_scale=dscale, ev_hg_norm=dhn, lb=dlb)


def _odd_fwd(x, g, w_qkv, w_f, b_f, w_out):
    n_qkv = 3 * x.shape[1]
    h = _rms_fwd(x, g, name="od_norm")
    zqkv = _mm2(h, w_qkv, "nt", MXU_DTYPE, name="od_qkv", n_b=n_qkv)
    zf = _mm2(h, w_f, "nt", F32, name="od_gate")
    fcol = _fox_prep(zf, b_f, name="od_fox_prep")
    o, lse = _fox_fwd(zqkv, fcol, name="od_fox")
    return _mm2(o, w_out, "nn", F32, name="od_out", add=x), (h, zqkv, zf, fcol, o, lse)


def _odd_bwd(x, g, w_qkv, w_f, b_f, w_out, saved, dxo, dxo_lo, put):
    h, zqkv, zf, fcol, o, lse = saved
    do = _mm2(dxo_lo, w_out, "nt", MXU_DTYPE, name="od_do")
    dw_out = _mm2(o, dxo_lo, "tn", MXU_DTYPE, name="od_dw_out")
    dq, dk, dv, rq, rk = _fox_bwd(zqkv, fcol, lse, o, do, name="od_fox_bwd")
    dfl, db_f = _fox_gate_bwd(rq[:, 0, :], rk[:, 0, :], zf, b_f, name="od_fox_gate_bwd")
    dz = jnp.concatenate([dq, dk, dv], axis=1)
    dw_qkv = _mm2(dz, h, "tn", MXU_DTYPE, name="od_dw_qkv")
    dw_f = _mm2(dfl, h, "tn", MXU_DTYPE, name="od_dw_gate")
    tok = put((dw_qkv, dw_f, dw_out))
    dh = _mm2(dz, w_qkv, "nn", F32, name="od_dh_qkv", dep=tok)
    dh = _mm2(dfl, w_f, "nn", F32, name="od_dh_gate", add=dh)
    dx, dx_lo, dgn = _rms_bwd(x, g, dh, dxo, name="od_norm_bwd")
    return dx, dx_lo, dict(od_norm=dgn, od_b_f=db_f)


def _local_step(x, mem, target, sp, get_w, put_dw):
    b_f = jnp.pad(sp["od_b_f"], ((0, 0), (0, LANES - sp["od_b_f"].shape[1])))
    lb = _lb_fwd(sp["lb_table"], 0, name="lb_fwd")
    fin = sp["final_norm"].reshape(1, -1)
    xn, xm, fn = sp["xa_norm"], sp["xa_mem_norm"], sp["ffn_norm"]
    h0 = _rms_fwd(x, sp["ev_norm"], name="ev_norm")
    memn = [_rms_fwd(mem, xm[l:l + 1], name=f"xa{l}_mem_norm") for l in range(xm.shape[0])]
    w_ev = get_w("ev", h0)
    x1, s_ev, w_ev_out = _even_fwd(x, h0, lb, w_ev[0], w_ev[1], sp["ev_pool_scale"], sp["ev_hg_norm"], w_ev[2])
    w_xa0 = get_w("xa0", x1)
    x2, s_xa0 = _xa_fwd(x1, memn[0], xn[0:1], *w_xa0, 0)
    w_ff0 = get_w("ffn0", x2)
    x3, s_ff0, wd0 = _ffn_fwd(x2, fn[0:1], *w_ff0, 0)
    w_ff0 = (w_ff0[0], w_ff0[1], wd0)
    w_qkv, w_f, w_od_out, od_norm = get_w("od", x3)
    x4, s_od = _odd_fwd(x3, od_norm, w_qkv, w_f, b_f, w_od_out)
    w_xa1 = get_w("xa1", x4)
    x5, s_xa1 = _xa_fwd(x4, memn[1], xn[1:2], *w_xa1, 1)
    w_ff1 = get_w("ffn1", x5)
    x6, s_ff1, wd1 = _ffn_fwd(x5, fn[1:2], *w_ff1, 1)
    w_ff1 = (w_ff1[0], w_ff1[1], wd1)
    loss, dx, dx_lo, d_fin = _loss_head(x6, fin, target, name="loss_head")
    put = lambda grp: functools.partial(put_dw, grp)
    dx, dx_lo, d_ffn1 = _ffn_bwd(x5, fn[1:2], *w_ff1, s_ff1, dx, dx_lo, 1, put("ffn1"))
    dx, dx_lo, d_xa1, d_xm1 = _xa_bwd(x4, mem, xn[1:2], xm[1:2], *w_xa1, s_xa1, dx, dx_lo, 1, put("xa1"))
    dx, dx_lo, d_od = _odd_bwd(x3, od_norm, w_qkv, w_f, b_f, w_od_out, s_od, dx, dx_lo, put("od"))
    dx, dx_lo, d_ffn0 = _ffn_bwd(x2, fn[0:1], *w_ff0, s_ff0, dx, dx_lo, 0, put("ffn0"))
    dx, dx_lo, d_xa0, d_xm0 = _xa_bwd(x1, mem, xn[0:1], xm[0:1], *w_xa0, s_xa0, dx, dx_lo, 0, put("xa0"))
    dx, _, d_ev = _even_bwd(x, sp["ev_norm"], lb, w_ev[0], w_ev[1], sp["ev_pool_scale"], sp["ev_hg_norm"],
                            w_ev_out, s_ev, dx, dx_lo, put("ev"))
    small = dict(
        lb_table=_lb_bwd(sp["lb_table"], d_ev["lb"], 0, name="lb_bwd"),
        ev_norm=d_ev["ev_norm"], ev_pool_scale=d_ev["ev_pool_scale"], ev_hg_norm=d_ev["ev_hg_norm"],
        od_norm=d_od["od_norm"], od_b_f=d_od["od_b_f"][:, :sp["od_b_f"].shape[1]],
        xa_norm=jnp.concatenate([d_xa0, d_xa1], axis=0), xa_mem_norm=jnp.concatenate([d_xm0, d_xm1], axis=0),
        ffn_norm=jnp.concatenate([d_ffn0, d_ffn1], axis=0), final_norm=d_fin.reshape(-1))
    return loss, dx, small


_SMALL = ("lb_table", "ev_norm", "ev_pool_scale", "ev_hg_norm", "od_norm", "od_b_f", "xa_norm", "xa_mem_norm",
          "ffn_norm", "final_norm")
_WEIGHTS = ("lb_table", "ev_norm", "ev_w_in", "ev_w_pool", "ev_pool_scale", "ev_hg_norm", "ev_w_out", "od_norm",
            "od_w_in", "od_b_f", "od_w_out", "xa_norm", "xa_mem_norm", "xa_wq", "xa_wkv", "xa_wo", "ffn_norm",
            "ffn_w_gate", "ffn_w_up", "ffn_w_down", "final_norm")


def _lo(a):
    return a.astype(MXU_DTYPE)


def _rows(v):
    flat = v.reshape(-1)
    return jnp.pad(flat, (0, (-flat.shape[0]) % LANES)).reshape(-1, LANES)


def kernel(x, mem, lb_table, ev_norm, ev_w_in, ev_w_pool, ev_pool_scale, ev_hg_norm, ev_w_out, od_norm, od_w_in, od_b_f, od_w_out, xa_norm, xa_mem_norm, xa_wq, xa_wkv, xa_wo, ffn_norm, ffn_w_gate, ffn_w_up, ffn_w_down, final_norm, loss_target, m_lb_table, m_ev_norm, m_ev_w_in, m_ev_w_pool, m_ev_pool_scale, m_ev_hg_norm, m_ev_w_out, m_od_norm, m_od_w_in, m_od_b_f, m_od_w_out, m_xa_norm, m_xa_mem_norm, m_xa_wq, m_xa_wkv, m_xa_wo, m_ffn_norm, m_ffn_w_gate, m_ffn_w_up, m_ffn_w_down, m_final_norm, v_lb_table, v_ev_norm, v_ev_w_in, v_ev_w_pool, v_ev_pool_scale, v_ev_hg_norm, v_ev_w_out, v_od_norm, v_od_w_in, v_od_b_f, v_od_w_out, v_xa_norm, v_xa_mem_norm, v_xa_wq, v_xa_wkv, v_xa_wo, v_ffn_norm, v_ffn_w_gate, v_ffn_w_up, v_ffn_w_down, v_final_norm):
    arg = dict(locals())
    d = x.shape[-1]
    layers = xa_wq.shape[0]
    me = _slot(_my_place())

    n_gate = od_b_f.shape[1]
    turned = {k: jnp.swapaxes(arg[k], 1, 2) for k in ("od_w_in", "ffn_w_gate", "ffn_w_up")}
    raw = dict(ev=[ev_w_in[0], ev_w_pool[0]], ev_out=[ev_w_out[0]],
               od=[turned["od_w_in"][0], od_w_out[0], od_norm])
    for l in range(layers):
        raw[f"xa{l}"] = [xa_wq[l], xa_wkv[l], xa_wo[l]]
        raw[f"ffn{l}"] = [turned["ffn_w_gate"][l], turned["ffn_w_up"][l]]
        raw[f"ffn{l}_down"] = [ffn_w_down[l]]
    order = ("ev", "ev_out", "xa0", "ffn0", "ffn0_down", "od", "xa1", "ffn1", "ffn1_down")
    gathers, relays, tok = {}, {}, None
    for grp in order:
        srcs = [w if tok is None else w + tok[0, 0] for w in raw[grp]]
        srcs = [w if grp == "od" and j == 2 else _lo(w) for j, w in enumerate(srcs)]
        gathers[grp], tok = _push_start(srcs, None, per_peer=False, masks=FIRST_HOP, name=f"gather_{grp}_start")
    last_start = tok

    def second_hop(grp, after):
        lands = _push_wait(gathers[grp], after, name=f"gather_{grp}_wait")
        relays[grp], token = _relay_start(lands, name=f"gather_{grp}_relay")
        return token

    def get_w(grp, after):
        i = order.index(grp)
        if i == 0:
            after = last_start if after is None else last_start + after[:8, :LANES].astype(F32)
        if grp not in relays:
            after = second_hop(grp, after)
        if 1 <= i < len(order) - 1:
            after = second_hop(order[i + 1], after)
        got = _relay_wait(relays[grp], after, name=f"gather_{grp}_relay_wait")
        if grp == "ev":
            w_in, w_pool = got
            w_pool = jnp.transpose(w_pool, (1, 0, 2, 3)).reshape(w_pool.shape[1], -1, w_pool.shape[3])
            return w_in, w_pool, lambda later: get_w("ev_out", later)[0].reshape(d, d)
        if grp.startswith("ffn") and not grp.endswith("down"):
            return got[0], got[1], lambda later: get_w(grp + "_down", later)[0]
        if grp == "od":
            w_in, w_out, nrm = got
            assert n_gate <= w_in.shape[1]
            w_f = jnp.pad(w_in[N_DEV - 1, w_in.shape[1] - n_gate:], ((0, LANES - n_gate), (0, 0)))
            return w_in.reshape(-1, d), w_f, w_out.reshape(d, d), nrm.reshape(1, d)
        if grp.startswith("xa"):
            return got[0].reshape(d, d), got[1], got[2].reshape(d, d)
        return tuple(got)

    def row_parts(g):
        return g.reshape(N_DEV, -1, g.shape[-1])

    scatters = {}

    def put_dw(grp, dws):
        if grp == "ev":
            dw_in, dw_pool, dw_out = dws
            gc = dw_pool.shape[1] // N_DEV
            dw_pool = _lo(jnp.transpose(dw_pool.reshape(dw_pool.shape[0], N_DEV, gc, -1), (1, 0, 2, 3)))
            parts = [dw_in, dw_pool, row_parts(dw_out)]
        elif grp == "od":
            dw_qkv, dw_f, dw_out = dws
            parts = [row_parts(jnp.concatenate([dw_qkv, dw_f[:n_gate]], axis=0)), row_parts(dw_out)]
        elif grp.startswith("xa"):
            parts = [row_parts(dws[0]), dws[1], row_parts(dws[2])]
        else:
            parts = list(dws)
        scatters[grp], token = _push_start(parts, None, per_peer=True, name=f"scatter_{grp}_start")
        return token

    sp = {k: arg[k] for k in _SMALL if k != "od_norm"}
    loss, dx, small = _local_step(x[0], mem[0], loss_target[0], sp, get_w, put_dw)

    pieces = [_rows(small[k]) for k in _SMALL]
    packed = jnp.concatenate(pieces + [_rows(loss)], axis=0)
    packed = jnp.pad(packed, ((0, (-packed.shape[0]) % 8), (0, 0)))
    total = _all_reduce_rows(packed, name="all_reduce_small")
    loss = total[sum(pc.shape[0] for pc in pieces), 0]
    small_g, at = {}, 0
    for k, pc in zip(_SMALL, pieces):
        n = small[k].size
        small_g[k] = total[at:at + pc.shape[0]].reshape(-1)[:n].reshape(small[k].shape)
        at += pc.shape[0]
    small_g["od_norm"] = lax.dynamic_slice_in_dim(small_g["od_norm"], me * od_norm.shape[1], od_norm.shape[1], axis=1)

    res = {}
    for k in _SMALL:
        w, m, v = arg[k], arg["m_" + k], arg["v_" + k]
        shp = (1, 1, w.shape[0]) if w.ndim == 1 else (1,) + w.shape
        out = _adamw(w.reshape(shp), m.reshape(shp), v.reshape(shp), small_g[k].reshape(shp), name=f"adamw_{k}")
        res[k] = [o.reshape(w.shape) for o in out]
    members = dict(ev=("ev_w_in", "ev_w_pool", "ev_w_out"), od=("od_w_in", "od_w_out"),
                   xa=("xa_wq", "xa_wkv", "xa_wo"), ffn=("ffn_w_gate", "ffn_w_up", "ffn_w_down"))
    after, stacked = dx, {}
    for grp in ("ffn1", "xa1", "od", "ffn0", "xa0", "ev"):
        got = _push_wait(scatters[grp], after, name=f"scatter_{grp}_wait")
        kind = grp.rstrip("01")
        for k, parts in zip(members[kind], got):
            w, m, v = [jnp.swapaxes(a, 1, 2) if k in turned else a for a in (arg[k], arg["m_" + k], arg["v_" + k])]
            if w.shape[0] == 1:
                shp = (1, -1, w.shape[-1])
                out = _adamw(w.reshape(shp), m.reshape(shp), v.reshape(shp), parts.reshape(N_DEV, -1, w.shape[-1]),
                             name=f"adamw_{k}")
            else:
                out = _adamw(w, m, v, parts, name=f"adamw_{k}{grp[-1]}", layer=int(grp[-1]), prev=stacked.get(k))
                stacked[k] = out
            res[k] = [jnp.swapaxes(o.reshape(w.shape), 1, 2) if k in turned else o.reshape(w.shape) for o in out]
            after = out[3][:1, :8, :LANES]

    outs = [loss, dx[None]]
    for j in range(4):
        outs += [res[k][j] for k in _WEIGHTS]
    return tuple(outs)
```

```python
import functools

import jax
import jax.numpy as jnp
from jax import lax
from jax.experimental import pallas as pl
from jax.experimental.pallas import tpu as pltpu

F32 = jnp.float32
MXU_DTYPE = jnp.bfloat16
EPS = 1e-6
N_DEV = 8
V7X_VMEM_BYTES = 64 * 1024 * 1024
VMEM_LIMIT_BYTES = V7X_VMEM_BYTES - 8 * 1024 * 1024
LANES = 128
HIGHEST = lax.Precision.HIGHEST
MESH = pl.DeviceIdType.MESH

HG_HEAD = 128
HG_CHUNK = 32
FOX_HEAD = 128
XA_HEADS = 4
POOL_GROUPS = 4

ADAM_LR = 0.001
ADAM_B1 = 0.9
ADAM_B2 = 0.999
ADAM_EPS = 1e-08
ADAM_WD = 0.01
ADAM_STEP = 10

_NN = ((1,), (0,))
_NT = ((1,), (1,))
_TN = ((0,), (0,))


def _dot(a, b, dims):
    return lax.dot_general(a.astype(MXU_DTYPE), b.astype(MXU_DTYPE), (dims, ((), ())),
                           preferred_element_type=F32)


def _dot_f32(a, b, dims):
    return lax.dot_general(a, b, (dims, ((), ())), preferred_element_type=F32, precision=HIGHEST)


def _cp(*sem):
    return pltpu.CompilerParams(dimension_semantics=sem, vmem_limit_bytes=VMEM_LIMIT_BYTES)


def _tile(n, pref, align=LANES):
    if n <= pref:
        return n
    t = (pref // align) * align
    while t >= align:
        if n % t == 0:
            return t
        t -= align
    return n


def _sigmoid(x):
    return jax.nn.sigmoid(x)


def _mm(a, b, mode, out_dtype, *, name, add=None, dep=None, reduce_b=False, b_split=False, o_split=False,
        n_b=None, tm=1024, tn=1024, tk=2048):
    ba, bb = a.shape[0], b.shape[0]
    if mode == "tn":
        kdim, m = a.shape[1], a.shape[2]
        tk = 2 * tk
    else:
        m, kdim = a.shape[1], a.shape[2]
    if b_split:
        s_cnt, b_rows, w = b.shape
        if mode == "nt":
            n = b_rows
            assert kdim == s_cnt * w
            tk = w
        else:
            n = s_cnt * w
            assert b_rows == kdim
            tn = w
        nb = ba
    else:
        n = b.shape[1] if mode == "nt" else b.shape[2]
        n = n if n_b is None else n_b
        nb = max(ba, bb)
    if not (b_split and mode != "nt"):
        tn = _tile(n, tn)
    if not (b_split and mode == "nt"):
        tk = _tile(kdim, tk)
    tm = _tile(m, tm)
    assert m % tm == 0 and n % tn == 0 and kdim % tk == 0, (name, m, n, kdim, tm, tn, tk)
    nk = kdim // tk
    if reduce_b:
        grid = (m // tm, n // tn, nb, nk)
        unpack = lambda i, j, bi, k: (bi, i, j, k)
        sem = ("parallel", "parallel", "arbitrary", "arbitrary")
        nred = nb * nk
    else:
        grid = (nb, m // tm, n // tn, nk)
        unpack = lambda bi, i, j, k: (bi, i, j, k)
        sem = ("parallel", "parallel", "parallel", "arbitrary")
        nred = nk

    def a_map(*g):
        bi, i, j, k = unpack(*g)
        ab = bi if ba > 1 else 0
        return (ab, k, i) if mode == "tn" else (ab, i, k)

    def b_map(*g):
        bi, i, j, k = unpack(*g)
        if b_split:
            return (k, j, 0) if mode == "nt" else (j, k, 0)
        bq = bi if bb > 1 else 0
        return (bq, j, k) if mode == "nt" else (bq, k, j)

    def o_map(*g):
        bi, i, j, k = unpack(*g)
        if o_split:
            return (j, i, 0)
        return (0 if reduce_b else bi, i, j)

    a_blk = (1, tk, tm) if mode == "tn" else (1, tm, tk)
    b_blk = (1, tn, tk) if mode == "nt" else (1, tk, tn)
    dims = {"nn": _NN, "nt": _NT, "tn": _TN}[mode]
    has_add = add is not None

    def body(*refs):
        a_ref, b_ref = refs[:2]
        if has_add:
            add_ref = refs[2]
        if nred == 1:
            o_ref = refs[-1]
            r = _dot(a_ref[0], b_ref[0], dims)
            if has_add:
                r = r + add_ref[0].astype(F32)
            o_ref[0] = r.astype(o_ref.dtype)
            return
        o_ref, acc_ref = refs[-2:]
        if reduce_b:
            step = pl.program_id(2) * nk + pl.program_id(3)
        else:
            step = pl.program_id(3)

        @pl.when(step == 0)
        def _():
            acc_ref[...] = _dot(a_ref[0], b_ref[0], dims)

        @pl.when(step > 0)
        def _():
            acc_ref[...] += _dot(a_ref[0], b_ref[0], dims)

        @pl.when(step == nred - 1)
        def _():
            r = acc_ref[...]
            if has_add:
                r = r + add_ref[0].astype(F32)
            o_ref[0] = r.astype(o_ref.dtype)

    in_specs = [pl.BlockSpec(a_blk, a_map), pl.BlockSpec(b_blk, b_map)]
    operands = [a, b]
    if has_add:
        in_specs.append(pl.BlockSpec((1, tm, tn), o_map))
        operands.append(add)
    if dep is not None:
        in_specs.append(pl.BlockSpec(memory_space=pl.ANY))
        operands.append(dep)
    if o_split:
        out_shape = jax.ShapeDtypeStruct((n // tn, m, tn), out_dtype)
    else:
        out_shape = jax.ShapeDtypeStruct((1 if reduce_b else nb, m, n), out_dtype)
    return pl.pallas_call(
        body, grid=grid, in_specs=in_specs, out_specs=pl.BlockSpec((1, tm, tn), o_map),
        out_shape=out_shape, scratch_shapes=[] if nred == 1 else [pltpu.VMEM((tm, tn), F32)],
        compiler_params=_cp(*sem), name=name)(*operands)


def _mm_nt_split(a, b, dep, *, name, tm=512, tn=1024):
    m, k = a.shape
    s, n, w = b.shape
    assert k == s * w and w % LANES == 0
    tm, tn = _tile(m, tm), _tile(n, tn)
    has_dep = dep is not None

    def body(*refs):
        a_ref, b_ref, o_ref = refs[0], refs[1], refs[-1]
        r = _dot(a_ref[:, 0:w], b_ref[0], _NT)
        for q in range(1, s):
            r = r + _dot(a_ref[:, q * w:(q + 1) * w], b_ref[q], _NT)
        o_ref[...] = r

    return pl.pallas_call(
        body, grid=(n // tn, m // tm),
        in_specs=[pl.BlockSpec((tm, k), lambda j, i: (i, 0)), pl.BlockSpec((s, tn, w), lambda j, i: (0, j, 0))]
        + ([pl.BlockSpec(memory_space=pl.ANY)] if has_dep else []),
        out_specs=pl.BlockSpec((tm, tn), lambda j, i: (i, j)), out_shape=jax.ShapeDtypeStruct((m, n), F32),
        compiler_params=_cp("parallel", "parallel"), name=name)(a, b, *([dep] if has_dep else []))


def _mm2(a, b, mode, out_dtype, *, name, add=None, **kw):
    b3 = b if kw.get("b_split") else b[None]
    r = _mm(a[None], b3, mode, out_dtype, name=name, add=None if add is None else add[None], **kw)
    return r if kw.get("o_split") else r[0]


def _rms_fwd(x, g, *, name, tb=512):
    t, d = x.shape
    tb = min(tb, t)

    def body(x_ref, g_ref, o_ref):
        xv = x_ref[...]
        r = lax.rsqrt(jnp.mean(xv * xv, axis=-1, keepdims=True) + EPS)
        o_ref[...] = (xv * r * g_ref[...]).astype(o_ref.dtype)

    return pl.pallas_call(
        body, grid=(t // tb,),
        in_specs=[pl.BlockSpec((tb, d), lambda i: (i, 0)), pl.BlockSpec((1, d), lambda i: (0, 0))],
        out_specs=pl.BlockSpec((tb, d), lambda i: (i, 0)),
        out_shape=jax.ShapeDtypeStruct((t, d), MXU_DTYPE), compiler_params=_cp("parallel"), name=name)(x, g)


def _rms_bwd(x, g, dh, dres, *, name, tb=512):
    t, d = x.shape
    tb = min(tb, t)
    has_res = dres is not None

    def body(*refs):
        if has_res:
            x_ref, g_ref, dh_ref, dres_ref, dx_ref, dxl_ref, dg_ref = refs
        else:
            x_ref, g_ref, dh_ref, dx_ref, dxl_ref, dg_ref = refs
        xv = x_ref[...]
        r = lax.rsqrt(jnp.mean(xv * xv, axis=-1, keepdims=True) + EPS)
        xh = xv * r
        dhv = dh_ref[...].astype(F32)

        @pl.when(pl.program_id(0) == 0)
        def _():
            dg_ref[...] = jnp.zeros_like(dg_ref)

        dg_ref[...] += jnp.sum(dhv * xh, axis=0, keepdims=True)
        dxh = dhv * g_ref[...]
        dx = r * (dxh - xh * jnp.mean(dxh * xh, axis=-1, keepdims=True))
        if has_res:
            dx = dx + dres_ref[...]
        dx_ref[...] = dx
        dxl_ref[...] = dx.astype(dxl_ref.dtype)

    row = pl.BlockSpec((tb, d), lambda i: (i, 0))
    vec = pl.BlockSpec((1, d), lambda i: (0, 0))
    operands = [x, g, dh] + ([dres] if has_res else [])
    return pl.pallas_call(
        body, grid=(t // tb,), in_specs=[row, vec, row] + ([row] if has_res else []),
        out_specs=[row, row, vec],
        out_shape=[jax.ShapeDtypeStruct((t, d), F32), jax.ShapeDtypeStruct((t, d), MXU_DTYPE),
                   jax.ShapeDtypeStruct((1, d), F32)],
        compiler_params=_cp("arbitrary"), name=name)(*operands)


def _loss_head(x, g, target, *, name, tb=512):
    t, d = x.shape
    tb = min(tb, t)

    def body(x_ref, g_ref, t_ref, loss_ref, dx_ref, dxl_ref, dg_ref):
        xv = x_ref[...]
        r = lax.rsqrt(jnp.mean(xv * xv, axis=-1, keepdims=True) + EPS)
        xh = xv * r
        gv = g_ref[...]
        err = xh * gv - t_ref[...]

        @pl.when(pl.program_id(0) == 0)
        def _():
            dg_ref[...] = jnp.zeros_like(dg_ref)
            loss_ref[...] = jnp.zeros_like(loss_ref)

        row_loss = jnp.mean(err * err, axis=-1, keepdims=True)
        loss_ref[...] += 0.5 * jnp.sum(row_loss, axis=0, keepdims=True)
        dy = err * (1.0 / d)
        dg_ref[...] += jnp.sum(dy * xh, axis=0, keepdims=True)
        dxh = dy * gv
        dx = r * (dxh - xh * jnp.mean(dxh * xh, axis=-1, keepdims=True))
        dx_ref[...] = dx
        dxl_ref[...] = dx.astype(dxl_ref.dtype)

    row = pl.BlockSpec((tb, d), lambda i: (i, 0))
    vec = pl.BlockSpec((1, d), lambda i: (0, 0))
    return pl.pallas_call(
        body, grid=(t // tb,), in_specs=[row, vec, row],
        out_specs=[pl.BlockSpec((1, 1), lambda i: (0, 0)), row, row, vec],
        out_shape=[jax.ShapeDtypeStruct((1, 1), F32), jax.ShapeDtypeStruct((t, d), F32),
                   jax.ShapeDtypeStruct((t, d), MXU_DTYPE), jax.ShapeDtypeStruct((1, d), F32)],
        compiler_params=_cp("arbitrary"), name=name)(x, g, target)


def _ffn_up(h, wg, wu, *, name, tb=1024):
    t, d = h.shape
    s, f, _ = wg.shape
    tb = min(tb, t)

    def body(h_ref, wg_ref, wu_ref, g_ref, u_ref, a_ref):
        hv = h_ref[...]
        gv = _dot(hv, wg_ref[0], _NT)
        uv = _dot(hv, wu_ref[0], _NT)
        g_ref[0] = gv.astype(g_ref.dtype)
        u_ref[0] = uv.astype(u_ref.dtype)
        a_ref[0] = (gv * _sigmoid(gv) * uv).astype(a_ref.dtype)

    wspec = pl.BlockSpec((1, f, d), lambda j, i: (j, 0, 0))
    ospec = pl.BlockSpec((1, tb, f), lambda j, i: (j, i, 0))
    return pl.pallas_call(
        body, grid=(s, t // tb),
        in_specs=[pl.BlockSpec((tb, d), lambda j, i: (i, 0)), wspec, wspec],
        out_specs=[ospec, ospec, ospec],
        out_shape=[jax.ShapeDtypeStruct((s, t, f), MXU_DTYPE)] * 3,
        compiler_params=_cp("parallel", "parallel"), name=name)(h, wg, wu)


def _ffn_dact(dy, wd, gate, up, *, name, tb=1024):
    t, d = dy.shape
    s, f, _ = wd.shape
    tb = min(tb, t)

    def body(dy_ref, wd_ref, g_ref, u_ref, dg_ref, du_ref):
        da = _dot(dy_ref[...], wd_ref[0], _NT)
        gv = g_ref[0].astype(F32)
        sg = _sigmoid(gv)
        du_ref[0] = (da * gv * sg).astype(du_ref.dtype)
        dg_ref[0] = (da * u_ref[0].astype(F32) * (sg * (1.0 + gv * (1.0 - sg)))).astype(dg_ref.dtype)

    aspec = pl.BlockSpec((1, tb, f), lambda i, j: (j, i, 0))
    return pl.pallas_call(
        body, grid=(t // tb, s),
        in_specs=[pl.BlockSpec((tb, d), lambda i, j: (i, 0)),
                  pl.BlockSpec((1, f, d), lambda i, j: (j, 0, 0)), aspec, aspec],
        out_specs=[aspec, aspec],
        out_shape=[jax.ShapeDtypeStruct((s, t, f), MXU_DTYPE), jax.ShapeDtypeStruct((s, t, f), MXU_DTYPE)],
        compiler_params=_cp("parallel", "parallel"), name=name)(dy, wd, gate, up)


def _ffn_down(act, wd, x, *, name, tm=512, tn=1024):
    s, t, f = act.shape
    d = wd.shape[2]
    tm, tn = _tile(t, tm), _tile(d, tn)

    def body(a_ref, w_ref, x_ref, o_ref):
        r = x_ref[...]
        for j in range(s):
            r = r + _dot(a_ref[j], w_ref[j], _NN)
        o_ref[...] = r

    xspec = pl.BlockSpec((tm, tn), lambda k, i: (i, k))
    return pl.pallas_call(
        body, grid=(d // tn, t // tm),
        in_specs=[pl.BlockSpec((s, tm, f), lambda k, i: (0, i, 0)), pl.BlockSpec((s, f, tn), lambda k, i: (0, 0, k)),
                  xspec],
        out_specs=xspec, out_shape=jax.ShapeDtypeStruct((t, d), F32),
        compiler_params=_cp("parallel", "parallel"), name=name)(act, wd, x)


def _ffn_dh(dgate, dup, wg, wu, dep, *, name, tm=512, tn=2048, sg=2):
    s, t, f = dgate.shape
    d = wg.shape[2]
    tm, tn = _tile(t, tm), _tile(d, tn)
    steps = s // sg
    has_dep = dep is not None

    def body(*refs):
        dg_ref, du_ref, wg_ref, wu_ref = refs[:4]
        o_ref, acc_ref = refs[-2:]
        j = pl.program_id(2)
        part = _dot(dg_ref[0], wg_ref[0], _NN) + _dot(du_ref[0], wu_ref[0], _NN)
        for q in range(1, sg):
            part = part + _dot(dg_ref[q], wg_ref[q], _NN) + _dot(du_ref[q], wu_ref[q], _NN)

        @pl.when(j == 0)
        def _():
            acc_ref[...] = part

        @pl.when(j > 0)
        def _():
            acc_ref[...] += part

        @pl.when(j == steps - 1)
        def _():
            o_ref[...] = acc_ref[...]

    aspec = pl.BlockSpec((sg, tm, f), lambda i, k, j: (j, i, 0))
    wspec = pl.BlockSpec((sg, f, tn), lambda i, k, j: (j, 0, k))
    return pl.pallas_call(
        body, grid=(t // tm, d // tn, steps),
        in_specs=[aspec, aspec, wspec, wspec] + ([pl.BlockSpec(memory_space=pl.ANY)] if has_dep else []),
        out_specs=pl.BlockSpec((tm, tn), lambda i, k, j: (i, k)),
        out_shape=jax.ShapeDtypeStruct((t, d), F32), scratch_shapes=[pltpu.VMEM((tm, tn), F32)],
        compiler_params=_cp("parallel", "parallel", "arbitrary"),
        name=name)(dgate, dup, wg, wu, *([dep] if has_dep else []))


def _xattn_fwd(q, kv, *, name, tb=512):
    t, d = q.shape
    m = kv.shape[0]
    hd = d // XA_HEADS
    tb = min(tb, t)
    scale = hd ** -0.5

    def body(q_ref, kv_ref, o_ref):
        for hh in range(XA_HEADS):
            cs = slice(hh * hd, (hh + 1) * hd)
            s = _dot(q_ref[:, cs], kv_ref[:, cs], _NT) * scale
            s = s - jnp.max(s, axis=-1, keepdims=True)
            e = jnp.exp(s)
            p = e / jnp.sum(e, axis=-1, keepdims=True)
            o_ref[:, cs] = _dot(p, kv_ref[:, d + hh * hd:d + (hh + 1) * hd], _NN).astype(o_ref.dtype)

    return pl.pallas_call(
        body, grid=(t // tb,),
        in_specs=[pl.BlockSpec((tb, d), lambda i: (i, 0)), pl.BlockSpec((m, 2 * d), lambda i: (0, 0))],
        out_specs=pl.BlockSpec((tb, d), lambda i: (i, 0)),
        out_shape=jax.ShapeDtypeStruct((t, d), MXU_DTYPE), compiler_params=_cp("parallel"), name=name)(q, kv)


def _xattn_bwd(q, kv, do, *, name, tb=512):
    t, d = q.shape
    m = kv.shape[0]
    hd = d // XA_HEADS
    tb = min(tb, t)
    scale = hd ** -0.5

    def body(q_ref, kv_ref, do_ref, dq_ref, dkv_ref):
        @pl.when(pl.program_id(0) == 0)
        def _():
            dkv_ref[...] = jnp.zeros_like(dkv_ref)

        for hh in range(XA_HEADS):
            cs = slice(hh * hd, (hh + 1) * hd)
            vs = slice(d + hh * hd, d + (hh + 1) * hd)
            qv, kk, vv, dov = q_ref[:, cs], kv_ref[:, cs], kv_ref[:, vs], do_ref[:, cs]
            s = _dot(qv, kk, _NT) * scale
            s = s - jnp.max(s, axis=-1, keepdims=True)
            e = jnp.exp(s)
            p = e / jnp.sum(e, axis=-1, keepdims=True)
            dkv_ref[:, vs] += _dot(p, dov, _TN)
            dp = _dot(dov, vv, _NT)
            ds = p * (dp - jnp.sum(p * dp, axis=-1, keepdims=True)) * scale
            dq_ref[:, cs] = _dot(ds, kk, _NN).astype(dq_ref.dtype)
            dkv_ref[:, cs] += _dot(ds, qv, _TN)

    row = pl.BlockSpec((tb, d), lambda i: (i, 0))
    full = pl.BlockSpec((m, 2 * d), lambda i: (0, 0))
    return pl.pallas_call(
        body, grid=(t // tb,), in_specs=[row, full, row], out_specs=[row, full],
        out_shape=[jax.ShapeDtypeStruct((t, d), MXU_DTYPE), jax.ShapeDtypeStruct((m, 2 * d), F32)],
        compiler_params=_cp("arbitrary"), name=name)(q, kv, do)


def _pool_window_stats(u, gi, reverse):
    t = u.shape[0]
    row = lax.broadcasted_iota(jnp.int32, u.shape, 0)
    s = u
    for j in range(POOL_GROUPS):
        sh = 1 << j
        if reverse:
            rolled = jnp.where(row < t - sh, pltpu.roll(s, t - sh, axis=0), 0.0)
        else:
            rolled = jnp.where(row >= sh, pltpu.roll(s, sh, axis=0), 0.0)
        s = jnp.where(j <= gi, s + rolled, s)
    return s, row


def _pool_fwd(z, w_pool, scale, *, name):
    t = z.shape[0]
    g_cnt, c, _ = w_pool.shape

    def body(z_ref, w_ref, s_ref, o_ref):
        gi = pl.program_id(0)
        u = z_ref[...]
        win, row = _pool_window_stats(u, gi, False)
        cnt = jnp.minimum(row + 1, lax.shift_left(jnp.int32(2), gi)).astype(F32)
        p = win / cnt - u
        o_ref[...] = (_dot(p, w_ref[0], _NN) * s_ref[...]).astype(o_ref.dtype)

    return pl.pallas_call(
        body, grid=(g_cnt,),
        in_specs=[pl.BlockSpec((t, c), lambda g: (0, g)), pl.BlockSpec((1, c, c), lambda g: (g, 0, 0)),
                  pl.BlockSpec((1, c), lambda g: (0, g))],
        out_specs=pl.BlockSpec((t, c), lambda g: (0, g)),
        out_shape=jax.ShapeDtypeStruct((t, 2 * g_cnt * c), MXU_DTYPE),
        compiler_params=_cp("parallel"), name=name)(z, w_pool, scale)


def _pool_bwd(z, w_pool, scale, dycat, *, name):
    t = z.shape[0]
    g_cnt, c, _ = w_pool.shape

    def body(z_ref, w_ref, s_ref, dy_ref, du_ref, dw_ref, ds_ref):
        gi = pl.program_id(0)
        u = z_ref[...]
        win, row = _pool_window_stats(u, gi, False)
        cnt = jnp.minimum(row + 1, lax.shift_left(jnp.int32(2), gi)).astype(F32)
        p = win / cnt - u
        y = _dot(p, w_ref[0], _NN)
        dya = dy_ref[...].astype(F32)
        ds_ref[...] = jnp.sum(dya * y, axis=0, keepdims=True)
        dy = dya * s_ref[...]
        dw_ref[0] = _dot(p, dy, _TN)
        dp = _dot(dy, w_ref[0], _NT)
        back, _ = _pool_window_stats(dp / cnt, gi, True)
        du_ref[...] = (back - dp).astype(du_ref.dtype)

    col = pl.BlockSpec((t, c), lambda g: (0, g))
    return pl.pallas_call(
        body, grid=(g_cnt,),
        in_specs=[col, pl.BlockSpec((1, c, c), lambda g: (g, 0, 0)), pl.BlockSpec((1, c), lambda g: (0, g)), col],
        out_specs=[col, pl.BlockSpec((1, c, c), lambda g: (g, 0, 0)), pl.BlockSpec((1, c), lambda g: (0, g))],
        out_shape=[jax.ShapeDtypeStruct((t, g_cnt * c), MXU_DTYPE), jax.ShapeDtypeStruct((g_cnt, c, c), F32),
                   jax.ShapeDtypeStruct((1, g_cnt * c), F32)],
        compiler_params=_cp("parallel"), name=name)(z, w_pool, scale, dycat)


def _chunk_tri(lower):
    r = lax.broadcasted_iota(jnp.int32, (LANES, LANES), 0)
    c = lax.broadcasted_iota(jnp.int32, (LANES, LANES), 1)
    same = (r // HG_CHUNK) == (c // HG_CHUNK)
    return jnp.where(same & ((c <= r) if lower else (c >= r)), 1.0, 0.0).astype(F32)


def _hgrn_prepare(q_ref, f_ref, lb_ref, qh_s, k_s, b_s, qt_s, kt_s, gl_s):
    tb = q_ref.shape[0]
    lb = lb_ref[...]
    sg = _sigmoid(f_ref[...])
    f = lb + (1.0 - lb) * sg
    logf = jnp.log(f)
    qv = q_ref[...]
    qh = qv * _sigmoid(qv) * (HG_HEAD ** -0.5)
    tri = _chunk_tri(True)
    for r in range(tb // LANES):
        rows = slice(r * LANES, (r + 1) * LANES)
        b_s[rows, :] = _dot_f32(tri, logf[rows, :], _NN)
    b = b_s[...]
    b3 = b.reshape(tb // HG_CHUNK, HG_CHUNK, HG_HEAD)
    bl = b3[:, HG_CHUNK - 1:HG_CHUNK, :]
    k = 1.0 - f
    qh_s[...] = qh
    k_s[...] = k
    qt_s[...] = qh * jnp.exp(b)
    kt_s[...] = k * jnp.exp(bl - b3).reshape(tb, HG_HEAD)
    gl_s[...] = jnp.exp(jnp.broadcast_to(bl, b3.shape)).reshape(tb, HG_HEAD)
    return sg, f


SUBLANES = 8
HG_GROUPS = HG_CHUNK // SUBLANES


def _hgrn_intra(qh, kk, bq, rows_a, rows_b):
    ones = jnp.ones((HG_HEAD, HG_HEAD), MXU_DTYPE)
    es, stack_a, stack_b, starts, at = [], [], [], [], 0
    for s in range(HG_CHUNK):
        lo = (s // SUBLANES) * SUBLANES
        e = jnp.exp(jnp.minimum(bq[lo:, :] - bq[s:s + 1, :], 0.0))
        es.append(e)
        stack_a.append(qh[lo:, :] * e * kk[s:s + 1, :])
        if rows_a is not None:
            stack_b.append(rows_a[lo:, :] * rows_b[s:s + 1, :])
        starts.append(at)
        at += HG_CHUNK - lo
    a_rep = _dot(jnp.concatenate(stack_a, axis=0), ones, _NN)
    d_rep = _dot(jnp.concatenate(stack_b, axis=0), ones, _NN) if rows_a is not None else None
    return es, a_rep, d_rep, starts


def _groups(v):
    return [v[g * SUBLANES:(g + 1) * SUBLANES, :] for g in range(HG_GROUPS)]


def _hgrn_fwd(z, lb, hg_norm, ycat, *, name, tb=1024):
    t = z.shape[0]
    mix_b = lb.shape[1]
    heads = mix_b // HG_HEAD
    off = (z.shape[1] - 4 * mix_b) // HG_HEAD
    tb = min(tb, t)
    ncb = tb // HG_CHUNK

    def body(q_ref, f_ref, i_ref, g_ref, lb_ref, hn_ref, ycat_in, y_ref, o_ref, st_ref,
             state, qh_s, k_s, b_s, qt_s, kt_s, gl_s, o_s):
        del ycat_in

        @pl.when(pl.program_id(1) == 0)
        def _():
            state[...] = jnp.zeros_like(state)

        _hgrn_prepare(q_ref, f_ref, lb_ref, qh_s, k_s, b_s, qt_s, kt_s, gl_s)
        row = lax.broadcasted_iota(jnp.int32, (SUBLANES, HG_HEAD), 0)

        def chunk(c, carry):
            rows = pl.ds(pl.multiple_of(c * HG_CHUNK, HG_CHUNK), HG_CHUNK)
            st = state[...]
            st_ref[0, c] = st
            vv = i_ref[rows, :]
            o = _groups(_dot(qt_s[rows, :], st, _NT))
            _, a_rep, _, starts = _hgrn_intra(qh_s[rows, :], k_s[rows, :], b_s[rows, :], None, None)
            for s in range(HG_CHUNK):
                g0 = s // SUBLANES
                for g in range(g0, HG_GROUPS):
                    at = starts[s] + (g - g0) * SUBLANES
                    piece = a_rep[at:at + SUBLANES, :] * vv[s:s + 1, :]
                    o[g] = o[g] + (jnp.where(row >= s - g0 * SUBLANES, piece, 0.0) if g == g0 else piece)
            o_s[rows, :] = jnp.concatenate(o, axis=0)
            state[...] = st * gl_s[rows, :][0:1, :] + _dot(vv, kt_s[rows, :], _TN)
            return carry

        lax.fori_loop(0, ncb, chunk, 0, unroll=2)
        o = o_s[...]
        o_ref[...] = o
        r = lax.rsqrt(jnp.mean(o * o, axis=-1, keepdims=True) + EPS)
        gv = g_ref[...]
        y_ref[...] = (o * r * hn_ref[...] * (gv * _sigmoid(gv))).astype(y_ref.dtype)

    def zcol(kind):
        return pl.BlockSpec((tb, HG_HEAD), lambda h, i: (i, off + kind * heads + h))

    scratch = [pltpu.VMEM((HG_HEAD, HG_HEAD), F32)] + [pltpu.VMEM((tb, HG_HEAD), F32)] * 7
    return pl.pallas_call(
        body, grid=(heads, t // tb),
        in_specs=[zcol(0), zcol(1), zcol(2), zcol(3), pl.BlockSpec((1, HG_HEAD), lambda h, i: (0, h)),
                  pl.BlockSpec((1, HG_HEAD), lambda h, i: (0, 0)), pl.BlockSpec(memory_space=pl.ANY)],
        out_specs=[pl.BlockSpec((tb, HG_HEAD), lambda h, i: (i, heads + h)),
                   pl.BlockSpec((tb, HG_HEAD), lambda h, i: (i, h)),
                   pl.BlockSpec((1, ncb, HG_HEAD, HG_HEAD), lambda h, i: (h, i, 0, 0))],
        out_shape=[jax.ShapeDtypeStruct(ycat.shape, ycat.dtype), jax.ShapeDtypeStruct((t, mix_b), F32),
                   jax.ShapeDtypeStruct((heads, t // HG_CHUNK, HG_HEAD, HG_HEAD), F32)],
        scratch_shapes=scratch, input_output_aliases={6: 0},
        compiler_params=_cp("parallel", "arbitrary"), name=name)(z, z, z, z, lb, hg_norm, ycat)


def _hgrn_bwd(z, lb, hg_norm, o_raw, states, dycat, *, name, tb=1024):
    t = z.shape[0]
    mix_b = lb.shape[1]
    heads = mix_b // HG_HEAD
    off = (z.shape[1] - 4 * mix_b) // HG_HEAD
    tb = min(tb, t)
    ncb = tb // HG_CHUNK
    nt = t // tb

    def body(q_ref, f_ref, i_ref, g_ref, lb_ref, hn_ref, o_ref, st_ref, dy_ref,
             dq_ref, dfl_ref, di_ref, dg_ref, dlb_ref, dhn_ref,
             dstate, qh_s, k_s, b_s, qt_s, kt_s, gl_s, do_s, dqh_s, dk_s, db_s):
        first = pl.program_id(1) == 0

        @pl.when(first)
        def _():
            dstate[...] = jnp.zeros_like(dstate)
            dlb_ref[...] = jnp.zeros_like(dlb_ref)

        @pl.when(first & (pl.program_id(0) == 0))
        def _():
            dhn_ref[...] = jnp.zeros_like(dhn_ref)

        sg, f = _hgrn_prepare(q_ref, f_ref, lb_ref, qh_s, k_s, b_s, qt_s, kt_s, gl_s)
        o = o_ref[...]
        r = lax.rsqrt(jnp.mean(o * o, axis=-1, keepdims=True) + EPS)
        oh = o * r
        gv = g_ref[...]
        sgg = _sigmoid(gv)
        dy = dy_ref[...].astype(F32)
        hn = hn_ref[...]
        dg_ref[...] = (dy * oh * hn * (sgg * (1.0 + gv * (1.0 - sgg)))).astype(dg_ref.dtype)
        don = dy * (gv * sgg)
        dhn_ref[...] += jnp.sum(don * oh, axis=0, keepdims=True)
        doh = don * hn
        do_s[...] = r * (doh - oh * jnp.mean(doh * oh, axis=-1, keepdims=True))
        row = lax.broadcasted_iota(jnp.int32, (SUBLANES, HG_HEAD), 0)

        def chunk(ci, carry):
            c = ncb - 1 - ci
            rows = pl.ds(pl.multiple_of(c * HG_CHUNK, HG_CHUNK), HG_CHUNK)
            st_prev = st_ref[0, c]
            dst = dstate[...]
            qh, kk, bq, vv = qh_s[rows, :], k_s[rows, :], b_s[rows, :], i_ref[rows, :]
            qt, kt, doo = qt_s[rows, :], kt_s[rows, :], do_s[rows, :]
            gl = gl_s[rows, :][0:1, :]
            es, a_rep, d_rep, starts = _hgrn_intra(qh, kk, bq, doo, vv)
            dqh = _groups(jnp.exp(bq) * _dot(doo, st_prev, _NN))
            dk = _groups(jnp.exp(bq[HG_CHUNK - 1:HG_CHUNK, :] - bq) * _dot(vv, dst, _NN))
            dv = _groups(_dot(kt, dst, _NT))
            qh_g, do_g = _groups(qh), _groups(doo)
            for s in range(HG_CHUNK):
                g0 = s // SUBLANES
                local = s - g0 * SUBLANES
                dk_acc = dv_acc = None
                for g in range(g0, HG_GROUPS):
                    at = (g - g0) * SUBLANES
                    wgt = d_rep[starts[s] + at:starts[s] + at + SUBLANES, :] * es[s][at:at + SUBLANES, :]
                    avo = a_rep[starts[s] + at:starts[s] + at + SUBLANES, :] * do_g[g]
                    if g == g0:
                        wgt = jnp.where(row >= local, wgt, 0.0)
                        avo = jnp.where(row >= local, avo, 0.0)
                    dqh[g] = dqh[g] + wgt * kk[s:s + 1, :]
                    dk_acc = wgt * qh_g[g] if dk_acc is None else dk_acc + wgt * qh_g[g]
                    dv_acc = avo if dv_acc is None else dv_acc + avo
                dk[g0] = dk[g0] + jnp.where(row == local, jnp.sum(dk_acc, axis=0, keepdims=True), 0.0)
                dv[g0] = dv[g0] + jnp.where(row == local, jnp.sum(dv_acc, axis=0, keepdims=True), 0.0)
            dqh, dk, dv = [jnp.concatenate(p, axis=0) for p in (dqh, dk, dv)]
            row_c = lax.broadcasted_iota(jnp.int32, (HG_CHUNK, HG_HEAD), 0)
            st_next = st_prev * gl + _dot(vv, kt, _TN)
            db = qh * dqh - kk * dk
            db = db + jnp.where(row_c == HG_CHUNK - 1, jnp.sum(st_next * dst, axis=0, keepdims=True), 0.0)
            dstate[...] = dst * gl + _dot(doo, qt, _TN)
            dqh_s[rows, :] = dqh
            dk_s[rows, :] = dk
            db_s[rows, :] = db
            di_ref[rows, :] = dv.astype(di_ref.dtype)
            return carry

        lax.fori_loop(0, ncb, chunk, 0, unroll=2)
        tri = _chunk_tri(False)
        lb_v = lb_ref[...]
        qv = q_ref[...]
        sgq = _sigmoid(qv)
        dq_ref[...] = (dqh_s[...] * (HG_HEAD ** -0.5) * (sgq * (1.0 + qv * (1.0 - sgq)))).astype(dq_ref.dtype)
        dlb = jnp.zeros((1, HG_HEAD), F32)
        for rr in range(tb // LANES):
            rws = slice(rr * LANES, (rr + 1) * LANES)
            dlogf = _dot_f32(tri, db_s[rws, :], _NN)
            df = dlogf / f[rws, :] - dk_s[rws, :]
            sgr = sg[rws, :]
            dfl_ref[rws, :] = (df * (1.0 - lb_v) * sgr * (1.0 - sgr)).astype(dfl_ref.dtype)
            dlb = dlb + jnp.sum(df * (1.0 - sgr), axis=0, keepdims=True)
        dlb_ref[...] += dlb

    def zcol(kind):
        return pl.BlockSpec((tb, HG_HEAD), lambda h, i: (nt - 1 - i, off + kind * heads + h))

    hcol = pl.BlockSpec((tb, HG_HEAD), lambda h, i: (nt - 1 - i, h))
    scratch = [pltpu.VMEM((HG_HEAD, HG_HEAD), F32)] + [pltpu.VMEM((tb, HG_HEAD), F32)] * 10
    out = jax.ShapeDtypeStruct((t, mix_b), MXU_DTYPE)
    return pl.pallas_call(
        body, grid=(heads, nt),
        in_specs=[zcol(0), zcol(1), zcol(2), zcol(3), pl.BlockSpec((1, HG_HEAD), lambda h, i: (0, h)),
                  pl.BlockSpec((1, HG_HEAD), lambda h, i: (0, 0)), hcol,
                  pl.BlockSpec((1, ncb, HG_HEAD, HG_HEAD), lambda h, i: (h, nt - 1 - i, 0, 0)),
                  pl.BlockSpec((tb, HG_HEAD), lambda h, i: (nt - 1 - i, heads + h))],
        out_specs=[hcol, hcol, hcol, hcol, pl.BlockSpec((1, HG_HEAD), lambda h, i: (0, h)),
                   pl.BlockSpec((1, HG_HEAD), lambda h, i: (0, 0))],
        out_shape=[out, out, out, out, jax.ShapeDtypeStruct((1, mix_b), F32),
                   jax.ShapeDtypeStruct((1, HG_HEAD), F32)],
        scratch_shapes=scratch, compiler_params=_cp("arbitrary", "arbitrary"),
        name=name)(z, z, z, z, lb, hg_norm, o_raw, states, dycat)


def _lb_fwd(lb_table, layer, *, name):
    rows, width = lb_table.shape

    def body(t_ref, o_ref):
        tv = t_ref[...]
        e = jnp.exp(tv - jnp.max(tv, axis=0, keepdims=True))
        sm = e / jnp.sum(e, axis=0, keepdims=True)
        o_ref[...] = jnp.sum(sm[1:layer + 2, :], axis=0, keepdims=True)

    return pl.pallas_call(body, out_shape=jax.ShapeDtypeStruct((1, width), F32), name=name)(lb_table)


def _lb_bwd(lb_table, dlb, layer, *, name):
    rows, width = lb_table.shape

    def body(t_ref, d_ref, o_ref):
        tv = t_ref[...]
        e = jnp.exp(tv - jnp.max(tv, axis=0, keepdims=True))
        sm = e / jnp.sum(e, axis=0, keepdims=True)
        ridx = lax.broadcasted_iota(jnp.int32, sm.shape, 0)
        dsm = jnp.where((ridx >= 1) & (ridx <= layer + 1), d_ref[...], 0.0)
        o_ref[...] = sm * (dsm - jnp.sum(sm * dsm, axis=0, keepdims=True))

    return pl.pallas_call(body, out_shape=jax.ShapeDtypeStruct((rows, width), F32), name=name)(lb_table, dlb)


FOX_BLOCK = 1024
FOX_DIAGONAL_STRIPS = 4


def _fox_prep(zf, b_f, *, name, blk=256):
    t = zf.shape[0]

    def body(z_ref, b_ref, fc_ref):
        r = lax.broadcasted_iota(jnp.int32, (blk, blk), 0)
        c = lax.broadcasted_iota(jnp.int32, (blk, blk), 1)
        tri = jnp.where(c <= r, 1.0, 0.0).astype(F32)
        carry = jnp.zeros((1, LANES), F32)
        for j in range(t // blk):
            rows = slice(j * blk, (j + 1) * blk)
            ls = jax.nn.log_sigmoid(z_ref[rows, :] + b_ref[...])
            fb = _dot_f32(tri, ls, _NN) + carry
            carry = fb[blk - 1:blk, :]
            fc_ref[rows, :] = fb

    return pl.pallas_call(
        body, out_shape=jax.ShapeDtypeStruct((t, LANES), F32),
        compiler_params=pltpu.CompilerParams(vmem_limit_bytes=VMEM_LIMIT_BYTES), name=name)(zf, b_f)


def _fox_head_column(fc_ref, fk_s, head):
    lane = lax.broadcasted_iota(jnp.int32, fc_ref.shape, 1)
    fk_s[...] = jnp.sum(jnp.where(lane == head, fc_ref[...], 0.0), axis=1, keepdims=True)


def _fox_scores(k_blk, q_blk, fk_blk, diagonal):
    s = _dot(k_blk, q_blk, _NT) * (FOX_HEAD ** -0.5) - fk_blk
    if diagonal:
        key = lax.broadcasted_iota(jnp.int32, s.shape, 0)
        qry = lax.broadcasted_iota(jnp.int32, s.shape, 1)
        s = jnp.where(key <= qry, s, -jnp.inf)
    return s


def _fox_fwd(zqkv, fcol, *, name):
    t = zqkv.shape[0]
    d = zqkv.shape[1] // 3
    heads = d // FOX_HEAD
    blk = min(FOX_BLOCK, t)
    nq = t // blk

    def body(q_ref, k_ref, v_ref, fc_ref, o_ref, lse_ref, fk_s):
        _fox_head_column(fc_ref, fk_s, pl.program_id(0))

        def q_block(qi, carry):
            qrows = pl.ds(pl.multiple_of(qi * blk, blk), blk)
            q_blk = q_ref[qrows, :]

            def update(st, krows, diagonal):
                m, l, acc = st
                s = _fox_scores(k_ref[krows, :], q_blk, fk_s[krows, :], diagonal)
                m_new = jnp.maximum(m, jnp.max(s, axis=0, keepdims=True))
                alpha = jnp.exp(m - m_new)
                p = jnp.exp(s - m_new)
                l = alpha * l + jnp.sum(p, axis=0, keepdims=True)
                acc = acc * alpha + _dot(v_ref[krows, :], p, _TN)
                return m_new, l, acc

            def k_block(kj, st):
                return update(st, pl.ds(pl.multiple_of(kj * blk, blk), blk), False)

            init = (jnp.full((1, blk), -jnp.inf, F32), jnp.zeros((1, blk), F32),
                    jnp.zeros((FOX_HEAD, blk), F32))
            m, l, acc = update(lax.fori_loop(0, qi, k_block, init), qrows, True)
            o_ref[qrows, :] = (acc / l).T.astype(o_ref.dtype)
            lse_ref[0, :, qrows] = m + jnp.log(l)
            return carry

        lax.fori_loop(0, nq, q_block, 0)

    def col(kind):
        return pl.BlockSpec((t, FOX_HEAD), lambda h: (0, kind * heads + h))

    rowvec = pl.BlockSpec((1, 1, t), lambda h: (h, 0, 0))
    return pl.pallas_call(
        body, grid=(heads,),
        in_specs=[col(0), col(1), col(2), pl.BlockSpec((t, LANES), lambda h: (0, 0))],
        out_specs=[pl.BlockSpec((t, FOX_HEAD), lambda h: (0, h)), rowvec],
        out_shape=[jax.ShapeDtypeStruct((t, d), MXU_DTYPE), jax.ShapeDtypeStruct((heads, 1, t), F32)],
        scratch_shapes=[pltpu.VMEM((t, 1), F32)],
        compiler_params=_cp("parallel"), name=name)(zqkv, zqkv, zqkv, fcol)


def _fox_bwd(zqkv, fcol, lse, o, do, *, name):
    t = zqkv.shape[0]
    d = zqkv.shape[1] // 3
    heads = d // FOX_HEAD
    blk = min(FOX_BLOCK, t)
    nq = t // blk
    strip = blk // FOX_DIAGONAL_STRIPS
    scale = FOX_HEAD ** -0.5

    def body(q_ref, k_ref, v_ref, fc_ref, lse_ref, o_ref, do_ref,
             dq_ref, dk_ref, dv_ref, rq_ref, rk_ref, dq_s, drow_s, fk_s, acc_s, rk_s):
        _fox_head_column(fc_ref, fk_s, pl.program_id(0))
        dq_s[...] = jnp.zeros_like(dq_s)
        rq_ref[...] = jnp.zeros_like(rq_ref)
        ones_f = jnp.ones((8, FOX_HEAD), F32)
        for j in range(nq):
            rows = slice(j * blk, (j + 1) * blk)
            prod = do_ref[rows, :].astype(F32) * o_ref[rows, :].astype(F32)
            drow_s[:, rows] = _dot_f32(ones_f, prod, _NT)

        def k_block(kj, carry):
            krows = pl.ds(pl.multiple_of(kj * blk, blk), blk)
            k_blk, v_blk, fk_blk = k_ref[krows, :], v_ref[krows, :], fk_s[krows, :]

            def pair(qrows, diagonal, keys=slice(0, blk)):
                q_blk, do_blk = q_ref[qrows, :], do_ref[qrows, :]
                s = _fox_scores(k_blk[keys, :], q_blk, fk_blk[keys, :], diagonal)
                p = jnp.exp(s - lse_ref[0, :, qrows])
                acc_s[1, keys, :] += _dot(p, do_blk, _NN)
                dp = _dot(v_blk[keys, :], do_blk, _NT)
                ds = (p * (dp - drow_s[0:1, qrows])).astype(MXU_DTYPE)
                acc_s[0, keys, :] += _dot(ds, q_blk, _NN)
                dq_s[qrows, :] += _dot(ds, k_blk[keys, :], _TN)
                ds_f = ds.astype(F32)
                rq_ref[0, :, qrows] += jnp.sum(ds_f, axis=0, keepdims=True)
                rk_s[keys, :] += jnp.sum(ds_f, axis=1, keepdims=True)

            def q_block(qi, carry2):
                pair(pl.ds(pl.multiple_of(qi * blk, blk), blk), False)
                return carry2

            acc_s[...] = jnp.zeros_like(acc_s)
            rk_s[...] = jnp.zeros_like(rk_s)
            for j in range(FOX_DIAGONAL_STRIPS):
                pair(pl.ds(pl.multiple_of(kj * blk + j * strip, strip), blk - j * strip), True,
                     slice(j * strip, (j + 1) * strip))
            lax.fori_loop(kj + 1, nq, q_block, 0)
            dk_ref[krows, :] = (acc_s[0] * scale).astype(dk_ref.dtype)
            dv_ref[krows, :] = acc_s[1].astype(dv_ref.dtype)
            rk_ref[0, :, krows] = jnp.broadcast_to(rk_s[...], (blk, FOX_HEAD)).T[0:1, :]
            return carry

        lax.fori_loop(0, nq, k_block, 0)
        dq_ref[...] = (dq_s[...] * scale).astype(dq_ref.dtype)

    def col(kind):
        return pl.BlockSpec((t, FOX_HEAD), lambda h: (0, kind * heads + h))

    hcol = pl.BlockSpec((t, FOX_HEAD), lambda h: (0, h))
    rowvec = pl.BlockSpec((1, 1, t), lambda h: (h, 0, 0))
    out = jax.ShapeDtypeStruct((t, d), MXU_DTYPE)
    vec = jax.ShapeDtypeStruct((heads, 1, t), F32)
    return pl.pallas_call(
        body, grid=(heads,),
        in_specs=[col(0), col(1), col(2), pl.BlockSpec((t, LANES), lambda h: (0, 0)), rowvec, hcol, hcol],
        out_specs=[hcol, hcol, hcol, rowvec, rowvec],
        out_shape=[out, out, out, vec, vec],
        scratch_shapes=[pltpu.VMEM((t, FOX_HEAD), F32), pltpu.VMEM((8, t), F32), pltpu.VMEM((t, 1), F32),
                        pltpu.VMEM((2, blk, FOX_HEAD), F32), pltpu.VMEM((blk, 1), F32)],
        compiler_params=_cp("parallel"), name=name)(zqkv, zqkv, zqkv, fcol, lse, o, do)


def _fox_gate_bwd(rq, rk, zf, b_f, *, name, blk=256):
    heads, t = rq.shape

    def body(rq_ref, rk_ref, z_ref, b_ref, dfl_ref, db_ref):
        r = lax.broadcasted_iota(jnp.int32, (blk, blk), 0)
        c = lax.broadcasted_iota(jnp.int32, (blk, blk), 1)
        tri = jnp.where(r >= c, 1.0, 0.0).astype(F32)
        carry = jnp.zeros((heads, 1), F32)
        db = jnp.zeros((1, LANES), F32)
        pad = jnp.zeros((LANES - heads, blk), F32)
        for j in reversed(range(t // blk)):
            cols = slice(j * blk, (j + 1) * blk)
            df = rq_ref[:, cols] - rk_ref[:, cols]
            dls = _dot_f32(df, tri, _NN) + carry
            carry = dls[:, 0:1]
            dls_t = jnp.concatenate([dls, pad], axis=0).T
            dfl = dls_t * _sigmoid(-(z_ref[cols, :] + b_ref[...]))
            dfl_ref[cols, :] = dfl.astype(dfl_ref.dtype)
            db = db + jnp.sum(dfl, axis=0, keepdims=True)
        db_ref[...] = db

    return pl.pallas_call(
        body, out_shape=[jax.ShapeDtypeStruct((t, LANES), MXU_DTYPE), jax.ShapeDtypeStruct((1, LANES), F32)],
        compiler_params=pltpu.CompilerParams(vmem_limit_bytes=VMEM_LIMIT_BYTES), name=name)(rq, rk, zf, b_f)


def _adamw(w, m, v, parts, *, name, layer=None, prev=None, tr=256):
    lcnt, r, c = w.shape
    p = parts.shape[0]
    li = 0 if layer is None else layer
    tr = _tile(r, tr, 16)
    tc = c if tr * c <= 256 * 2048 else _tile(c, 256)
    has_prev = prev is not None

    def body(*refs):
        w_ref, m_ref, v_ref, p_ref = refs[:4]
        g_ref, d_ref, nm_ref, nv_ref = refs[-4:]
        g = p_ref[0].astype(F32)
        for j in range(1, p):
            g = g + p_ref[j].astype(F32)
        wv = w_ref[0]
        mn = ADAM_B1 * m_ref[0] + (1.0 - ADAM_B1) * g
        vn = ADAM_B2 * v_ref[0] + (1.0 - ADAM_B2) * (g * g)
        m_hat = mn / (1.0 - ADAM_B1 ** ADAM_STEP)
        v_hat = vn / (1.0 - ADAM_B2 ** ADAM_STEP)
        g_ref[0] = g
        d_ref[0] = -ADAM_LR * (m_hat / (jnp.sqrt(v_hat) + ADAM_EPS) + ADAM_WD * wv)
        nm_ref[0] = mn
        nv_ref[0] = vn

    slab = pl.BlockSpec((1, tr, tc), lambda i, j: (li, i, j))
    in_specs = [slab, slab, slab, pl.BlockSpec((p, tr, tc), lambda i, j: (0, i, j))]
    operands = [w, m, v, parts]
    aliases = {}
    if has_prev:
        in_specs += [pl.BlockSpec(memory_space=pl.ANY)] * 4
        operands += list(prev)
        aliases = {4: 0, 5: 1, 6: 2, 7: 3}
    shp = jax.ShapeDtypeStruct((lcnt, r, c), F32)
    return pl.pallas_call(
        body, grid=(r // tr, c // tc), in_specs=in_specs, out_specs=[slab] * 4, out_shape=[shp] * 4,
        input_output_aliases=aliases, compiler_params=_cp("parallel", "parallel"), name=name)(*operands)


def _my_place():
    return lax.axis_index("x"), lax.axis_index("y"), lax.axis_index("c")


def _slot(p):
    return 4 * p[0] + 2 * p[1] + p[2]


def _peer(me, mask):
    x, y, c = me
    return (1 - x if mask & 4 else x, 1 - y if mask & 2 else y, 1 - c if mask & 1 else c)


_HBM = pl.BlockSpec(memory_space=pltpu.HBM)
_SEM = pl.BlockSpec(memory_space=pltpu.SEMAPHORE)
_ANY = pl.BlockSpec(memory_space=pl.ANY)
_EFFECT = pltpu.SideEffectType.DATAFLOW_SIDE_EFFECTING


def _push_copy(src_refs, land_refs, send_sems, recv_sems, a, mask, me, per_peer, outgoing):
    peer = _peer(me, mask)
    src = src_refs[a].at[_slot(peer)] if per_peer else land_refs[a].at[_slot(me)]
    dst = land_refs[a].at[_slot(me) if outgoing else _slot(peer)]
    k = a * (N_DEV - 1) + mask - 1
    return pltpu.make_async_remote_copy(
        src_ref=src, dst_ref=dst, send_sem=send_sems.at[k], recv_sem=recv_sems.at[k],
        device_id=peer, device_id_type=MESH)


ALL_PEERS = tuple(range(1, N_DEV))
CHIP_PEERS = (2, 4, 6)
FIRST_HOP = (1,) + CHIP_PEERS


def _push_start(srcs, dep, *, per_peer, name, masks=ALL_PEERS):
    n = len(srcs)
    mine = _slot(_my_place())
    lands = []
    for s in srcs:
        own = lax.dynamic_index_in_dim(s, mine, 0, keepdims=True) if per_peer else s[None]
        shape = s.shape if per_peer else (N_DEV,) + s.shape
        lands.append(lax.dynamic_update_slice_in_dim(lax.empty(shape, s.dtype), own, mine, 0))
    has_dep = dep is not None
    n_src = n if per_peer else 0
    n_buf = n_src + n

    def body(*refs):
        src_refs, land_refs = refs[:n_src], refs[n_src:n_buf]
        send_sems, recv_sems = refs[n_buf + has_dep], refs[n_buf + has_dep + 1]
        token = refs[-1]
        me = _my_place()
        for a in range(n):
            for mask in masks:
                _push_copy(src_refs, land_refs, send_sems, recv_sems, a, mask, me, per_peer, True).start()
        token[...] = jnp.zeros_like(token)

    hbm_in = [pltpu.with_memory_space_constraint(v, pltpu.HBM) for v in (list(srcs) if per_peer else []) + lands]
    out = pl.pallas_call(
        body, name=name,
        out_shape=(pltpu.SemaphoreType.DMA((n * (N_DEV - 1),)), pltpu.SemaphoreType.DMA((n * (N_DEV - 1),)),
                   *[pltpu.HBM(v.shape, v.dtype) for v in hbm_in], jax.ShapeDtypeStruct((8, LANES), F32)),
        in_specs=[_HBM] * n_buf + ([_ANY] if has_dep else []),
        out_specs=(_SEM, _SEM, *[_HBM] * n_buf, pl.BlockSpec(memory_space=pltpu.VMEM)),
        input_output_aliases={i: 2 + i for i in range(n_buf)},
        compiler_params=pltpu.CompilerParams(has_side_effects=_EFFECT),
    )(*hbm_in, *([dep] if has_dep else []))
    return (n, per_peer, masks, out[:-1]), out[-1]


def _push_wait(handle, after, *, name):
    n, per_peer, masks, (send_sems, recv_sems, *bufs) = handle
    n_src = n if per_peer else 0
    n_buf = n_src + n

    def body(*refs):
        src_refs, land_refs = refs[:n_src], refs[n_src:n_buf]
        send_sems, recv_sems = refs[n_buf], refs[n_buf + 1]
        me = _my_place()
        for a in range(n):
            for mask in masks:
                cp = _push_copy(src_refs, land_refs, send_sems, recv_sems, a, mask, me, per_peer, False)
                cp.wait_send()
                cp.wait_recv()

    out = pl.pallas_call(
        body, name=name, out_shape=tuple(pltpu.HBM(v.shape, v.dtype) for v in bufs),
        in_specs=[_HBM] * n_buf + [_SEM, _SEM, _ANY], out_specs=tuple([_HBM] * n_buf),
        input_output_aliases={i: i for i in range(n_buf)},
        compiler_params=pltpu.CompilerParams(has_side_effects=_EFFECT),
    )(*bufs, send_sems, recv_sems, after)
    return list(out[n_src:])


def _relay_copy(land_refs, send_sems, recv_sems, a, j, me, outgoing):
    sibling = _peer(me, 1)
    out_slot = _slot(_peer(me, CHIP_PEERS[j]))
    in_slot = _slot(_peer(sibling, CHIP_PEERS[j]))
    k = a * len(CHIP_PEERS) + j
    return pltpu.make_async_remote_copy(
        src_ref=land_refs[a].at[out_slot], dst_ref=land_refs[a].at[out_slot if outgoing else in_slot],
        send_sem=send_sems.at[k], recv_sem=recv_sems.at[k], device_id=sibling, device_id_type=MESH)


def _relay_start(lands, *, name):
    n = len(lands)

    def body(*refs):
        land_refs, send_sems, recv_sems, token = refs[:n], refs[n], refs[n + 1], refs[-1]
        me = _my_place()
        for a in range(n):
            for j in range(len(CHIP_PEERS)):
                _relay_copy(land_refs, send_sems, recv_sems, a, j, me, True).start()
        token[...] = jnp.zeros_like(token)

    hbm_in = [pltpu.with_memory_space_constraint(v, pltpu.HBM) for v in lands]
    n_sem = n * len(CHIP_PEERS)
    out = pl.pallas_call(
        body, name=name,
        out_shape=(pltpu.SemaphoreType.DMA((n_sem,)), pltpu.SemaphoreType.DMA((n_sem,)),
                   *[pltpu.HBM(v.shape, v.dtype) for v in hbm_in], jax.ShapeDtypeStruct((8, LANES), F32)),
        in_specs=[_HBM] * n, out_specs=(_SEM, _SEM, *[_HBM] * n, pl.BlockSpec(memory_space=pltpu.VMEM)),
        input_output_aliases={i: 2 + i for i in range(n)},
        compiler_params=pltpu.CompilerParams(has_side_effects=_EFFECT),
    )(*hbm_in)
    return (n, out[:-1]), out[-1]


def _relay_wait(handle, after, *, name):
    n, (send_sems, recv_sems, *bufs) = handle

    def body(*refs):
        land_refs, send_sems, recv_sems = refs[:n], refs[n], refs[n + 1]
        me = _my_place()
        for a in range(n):
            for j in range(len(CHIP_PEERS)):
                cp = _relay_copy(land_refs, send_sems, recv_sems, a, j, me, False)
                cp.wait_send()
                cp.wait_recv()

    out = pl.pallas_call(
        body, name=name, out_shape=tuple(pltpu.HBM(v.shape, v.dtype) for v in bufs),
        in_specs=[_HBM] * n + [_SEM, _SEM, _ANY], out_specs=tuple([_HBM] * n),
        input_output_aliases={i: i for i in range(n)},
        compiler_params=pltpu.CompilerParams(has_side_effects=_EFFECT),
    )(*bufs, send_sems, recv_sems, after)
    return list(out)


def _all_reduce_rows(v, *, name):
    r, c = v.shape

    def body(v_ref, o_ref, buf, send_sems, recv_sems):
        me = _my_place()
        mine = _slot(me)
        sends = []
        for mask in range(1, N_DEV):
            peer = _peer(me, mask)
            sends.append(pltpu.make_async_remote_copy(
                src_ref=v_ref, dst_ref=buf.at[mine], send_sem=send_sems.at[mask - 1],
                recv_sem=recv_sems.at[mask - 1], device_id=peer, device_id_type=MESH))
        for cp in sends:
            cp.start()
        buf[mine] = v_ref[...]
        for mask in range(1, N_DEV):
            peer = _peer(me, mask)
            pltpu.make_async_remote_copy(
                src_ref=v_ref, dst_ref=buf.at[_slot(peer)], send_sem=send_sems.at[mask - 1],
                recv_sem=recv_sems.at[mask - 1], device_id=peer, device_id_type=MESH).wait_recv()
        for cp in sends:
            cp.wait_send()
        total = buf[0]
        for j in range(1, N_DEV):
            total = total + buf[j]
        o_ref[...] = total

    vm = pl.BlockSpec(memory_space=pltpu.VMEM)
    return pl.pallas_call(
        body, in_specs=[vm], out_specs=vm, out_shape=jax.ShapeDtypeStruct((r, c), F32),
        scratch_shapes=[pltpu.VMEM((N_DEV, r, c), F32), pltpu.SemaphoreType.DMA((7,)),
                        pltpu.SemaphoreType.DMA((7,))],
        name=name)(v)


def _xa_fwd(x, memn, g_x, wq, wkv, wo, tag):
    hx = _rms_fwd(x, g_x, name=f"xa{tag}_norm")
    q = _mm2(hx, wq, "nn", MXU_DTYPE, name=f"xa{tag}_q", tm=2048)
    kv = _mm2(memn, wkv, "nn", MXU_DTYPE, name=f"xa{tag}_kv", b_split=True)
    o = _xattn_fwd(q, kv, name=f"xa{tag}_attn")
    wo = wo(o) if callable(wo) else wo
    return _mm2(o, wo, "nn", F32, name=f"xa{tag}_out", add=x), (hx, memn, q, kv, o), wo


def _xa_bwd(x, mem, g_x, g_m, wq, wkv, wo, saved, dxo, dxo_lo, tag, put):
    hx, memn, q, kv, o = saved
    do = _mm2(dxo_lo, wo, "nt", MXU_DTYPE, name=f"xa{tag}_do")
    dwo = _mm2(o, dxo_lo, "tn", MXU_DTYPE, name=f"xa{tag}_dwo")
    dq, dkv = _xattn_bwd(q, kv, do, name=f"xa{tag}_attn_bwd")
    dwq = _mm2(hx, dq, "tn", MXU_DTYPE, name=f"xa{tag}_dwq")
    dwkv = _mm2(memn, dkv, "tn", MXU_DTYPE, name=f"xa{tag}_dwkv", o_split=True, tn=wkv.shape[2])
    tok = put((dwq, dwkv, dwo))
    dhx = _mm2(dq, wq, "nt", F32, name=f"xa{tag}_dh", dep=tok)
    dx, dx_lo, dgx = _rms_bwd(x, g_x, dhx, dxo, name=f"xa{tag}_norm_bwd")
    dmemn = _mm2(dkv, wkv, "nt", F32, name=f"xa{tag}_dmem", b_split=True)
    _, _, dgm = _rms_bwd(mem, g_m, dmemn, None, name=f"xa{tag}_mem_norm_bwd")
    return dx, dx_lo, dgx, dgm


def _ffn_fwd(x, g, wg, wu, wd, tag):
    h = _rms_fwd(x, g, name=f"ffn{tag}_norm")
    gate, up, act = _ffn_up(h, wg, wu, name=f"ffn{tag}_up")
    wd = wd(act) if callable(wd) else wd
    return _ffn_down(act, wd, x, name=f"ffn{tag}_down"), (h, gate, up, act), wd


def _ffn_bwd(x, g, wg, wu, wd, saved, dxo, dxo_lo, tag, put):
    h, gate, up, act = saved
    dwd = _mm(act, dxo_lo[None], "tn", MXU_DTYPE, name=f"ffn{tag}_dwd")
    dgate, dup = _ffn_dact(dxo_lo, wd, gate, up, name=f"ffn{tag}_dact")
    dwg = _mm(dgate, h[None], "tn", MXU_DTYPE, name=f"ffn{tag}_dwg")
    dwu = _mm(dup, h[None], "tn", MXU_DTYPE, name=f"ffn{tag}_dwu")
    tok = put((dwg, dwu, dwd))
    dh = _ffn_dh(dgate, dup, wg, wu, tok, name=f"ffn{tag}_dh")
    dx, dx_lo, dg = _rms_bwd(x, g, dh, dxo, name=f"ffn{tag}_norm_bwd")
    return dx, dx_lo, dg


def _even_fwd(x, h, lb, w_in, w_pool, pool_scale, hg_norm, w_out):
    z = _mm2(h, w_in, "nn", F32, name="ev_in", b_split=True)
    ycat = _pool_fwd(z, w_pool, pool_scale, name="ev_pool")
    ycat, o_raw, states = _hgrn_fwd(z, lb, hg_norm, ycat, name="ev_hgrn")
    w_out = w_out(ycat) if callable(w_out) else w_out
    return _mm2(ycat, w_out, "nn", F32, name="ev_out", add=x), (h, z, ycat, o_raw, states), w_out


def _even_bwd(x, g, lb, w_in, w_pool, pool_scale, hg_norm, w_out, saved, dxo, dxo_lo, put):
    h, z, ycat, o_raw, states = saved
    dycat = _mm2(dxo_lo, w_out, "nt", MXU_DTYPE, name="ev_dy")
    dw_out = _mm2(ycat, dxo_lo, "tn", MXU_DTYPE, name="ev_dw_out")
    du, dw_pool, dscale = _pool_bwd(z, w_pool, pool_scale, dycat, name="ev_pool_bwd")
    dq, dfl, di, dg, dlb, dhn = _hgrn_bwd(z, lb, hg_norm, o_raw, states, dycat, name="ev_hgrn_bwd")
    dz = jnp.concatenate([du, dq, dfl, di, dg], axis=1)
    dw_in = _mm2(h, dz, "tn", MXU_DTYPE, name="ev_dw_in", o_split=True, tn=w_in.shape[2])
    tok = put((dw_in, dw_pool, dw_out))
    dh = _mm_nt_split(dz, w_in, tok, name="ev_dh")
    dx, dx_lo, dgn = _rms_bwd(x, g, dh, dxo, name="ev_norm_bwd")
    return dx, dx_lo, dict(ev_norm=dgn, ev_pool_scale=dscale, ev_hg_norm=dhn, lb=dlb)


def _odd_fwd(x, g, w_qkv, w_f, b_f, w_out):
    n_qkv = 3 * x.shape[1]
    h = _rms_fwd(x, g, name="od_norm")
    zqkv = _mm2(h, w_qkv, "nt", MXU_DTYPE, name="od_qkv", n_b=n_qkv)
    zf = _mm2(h, w_f, "nt", F32, name="od_gate")
    fcol = _fox_prep(zf, b_f, name="od_fox_prep")
    o, lse = _fox_fwd(zqkv, fcol, name="od_fox")
    w_out = w_out(o) if callable(w_out) else w_out
    return _mm2(o, w_out, "nn", F32, name="od_out", add=x), (h, zqkv, zf, fcol, o, lse), w_out


def _odd_bwd(x, g, w_qkv, w_f, b_f, w_out, saved, dxo, dxo_lo, put):
    h, zqkv, zf, fcol, o, lse = saved
    do = _mm2(dxo_lo, w_out, "nt", MXU_DTYPE, name="od_do")
    dw_out = _mm2(o, dxo_lo, "tn", MXU_DTYPE, name="od_dw_out")
    dq, dk, dv, rq, rk = _fox_bwd(zqkv, fcol, lse, o, do, name="od_fox_bwd")
    dfl, db_f = _fox_gate_bwd(rq[:, 0, :], rk[:, 0, :], zf, b_f, name="od_fox_gate_bwd")
    dz = jnp.concatenate([dq, dk, dv], axis=1)
    dw_qkv = _mm2(dz, h, "tn", MXU_DTYPE, name="od_dw_qkv")
    dw_f = _mm2(dfl, h, "tn", MXU_DTYPE, name="od_dw_gate")
    tok = put((dw_qkv, dw_f, dw_out))
    dh = _mm2(dz, w_qkv, "nn", F32, name="od_dh_qkv", dep=tok)
    dh = _mm2(dfl, w_f, "nn", F32, name="od_dh_gate", add=dh)
    dx, dx_lo, dgn = _rms_bwd(x, g, dh, dxo, name="od_norm_bwd")
    return dx, dx_lo, dict(od_norm=dgn, od_b_f=db_f)


def _local_step(x, mem, target, sp, get_w, put_dw):
    b_f = jnp.pad(sp["od_b_f"], ((0, 0), (0, LANES - sp["od_b_f"].shape[1])))
    lb = _lb_fwd(sp["lb_table"], 0, name="lb_fwd")
    fin = sp["final_norm"].reshape(1, -1)
    xn, xm, fn = sp["xa_norm"], sp["xa_mem_norm"], sp["ffn_norm"]
    h0 = _rms_fwd(x, sp["ev_norm"], name="ev_norm")
    memn = [_rms_fwd(mem, xm[l:l + 1], name=f"xa{l}_mem_norm") for l in range(xm.shape[0])]
    w_ev = get_w("ev", h0)
    x1, s_ev, w_ev_out = _even_fwd(x, h0, lb, w_ev[0], w_ev[1], sp["ev_pool_scale"], sp["ev_hg_norm"], w_ev[2])
    w_xa0 = get_w("xa0", x1)
    x2, s_xa0, wo0 = _xa_fwd(x1, memn[0], xn[0:1], *w_xa0, 0)
    w_xa0 = (w_xa0[0], w_xa0[1], wo0)
    w_ff0 = get_w("ffn0", x2)
    x3, s_ff0, wd0 = _ffn_fwd(x2, fn[0:1], *w_ff0, 0)
    w_ff0 = (w_ff0[0], w_ff0[1], wd0)
    w_qkv, w_f, w_od_out, od_norm = get_w("od", x3)
    x4, s_od, w_od_out = _odd_fwd(x3, od_norm, w_qkv, w_f, b_f, w_od_out)
    w_xa1 = get_w("xa1", x4)
    x5, s_xa1, wo1 = _xa_fwd(x4, memn[1], xn[1:2], *w_xa1, 1)
    w_xa1 = (w_xa1[0], w_xa1[1], wo1)
    w_ff1 = get_w("ffn1", x5)
    x6, s_ff1, wd1 = _ffn_fwd(x5, fn[1:2], *w_ff1, 1)
    w_ff1 = (w_ff1[0], w_ff1[1], wd1)
    loss, dx, dx_lo, d_fin = _loss_head(x6, fin, target, name="loss_head")
    put = lambda grp: functools.partial(put_dw, grp)
    dx, dx_lo, d_ffn1 = _ffn_bwd(x5, fn[1:2], *w_ff1, s_ff1, dx, dx_lo, 1, put("ffn1"))
    dx, dx_lo, d_xa1, d_xm1 = _xa_bwd(x4, mem, xn[1:2], xm[1:2], *w_xa1, s_xa1, dx, dx_lo, 1, put("xa1"))
    dx, dx_lo, d_od = _odd_bwd(x3, od_norm, w_qkv, w_f, b_f, w_od_out, s_od, dx, dx_lo, put("od"))
    dx, dx_lo, d_ffn0 = _ffn_bwd(x2, fn[0:1], *w_ff0, s_ff0, dx, dx_lo, 0, put("ffn0"))
    dx, dx_lo, d_xa0, d_xm0 = _xa_bwd(x1, mem, xn[0:1], xm[0:1], *w_xa0, s_xa0, dx, dx_lo, 0, put("xa0"))
    dx, _, d_ev = _even_bwd(x, sp["ev_norm"], lb, w_ev[0], w_ev[1], sp["ev_pool_scale"], sp["ev_hg_norm"],
                            w_ev_out, s_ev, dx, dx_lo, put("ev"))
    small = dict(
        lb_table=_lb_bwd(sp["lb_table"], d_ev["lb"], 0, name="lb_bwd"),
        ev_norm=d_ev["ev_norm"], ev_pool_scale=d_ev["ev_pool_scale"], ev_hg_norm=d_ev["ev_hg_norm"],
        od_norm=d_od["od_norm"], od_b_f=d_od["od_b_f"][:, :sp["od_b_f"].shape[1]],
        xa_norm=jnp.concatenate([d_xa0, d_xa1], axis=0), xa_mem_norm=jnp.concatenate([d_xm0, d_xm1], axis=0),
        ffn_norm=jnp.concatenate([d_ffn0, d_ffn1], axis=0), final_norm=d_fin.reshape(-1))
    return loss, dx, small


_SMALL = ("lb_table", "ev_norm", "ev_pool_scale", "ev_hg_norm", "od_norm", "od_b_f", "xa_norm", "xa_mem_norm",
          "ffn_norm", "final_norm")
_WEIGHTS = ("lb_table", "ev_norm", "ev_w_in", "ev_w_pool", "ev_pool_scale", "ev_hg_norm", "ev_w_out", "od_norm",
            "od_w_in", "od_b_f", "od_w_out", "xa_norm", "xa_mem_norm", "xa_wq", "xa_wkv", "xa_wo", "ffn_norm",
            "ffn_w_gate", "ffn_w_up", "ffn_w_down", "final_norm")


def _lo(a):
    return a.astype(MXU_DTYPE)


def _rows(v):
    flat = v.reshape(-1)
    return jnp.pad(flat, (0, (-flat.shape[0]) % LANES)).reshape(-1, LANES)


def kernel(x, mem, lb_table, ev_norm, ev_w_in, ev_w_pool, ev_pool_scale, ev_hg_norm, ev_w_out, od_norm, od_w_in, od_b_f, od_w_out, xa_norm, xa_mem_norm, xa_wq, xa_wkv, xa_wo, ffn_norm, ffn_w_gate, ffn_w_up, ffn_w_down, final_norm, loss_target, m_lb_table, m_ev_norm, m_ev_w_in, m_ev_w_pool, m_ev_pool_scale, m_ev_hg_norm, m_ev_w_out, m_od_norm, m_od_w_in, m_od_b_f, m_od_w_out, m_xa_norm, m_xa_mem_norm, m_xa_wq, m_xa_wkv, m_xa_wo, m_ffn_norm, m_ffn_w_gate, m_ffn_w_up, m_ffn_w_down, m_final_norm, v_lb_table, v_ev_norm, v_ev_w_in, v_ev_w_pool, v_ev_pool_scale, v_ev_hg_norm, v_ev_w_out, v_od_norm, v_od_w_in, v_od_b_f, v_od_w_out, v_xa_norm, v_xa_mem_norm, v_xa_wq, v_xa_wkv, v_xa_wo, v_ffn_norm, v_ffn_w_gate, v_ffn_w_up, v_ffn_w_down, v_final_norm):
    arg = dict(locals())
    d = x.shape[-1]
    layers = xa_wq.shape[0]
    me = _slot(_my_place())

    n_gate = od_b_f.shape[1]
    turned = {k: jnp.swapaxes(arg[k], 1, 2) for k in ("od_w_in", "ffn_w_gate", "ffn_w_up")}
    raw = dict(ev=[ev_w_in[0], ev_w_pool[0]], ev_out=[ev_w_out[0]],
               od=[turned["od_w_in"][0], od_norm], od_out=[od_w_out[0]])
    for l in range(layers):
        raw[f"xa{l}"] = [xa_wq[l], xa_wkv[l]]
        raw[f"xa{l}_out"] = [xa_wo[l]]
        raw[f"ffn{l}"] = [turned["ffn_w_gate"][l], turned["ffn_w_up"][l]]
        raw[f"ffn{l}_down"] = [ffn_w_down[l]]
    order = ("ev", "ev_out", "xa0", "xa0_out", "ffn0", "ffn0_down", "od", "od_out", "xa1", "xa1_out", "ffn1",
             "ffn1_down")
    gathers, relays, tok = {}, {}, None
    for grp in order:
        srcs = [w if tok is None else w + tok[0, 0] for w in raw[grp]]
        srcs = [w if grp == "od" and j == 1 else _lo(w) for j, w in enumerate(srcs)]
        gathers[grp], tok = _push_start(srcs, None, per_peer=False, masks=FIRST_HOP, name=f"gather_{grp}_start")
    last_start = tok

    def second_hop(grp, after):
        lands = _push_wait(gathers[grp], after, name=f"gather_{grp}_wait")
        relays[grp], token = _relay_start(lands, name=f"gather_{grp}_relay")
        return token

    def get_w(grp, after):
        i = order.index(grp)
        if i == 0:
            after = last_start if after is None else last_start + after[:8, :LANES].astype(F32)
        if grp not in relays:
            after = second_hop(grp, after)
        if 1 <= i < len(order) - 1:
            after = second_hop(order[i + 1], after)
        got = _relay_wait(relays[grp], after, name=f"gather_{grp}_relay_wait")
        if grp == "ev":
            w_in, w_pool = got
            w_pool = jnp.transpose(w_pool, (1, 0, 2, 3)).reshape(w_pool.shape[1], -1, w_pool.shape[3])
            return w_in, w_pool, lambda later: get_w("ev_out", later)[0].reshape(d, d)
        if grp.startswith("ffn") and not grp.endswith("down"):
            return got[0], got[1], lambda later: get_w(grp + "_down", later)[0]
        if grp == "od":
            w_in, nrm = got
            assert n_gate <= w_in.shape[1]
            w_f = jnp.pad(w_in[N_DEV - 1, w_in.shape[1] - n_gate:], ((0, LANES - n_gate), (0, 0)))
            return (w_in.reshape(-1, d), w_f, lambda later: get_w("od_out", later)[0].reshape(d, d),
                    nrm.reshape(1, d))
        if grp.startswith("xa") and not grp.endswith("out"):
            return got[0].reshape(d, d), got[1], lambda later: get_w(grp + "_out", later)[0].reshape(d, d)
        return tuple(got)

    def row_parts(g):
        return g.reshape(N_DEV, -1, g.shape[-1])

    scatters = {}

    def put_dw(grp, dws):
        if grp == "ev":
            dw_in, dw_pool, dw_out = dws
            gc = dw_pool.shape[1] // N_DEV
            dw_pool = _lo(jnp.transpose(dw_pool.reshape(dw_pool.shape[0], N_DEV, gc, -1), (1, 0, 2, 3)))
            parts = [dw_in, dw_pool, row_parts(dw_out)]
        elif grp == "od":
            dw_qkv, dw_f, dw_out = dws
            parts = [row_parts(jnp.concatenate([dw_qkv, dw_f[:n_gate]], axis=0)), row_parts(dw_out)]
        elif grp.startswith("xa"):
            parts = [row_parts(dws[0]), dws[1], row_parts(dws[2])]
        else:
            parts = list(dws)
        scatters[grp], token = _push_start(parts, None, per_peer=True, name=f"scatter_{grp}_start")
        return token

    sp = {k: arg[k] for k in _SMALL if k != "od_norm"}
    loss, dx, small = _local_step(x[0], mem[0], loss_target[0], sp, get_w, put_dw)

    pieces = [_rows(small[k]) for k in _SMALL]
    packed = jnp.concatenate(pieces + [_rows(loss)], axis=0)
    packed = jnp.pad(packed, ((0, (-packed.shape[0]) % 8), (0, 0)))
    total = _all_reduce_rows(packed, name="all_reduce_small")
    loss = total[sum(pc.shape[0] for pc in pieces), 0]
    small_g, at = {}, 0
    for k, pc in zip(_SMALL, pieces):
        n = small[k].size
        small_g[k] = total[at:at + pc.shape[0]].reshape(-1)[:n].reshape(small[k].shape)
        at += pc.shape[0]
    small_g["od_norm"] = lax.dynamic_slice_in_dim(small_g["od_norm"], me * od_norm.shape[1], od_norm.shape[1], axis=1)

    res = {}
    for k in _SMALL:
        w, m, v = arg[k], arg["m_" + k], arg["v_" + k]
        shp = (1, 1, w.shape[0]) if w.ndim == 1 else (1,) + w.shape
        out = _adamw(w.reshape(shp), m.reshape(shp), v.reshape(shp), small_g[k].reshape(shp), name=f"adamw_{k}")
        res[k] = [o.reshape(w.shape) for o in out]
    members = dict(ev=("ev_w_in", "ev_w_pool", "ev_w_out"), od=("od_w_in", "od_w_out"),
                   xa=("xa_wq", "xa_wkv", "xa_wo"), ffn=("ffn_w_gate", "ffn_w_up", "ffn_w_down"))
    after, stacked = dx, {}
    for grp in ("ffn1", "xa1", "od", "ffn0", "xa0", "ev"):
        got = _push_wait(scatters[grp], after, name=f"scatter_{grp}_wait")
        kind = grp.rstrip("01")
        for k, parts in zip(members[kind], got):
            w, m, v = [jnp.swapaxes(a, 1, 2) if k in turned else a for a in (arg[k], arg["m_" + k], arg["v_" + k])]
            if w.shape[0] == 1:
                shp = (1, -1, w.shape[-1])
                out = _adamw(w.reshape(shp), m.reshape(shp), v.reshape(shp), parts.reshape(N_DEV, -1, w.shape[-1]),
                             name=f"adamw_{k}")
            else:
                out = _adamw(w, m, v, parts, name=f"adamw_{k}{grp[-1]}", layer=int(grp[-1]), prev=stacked.get(k))
                stacked[k] = out
            res[k] = [jnp.swapaxes(o.reshape(w.shape), 1, 2) if k in turned else o.reshape(w.shape) for o in out]
            after = out[3][:1, :8, :LANES]

    outs = [loss, dx[None]]
    for j in range(4):
        outs += [res[k][j] for k in _WEIGHTS]
    return tuple(outs)
```
